```python
import math
import jax
import jax.numpy as jnp
from jax import lax
import numpy as np

D_MODEL = 1024
BATCH = 2
SEQ = 8192
DEPTH = 4
DEC_BATCH = 32
DEC_SEQ = 8
PAST_LEN = 8192
PAGE_SIZE = 128

N_A_LAYERS = DEPTH // 2
N_B_LAYERS = DEPTH - N_A_LAYERS
RET_HEADS = 4
RET_DK = D_MODEL // RET_HEADS
RET_DV = 2 * D_MODEL // RET_HEADS
RET_CHUNK = 128
N_HEADS = 16
N_KV = 4
HEAD_DIM = 64
GROUP = N_HEADS // N_KV
CMP_LEN = 32
CMP_STRIDE = 16
CMP_HID = 128
SEL_BLOCK = 64
SEL_TOPK = 16
WINDOW = 512
Q_BLOCK = 128
D_FF = 2816
CONV_W = 3
ROPE_THETA = 10000.0
EPS = 1e-6
NEG_INF = -1e30
TINY = 1e-30
SEL_FORCE = 1e6
SEL_NEG = -1e6

kernel_name = 'yoco_retnet_nsa_convffn_step'


def rmsnorm(x, g):
    xf = x.astype(jnp.float32)
    y = xf * lax.rsqrt(jnp.mean(xf * xf, axis=-1, keepdims=True) + EPS)
    return (y * g.astype(jnp.float32)).astype(x.dtype)


def head_rms(x):
    xf = x.astype(jnp.float32)
    return (xf * lax.rsqrt(jnp.mean(xf * xf, axis=-1, keepdims=True) + EPS)).astype(x.dtype)


def rope(x, pos):
    half = x.shape[-1] // 2
    inv = jnp.exp(-math.log(ROPE_THETA) * jnp.arange(half, dtype=jnp.float32) / half)
    ang = pos.astype(jnp.float32)[:, None] * inv[None, :]
    cos = jnp.cos(ang)[:, None, :]
    sin = jnp.sin(ang)[:, None, :]
    xf = x.astype(jnp.float32)
    x1, x2 = xf[..., :half], xf[..., half:]
    return jnp.concatenate([x1 * cos - x2 * sin, x2 * cos + x1 * sin], axis=-1).astype(x.dtype)


def masked_softmax(s, mask):
    s = jnp.where(mask, s, NEG_INF)
    m = jnp.max(s, axis=-1, keepdims=True)
    e = jnp.where(mask, jnp.exp(s - m), 0.0)
    return e / jnp.maximum(jnp.sum(e, axis=-1, keepdims=True), TINY)


def retention_chunk(S, q, k, v, log_gamma):
    L = q.shape[1]
    idx = jnp.arange(L, dtype=jnp.float32)
    diff = idx[:, None] - idx[None, :]
    decay = jnp.where(diff >= 0, jnp.exp(jnp.maximum(diff, 0.0)[None] * log_gamma[:, None, None]), 0.0)
    qf, kf, vf = q.astype(jnp.float32), k.astype(jnp.float32), v.astype(jnp.float32)
    scores = jnp.einsum('blhd,bmhd->bhlm', qf, kf) * decay[None]
    o = jnp.einsum('bhlm,bmhe->blhe', scores, vf)
    q_dec = jnp.exp((idx + 1.0)[:, None] * log_gamma[None, :])
    o = o + jnp.einsum('blhd,bhde->blhe', qf, S) * q_dec[None, :, :, None]
    k_dec = jnp.exp((L - 1.0 - idx)[:, None] * log_gamma[None, :])
    S = S * jnp.exp(L * log_gamma)[None, :, None, None] + jnp.einsum('blhd,blhe->bhde', kf * k_dec[None, :, :, None], vf)
    return S, o


def retention_mixer(h, S0, pos, w_in, w_out):
    B, T, _ = h.shape
    proj = h @ w_in
    q = proj[..., :D_MODEL].reshape(B, T, RET_HEADS, RET_DK)
    k = proj[..., D_MODEL:2 * D_MODEL].reshape(B, T, RET_HEADS, RET_DK)
    v = proj[..., 2 * D_MODEL:4 * D_MODEL].reshape(B, T, RET_HEADS, RET_DV)
    g = proj[..., 4 * D_MODEL:]
    q = rope(q, pos)
    k = rope(k, pos) * (RET_DK ** -0.5)
    log_gamma = jnp.log1p(-jnp.exp2(-5.0 - jnp.arange(RET_HEADS, dtype=jnp.float32)))
    chunk = RET_CHUNK if T % RET_CHUNK == 0 else T
    n = T // chunk

    def to_chunks(a):
        return a.reshape(B, n, chunk, RET_HEADS, a.shape[-1]).swapaxes(0, 1)

    def step(S, inp):
        return retention_chunk(S, inp[0], inp[1], inp[2], log_gamma)

    S, o = lax.scan(step, S0.astype(jnp.float32), (to_chunks(q), to_chunks(k), to_chunks(v)))
    o = o.swapaxes(0, 1).reshape(B, T, RET_HEADS, RET_DV)
    o = head_rms(o).reshape(B, T, RET_HEADS * RET_DV).astype(h.dtype)
    return (o * jax.nn.silu(g)) @ w_out, S


def conv_ffn(h, buf, w_in, conv_w, conv_b, w_out):
    T = h.shape[1]
    proj = h @ w_in
    u, gate = proj[..., :D_FF], proj[..., D_FF:]
    up = jnp.concatenate([buf.astype(u.dtype), u], axis=1)
    c = conv_b + conv_w[0] * up[:, 0:T]
    for j in range(1, CONV_W):
        c = c + conv_w[j] * up[:, j:j + T]
    out = (jax.nn.gelu(c) * gate) @ w_out
    return out, up[:, up.shape[1] - (CONV_W - 1):]


def compress(rows, cmp_pos, cmp_w1, cmp_w2, k_gain):
    B, L = rows.shape[0], rows.shape[1]
    R = CMP_LEN // CMP_STRIDE
    n_sub = L // CMP_STRIDE
    n_cmp = n_sub - R + 1
    sub = rows[:, :n_sub * CMP_STRIDE].reshape(B, n_sub, CMP_STRIDE, 2, N_KV, HEAD_DIM)
    w1 = cmp_w1.reshape(2, R, CMP_STRIDE, HEAD_DIM, CMP_HID)
    part = jnp.einsum('bnsckd,crsdh->rbnckh', sub, w1)
    hid = part[0][:, 0:n_cmp]
    for r in range(1, R):
        hid = hid + part[r][:, r:r + n_cmp]
    pos_bias = jnp.einsum('cld,cldh->ch', cmp_pos, cmp_w1.reshape(2, CMP_LEN, HEAD_DIM, CMP_HID))
    hid = jax.nn.gelu(hid + pos_bias[:, None, :].astype(hid.dtype))
    out = jnp.einsum('bnckh,chd->bnckd', hid, cmp_w2)
    return head_rms(out[:, :, 0]) * k_gain, out[:, :, 1]


def build_shared(x, pos, past_len, past_cmp, past_sel, past_win,
                 kv_norm, kv_w, kv_knorm, cmp_pos, cmp_w1, cmp_w2):
    B, T, _ = x.shape
    kv = (rmsnorm(x, kv_norm) @ kv_w).reshape(B, T, 6, N_KV, HEAD_DIM)
    new_cmp = kv[:, :, 0:2]
    new_sel = jnp.stack([rope(head_rms(kv[:, :, 2]) * kv_knorm[1], pos), kv[:, :, 3]], axis=2)
    new_win = jnp.stack([rope(head_rms(kv[:, :, 4]) * kv_knorm[2], pos), kv[:, :, 5]], axis=2)
    all_cmp = jnp.concatenate([past_cmp.astype(new_cmp.dtype), new_cmp], axis=1)
    cmp_k, cmp_v = compress(all_cmp, cmp_pos, cmp_w1, cmp_w2, kv_knorm[0])
    all_sel = jnp.concatenate([past_sel.astype(new_sel.dtype), new_sel], axis=1)
    L = all_sel.shape[1]
    pad = (-L) % SEL_BLOCK
    all_sel = jnp.pad(all_sel, ((0, 0), (0, pad), (0, 0), (0, 0), (0, 0)))
    blocks = all_sel.reshape(B, (L + pad) // SEL_BLOCK, SEL_BLOCK, 2, N_KV, HEAD_DIM).transpose(3, 0, 4, 1, 2, 5)
    all_win = jnp.concatenate([past_win.astype(new_win.dtype), new_win], axis=1)
    win_state = all_win[:, all_win.shape[1] - min(WINDOW, past_len + T):]
    win_keys = jnp.pad(all_win, ((0, 0), (WINDOW - past_win.shape[1], 0), (0, 0), (0, 0), (0, 0)))
    shared = (cmp_k, cmp_v, blocks[0], blocks[1], win_keys[:, :, 0], win_keys[:, :, 1])
    return shared, new_cmp, new_sel, win_state


def nsa_attend(q, gates, qpos, cmp_k, cmp_v, sel_k, sel_v, win_k, win_v, win_pos):
    B, Q = q.shape[0], q.shape[1]
    scale = HEAD_DIM ** -0.5
    q_c = q.reshape(B, Q, N_KV, GROUP, HEAD_DIM)
    q_r = rope(q, qpos).reshape(B, Q, N_KV, GROUP, HEAD_DIM)
    n_cmp = cmp_k.shape[1]
    c_start = jnp.arange(n_cmp) * CMP_STRIDE
    m_c = (c_start + CMP_LEN - 1)[None, :] <= qpos[:, None]
    s_c = jnp.einsum('bqkgd,bnkd->bkgqn', q_c, cmp_k).astype(jnp.float32) * scale
    p_c = masked_softmax(s_c, m_c)
    o_c = jnp.einsum('bkgqn,bnkd->bqkgd', p_c.astype(cmp_v.dtype), cmp_v)
    n_sel = sel_k.shape[2]
    s_start = jnp.arange(n_sel) * SEL_BLOCK
    overlap = (jnp.minimum(c_start[:, None] + CMP_LEN, s_start[None, :] + SEL_BLOCK)
               - jnp.maximum(c_start[:, None], s_start[None, :]))
    w_map = jnp.maximum(overlap, 0).astype(jnp.float32) / CMP_LEN
    imp = jnp.einsum('bkgqn,ns->bkqs', p_c, w_map)
    blk = jnp.arange(n_sel)[None, :]
    cur = (qpos // SEL_BLOCK)[:, None]
    forced = (blk == 0) | (blk == cur) | (blk == cur - 1)
    imp = jnp.where(forced, SEL_FORCE, jnp.where(s_start[None, :] <= qpos[:, None], imp, SEL_NEG))
    top_v, top_i = lax.top_k(imp, min(SEL_TOPK, n_sel))
    n_top = top_i.shape[-1]
    bi = jnp.arange(B)[:, None, None, None]
    ki = jnp.arange(N_KV)[None, :, None, None]
    g_k = sel_k[bi, ki, top_i]
    g_v = sel_v[bi, ki, top_i]
    tok = top_i[..., None] * SEL_BLOCK + jnp.arange(SEL_BLOCK)
    m_s = (top_v > 0.5 * SEL_NEG)[..., None] & (tok <= qpos[None, None, :, None, None])
    s_s = jnp.einsum('bqkgd,bkqtsd->bkgqts', q_r, g_k).astype(jnp.float32) * scale
    p_s = masked_softmax(s_s.reshape(B, N_KV, GROUP, Q, n_top * SEL_BLOCK),
                         m_s.reshape(B, N_KV, 1, Q, n_top * SEL_BLOCK))
    o_s = jnp.einsum('bkgqts,bkqtsd->bqkgd', p_s.reshape(s_s.shape).astype(sel_v.dtype), g_v)
    m_w = ((win_pos[None, :] <= qpos[:, None]) & (win_pos[None, :] > qpos[:, None] - WINDOW)
           & (win_pos[None, :] >= 0))
    s_w = jnp.einsum('bqkgd,bwkd->bkgqw', q_r, win_k).astype(jnp.float32) * scale
    p_w = masked_softmax(s_w, m_w)
    o_w = jnp.einsum('bkgqw,bwkd->bqkgd', p_w.astype(win_v.dtype), win_v)
    g = gates.reshape(B, Q, N_KV, GROUP, 3).astype(o_c.dtype)
    o = g[..., 0:1] * o_c + g[..., 1:2] * o_s + g[..., 2:3] * o_w
    return o.reshape(B, Q, N_HEADS * HEAD_DIM)


def nsa_mixer(h, past_len, shared, w_qg, q_gain, w_o):
    cmp_k, cmp_v, sel_k, sel_v, win_k, win_v = shared
    B, T, _ = h.shape
    proj = h @ w_qg
    q = head_rms(proj[..., :N_HEADS * HEAD_DIM].reshape(B, T, N_HEADS, HEAD_DIM)) * q_gain
    gates = jax.nn.sigmoid(proj[..., N_HEADS * HEAD_DIM:].astype(jnp.float32)).reshape(B, T, N_HEADS, 3)
    qb = Q_BLOCK if T % Q_BLOCK == 0 else T
    n_qb = T // qb
    q_blk = q.reshape(B, n_qb, qb, N_HEADS, HEAD_DIM).swapaxes(0, 1)
    g_blk = gates.reshape(B, n_qb, qb, N_HEADS, 3).swapaxes(0, 1)

    def one_block(args):
        i, qi, gi = args
        qs = i * qb
        qpos = past_len + qs + jnp.arange(qb)
        wk = lax.dynamic_slice_in_dim(win_k, qs, WINDOW + qb, axis=1)
        wv = lax.dynamic_slice_in_dim(win_v, qs, WINDOW + qb, axis=1)
        wpos = past_len - WINDOW + qs + jnp.arange(WINDOW + qb)
        return nsa_attend(qi, gi, qpos, cmp_k, cmp_v, sel_k, sel_v, wk, wv, wpos)

    o = lax.map(one_block, (jnp.arange(n_qb), q_blk, g_blk))
    o = o.swapaxes(0, 1).reshape(B, T, N_HEADS * HEAD_DIM)
    return o.astype(h.dtype) @ w_o


def run_trunk(x, past_len, ret_s0, conv0, past_cmp, past_sel, past_win, params):
    (norm_mix, norm_ffn, ret_w_in, ret_w_out, ffn_w_in, ffn_conv_w, ffn_conv_b, ffn_w_out,
     kv_norm, kv_w, kv_knorm, cmp_pos, cmp_w1, cmp_w2, nsa_w_qg, nsa_qnorm, nsa_w_o) = params
    T = x.shape[1]
    pos = past_len + jnp.arange(T)
    ret_states = []
    conv_states = []
    for layer in range(DEPTH):
        if layer == N_A_LAYERS:
            shared, new_cmp, new_sel, new_win = build_shared(
                x, pos, past_len, past_cmp, past_sel, past_win,
                kv_norm, kv_w, kv_knorm, cmp_pos, cmp_w1, cmp_w2)
        h = rmsnorm(x, norm_mix[layer])
        if layer < N_A_LAYERS:
            mix, s_new = retention_mixer(h, ret_s0[layer], pos, ret_w_in[layer], ret_w_out[layer])
            ret_states.append(s_new)
        else:
            j = layer - N_A_LAYERS
            mix = nsa_mixer(h, past_len, shared, nsa_w_qg[j], nsa_qnorm[j], nsa_w_o[j])
        x = x + mix
        f, buf = conv_ffn(rmsnorm(x, norm_ffn[layer]), conv0[layer], ffn_w_in[layer],
                          ffn_conv_w[layer], ffn_conv_b[layer], ffn_w_out[layer])
        conv_states.append(buf)
        x = x + f
    return x, jnp.stack(ret_states), jnp.stack(conv_states), new_cmp, new_sel, new_win


def setup_inputs(seed: int = 0) -> dict:
    key = jax.random.key(seed)
    ks = jax.random.split(key, 26)
    f32 = jnp.float32

    def nrm(k, shape, scale=1.0):
        return jax.random.normal(k, shape, f32) * scale

    n_pages = PAST_LEN // PAGE_SIZE
    n_used = DEC_BATCH * n_pages
    n_pool = n_used + max(n_used // 4, 1)
    win_buf = min(WINDOW, PAST_LEN)
    page_table = jax.random.permutation(ks[0], n_pool)[:n_used].reshape(DEC_BATCH, n_pages).astype(jnp.int32)
    qg_width = N_HEADS * HEAD_DIM + 3 * N_HEADS
    return {
        'x_prompt': nrm(ks[1], (BATCH, SEQ, D_MODEL)),
        'x_sample': nrm(ks[2], (DEC_BATCH, DEC_SEQ, D_MODEL)),
        'cache_cmp_kv': nrm(ks[3], (n_pool, PAGE_SIZE, 2, N_KV, HEAD_DIM)),
        'cache_sel_kv': nrm(ks[4], (n_pool, PAGE_SIZE, 2, N_KV, HEAD_DIM)),
        'cache_win_kv': nrm(ks[5], (DEC_BATCH, win_buf, 2, N_KV, HEAD_DIM)),
        'state_ret': nrm(ks[6], (N_A_LAYERS, DEC_BATCH, RET_HEADS, RET_DK, RET_DV), 0.5),
        'state_conv': nrm(ks[7], (DEPTH, DEC_BATCH, CONV_W - 1, D_FF)),
        'page_table': page_table,
        'norm_mix': 1.0 + nrm(ks[8], (DEPTH, D_MODEL), 0.01),
        'norm_ffn': 1.0 + nrm(ks[9], (DEPTH, D_MODEL), 0.01),
        'ret_w_in': nrm(ks[10], (N_A_LAYERS, D_MODEL, 6 * D_MODEL), D_MODEL ** -0.5),
        'ret_w_out': nrm(ks[11], (N_A_LAYERS, 2 * D_MODEL, D_MODEL), (2 * D_MODEL) ** -0.5),
        'ffn_w_in': nrm(ks[12], (DEPTH, D_MODEL, 2 * D_FF), D_MODEL ** -0.5),
        'ffn_conv_w': nrm(ks[13], (DEPTH, CONV_W, D_FF), CONV_W ** -0.5),
        'ffn_conv_b': nrm(ks[14], (DEPTH, D_FF), 0.01),
        'ffn_w_out': nrm(ks[15], (DEPTH, D_FF, D_MODEL), D_FF ** -0.5),
        'kv_norm': 1.0 + nrm(ks[16], (D_MODEL,), 0.01),
        'kv_w': nrm(ks[17], (D_MODEL, 6 * N_KV * HEAD_DIM), D_MODEL ** -0.5),
        'kv_knorm': 1.0 + nrm(ks[18], (3, HEAD_DIM), 0.01),
        'cmp_pos': nrm(ks[19], (2, CMP_LEN, HEAD_DIM), 0.5),
        'cmp_w1': nrm(ks[20], (2, CMP_LEN * HEAD_DIM, CMP_HID), (CMP_LEN * HEAD_DIM) ** -0.5),
        'cmp_w2': nrm(ks[21], (2, CMP_HID, HEAD_DIM), CMP_HID ** -0.5),
        'nsa_w_qg': nrm(ks[22], (N_B_LAYERS, D_MODEL, qg_width), D_MODEL ** -0.5),
        'nsa_qnorm': 1.0 + nrm(ks[23], (N_B_LAYERS, HEAD_DIM), 0.01),
        'nsa_w_o': nrm(ks[24], (N_B_LAYERS, N_HEADS * HEAD_DIM, D_MODEL), (N_HEADS * HEAD_DIM) ** -0.5),
    }


def reference(x_prompt, x_sample, cache_cmp_kv, cache_sel_kv, cache_win_kv, state_ret, state_conv,
              page_table, norm_mix, norm_ffn, ret_w_in, ret_w_out, ffn_w_in, ffn_conv_w, ffn_conv_b,
              ffn_w_out, kv_norm, kv_w, kv_knorm, cmp_pos, cmp_w1, cmp_w2, nsa_w_qg, nsa_qnorm, nsa_w_o):
    params = (norm_mix, norm_ffn, ret_w_in, ret_w_out, ffn_w_in, ffn_conv_w, ffn_conv_b, ffn_w_out,
              kv_norm, kv_w, kv_knorm, cmp_pos, cmp_w1, cmp_w2, nsa_w_qg, nsa_qnorm, nsa_w_o)
    B = x_prompt.shape[0]
    dt = x_prompt.dtype
    zero_ret = jnp.zeros((N_A_LAYERS, B, RET_HEADS, RET_DK, RET_DV), jnp.float32)
    zero_conv = jnp.zeros((DEPTH, B, CONV_W - 1, D_FF), dt)
    empty = jnp.zeros((B, 0, 2, N_KV, HEAD_DIM), dt)
    y_p, ret_p, conv_p, cmp_p, sel_p, win_p = run_trunk(
        x_prompt, 0, zero_ret, zero_conv, empty, empty, empty, params)
    db = x_sample.shape[0]
    past_len = page_table.shape[1] * PAGE_SIZE
    past_cmp = cache_cmp_kv[page_table].reshape(db, past_len, 2, N_KV, HEAD_DIM)
    past_sel = cache_sel_kv[page_table].reshape(db, past_len, 2, N_KV, HEAD_DIM)
    y_s, ret_s, conv_s, cmp_s, sel_s, win_s = run_trunk(
        x_sample, past_len, state_ret, state_conv, past_cmp, past_sel, cache_win_kv, params)
    return (y_p, y_s, ret_p, ret_s, conv_p, conv_s, cmp_p, cmp_s, sel_p, sel_s, win_p, win_s)
```

```python
import functools
import math

import jax
import jax.numpy as jnp
import numpy as np
from jax import lax
from jax.experimental import pallas as pl
from jax.experimental.pallas import tpu as pltpu

F32 = jnp.float32
BF16 = jnp.bfloat16

D_MODEL = 1024
PAGE = 128
RET_HEADS = 4
RET_DK = 256
RET_DV = 512
RET_CHUNK = 128
N_HEADS = 16
N_KV = 4
HEAD_DIM = 64
CMP_LEN = 32
CMP_STRIDE = 16
CMP_HID = 128
SEL_BLOCK = 64
SEL_TOPK = 16
WINDOW = 512
Q_BLOCK = 128
D_FF = 2816
ROPE_THETA = 10000.0
EPS = 1e-6
NEG_INF = -1e30
TINY = 1e-30
SEL_FORCE = 1e6
SEL_NEG = -1e6

LANES = 128
KEY_TILE = 512
VMEM_LIMIT = 48 * 1024 * 1024


def _cparams(sem):
    return pltpu.CompilerParams(dimension_semantics=sem, vmem_limit_bytes=VMEM_LIMIT)


def _nt_dot(a, b):
    return lax.dot_general(a, b, (((1,), (1,)), ((), ())), preferred_element_type=F32)


def _tn_dot(a, b):
    return lax.dot_general(a, b, (((0,), (0,)), ((), ())), preferred_element_type=F32)


def _gelu(x):
    return 0.5 * x * (1.0 + jnp.tanh(math.sqrt(2.0 / math.pi) * (x + 0.044715 * (x * x * x))))


def _rms_rows(x, g):
    r = lax.rsqrt(jnp.mean(x * x, axis=-1, keepdims=True) + EPS)
    return x * r * g


def _head_ms(x, bd):
    x2 = x * x
    hi = x2.astype(BF16)
    lo = (x2 - hi.astype(F32)).astype(BF16)
    s = jnp.dot(hi, bd, preferred_element_type=F32) + jnp.dot(lo, bd, preferred_element_type=F32)
    return s * (1.0 / HEAD_DIM)


def _rope64(x, cos, sin):
    lane = lax.broadcasted_iota(jnp.int32, x.shape, 1)
    sw = jnp.where((lane & 63) < 32, pltpu.roll(x, 96, 1), pltpu.roll(x, 32, 1))
    return x * cos + sw * sin


def _pack_pair(k2, v2, dtype):
    lane = lax.broadcasted_iota(jnp.int32, k2.shape, 1)
    lo = lane < HEAD_DIM
    even = jnp.where(lo, k2, pltpu.roll(v2, HEAD_DIM, 1)).astype(dtype)
    odd = jnp.where(lo, pltpu.roll(k2, HEAD_DIM, 1), v2).astype(dtype)
    return even, odd


def _norm_matmul_kernel(x_ref, g_ref, w_ref, o_ref, h_ref):
    @pl.when(pl.program_id(1) == 0)
    def _():
        h_ref[...] = _rms_rows(x_ref[...], g_ref[...]).astype(BF16)

    o_ref[...] = jnp.dot(h_ref[...], w_ref[...], preferred_element_type=F32).astype(o_ref.dtype)


def norm_matmul(x, g, w, tm, tn):
    M, D = x.shape
    N = w.shape[1]
    return pl.pallas_call(
        _norm_matmul_kernel,
        out_shape=jax.ShapeDtypeStruct((M, N), F32),
        grid=(M // tm, N // tn),
        in_specs=[pl.BlockSpec((tm, D), lambda i, j: (i, 0)),
                  pl.BlockSpec((1, D), lambda i, j: (0, 0)),
                  pl.BlockSpec((D, tn), lambda i, j: (0, j))],
        out_specs=pl.BlockSpec((tm, tn), lambda i, j: (i, j)),
        scratch_shapes=[pltpu.VMEM((tm, D), BF16)],
        compiler_params=_cparams(("parallel", "arbitrary")),
        name="norm_matmul",
    )(x, g.reshape(1, D), w)


def _matmul_res_kernel(a_ref, w_ref, r_ref, o_ref):
    o_ref[...] = r_ref[...] + jnp.dot(a_ref[...].astype(BF16), w_ref[...],
                                      preferred_element_type=F32)


def matmul_res(a, w, res, tm):
    M, K = a.shape
    N = w.shape[1]
    return pl.pallas_call(
        _matmul_res_kernel,
        out_shape=jax.ShapeDtypeStruct((M, N), F32),
        grid=(M // tm,),
        in_specs=[pl.BlockSpec((tm, K), lambda i: (i, 0)),
                  pl.BlockSpec((K, N), lambda i: (0, 0)),
                  pl.BlockSpec((tm, N), lambda i: (i, 0))],
        out_specs=pl.BlockSpec((tm, N), lambda i: (i, 0)),
        compiler_params=_cparams(("parallel",)),
        name="matmul_res",
    )(a, w, res)


def _retention_kernel(lg_ref, gl_ref, q_ref, k_ref, v_ref, g_ref, cos_ref, sin_ref, s0_ref,
                      o_ref, sout_ref, S_ref, *, L, n_chunk):
    h = pl.program_id(1)
    t = pl.program_id(2)
    lg = lg_ref[h]
    gl = gl_ref[h]

    @pl.when(t == 0)
    def _():
        S_ref[...] = s0_ref[0, 0]

    ii = lax.broadcasted_iota(jnp.int32, (L, L), 0)
    jj = lax.broadcasted_iota(jnp.int32, (L, L), 1)
    diff = (ii - jj).astype(F32)
    decay = jnp.where(diff >= 0, jnp.exp(jnp.maximum(diff, 0.0) * lg), 0.0)
    idx = lax.broadcasted_iota(jnp.int32, (L, 1), 0).astype(F32)
    q_dec = jnp.exp((idx + 1.0) * lg)
    k_dec = jnp.exp((L - 1.0 - idx) * lg)
    half = RET_DK // 2

    for c in range(n_chunk):
        rows = pl.ds(c * L, L)
        cos = cos_ref[rows, :]
        sin = sin_ref[rows, :]

        def rope(x):
            x1, x2 = x[:, :half], x[:, half:]
            return jnp.concatenate([x1 * cos - x2 * sin, x2 * cos + x1 * sin], axis=1)

        qr = rope(q_ref[0, rows, :])
        kr = rope(k_ref[0, rows, :]) * (RET_DK ** -0.5)
        qb = qr.astype(BF16)
        vb = v_ref[0, rows, :].astype(BF16)
        sc = _nt_dot(qb, kr.astype(BF16)) * decay
        S = S_ref[...]
        o = jnp.dot(sc.astype(BF16), vb, preferred_element_type=F32)
        o = o + jnp.dot(qb, S.astype(BF16), preferred_element_type=F32) * q_dec
        S_ref[...] = S * gl + _tn_dot((kr * k_dec).astype(BF16), vb)
        on = o * lax.rsqrt(jnp.mean(o * o, axis=-1, keepdims=True) + EPS)
        g = g_ref[0, rows, :]
        o_ref[0, rows, :] = (on * (g * jax.nn.sigmoid(g))).astype(o_ref.dtype)

    @pl.when(t == pl.num_programs(2) - 1)
    def _():
        sout_ref[0, 0] = S_ref[...]


def retention(proj, s0, cos, sin, lg, gl, L, tb):
    B, T, _ = proj.shape
    n_chunk = tb // L
    odt = BF16 if tb % 16 == 0 else F32
    kern = functools.partial(_retention_kernel, L=L, n_chunk=n_chunk)
    grid_spec = pltpu.PrefetchScalarGridSpec(
        num_scalar_prefetch=2,
        grid=(B, RET_HEADS, T // tb),
        in_specs=[
            pl.BlockSpec((1, tb, RET_DK), lambda b, h, t, *_: (b, t, h)),
            pl.BlockSpec((1, tb, RET_DK), lambda b, h, t, *_: (b, t, RET_HEADS + h)),
            pl.BlockSpec((1, tb, RET_DV), lambda b, h, t, *_: (b, t, RET_HEADS + h)),
            pl.BlockSpec((1, tb, RET_DV), lambda b, h, t, *_: (b, t, 2 * RET_HEADS + h)),
            pl.BlockSpec((tb, RET_DK // 2), lambda b, h, t, *_: (t, 0)),
            pl.BlockSpec((tb, RET_DK // 2), lambda b, h, t, *_: (t, 0)),
            pl.BlockSpec((1, 1, RET_DK, RET_DV), lambda b, h, t, *_: (b, h, 0, 0)),
        ],
        out_specs=[
            pl.BlockSpec((1, tb, RET_DV), lambda b, h, t, *_: (b, t, h)),
            pl.BlockSpec((1, 1, RET_DK, RET_DV), lambda b, h, t, *_: (b, h, 0, 0)),
        ],
        scratch_shapes=[pltpu.VMEM((RET_DK, RET_DV), F32)],
    )
    return pl.pallas_call(
        kern,
        out_shape=[jax.ShapeDtypeStruct((B, T, RET_HEADS * RET_DV), odt),
                   jax.ShapeDtypeStruct((B, RET_HEADS, RET_DK, RET_DV), F32)],
        grid_spec=grid_spec,
        compiler_params=_cparams(("parallel", "parallel", "arbitrary")),
        name="retention",
    )(lg, gl, proj, proj, proj, proj, cos, sin, s0)


def _ffn_mid_kernel(u_ref, gt_ref, halo_ref, cw_ref, cb_ref, o_ref):
    u = u_ref[0]
    hl = halo_ref[0, 0]
    row = lax.broadcasted_iota(jnp.int32, u.shape, 0)
    u1 = jnp.where(row == 0, hl[1:2], pltpu.roll(u, 1, 0))
    u2 = jnp.where(row == 0, hl[0:1], jnp.where(row == 1, hl[1:2], pltpu.roll(u, 2, 0)))
    c = cb_ref[...] + cw_ref[0:1] * u2
    c = c + cw_ref[1:2] * u1
    c = c + cw_ref[2:3] * u
    o_ref[0] = (_gelu(c) * gt_ref[0]).astype(o_ref.dtype)


def ffn_mid(proj, buf, conv_w, conv_b, tt):
    B, T, _ = proj.shape
    nt = T // tt
    if nt > 1:
        tails = proj[:, :, :D_FF].reshape(B, nt, tt, D_FF)[:, :-1, tt - 2:, :]
        halo = jnp.concatenate([buf[:, None], tails], axis=1)
    else:
        halo = buf[:, None]
    odt = BF16 if tt % 16 == 0 else F32
    return pl.pallas_call(
        _ffn_mid_kernel,
        out_shape=jax.ShapeDtypeStruct((B, T, D_FF), odt),
        grid=(B, nt),
        in_specs=[pl.BlockSpec((1, tt, D_FF), lambda b, t: (b, t, 0)),
                  pl.BlockSpec((1, tt, D_FF), lambda b, t: (b, t, 1)),
                  pl.BlockSpec((1, 1, 2, D_FF), lambda b, t: (b, t, 0, 0)),
                  pl.BlockSpec((3, D_FF), lambda b, t: (0, 0)),
                  pl.BlockSpec((1, D_FF), lambda b, t: (0, 0))],
        out_specs=pl.BlockSpec((1, tt, D_FF), lambda b, t: (b, t, 0)),
        compiler_params=_cparams(("parallel", "parallel")),
        name="ffn_mid",
    )(proj, proj, halo, conv_w, conv_b.reshape(1, D_FF))


def _kv_kernel(x_ref, g_ref, w_ref, gain_ref, cos_ref, sin_ref, bd_ref,
               kv_ref, selp_ref, winp_ref):
    h = _rms_rows(x_ref[0], g_ref[...]).astype(BF16)
    y = jnp.dot(h, w_ref[...], preferred_element_type=F32)
    cos = cos_ref[...]
    sin = sin_ref[...]
    bd = bd_ref[...]
    nk = N_KV * HEAD_DIM
    kv_ref[0, :, 0:2 * nk] = y[:, 0:2 * nk]
    for br, pack_ref in enumerate((selp_ref, winp_ref)):
        base = 2 * nk * (br + 1)
        kv_ref[0, :, base + nk:base + 2 * nk] = y[:, base + nk:base + 2 * nk]
        for p in range(2):
            kx = y[:, base + LANES * p:base + LANES * (p + 1)]
            kn = kx * lax.rsqrt(_head_ms(kx, bd) + EPS) * gain_ref[br + 1:br + 2, :]
            kr = _rope64(kn, cos, sin)
            kv_ref[0, :, base + LANES * p:base + LANES * (p + 1)] = kr
            vx = y[:, base + nk + LANES * p:base + nk + LANES * (p + 1)]
            even, odd = _pack_pair(kr, vx, pack_ref.dtype)
            pack_ref[0, 2 * p] = even
            pack_ref[0, 2 * p + 1] = odd


def kv_project(x, g, w, gains, cos, sin, bd, tm):
    B, T, D = x.shape
    N = w.shape[1]
    pdt = BF16 if tm % 16 == 0 else F32
    return pl.pallas_call(
        _kv_kernel,
        out_shape=[jax.ShapeDtypeStruct((B, T, N), F32),
                   jax.ShapeDtypeStruct((B, N_KV, T, LANES), pdt),
                   jax.ShapeDtypeStruct((B, N_KV, T, LANES), pdt)],
        grid=(B, T // tm),
        in_specs=[pl.BlockSpec((1, tm, D), lambda b, t: (b, t, 0)),
                  pl.BlockSpec((1, D), lambda b, t: (0, 0)),
                  pl.BlockSpec((D, N), lambda b, t: (0, 0)),
                  pl.BlockSpec((3, LANES), lambda b, t: (0, 0)),
                  pl.BlockSpec((tm, LANES), lambda b, t: (t, 0)),
                  pl.BlockSpec((tm, LANES), lambda b, t: (t, 0)),
                  pl.BlockSpec((LANES, LANES), lambda b, t: (0, 0))],
        out_specs=[pl.BlockSpec((1, tm, N), lambda b, t: (b, t, 0)),
                   pl.BlockSpec((1, N_KV, tm, LANES), lambda b, t: (b, 0, t, 0)),
                   pl.BlockSpec((1, N_KV, tm, LANES), lambda b, t: (b, 0, t, 0))],
        compiler_params=_cparams(("parallel", "parallel")),
        name="kv_project",
    )(x, g.reshape(1, D), w, gains, cos, sin, bd)


def _qg_kernel(x_ref, g_ref, w_ref, gain_ref, cos_ref, sin_ref, bd_ref,
               qc_ref, qr_ref, gate_ref, *, qb, n_qb):
    h = _rms_rows(x_ref[0], g_ref[...]).astype(BF16)
    y = jnp.dot(h, w_ref[...], preferred_element_type=F32)
    cos = cos_ref[...]
    sin = sin_ref[...]
    bd = bd_ref[...]
    nq = N_HEADS * HEAD_DIM
    gate_ref[0] = jax.nn.sigmoid(y[:, nq:nq + LANES])
    lane = lax.broadcasted_iota(jnp.int32, (y.shape[0], LANES), 1)
    lo = lane < HEAD_DIM
    group = N_HEADS // N_KV
    for p in range(N_HEADS // 2):
        qx = y[:, LANES * p:LANES * (p + 1)]
        qn = qx * lax.rsqrt(_head_ms(qx, bd) + EPS) * gain_ref[...]
        qr = _rope64(qn, cos, sin)
        for src, dst in ((qn, qc_ref), (qr, qr_ref)):
            for par in range(2):
                hd = 2 * p + par
                kvh, gi = hd // group, hd % group
                v = src if par == 0 else pltpu.roll(src, HEAD_DIM, 1)
                v = jnp.where(lo, v, 0.0).astype(dst.dtype)
                for j in range(n_qb):
                    dst[0, kvh, j, gi * qb:(gi + 1) * qb, :] = v[j * qb:(j + 1) * qb, :]


def qg_project(x, g, w, gain, cos, sin, bd, tm, qb):
    B, T, D = x.shape
    N = w.shape[1]
    n_qb = tm // qb
    group = N_HEADS // N_KV
    qdt = BF16 if qb % 16 == 0 else F32
    kern = functools.partial(_qg_kernel, qb=qb, n_qb=n_qb)
    qshape = jax.ShapeDtypeStruct((B, N_KV, T // qb, group * qb, LANES), qdt)
    qspec = pl.BlockSpec((1, N_KV, n_qb, group * qb, LANES), lambda b, t: (b, 0, t, 0, 0))
    return pl.pallas_call(
        kern,
        out_shape=[qshape, qshape, jax.ShapeDtypeStruct((B, T, LANES), F32)],
        grid=(B, T // tm),
        in_specs=[pl.BlockSpec((1, tm, D), lambda b, t: (b, t, 0)),
                  pl.BlockSpec((1, D), lambda b, t: (0, 0)),
                  pl.BlockSpec((D, N), lambda b, t: (0, 0)),
                  pl.BlockSpec((1, LANES), lambda b, t: (0, 0)),
                  pl.BlockSpec((tm, LANES), lambda b, t: (t, 0)),
                  pl.BlockSpec((tm, LANES), lambda b, t: (t, 0)),
                  pl.BlockSpec((LANES, LANES), lambda b, t: (0, 0))],
        out_specs=[qspec, qspec, pl.BlockSpec((1, tm, LANES), lambda b, t: (b, t, 0))],
        compiler_params=_cparams(("parallel", "parallel")),
        name="qg_project",
    )(x, g.reshape(1, D), w, gain, cos, sin, bd)


def _pos_bias_kernel(p_ref, w_ref, o_ref):
    o_ref[0] = jnp.dot(p_ref[0].astype(BF16), w_ref[0].astype(BF16), preferred_element_type=F32)


def pos_bias(cmp_pos, cmp_w1):
    K = CMP_LEN * HEAD_DIM
    p = jnp.broadcast_to(cmp_pos.reshape(2, 1, K), (2, 8, K))
    out = pl.pallas_call(
        _pos_bias_kernel,
        out_shape=jax.ShapeDtypeStruct((2, 8, CMP_HID), F32),
        grid=(2,),
        in_specs=[pl.BlockSpec((1, 8, K), lambda c: (c, 0, 0)),
                  pl.BlockSpec((1, K, CMP_HID), lambda c: (c, 0, 0))],
        out_specs=pl.BlockSpec((1, 8, CMP_HID), lambda c: (c, 0, 0)),
        name="pos_bias",
    )(p, cmp_w1)
    return out[:, 0, :]


def _compress_kernel(pt_ref, *refs, n_pg):
    pages = refs[:n_pg]
    w1_ref, pb_ref, w2_ref, gain_ref, out_ref, carry_ref, slab_ref = refs[n_pg:]
    g = pl.program_id(1)

    @pl.when(g == 0)
    def _():
        carry_ref[...] = jnp.zeros_like(carry_ref)

    n = n_pg * (PAGE // CMP_STRIDE)
    lane = lax.broadcasted_iota(jnp.int32, (n, LANES), 1)
    row = lax.broadcasted_iota(jnp.int32, (n, LANES), 0)
    lo = lane < HEAD_DIM
    for pair in range(N_KV // 2):
        res = [jnp.zeros((n, LANES), F32), jnp.zeros((n, LANES), F32)]
        for c in range(2):
            col = c * N_KV * HEAD_DIM + pair * LANES
            for i in range(n_pg):
                slab_ref[PAGE * i:PAGE * (i + 1), :] = pages[i][0, :, col:col + LANES]
            for par in range(2):
                keep = lo if par == 0 else jnp.logical_not(lo)
                acc = jnp.zeros((n, 2 * CMP_HID), F32)
                for s in range(CMP_STRIDE):
                    xs = slab_ref[pl.ds(s, n, stride=CMP_STRIDE), :]
                    xm = jnp.where(keep, xs, 0.0).astype(BF16)
                    acc = acc + jnp.dot(xm, w1_ref[c, s], preferred_element_type=F32)
                p0 = acc[:, :CMP_HID]
                p1 = acc[:, CMP_HID:]
                ci = 2 * (2 * pair + par) + c
                prev = jnp.where(row == 0, carry_ref[ci, 7:8, :], pltpu.roll(p0, 1, 0))
                carry_ref[ci] = p0[n - 8:n, :]
                hid = _gelu(prev + p1 + pb_ref[c:c + 1, :]).astype(BF16)
                res[par] = res[par] + jnp.dot(hid, w2_ref[c], preferred_element_type=F32)
        for par in range(2):
            r = res[par]
            ms = jnp.sum(jnp.where(lo, r * r, 0.0), axis=-1, keepdims=True) * (1.0 / HEAD_DIM)
            kn = r * lax.rsqrt(ms + EPS) * gain_ref[...]
            out_ref[0, 2 * pair + par] = jnp.where(lo, kn, r).astype(out_ref.dtype)


def _page_index(b, g, pt_ref, *, i, n_pg):
    return (pt_ref[b, g * n_pg + i], 0, 0)


def compress(pages_arr, table, w1dup, pb, w2p, gain):
    B, n_pages = table.shape
    n_pg = 16 if n_pages % 16 == 0 else n_pages
    n = n_pg * (PAGE // CMP_STRIDE)
    n_sub = n_pages * (PAGE // CMP_STRIDE)
    width = 2 * N_KV * HEAD_DIM
    in_specs = [pl.BlockSpec((1, PAGE, width), functools.partial(_page_index, i=i, n_pg=n_pg))
                for i in range(n_pg)]
    in_specs += [pl.BlockSpec(w1dup.shape, lambda b, g, pt: (0, 0, 0, 0)),
                 pl.BlockSpec(pb.shape, lambda b, g, pt: (0, 0)),
                 pl.BlockSpec(w2p.shape, lambda b, g, pt: (0, 0, 0)),
                 pl.BlockSpec((1, LANES), lambda b, g, pt: (0, 0))]
    grid_spec = pltpu.PrefetchScalarGridSpec(
        num_scalar_prefetch=1,
        grid=(B, n_pages // n_pg),
        in_specs=in_specs,
        out_specs=pl.BlockSpec((1, N_KV, n, LANES), lambda b, g, pt: (b, 0, g, 0)),
        scratch_shapes=[pltpu.VMEM((2 * N_KV, 8, CMP_HID), F32),
                        pltpu.VMEM((n_pg * PAGE, LANES), F32)],
    )
    return pl.pallas_call(
        functools.partial(_compress_kernel, n_pg=n_pg),
        out_shape=jax.ShapeDtypeStruct((B, N_KV, n_sub, LANES), BF16),
        grid_spec=grid_spec,
        compiler_params=_cparams(("parallel", "arbitrary")),
        name="compress",
    )(table, *([pages_arr] * n_pg), w1dup, pb, w2p, gain)


def _repack_kernel(pt_ref, p0, p1, p2, p3, new_ref, out_ref, *, n_full, t_new):
    t = pl.program_id(1)
    nk = N_KV * HEAD_DIM

    @pl.when(t < n_full)
    def _():
        for i, pg in enumerate((p0, p1, p2, p3)):
            x = pg[0]
            for p in range(2):
                even, odd = _pack_pair(x[:, LANES * p:LANES * (p + 1)],
                                       x[:, nk + LANES * p:nk + LANES * (p + 1)], out_ref.dtype)
                out_ref[0, 2 * p, PAGE * i:PAGE * (i + 1), :] = even
                out_ref[0, 2 * p + 1, PAGE * i:PAGE * (i + 1), :] = odd

    @pl.when(t == n_full)
    def _():
        pad = jnp.zeros((N_KV, KEY_TILE - t_new, LANES), F32)
        out_ref[0] = jnp.concatenate([new_ref[0].astype(F32), pad], axis=1).astype(out_ref.dtype)


def _repack_page_index(b, t, pt_ref, *, i, n_pages):
    return (pt_ref[b, jnp.minimum(4 * t + i, n_pages - 1)], 0, 0)


def repack(pages_arr, table, new_pack):
    B, n_pages = table.shape
    n_full = n_pages // 4
    t_new = new_pack.shape[2]
    width = 2 * N_KV * HEAD_DIM
    in_specs = [pl.BlockSpec((1, PAGE, width),
                             functools.partial(_repack_page_index, i=i, n_pages=n_pages))
                for i in range(4)]
    in_specs.append(pl.BlockSpec((1, N_KV, t_new, LANES), lambda b, t, pt: (b, 0, 0, 0)))
    grid_spec = pltpu.PrefetchScalarGridSpec(
        num_scalar_prefetch=1,
        grid=(B, n_full + 1),
        in_specs=in_specs,
        out_specs=pl.BlockSpec((1, N_KV, KEY_TILE, LANES), lambda b, t, pt: (b, 0, t, 0)),
    )
    return pl.pallas_call(
        functools.partial(_repack_kernel, n_full=n_full, t_new=t_new),
        out_shape=jax.ShapeDtypeStruct((B, N_KV, (n_full + 1) * KEY_TILE, LANES), BF16),
        grid_spec=grid_spec,
        compiler_params=_cparams(("parallel", "arbitrary")),
        name="repack",
    )(table, pages_arr, pages_arr, pages_arr, pages_arr, new_pack)


def _softmax_step(s, mask, kv, m_ref, l_ref, a_ref, k):
    s = jnp.where(mask, s, NEG_INF)
    m_old = m_ref[k]
    m_new = jnp.maximum(m_old, jnp.max(s, axis=-1, keepdims=True))
    alpha = jnp.exp(m_old - m_new)
    p = jnp.where(mask, jnp.exp(s - m_new), 0.0)
    l_ref[k] = alpha * l_ref[k] + jnp.sum(p, axis=-1, keepdims=True)
    a_ref[k] = alpha * a_ref[k] + jnp.dot(p.astype(BF16), kv, preferred_element_type=F32)
    m_ref[k] = m_new


def _nsa_kernel(qi_ref, kt_ref, wt_ref, wf_ref, last_ref,
                qc_ref, qr_ref, gate_ref, cmp_ref, sel_ref, win_ref, wmap_ref,
                o_ref,
                selm_ref, oc_ref, ms_ref, ls_ref, as_ref, mw_ref, lw_ref, aw_ref,
                *, qb, n_sel, n_selp, n_cmp, q0, w_off):
    step = pl.program_id(1)
    qi = qi_ref[step]
    kt = kt_ref[step]
    group = N_HEADS // N_KV
    R = group * qb
    rowq = lax.broadcasted_iota(jnp.int32, (R, 1), 0) & (qb - 1)
    qpos = q0 + qi * qb + rowq
    qpos_q = q0 + qi * qb + lax.broadcasted_iota(jnp.int32, (qb, 1), 0)

    @pl.when(kt == 0)
    def _first():
        for m_ref, l_ref, a_ref in ((ms_ref, ls_ref, as_ref), (mw_ref, lw_ref, aw_ref)):
            m_ref[...] = jnp.full(m_ref.shape, NEG_INF, F32)
            l_ref[...] = jnp.zeros(l_ref.shape, F32)
            a_ref[...] = jnp.zeros(a_ref.shape, F32)
        n_idx = lax.broadcasted_iota(jnp.int32, (1, n_cmp), 1)
        cvalid = (n_idx >= 1) & ((n_idx - 1) * CMP_STRIDE + CMP_LEN - 1 <= qpos)
        blk = lax.broadcasted_iota(jnp.int32, (qb, n_selp), 1)
        blk_f = blk.astype(F32)
        cur = qpos_q >> 6
        forced = (blk == 0) | (blk == cur) | (blk == cur - 1)
        reach = blk * SEL_BLOCK <= qpos_q
        real = blk < n_sel
        for k in range(N_KV):
            ckv = cmp_ref[0, k]
            s = _nt_dot(qc_ref[0, k, 0].astype(BF16), ckv)
            s = jnp.where(cvalid, s, NEG_INF)
            m = jnp.max(s, axis=-1, keepdims=True)
            e = jnp.where(cvalid, jnp.exp(s - m), 0.0)
            p = e / jnp.maximum(jnp.sum(e, axis=-1, keepdims=True), TINY)
            oc_ref[k] = jnp.dot(p.astype(BF16), ckv, preferred_element_type=F32)
            psum = p[0:qb]
            for gi in range(1, group):
                psum = psum + p[gi * qb:(gi + 1) * qb]
            hi = psum.astype(BF16)
            lo = (psum - hi.astype(F32)).astype(BF16)
            imp = _nt_dot(hi, wmap_ref[...]) + _nt_dot(lo, wmap_ref[...])
            v = jnp.where(forced, SEL_FORCE, jnp.where(reach, imp, SEL_NEG))
            v = jnp.where(real, v, -jnp.inf)

            def pick_one(_, carry):
                v, sel = carry
                m = jnp.max(v, axis=-1, keepdims=True)
                first = jnp.min(jnp.where(v == m, blk_f, float(n_selp)), axis=-1, keepdims=True)
                pick = blk_f == first
                sel = jnp.where(pick & (m > 0.5 * SEL_NEG), 1.0, sel)
                return jnp.where(pick, -jnp.inf, v), sel

            _, sel = lax.fori_loop(0, SEL_TOPK, pick_one, (v, jnp.zeros((qb, n_selp), F32)))
            selm_ref[k] = sel

    tok = kt * KEY_TILE + lax.broadcasted_iota(jnp.int32, (1, KEY_TILE), 1)
    blk_of_tok = tok >> 6
    expand = jnp.where(
        lax.broadcasted_iota(jnp.int32, (n_selp, KEY_TILE), 0) == blk_of_tok, 1.0, 0.0).astype(BF16)
    causal = tok <= qpos
    for k in range(N_KV):
        kv = sel_ref[0, k]
        s = _nt_dot(qr_ref[0, k, 0].astype(BF16), kv)
        sm = jnp.dot(selm_ref[k].astype(BF16), expand, preferred_element_type=F32)
        sm = jnp.concatenate([sm] * group, axis=0)
        _softmax_step(s, causal & (sm > 0.5), kv, ms_ref, ls_ref, as_ref, k)

    @pl.when(wf_ref[step] == 1)
    def _window():
        wpos = w_off + wt_ref[step] * KEY_TILE + lax.broadcasted_iota(jnp.int32, (1, KEY_TILE), 1)
        mask = (wpos <= qpos) & (wpos > qpos - WINDOW) & (wpos >= 0)
        for k in range(N_KV):
            kv = win_ref[0, k]
            s = _nt_dot(qr_ref[0, k, 0].astype(BF16), kv)
            _softmax_step(s, mask, kv, mw_ref, lw_ref, aw_ref, k)

    @pl.when(last_ref[step] == 1)
    def _finish():
        gate = gate_ref[0]
        for k in range(N_KV):
            for gi in range(group):
                hd = k * group + gi
                rs = slice(gi * qb, (gi + 1) * qb)
                o_s = as_ref[k, rs, :] / jnp.maximum(ls_ref[k, rs, :], TINY)
                o_w = aw_ref[k, rs, :] / jnp.maximum(lw_ref[k, rs, :], TINY)
                o = gate[:, 3 * hd:3 * hd + 1] * oc_ref[k, rs, :]
                o = o + gate[:, 3 * hd + 1:3 * hd + 2] * o_s
                o = o + gate[:, 3 * hd + 2:3 * hd + 3] * o_w
                o_ref[0, :, LANES * hd:LANES * (hd + 1)] = o.astype(o_ref.dtype)


def _nsa_tables(T, qb, q0, w_off):
    rows = []
    for qi in range(T // qb):
        q_lo = q0 + qi * qb
        q_hi = q_lo + qb - 1
        last_kt = q_hi // KEY_TILE
        w_lo = max(q_lo - WINDOW + 1, w_off)
        wt0 = (w_lo - w_off) // KEY_TILE
        wt1 = (q_hi - w_off) // KEY_TILE
        n_w = wt1 - wt0 + 1
        assert n_w <= last_kt + 1
        for kt in range(last_kt + 1):
            rows.append((qi, kt, wt0 + min(kt, n_w - 1), int(kt < n_w), int(kt == last_kt)))
    tab = np.asarray(rows, np.int32).T
    return [jnp.asarray(tab[i]) for i in range(5)]


def _overlap_map(n_cmp_rows, n_selp):
    m = np.arange(n_cmp_rows)[None, :]
    s = np.arange(n_selp)[:, None]
    c0 = (m - 1) * CMP_STRIDE
    ov = np.minimum(c0 + CMP_LEN, s * SEL_BLOCK + SEL_BLOCK) - np.maximum(c0, s * SEL_BLOCK)
    w = np.maximum(ov, 0).astype(np.float32) / CMP_LEN
    w[:, 0] = 0.0
    return jnp.asarray(w, BF16)


def nsa_attend(qc, qr, gates, cmp_p, sel_p, win_p, T, qb, q0, w_off, n_sel):
    B = qc.shape[0]
    group = N_HEADS // N_KV
    R = group * qb
    n_cmp = cmp_p.shape[2]
    n_selp = -(-n_sel // LANES) * LANES
    tabs = _nsa_tables(T, qb, q0, w_off)
    n_steps = int(tabs[0].shape[0])
    wmap = _overlap_map(n_cmp, n_selp)
    odt = BF16 if qb % 16 == 0 else F32
    kern = functools.partial(_nsa_kernel, qb=qb, n_sel=n_sel, n_selp=n_selp, n_cmp=n_cmp,
                             q0=q0, w_off=w_off)
    qspec = pl.BlockSpec((1, N_KV, 1, R, LANES), lambda b, s, qi, kt, wt, wf, la: (b, 0, qi[s], 0, 0))
    grid_spec = pltpu.PrefetchScalarGridSpec(
        num_scalar_prefetch=5,
        grid=(B, n_steps),
        in_specs=[
            qspec, qspec,
            pl.BlockSpec((1, qb, LANES), lambda b, s, qi, kt, wt, wf, la: (b, qi[s], 0)),
            pl.BlockSpec((1, N_KV, n_cmp, LANES), lambda b, s, qi, kt, wt, wf, la: (b, 0, 0, 0)),
            pl.BlockSpec((1, N_KV, KEY_TILE, LANES), lambda b, s, qi, kt, wt, wf, la: (b, 0, kt[s], 0)),
            pl.BlockSpec((1, N_KV, KEY_TILE, LANES), lambda b, s, qi, kt, wt, wf, la: (b, 0, wt[s], 0)),
            pl.BlockSpec((n_selp, n_cmp), lambda b, s, qi, kt, wt, wf, la: (0, 0)),
        ],
        out_specs=pl.BlockSpec((1, qb, N_HEADS * LANES), lambda b, s, qi, kt, wt, wf, la: (b, qi[s], 0)),
        scratch_shapes=[
            pltpu.VMEM((N_KV, qb, n_selp), F32),
            pltpu.VMEM((N_KV, R, LANES), F32),
            pltpu.VMEM((N_KV, R, 1), F32), pltpu.VMEM((N_KV, R, 1), F32), pltpu.VMEM((N_KV, R, LANES), F32),
            pltpu.VMEM((N_KV, R, 1), F32), pltpu.VMEM((N_KV, R, 1), F32), pltpu.VMEM((N_KV, R, LANES), F32),
        ],
    )
    return pl.pallas_call(
        kern,
        out_shape=jax.ShapeDtypeStruct((B, T, N_HEADS * LANES), odt),
        grid_spec=grid_spec,
        compiler_params=_cparams(("parallel", "arbitrary")),
        name="nsa_attend",
    )(*tabs, qc, qr, gates, cmp_p, sel_p, win_p, wmap)


def _rope_tables(pos, half):
    inv = jnp.exp(-math.log(ROPE_THETA) * jnp.arange(half, dtype=F32) / half)
    ang = pos.astype(F32)[:, None] * inv[None, :]
    return jnp.cos(ang), jnp.sin(ang)


def _prep_weights(ret_w_in, ret_w_out, ffn_w_in, ffn_w_out, kv_w, kv_knorm, cmp_w1, cmp_w2,
                  nsa_w_qg, nsa_qnorm, nsa_w_o):
    n_b = nsa_w_qg.shape[0]
    nq = N_HEADS * HEAD_DIM
    qg_pad = nq + LANES - nsa_w_qg.shape[2]
    w_qg = jnp.pad(nsa_w_qg, ((0, 0), (0, 0), (0, qg_pad))).astype(BF16)
    w_o = jnp.pad(nsa_w_o.reshape(n_b, N_HEADS, 1, HEAD_DIM, D_MODEL),
                  ((0, 0), (0, 0), (1, 0), (0, 0), (0, 0))).reshape(n_b, N_HEADS * LANES, D_MODEL)
    R = CMP_LEN // CMP_STRIDE
    w1 = cmp_w1.reshape(2, R, CMP_STRIDE, HEAD_DIM, CMP_HID).transpose(0, 2, 3, 1, 4)
    w1 = w1.reshape(2, CMP_STRIDE, HEAD_DIM, R * CMP_HID)
    w1dup = jnp.concatenate([w1, w1], axis=2).astype(BF16)
    z = jnp.zeros((CMP_HID, HEAD_DIM), F32)
    w2p = jnp.stack([jnp.concatenate([cmp_w2[0], z], axis=1),
                     jnp.concatenate([z, cmp_w2[1]], axis=1)]).astype(BF16)
    ones = jnp.ones((HEAD_DIM,), F32)
    return dict(
        ret_w_in=ret_w_in.astype(BF16), ret_w_out=ret_w_out.astype(BF16),
        ffn_w_in=ffn_w_in.astype(BF16), ffn_w_out=ffn_w_out.astype(BF16),
        kv_w=kv_w.astype(BF16), w_qg=w_qg, w_o=w_o.astype(BF16),
        kv_gain=jnp.tile(kv_knorm, (1, 2)),
        cmp_gain=jnp.concatenate([kv_knorm[0], ones]).reshape(1, LANES),
        q_gain=jnp.tile(nsa_qnorm, (1, 2)) * (HEAD_DIM ** -0.5),
        w1dup=w1dup, w2p=w2p,
        bd=jnp.asarray(np.kron(np.eye(2), np.ones((HEAD_DIM, HEAD_DIM))), BF16),
    )


def _trunk(x, past_len, ret_s0, conv0, ctx, W, P):
    B, T, D = x.shape
    M = B * T
    depth = P["norm_mix"].shape[0]
    n_a = P["ret_w_in"].shape[0]
    pos = past_len + jnp.arange(T)
    cos_r, sin_r = _rope_tables(pos, RET_DK // 2)
    c32, s32 = _rope_tables(pos, HEAD_DIM // 2)
    cos_n = jnp.tile(c32, (1, 4))
    sin_n = jnp.concatenate([-s32, s32, -s32, s32], axis=1)
    lg = jnp.log1p(-jnp.exp2(-5.0 - jnp.arange(RET_HEADS, dtype=F32)))
    L = RET_CHUNK if T % RET_CHUNK == 0 else T
    gl = jnp.exp(L * lg)
    tm = min(512, M)
    tb = min(512, T)
    tq = min(512, T)
    qb = Q_BLOCK if T % Q_BLOCK == 0 else T
    tt = min(256, T)

    x2 = x.reshape(M, D)
    ret_states, conv_states = [], []
    for layer in range(depth):
        if layer == n_a:
            kvp, selp, winp = kv_project(x2.reshape(B, T, D), P["kv_norm"], W["kv_w"], W["kv_gain"],
                                         cos_n, sin_n, W["bd"], tq)
            nk2 = 2 * N_KV * HEAD_DIM
            if ctx is None:
                table = jnp.arange(M // PAGE, dtype=jnp.int32).reshape(B, T // PAGE)
                cmp_p = compress(kvp.reshape(M // PAGE, PAGE, 3 * nk2), table,
                                 W["w1dup"], W["pb"], W["w2p"], W["cmp_gain"])
                sel_p, win_p = selp, winp
                w_off = 0
            else:
                cache_cmp, cache_sel, cache_win, table = ctx
                n_pool = cache_cmp.shape[0]
                cmp_p = compress(cache_cmp.reshape(n_pool, PAGE, nk2), table,
                                 W["w1dup"], W["pb"], W["w2p"], W["cmp_gain"])
                sel_p = repack(cache_sel.reshape(n_pool, PAGE, nk2), table, selp)
                wl = cache_win.shape[1]
                wtab = jnp.arange(B * (wl // PAGE), dtype=jnp.int32).reshape(B, wl // PAGE)
                win_p = repack(cache_win.reshape(B * (wl // PAGE), PAGE, nk2), wtab, winp)
                w_off = past_len - wl
            n_sel = -(-(past_len + T) // SEL_BLOCK)
        h_norm = P["norm_mix"][layer]
        if layer < n_a:
            proj = norm_matmul(x2, h_norm, W["ret_w_in"][layer], tm, 1024)
            og, s_new = retention(proj.reshape(B, T, -1), ret_s0[layer], cos_r, sin_r, lg, gl, L, tb)
            ret_states.append(s_new)
            x2 = matmul_res(og.reshape(M, -1), W["ret_w_out"][layer], x2, tm)
        else:
            j = layer - n_a
            qc, qr, gates = qg_project(x2.reshape(B, T, D), h_norm, W["w_qg"][j], W["q_gain"][j:j + 1],
                                       cos_n, sin_n, W["bd"], tq, qb)
            o = nsa_attend(qc, qr, gates, cmp_p, sel_p, win_p, T, qb, past_len, w_off, n_sel)
            x2 = matmul_res(o.reshape(M, -1), W["w_o"][j], x2, tm)
        proj = norm_matmul(x2, P["norm_ffn"][layer], W["ffn_w_in"][layer], tm, 512)
        proj3 = proj.reshape(B, T, 2 * D_FF)
        act = ffn_mid(proj3, conv0[layer], P["ffn_conv_w"][layer], P["ffn_conv_b"][layer], tt)
        conv_states.append(proj3[:, T - 2:, :D_FF])
        x2 = matmul_res(act.reshape(M, D_FF), W["ffn_w_out"][layer], x2, tm)

    nk = N_KV * HEAD_DIM
    new_cmp = kvp[:, :, 0:2 * nk].reshape(B, T, 2, N_KV, HEAD_DIM)
    new_sel = kvp[:, :, 2 * nk:4 * nk].reshape(B, T, 2, N_KV, HEAD_DIM)
    new_win = kvp[:, :, 4 * nk:6 * nk].reshape(B, T, 2, N_KV, HEAD_DIM)
    return (x2.reshape(B, T, D), jnp.stack(ret_states), jnp.stack(conv_states),
            new_cmp, new_sel, new_win)


def kernel(x_prompt, x_sample, cache_cmp_kv, cache_sel_kv, cache_win_kv, state_ret, state_conv,
           page_table, norm_mix, norm_ffn, ret_w_in, ret_w_out, ffn_w_in, ffn_conv_w, ffn_conv_b,
           ffn_w_out, kv_norm, kv_w, kv_knorm, cmp_pos, cmp_w1, cmp_w2, nsa_w_qg, nsa_qnorm, nsa_w_o):
    W = _prep_weights(ret_w_in, ret_w_out, ffn_w_in, ffn_w_out, kv_w, kv_knorm, cmp_w1, cmp_w2,
                      nsa_w_qg, nsa_qnorm, nsa_w_o)
    W["pb"] = pos_bias(cmp_pos, cmp_w1)
    P = dict(norm_mix=norm_mix, norm_ffn=norm_ffn, ret_w_in=ret_w_in, ffn_conv_w=ffn_conv_w,
             ffn_conv_b=ffn_conv_b, kv_norm=kv_norm)
    depth = norm_mix.shape[0]
    n_a = ret_w_in.shape[0]
    B, T, _ = x_prompt.shape
    zero_ret = jnp.zeros((n_a, B, RET_HEADS, RET_DK, RET_DV), F32)
    zero_conv = jnp.zeros((depth, B, 2, D_FF), F32)
    y_p, ret_p, conv_p, cmp_p, sel_p, win_p = _trunk(x_prompt, 0, zero_ret, zero_conv, None, W, P)
    win_p = win_p[:, T - min(WINDOW, T):]

    db, ts, _ = x_sample.shape
    past_len = page_table.shape[1] * PAGE
    ctx = (cache_cmp_kv, cache_sel_kv, cache_win_kv, page_table)
    y_s, ret_s, conv_s, cmp_s, sel_s, win_new = _trunk(x_sample, past_len, state_ret, state_conv,
                                                        ctx, W, P)
    all_win = jnp.concatenate([cache_win_kv, win_new], axis=1)
    win_s = all_win[:, all_win.shape[1] - min(WINDOW, past_len + ts):]
    return (y_p, y_s, ret_p, ret_s, conv_p, conv_s, cmp_p, cmp_s, sel_p, sel_s, win_p, win_s)
```

```python
import functools
import math

import jax
import jax.numpy as jnp
import numpy as np
from jax import lax
from jax.experimental import pallas as pl
from jax.experimental.pallas import tpu as pltpu

F32 = jnp.float32
BF16 = jnp.bfloat16

D_MODEL = 1024
PAGE = 128
RET_HEADS = 4
RET_DK = 256
RET_DV = 512
RET_CHUNK = 128
N_HEADS = 16
N_KV = 4
HEAD_DIM = 64
CMP_LEN = 32
CMP_STRIDE = 16
CMP_HID = 128
SEL_BLOCK = 64
SEL_TOPK = 16
WINDOW = 512
Q_BLOCK = 128
D_FF = 2816
ROPE_THETA = 10000.0
EPS = 1e-6
NEG_INF = -1e30
TINY = 1e-30
SEL_FORCE = 1e6
SEL_NEG = -1e6

LANES = 128
KEY_TILE = 512
VMEM_LIMIT = 48 * 1024 * 1024


def _cparams(sem):
    return pltpu.CompilerParams(dimension_semantics=sem, vmem_limit_bytes=VMEM_LIMIT)


def _nt_dot(a, b):
    return lax.dot_general(a, b, (((1,), (1,)), ((), ())), preferred_element_type=F32)


def _tn_dot(a, b):
    return lax.dot_general(a, b, (((0,), (0,)), ((), ())), preferred_element_type=F32)


def _gelu(x):
    return 0.5 * x * (1.0 + jnp.tanh(math.sqrt(2.0 / math.pi) * (x + 0.044715 * (x * x * x))))


def _rms_rows(x, g):
    r = lax.rsqrt(jnp.mean(x * x, axis=-1, keepdims=True) + EPS)
    return x * r * g


def _head_ms(x, bd):
    x2 = x * x
    hi = x2.astype(BF16)
    lo = (x2 - hi.astype(F32)).astype(BF16)
    s = jnp.dot(hi, bd, preferred_element_type=F32) + jnp.dot(lo, bd, preferred_element_type=F32)
    return s * (1.0 / HEAD_DIM)


def _rope64(x, cos, sin):
    lane = lax.broadcasted_iota(jnp.int32, x.shape, 1)
    sw = jnp.where((lane & 63) < 32, pltpu.roll(x, 96, 1), pltpu.roll(x, 32, 1))
    return x * cos + sw * sin


def _pack_pair(k2, v2, dtype):
    lane = lax.broadcasted_iota(jnp.int32, k2.shape, 1)
    lo = lane < HEAD_DIM
    even = jnp.where(lo, k2, pltpu.roll(v2, HEAD_DIM, 1)).astype(dtype)
    odd = jnp.where(lo, pltpu.roll(k2, HEAD_DIM, 1), v2).astype(dtype)
    return even, odd


def _norm_matmul_kernel(x_ref, g_ref, w_ref, o_ref, h_ref):
    @pl.when(pl.program_id(1) == 0)
    def _():
        h_ref[...] = _rms_rows(x_ref[...], g_ref[...]).astype(BF16)

    o_ref[...] = jnp.dot(h_ref[...], w_ref[...], preferred_element_type=F32).astype(o_ref.dtype)


def norm_matmul(x, g, w, tm, tn):
    M, D = x.shape
    N = w.shape[1]
    return pl.pallas_call(
        _norm_matmul_kernel,
        out_shape=jax.ShapeDtypeStruct((M, N), F32),
        grid=(M // tm, N // tn),
        in_specs=[pl.BlockSpec((tm, D), lambda i, j: (i, 0)),
                  pl.BlockSpec((1, D), lambda i, j: (0, 0)),
                  pl.BlockSpec((D, tn), lambda i, j: (0, j))],
        out_specs=pl.BlockSpec((tm, tn), lambda i, j: (i, j)),
        scratch_shapes=[pltpu.VMEM((tm, D), BF16)],
        compiler_params=_cparams(("parallel", "arbitrary")),
        name="norm_matmul",
    )(x, g.reshape(1, D), w)


def _matmul_res_kernel(a_ref, w_ref, r_ref, o_ref):
    o_ref[...] = r_ref[...] + jnp.dot(a_ref[...].astype(BF16), w_ref[...],
                                      preferred_element_type=F32)


def matmul_res(a, w, res, tm):
    M, K = a.shape
    N = w.shape[1]
    return pl.pallas_call(
        _matmul_res_kernel,
        out_shape=jax.ShapeDtypeStruct((M, N), F32),
        grid=(M // tm,),
        in_specs=[pl.BlockSpec((tm, K), lambda i: (i, 0)),
                  pl.BlockSpec((K, N), lambda i: (0, 0)),
                  pl.BlockSpec((tm, N), lambda i: (i, 0))],
        out_specs=pl.BlockSpec((tm, N), lambda i: (i, 0)),
        compiler_params=_cparams(("parallel",)),
        name="matmul_res",
    )(a, w, res)


def _retention_kernel(lg_ref, gl_ref, q_ref, k_ref, v_ref, g_ref, cos_ref, sin_ref, s0_ref,
                      o_ref, sout_ref, S_ref, *, L, n_chunk):
    h = pl.program_id(1)
    t = pl.program_id(2)
    lg = lg_ref[h]
    gl = gl_ref[h]

    @pl.when(t == 0)
    def _():
        S_ref[...] = s0_ref[0, 0]

    ii = lax.broadcasted_iota(jnp.int32, (L, L), 0)
    jj = lax.broadcasted_iota(jnp.int32, (L, L), 1)
    diff = (ii - jj).astype(F32)
    decay = jnp.where(diff >= 0, jnp.exp(jnp.maximum(diff, 0.0) * lg), 0.0)
    idx = lax.broadcasted_iota(jnp.int32, (L, 1), 0).astype(F32)
    q_dec = jnp.exp((idx + 1.0) * lg)
    k_dec = jnp.exp((L - 1.0 - idx) * lg)
    half = RET_DK // 2

    for c in range(n_chunk):
        rows = pl.ds(c * L, L)
        cos = cos_ref[rows, :]
        sin = sin_ref[rows, :]

        def rope(x):
            x1, x2 = x[:, :half], x[:, half:]
            return jnp.concatenate([x1 * cos - x2 * sin, x2 * cos + x1 * sin], axis=1)

        qr = rope(q_ref[0, rows, :])
        kr = rope(k_ref[0, rows, :]) * (RET_DK ** -0.5)
        qb = qr.astype(BF16)
        vb = v_ref[0, rows, :].astype(BF16)
        sc = _nt_dot(qb, kr.astype(BF16)) * decay
        S = S_ref[...]
        o = jnp.dot(sc.astype(BF16), vb, preferred_element_type=F32)
        o = o + jnp.dot(qb, S.astype(BF16), preferred_element_type=F32) * q_dec
        S_ref[...] = S * gl + _tn_dot((kr * k_dec).astype(BF16), vb)
        on = o * lax.rsqrt(jnp.mean(o * o, axis=-1, keepdims=True) + EPS)
        g = g_ref[0, rows, :]
        o_ref[0, rows, :] = (on * (g * jax.nn.sigmoid(g))).astype(o_ref.dtype)

    @pl.when(t == pl.num_programs(2) - 1)
    def _():
        sout_ref[0, 0] = S_ref[...]


def retention(proj, s0, cos, sin, lg, gl, L, tb):
    B, T, _ = proj.shape
    n_chunk = tb // L
    odt = BF16 if tb % 16 == 0 else F32
    kern = functools.partial(_retention_kernel, L=L, n_chunk=n_chunk)
    grid_spec = pltpu.PrefetchScalarGridSpec(
        num_scalar_prefetch=2,
        grid=(B, RET_HEADS, T // tb),
        in_specs=[
            pl.BlockSpec((1, tb, RET_DK), lambda b, h, t, *_: (b, t, h)),
            pl.BlockSpec((1, tb, RET_DK), lambda b, h, t, *_: (b, t, RET_HEADS + h)),
            pl.BlockSpec((1, tb, RET_DV), lambda b, h, t, *_: (b, t, RET_HEADS + h)),
            pl.BlockSpec((1, tb, RET_DV), lambda b, h, t, *_: (b, t, 2 * RET_HEADS + h)),
            pl.BlockSpec((tb, RET_DK // 2), lambda b, h, t, *_: (t, 0)),
            pl.BlockSpec((tb, RET_DK // 2), lambda b, h, t, *_: (t, 0)),
            pl.BlockSpec((1, 1, RET_DK, RET_DV), lambda b, h, t, *_: (b, h, 0, 0)),
        ],
        out_specs=[
            pl.BlockSpec((1, tb, RET_DV), lambda b, h, t, *_: (b, t, h)),
            pl.BlockSpec((1, 1, RET_DK, RET_DV), lambda b, h, t, *_: (b, h, 0, 0)),
        ],
        scratch_shapes=[pltpu.VMEM((RET_DK, RET_DV), F32)],
    )
    return pl.pallas_call(
        kern,
        out_shape=[jax.ShapeDtypeStruct((B, T, RET_HEADS * RET_DV), odt),
                   jax.ShapeDtypeStruct((B, RET_HEADS, RET_DK, RET_DV), F32)],
        grid_spec=grid_spec,
        compiler_params=_cparams(("parallel", "parallel", "arbitrary")),
        name="retention",
    )(lg, gl, proj, proj, proj, proj, cos, sin, s0)


def _ffn_mid_kernel(u_ref, gt_ref, halo_ref, cw_ref, cb_ref, o_ref):
    u = u_ref[0]
    hl = halo_ref[0, 0]
    row = lax.broadcasted_iota(jnp.int32, u.shape, 0)
    u1 = jnp.where(row == 0, hl[1:2], pltpu.roll(u, 1, 0))
    u2 = jnp.where(row == 0, hl[0:1], jnp.where(row == 1, hl[1:2], pltpu.roll(u, 2, 0)))
    c = cb_ref[...] + cw_ref[0:1] * u2
    c = c + cw_ref[1:2] * u1
    c = c + cw_ref[2:3] * u
    o_ref[0] = (_gelu(c) * gt_ref[0]).astype(o_ref.dtype)


def ffn_mid(proj, buf, conv_w, conv_b, tt):
    B, T, _ = proj.shape
    nt = T // tt
    if nt > 1:
        tails = proj[:, :, :D_FF].reshape(B, nt, tt, D_FF)[:, :-1, tt - 2:, :]
        halo = jnp.concatenate([buf[:, None], tails], axis=1)
    else:
        halo = buf[:, None]
    odt = BF16 if tt % 16 == 0 else F32
    return pl.pallas_call(
        _ffn_mid_kernel,
        out_shape=jax.ShapeDtypeStruct((B, T, D_FF), odt),
        grid=(B, nt),
        in_specs=[pl.BlockSpec((1, tt, D_FF), lambda b, t: (b, t, 0)),
                  pl.BlockSpec((1, tt, D_FF), lambda b, t: (b, t, 1)),
                  pl.BlockSpec((1, 1, 2, D_FF), lambda b, t: (b, t, 0, 0)),
                  pl.BlockSpec((3, D_FF), lambda b, t: (0, 0)),
                  pl.BlockSpec((1, D_FF), lambda b, t: (0, 0))],
        out_specs=pl.BlockSpec((1, tt, D_FF), lambda b, t: (b, t, 0)),
        compiler_params=_cparams(("parallel", "parallel")),
        name="ffn_mid",
    )(proj, proj, halo, conv_w, conv_b.reshape(1, D_FF))


def _kv_kernel(x_ref, g_ref, w_ref, gain_ref, cos_ref, sin_ref, bd_ref,
               kv_ref, selp_ref, winp_ref):
    h = _rms_rows(x_ref[0], g_ref[...]).astype(BF16)
    y = jnp.dot(h, w_ref[...], preferred_element_type=F32)
    cos = cos_ref[...]
    sin = sin_ref[...]
    bd = bd_ref[...]
    nk = N_KV * HEAD_DIM
    kv_ref[0, :, 0:2 * nk] = y[:, 0:2 * nk]
    for br, pack_ref in enumerate((selp_ref, winp_ref)):
        base = 2 * nk * (br + 1)
        kv_ref[0, :, base + nk:base + 2 * nk] = y[:, base + nk:base + 2 * nk]
        for p in range(2):
            kx = y[:, base + LANES * p:base + LANES * (p + 1)]
            kn = kx * lax.rsqrt(_head_ms(kx, bd) + EPS) * gain_ref[br + 1:br + 2, :]
            kr = _rope64(kn, cos, sin)
            kv_ref[0, :, base + LANES * p:base + LANES * (p + 1)] = kr
            vx = y[:, base + nk + LANES * p:base + nk + LANES * (p + 1)]
            even, odd = _pack_pair(kr, vx, pack_ref.dtype)
            pack_ref[0, 2 * p] = even
            pack_ref[0, 2 * p + 1] = odd


def kv_project(x, g, w, gains, cos, sin, bd, tm):
    B, T, D = x.shape
    N = w.shape[1]
    pdt = BF16 if tm % 16 == 0 else F32
    return pl.pallas_call(
        _kv_kernel,
        out_shape=[jax.ShapeDtypeStruct((B, T, N), F32),
                   jax.ShapeDtypeStruct((B, N_KV, T, LANES), pdt),
                   jax.ShapeDtypeStruct((B, N_KV, T, LANES), pdt)],
        grid=(B, T // tm),
        in_specs=[pl.BlockSpec((1, tm, D), lambda b, t: (b, t, 0)),
                  pl.BlockSpec((1, D), lambda b, t: (0, 0)),
                  pl.BlockSpec((D, N), lambda b, t: (0, 0)),
                  pl.BlockSpec((3, LANES), lambda b, t: (0, 0)),
                  pl.BlockSpec((tm, LANES), lambda b, t: (t, 0)),
                  pl.BlockSpec((tm, LANES), lambda b, t: (t, 0)),
                  pl.BlockSpec((LANES, LANES), lambda b, t: (0, 0))],
        out_specs=[pl.BlockSpec((1, tm, N), lambda b, t: (b, t, 0)),
                   pl.BlockSpec((1, N_KV, tm, LANES), lambda b, t: (b, 0, t, 0)),
                   pl.BlockSpec((1, N_KV, tm, LANES), lambda b, t: (b, 0, t, 0))],
        compiler_params=_cparams(("parallel", "parallel")),
        name="kv_project",
    )(x, g.reshape(1, D), w, gains, cos, sin, bd)


def _qg_kernel(x_ref, g_ref, w_ref, gain_ref, cos_ref, sin_ref, bd_ref,
               qc_ref, qr_ref, gate_ref, *, qb, n_qb):
    h = _rms_rows(x_ref[0], g_ref[...]).astype(BF16)
    y = jnp.dot(h, w_ref[...], preferred_element_type=F32)
    cos = cos_ref[...]
    sin = sin_ref[...]
    bd = bd_ref[...]
    nq = N_HEADS * HEAD_DIM
    gate_ref[0] = jax.nn.sigmoid(y[:, nq:nq + LANES])
    lane = lax.broadcasted_iota(jnp.int32, (y.shape[0], LANES), 1)
    lo = lane < HEAD_DIM
    group = N_HEADS // N_KV
    for p in range(N_HEADS // 2):
        qx = y[:, LANES * p:LANES * (p + 1)]
        qn = qx * lax.rsqrt(_head_ms(qx, bd) + EPS) * gain_ref[...]
        qr = _rope64(qn, cos, sin)
        for src, dst in ((qn, qc_ref), (qr, qr_ref)):
            for par in range(2):
                hd = 2 * p + par
                kvh, gi = hd // group, hd % group
                v = src if par == 0 else pltpu.roll(src, HEAD_DIM, 1)
                v = jnp.where(lo, v, 0.0).astype(dst.dtype)
                for j in range(n_qb):
                    dst[0, kvh, j, gi * qb:(gi + 1) * qb, :] = v[j * qb:(j + 1) * qb, :]


def qg_project(x, g, w, gain, cos, sin, bd, tm, qb):
    B, T, D = x.shape
    N = w.shape[1]
    n_qb = tm // qb
    group = N_HEADS // N_KV
    qdt = BF16 if qb % 16 == 0 else F32
    kern = functools.partial(_qg_kernel, qb=qb, n_qb=n_qb)
    qshape = jax.ShapeDtypeStruct((B, N_KV, T // qb, group * qb, LANES), qdt)
    qspec = pl.BlockSpec((1, N_KV, n_qb, group * qb, LANES), lambda b, t: (b, 0, t, 0, 0))
    return pl.pallas_call(
        kern,
        out_shape=[qshape, qshape, jax.ShapeDtypeStruct((B, T, LANES), F32)],
        grid=(B, T // tm),
        in_specs=[pl.BlockSpec((1, tm, D), lambda b, t: (b, t, 0)),
                  pl.BlockSpec((1, D), lambda b, t: (0, 0)),
                  pl.BlockSpec((D, N), lambda b, t: (0, 0)),
                  pl.BlockSpec((1, LANES), lambda b, t: (0, 0)),
                  pl.BlockSpec((tm, LANES), lambda b, t: (t, 0)),
                  pl.BlockSpec((tm, LANES), lambda b, t: (t, 0)),
                  pl.BlockSpec((LANES, LANES), lambda b, t: (0, 0))],
        out_specs=[qspec, qspec, pl.BlockSpec((1, tm, LANES), lambda b, t: (b, t, 0))],
        compiler_params=_cparams(("parallel", "parallel")),
        name="qg_project",
    )(x, g.reshape(1, D), w, gain, cos, sin, bd)


def _pos_bias_kernel(p_ref, w_ref, o_ref):
    o_ref[0] = jnp.dot(p_ref[0].astype(BF16), w_ref[0].astype(BF16), preferred_element_type=F32)


def pos_bias(cmp_pos, cmp_w1):
    K = CMP_LEN * HEAD_DIM
    p = jnp.broadcast_to(cmp_pos.reshape(2, 1, K), (2, 8, K))
    out = pl.pallas_call(
        _pos_bias_kernel,
        out_shape=jax.ShapeDtypeStruct((2, 8, CMP_HID), F32),
        grid=(2,),
        in_specs=[pl.BlockSpec((1, 8, K), lambda c: (c, 0, 0)),
                  pl.BlockSpec((1, K, CMP_HID), lambda c: (c, 0, 0))],
        out_specs=pl.BlockSpec((1, 8, CMP_HID), lambda c: (c, 0, 0)),
        name="pos_bias",
    )(p, cmp_w1)
    return out[:, 0, :]


def _compress_kernel(pt_ref, *refs, n_pg):
    pages = refs[:n_pg]
    w1_ref, pb_ref, w2_ref, gain_ref, out_ref, carry_ref, slab_ref = refs[n_pg:]
    g = pl.program_id(1)

    @pl.when(g == 0)
    def _():
        carry_ref[...] = jnp.zeros_like(carry_ref)

    n = n_pg * (PAGE // CMP_STRIDE)
    lane = lax.broadcasted_iota(jnp.int32, (n, LANES), 1)
    row = lax.broadcasted_iota(jnp.int32, (n, LANES), 0)
    lo = lane < HEAD_DIM
    for pair in range(N_KV // 2):
        res = [jnp.zeros((n, LANES), F32), jnp.zeros((n, LANES), F32)]
        for c in range(2):
            col = c * N_KV * HEAD_DIM + pair * LANES
            for i in range(n_pg):
                slab_ref[PAGE * i:PAGE * (i + 1), :] = pages[i][0, :, col:col + LANES]
            for par in range(2):
                keep = lo if par == 0 else jnp.logical_not(lo)
                acc = jnp.zeros((n, 2 * CMP_HID), F32)
                for s in range(CMP_STRIDE):
                    xs = slab_ref[pl.ds(s, n, stride=CMP_STRIDE), :]
                    xm = jnp.where(keep, xs, 0.0).astype(BF16)
                    acc = acc + jnp.dot(xm, w1_ref[c, s], preferred_element_type=F32)
                p0 = acc[:, :CMP_HID]
                p1 = acc[:, CMP_HID:]
                ci = 2 * (2 * pair + par) + c
                prev = jnp.where(row == 0, carry_ref[ci, 7:8, :], pltpu.roll(p0, 1, 0))
                carry_ref[ci] = p0[n - 8:n, :]
                hid = _gelu(prev + p1 + pb_ref[c:c + 1, :]).astype(BF16)
                res[par] = res[par] + jnp.dot(hid, w2_ref[c], preferred_element_type=F32)
        for par in range(2):
            r = res[par]
            ms = jnp.sum(jnp.where(lo, r * r, 0.0), axis=-1, keepdims=True) * (1.0 / HEAD_DIM)
            kn = r * lax.rsqrt(ms + EPS) * gain_ref[...]
            out_ref[0, 2 * pair + par] = jnp.where(lo, kn, r).astype(out_ref.dtype)


def _page_index(b, g, pt_ref, *, i, n_pg):
    return (pt_ref[b, g * n_pg + i], 0, 0)


def compress(pages_arr, table, w1dup, pb, w2p, gain):
    B, n_pages = table.shape
    n_pg = 16 if n_pages % 16 == 0 else n_pages
    n = n_pg * (PAGE // CMP_STRIDE)
    n_sub = n_pages * (PAGE // CMP_STRIDE)
    width = 2 * N_KV * HEAD_DIM
    in_specs = [pl.BlockSpec((1, PAGE, width), functools.partial(_page_index, i=i, n_pg=n_pg))
                for i in range(n_pg)]
    in_specs += [pl.BlockSpec(w1dup.shape, lambda b, g, pt: (0, 0, 0, 0)),
                 pl.BlockSpec(pb.shape, lambda b, g, pt: (0, 0)),
                 pl.BlockSpec(w2p.shape, lambda b, g, pt: (0, 0, 0)),
                 pl.BlockSpec((1, LANES), lambda b, g, pt: (0, 0))]
    grid_spec = pltpu.PrefetchScalarGridSpec(
        num_scalar_prefetch=1,
        grid=(B, n_pages // n_pg),
        in_specs=in_specs,
        out_specs=pl.BlockSpec((1, N_KV, n, LANES), lambda b, g, pt: (b, 0, g, 0)),
        scratch_shapes=[pltpu.VMEM((2 * N_KV, 8, CMP_HID), F32),
                        pltpu.VMEM((n_pg * PAGE, LANES), F32)],
    )
    return pl.pallas_call(
        functools.partial(_compress_kernel, n_pg=n_pg),
        out_shape=jax.ShapeDtypeStruct((B, N_KV, n_sub, LANES), BF16),
        grid_spec=grid_spec,
        compiler_params=_cparams(("parallel", "arbitrary")),
        name="compress",
    )(table, *([pages_arr] * n_pg), w1dup, pb, w2p, gain)


def _repack_kernel(pt_ref, p0, p1, p2, p3, new_ref, out_ref, *, n_full, t_new):
    t = pl.program_id(1)
    nk = N_KV * HEAD_DIM

    @pl.when(t < n_full)
    def _():
        for i, pg in enumerate((p0, p1, p2, p3)):
            x = pg[0]
            for p in range(2):
                even, odd = _pack_pair(x[:, LANES * p:LANES * (p + 1)],
                                       x[:, nk + LANES * p:nk + LANES * (p + 1)], out_ref.dtype)
                out_ref[0, 2 * p, PAGE * i:PAGE * (i + 1), :] = even
                out_ref[0, 2 * p + 1, PAGE * i:PAGE * (i + 1), :] = odd

    @pl.when(t == n_full)
    def _():
        pad = jnp.zeros((N_KV, KEY_TILE - t_new, LANES), F32)
        out_ref[0] = jnp.concatenate([new_ref[0].astype(F32), pad], axis=1).astype(out_ref.dtype)


def _repack_page_index(b, t, pt_ref, *, i, n_pages):
    return (pt_ref[b, jnp.minimum(4 * t + i, n_pages - 1)], 0, 0)


def repack(pages_arr, table, new_pack):
    B, n_pages = table.shape
    n_full = n_pages // 4
    t_new = new_pack.shape[2]
    width = 2 * N_KV * HEAD_DIM
    in_specs = [pl.BlockSpec((1, PAGE, width),
                             functools.partial(_repack_page_index, i=i, n_pages=n_pages))
                for i in range(4)]
    in_specs.append(pl.BlockSpec((1, N_KV, t_new, LANES), lambda b, t, pt: (b, 0, 0, 0)))
    grid_spec = pltpu.PrefetchScalarGridSpec(
        num_scalar_prefetch=1,
        grid=(B, n_full + 1),
        in_specs=in_specs,
        out_specs=pl.BlockSpec((1, N_KV, KEY_TILE, LANES), lambda b, t, pt: (b, 0, t, 0)),
    )
    return pl.pallas_call(
        functools.partial(_repack_kernel, n_full=n_full, t_new=t_new),
        out_shape=jax.ShapeDtypeStruct((B, N_KV, (n_full + 1) * KEY_TILE, LANES), BF16),
        grid_spec=grid_spec,
        compiler_params=_cparams(("parallel", "arbitrary")),
        name="repack",
    )(table, pages_arr, pages_arr, pages_arr, pages_arr, new_pack)


def _softmax_step(s, mask, kv, m_ref, l_ref, a_ref, k):
    s = jnp.where(mask, s, NEG_INF)
    m_old = m_ref[k]
    m_new = jnp.maximum(m_old, jnp.max(s, axis=-1, keepdims=True))
    alpha = jnp.exp(m_old - m_new)
    p = jnp.where(mask, jnp.exp(s - m_new), 0.0)
    l_ref[k] = alpha * l_ref[k] + jnp.sum(p, axis=-1, keepdims=True)
    a_ref[k] = alpha * a_ref[k] + jnp.dot(p.astype(BF16), kv, preferred_element_type=F32)
    m_ref[k] = m_new


def _nsa_kernel(qi_ref, kt_ref, wt_ref, wf_ref, last_ref,
                qc_ref, qr_ref, gate_ref, cmp_ref, sel_ref, win_ref, wmap_ref,
                o_ref,
                selm_ref, oc_ref, ms_ref, ls_ref, as_ref, mw_ref, lw_ref, aw_ref,
                *, qb, n_sel, n_selp, n_cmp, q0, w_off):
    step = pl.program_id(1)
    qi = qi_ref[step]
    kt = kt_ref[step]
    group = N_HEADS // N_KV
    R = group * qb
    rowq = lax.broadcasted_iota(jnp.int32, (R, 1), 0) & (qb - 1)
    qpos = q0 + qi * qb + rowq
    qpos_q = q0 + qi * qb + lax.broadcasted_iota(jnp.int32, (qb, 1), 0)

    @pl.when(kt == 0)
    def _first():
        for m_ref, l_ref, a_ref in ((ms_ref, ls_ref, as_ref), (mw_ref, lw_ref, aw_ref)):
            m_ref[...] = jnp.full(m_ref.shape, NEG_INF, F32)
            l_ref[...] = jnp.zeros(l_ref.shape, F32)
            a_ref[...] = jnp.zeros(a_ref.shape, F32)
        n_idx = lax.broadcasted_iota(jnp.int32, (1, n_cmp), 1)
        cvalid = (n_idx >= 1) & ((n_idx - 1) * CMP_STRIDE + CMP_LEN - 1 <= qpos)
        blk = lax.broadcasted_iota(jnp.int32, (qb, n_selp), 1)
        blk_f = blk.astype(F32)
        cur = qpos_q >> 6
        forced = (blk == 0) | (blk == cur) | (blk == cur - 1)
        reach = blk * SEL_BLOCK <= qpos_q
        real = blk < n_sel
        for k in range(N_KV):
            ckv = cmp_ref[0, k]
            s = _nt_dot(qc_ref[0, k, 0].astype(BF16), ckv)
            s = jnp.where(cvalid, s, NEG_INF)
            m = jnp.max(s, axis=-1, keepdims=True)
            e = jnp.where(cvalid, jnp.exp(s - m), 0.0)
            p = e / jnp.maximum(jnp.sum(e, axis=-1, keepdims=True), TINY)
            oc_ref[k] = jnp.dot(p.astype(BF16), ckv, preferred_element_type=F32)
            psum = p[0:qb]
            for gi in range(1, group):
                psum = psum + p[gi * qb:(gi + 1) * qb]
            hi = psum.astype(BF16)
            lo = (psum - hi.astype(F32)).astype(BF16)
            imp = _nt_dot(hi, wmap_ref[...]) + _nt_dot(lo, wmap_ref[...])
            v = jnp.where(forced, SEL_FORCE, jnp.where(reach, imp, SEL_NEG))
            v = jnp.where(real, v, -jnp.inf)

            def pick_one(_, carry):
                v, sel = carry
                m = jnp.max(v, axis=-1, keepdims=True)
                first = jnp.min(jnp.where(v == m, blk_f, float(n_selp)), axis=-1, keepdims=True)
                pick = blk_f == first
                sel = jnp.where(pick & (m > 0.5 * SEL_NEG), 1.0, sel)
                return jnp.where(pick, -jnp.inf, v), sel

            _, sel = lax.fori_loop(0, SEL_TOPK, pick_one, (v, jnp.zeros((qb, n_selp), F32)))
            selm_ref[k] = sel

    tok = kt * KEY_TILE + lax.broadcasted_iota(jnp.int32, (1, KEY_TILE), 1)
    blk_of_tok = tok >> 6
    expand = jnp.where(
        lax.broadcasted_iota(jnp.int32, (n_selp, KEY_TILE), 0) == blk_of_tok, 1.0, 0.0).astype(BF16)
    causal = tok <= qpos
    for k in range(N_KV):
        kv = sel_ref[0, k]
        s = _nt_dot(qr_ref[0, k, 0].astype(BF16), kv)
        sm = jnp.dot(selm_ref[k].astype(BF16), expand, preferred_element_type=F32)
        sm = jnp.concatenate([sm] * group, axis=0)
        _softmax_step(s, causal & (sm > 0.5), kv, ms_ref, ls_ref, as_ref, k)

    @pl.when(wf_ref[step] == 1)
    def _window():
        wpos = w_off + wt_ref[step] * KEY_TILE + lax.broadcasted_iota(jnp.int32, (1, KEY_TILE), 1)
        mask = (wpos <= qpos) & (wpos > qpos - WINDOW) & (wpos >= 0)
        for k in range(N_KV):
            kv = win_ref[0, k]
            s = _nt_dot(qr_ref[0, k, 0].astype(BF16), kv)
            _softmax_step(s, mask, kv, mw_ref, lw_ref, aw_ref, k)

    @pl.when(last_ref[step] == 1)
    def _finish():
        gate = gate_ref[0]
        for k in range(N_KV):
            for gi in range(group):
                hd = k * group + gi
                rs = slice(gi * qb, (gi + 1) * qb)
                o_s = as_ref[k, rs, :] / jnp.maximum(ls_ref[k, rs, :], TINY)
                o_w = aw_ref[k, rs, :] / jnp.maximum(lw_ref[k, rs, :], TINY)
                o = gate[:, 3 * hd:3 * hd + 1] * oc_ref[k, rs, :]
                o = o + gate[:, 3 * hd + 1:3 * hd + 2] * o_s
                o = o + gate[:, 3 * hd + 2:3 * hd + 3] * o_w
                o_ref[0, :, LANES * hd:LANES * (hd + 1)] = o.astype(o_ref.dtype)


def _nsa_tables(T, qb, q0, w_off):
    rows = []
    for qi in range(T // qb):
        q_lo = q0 + qi * qb
        q_hi = q_lo + qb - 1
        last_kt = q_hi // KEY_TILE
        w_lo = max(q_lo - WINDOW + 1, w_off)
        wt0 = (w_lo - w_off) // KEY_TILE
        wt1 = (q_hi - w_off) // KEY_TILE
        n_w = wt1 - wt0 + 1
        assert n_w <= last_kt + 1
        for kt in range(last_kt + 1):
            rows.append((qi, kt, wt0 + min(kt, n_w - 1), int(kt < n_w), int(kt == last_kt)))
    tab = np.asarray(rows, np.int32).T
    return [jnp.asarray(tab[i]) for i in range(5)]


def _overlap_map(n_cmp_rows, n_selp):
    m = np.arange(n_cmp_rows)[None, :]
    s = np.arange(n_selp)[:, None]
    c0 = (m - 1) * CMP_STRIDE
    ov = np.minimum(c0 + CMP_LEN, s * SEL_BLOCK + SEL_BLOCK) - np.maximum(c0, s * SEL_BLOCK)
    w = np.maximum(ov, 0).astype(np.float32) / CMP_LEN
    w[:, 0] = 0.0
    return jnp.asarray(w, BF16)


def nsa_attend(qc, qr, gates, cmp_p, sel_p, win_p, T, qb, q0, w_off, n_sel):
    B = qc.shape[0]
    group = N_HEADS // N_KV
    R = group * qb
    n_cmp = cmp_p.shape[2]
    n_selp = -(-n_sel // LANES) * LANES
    tabs = _nsa_tables(T, qb, q0, w_off)
    n_steps = int(tabs[0].shape[0])
    wmap = _overlap_map(n_cmp, n_selp)
    odt = BF16 if qb % 16 == 0 else F32
    kern = functools.partial(_nsa_kernel, qb=qb, n_sel=n_sel, n_selp=n_selp, n_cmp=n_cmp,
                             q0=q0, w_off=w_off)
    qspec = pl.BlockSpec((1, N_KV, 1, R, LANES), lambda b, s, qi, kt, wt, wf, la: (b, 0, qi[s], 0, 0))
    grid_spec = pltpu.PrefetchScalarGridSpec(
        num_scalar_prefetch=5,
        grid=(B, n_steps),
        in_specs=[
            qspec, qspec,
            pl.BlockSpec((1, qb, LANES), lambda b, s, qi, kt, wt, wf, la: (b, qi[s], 0)),
            pl.BlockSpec((1, N_KV, n_cmp, LANES), lambda b, s, qi, kt, wt, wf, la: (b, 0, 0, 0)),
            pl.BlockSpec((1, N_KV, KEY_TILE, LANES), lambda b, s, qi, kt, wt, wf, la: (b, 0, kt[s], 0)),
            pl.BlockSpec((1, N_KV, KEY_TILE, LANES), lambda b, s, qi, kt, wt, wf, la: (b, 0, wt[s], 0)),
            pl.BlockSpec((n_selp, n_cmp), lambda b, s, qi, kt, wt, wf, la: (0, 0)),
        ],
        out_specs=pl.BlockSpec((1, qb, N_HEADS * LANES), lambda b, s, qi, kt, wt, wf, la: (b, qi[s], 0)),
        scratch_shapes=[
            pltpu.VMEM((N_KV, qb, n_selp), F32),
            pltpu.VMEM((N_KV, R, LANES), F32),
            pltpu.VMEM((N_KV, R, 1), F32), pltpu.VMEM((N_KV, R, 1), F32), pltpu.VMEM((N_KV, R, LANES), F32),
            pltpu.VMEM((N_KV, R, 1), F32), pltpu.VMEM((N_KV, R, 1), F32), pltpu.VMEM((N_KV, R, LANES), F32),
        ],
    )
    return pl.pallas_call(
        kern,
        out_shape=jax.ShapeDtypeStruct((B, T, N_HEADS * LANES), odt),
        grid_spec=grid_spec,
        compiler_params=_cparams(("parallel", "arbitrary")),
        name="nsa_attend",
    )(*tabs, qc, qr, gates, cmp_p, sel_p, win_p, wmap)


def _qgt_kernel(x_ref, g_ref, w_ref, gain_ref, cos_ref, sin_ref, bd_ref,
                qc_ref, qr_ref, gate_ref, *, qb, n_qb):
    h = _rms_rows(x_ref[0], g_ref[...]).astype(BF16)
    y = jnp.dot(h, w_ref[...], preferred_element_type=F32)
    cos = cos_ref[...]
    sin = sin_ref[...]
    bd = bd_ref[...]
    nq = N_HEADS * HEAD_DIM
    group = N_HEADS // N_KV
    gate_t = jax.nn.sigmoid(y[:, nq:nq + LANES]).T
    for j in range(n_qb):
        gate_ref[0, j] = gate_t[:, j * qb:(j + 1) * qb]
    pad = jnp.zeros((N_KV, n_qb, HEAD_DIM, group * qb), qc_ref.dtype)
    qc_ref[0, :, :, HEAD_DIM:, :] = pad
    qr_ref[0, :, :, HEAD_DIM:, :] = pad
    for p in range(N_HEADS // 2):
        qx = y[:, LANES * p:LANES * (p + 1)]
        qn = qx * lax.rsqrt(_head_ms(qx, bd) + EPS) * gain_ref[...]
        qr = _rope64(qn, cos, sin)
        for src, dst in ((qn, qc_ref), (qr, qr_ref)):
            st = src.T.astype(dst.dtype)
            for par in range(2):
                kvh, gi = divmod(2 * p + par, group)
                for j in range(n_qb):
                    dst[0, kvh, j, 0:HEAD_DIM, gi * qb:(gi + 1) * qb] = (
                        st[par * HEAD_DIM:(par + 1) * HEAD_DIM, j * qb:(j + 1) * qb])


def qg_project_t(x, g, w, gain, cos, sin, bd, tm, qb):
    B, T, D = x.shape
    N = w.shape[1]
    n_qb = tm // qb
    group = N_HEADS // N_KV
    kern = functools.partial(_qgt_kernel, qb=qb, n_qb=n_qb)
    qshape = jax.ShapeDtypeStruct((B, N_KV, T // qb, LANES, group * qb), BF16)
    qspec = pl.BlockSpec((1, N_KV, n_qb, LANES, group * qb), lambda b, t: (b, 0, t, 0, 0))
    return pl.pallas_call(
        kern,
        out_shape=[qshape, qshape, jax.ShapeDtypeStruct((B, T // qb, LANES, qb), F32)],
        grid=(B, T // tm),
        in_specs=[pl.BlockSpec((1, tm, D), lambda b, t: (b, t, 0)),
                  pl.BlockSpec((1, D), lambda b, t: (0, 0)),
                  pl.BlockSpec((D, N), lambda b, t: (0, 0)),
                  pl.BlockSpec((1, LANES), lambda b, t: (0, 0)),
                  pl.BlockSpec((tm, LANES), lambda b, t: (t, 0)),
                  pl.BlockSpec((tm, LANES), lambda b, t: (t, 0)),
                  pl.BlockSpec((LANES, LANES), lambda b, t: (0, 0))],
        out_specs=[qspec, qspec, pl.BlockSpec((1, n_qb, LANES, qb), lambda b, t: (b, t, 0, 0))],
        compiler_params=_cparams(("parallel", "parallel")),
        name="qg_project_t",
    )(x, g.reshape(1, D), w, gain, cos, sin, bd)


def _nsa_t_kernel(qc_ref, qr_ref, gate_ref, cmp_ref, sel_ref, win_ref, wmap_ref, o_ref,
                  selneg_ref, m_ref, l_ref, acc_ref,
                  *, qb, n_sel, n_selp, n_cmp, q0, w_off, w_rows, l_win):
    qi = pl.program_id(1)
    group = N_HEADS // N_KV
    R = group * qb
    blocks_per_tile = KEY_TILE // SEL_BLOCK
    q_lo = q0 + qi * qb
    qpos_q = q_lo + lax.broadcasted_iota(jnp.int32, (1, qb), 1)
    n_kt = (q_lo + qb - 1) // KEY_TILE + 1
    w_start = pl.multiple_of(jnp.clip(q_lo - WINDOW - w_off, 0, l_win - w_rows), LANES)
    gate = gate_ref[0, 0]

    def lanes4(a):
        return jnp.concatenate([a] * group, axis=1)

    m_idx = lax.broadcasted_iota(jnp.int32, (n_cmp, qb), 0)
    cvalid = (m_idx >= 1) & ((m_idx - 1) * CMP_STRIDE + CMP_LEN - 1 <= qpos_q)
    cbias = lanes4(jnp.where(cvalid, 0.0, NEG_INF))
    any_c = lanes4(qpos_q >= CMP_LEN - 1)
    blk = lax.broadcasted_iota(jnp.int32, (n_selp, qb), 0)
    blk_f = blk.astype(F32)
    cur = qpos_q >> 6
    forced = (blk == 0) | (blk == cur) | (blk == cur - 1)
    reach = blk * SEL_BLOCK <= qpos_q
    real = blk < n_sel
    wpos = w_off + w_start + lax.broadcasted_iota(jnp.int32, (w_rows, qb), 0)
    wbias = lanes4(jnp.where((wpos <= qpos_q) & (wpos > qpos_q - WINDOW) & (wpos >= 0), 0.0, NEG_INF))
    row_t = lax.broadcasted_iota(jnp.int32, (KEY_TILE, qb), 0)

    for k in range(N_KV):
        ckv = cmp_ref[0, k]
        s = jnp.dot(ckv, qc_ref[0, k, 0], preferred_element_type=F32) + cbias
        e = jnp.exp(s - jnp.max(s, axis=0, keepdims=True))
        den = jnp.maximum(jnp.sum(e, axis=0, keepdims=True), TINY)
        p = e * jnp.where(any_c, 1.0 / den, 0.0)
        oc = _tn_dot(ckv, p.astype(BF16))
        psum = p[:, 0:qb]
        for gi in range(1, group):
            psum = psum + p[:, gi * qb:(gi + 1) * qb]
        hi = psum.astype(BF16)
        lo = (psum - hi.astype(F32)).astype(BF16)
        imp = (jnp.dot(wmap_ref[...], hi, preferred_element_type=F32)
               + jnp.dot(wmap_ref[...], lo, preferred_element_type=F32))
        v = jnp.where(forced, SEL_FORCE, jnp.where(reach, imp, SEL_NEG))
        v = jnp.where(real, v, -jnp.inf)

        def pick_one(_, carry):
            v, sel = carry
            m = jnp.max(v, axis=0, keepdims=True)
            first = jnp.min(jnp.where(v == m, blk_f, float(n_selp)), axis=0, keepdims=True)
            pick = blk_f == first
            sel = jnp.where(pick & (m > 0.5 * SEL_NEG), 0.0, sel)
            return jnp.where(pick, -jnp.inf, v), sel

        _, sel = lax.fori_loop(0, SEL_TOPK, pick_one, (v, jnp.full((n_selp, qb), NEG_INF, F32)))
        selneg_ref[...] = sel

        qr = qr_ref[0, k, 0]
        m_ref[...] = jnp.full(m_ref.shape, NEG_INF, F32)
        l_ref[...] = jnp.zeros(l_ref.shape, F32)
        acc_ref[...] = jnp.zeros(acc_ref.shape, F32)

        def tile(kt, carry):
            start = pl.multiple_of(kt * KEY_TILE, KEY_TILE)
            kv = sel_ref[0, k, pl.ds(start, KEY_TILE), :]
            s = jnp.dot(kv, qr, preferred_element_type=F32)
            pieces = [jnp.broadcast_to(selneg_ref[pl.ds(kt * blocks_per_tile + j, 1), :], (SEL_BLOCK, qb))
                      for j in range(blocks_per_tile)]
            bias = jnp.concatenate(pieces, axis=0) + jnp.where(start + row_t <= qpos_q, 0.0, NEG_INF)
            s = s + lanes4(bias)
            m_old = m_ref[...]
            m_new = jnp.maximum(m_old, jnp.max(s, axis=0, keepdims=True))
            alpha = jnp.exp(m_old - m_new)
            p = jnp.exp(s - m_new)
            l_ref[...] = alpha * l_ref[...] + jnp.sum(p, axis=0, keepdims=True)
            acc_ref[...] = alpha * acc_ref[...] + _tn_dot(kv, p.astype(BF16))
            m_ref[...] = m_new
            return carry

        lax.fori_loop(0, n_kt, tile, 0)
        o_s = acc_ref[...] * (1.0 / jnp.maximum(l_ref[...], TINY))

        wkv = win_ref[0, k, pl.ds(w_start, w_rows), :]
        s = jnp.dot(wkv, qr, preferred_element_type=F32) + wbias
        e = jnp.exp(s - jnp.max(s, axis=0, keepdims=True))
        den = jnp.maximum(jnp.sum(e, axis=0, keepdims=True), TINY)
        o_w = _tn_dot(wkv, e.astype(BF16)) * (1.0 / den)

        for gi in range(group):
            hd = k * group + gi
            sl = slice(gi * qb, (gi + 1) * qb)
            o = gate[3 * hd:3 * hd + 1, :] * oc[:, sl]
            o = o + gate[3 * hd + 1:3 * hd + 2, :] * o_s[:, sl]
            o = o + gate[3 * hd + 2:3 * hd + 3, :] * o_w[:, sl]
            o_ref[0, :, LANES * hd:LANES * (hd + 1)] = o.T.astype(o_ref.dtype)


def nsa_attend_t(qc, qr, gates, cmp_p, sel_p, win_p, T, qb, q0, w_off, n_sel):
    B = qc.shape[0]
    group = N_HEADS // N_KV
    R = group * qb
    n_cmp = cmp_p.shape[2]
    n_selp = -(-n_sel // LANES) * LANES
    l_sel = sel_p.shape[2]
    l_win = win_p.shape[2]
    w_rows = WINDOW + max(qb, LANES)
    assert l_win >= w_rows and l_sel >= ((q0 + T - 1) // KEY_TILE + 1) * KEY_TILE
    wmap = _overlap_map(n_cmp, n_selp)
    kern = functools.partial(_nsa_t_kernel, qb=qb, n_sel=n_sel, n_selp=n_selp, n_cmp=n_cmp,
                             q0=q0, w_off=w_off, w_rows=w_rows, l_win=l_win)
    qspec = pl.BlockSpec((1, N_KV, 1, LANES, R), lambda b, i: (b, 0, i, 0, 0))
    resident = dict(pipeline_mode=pl.Buffered(1))
    return pl.pallas_call(
        kern,
        out_shape=jax.ShapeDtypeStruct((B, T, N_HEADS * LANES), BF16),
        grid=(B, T // qb),
        in_specs=[
            qspec, qspec,
            pl.BlockSpec((1, 1, LANES, qb), lambda b, i: (b, i, 0, 0)),
            pl.BlockSpec((1, N_KV, n_cmp, LANES), lambda b, i: (b, 0, 0, 0)),
            pl.BlockSpec((1, N_KV, l_sel, LANES), lambda b, i: (b, 0, 0, 0), **resident),
            pl.BlockSpec((1, N_KV, l_win, LANES), lambda b, i: (b, 0, 0, 0), **resident),
            pl.BlockSpec((n_selp, n_cmp), lambda b, i: (0, 0)),
        ],
        out_specs=pl.BlockSpec((1, qb, N_HEADS * LANES), lambda b, i: (b, i, 0)),
        scratch_shapes=[
            pltpu.VMEM((n_selp, qb), F32),
            pltpu.VMEM((1, R), F32), pltpu.VMEM((1, R), F32), pltpu.VMEM((LANES, R), F32),
        ],
        compiler_params=_cparams(("parallel", "arbitrary")),
        name="nsa_attend_t",
    )(qc, qr, gates, cmp_p, sel_p, win_p, wmap)


def _rope_tables(pos, half):
    inv = jnp.exp(-math.log(ROPE_THETA) * jnp.arange(half, dtype=F32) / half)
    ang = pos.astype(F32)[:, None] * inv[None, :]
    return jnp.cos(ang), jnp.sin(ang)


def _prep_weights(ret_w_in, ret_w_out, ffn_w_in, ffn_w_out, kv_w, kv_knorm, cmp_w1, cmp_w2,
                  nsa_w_qg, nsa_qnorm, nsa_w_o):
    n_b = nsa_w_qg.shape[0]
    nq = N_HEADS * HEAD_DIM
    qg_pad = nq + LANES - nsa_w_qg.shape[2]
    w_qg = jnp.pad(nsa_w_qg, ((0, 0), (0, 0), (0, qg_pad))).astype(BF16)
    w_o = jnp.pad(nsa_w_o.reshape(n_b, N_HEADS, 1, HEAD_DIM, D_MODEL),
                  ((0, 0), (0, 0), (1, 0), (0, 0), (0, 0))).reshape(n_b, N_HEADS * LANES, D_MODEL)
    R = CMP_LEN // CMP_STRIDE
    w1 = cmp_w1.reshape(2, R, CMP_STRIDE, HEAD_DIM, CMP_HID).transpose(0, 2, 3, 1, 4)
    w1 = w1.reshape(2, CMP_STRIDE, HEAD_DIM, R * CMP_HID)
    w1dup = jnp.concatenate([w1, w1], axis=2).astype(BF16)
    z = jnp.zeros((CMP_HID, HEAD_DIM), F32)
    w2p = jnp.stack([jnp.concatenate([cmp_w2[0], z], axis=1),
                     jnp.concatenate([z, cmp_w2[1]], axis=1)]).astype(BF16)
    ones = jnp.ones((HEAD_DIM,), F32)
    return dict(
        ret_w_in=ret_w_in.astype(BF16), ret_w_out=ret_w_out.astype(BF16),
        ffn_w_in=ffn_w_in.astype(BF16), ffn_w_out=ffn_w_out.astype(BF16),
        kv_w=kv_w.astype(BF16), w_qg=w_qg, w_o=w_o.astype(BF16),
        kv_gain=jnp.tile(kv_knorm, (1, 2)),
        cmp_gain=jnp.concatenate([kv_knorm[0], ones]).reshape(1, LANES),
        q_gain=jnp.tile(nsa_qnorm, (1, 2)) * (HEAD_DIM ** -0.5),
        w1dup=w1dup, w2p=w2p,
        bd=jnp.asarray(np.kron(np.eye(2), np.ones((HEAD_DIM, HEAD_DIM))), BF16),
    )


def _trunk(x, past_len, ret_s0, conv0, ctx, W, P):
    B, T, D = x.shape
    M = B * T
    depth = P["norm_mix"].shape[0]
    n_a = P["ret_w_in"].shape[0]
    pos = past_len + jnp.arange(T)
    cos_r, sin_r = _rope_tables(pos, RET_DK // 2)
    c32, s32 = _rope_tables(pos, HEAD_DIM // 2)
    cos_n = jnp.tile(c32, (1, 4))
    sin_n = jnp.concatenate([-s32, s32, -s32, s32], axis=1)
    lg = jnp.log1p(-jnp.exp2(-5.0 - jnp.arange(RET_HEADS, dtype=F32)))
    L = RET_CHUNK if T % RET_CHUNK == 0 else T
    gl = jnp.exp(L * lg)
    tm = min(512, M)
    tb = min(512, T)
    tq = min(512, T)
    qb = Q_BLOCK if T % Q_BLOCK == 0 else T
    tt = min(256, T)

    x2 = x.reshape(M, D)
    ret_states, conv_states = [], []
    for layer in range(depth):
        if layer == n_a:
            kvp, selp, winp = kv_project(x2.reshape(B, T, D), P["kv_norm"], W["kv_w"], W["kv_gain"],
                                         cos_n, sin_n, W["bd"], tq)
            nk2 = 2 * N_KV * HEAD_DIM
            if ctx is None:
                table = jnp.arange(M // PAGE, dtype=jnp.int32).reshape(B, T // PAGE)
                cmp_p = compress(kvp.reshape(M // PAGE, PAGE, 3 * nk2), table,
                                 W["w1dup"], W["pb"], W["w2p"], W["cmp_gain"])
                sel_p, win_p = selp, winp
                w_off = 0
            else:
                cache_cmp, cache_sel, cache_win, table = ctx
                n_pool = cache_cmp.shape[0]
                cmp_p = compress(cache_cmp.reshape(n_pool, PAGE, nk2), table,
                                 W["w1dup"], W["pb"], W["w2p"], W["cmp_gain"])
                sel_p = repack(cache_sel.reshape(n_pool, PAGE, nk2), table, selp)
                wl = cache_win.shape[1]
                wtab = jnp.arange(B * (wl // PAGE), dtype=jnp.int32).reshape(B, wl // PAGE)
                win_p = repack(cache_win.reshape(B * (wl // PAGE), PAGE, nk2), wtab, winp)
                w_off = past_len - wl
            n_sel = -(-(past_len + T) // SEL_BLOCK)
        h_norm = P["norm_mix"][layer]
        if layer < n_a:
            proj = norm_matmul(x2, h_norm, W["ret_w_in"][layer], tm, 1024)
            og, s_new = retention(proj.reshape(B, T, -1), ret_s0[layer], cos_r, sin_r, lg, gl, L, tb)
            ret_states.append(s_new)
            x2 = matmul_res(og.reshape(M, -1), W["ret_w_out"][layer], x2, tm)
        else:
            j = layer - n_a
            project, attend = (qg_project_t, nsa_attend_t) if qb % LANES == 0 else (qg_project, nsa_attend)
            qc, qr, gates = project(x2.reshape(B, T, D), h_norm, W["w_qg"][j], W["q_gain"][j:j + 1],
                                    cos_n, sin_n, W["bd"], tq, qb)
            o = attend(qc, qr, gates, cmp_p, sel_p, win_p, T, qb, past_len, w_off, n_sel)
            x2 = matmul_res(o.reshape(M, -1), W["w_o"][j], x2, tm)
        proj = norm_matmul(x2, P["norm_ffn"][layer], W["ffn_w_in"][layer], tm, 512)
        proj3 = proj.reshape(B, T, 2 * D_FF)
        act = ffn_mid(proj3, conv0[layer], P["ffn_conv_w"][layer], P["ffn_conv_b"][layer], tt)
        conv_states.append(proj3[:, T - 2:, :D_FF])
        x2 = matmul_res(act.reshape(M, D_FF), W["ffn_w_out"][layer], x2, tm)

    nk = N_KV * HEAD_DIM
    new_cmp = kvp[:, :, 0:2 * nk].reshape(B, T, 2, N_KV, HEAD_DIM)
    new_sel = kvp[:, :, 2 * nk:4 * nk].reshape(B, T, 2, N_KV, HEAD_DIM)
    new_win = kvp[:, :, 4 * nk:6 * nk].reshape(B, T, 2, N_KV, HEAD_DIM)
    return (x2.reshape(B, T, D), jnp.stack(ret_states), jnp.stack(conv_states),
            new_cmp, new_sel, new_win)


def kernel(x_prompt, x_sample, cache_cmp_kv, cache_sel_kv, cache_win_kv, state_ret, state_conv,
           page_table, norm_mix, norm_ffn, ret_w_in, ret_w_out, ffn_w_in, ffn_conv_w, ffn_conv_b,
           ffn_w_out, kv_norm, kv_w, kv_knorm, cmp_pos, cmp_w1, cmp_w2, nsa_w_qg, nsa_qnorm, nsa_w_o):
    W = _prep_weights(ret_w_in, ret_w_out, ffn_w_in, ffn_w_out, kv_w, kv_knorm, cmp_w1, cmp_w2,
                      nsa_w_qg, nsa_qnorm, nsa_w_o)
    W["pb"] = pos_bias(cmp_pos, cmp_w1)
    P = dict(norm_mix=norm_mix, norm_ffn=norm_ffn, ret_w_in=ret_w_in, ffn_conv_w=ffn_conv_w,
             ffn_conv_b=ffn_conv_b, kv_norm=kv_norm)
    depth = norm_mix.shape[0]
    n_a = ret_w_in.shape[0]
    B, T, _ = x_prompt.shape
    zero_ret = jnp.zeros((n_a, B, RET_HEADS, RET_DK, RET_DV), F32)
    zero_conv = jnp.zeros((depth, B, 2, D_FF), F32)
    y_p, ret_p, conv_p, cmp_p, sel_p, win_p = _trunk(x_prompt, 0, zero_ret, zero_conv, None, W, P)
    win_p = win_p[:, T - min(WINDOW, T):]

    db, ts, _ = x_sample.shape
    past_len = page_table.shape[1] * PAGE
    ctx = (cache_cmp_kv, cache_sel_kv, cache_win_kv, page_table)
    y_s, ret_s, conv_s, cmp_s, sel_s, win_new = _trunk(x_sample, past_len, state_ret, state_conv,
                                                        ctx, W, P)
    all_win = jnp.concatenate([cache_win_kv, win_new], axis=1)
    win_s = all_win[:, all_win.shape[1] - min(WINDOW, past_len + ts):]
    return (y_p, y_s, ret_p, ret_s, conv_p, conv_s, cmp_p, cmp_s, sel_p, sel_s, win_p, win_s)
```

```python
import functools
import math

import jax
import jax.numpy as jnp
import numpy as np
from jax import lax
from jax.experimental import pallas as pl
from jax.experimental.pallas import tpu as pltpu

F32 = jnp.float32
BF16 = jnp.bfloat16

D_MODEL = 1024
PAGE = 128
RET_HEADS = 4
RET_DK = 256
RET_DV = 512
RET_CHUNK = 128
N_HEADS = 16
N_KV = 4
HEAD_DIM = 64
CMP_LEN = 32
CMP_STRIDE = 16
CMP_HID = 128
SEL_BLOCK = 64
SEL_TOPK = 16
WINDOW = 512
Q_BLOCK = 128
D_FF = 2816
ROPE_THETA = 10000.0
EPS = 1e-6
NEG_INF = -1e30
TINY = 1e-30
SEL_FORCE = 1e6
SEL_NEG = -1e6

LANES = 128
KEY_TILE = 512
HALO = 16
VMEM_LIMIT = 48 * 1024 * 1024


def _cparams(sem):
    return pltpu.CompilerParams(dimension_semantics=sem, vmem_limit_bytes=VMEM_LIMIT)


def _nt_dot(a, b):
    return lax.dot_general(a, b, (((1,), (1,)), ((), ())), preferred_element_type=F32)


def _tn_dot(a, b):
    return lax.dot_general(a, b, (((0,), (0,)), ((), ())), preferred_element_type=F32)


def _gelu(x):
    return 0.5 * x * (1.0 + jnp.tanh(math.sqrt(2.0 / math.pi) * (x + 0.044715 * (x * x * x))))


def _rms_rows(x, g):
    r = lax.rsqrt(jnp.mean(x * x, axis=-1, keepdims=True) + EPS)
    return x * r * g


def _head_ms(x, bd):
    x2 = x * x
    hi = x2.astype(BF16)
    lo = (x2 - hi.astype(F32)).astype(BF16)
    s = jnp.dot(hi, bd, preferred_element_type=F32) + jnp.dot(lo, bd, preferred_element_type=F32)
    return s * (1.0 / HEAD_DIM)


def _rope64(x, cos, sin):
    lane = lax.broadcasted_iota(jnp.int32, x.shape, 1)
    sw = jnp.where((lane & 63) < 32, pltpu.roll(x, 96, 1), pltpu.roll(x, 32, 1))
    return x * cos + sw * sin


def _pack_pair(k2, v2, dtype):
    lane = lax.broadcasted_iota(jnp.int32, k2.shape, 1)
    lo = lane < HEAD_DIM
    even = jnp.where(lo, k2, pltpu.roll(v2, HEAD_DIM, 1)).astype(dtype)
    odd = jnp.where(lo, pltpu.roll(k2, HEAD_DIM, 1), v2).astype(dtype)
    return even, odd


def _norm_matmul_kernel(x_ref, g_ref, w_ref, o_ref, h_ref):
    @pl.when(pl.program_id(1) == 0)
    def _():
        h_ref[...] = _rms_rows(x_ref[...], g_ref[...]).astype(BF16)

    o_ref[...] = jnp.dot(h_ref[...], w_ref[...], preferred_element_type=F32).astype(o_ref.dtype)


def norm_matmul(x, g, w, tm, tn):
    M, D = x.shape
    N = w.shape[1]
    return pl.pallas_call(
        _norm_matmul_kernel,
        out_shape=jax.ShapeDtypeStruct((M, N), F32),
        grid=(M // tm, N // tn),
        in_specs=[pl.BlockSpec((tm, D), lambda i, j: (i, 0)),
                  pl.BlockSpec((1, D), lambda i, j: (0, 0)),
                  pl.BlockSpec((D, tn), lambda i, j: (0, j))],
        out_specs=pl.BlockSpec((tm, tn), lambda i, j: (i, j)),
        scratch_shapes=[pltpu.VMEM((tm, D), BF16)],
        compiler_params=_cparams(("parallel", "arbitrary")),
        name="norm_matmul",
    )(x, g.reshape(1, D), w)


def _matmul_res_kernel(a_ref, w_ref, r_ref, o_ref):
    o_ref[...] = r_ref[...] + jnp.dot(a_ref[...].astype(BF16), w_ref[...],
                                      preferred_element_type=F32)


def matmul_res(a, w, res, tm):
    M, K = a.shape
    N = w.shape[1]
    return pl.pallas_call(
        _matmul_res_kernel,
        out_shape=jax.ShapeDtypeStruct((M, N), F32),
        grid=(M // tm,),
        in_specs=[pl.BlockSpec((tm, K), lambda i: (i, 0)),
                  pl.BlockSpec((K, N), lambda i: (0, 0)),
                  pl.BlockSpec((tm, N), lambda i: (i, 0))],
        out_specs=pl.BlockSpec((tm, N), lambda i: (i, 0)),
        compiler_params=_cparams(("parallel",)),
        name="matmul_res",
    )(a, w, res)


def _retention_kernel(lg_ref, gl_ref, q_ref, k_ref, v_ref, g_ref, cos_ref, sin_ref, s0_ref,
                      o_ref, sout_ref, S_ref, *, L, n_chunk):
    h = pl.program_id(1)
    t = pl.program_id(2)
    lg = lg_ref[h]
    gl = gl_ref[h]

    @pl.when(t == 0)
    def _():
        S_ref[...] = s0_ref[0, 0]

    ii = lax.broadcasted_iota(jnp.int32, (L, L), 0)
    jj = lax.broadcasted_iota(jnp.int32, (L, L), 1)
    diff = (ii - jj).astype(F32)
    decay = jnp.where(diff >= 0, jnp.exp(jnp.maximum(diff, 0.0) * lg), 0.0)
    idx = lax.broadcasted_iota(jnp.int32, (L, 1), 0).astype(F32)
    q_dec = jnp.exp((idx + 1.0) * lg)
    k_dec = jnp.exp((L - 1.0 - idx) * lg)
    half = RET_DK // 2

    for c in range(n_chunk):
        rows = pl.ds(c * L, L)
        cos = cos_ref[rows, :]
        sin = sin_ref[rows, :]

        def rope(x):
            x1, x2 = x[:, :half], x[:, half:]
            return jnp.concatenate([x1 * cos - x2 * sin, x2 * cos + x1 * sin], axis=1)

        qr = rope(q_ref[0, rows, :])
        kr = rope(k_ref[0, rows, :]) * (RET_DK ** -0.5)
        qb = qr.astype(BF16)
        vb = v_ref[0, rows, :].astype(BF16)
        sc = _nt_dot(qb, kr.astype(BF16)) * decay
        S = S_ref[...]
        o = jnp.dot(sc.astype(BF16), vb, preferred_element_type=F32)
        o = o + jnp.dot(qb, S.astype(BF16), preferred_element_type=F32) * q_dec
        S_ref[...] = S * gl + _tn_dot((kr * k_dec).astype(BF16), vb)
        on = o * lax.rsqrt(jnp.mean(o * o, axis=-1, keepdims=True) + EPS)
        g = g_ref[0, rows, :]
        o_ref[0, rows, :] = (on * (g * jax.nn.sigmoid(g))).astype(o_ref.dtype)

    @pl.when(t == pl.num_programs(2) - 1)
    def _():
        sout_ref[0, 0] = S_ref[...]


def retention(proj, s0, cos, sin, lg, gl, L, tb):
    B, T, _ = proj.shape
    n_chunk = tb // L
    odt = BF16 if tb % 16 == 0 else F32
    kern = functools.partial(_retention_kernel, L=L, n_chunk=n_chunk)
    grid_spec = pltpu.PrefetchScalarGridSpec(
        num_scalar_prefetch=2,
        grid=(B, RET_HEADS, T // tb),
        in_specs=[
            pl.BlockSpec((1, tb, RET_DK), lambda b, h, t, *_: (b, t, h)),
            pl.BlockSpec((1, tb, RET_DK), lambda b, h, t, *_: (b, t, RET_HEADS + h)),
            pl.BlockSpec((1, tb, RET_DV), lambda b, h, t, *_: (b, t, RET_HEADS + h)),
            pl.BlockSpec((1, tb, RET_DV), lambda b, h, t, *_: (b, t, 2 * RET_HEADS + h)),
            pl.BlockSpec((tb, RET_DK // 2), lambda b, h, t, *_: (t, 0)),
            pl.BlockSpec((tb, RET_DK // 2), lambda b, h, t, *_: (t, 0)),
            pl.BlockSpec((1, 1, RET_DK, RET_DV), lambda b, h, t, *_: (b, h, 0, 0)),
        ],
        out_specs=[
            pl.BlockSpec((1, tb, RET_DV), lambda b, h, t, *_: (b, t, h)),
            pl.BlockSpec((1, 1, RET_DK, RET_DV), lambda b, h, t, *_: (b, h, 0, 0)),
        ],
        scratch_shapes=[pltpu.VMEM((RET_DK, RET_DV), F32)],
    )
    return pl.pallas_call(
        kern,
        out_shape=[jax.ShapeDtypeStruct((B, T, RET_HEADS * RET_DV), odt),
                   jax.ShapeDtypeStruct((B, RET_HEADS, RET_DK, RET_DV), F32)],
        grid_spec=grid_spec,
        compiler_params=_cparams(("parallel", "parallel", "arbitrary")),
        name="retention",
    )(lg, gl, proj, proj, proj, proj, cos, sin, s0)


def _ffn_mid_kernel(u_ref, gt_ref, halo_ref, cw_ref, cb_ref, o_ref):
    u = u_ref[0]
    hl = halo_ref[0, 0]
    row = lax.broadcasted_iota(jnp.int32, u.shape, 0)
    u1 = jnp.where(row == 0, hl[1:2], pltpu.roll(u, 1, 0))
    u2 = jnp.where(row == 0, hl[0:1], jnp.where(row == 1, hl[1:2], pltpu.roll(u, 2, 0)))
    c = cb_ref[...] + cw_ref[0:1] * u2
    c = c + cw_ref[1:2] * u1
    c = c + cw_ref[2:3] * u
    o_ref[0] = (_gelu(c) * gt_ref[0]).astype(o_ref.dtype)


def ffn_mid(proj, buf, conv_w, conv_b, tt):
    B, T, _ = proj.shape
    nt = T // tt
    if nt > 1:
        tails = proj[:, :, :D_FF].reshape(B, nt, tt, D_FF)[:, :-1, tt - 2:, :]
        halo = jnp.concatenate([buf[:, None], tails], axis=1)
    else:
        halo = buf[:, None]
    odt = BF16 if tt % 16 == 0 else F32
    return pl.pallas_call(
        _ffn_mid_kernel,
        out_shape=jax.ShapeDtypeStruct((B, T, D_FF), odt),
        grid=(B, nt),
        in_specs=[pl.BlockSpec((1, tt, D_FF), lambda b, t: (b, t, 0)),
                  pl.BlockSpec((1, tt, D_FF), lambda b, t: (b, t, 1)),
                  pl.BlockSpec((1, 1, 2, D_FF), lambda b, t: (b, t, 0, 0)),
                  pl.BlockSpec((3, D_FF), lambda b, t: (0, 0)),
                  pl.BlockSpec((1, D_FF), lambda b, t: (0, 0))],
        out_specs=pl.BlockSpec((1, tt, D_FF), lambda b, t: (b, t, 0)),
        compiler_params=_cparams(("parallel", "parallel")),
        name="ffn_mid",
    )(proj, proj, halo, conv_w, conv_b.reshape(1, D_FF))


def _ffn_in_kernel(x_ref, xh_ref, g_ref, wu_ref, wg_ref, buf_ref, cw_ref, cb_ref,
                   act_ref, tail_ref, h_ref, hh_ref, *, tiles_per_seq):
    i = pl.program_id(0)

    @pl.when(pl.program_id(1) == 0)
    def _():
        h_ref[...] = _rms_rows(x_ref[...], g_ref[...]).astype(BF16)
        hh_ref[...] = _rms_rows(xh_ref[...], g_ref[...]).astype(BF16)

    h = h_ref[...]
    u = jnp.dot(h, wu_ref[...], preferred_element_type=F32)
    gt = jnp.dot(h, wg_ref[...], preferred_element_type=F32)
    uh = jnp.dot(hh_ref[...], wu_ref[...], preferred_element_type=F32)
    seq_start = (i % tiles_per_seq) == 0
    hl = jnp.where(seq_start, buf_ref[0], uh[HALO - 2:, :])
    row = lax.broadcasted_iota(jnp.int32, u.shape, 0)
    u1 = jnp.where(row == 0, hl[1:2], pltpu.roll(u, 1, 0))
    u2 = jnp.where(row == 0, hl[0:1], jnp.where(row == 1, hl[1:2], pltpu.roll(u, 2, 0)))
    c = cb_ref[...] + cw_ref[0:1] * u2
    c = c + cw_ref[1:2] * u1
    c = c + cw_ref[2:3] * u
    act_ref[...] = (_gelu(c) * gt).astype(act_ref.dtype)
    tail_ref[0] = u[u.shape[0] - 2:, :]


def ffn_in(x, g, w, buf, conv_w, conv_b, T, tm, tn):
    M, D = x.shape
    B = M // T
    n_col = D_FF // tn
    tiles_per_seq = T // tm
    kern = functools.partial(_ffn_in_kernel, tiles_per_seq=tiles_per_seq)
    act, tails = pl.pallas_call(
        kern,
        out_shape=[jax.ShapeDtypeStruct((M, D_FF), BF16),
                   jax.ShapeDtypeStruct((M // tm, 2, D_FF), F32)],
        grid=(M // tm, n_col),
        in_specs=[pl.BlockSpec((tm, D), lambda i, j: (i, 0)),
                  pl.BlockSpec((HALO, D), lambda i, j: (jnp.maximum(i * (tm // HALO) - 1, 0), 0)),
                  pl.BlockSpec((1, D), lambda i, j: (0, 0)),
                  pl.BlockSpec((D, tn), lambda i, j: (0, j)),
                  pl.BlockSpec((D, tn), lambda i, j: (0, n_col + j)),
                  pl.BlockSpec((1, 2, tn), lambda i, j: (i // tiles_per_seq, 0, j)),
                  pl.BlockSpec((3, tn), lambda i, j: (0, j)),
                  pl.BlockSpec((1, tn), lambda i, j: (0, j))],
        out_specs=[pl.BlockSpec((tm, tn), lambda i, j: (i, j)),
                   pl.BlockSpec((1, 2, tn), lambda i, j: (i, 0, j))],
        scratch_shapes=[pltpu.VMEM((tm, D), BF16), pltpu.VMEM((HALO, D), BF16)],
        compiler_params=_cparams(("parallel", "arbitrary")),
        name="ffn_in",
    )(x, x, g.reshape(1, D), w, w, buf, conv_w, conv_b.reshape(1, D_FF))
    return act, tails[tiles_per_seq - 1::tiles_per_seq]


def _kv_kernel(x_ref, g_ref, w_ref, gain_ref, cos_ref, sin_ref, bd_ref,
               kv_ref, selp_ref, winp_ref):
    h = _rms_rows(x_ref[0], g_ref[...]).astype(BF16)
    y = jnp.dot(h, w_ref[...], preferred_element_type=F32)
    cos = cos_ref[...]
    sin = sin_ref[...]
    bd = bd_ref[...]
    nk = N_KV * HEAD_DIM
    kv_ref[0, :, 0:2 * nk] = y[:, 0:2 * nk]
    for br, pack_ref in enumerate((selp_ref, winp_ref)):
        base = 2 * nk * (br + 1)
        kv_ref[0, :, base + nk:base + 2 * nk] = y[:, base + nk:base + 2 * nk]
        for p in range(2):
            kx = y[:, base + LANES * p:base + LANES * (p + 1)]
            kn = kx * lax.rsqrt(_head_ms(kx, bd) + EPS) * gain_ref[br + 1:br + 2, :]
            kr = _rope64(kn, cos, sin)
            kv_ref[0, :, base + LANES * p:base + LANES * (p + 1)] = kr
            vx = y[:, base + nk + LANES * p:base + nk + LANES * (p + 1)]
            even, odd = _pack_pair(kr, vx, pack_ref.dtype)
            pack_ref[0, 2 * p] = even
            pack_ref[0, 2 * p + 1] = odd


def kv_project(x, g, w, gains, cos, sin, bd, tm):
    B, T, D = x.shape
    N = w.shape[1]
    pdt = BF16 if tm % 16 == 0 else F32
    return pl.pallas_call(
        _kv_kernel,
        out_shape=[jax.ShapeDtypeStruct((B, T, N), F32),
                   jax.ShapeDtypeStruct((B, N_KV, T, LANES), pdt),
                   jax.ShapeDtypeStruct((B, N_KV, T, LANES), pdt)],
        grid=(B, T // tm),
        in_specs=[pl.BlockSpec((1, tm, D), lambda b, t: (b, t, 0)),
                  pl.BlockSpec((1, D), lambda b, t: (0, 0)),
                  pl.BlockSpec((D, N), lambda b, t: (0, 0)),
                  pl.BlockSpec((3, LANES), lambda b, t: (0, 0)),
                  pl.BlockSpec((tm, LANES), lambda b, t: (t, 0)),
                  pl.BlockSpec((tm, LANES), lambda b, t: (t, 0)),
                  pl.BlockSpec((LANES, LANES), lambda b, t: (0, 0))],
        out_specs=[pl.BlockSpec((1, tm, N), lambda b, t: (b, t, 0)),
                   pl.BlockSpec((1, N_KV, tm, LANES), lambda b, t: (b, 0, t, 0)),
                   pl.BlockSpec((1, N_KV, tm, LANES), lambda b, t: (b, 0, t, 0))],
        compiler_params=_cparams(("parallel", "parallel")),
        name="kv_project",
    )(x, g.reshape(1, D), w, gains, cos, sin, bd)


def _qg_kernel(x_ref, g_ref, w_ref, gain_ref, cos_ref, sin_ref, bd_ref,
               qc_ref, qr_ref, gate_ref, *, qb, n_qb):
    h = _rms_rows(x_ref[0], g_ref[...]).astype(BF16)
    y = jnp.dot(h, w_ref[...], preferred_element_type=F32)
    cos = cos_ref[...]
    sin = sin_ref[...]
    bd = bd_ref[...]
    nq = N_HEADS * HEAD_DIM
    gate_ref[0] = jax.nn.sigmoid(y[:, nq:nq + LANES])
    lane = lax.broadcasted_iota(jnp.int32, (y.shape[0], LANES), 1)
    lo = lane < HEAD_DIM
    group = N_HEADS // N_KV
    for p in range(N_HEADS // 2):
        qx = y[:, LANES * p:LANES * (p + 1)]
        qn = qx * lax.rsqrt(_head_ms(qx, bd) + EPS) * gain_ref[...]
        qr = _rope64(qn, cos, sin)
        for src, dst in ((qn, qc_ref), (qr, qr_ref)):
            for par in range(2):
                hd = 2 * p + par
                kvh, gi = hd // group, hd % group
                v = src if par == 0 else pltpu.roll(src, HEAD_DIM, 1)
                v = jnp.where(lo, v, 0.0).astype(dst.dtype)
                for j in range(n_qb):
                    dst[0, kvh, j, gi * qb:(gi + 1) * qb, :] = v[j * qb:(j + 1) * qb, :]


def qg_project(x, g, w, gain, cos, sin, bd, tm, qb):
    B, T, D = x.shape
    N = w.shape[1]
    n_qb = tm // qb
    group = N_HEADS // N_KV
    qdt = BF16 if qb % 16 == 0 else F32
    kern = functools.partial(_qg_kernel, qb=qb, n_qb=n_qb)
    qshape = jax.ShapeDtypeStruct((B, N_KV, T // qb, group * qb, LANES), qdt)
    qspec = pl.BlockSpec((1, N_KV, n_qb, group * qb, LANES), lambda b, t: (b, 0, t, 0, 0))
    return pl.pallas_call(
        kern,
        out_shape=[qshape, qshape, jax.ShapeDtypeStruct((B, T, LANES), F32)],
        grid=(B, T // tm),
        in_specs=[pl.BlockSpec((1, tm, D), lambda b, t: (b, t, 0)),
                  pl.BlockSpec((1, D), lambda b, t: (0, 0)),
                  pl.BlockSpec((D, N), lambda b, t: (0, 0)),
                  pl.BlockSpec((1, LANES), lambda b, t: (0, 0)),
                  pl.BlockSpec((tm, LANES), lambda b, t: (t, 0)),
                  pl.BlockSpec((tm, LANES), lambda b, t: (t, 0)),
                  pl.BlockSpec((LANES, LANES), lambda b, t: (0, 0))],
        out_specs=[qspec, qspec, pl.BlockSpec((1, tm, LANES), lambda b, t: (b, t, 0))],
        compiler_params=_cparams(("parallel", "parallel")),
        name="qg_project",
    )(x, g.reshape(1, D), w, gain, cos, sin, bd)


def _pos_bias_kernel(p_ref, w_ref, o_ref):
    o_ref[0] = jnp.dot(p_ref[0].astype(BF16), w_ref[0].astype(BF16), preferred_element_type=F32)


def pos_bias(cmp_pos, cmp_w1):
    K = CMP_LEN * HEAD_DIM
    p = jnp.broadcast_to(cmp_pos.reshape(2, 1, K), (2, 8, K))
    out = pl.pallas_call(
        _pos_bias_kernel,
        out_shape=jax.ShapeDtypeStruct((2, 8, CMP_HID), F32),
        grid=(2,),
        in_specs=[pl.BlockSpec((1, 8, K), lambda c: (c, 0, 0)),
                  pl.BlockSpec((1, K, CMP_HID), lambda c: (c, 0, 0))],
        out_specs=pl.BlockSpec((1, 8, CMP_HID), lambda c: (c, 0, 0)),
        name="pos_bias",
    )(p, cmp_w1)
    return out[:, 0, :]


def _compress_kernel(pt_ref, *refs, n_pg):
    pages = refs[:n_pg]
    w1_ref, pb_ref, w2_ref, gain_ref, out_ref, carry_ref, slab_ref = refs[n_pg:]
    g = pl.program_id(1)

    @pl.when(g == 0)
    def _():
        carry_ref[...] = jnp.zeros_like(carry_ref)

    n = n_pg * (PAGE // CMP_STRIDE)
    lane = lax.broadcasted_iota(jnp.int32, (n, LANES), 1)
    row = lax.broadcasted_iota(jnp.int32, (n, LANES), 0)
    lo = lane < HEAD_DIM
    for pair in range(N_KV // 2):
        res = [jnp.zeros((n, LANES), F32), jnp.zeros((n, LANES), F32)]
        for c in range(2):
            col = c * N_KV * HEAD_DIM + pair * LANES
            for i in range(n_pg):
                slab_ref[PAGE * i:PAGE * (i + 1), :] = pages[i][0, :, col:col + LANES]
            for par in range(2):
                keep = lo if par == 0 else jnp.logical_not(lo)
                acc = jnp.zeros((n, 2 * CMP_HID), F32)
                for s in range(CMP_STRIDE):
                    xs = slab_ref[pl.ds(s, n, stride=CMP_STRIDE), :]
                    xm = jnp.where(keep, xs, 0.0).astype(BF16)
                    acc = acc + jnp.dot(xm, w1_ref[c, s], preferred_element_type=F32)
                p0 = acc[:, :CMP_HID]
                p1 = acc[:, CMP_HID:]
                ci = 2 * (2 * pair + par) + c
                prev = jnp.where(row == 0, carry_ref[ci, 7:8, :], pltpu.roll(p0, 1, 0))
                carry_ref[ci] = p0[n - 8:n, :]
                hid = _gelu(prev + p1 + pb_ref[c:c + 1, :]).astype(BF16)
                res[par] = res[par] + jnp.dot(hid, w2_ref[c], preferred_element_type=F32)
        for par in range(2):
            r = res[par]
            ms = jnp.sum(jnp.where(lo, r * r, 0.0), axis=-1, keepdims=True) * (1.0 / HEAD_DIM)
            kn = r * lax.rsqrt(ms + EPS) * gain_ref[...]
            out_ref[0, 2 * pair + par] = jnp.where(lo, kn, r).astype(out_ref.dtype)


def _page_index(b, g, pt_ref, *, i, n_pg):
    return (pt_ref[b, g * n_pg + i], 0, 0)


def compress(pages_arr, table, w1dup, pb, w2p, gain):
    B, n_pages = table.shape
    n_pg = 16 if n_pages % 16 == 0 else n_pages
    n = n_pg * (PAGE // CMP_STRIDE)
    n_sub = n_pages * (PAGE // CMP_STRIDE)
    width = 2 * N_KV * HEAD_DIM
    in_specs = [pl.BlockSpec((1, PAGE, width), functools.partial(_page_index, i=i, n_pg=n_pg))
                for i in range(n_pg)]
    in_specs += [pl.BlockSpec(w1dup.shape, lambda b, g, pt: (0, 0, 0, 0)),
                 pl.BlockSpec(pb.shape, lambda b, g, pt: (0, 0)),
                 pl.BlockSpec(w2p.shape, lambda b, g, pt: (0, 0, 0)),
                 pl.BlockSpec((1, LANES), lambda b, g, pt: (0, 0))]
    grid_spec = pltpu.PrefetchScalarGridSpec(
        num_scalar_prefetch=1,
        grid=(B, n_pages // n_pg),
        in_specs=in_specs,
        out_specs=pl.BlockSpec((1, N_KV, n, LANES), lambda b, g, pt: (b, 0, g, 0)),
        scratch_shapes=[pltpu.VMEM((2 * N_KV, 8, CMP_HID), F32),
                        pltpu.VMEM((n_pg * PAGE, LANES), F32)],
    )
    return pl.pallas_call(
        functools.partial(_compress_kernel, n_pg=n_pg),
        out_shape=jax.ShapeDtypeStruct((B, N_KV, n_sub, LANES), BF16),
        grid_spec=grid_spec,
        compiler_params=_cparams(("parallel", "arbitrary")),
        name="compress",
    )(table, *([pages_arr] * n_pg), w1dup, pb, w2p, gain)


def _repack_kernel(pt_ref, p0, p1, p2, p3, new_ref, out_ref, *, n_full, t_new):
    t = pl.program_id(1)
    nk = N_KV * HEAD_DIM

    @pl.when(t < n_full)
    def _():
        for i, pg in enumerate((p0, p1, p2, p3)):
            x = pg[0]
            for p in range(2):
                even, odd = _pack_pair(x[:, LANES * p:LANES * (p + 1)],
                                       x[:, nk + LANES * p:nk + LANES * (p + 1)], out_ref.dtype)
                out_ref[0, 2 * p, PAGE * i:PAGE * (i + 1), :] = even
                out_ref[0, 2 * p + 1, PAGE * i:PAGE * (i + 1), :] = odd

    @pl.when(t == n_full)
    def _():
        pad = jnp.zeros((N_KV, KEY_TILE - t_new, LANES), F32)
        out_ref[0] = jnp.concatenate([new_ref[0].astype(F32), pad], axis=1).astype(out_ref.dtype)


def _repack_page_index(b, t, pt_ref, *, i, n_pages):
    return (pt_ref[b, jnp.minimum(4 * t + i, n_pages - 1)], 0, 0)


def repack(pages_arr, table, new_pack):
    B, n_pages = table.shape
    n_full = n_pages // 4
    t_new = new_pack.shape[2]
    width = 2 * N_KV * HEAD_DIM
    in_specs = [pl.BlockSpec((1, PAGE, width),
                             functools.partial(_repack_page_index, i=i, n_pages=n_pages))
                for i in range(4)]
    in_specs.append(pl.BlockSpec((1, N_KV, t_new, LANES), lambda b, t, pt: (b, 0, 0, 0)))
    grid_spec = pltpu.PrefetchScalarGridSpec(
        num_scalar_prefetch=1,
        grid=(B, n_full + 1),
        in_specs=in_specs,
        out_specs=pl.BlockSpec((1, N_KV, KEY_TILE, LANES), lambda b, t, pt: (b, 0, t, 0)),
    )
    return pl.pallas_call(
        functools.partial(_repack_kernel, n_full=n_full, t_new=t_new),
        out_shape=jax.ShapeDtypeStruct((B, N_KV, (n_full + 1) * KEY_TILE, LANES), BF16),
        grid_spec=grid_spec,
        compiler_params=_cparams(("parallel", "arbitrary")),
        name="repack",
    )(table, pages_arr, pages_arr, pages_arr, pages_arr, new_pack)


def _softmax_step(s, mask, kv, m_ref, l_ref, a_ref, k):
    s = jnp.where(mask, s, NEG_INF)
    m_old = m_ref[k]
    m_new = jnp.maximum(m_old, jnp.max(s, axis=-1, keepdims=True))
    alpha = jnp.exp(m_old - m_new)
    p = jnp.where(mask, jnp.exp(s - m_new), 0.0)
    l_ref[k] = alpha * l_ref[k] + jnp.sum(p, axis=-1, keepdims=True)
    a_ref[k] = alpha * a_ref[k] + jnp.dot(p.astype(BF16), kv, preferred_element_type=F32)
    m_ref[k] = m_new


def _nsa_kernel(qi_ref, kt_ref, wt_ref, wf_ref, last_ref,
                qc_ref, qr_ref, gate_ref, cmp_ref, sel_ref, win_ref, wmap_ref,
                o_ref,
                selm_ref, oc_ref, ms_ref, ls_ref, as_ref, mw_ref, lw_ref, aw_ref,
                *, qb, n_sel, n_selp, n_cmp, q0, w_off):
    step = pl.program_id(1)
    qi = qi_ref[step]
    kt = kt_ref[step]
    group = N_HEADS // N_KV
    R = group * qb
    rowq = lax.broadcasted_iota(jnp.int32, (R, 1), 0) & (qb - 1)
    qpos = q0 + qi * qb + rowq
    qpos_q = q0 + qi * qb + lax.broadcasted_iota(jnp.int32, (qb, 1), 0)

    @pl.when(kt == 0)
    def _first():
        for m_ref, l_ref, a_ref in ((ms_ref, ls_ref, as_ref), (mw_ref, lw_ref, aw_ref)):
            m_ref[...] = jnp.full(m_ref.shape, NEG_INF, F32)
            l_ref[...] = jnp.zeros(l_ref.shape, F32)
            a_ref[...] = jnp.zeros(a_ref.shape, F32)
        n_idx = lax.broadcasted_iota(jnp.int32, (1, n_cmp), 1)
        cvalid = (n_idx >= 1) & ((n_idx - 1) * CMP_STRIDE + CMP_LEN - 1 <= qpos)
        blk = lax.broadcasted_iota(jnp.int32, (qb, n_selp), 1)
        blk_f = blk.astype(F32)
        cur = qpos_q >> 6
        forced = (blk == 0) | (blk == cur) | (blk == cur - 1)
        reach = blk * SEL_BLOCK <= qpos_q
        real = blk < n_sel
        for k in range(N_KV):
            ckv = cmp_ref[0, k]
            s = _nt_dot(qc_ref[0, k, 0].astype(BF16), ckv)
            s = jnp.where(cvalid, s, NEG_INF)
            m = jnp.max(s, axis=-1, keepdims=True)
            e = jnp.where(cvalid, jnp.exp(s - m), 0.0)
            p = e / jnp.maximum(jnp.sum(e, axis=-1, keepdims=True), TINY)
            oc_ref[k] = jnp.dot(p.astype(BF16), ckv, preferred_element_type=F32)
            psum = p[0:qb]
            for gi in range(1, group):
                psum = psum + p[gi * qb:(gi + 1) * qb]
            hi = psum.astype(BF16)
            lo = (psum - hi.astype(F32)).astype(BF16)
            imp = _nt_dot(hi, wmap_ref[...]) + _nt_dot(lo, wmap_ref[...])
            v = jnp.where(forced, SEL_FORCE, jnp.where(reach, imp, SEL_NEG))
            v = jnp.where(real, v, -jnp.inf)

            def pick_one(_, carry):
                v, sel = carry
                m = jnp.max(v, axis=-1, keepdims=True)
                first = jnp.min(jnp.where(v == m, blk_f, float(n_selp)), axis=-1, keepdims=True)
                pick = blk_f == first
                sel = jnp.where(pick & (m > 0.5 * SEL_NEG), 1.0, sel)
                return jnp.where(pick, -jnp.inf, v), sel

            _, sel = lax.fori_loop(0, SEL_TOPK, pick_one, (v, jnp.zeros((qb, n_selp), F32)))
            selm_ref[k] = sel

    tok = kt * KEY_TILE + lax.broadcasted_iota(jnp.int32, (1, KEY_TILE), 1)
    blk_of_tok = tok >> 6
    expand = jnp.where(
        lax.broadcasted_iota(jnp.int32, (n_selp, KEY_TILE), 0) == blk_of_tok, 1.0, 0.0).astype(BF16)
    causal = tok <= qpos
    for k in range(N_KV):
        kv = sel_ref[0, k]
        s = _nt_dot(qr_ref[0, k, 0].astype(BF16), kv)
        sm = jnp.dot(selm_ref[k].astype(BF16), expand, preferred_element_type=F32)
        sm = jnp.concatenate([sm] * group, axis=0)
        _softmax_step(s, causal & (sm > 0.5), kv, ms_ref, ls_ref, as_ref, k)

    @pl.when(wf_ref[step] == 1)
    def _window():
        wpos = w_off + wt_ref[step] * KEY_TILE + lax.broadcasted_iota(jnp.int32, (1, KEY_TILE), 1)
        mask = (wpos <= qpos) & (wpos > qpos - WINDOW) & (wpos >= 0)
        for k in range(N_KV):
            kv = win_ref[0, k]
            s = _nt_dot(qr_ref[0, k, 0].astype(BF16), kv)
            _softmax_step(s, mask, kv, mw_ref, lw_ref, aw_ref, k)

    @pl.when(last_ref[step] == 1)
    def _finish():
        gate = gate_ref[0]
        for k in range(N_KV):
            for gi in range(group):
                hd = k * group + gi
                rs = slice(gi * qb, (gi + 1) * qb)
                o_s = as_ref[k, rs, :] / jnp.maximum(ls_ref[k, rs, :], TINY)
                o_w = aw_ref[k, rs, :] / jnp.maximum(lw_ref[k, rs, :], TINY)
                o = gate[:, 3 * hd:3 * hd + 1] * oc_ref[k, rs, :]
                o = o + gate[:, 3 * hd + 1:3 * hd + 2] * o_s
                o = o + gate[:, 3 * hd + 2:3 * hd + 3] * o_w
                o_ref[0, :, LANES * hd:LANES * (hd + 1)] = o.astype(o_ref.dtype)


def _nsa_tables(T, qb, q0, w_off):
    rows = []
    for qi in range(T // qb):
        q_lo = q0 + qi * qb
        q_hi = q_lo + qb - 1
        last_kt = q_hi // KEY_TILE
        w_lo = max(q_lo - WINDOW + 1, w_off)
        wt0 = (w_lo - w_off) // KEY_TILE
        wt1 = (q_hi - w_off) // KEY_TILE
        n_w = wt1 - wt0 + 1
        assert n_w <= last_kt + 1
        for kt in range(last_kt + 1):
            rows.append((qi, kt, wt0 + min(kt, n_w - 1), int(kt < n_w), int(kt == last_kt)))
    tab = np.asarray(rows, np.int32).T
    return [jnp.asarray(tab[i]) for i in range(5)]


def _overlap_map(n_cmp_rows, n_selp):
    m = np.arange(n_cmp_rows)[None, :]
    s = np.arange(n_selp)[:, None]
    c0 = (m - 1) * CMP_STRIDE
    ov = np.minimum(c0 + CMP_LEN, s * SEL_BLOCK + SEL_BLOCK) - np.maximum(c0, s * SEL_BLOCK)
    w = np.maximum(ov, 0).astype(np.float32) / CMP_LEN
    w[:, 0] = 0.0
    return jnp.asarray(w, BF16)


def nsa_attend(qc, qr, gates, cmp_p, sel_p, win_p, T, qb, q0, w_off, n_sel):
    B = qc.shape[0]
    group = N_HEADS // N_KV
    R = group * qb
    n_cmp = cmp_p.shape[2]
    n_selp = -(-n_sel // LANES) * LANES
    tabs = _nsa_tables(T, qb, q0, w_off)
    n_steps = int(tabs[0].shape[0])
    wmap = _overlap_map(n_cmp, n_selp)
    odt = BF16 if qb % 16 == 0 else F32
    kern = functools.partial(_nsa_kernel, qb=qb, n_sel=n_sel, n_selp=n_selp, n_cmp=n_cmp,
                             q0=q0, w_off=w_off)
    qspec = pl.BlockSpec((1, N_KV, 1, R, LANES), lambda b, s, qi, kt, wt, wf, la: (b, 0, qi[s], 0, 0))
    grid_spec = pltpu.PrefetchScalarGridSpec(
        num_scalar_prefetch=5,
        grid=(B, n_steps),
        in_specs=[
            qspec, qspec,
            pl.BlockSpec((1, qb, LANES), lambda b, s, qi, kt, wt, wf, la: (b, qi[s], 0)),
            pl.BlockSpec((1, N_KV, n_cmp, LANES), lambda b, s, qi, kt, wt, wf, la: (b, 0, 0, 0)),
            pl.BlockSpec((1, N_KV, KEY_TILE, LANES), lambda b, s, qi, kt, wt, wf, la: (b, 0, kt[s], 0)),
            pl.BlockSpec((1, N_KV, KEY_TILE, LANES), lambda b, s, qi, kt, wt, wf, la: (b, 0, wt[s], 0)),
            pl.BlockSpec((n_selp, n_cmp), lambda b, s, qi, kt, wt, wf, la: (0, 0)),
        ],
        out_specs=pl.BlockSpec((1, qb, N_HEADS * LANES), lambda b, s, qi, kt, wt, wf, la: (b, qi[s], 0)),
        scratch_shapes=[
            pltpu.VMEM((N_KV, qb, n_selp), F32),
            pltpu.VMEM((N_KV, R, LANES), F32),
            pltpu.VMEM((N_KV, R, 1), F32), pltpu.VMEM((N_KV, R, 1), F32), pltpu.VMEM((N_KV, R, LANES), F32),
            pltpu.VMEM((N_KV, R, 1), F32), pltpu.VMEM((N_KV, R, 1), F32), pltpu.VMEM((N_KV, R, LANES), F32),
        ],
    )
    return pl.pallas_call(
        kern,
        out_shape=jax.ShapeDtypeStruct((B, T, N_HEADS * LANES), odt),
        grid_spec=grid_spec,
        compiler_params=_cparams(("parallel", "arbitrary")),
        name="nsa_attend",
    )(*tabs, qc, qr, gates, cmp_p, sel_p, win_p, wmap)


def _qgt_kernel(x_ref, g_ref, w_ref, gain_ref, cos_ref, sin_ref, bd_ref,
                qc_ref, qr_ref, gate_ref, *, qb, n_qb):
    h = _rms_rows(x_ref[0], g_ref[...]).astype(BF16)
    y = jnp.dot(h, w_ref[...], preferred_element_type=F32)
    cos = cos_ref[...]
    sin = sin_ref[...]
    bd = bd_ref[...]
    nq = N_HEADS * HEAD_DIM
    group = N_HEADS // N_KV
    gate_t = jax.nn.sigmoid(y[:, nq:nq + LANES]).T
    for j in range(n_qb):
        gate_ref[0, j] = gate_t[:, j * qb:(j + 1) * qb]
    pad = jnp.zeros((N_KV, n_qb, HEAD_DIM, group * qb), qc_ref.dtype)
    qc_ref[0, :, :, HEAD_DIM:, :] = pad
    qr_ref[0, :, :, HEAD_DIM:, :] = pad
    for p in range(N_HEADS // 2):
        qx = y[:, LANES * p:LANES * (p + 1)]
        qn = qx * lax.rsqrt(_head_ms(qx, bd) + EPS) * gain_ref[...]
        qr = _rope64(qn, cos, sin)
        for src, dst in ((qn, qc_ref), (qr, qr_ref)):
            st = src.T.astype(dst.dtype)
            for par in range(2):
                kvh, gi = divmod(2 * p + par, group)
                for j in range(n_qb):
                    dst[0, kvh, j, 0:HEAD_DIM, gi * qb:(gi + 1) * qb] = (
                        st[par * HEAD_DIM:(par + 1) * HEAD_DIM, j * qb:(j + 1) * qb])


def qg_project_t(x, g, w, gain, cos, sin, bd, tm, qb):
    B, T, D = x.shape
    N = w.shape[1]
    n_qb = tm // qb
    group = N_HEADS // N_KV
    kern = functools.partial(_qgt_kernel, qb=qb, n_qb=n_qb)
    qshape = jax.ShapeDtypeStruct((B, N_KV, T // qb, LANES, group * qb), BF16)
    qspec = pl.BlockSpec((1, N_KV, n_qb, LANES, group * qb), lambda b, t: (b, 0, t, 0, 0))
    return pl.pallas_call(
        kern,
        out_shape=[qshape, qshape, jax.ShapeDtypeStruct((B, T // qb, LANES, qb), F32)],
        grid=(B, T // tm),
        in_specs=[pl.BlockSpec((1, tm, D), lambda b, t: (b, t, 0)),
                  pl.BlockSpec((1, D), lambda b, t: (0, 0)),
                  pl.BlockSpec((D, N), lambda b, t: (0, 0)),
                  pl.BlockSpec((1, LANES), lambda b, t: (0, 0)),
                  pl.BlockSpec((tm, LANES), lambda b, t: (t, 0)),
                  pl.BlockSpec((tm, LANES), lambda b, t: (t, 0)),
                  pl.BlockSpec((LANES, LANES), lambda b, t: (0, 0))],
        out_specs=[qspec, qspec, pl.BlockSpec((1, n_qb, LANES, qb), lambda b, t: (b, t, 0, 0))],
        compiler_params=_cparams(("parallel", "parallel")),
        name="qg_project_t",
    )(x, g.reshape(1, D), w, gain, cos, sin, bd)


def _nsa_t_kernel(qc_ref, qr_ref, gate_ref, cmp_ref, sel_ref, win_ref, wmap_ref, o_ref,
                  selneg_ref, m_ref, l_ref, acc_ref,
                  *, qb, n_sel, n_selp, n_cmp, q0, w_off, w_rows, l_win):
    qi = pl.program_id(1)
    group = N_HEADS // N_KV
    R = group * qb
    blocks_per_tile = KEY_TILE // SEL_BLOCK
    q_lo = q0 + qi * qb
    qpos_q = q_lo + lax.broadcasted_iota(jnp.int32, (1, qb), 1)
    n_kt = (q_lo + qb - 1) // KEY_TILE + 1
    w_start = pl.multiple_of(jnp.clip(q_lo - WINDOW - w_off, 0, l_win - w_rows), LANES)
    gate = gate_ref[0, 0]

    def lanes4(a):
        return jnp.concatenate([a] * group, axis=1)

    m_idx = lax.broadcasted_iota(jnp.int32, (n_cmp, qb), 0)
    cvalid = (m_idx >= 1) & ((m_idx - 1) * CMP_STRIDE + CMP_LEN - 1 <= qpos_q)
    cbias = lanes4(jnp.where(cvalid, 0.0, NEG_INF))
    any_c = lanes4(qpos_q >= CMP_LEN - 1)
    blk = lax.broadcasted_iota(jnp.int32, (n_selp, qb), 0)
    blk_f = blk.astype(F32)
    cur = qpos_q >> 6
    forced = (blk == 0) | (blk == cur) | (blk == cur - 1)
    reach = blk * SEL_BLOCK <= qpos_q
    real = blk < n_sel
    wpos = w_off + w_start + lax.broadcasted_iota(jnp.int32, (w_rows, qb), 0)
    wbias = lanes4(jnp.where((wpos <= qpos_q) & (wpos > qpos_q - WINDOW) & (wpos >= 0), 0.0, NEG_INF))
    row_t = lax.broadcasted_iota(jnp.int32, (KEY_TILE, qb), 0)

    for k in range(N_KV):
        ckv = cmp_ref[0, k]
        s = jnp.dot(ckv, qc_ref[0, k, 0], preferred_element_type=F32) + cbias
        e = jnp.exp2(s - jnp.max(s, axis=0, keepdims=True))
        den = jnp.maximum(jnp.sum(e, axis=0, keepdims=True), TINY)
        p = e * jnp.where(any_c, 1.0 / den, 0.0)
        oc = _tn_dot(ckv, p.astype(BF16))
        psum = p[:, 0:qb]
        for gi in range(1, group):
            psum = psum + p[:, gi * qb:(gi + 1) * qb]
        hi = psum.astype(BF16)
        lo = (psum - hi.astype(F32)).astype(BF16)
        imp = (jnp.dot(wmap_ref[...], hi, preferred_element_type=F32)
               + jnp.dot(wmap_ref[...], lo, preferred_element_type=F32))
        v = jnp.where(forced, SEL_FORCE, jnp.where(reach, imp, SEL_NEG))
        v = jnp.where(real, v, -jnp.inf)

        def pick_one(_, carry):
            v, sel = carry
            m = jnp.max(v, axis=0, keepdims=True)
            first = jnp.min(jnp.where(v == m, blk_f, float(n_selp)), axis=0, keepdims=True)
            pick = blk_f == first
            sel = jnp.where(pick & (m > 0.5 * SEL_NEG), 0.0, sel)
            return jnp.where(pick, -jnp.inf, v), sel

        _, sel = lax.fori_loop(0, SEL_TOPK, pick_one, (v, jnp.full((n_selp, qb), NEG_INF, F32)))
        selneg_ref[...] = sel

        qr = qr_ref[0, k, 0]
        m_ref[...] = jnp.full(m_ref.shape, NEG_INF, F32)
        l_ref[...] = jnp.zeros(l_ref.shape, F32)
        acc_ref[...] = jnp.zeros(acc_ref.shape, F32)

        def tile(kt, carry):
            start = pl.multiple_of(kt * KEY_TILE, KEY_TILE)
            kv = sel_ref[0, k, pl.ds(start, KEY_TILE), :]
            s = jnp.dot(kv, qr, preferred_element_type=F32)
            pieces = [jnp.broadcast_to(selneg_ref[pl.ds(kt * blocks_per_tile + j, 1), :], (SEL_BLOCK, qb))
                      for j in range(blocks_per_tile)]
            bias = jnp.concatenate(pieces, axis=0) + jnp.where(start + row_t <= qpos_q, 0.0, NEG_INF)
            s = s + lanes4(bias)
            m_old = m_ref[...]
            m_new = jnp.maximum(m_old, jnp.max(s, axis=0, keepdims=True))
            alpha = jnp.exp2(m_old - m_new)
            p = jnp.exp2(s - m_new)
            l_ref[...] = alpha * l_ref[...] + jnp.sum(p, axis=0, keepdims=True)
            acc_ref[...] = alpha * acc_ref[...] + _tn_dot(kv, p.astype(BF16))
            m_ref[...] = m_new
            return carry

        lax.fori_loop(0, n_kt, tile, 0)
        o_s = acc_ref[...] * (1.0 / jnp.maximum(l_ref[...], TINY))

        wkv = win_ref[0, k, pl.ds(w_start, w_rows), :]
        s = jnp.dot(wkv, qr, preferred_element_type=F32) + wbias
        e = jnp.exp2(s - jnp.max(s, axis=0, keepdims=True))
        den = jnp.maximum(jnp.sum(e, axis=0, keepdims=True), TINY)
        o_w = _tn_dot(wkv, e.astype(BF16)) * (1.0 / den)

        for gi in range(group):
            hd = k * group + gi
            sl = slice(gi * qb, (gi + 1) * qb)
            o = gate[3 * hd:3 * hd + 1, :] * oc[:, sl]
            o = o + gate[3 * hd + 1:3 * hd + 2, :] * o_s[:, sl]
            o = o + gate[3 * hd + 2:3 * hd + 3, :] * o_w[:, sl]
            o_ref[0, :, LANES * hd:LANES * (hd + 1)] = o.T.astype(o_ref.dtype)


def nsa_attend_t(qc, qr, gates, cmp_p, sel_p, win_p, T, qb, q0, w_off, n_sel):
    B = qc.shape[0]
    group = N_HEADS // N_KV
    R = group * qb
    n_cmp = cmp_p.shape[2]
    n_selp = -(-n_sel // LANES) * LANES
    l_sel = sel_p.shape[2]
    l_win = win_p.shape[2]
    w_rows = WINDOW + max(qb, LANES)
    assert l_win >= w_rows and l_sel >= ((q0 + T - 1) // KEY_TILE + 1) * KEY_TILE
    wmap = _overlap_map(n_cmp, n_selp)
    kern = functools.partial(_nsa_t_kernel, qb=qb, n_sel=n_sel, n_selp=n_selp, n_cmp=n_cmp,
                             q0=q0, w_off=w_off, w_rows=w_rows, l_win=l_win)
    qspec = pl.BlockSpec((1, N_KV, 1, LANES, R), lambda b, i: (b, 0, i, 0, 0))
    resident = dict(pipeline_mode=pl.Buffered(1))
    return pl.pallas_call(
        kern,
        out_shape=jax.ShapeDtypeStruct((B, T, N_HEADS * LANES), BF16),
        grid=(B, T // qb),
        in_specs=[
            qspec, qspec,
            pl.BlockSpec((1, 1, LANES, qb), lambda b, i: (b, i, 0, 0)),
            pl.BlockSpec((1, N_KV, n_cmp, LANES), lambda b, i: (b, 0, 0, 0)),
            pl.BlockSpec((1, N_KV, l_sel, LANES), lambda b, i: (b, 0, 0, 0), **resident),
            pl.BlockSpec((1, N_KV, l_win, LANES), lambda b, i: (b, 0, 0, 0), **resident),
            pl.BlockSpec((n_selp, n_cmp), lambda b, i: (0, 0)),
        ],
        out_specs=pl.BlockSpec((1, qb, N_HEADS * LANES), lambda b, i: (b, i, 0)),
        scratch_shapes=[
            pltpu.VMEM((n_selp, qb), F32),
            pltpu.VMEM((1, R), F32), pltpu.VMEM((1, R), F32), pltpu.VMEM((LANES, R), F32),
        ],
        compiler_params=_cparams(("parallel", "arbitrary")),
        name="nsa_attend_t",
    )(qc, qr, gates, cmp_p, sel_p, win_p, wmap)


def _nsa_fast_kernel(shift_ref, qc_ref, qr_ref, gate_ref, cmp_ref, sel_ref, win_ref, wmap_ref, hot_ref,
                     o_ref, qaug_ref, oc_ref, l_ref, acc_ref,
                     *, qb, n_sel, n_selp, n_cmp, q0, w_off, w_rows, l_win):
    qi = pl.program_id(1)
    group = N_HEADS // N_KV
    R = group * qb
    q_lo = q0 + qi * qb
    qpos_q = q_lo + lax.broadcasted_iota(jnp.int32, (1, qb), 1)
    n_kt = (q_lo + qb - 1) // KEY_TILE + 1
    w_start = pl.multiple_of(jnp.clip(q_lo - WINDOW - w_off, 0, l_win - w_rows), LANES)
    shift_c = shift_ref[0]
    shift_s = shift_ref[1]
    shift_w = shift_ref[2]

    def lanes4(a):
        return jnp.concatenate([a] * group, axis=1)

    m_idx = lax.broadcasted_iota(jnp.int32, (n_cmp, qb), 0)
    cvalid = (m_idx >= 1) & ((m_idx - 1) * CMP_STRIDE + CMP_LEN - 1 <= qpos_q)
    cbias = lanes4(jnp.where(cvalid, -shift_c, NEG_INF))
    any_c = lanes4(qpos_q >= CMP_LEN - 1)
    blk = lax.broadcasted_iota(jnp.int32, (n_selp, qb), 0)
    blk_f = blk.astype(F32)
    cur = qpos_q >> 6
    forced = (blk == 0) | (blk == cur) | (blk == cur - 1)
    reach = blk * SEL_BLOCK <= qpos_q
    real = blk < n_sel

    imps = []
    for k in range(N_KV):
        ckv = cmp_ref[0, k]
        e = jnp.exp2(jnp.dot(ckv, qc_ref[0, k, 0], preferred_element_type=F32) + cbias)
        den = jnp.maximum(jnp.sum(e, axis=0, keepdims=True), TINY)
        p = e * jnp.where(any_c, 1.0 / den, 0.0)
        oc_ref[k] = _tn_dot(ckv, p.astype(BF16))
        psum = p[:, 0:qb]
        for gi in range(1, group):
            psum = psum + p[:, gi * qb:(gi + 1) * qb]
        hi = psum.astype(BF16)
        lo = (psum - hi.astype(F32)).astype(BF16)
        imp = (jnp.dot(wmap_ref[...], hi, preferred_element_type=F32)
               + jnp.dot(wmap_ref[...], lo, preferred_element_type=F32))
        v = jnp.where(forced, SEL_FORCE, jnp.where(reach, imp, SEL_NEG))
        imps.append(jnp.where(real, v, -jnp.inf))

    def pick_one(_, carry):
        out = []
        for v, sel in carry:
            m = jnp.max(v, axis=0, keepdims=True)
            first = jnp.min(jnp.where(v == m, blk_f, float(n_selp)), axis=0, keepdims=True)
            pick = blk_f == first
            sel = jnp.where(pick & (m > 0.5 * SEL_NEG), -shift_s, sel)
            out.append((jnp.where(pick, -jnp.inf, v), sel))
        return tuple(out)

    none = jnp.full((n_selp, qb), NEG_INF, F32)
    picked = lax.fori_loop(0, SEL_TOPK, pick_one, tuple((v, none) for v in imps))
    for k in range(N_KV):
        qaug_ref[k, 0:LANES, :] = qr_ref[0, k, 0]
        qaug_ref[k, LANES:, :] = lanes4(picked[k][1]).astype(BF16)

    l_ref[...] = jnp.zeros(l_ref.shape, F32)
    acc_ref[...] = jnp.zeros(acc_ref.shape, F32)

    def tile(start, causal_bias):
        hot = hot_ref[pl.ds(start, KEY_TILE), :]
        for k in range(N_KV):
            kv = sel_ref[0, k, pl.ds(start, KEY_TILE), :]
            s = jnp.dot(jnp.concatenate([kv, hot], axis=1), qaug_ref[k], preferred_element_type=F32)
            if causal_bias is not None:
                s = s + causal_bias
            p = jnp.exp2(s)
            l_ref[k] += jnp.sum(p.reshape(KEY_TILE // 8, 8, R), axis=0)
            acc_ref[k] += _tn_dot(kv, p.astype(BF16))

    def full_tile(kt, carry):
        tile(pl.multiple_of(kt * KEY_TILE, KEY_TILE), None)
        return carry

    lax.fori_loop(0, n_kt - 1, full_tile, 0)
    d_start = pl.multiple_of((n_kt - 1) * KEY_TILE, KEY_TILE)
    row_t = lax.broadcasted_iota(jnp.int32, (KEY_TILE, qb), 0)
    tile(d_start, lanes4(jnp.where(d_start + row_t <= qpos_q, 0.0, NEG_INF)))

    wpos = w_off + w_start + lax.broadcasted_iota(jnp.int32, (w_rows, qb), 0)
    wvalid = (wpos <= qpos_q) & (wpos > qpos_q - WINDOW) & (wpos >= 0)
    wbias = lanes4(jnp.where(wvalid, -shift_w, NEG_INF))
    gate = gate_ref[0, 0]
    for k in range(N_KV):
        wkv = win_ref[0, k, pl.ds(w_start, w_rows), :]
        e = jnp.exp2(jnp.dot(wkv, qr_ref[0, k, 0], preferred_element_type=F32) + wbias)
        den = jnp.maximum(jnp.sum(e, axis=0, keepdims=True), TINY)
        o_w = _tn_dot(wkv, e.astype(BF16)) * (1.0 / den)
        l_s = jnp.maximum(jnp.sum(l_ref[k], axis=0, keepdims=True), TINY)
        o_s = acc_ref[k] * (1.0 / l_s)
        oc = oc_ref[k]
        for gi in range(group):
            hd = k * group + gi
            sl = slice(gi * qb, (gi + 1) * qb)
            o = gate[3 * hd:3 * hd + 1, :] * oc[:, sl]
            o = o + gate[3 * hd + 1:3 * hd + 2, :] * o_s[:, sl]
            o = o + gate[3 * hd + 2:3 * hd + 3, :] * o_w[:, sl]
            o_ref[0, :, LANES * hd:LANES * (hd + 1)] = o.T.astype(o_ref.dtype)


def nsa_attend_fast(shifts, qc, qr, gates, cmp_p, sel_p, win_p, T, qb, q0, w_off, n_sel):
    B = qc.shape[0]
    group = N_HEADS // N_KV
    R = group * qb
    n_cmp = cmp_p.shape[2]
    n_selp = -(-n_sel // LANES) * LANES
    l_sel = sel_p.shape[2]
    l_win = win_p.shape[2]
    w_rows = WINDOW + max(qb, LANES)
    assert l_win >= w_rows and l_sel >= ((q0 + T - 1) // KEY_TILE + 1) * KEY_TILE
    wmap = _overlap_map(n_cmp, n_selp)
    hot = np.zeros((l_sel, n_selp), np.float32)
    hot[np.arange(l_sel), np.arange(l_sel) // SEL_BLOCK] = 1.0
    hot = jnp.asarray(hot, BF16)
    kern = functools.partial(_nsa_fast_kernel, qb=qb, n_sel=n_sel, n_selp=n_selp, n_cmp=n_cmp,
                             q0=q0, w_off=w_off, w_rows=w_rows, l_win=l_win)
    qspec = pl.BlockSpec((1, N_KV, 1, LANES, R), lambda b, i, sh: (b, 0, i, 0, 0))
    resident = dict(pipeline_mode=pl.Buffered(1))
    grid_spec = pltpu.PrefetchScalarGridSpec(
        num_scalar_prefetch=1,
        grid=(B, T // qb),
        in_specs=[
            qspec, qspec,
            pl.BlockSpec((1, 1, LANES, qb), lambda b, i, sh: (b, i, 0, 0)),
            pl.BlockSpec((1, N_KV, n_cmp, LANES), lambda b, i, sh: (b, 0, 0, 0)),
            pl.BlockSpec((1, N_KV, l_sel, LANES), lambda b, i, sh: (b, 0, 0, 0), **resident),
            pl.BlockSpec((1, N_KV, l_win, LANES), lambda b, i, sh: (b, 0, 0, 0), **resident),
            pl.BlockSpec((n_selp, n_cmp), lambda b, i, sh: (0, 0)),
            pl.BlockSpec((l_sel, n_selp), lambda b, i, sh: (0, 0), **resident),
        ],
        out_specs=pl.BlockSpec((1, qb, N_HEADS * LANES), lambda b, i, sh: (b, i, 0)),
        scratch_shapes=[
            pltpu.VMEM((N_KV, 2 * LANES, R), BF16),
            pltpu.VMEM((N_KV, LANES, R), F32),
            pltpu.VMEM((N_KV, 8, R), F32),
            pltpu.VMEM((N_KV, LANES, R), F32),
        ],
    )
    return pl.pallas_call(
        kern,
        out_shape=jax.ShapeDtypeStruct((B, T, N_HEADS * LANES), BF16),
        grid_spec=grid_spec,
        compiler_params=_cparams(("parallel", "arbitrary")),
        name="nsa_attend_fast",
    )(shifts, qc, qr, gates, cmp_p, sel_p, win_p, wmap, hot)


def _rope_tables(pos, half):
    inv = jnp.exp(-math.log(ROPE_THETA) * jnp.arange(half, dtype=F32) / half)
    ang = pos.astype(F32)[:, None] * inv[None, :]
    return jnp.cos(ang), jnp.sin(ang)


def _prep_weights(ret_w_in, ret_w_out, ffn_w_in, ffn_w_out, kv_w, kv_knorm, cmp_w1, cmp_w2,
                  nsa_w_qg, nsa_qnorm, nsa_w_o):
    n_b = nsa_w_qg.shape[0]
    nq = N_HEADS * HEAD_DIM
    qg_pad = nq + LANES - nsa_w_qg.shape[2]
    w_qg = jnp.pad(nsa_w_qg, ((0, 0), (0, 0), (0, qg_pad))).astype(BF16)
    w_o = jnp.pad(nsa_w_o.reshape(n_b, N_HEADS, 1, HEAD_DIM, D_MODEL),
                  ((0, 0), (0, 0), (1, 0), (0, 0), (0, 0))).reshape(n_b, N_HEADS * LANES, D_MODEL)
    R = CMP_LEN // CMP_STRIDE
    w1 = cmp_w1.reshape(2, R, CMP_STRIDE, HEAD_DIM, CMP_HID).transpose(0, 2, 3, 1, 4)
    w1 = w1.reshape(2, CMP_STRIDE, HEAD_DIM, R * CMP_HID)
    w1dup = jnp.concatenate([w1, w1], axis=2).astype(BF16)
    z = jnp.zeros((CMP_HID, HEAD_DIM), F32)
    w2p = jnp.stack([jnp.concatenate([cmp_w2[0], z], axis=1),
                     jnp.concatenate([z, cmp_w2[1]], axis=1)]).astype(BF16)
    ones = jnp.ones((HEAD_DIM,), F32)
    return dict(
        ret_w_in=ret_w_in.astype(BF16), ret_w_out=ret_w_out.astype(BF16),
        ffn_w_in=ffn_w_in.astype(BF16), ffn_w_out=ffn_w_out.astype(BF16),
        kv_w=kv_w.astype(BF16), w_qg=w_qg, w_o=w_o.astype(BF16),
        kv_gain=jnp.tile(kv_knorm, (1, 2)),
        cmp_gain=jnp.concatenate([kv_knorm[0], ones]).reshape(1, LANES),
        q_gain=jnp.tile(nsa_qnorm, (1, 2)) * (HEAD_DIM ** -0.5),
        q_gain2=jnp.tile(nsa_qnorm, (1, 2)) * (HEAD_DIM ** -0.5 * math.log2(math.e)),
        score_bound=(1.05 * HEAD_DIM ** 0.5 * math.log2(math.e))
        * jnp.max(jnp.abs(nsa_qnorm), axis=1)[:, None] * jnp.max(jnp.abs(kv_knorm), axis=1)[None, :],
        w1dup=w1dup, w2p=w2p,
        bd=jnp.asarray(np.kron(np.eye(2), np.ones((HEAD_DIM, HEAD_DIM))), BF16),
    )


def _trunk(x, past_len, ret_s0, conv0, ctx, W, P):
    B, T, D = x.shape
    M = B * T
    depth = P["norm_mix"].shape[0]
    n_a = P["ret_w_in"].shape[0]
    pos = past_len + jnp.arange(T)
    cos_r, sin_r = _rope_tables(pos, RET_DK // 2)
    c32, s32 = _rope_tables(pos, HEAD_DIM // 2)
    cos_n = jnp.tile(c32, (1, 4))
    sin_n = jnp.concatenate([-s32, s32, -s32, s32], axis=1)
    lg = jnp.log1p(-jnp.exp2(-5.0 - jnp.arange(RET_HEADS, dtype=F32)))
    L = RET_CHUNK if T % RET_CHUNK == 0 else T
    gl = jnp.exp(L * lg)
    tm = min(512, M)
    tf = 1024 if T % 1024 == 0 else 512
    tb = min(512, T)
    tq = min(512, T)
    qb = Q_BLOCK if T % Q_BLOCK == 0 else T
    tt = min(256, T)

    x2 = x.reshape(M, D)
    ret_states, conv_states = [], []
    for layer in range(depth):
        if layer == n_a:
            kvp, selp, winp = kv_project(x2.reshape(B, T, D), P["kv_norm"], W["kv_w"], W["kv_gain"],
                                         cos_n, sin_n, W["bd"], tq)
            nk2 = 2 * N_KV * HEAD_DIM
            if ctx is None:
                table = jnp.arange(M // PAGE, dtype=jnp.int32).reshape(B, T // PAGE)
                cmp_p = compress(kvp.reshape(M // PAGE, PAGE, 3 * nk2), table,
                                 W["w1dup"], W["pb"], W["w2p"], W["cmp_gain"])
                sel_p, win_p = selp, winp
                w_off = 0
            else:
                cache_cmp, cache_sel, cache_win, table = ctx
                n_pool = cache_cmp.shape[0]
                cmp_p = compress(cache_cmp.reshape(n_pool, PAGE, nk2), table,
                                 W["w1dup"], W["pb"], W["w2p"], W["cmp_gain"])
                sel_p = repack(cache_sel.reshape(n_pool, PAGE, nk2), table, selp)
                wl = cache_win.shape[1]
                wtab = jnp.arange(B * (wl // PAGE), dtype=jnp.int32).reshape(B, wl // PAGE)
                win_p = repack(cache_win.reshape(B * (wl // PAGE), PAGE, nk2), wtab, winp)
                w_off = past_len - wl
            n_sel = -(-(past_len + T) // SEL_BLOCK)
        h_norm = P["norm_mix"][layer]
        if layer < n_a:
            proj = norm_matmul(x2, h_norm, W["ret_w_in"][layer], tm, 1024)
            og, s_new = retention(proj.reshape(B, T, -1), ret_s0[layer], cos_r, sin_r, lg, gl, L, tb)
            ret_states.append(s_new)
            x2 = matmul_res(og.reshape(M, -1), W["ret_w_out"][layer], x2, tm)
        else:
            j = layer - n_a
            if qb % LANES == 0:
                qc, qr, gates = qg_project_t(x2.reshape(B, T, D), h_norm, W["w_qg"][j],
                                             W["q_gain2"][j:j + 1], cos_n, sin_n, W["bd"], tq, qb)
                shifts = W["score_bound"][j]
                args = (qc, qr, gates, cmp_p, sel_p, win_p)
                o = lax.cond(
                    jnp.max(shifts) <= 30.0,
                    lambda a: nsa_attend_fast(shifts, *a, T, qb, past_len, w_off, n_sel),
                    lambda a: nsa_attend_t(*a, T, qb, past_len, w_off, n_sel),
                    args)
            else:
                qc, qr, gates = qg_project(x2.reshape(B, T, D), h_norm, W["w_qg"][j],
                                           W["q_gain"][j:j + 1], cos_n, sin_n, W["bd"], tq, qb)
                o = nsa_attend(qc, qr, gates, cmp_p, sel_p, win_p, T, qb, past_len, w_off, n_sel)
            x2 = matmul_res(o.reshape(M, -1), W["w_o"][j], x2, tm)
        if T % tf == 0:
            act, tail = ffn_in(x2, P["norm_ffn"][layer], W["ffn_w_in"][layer], conv0[layer],
                               P["ffn_conv_w"][layer], P["ffn_conv_b"][layer], T, tf, 256)
            conv_states.append(tail)
        else:
            proj = norm_matmul(x2, P["norm_ffn"][layer], W["ffn_w_in"][layer], tm, 512)
            proj3 = proj.reshape(B, T, 2 * D_FF)
            act = ffn_mid(proj3, conv0[layer], P["ffn_conv_w"][layer], P["ffn_conv_b"][layer], tt)
            conv_states.append(proj3[:, T - 2:, :D_FF])
        x2 = matmul_res(act.reshape(M, D_FF), W["ffn_w_out"][layer], x2, tm)

    nk = N_KV * HEAD_DIM
    new_cmp = kvp[:, :, 0:2 * nk].reshape(B, T, 2, N_KV, HEAD_DIM)
    new_sel = kvp[:, :, 2 * nk:4 * nk].reshape(B, T, 2, N_KV, HEAD_DIM)
    new_win = kvp[:, :, 4 * nk:6 * nk].reshape(B, T, 2, N_KV, HEAD_DIM)
    return (x2.reshape(B, T, D), jnp.stack(ret_states), jnp.stack(conv_states),
            new_cmp, new_sel, new_win)


def kernel(x_prompt, x_sample, cache_cmp_kv, cache_sel_kv, cache_win_kv, state_ret, state_conv,
           page_table, norm_mix, norm_ffn, ret_w_in, ret_w_out, ffn_w_in, ffn_conv_w, ffn_conv_b,
           ffn_w_out, kv_norm, kv_w, kv_knorm, cmp_pos, cmp_w1, cmp_w2, nsa_w_qg, nsa_qnorm, nsa_w_o):
    W = _prep_weights(ret_w_in, ret_w_out, ffn_w_in, ffn_w_out, kv_w, kv_knorm, cmp_w1, cmp_w2,
                      nsa_w_qg, nsa_qnorm, nsa_w_o)
    W["pb"] = pos_bias(cmp_pos, cmp_w1)
    P = dict(norm_mix=norm_mix, norm_ffn=norm_ffn, ret_w_in=ret_w_in, ffn_conv_w=ffn_conv_w,
             ffn_conv_b=ffn_conv_b, kv_norm=kv_norm)
    depth = norm_mix.shape[0]
    n_a = ret_w_in.shape[0]
    B, T, _ = x_prompt.shape
    zero_ret = jnp.zeros((n_a, B, RET_HEADS, RET_DK, RET_DV), F32)
    zero_conv = jnp.zeros((depth, B, 2, D_FF), F32)
    y_p, ret_p, conv_p, cmp_p, sel_p, win_p = _trunk(x_prompt, 0, zero_ret, zero_conv, None, W, P)
    win_p = win_p[:, T - min(WINDOW, T):]

    db, ts, _ = x_sample.shape
    past_len = page_table.shape[1] * PAGE
    ctx = (cache_cmp_kv, cache_sel_kv, cache_win_kv, page_table)
    y_s, ret_s, conv_s, cmp_s, sel_s, win_new = _trunk(x_sample, past_len, state_ret, state_conv,
                                                        ctx, W, P)
    all_win = jnp.concatenate([cache_win_kv, win_new], axis=1)
    win_s = all_win[:, all_win.shape[1] - min(WINDOW, past_len + ts):]
    return (y_p, y_s, ret_p, ret_s, conv_p, conv_s, cmp_p, cmp_s, sel_p, sel_s, win_p, win_s)
```

```python
import functools
import math

import jax
import jax.numpy as jnp
import numpy as np
from jax import lax
from jax.experimental import pallas as pl
from jax.experimental.pallas import tpu as pltpu

F32 = jnp.float32
BF16 = jnp.bfloat16

D_MODEL = 1024
PAGE = 128
RET_HEADS = 4
RET_DK = 256
RET_DV = 512
RET_CHUNK = 128
N_HEADS = 16
N_KV = 4
HEAD_DIM = 64
CMP_LEN = 32
CMP_STRIDE = 16
CMP_HID = 128
SEL_BLOCK = 64
SEL_TOPK = 16
WINDOW = 512
Q_BLOCK = 128
D_FF = 2816
ROPE_THETA = 10000.0
EPS = 1e-6
NEG_INF = -1e30
TINY = 1e-30
SEL_FORCE = 1e6
SEL_NEG = -1e6

LANES = 128
KEY_TILE = 512
HALO = 16
VMEM_LIMIT = 48 * 1024 * 1024


def _cparams(sem):
    return pltpu.CompilerParams(dimension_semantics=sem, vmem_limit_bytes=VMEM_LIMIT)


def _nt_dot(a, b):
    return lax.dot_general(a, b, (((1,), (1,)), ((), ())), preferred_element_type=F32)


def _tn_dot(a, b):
    return lax.dot_general(a, b, (((0,), (0,)), ((), ())), preferred_element_type=F32)


def _gelu(x):
    return 0.5 * x * (1.0 + jnp.tanh(math.sqrt(2.0 / math.pi) * (x + 0.044715 * (x * x * x))))


def _rms_rows(x, g):
    r = lax.rsqrt(jnp.mean(x * x, axis=-1, keepdims=True) + EPS)
    return x * r * g


def _head_ms(x, bd):
    x2 = x * x
    hi = x2.astype(BF16)
    lo = (x2 - hi.astype(F32)).astype(BF16)
    s = jnp.dot(hi, bd, preferred_element_type=F32) + jnp.dot(lo, bd, preferred_element_type=F32)
    return s * (1.0 / HEAD_DIM)


def _rope64(x, cos, sin):
    lane = lax.broadcasted_iota(jnp.int32, x.shape, 1)
    sw = jnp.where((lane & 63) < 32, pltpu.roll(x, 96, 1), pltpu.roll(x, 32, 1))
    return x * cos + sw * sin


def _pack_pair(k2, v2, dtype):
    lane = lax.broadcasted_iota(jnp.int32, k2.shape, 1)
    lo = lane < HEAD_DIM
    even = jnp.where(lo, k2, pltpu.roll(v2, HEAD_DIM, 1)).astype(dtype)
    odd = jnp.where(lo, pltpu.roll(k2, HEAD_DIM, 1), v2).astype(dtype)
    return even, odd


def _norm_matmul_kernel(x_ref, g_ref, w_ref, o_ref, h_ref):
    @pl.when(pl.program_id(1) == 0)
    def _():
        h_ref[...] = _rms_rows(x_ref[...], g_ref[...]).astype(BF16)

    o_ref[...] = jnp.dot(h_ref[...], w_ref[...], preferred_element_type=F32).astype(o_ref.dtype)


def norm_matmul(x, g, w, tm, tn):
    M, D = x.shape
    N = w.shape[1]
    return pl.pallas_call(
        _norm_matmul_kernel,
        out_shape=jax.ShapeDtypeStruct((M, N), F32),
        grid=(M // tm, N // tn),
        in_specs=[pl.BlockSpec((tm, D), lambda i, j: (i, 0)),
                  pl.BlockSpec((1, D), lambda i, j: (0, 0)),
                  pl.BlockSpec((D, tn), lambda i, j: (0, j))],
        out_specs=pl.BlockSpec((tm, tn), lambda i, j: (i, j)),
        scratch_shapes=[pltpu.VMEM((tm, D), BF16)],
        compiler_params=_cparams(("parallel", "arbitrary")),
        name="norm_matmul",
    )(x, g.reshape(1, D), w)


def _matmul_res_kernel(a_ref, w_ref, r_ref, o_ref):
    o_ref[...] = r_ref[...] + jnp.dot(a_ref[...].astype(BF16), w_ref[...],
                                      preferred_element_type=F32)


def matmul_res(a, w, res, tm):
    M, K = a.shape
    N = w.shape[1]
    return pl.pallas_call(
        _matmul_res_kernel,
        out_shape=jax.ShapeDtypeStruct((M, N), F32),
        grid=(M // tm,),
        in_specs=[pl.BlockSpec((tm, K), lambda i: (i, 0)),
                  pl.BlockSpec((K, N), lambda i: (0, 0)),
                  pl.BlockSpec((tm, N), lambda i: (i, 0))],
        out_specs=pl.BlockSpec((tm, N), lambda i: (i, 0)),
        compiler_params=_cparams(("parallel",)),
        name="matmul_res",
    )(a, w, res)


def _retention_kernel(lg_ref, gl_ref, q_ref, k_ref, v_ref, g_ref, cos_ref, sin_ref, s0_ref,
                      o_ref, sout_ref, S_ref, *, L, n_chunk):
    h = pl.program_id(1)
    t = pl.program_id(2)
    lg = lg_ref[h]
    gl = gl_ref[h]

    @pl.when(t == 0)
    def _():
        S_ref[...] = s0_ref[0, 0]

    ii = lax.broadcasted_iota(jnp.int32, (L, L), 0)
    jj = lax.broadcasted_iota(jnp.int32, (L, L), 1)
    diff = (ii - jj).astype(F32)
    decay = jnp.where(diff >= 0, jnp.exp(jnp.maximum(diff, 0.0) * lg), 0.0)
    idx = lax.broadcasted_iota(jnp.int32, (L, 1), 0).astype(F32)
    q_dec = jnp.exp((idx + 1.0) * lg)
    k_dec = jnp.exp((L - 1.0 - idx) * lg)
    half = RET_DK // 2

    for c in range(n_chunk):
        rows = pl.ds(c * L, L)
        cos = cos_ref[rows, :]
        sin = sin_ref[rows, :]

        def rope(x):
            x1, x2 = x[:, :half], x[:, half:]
            return jnp.concatenate([x1 * cos - x2 * sin, x2 * cos + x1 * sin], axis=1)

        qr = rope(q_ref[0, rows, :])
        kr = rope(k_ref[0, rows, :]) * (RET_DK ** -0.5)
        qb = qr.astype(BF16)
        vb = v_ref[0, rows, :].astype(BF16)
        sc = _nt_dot(qb, kr.astype(BF16)) * decay
        S = S_ref[...]
        o = jnp.dot(sc.astype(BF16), vb, preferred_element_type=F32)
        o = o + jnp.dot(qb, S.astype(BF16), preferred_element_type=F32) * q_dec
        S_ref[...] = S * gl + _tn_dot((kr * k_dec).astype(BF16), vb)
        on = o * lax.rsqrt(jnp.mean(o * o, axis=-1, keepdims=True) + EPS)
        g = g_ref[0, rows, :]
        o_ref[0, rows, :] = (on * (g * jax.nn.sigmoid(g))).astype(o_ref.dtype)

    @pl.when(t == pl.num_programs(2) - 1)
    def _():
        sout_ref[0, 0] = S_ref[...]


def retention(proj, s0, cos, sin, lg, gl, L, tb):
    B, T, _ = proj.shape
    n_chunk = tb // L
    odt = BF16 if tb % 16 == 0 else F32
    kern = functools.partial(_retention_kernel, L=L, n_chunk=n_chunk)
    grid_spec = pltpu.PrefetchScalarGridSpec(
        num_scalar_prefetch=2,
        grid=(B, RET_HEADS, T // tb),
        in_specs=[
            pl.BlockSpec((1, tb, RET_DK), lambda b, h, t, *_: (b, t, h)),
            pl.BlockSpec((1, tb, RET_DK), lambda b, h, t, *_: (b, t, RET_HEADS + h)),
            pl.BlockSpec((1, tb, RET_DV), lambda b, h, t, *_: (b, t, RET_HEADS + h)),
            pl.BlockSpec((1, tb, RET_DV), lambda b, h, t, *_: (b, t, 2 * RET_HEADS + h)),
            pl.BlockSpec((tb, RET_DK // 2), lambda b, h, t, *_: (t, 0)),
            pl.BlockSpec((tb, RET_DK // 2), lambda b, h, t, *_: (t, 0)),
            pl.BlockSpec((1, 1, RET_DK, RET_DV), lambda b, h, t, *_: (b, h, 0, 0)),
        ],
        out_specs=[
            pl.BlockSpec((1, tb, RET_DV), lambda b, h, t, *_: (b, t, h)),
            pl.BlockSpec((1, 1, RET_DK, RET_DV), lambda b, h, t, *_: (b, h, 0, 0)),
        ],
        scratch_shapes=[pltpu.VMEM((RET_DK, RET_DV), F32)],
    )
    return pl.pallas_call(
        kern,
        out_shape=[jax.ShapeDtypeStruct((B, T, RET_HEADS * RET_DV), odt),
                   jax.ShapeDtypeStruct((B, RET_HEADS, RET_DK, RET_DV), F32)],
        grid_spec=grid_spec,
        compiler_params=_cparams(("parallel", "parallel", "arbitrary")),
        name="retention",
    )(lg, gl, proj, proj, proj, proj, cos, sin, s0)


def _ffn_mid_kernel(u_ref, gt_ref, halo_ref, cw_ref, cb_ref, o_ref):
    u = u_ref[0]
    hl = halo_ref[0, 0]
    row = lax.broadcasted_iota(jnp.int32, u.shape, 0)
    u1 = jnp.where(row == 0, hl[1:2], pltpu.roll(u, 1, 0))
    u2 = jnp.where(row == 0, hl[0:1], jnp.where(row == 1, hl[1:2], pltpu.roll(u, 2, 0)))
    c = cb_ref[...] + cw_ref[0:1] * u2
    c = c + cw_ref[1:2] * u1
    c = c + cw_ref[2:3] * u
    o_ref[0] = (_gelu(c) * gt_ref[0]).astype(o_ref.dtype)


def ffn_mid(proj, buf, conv_w, conv_b, tt):
    B, T, _ = proj.shape
    nt = T // tt
    if nt > 1:
        tails = proj[:, :, :D_FF].reshape(B, nt, tt, D_FF)[:, :-1, tt - 2:, :]
        halo = jnp.concatenate([buf[:, None], tails], axis=1)
    else:
        halo = buf[:, None]
    odt = BF16 if tt % 16 == 0 else F32
    return pl.pallas_call(
        _ffn_mid_kernel,
        out_shape=jax.ShapeDtypeStruct((B, T, D_FF), odt),
        grid=(B, nt),
        in_specs=[pl.BlockSpec((1, tt, D_FF), lambda b, t: (b, t, 0)),
                  pl.BlockSpec((1, tt, D_FF), lambda b, t: (b, t, 1)),
                  pl.BlockSpec((1, 1, 2, D_FF), lambda b, t: (b, t, 0, 0)),
                  pl.BlockSpec((3, D_FF), lambda b, t: (0, 0)),
                  pl.BlockSpec((1, D_FF), lambda b, t: (0, 0))],
        out_specs=pl.BlockSpec((1, tt, D_FF), lambda b, t: (b, t, 0)),
        compiler_params=_cparams(("parallel", "parallel")),
        name="ffn_mid",
    )(proj, proj, halo, conv_w, conv_b.reshape(1, D_FF))


def _ffn_in_kernel(x_ref, xh_ref, g_ref, wu_ref, wg_ref, buf_ref, cw_ref, cb_ref,
                   act_ref, tail_ref, h_ref, hh_ref, *, tiles_per_seq):
    i = pl.program_id(0)

    @pl.when(pl.program_id(1) == 0)
    def _():
        h_ref[...] = _rms_rows(x_ref[...], g_ref[...]).astype(BF16)
        hh_ref[...] = _rms_rows(xh_ref[...], g_ref[...]).astype(BF16)

    h = h_ref[...]
    u = jnp.dot(h, wu_ref[...], preferred_element_type=F32)
    gt = jnp.dot(h, wg_ref[...], preferred_element_type=F32)
    uh = jnp.dot(hh_ref[...], wu_ref[...], preferred_element_type=F32)
    seq_start = (i % tiles_per_seq) == 0
    hl = jnp.where(seq_start, buf_ref[0], uh[HALO - 2:, :])
    row = lax.broadcasted_iota(jnp.int32, u.shape, 0)
    u1 = jnp.where(row == 0, hl[1:2], pltpu.roll(u, 1, 0))
    u2 = jnp.where(row == 0, hl[0:1], jnp.where(row == 1, hl[1:2], pltpu.roll(u, 2, 0)))
    c = cb_ref[...] + cw_ref[0:1] * u2
    c = c + cw_ref[1:2] * u1
    c = c + cw_ref[2:3] * u
    act_ref[...] = (_gelu(c) * gt).astype(act_ref.dtype)
    tail_ref[0] = u[u.shape[0] - 2:, :]


def ffn_in(x, g, w, buf, conv_w, conv_b, T, tm, tn):
    M, D = x.shape
    B = M // T
    n_col = D_FF // tn
    tiles_per_seq = T // tm
    kern = functools.partial(_ffn_in_kernel, tiles_per_seq=tiles_per_seq)
    act, tails = pl.pallas_call(
        kern,
        out_shape=[jax.ShapeDtypeStruct((M, D_FF), BF16),
                   jax.ShapeDtypeStruct((M // tm, 2, D_FF), F32)],
        grid=(M // tm, n_col),
        in_specs=[pl.BlockSpec((tm, D), lambda i, j: (i, 0)),
                  pl.BlockSpec((HALO, D), lambda i, j: (jnp.maximum(i * (tm // HALO) - 1, 0), 0)),
                  pl.BlockSpec((1, D), lambda i, j: (0, 0)),
                  pl.BlockSpec((D, tn), lambda i, j: (0, j)),
                  pl.BlockSpec((D, tn), lambda i, j: (0, n_col + j)),
                  pl.BlockSpec((1, 2, tn), lambda i, j: (i // tiles_per_seq, 0, j)),
                  pl.BlockSpec((3, tn), lambda i, j: (0, j)),
                  pl.BlockSpec((1, tn), lambda i, j: (0, j))],
        out_specs=[pl.BlockSpec((tm, tn), lambda i, j: (i, j)),
                   pl.BlockSpec((1, 2, tn), lambda i, j: (i, 0, j))],
        scratch_shapes=[pltpu.VMEM((tm, D), BF16), pltpu.VMEM((HALO, D), BF16)],
        compiler_params=_cparams(("parallel", "arbitrary")),
        name="ffn_in",
    )(x, x, g.reshape(1, D), w, w, buf, conv_w, conv_b.reshape(1, D_FF))
    return act, tails[tiles_per_seq - 1::tiles_per_seq]


def _kv_kernel(x_ref, g_ref, w_ref, gain_ref, cos_ref, sin_ref, bd_ref,
               kv_ref, selp_ref, winp_ref):
    h = _rms_rows(x_ref[0], g_ref[...]).astype(BF16)
    y = jnp.dot(h, w_ref[...], preferred_element_type=F32)
    cos = cos_ref[...]
    sin = sin_ref[...]
    bd = bd_ref[...]
    nk = N_KV * HEAD_DIM
    kv_ref[0, :, 0:2 * nk] = y[:, 0:2 * nk]
    for br, pack_ref in enumerate((selp_ref, winp_ref)):
        base = 2 * nk * (br + 1)
        kv_ref[0, :, base + nk:base + 2 * nk] = y[:, base + nk:base + 2 * nk]
        for p in range(2):
            kx = y[:, base + LANES * p:base + LANES * (p + 1)]
            kn = kx * lax.rsqrt(_head_ms(kx, bd) + EPS) * gain_ref[br + 1:br + 2, :]
            kr = _rope64(kn, cos, sin)
            kv_ref[0, :, base + LANES * p:base + LANES * (p + 1)] = kr
            vx = y[:, base + nk + LANES * p:base + nk + LANES * (p + 1)]
            even, odd = _pack_pair(kr, vx, pack_ref.dtype)
            pack_ref[0, 2 * p] = even
            pack_ref[0, 2 * p + 1] = odd


def kv_project(x, g, w, gains, cos, sin, bd, tm):
    B, T, D = x.shape
    N = w.shape[1]
    pdt = BF16 if tm % 16 == 0 else F32
    return pl.pallas_call(
        _kv_kernel,
        out_shape=[jax.ShapeDtypeStruct((B, T, N), F32),
                   jax.ShapeDtypeStruct((B, N_KV, T, LANES), pdt),
                   jax.ShapeDtypeStruct((B, N_KV, T, LANES), pdt)],
        grid=(B, T // tm),
        in_specs=[pl.BlockSpec((1, tm, D), lambda b, t: (b, t, 0)),
                  pl.BlockSpec((1, D), lambda b, t: (0, 0)),
                  pl.BlockSpec((D, N), lambda b, t: (0, 0)),
                  pl.BlockSpec((3, LANES), lambda b, t: (0, 0)),
                  pl.BlockSpec((tm, LANES), lambda b, t: (t, 0)),
                  pl.BlockSpec((tm, LANES), lambda b, t: (t, 0)),
                  pl.BlockSpec((LANES, LANES), lambda b, t: (0, 0))],
        out_specs=[pl.BlockSpec((1, tm, N), lambda b, t: (b, t, 0)),
                   pl.BlockSpec((1, N_KV, tm, LANES), lambda b, t: (b, 0, t, 0)),
                   pl.BlockSpec((1, N_KV, tm, LANES), lambda b, t: (b, 0, t, 0))],
        compiler_params=_cparams(("parallel", "parallel")),
        name="kv_project",
    )(x, g.reshape(1, D), w, gains, cos, sin, bd)


def _qg_kernel(x_ref, g_ref, w_ref, gain_ref, cos_ref, sin_ref, bd_ref,
               qc_ref, qr_ref, gate_ref, *, qb, n_qb):
    h = _rms_rows(x_ref[0], g_ref[...]).astype(BF16)
    y = jnp.dot(h, w_ref[...], preferred_element_type=F32)
    cos = cos_ref[...]
    sin = sin_ref[...]
    bd = bd_ref[...]
    nq = N_HEADS * HEAD_DIM
    gate_ref[0] = jax.nn.sigmoid(y[:, nq:nq + LANES])
    lane = lax.broadcasted_iota(jnp.int32, (y.shape[0], LANES), 1)
    lo = lane < HEAD_DIM
    group = N_HEADS // N_KV
    for p in range(N_HEADS // 2):
        qx = y[:, LANES * p:LANES * (p + 1)]
        qn = qx * lax.rsqrt(_head_ms(qx, bd) + EPS) * gain_ref[...]
        qr = _rope64(qn, cos, sin)
        for src, dst in ((qn, qc_ref), (qr, qr_ref)):
            for par in range(2):
                hd = 2 * p + par
                kvh, gi = hd // group, hd % group
                v = src if par == 0 else pltpu.roll(src, HEAD_DIM, 1)
                v = jnp.where(lo, v, 0.0).astype(dst.dtype)
                for j in range(n_qb):
                    dst[0, kvh, j, gi * qb:(gi + 1) * qb, :] = v[j * qb:(j + 1) * qb, :]


def qg_project(x, g, w, gain, cos, sin, bd, tm, qb):
    B, T, D = x.shape
    N = w.shape[1]
    n_qb = tm // qb
    group = N_HEADS // N_KV
    qdt = BF16 if qb % 16 == 0 else F32
    kern = functools.partial(_qg_kernel, qb=qb, n_qb=n_qb)
    qshape = jax.ShapeDtypeStruct((B, N_KV, T // qb, group * qb, LANES), qdt)
    qspec = pl.BlockSpec((1, N_KV, n_qb, group * qb, LANES), lambda b, t: (b, 0, t, 0, 0))
    return pl.pallas_call(
        kern,
        out_shape=[qshape, qshape, jax.ShapeDtypeStruct((B, T, LANES), F32)],
        grid=(B, T // tm),
        in_specs=[pl.BlockSpec((1, tm, D), lambda b, t: (b, t, 0)),
                  pl.BlockSpec((1, D), lambda b, t: (0, 0)),
                  pl.BlockSpec((D, N), lambda b, t: (0, 0)),
                  pl.BlockSpec((1, LANES), lambda b, t: (0, 0)),
                  pl.BlockSpec((tm, LANES), lambda b, t: (t, 0)),
                  pl.BlockSpec((tm, LANES), lambda b, t: (t, 0)),
                  pl.BlockSpec((LANES, LANES), lambda b, t: (0, 0))],
        out_specs=[qspec, qspec, pl.BlockSpec((1, tm, LANES), lambda b, t: (b, t, 0))],
        compiler_params=_cparams(("parallel", "parallel")),
        name="qg_project",
    )(x, g.reshape(1, D), w, gain, cos, sin, bd)


def _pos_bias_kernel(p_ref, w_ref, o_ref):
    o_ref[0] = jnp.dot(p_ref[0].astype(BF16), w_ref[0].astype(BF16), preferred_element_type=F32)


def pos_bias(cmp_pos, cmp_w1):
    K = CMP_LEN * HEAD_DIM
    p = jnp.broadcast_to(cmp_pos.reshape(2, 1, K), (2, 8, K))
    out = pl.pallas_call(
        _pos_bias_kernel,
        out_shape=jax.ShapeDtypeStruct((2, 8, CMP_HID), F32),
        grid=(2,),
        in_specs=[pl.BlockSpec((1, 8, K), lambda c: (c, 0, 0)),
                  pl.BlockSpec((1, K, CMP_HID), lambda c: (c, 0, 0))],
        out_specs=pl.BlockSpec((1, 8, CMP_HID), lambda c: (c, 0, 0)),
        name="pos_bias",
    )(p, cmp_w1)
    return out[:, 0, :]


def _compress_kernel(pt_ref, *refs, n_pg):
    pages = refs[:n_pg]
    w1_ref, pb_ref, w2_ref, gain_ref, out_ref, carry_ref, slab_ref = refs[n_pg:]
    g = pl.program_id(1)

    @pl.when(g == 0)
    def _():
        carry_ref[...] = jnp.zeros_like(carry_ref)

    n = n_pg * (PAGE // CMP_STRIDE)
    lane = lax.broadcasted_iota(jnp.int32, (n, LANES), 1)
    row = lax.broadcasted_iota(jnp.int32, (n, LANES), 0)
    lo = lane < HEAD_DIM
    for pair in range(N_KV // 2):
        res = [jnp.zeros((n, LANES), F32), jnp.zeros((n, LANES), F32)]
        for c in range(2):
            col = c * N_KV * HEAD_DIM + pair * LANES
            for i in range(n_pg):
                slab_ref[PAGE * i:PAGE * (i + 1), :] = pages[i][0, :, col:col + LANES]
            for par in range(2):
                keep = lo if par == 0 else jnp.logical_not(lo)
                acc = jnp.zeros((n, 2 * CMP_HID), F32)
                for s in range(CMP_STRIDE):
                    xs = slab_ref[pl.ds(s, n, stride=CMP_STRIDE), :]
                    xm = jnp.where(keep, xs, 0.0).astype(BF16)
                    acc = acc + jnp.dot(xm, w1_ref[c, s], preferred_element_type=F32)
                p0 = acc[:, :CMP_HID]
                p1 = acc[:, CMP_HID:]
                ci = 2 * (2 * pair + par) + c
                prev = jnp.where(row == 0, carry_ref[ci, 7:8, :], pltpu.roll(p0, 1, 0))
                carry_ref[ci] = p0[n - 8:n, :]
                hid = _gelu(prev + p1 + pb_ref[c:c + 1, :]).astype(BF16)
                res[par] = res[par] + jnp.dot(hid, w2_ref[c], preferred_element_type=F32)
        for par in range(2):
            r = res[par]
            ms = jnp.sum(jnp.where(lo, r * r, 0.0), axis=-1, keepdims=True) * (1.0 / HEAD_DIM)
            kn = r * lax.rsqrt(ms + EPS) * gain_ref[...]
            out_ref[0, 2 * pair + par] = jnp.where(lo, kn, r).astype(out_ref.dtype)


def _page_index(b, g, pt_ref, *, i, n_pg):
    return (pt_ref[b, g * n_pg + i], 0, 0)


def compress(pages_arr, table, w1dup, pb, w2p, gain):
    B, n_pages = table.shape
    n_pg = 16 if n_pages % 16 == 0 else n_pages
    n = n_pg * (PAGE // CMP_STRIDE)
    n_sub = n_pages * (PAGE // CMP_STRIDE)
    width = 2 * N_KV * HEAD_DIM
    in_specs = [pl.BlockSpec((1, PAGE, width), functools.partial(_page_index, i=i, n_pg=n_pg))
                for i in range(n_pg)]
    in_specs += [pl.BlockSpec(w1dup.shape, lambda b, g, pt: (0, 0, 0, 0)),
                 pl.BlockSpec(pb.shape, lambda b, g, pt: (0, 0)),
                 pl.BlockSpec(w2p.shape, lambda b, g, pt: (0, 0, 0)),
                 pl.BlockSpec((1, LANES), lambda b, g, pt: (0, 0))]
    grid_spec = pltpu.PrefetchScalarGridSpec(
        num_scalar_prefetch=1,
        grid=(B, n_pages // n_pg),
        in_specs=in_specs,
        out_specs=pl.BlockSpec((1, N_KV, n, LANES), lambda b, g, pt: (b, 0, g, 0)),
        scratch_shapes=[pltpu.VMEM((2 * N_KV, 8, CMP_HID), F32),
                        pltpu.VMEM((n_pg * PAGE, LANES), F32)],
    )
    return pl.pallas_call(
        functools.partial(_compress_kernel, n_pg=n_pg),
        out_shape=jax.ShapeDtypeStruct((B, N_KV, n_sub, LANES), BF16),
        grid_spec=grid_spec,
        compiler_params=_cparams(("parallel", "arbitrary")),
        name="compress",
    )(table, *([pages_arr] * n_pg), w1dup, pb, w2p, gain)


def _repack_kernel(pt_ref, p0, p1, p2, p3, new_ref, out_ref, *, n_full, t_new):
    t = pl.program_id(1)
    nk = N_KV * HEAD_DIM

    @pl.when(t < n_full)
    def _():
        for i, pg in enumerate((p0, p1, p2, p3)):
            x = pg[0]
            for p in range(2):
                even, odd = _pack_pair(x[:, LANES * p:LANES * (p + 1)],
                                       x[:, nk + LANES * p:nk + LANES * (p + 1)], out_ref.dtype)
                out_ref[0, 2 * p, PAGE * i:PAGE * (i + 1), :] = even
                out_ref[0, 2 * p + 1, PAGE * i:PAGE * (i + 1), :] = odd

    @pl.when(t == n_full)
    def _():
        pad = jnp.zeros((N_KV, KEY_TILE - t_new, LANES), F32)
        out_ref[0] = jnp.concatenate([new_ref[0].astype(F32), pad], axis=1).astype(out_ref.dtype)


def _repack_page_index(b, t, pt_ref, *, i, n_pages):
    return (pt_ref[b, jnp.minimum(4 * t + i, n_pages - 1)], 0, 0)


def repack(pages_arr, table, new_pack):
    B, n_pages = table.shape
    n_full = n_pages // 4
    t_new = new_pack.shape[2]
    width = 2 * N_KV * HEAD_DIM
    in_specs = [pl.BlockSpec((1, PAGE, width),
                             functools.partial(_repack_page_index, i=i, n_pages=n_pages))
                for i in range(4)]
    in_specs.append(pl.BlockSpec((1, N_KV, t_new, LANES), lambda b, t, pt: (b, 0, 0, 0)))
    grid_spec = pltpu.PrefetchScalarGridSpec(
        num_scalar_prefetch=1,
        grid=(B, n_full + 1),
        in_specs=in_specs,
        out_specs=pl.BlockSpec((1, N_KV, KEY_TILE, LANES), lambda b, t, pt: (b, 0, t, 0)),
    )
    return pl.pallas_call(
        functools.partial(_repack_kernel, n_full=n_full, t_new=t_new),
        out_shape=jax.ShapeDtypeStruct((B, N_KV, (n_full + 1) * KEY_TILE, LANES), BF16),
        grid_spec=grid_spec,
        compiler_params=_cparams(("parallel", "arbitrary")),
        name="repack",
    )(table, pages_arr, pages_arr, pages_arr, pages_arr, new_pack)


def _softmax_step(s, mask, kv, m_ref, l_ref, a_ref, k):
    s = jnp.where(mask, s, NEG_INF)
    m_old = m_ref[k]
    m_new = jnp.maximum(m_old, jnp.max(s, axis=-1, keepdims=True))
    alpha = jnp.exp(m_old - m_new)
    p = jnp.where(mask, jnp.exp(s - m_new), 0.0)
    l_ref[k] = alpha * l_ref[k] + jnp.sum(p, axis=-1, keepdims=True)
    a_ref[k] = alpha * a_ref[k] + jnp.dot(p.astype(BF16), kv, preferred_element_type=F32)
    m_ref[k] = m_new


def _nsa_kernel(qi_ref, kt_ref, wt_ref, wf_ref, last_ref,
                qc_ref, qr_ref, gate_ref, cmp_ref, sel_ref, win_ref, wmap_ref,
                o_ref,
                selm_ref, oc_ref, ms_ref, ls_ref, as_ref, mw_ref, lw_ref, aw_ref,
                *, qb, n_sel, n_selp, n_cmp, q0, w_off):
    step = pl.program_id(1)
    qi = qi_ref[step]
    kt = kt_ref[step]
    group = N_HEADS // N_KV
    R = group * qb
    rowq = lax.broadcasted_iota(jnp.int32, (R, 1), 0) & (qb - 1)
    qpos = q0 + qi * qb + rowq
    qpos_q = q0 + qi * qb + lax.broadcasted_iota(jnp.int32, (qb, 1), 0)

    @pl.when(kt == 0)
    def _first():
        for m_ref, l_ref, a_ref in ((ms_ref, ls_ref, as_ref), (mw_ref, lw_ref, aw_ref)):
            m_ref[...] = jnp.full(m_ref.shape, NEG_INF, F32)
            l_ref[...] = jnp.zeros(l_ref.shape, F32)
            a_ref[...] = jnp.zeros(a_ref.shape, F32)
        n_idx = lax.broadcasted_iota(jnp.int32, (1, n_cmp), 1)
        cvalid = (n_idx >= 1) & ((n_idx - 1) * CMP_STRIDE + CMP_LEN - 1 <= qpos)
        blk = lax.broadcasted_iota(jnp.int32, (qb, n_selp), 1)
        blk_f = blk.astype(F32)
        cur = qpos_q >> 6
        forced = (blk == 0) | (blk == cur) | (blk == cur - 1)
        reach = blk * SEL_BLOCK <= qpos_q
        real = blk < n_sel
        for k in range(N_KV):
            ckv = cmp_ref[0, k]
            s = _nt_dot(qc_ref[0, k, 0].astype(BF16), ckv)
            s = jnp.where(cvalid, s, NEG_INF)
            m = jnp.max(s, axis=-1, keepdims=True)
            e = jnp.where(cvalid, jnp.exp(s - m), 0.0)
            p = e / jnp.maximum(jnp.sum(e, axis=-1, keepdims=True), TINY)
            oc_ref[k] = jnp.dot(p.astype(BF16), ckv, preferred_element_type=F32)
            psum = p[0:qb]
            for gi in range(1, group):
                psum = psum + p[gi * qb:(gi + 1) * qb]
            hi = psum.astype(BF16)
            lo = (psum - hi.astype(F32)).astype(BF16)
            imp = _nt_dot(hi, wmap_ref[...]) + _nt_dot(lo, wmap_ref[...])
            v = jnp.where(forced, SEL_FORCE, jnp.where(reach, imp, SEL_NEG))
            v = jnp.where(real, v, -jnp.inf)

            def pick_one(_, carry):
                v, sel = carry
                m = jnp.max(v, axis=-1, keepdims=True)
                first = jnp.min(jnp.where(v == m, blk_f, float(n_selp)), axis=-1, keepdims=True)
                pick = blk_f == first
                sel = jnp.where(pick & (m > 0.5 * SEL_NEG), 1.0, sel)
                return jnp.where(pick, -jnp.inf, v), sel

            _, sel = lax.fori_loop(0, SEL_TOPK, pick_one, (v, jnp.zeros((qb, n_selp), F32)))
            selm_ref[k] = sel

    tok = kt * KEY_TILE + lax.broadcasted_iota(jnp.int32, (1, KEY_TILE), 1)
    blk_of_tok = tok >> 6
    expand = jnp.where(
        lax.broadcasted_iota(jnp.int32, (n_selp, KEY_TILE), 0) == blk_of_tok, 1.0, 0.0).astype(BF16)
    causal = tok <= qpos
    for k in range(N_KV):
        kv = sel_ref[0, k]
        s = _nt_dot(qr_ref[0, k, 0].astype(BF16), kv)
        sm = jnp.dot(selm_ref[k].astype(BF16), expand, preferred_element_type=F32)
        sm = jnp.concatenate([sm] * group, axis=0)
        _softmax_step(s, causal & (sm > 0.5), kv, ms_ref, ls_ref, as_ref, k)

    @pl.when(wf_ref[step] == 1)
    def _window():
        wpos = w_off + wt_ref[step] * KEY_TILE + lax.broadcasted_iota(jnp.int32, (1, KEY_TILE), 1)
        mask = (wpos <= qpos) & (wpos > qpos - WINDOW) & (wpos >= 0)
        for k in range(N_KV):
            kv = win_ref[0, k]
            s = _nt_dot(qr_ref[0, k, 0].astype(BF16), kv)
            _softmax_step(s, mask, kv, mw_ref, lw_ref, aw_ref, k)

    @pl.when(last_ref[step] == 1)
    def _finish():
        gate = gate_ref[0]
        for k in range(N_KV):
            for gi in range(group):
                hd = k * group + gi
                rs = slice(gi * qb, (gi + 1) * qb)
                o_s = as_ref[k, rs, :] / jnp.maximum(ls_ref[k, rs, :], TINY)
                o_w = aw_ref[k, rs, :] / jnp.maximum(lw_ref[k, rs, :], TINY)
                o = gate[:, 3 * hd:3 * hd + 1] * oc_ref[k, rs, :]
                o = o + gate[:, 3 * hd + 1:3 * hd + 2] * o_s
                o = o + gate[:, 3 * hd + 2:3 * hd + 3] * o_w
                o_ref[0, :, LANES * hd:LANES * (hd + 1)] = o.astype(o_ref.dtype)


def _nsa_tables(T, qb, q0, w_off):
    rows = []
    for qi in range(T // qb):
        q_lo = q0 + qi * qb
        q_hi = q_lo + qb - 1
        last_kt = q_hi // KEY_TILE
        w_lo = max(q_lo - WINDOW + 1, w_off)
        wt0 = (w_lo - w_off) // KEY_TILE
        wt1 = (q_hi - w_off) // KEY_TILE
        n_w = wt1 - wt0 + 1
        assert n_w <= last_kt + 1
        for kt in range(last_kt + 1):
            rows.append((qi, kt, wt0 + min(kt, n_w - 1), int(kt < n_w), int(kt == last_kt)))
    tab = np.asarray(rows, np.int32).T
    return [jnp.asarray(tab[i]) for i in range(5)]


def _overlap_map(n_cmp_rows, n_selp):
    m = np.arange(n_cmp_rows)[None, :]
    s = np.arange(n_selp)[:, None]
    c0 = (m - 1) * CMP_STRIDE
    ov = np.minimum(c0 + CMP_LEN, s * SEL_BLOCK + SEL_BLOCK) - np.maximum(c0, s * SEL_BLOCK)
    w = np.maximum(ov, 0).astype(np.float32) / CMP_LEN
    w[:, 0] = 0.0
    return jnp.asarray(w, BF16)


def nsa_attend(qc, qr, gates, cmp_p, sel_p, win_p, T, qb, q0, w_off, n_sel):
    B = qc.shape[0]
    group = N_HEADS // N_KV
    R = group * qb
    n_cmp = cmp_p.shape[2]
    n_selp = -(-n_sel // LANES) * LANES
    tabs = _nsa_tables(T, qb, q0, w_off)
    n_steps = int(tabs[0].shape[0])
    wmap = _overlap_map(n_cmp, n_selp)
    odt = BF16 if qb % 16 == 0 else F32
    kern = functools.partial(_nsa_kernel, qb=qb, n_sel=n_sel, n_selp=n_selp, n_cmp=n_cmp,
                             q0=q0, w_off=w_off)
    qspec = pl.BlockSpec((1, N_KV, 1, R, LANES), lambda b, s, qi, kt, wt, wf, la: (b, 0, qi[s], 0, 0))
    grid_spec = pltpu.PrefetchScalarGridSpec(
        num_scalar_prefetch=5,
        grid=(B, n_steps),
        in_specs=[
            qspec, qspec,
            pl.BlockSpec((1, qb, LANES), lambda b, s, qi, kt, wt, wf, la: (b, qi[s], 0)),
            pl.BlockSpec((1, N_KV, n_cmp, LANES), lambda b, s, qi, kt, wt, wf, la: (b, 0, 0, 0)),
            pl.BlockSpec((1, N_KV, KEY_TILE, LANES), lambda b, s, qi, kt, wt, wf, la: (b, 0, kt[s], 0)),
            pl.BlockSpec((1, N_KV, KEY_TILE, LANES), lambda b, s, qi, kt, wt, wf, la: (b, 0, wt[s], 0)),
            pl.BlockSpec((n_selp, n_cmp), lambda b, s, qi, kt, wt, wf, la: (0, 0)),
        ],
        out_specs=pl.BlockSpec((1, qb, N_HEADS * LANES), lambda b, s, qi, kt, wt, wf, la: (b, qi[s], 0)),
        scratch_shapes=[
            pltpu.VMEM((N_KV, qb, n_selp), F32),
            pltpu.VMEM((N_KV, R, LANES), F32),
            pltpu.VMEM((N_KV, R, 1), F32), pltpu.VMEM((N_KV, R, 1), F32), pltpu.VMEM((N_KV, R, LANES), F32),
            pltpu.VMEM((N_KV, R, 1), F32), pltpu.VMEM((N_KV, R, 1), F32), pltpu.VMEM((N_KV, R, LANES), F32),
        ],
    )
    return pl.pallas_call(
        kern,
        out_shape=jax.ShapeDtypeStruct((B, T, N_HEADS * LANES), odt),
        grid_spec=grid_spec,
        compiler_params=_cparams(("parallel", "arbitrary")),
        name="nsa_attend",
    )(*tabs, qc, qr, gates, cmp_p, sel_p, win_p, wmap)


def _qgt_kernel(x_ref, g_ref, w_ref, gain_ref, cos_ref, sin_ref, bd_ref,
                qc_ref, qr_ref, gate_ref, *, qb, n_qb):
    h = _rms_rows(x_ref[0], g_ref[...]).astype(BF16)
    y = jnp.dot(h, w_ref[...], preferred_element_type=F32)
    cos = cos_ref[...]
    sin = sin_ref[...]
    bd = bd_ref[...]
    nq = N_HEADS * HEAD_DIM
    group = N_HEADS // N_KV
    gate_t = jax.nn.sigmoid(y[:, nq:nq + LANES]).T
    for j in range(n_qb):
        gate_ref[0, j] = gate_t[:, j * qb:(j + 1) * qb]
    pad = jnp.zeros((N_KV, n_qb, HEAD_DIM, group * qb), qc_ref.dtype)
    qc_ref[0, :, :, HEAD_DIM:, :] = pad
    qr_ref[0, :, :, HEAD_DIM:, :] = pad
    for p in range(N_HEADS // 2):
        qx = y[:, LANES * p:LANES * (p + 1)]
        qn = qx * lax.rsqrt(_head_ms(qx, bd) + EPS) * gain_ref[...]
        qr = _rope64(qn, cos, sin)
        for src, dst in ((qn, qc_ref), (qr, qr_ref)):
            st = src.T.astype(dst.dtype)
            for par in range(2):
                kvh, gi = divmod(2 * p + par, group)
                for j in range(n_qb):
                    dst[0, kvh, j, 0:HEAD_DIM, gi * qb:(gi + 1) * qb] = (
                        st[par * HEAD_DIM:(par + 1) * HEAD_DIM, j * qb:(j + 1) * qb])


def qg_project_t(x, g, w, gain, cos, sin, bd, tm, qb):
    B, T, D = x.shape
    N = w.shape[1]
    n_qb = tm // qb
    group = N_HEADS // N_KV
    kern = functools.partial(_qgt_kernel, qb=qb, n_qb=n_qb)
    qshape = jax.ShapeDtypeStruct((B, N_KV, T // qb, LANES, group * qb), BF16)
    qspec = pl.BlockSpec((1, N_KV, n_qb, LANES, group * qb), lambda b, t: (b, 0, t, 0, 0))
    return pl.pallas_call(
        kern,
        out_shape=[qshape, qshape, jax.ShapeDtypeStruct((B, T // qb, LANES, qb), F32)],
        grid=(B, T // tm),
        in_specs=[pl.BlockSpec((1, tm, D), lambda b, t: (b, t, 0)),
                  pl.BlockSpec((1, D), lambda b, t: (0, 0)),
                  pl.BlockSpec((D, N), lambda b, t: (0, 0)),
                  pl.BlockSpec((1, LANES), lambda b, t: (0, 0)),
                  pl.BlockSpec((tm, LANES), lambda b, t: (t, 0)),
                  pl.BlockSpec((tm, LANES), lambda b, t: (t, 0)),
                  pl.BlockSpec((LANES, LANES), lambda b, t: (0, 0))],
        out_specs=[qspec, qspec, pl.BlockSpec((1, n_qb, LANES, qb), lambda b, t: (b, t, 0, 0))],
        compiler_params=_cparams(("parallel", "parallel")),
        name="qg_project_t",
    )(x, g.reshape(1, D), w, gain, cos, sin, bd)


def _nsa_t_kernel(qc_ref, qr_ref, gate_ref, cmp_ref, sel_ref, win_ref, wmap_ref, o_ref,
                  selneg_ref, m_ref, l_ref, acc_ref,
                  *, qb, n_sel, n_selp, n_cmp, q0, w_off, w_rows, l_win):
    qi = pl.program_id(1)
    group = N_HEADS // N_KV
    R = group * qb
    blocks_per_tile = KEY_TILE // SEL_BLOCK
    q_lo = q0 + qi * qb
    qpos_q = q_lo + lax.broadcasted_iota(jnp.int32, (1, qb), 1)
    n_kt = (q_lo + qb - 1) // KEY_TILE + 1
    w_start = pl.multiple_of(jnp.clip(q_lo - WINDOW - w_off, 0, l_win - w_rows), LANES)
    gate = gate_ref[0, 0]

    def lanes4(a):
        return jnp.concatenate([a] * group, axis=1)

    m_idx = lax.broadcasted_iota(jnp.int32, (n_cmp, qb), 0)
    cvalid = (m_idx >= 1) & ((m_idx - 1) * CMP_STRIDE + CMP_LEN - 1 <= qpos_q)
    cbias = lanes4(jnp.where(cvalid, 0.0, NEG_INF))
    any_c = lanes4(qpos_q >= CMP_LEN - 1)
    blk = lax.broadcasted_iota(jnp.int32, (n_selp, qb), 0)
    blk_f = blk.astype(F32)
    cur = qpos_q >> 6
    forced = (blk == 0) | (blk == cur) | (blk == cur - 1)
    reach = blk * SEL_BLOCK <= qpos_q
    real = blk < n_sel
    wpos = w_off + w_start + lax.broadcasted_iota(jnp.int32, (w_rows, qb), 0)
    wbias = lanes4(jnp.where((wpos <= qpos_q) & (wpos > qpos_q - WINDOW) & (wpos >= 0), 0.0, NEG_INF))
    row_t = lax.broadcasted_iota(jnp.int32, (KEY_TILE, qb), 0)

    for k in range(N_KV):
        ckv = cmp_ref[0, k]
        s = jnp.dot(ckv, qc_ref[0, k, 0], preferred_element_type=F32) + cbias
        e = jnp.exp2(s - jnp.max(s, axis=0, keepdims=True))
        den = jnp.maximum(jnp.sum(e, axis=0, keepdims=True), TINY)
        p = e * jnp.where(any_c, 1.0 / den, 0.0)
        oc = _tn_dot(ckv, p.astype(BF16))
        psum = p[:, 0:qb]
        for gi in range(1, group):
            psum = psum + p[:, gi * qb:(gi + 1) * qb]
        hi = psum.astype(BF16)
        lo = (psum - hi.astype(F32)).astype(BF16)
        imp = (jnp.dot(wmap_ref[...], hi, preferred_element_type=F32)
               + jnp.dot(wmap_ref[...], lo, preferred_element_type=F32))
        v = jnp.where(forced, SEL_FORCE, jnp.where(reach, imp, SEL_NEG))
        v = jnp.where(real, v, -jnp.inf)

        def pick_one(_, carry):
            v, sel = carry
            m = jnp.max(v, axis=0, keepdims=True)
            first = jnp.min(jnp.where(v == m, blk_f, float(n_selp)), axis=0, keepdims=True)
            pick = blk_f == first
            sel = jnp.where(pick & (m > 0.5 * SEL_NEG), 0.0, sel)
            return jnp.where(pick, -jnp.inf, v), sel

        _, sel = lax.fori_loop(0, SEL_TOPK, pick_one, (v, jnp.full((n_selp, qb), NEG_INF, F32)))
        selneg_ref[...] = sel

        qr = qr_ref[0, k, 0]
        m_ref[...] = jnp.full(m_ref.shape, NEG_INF, F32)
        l_ref[...] = jnp.zeros(l_ref.shape, F32)
        acc_ref[...] = jnp.zeros(acc_ref.shape, F32)

        def tile(kt, carry):
            start = pl.multiple_of(kt * KEY_TILE, KEY_TILE)
            kv = sel_ref[0, k, pl.ds(start, KEY_TILE), :]
            s = jnp.dot(kv, qr, preferred_element_type=F32)
            pieces = [jnp.broadcast_to(selneg_ref[pl.ds(kt * blocks_per_tile + j, 1), :], (SEL_BLOCK, qb))
                      for j in range(blocks_per_tile)]
            bias = jnp.concatenate(pieces, axis=0) + jnp.where(start + row_t <= qpos_q, 0.0, NEG_INF)
            s = s + lanes4(bias)
            m_old = m_ref[...]
            m_new = jnp.maximum(m_old, jnp.max(s, axis=0, keepdims=True))
            alpha = jnp.exp2(m_old - m_new)
            p = jnp.exp2(s - m_new)
            l_ref[...] = alpha * l_ref[...] + jnp.sum(p, axis=0, keepdims=True)
            acc_ref[...] = alpha * acc_ref[...] + _tn_dot(kv, p.astype(BF16))
            m_ref[...] = m_new
            return carry

        lax.fori_loop(0, n_kt, tile, 0)
        o_s = acc_ref[...] * (1.0 / jnp.maximum(l_ref[...], TINY))

        wkv = win_ref[0, k, pl.ds(w_start, w_rows), :]
        s = jnp.dot(wkv, qr, preferred_element_type=F32) + wbias
        e = jnp.exp2(s - jnp.max(s, axis=0, keepdims=True))
        den = jnp.maximum(jnp.sum(e, axis=0, keepdims=True), TINY)
        o_w = _tn_dot(wkv, e.astype(BF16)) * (1.0 / den)

        for gi in range(group):
            hd = k * group + gi
            sl = slice(gi * qb, (gi + 1) * qb)
            o = gate[3 * hd:3 * hd + 1, :] * oc[:, sl]
            o = o + gate[3 * hd + 1:3 * hd + 2, :] * o_s[:, sl]
            o = o + gate[3 * hd + 2:3 * hd + 3, :] * o_w[:, sl]
            o_ref[0, :, LANES * hd:LANES * (hd + 1)] = o.T.astype(o_ref.dtype)


def nsa_attend_t(qc, qr, gates, cmp_p, sel_p, win_p, T, qb, q0, w_off, n_sel):
    B = qc.shape[0]
    group = N_HEADS // N_KV
    R = group * qb
    n_cmp = cmp_p.shape[2]
    n_selp = -(-n_sel // LANES) * LANES
    l_sel = sel_p.shape[2]
    l_win = win_p.shape[2]
    w_rows = WINDOW + max(qb, LANES)
    assert l_win >= w_rows and l_sel >= ((q0 + T - 1) // KEY_TILE + 1) * KEY_TILE
    wmap = _overlap_map(n_cmp, n_selp)
    kern = functools.partial(_nsa_t_kernel, qb=qb, n_sel=n_sel, n_selp=n_selp, n_cmp=n_cmp,
                             q0=q0, w_off=w_off, w_rows=w_rows, l_win=l_win)
    qspec = pl.BlockSpec((1, N_KV, 1, LANES, R), lambda b, i: (b, 0, i, 0, 0))
    resident = dict(pipeline_mode=pl.Buffered(1))
    return pl.pallas_call(
        kern,
        out_shape=jax.ShapeDtypeStruct((B, T, N_HEADS * LANES), BF16),
        grid=(B, T // qb),
        in_specs=[
            qspec, qspec,
            pl.BlockSpec((1, 1, LANES, qb), lambda b, i: (b, i, 0, 0)),
            pl.BlockSpec((1, N_KV, n_cmp, LANES), lambda b, i: (b, 0, 0, 0)),
            pl.BlockSpec((1, N_KV, l_sel, LANES), lambda b, i: (b, 0, 0, 0), **resident),
            pl.BlockSpec((1, N_KV, l_win, LANES), lambda b, i: (b, 0, 0, 0), **resident),
            pl.BlockSpec((n_selp, n_cmp), lambda b, i: (0, 0)),
        ],
        out_specs=pl.BlockSpec((1, qb, N_HEADS * LANES), lambda b, i: (b, i, 0)),
        scratch_shapes=[
            pltpu.VMEM((n_selp, qb), F32),
            pltpu.VMEM((1, R), F32), pltpu.VMEM((1, R), F32), pltpu.VMEM((LANES, R), F32),
        ],
        compiler_params=_cparams(("parallel", "arbitrary")),
        name="nsa_attend_t",
    )(qc, qr, gates, cmp_p, sel_p, win_p, wmap)


def _nsa_fast_kernel(shift_ref, qc_ref, qr_ref, gate_ref, cmp_ref, sel_ref, win_ref, wmap_ref, hot_ref,
                     o_ref, qaug_ref, oc_ref, l_ref, acc_ref,
                     *, qb, n_sel, n_selp, n_cmp, q0, w_off, w_rows, l_win):
    qi = pl.program_id(1)
    group = N_HEADS // N_KV
    R = group * qb
    q_lo = q0 + qi * qb
    qpos_q = q_lo + lax.broadcasted_iota(jnp.int32, (1, qb), 1)
    n_kt = (q_lo + qb - 1) // KEY_TILE + 1
    w_start = pl.multiple_of(jnp.clip(q_lo - WINDOW - w_off, 0, l_win - w_rows), LANES)
    shift_c = shift_ref[0]
    shift_s = shift_ref[1]
    shift_w = shift_ref[2]

    def lanes4(a):
        return jnp.concatenate([a] * group, axis=1)

    m_idx = lax.broadcasted_iota(jnp.int32, (n_cmp, qb), 0)
    cvalid = (m_idx >= 1) & ((m_idx - 1) * CMP_STRIDE + CMP_LEN - 1 <= qpos_q)
    cbias = lanes4(jnp.where(cvalid, -shift_c, NEG_INF))
    any_c = lanes4(qpos_q >= CMP_LEN - 1)
    blk = lax.broadcasted_iota(jnp.int32, (n_selp, qb), 0)
    blk_f = blk.astype(F32)
    cur = qpos_q >> 6
    forced = (blk == 0) | (blk == cur) | (blk == cur - 1)
    reach = blk * SEL_BLOCK <= qpos_q
    real = blk < n_sel

    imps = []
    for k in range(N_KV):
        ckv = cmp_ref[0, k]
        e = jnp.exp2(jnp.dot(ckv, qc_ref[0, k, 0], preferred_element_type=F32) + cbias)
        den = jnp.maximum(jnp.sum(e, axis=0, keepdims=True), TINY)
        p = e * jnp.where(any_c, 1.0 / den, 0.0)
        oc_ref[k] = _tn_dot(ckv, p.astype(BF16))
        psum = p[:, 0:qb]
        for gi in range(1, group):
            psum = psum + p[:, gi * qb:(gi + 1) * qb]
        hi = psum.astype(BF16)
        lo = (psum - hi.astype(F32)).astype(BF16)
        imp = (jnp.dot(wmap_ref[...], hi, preferred_element_type=F32)
               + jnp.dot(wmap_ref[...], lo, preferred_element_type=F32))
        v = jnp.where(forced, SEL_FORCE, jnp.where(reach, imp, SEL_NEG))
        imps.append(jnp.where(real, v, -jnp.inf))

    def pick_one(_, carry):
        out = []
        for v, sel in carry:
            m = jnp.max(v, axis=0, keepdims=True)
            first = jnp.min(jnp.where(v == m, blk_f, float(n_selp)), axis=0, keepdims=True)
            pick = blk_f == first
            sel = jnp.where(pick & (m > 0.5 * SEL_NEG), -shift_s, sel)
            out.append((jnp.where(pick, -jnp.inf, v), sel))
        return tuple(out)

    none = jnp.full((n_selp, qb), NEG_INF, F32)
    picked = lax.fori_loop(0, SEL_TOPK, pick_one, tuple((v, none) for v in imps))
    for k in range(N_KV):
        qaug_ref[k, 0:LANES, :] = qr_ref[0, k, 0]
        qaug_ref[k, LANES:, :] = lanes4(picked[k][1]).astype(BF16)

    l_ref[...] = jnp.zeros(l_ref.shape, F32)
    acc_ref[...] = jnp.zeros(acc_ref.shape, F32)

    def tile(start, causal_bias):
        hot = hot_ref[pl.ds(start, KEY_TILE), :]
        for k in range(N_KV):
            kv = sel_ref[0, k, pl.ds(start, KEY_TILE), :]
            s = jnp.dot(jnp.concatenate([kv, hot], axis=1), qaug_ref[k], preferred_element_type=F32)
            if causal_bias is not None:
                s = s + causal_bias
            p = jnp.exp2(s)
            l_ref[k] += jnp.sum(p.reshape(KEY_TILE // 8, 8, R), axis=0)
            acc_ref[k] += _tn_dot(kv, p.astype(BF16))

    def full_tile(kt, carry):
        tile(pl.multiple_of(kt * KEY_TILE, KEY_TILE), None)
        return carry

    lax.fori_loop(0, n_kt - 1, full_tile, 0)
    d_start = pl.multiple_of((n_kt - 1) * KEY_TILE, KEY_TILE)
    row_t = lax.broadcasted_iota(jnp.int32, (KEY_TILE, qb), 0)
    tile(d_start, lanes4(jnp.where(d_start + row_t <= qpos_q, 0.0, NEG_INF)))

    wpos = w_off + w_start + lax.broadcasted_iota(jnp.int32, (w_rows, qb), 0)
    wvalid = (wpos <= qpos_q) & (wpos > qpos_q - WINDOW) & (wpos >= 0)
    wbias = lanes4(jnp.where(wvalid, -shift_w, NEG_INF))
    gate = gate_ref[0, 0]
    for k in range(N_KV):
        wkv = win_ref[0, k, pl.ds(w_start, w_rows), :]
        e = jnp.exp2(jnp.dot(wkv, qr_ref[0, k, 0], preferred_element_type=F32) + wbias)
        den = jnp.maximum(jnp.sum(e, axis=0, keepdims=True), TINY)
        o_w = _tn_dot(wkv, e.astype(BF16)) * (1.0 / den)
        l_s = jnp.maximum(jnp.sum(l_ref[k], axis=0, keepdims=True), TINY)
        o_s = acc_ref[k] * (1.0 / l_s)
        oc = oc_ref[k]
        for gi in range(group):
            hd = k * group + gi
            sl = slice(gi * qb, (gi + 1) * qb)
            o = gate[3 * hd:3 * hd + 1, :] * oc[:, sl]
            o = o + gate[3 * hd + 1:3 * hd + 2, :] * o_s[:, sl]
            o = o + gate[3 * hd + 2:3 * hd + 3, :] * o_w[:, sl]
            o_ref[0, :, LANES * hd:LANES * (hd + 1)] = o.T.astype(o_ref.dtype)


def nsa_attend_fast(shifts, qc, qr, gates, cmp_p, sel_p, win_p, T, qb, q0, w_off, n_sel):
    B = qc.shape[0]
    group = N_HEADS // N_KV
    R = group * qb
    n_cmp = cmp_p.shape[2]
    n_selp = -(-n_sel // LANES) * LANES
    l_sel = sel_p.shape[2]
    l_win = win_p.shape[2]
    w_rows = WINDOW + max(qb, LANES)
    assert l_win >= w_rows and l_sel >= ((q0 + T - 1) // KEY_TILE + 1) * KEY_TILE
    wmap = _overlap_map(n_cmp, n_selp)
    hot = np.zeros((l_sel, n_selp), np.float32)
    hot[np.arange(l_sel), np.arange(l_sel) // SEL_BLOCK] = 1.0
    hot = jnp.asarray(hot, BF16)
    kern = functools.partial(_nsa_fast_kernel, qb=qb, n_sel=n_sel, n_selp=n_selp, n_cmp=n_cmp,
                             q0=q0, w_off=w_off, w_rows=w_rows, l_win=l_win)
    qspec = pl.BlockSpec((1, N_KV, 1, LANES, R), lambda b, i, sh: (b, 0, i, 0, 0))
    resident = dict(pipeline_mode=pl.Buffered(1))
    grid_spec = pltpu.PrefetchScalarGridSpec(
        num_scalar_prefetch=1,
        grid=(B, T // qb),
        in_specs=[
            qspec, qspec,
            pl.BlockSpec((1, 1, LANES, qb), lambda b, i, sh: (b, i, 0, 0)),
            pl.BlockSpec((1, N_KV, n_cmp, LANES), lambda b, i, sh: (b, 0, 0, 0)),
            pl.BlockSpec((1, N_KV, l_sel, LANES), lambda b, i, sh: (b, 0, 0, 0), **resident),
            pl.BlockSpec((1, N_KV, l_win, LANES), lambda b, i, sh: (b, 0, 0, 0), **resident),
            pl.BlockSpec((n_selp, n_cmp), lambda b, i, sh: (0, 0)),
            pl.BlockSpec((l_sel, n_selp), lambda b, i, sh: (0, 0), **resident),
        ],
        out_specs=pl.BlockSpec((1, qb, N_HEADS * LANES), lambda b, i, sh: (b, i, 0)),
        scratch_shapes=[
            pltpu.VMEM((N_KV, 2 * LANES, R), BF16),
            pltpu.VMEM((N_KV, LANES, R), F32),
            pltpu.VMEM((N_KV, 8, R), F32),
            pltpu.VMEM((N_KV, LANES, R), F32),
        ],
    )
    return pl.pallas_call(
        kern,
        out_shape=jax.ShapeDtypeStruct((B, T, N_HEADS * LANES), BF16),
        grid_spec=grid_spec,
        compiler_params=_cparams(("parallel", "arbitrary")),
        name="nsa_attend_fast",
    )(shifts, qc, qr, gates, cmp_p, sel_p, win_p, wmap, hot)


def _nsa_dec_kernel(shift_ref, pt_ref, *refs, n_pg, qb, n_sel, n_selp, n_cmp, q0, w_off):
    pages = refs[:n_pg]
    (qcp_ref, qbd_ref, gate_ref, cmp_ref, new_ref, win_ref, wmap_ref, fold_ref, hot_ref,
     o_ref, qaug_ref, oc_ref, l_ref, acc_ref) = refs[n_pg:]
    g = pl.program_id(1)
    nk = N_KV * HEAD_DIM
    t_new = new_ref.shape[1]
    lane = lax.broadcasted_iota(jnp.int32, (1, LANES), 1)
    qpos = q0 + (lane & (qb - 1))
    shift_c = shift_ref[0]
    shift_s = shift_ref[1]
    shift_w = shift_ref[2]

    @pl.when(g == 0)
    def _first():
        s = jnp.dot(cmp_ref[0, 0], qcp_ref[0, 0], preferred_element_type=F32)
        for k in range(1, N_KV):
            s = s + jnp.dot(cmp_ref[0, k], qcp_ref[0, k], preferred_element_type=F32)
        m_idx = lax.broadcasted_iota(jnp.int32, (n_cmp, LANES), 0)
        cvalid = (m_idx >= 1) & ((m_idx - 1) * CMP_STRIDE + CMP_LEN - 1 <= qpos)
        e = jnp.exp2(s + jnp.where(cvalid, -shift_c, NEG_INF))
        den = jnp.maximum(jnp.sum(e, axis=0, keepdims=True), TINY)
        p = e * jnp.where(qpos >= CMP_LEN - 1, 1.0 / den, 0.0)
        pb = p.astype(BF16)
        for k in range(N_KV):
            oc_ref[k * HEAD_DIM:(k + 1) * HEAD_DIM, :] = _tn_dot(cmp_ref[0, k], pb)[HEAD_DIM:, :]
        fold = fold_ref[...]
        p_lo = (p - pb.astype(F32)).astype(BF16)
        psum = jnp.dot(pb, fold, preferred_element_type=F32) + jnp.dot(p_lo, fold, preferred_element_type=F32)
        hi = psum.astype(BF16)
        lo = (psum - hi.astype(F32)).astype(BF16)
        imp = (jnp.dot(wmap_ref[...], hi, preferred_element_type=F32)
               + jnp.dot(wmap_ref[...], lo, preferred_element_type=F32))
        blk = lax.broadcasted_iota(jnp.int32, (n_selp, LANES), 0)
        blk_f = blk.astype(F32)
        cur = qpos >> 6
        forced = (blk == 0) | (blk == cur) | (blk == cur - 1)
        v = jnp.where(forced, SEL_FORCE, jnp.where(blk * SEL_BLOCK <= qpos, imp, SEL_NEG))
        v = jnp.where(blk < n_sel, v, -jnp.inf)

        def pick_one(_, carry):
            v, sel = carry
            m = jnp.max(v, axis=0, keepdims=True)
            first = jnp.min(jnp.where(v == m, blk_f, float(n_selp)), axis=0, keepdims=True)
            pick = blk_f == first
            sel = jnp.where(pick & (m > 0.5 * SEL_NEG), -shift_s, sel)
            return jnp.where(pick, -jnp.inf, v), sel

        _, sel = lax.fori_loop(0, SEL_TOPK, pick_one, (v, jnp.full((n_selp, LANES), NEG_INF, F32)))
        qaug_ref[0:nk, :] = qbd_ref[0]
        qaug_ref[nk:, :] = sel.astype(BF16)
        l_ref[...] = jnp.zeros(l_ref.shape, F32)
        acc_ref[...] = jnp.zeros(acc_ref.shape, F32)

    def attend(kx, vx, hot, bias):
        s = jnp.dot(jnp.concatenate([kx, hot], axis=1), qaug_ref[...], preferred_element_type=F32)
        if bias is not None:
            s = s + bias
        p = jnp.exp2(s)
        l_ref[...] += jnp.sum(p.reshape(p.shape[0] // 8, 8, LANES), axis=0)
        acc_ref[...] += _tn_dot(vx, p.astype(BF16))

    per_tile = KEY_TILE // PAGE
    for t in range(n_pg // per_tile):
        x = jnp.concatenate([pages[per_tile * t + i][0] for i in range(per_tile)], axis=0).astype(BF16)
        start = pl.multiple_of((g * (n_pg // per_tile) + t) * KEY_TILE, KEY_TILE)
        attend(x[:, :nk], x[:, nk:], hot_ref[pl.ds(start, KEY_TILE), :], None)

    @pl.when(g == pl.num_programs(1) - 1)
    def _last():
        pad = jnp.zeros((HALO - t_new, 2 * nk), F32)
        row = lax.broadcasted_iota(jnp.int32, (HALO, LANES), 0)
        new_ok = (row < t_new) & (q0 + row <= qpos)
        xn = jnp.concatenate([new_ref[0, :, 2 * nk:4 * nk], pad], axis=0).astype(BF16)
        hot_new = jnp.where(
            lax.broadcasted_iota(jnp.int32, (HALO, n_selp), 1) == q0 // SEL_BLOCK, 1.0, 0.0).astype(BF16)
        attend(xn[:, :nk], xn[:, nk:], hot_new, jnp.where(new_ok, 0.0, NEG_INF))

        qbd = qbd_ref[0]
        xw = win_ref[0].astype(BF16)
        wrow = lax.broadcasted_iota(jnp.int32, (xw.shape[0], LANES), 0)
        wpos = w_off + wrow
        wvalid = (wpos <= qpos) & (wpos > qpos - WINDOW) & (wpos >= 0)
        e_c = jnp.exp2(jnp.dot(xw[:, :nk], qbd, preferred_element_type=F32)
                       + jnp.where(wvalid, -shift_w, NEG_INF))
        xwn = jnp.concatenate([new_ref[0, :, 4 * nk:6 * nk], pad], axis=0).astype(BF16)
        npos = q0 + row
        nvalid = (row < t_new) & (npos <= qpos) & (npos > qpos - WINDOW)
        e_n = jnp.exp2(jnp.dot(xwn[:, :nk], qbd, preferred_element_type=F32)
                       + jnp.where(nvalid, -shift_w, NEG_INF))
        den = jnp.sum(e_c, axis=0, keepdims=True) + jnp.sum(e_n, axis=0, keepdims=True)
        o_w = _tn_dot(xw[:, nk:], e_c.astype(BF16)) + _tn_dot(xwn[:, nk:], e_n.astype(BF16))
        o_w = o_w * (1.0 / jnp.maximum(den, TINY))
        l_s = jnp.maximum(jnp.sum(l_ref[...], axis=0, keepdims=True), TINY)
        o_s = acc_ref[...] * (1.0 / l_s)
        gate = gate_ref[0]
        o_ref[0] = gate[0:1, :] * oc_ref[...] + gate[1:2, :] * o_s + gate[2:3, :] * o_w


def _dec_page_index(b, g, sh, pt, *, i, n_pg):
    return (pt[b, g * n_pg + i], 0, 0)


def nsa_attend_dec(shifts, table, qc, qr, gates, cmp_p, sel_pages, kvp, cache_win, qb, q0, n_sel):
    B, n_pages = table.shape
    group = N_HEADS // N_KV
    assert N_KV * group * qb == LANES
    nk = N_KV * HEAD_DIM
    n_pg = 16 if n_pages % 16 == 0 else n_pages
    n_cmp = cmp_p.shape[2]
    n_selp = -(-n_sel // LANES) * LANES
    wl = cache_win.shape[1]
    eye = jnp.eye(N_KV, dtype=F32)

    def spread(q):
        qt = q[:, :, 0, :, :HEAD_DIM].astype(F32).transpose(0, 1, 3, 2)
        return jnp.einsum("bkdr,kj->bkdjr", qt, eye).reshape(B, N_KV, HEAD_DIM, LANES)

    qc_pad = jnp.pad(spread(qc), ((0, 0), (0, 0), (0, LANES - HEAD_DIM), (0, 0))).astype(BF16)
    q_bd = spread(qr).reshape(B, nk, LANES).astype(BF16)
    gate_l = gates[:, :, :3 * N_HEADS].reshape(B, qb, N_KV, group, 3).transpose(0, 4, 2, 3, 1)
    gate_l = gate_l.reshape(B, 3, LANES)
    wmap = _overlap_map(n_cmp, n_selp)
    lane = np.arange(LANES)
    fold = jnp.asarray((lane[:, None] // (group * qb) == lane[None, :] // (group * qb))
                       & (lane[:, None] % qb == lane[None, :] % qb), BF16)
    l_past = n_pages * PAGE
    hot = np.zeros((l_past, n_selp), np.float32)
    hot[np.arange(l_past), np.arange(l_past) // SEL_BLOCK] = 1.0
    hot = jnp.asarray(hot, BF16)
    kern = functools.partial(_nsa_dec_kernel, n_pg=n_pg, qb=qb, n_sel=n_sel, n_selp=n_selp,
                             n_cmp=n_cmp, q0=q0, w_off=q0 - wl)
    in_specs = [pl.BlockSpec((1, PAGE, 2 * nk), functools.partial(_dec_page_index, i=i, n_pg=n_pg))
                for i in range(n_pg)]
    in_specs += [
        pl.BlockSpec((1, N_KV, LANES, LANES), lambda b, g, sh, pt: (b, 0, 0, 0)),
        pl.BlockSpec((1, nk, LANES), lambda b, g, sh, pt: (b, 0, 0)),
        pl.BlockSpec((1, 3, LANES), lambda b, g, sh, pt: (b, 0, 0)),
        pl.BlockSpec((1, N_KV, n_cmp, LANES), lambda b, g, sh, pt: (b, 0, 0, 0)),
        pl.BlockSpec((1, qb, 6 * nk), lambda b, g, sh, pt: (b, 0, 0)),
        pl.BlockSpec((1, wl, 2 * nk), lambda b, g, sh, pt: (b, 0, 0)),
        pl.BlockSpec((n_selp, n_cmp), lambda b, g, sh, pt: (0, 0)),
        pl.BlockSpec((LANES, LANES), lambda b, g, sh, pt: (0, 0)),
        pl.BlockSpec((l_past, n_selp), lambda b, g, sh, pt: (0, 0), pipeline_mode=pl.Buffered(1)),
    ]
    grid_spec = pltpu.PrefetchScalarGridSpec(
        num_scalar_prefetch=2,
        grid=(B, n_pages // n_pg),
        in_specs=in_specs,
        out_specs=pl.BlockSpec((1, nk, LANES), lambda b, g, sh, pt: (b, 0, 0)),
        scratch_shapes=[
            pltpu.VMEM((nk + n_selp, LANES), BF16),
            pltpu.VMEM((nk, LANES), F32),
            pltpu.VMEM((8, LANES), F32),
            pltpu.VMEM((nk, LANES), F32),
        ],
    )
    o_t = pl.pallas_call(
        kern,
        out_shape=jax.ShapeDtypeStruct((B, nk, LANES), F32),
        grid_spec=grid_spec,
        compiler_params=_cparams(("parallel", "arbitrary")),
        name="nsa_attend_dec",
    )(shifts, table, *([sel_pages] * n_pg), qc_pad, q_bd, gate_l, cmp_p, kvp, cache_win, wmap, fold, hot)
    o6 = o_t.reshape(B, N_KV, HEAD_DIM, N_KV, group, qb)
    o5 = jnp.einsum("bkdkgq->bqkgd", o6)
    return o5.reshape(B * qb, N_HEADS * HEAD_DIM)


def _rope_tables(pos, half):
    inv = jnp.exp(-math.log(ROPE_THETA) * jnp.arange(half, dtype=F32) / half)
    ang = pos.astype(F32)[:, None] * inv[None, :]
    return jnp.cos(ang), jnp.sin(ang)


def _prep_weights(ret_w_in, ret_w_out, ffn_w_in, ffn_w_out, kv_w, kv_knorm, cmp_w1, cmp_w2,
                  nsa_w_qg, nsa_qnorm, nsa_w_o):
    n_b = nsa_w_qg.shape[0]
    nq = N_HEADS * HEAD_DIM
    qg_pad = nq + LANES - nsa_w_qg.shape[2]
    w_qg = jnp.pad(nsa_w_qg, ((0, 0), (0, 0), (0, qg_pad))).astype(BF16)
    w_o = jnp.pad(nsa_w_o.reshape(n_b, N_HEADS, 1, HEAD_DIM, D_MODEL),
                  ((0, 0), (0, 0), (1, 0), (0, 0), (0, 0))).reshape(n_b, N_HEADS * LANES, D_MODEL)
    R = CMP_LEN // CMP_STRIDE
    w1 = cmp_w1.reshape(2, R, CMP_STRIDE, HEAD_DIM, CMP_HID).transpose(0, 2, 3, 1, 4)
    w1 = w1.reshape(2, CMP_STRIDE, HEAD_DIM, R * CMP_HID)
    w1dup = jnp.concatenate([w1, w1], axis=2).astype(BF16)
    z = jnp.zeros((CMP_HID, HEAD_DIM), F32)
    w2p = jnp.stack([jnp.concatenate([cmp_w2[0], z], axis=1),
                     jnp.concatenate([z, cmp_w2[1]], axis=1)]).astype(BF16)
    ones = jnp.ones((HEAD_DIM,), F32)
    return dict(
        ret_w_in=ret_w_in.astype(BF16), ret_w_out=ret_w_out.astype(BF16),
        ffn_w_in=ffn_w_in.astype(BF16), ffn_w_out=ffn_w_out.astype(BF16),
        kv_w=kv_w.astype(BF16), w_qg=w_qg, w_o=w_o.astype(BF16), w_o_raw=nsa_w_o.astype(BF16),
        kv_gain=jnp.tile(kv_knorm, (1, 2)),
        cmp_gain=jnp.concatenate([kv_knorm[0], ones]).reshape(1, LANES),
        q_gain=jnp.tile(nsa_qnorm, (1, 2)) * (HEAD_DIM ** -0.5),
        q_gain2=jnp.tile(nsa_qnorm, (1, 2)) * (HEAD_DIM ** -0.5 * math.log2(math.e)),
        score_bound=(1.05 * HEAD_DIM ** 0.5 * math.log2(math.e))
        * jnp.max(jnp.abs(nsa_qnorm), axis=1)[:, None] * jnp.max(jnp.abs(kv_knorm), axis=1)[None, :],
        w1dup=w1dup, w2p=w2p,
        bd=jnp.asarray(np.kron(np.eye(2), np.ones((HEAD_DIM, HEAD_DIM))), BF16),
    )


def _trunk(x, past_len, ret_s0, conv0, ctx, W, P):
    B, T, D = x.shape
    M = B * T
    depth = P["norm_mix"].shape[0]
    n_a = P["ret_w_in"].shape[0]
    pos = past_len + jnp.arange(T)
    cos_r, sin_r = _rope_tables(pos, RET_DK // 2)
    c32, s32 = _rope_tables(pos, HEAD_DIM // 2)
    cos_n = jnp.tile(c32, (1, 4))
    sin_n = jnp.concatenate([-s32, s32, -s32, s32], axis=1)
    lg = jnp.log1p(-jnp.exp2(-5.0 - jnp.arange(RET_HEADS, dtype=F32)))
    L = RET_CHUNK if T % RET_CHUNK == 0 else T
    gl = jnp.exp(L * lg)
    tm = min(512, M)
    tf = 1024 if T % 1024 == 0 else 512
    tb = min(512, T)
    tq = min(512, T)
    qb = Q_BLOCK if T % Q_BLOCK == 0 else T
    tt = min(256, T)

    x2 = x.reshape(M, D)
    ret_states, conv_states = [], []
    for layer in range(depth):
        if layer == n_a:
            kvp, selp, winp = kv_project(x2.reshape(B, T, D), P["kv_norm"], W["kv_w"], W["kv_gain"],
                                         cos_n, sin_n, W["bd"], tq)
            nk2 = 2 * N_KV * HEAD_DIM
            if ctx is None:
                table = jnp.arange(M // PAGE, dtype=jnp.int32).reshape(B, T // PAGE)
                cmp_p = compress(kvp.reshape(M // PAGE, PAGE, 3 * nk2), table,
                                 W["w1dup"], W["pb"], W["w2p"], W["cmp_gain"])
                sel_p, win_p = selp, winp
                w_off = 0
            else:
                cache_cmp, cache_sel, cache_win, table = ctx
                n_pool = cache_cmp.shape[0]
                cmp_p = compress(cache_cmp.reshape(n_pool, PAGE, nk2), table,
                                 W["w1dup"], W["pb"], W["w2p"], W["cmp_gain"])
                sel_pages = cache_sel.reshape(n_pool, PAGE, nk2)
                wl = cache_win.shape[1]
                win_rows = cache_win.reshape(B, wl, nk2)
                w_off = past_len - wl
            n_sel = -(-(past_len + T) // SEL_BLOCK)
        h_norm = P["norm_mix"][layer]
        if layer < n_a:
            proj = norm_matmul(x2, h_norm, W["ret_w_in"][layer], tm, 1024)
            og, s_new = retention(proj.reshape(B, T, -1), ret_s0[layer], cos_r, sin_r, lg, gl, L, tb)
            ret_states.append(s_new)
            x2 = matmul_res(og.reshape(M, -1), W["ret_w_out"][layer], x2, tm)
        else:
            j = layer - n_a
            if qb % LANES == 0:
                qc, qr, gates = qg_project_t(x2.reshape(B, T, D), h_norm, W["w_qg"][j],
                                             W["q_gain2"][j:j + 1], cos_n, sin_n, W["bd"], tq, qb)
                shifts = W["score_bound"][j]
                args = (qc, qr, gates, cmp_p, sel_p, win_p)
                o = lax.cond(
                    jnp.max(shifts) <= 30.0,
                    lambda a: nsa_attend_fast(shifts, *a, T, qb, past_len, w_off, n_sel),
                    lambda a: nsa_attend_t(*a, T, qb, past_len, w_off, n_sel),
                    args)
                x2 = matmul_res(o.reshape(M, -1), W["w_o"][j], x2, tm)
            else:
                shifts = W["score_bound"][j]
                x3 = x2.reshape(B, T, D)

                def fast(x3, j=j, h_norm=h_norm, shifts=shifts):
                    qc, qr, gates = qg_project(x3, h_norm, W["w_qg"][j], W["q_gain2"][j:j + 1],
                                               cos_n, sin_n, W["bd"], tq, qb)
                    return nsa_attend_dec(shifts, table, qc, qr, gates, cmp_p, sel_pages, kvp,
                                          win_rows, qb, past_len, n_sel)

                def general(x3, j=j, h_norm=h_norm):
                    qc, qr, gates = qg_project(x3, h_norm, W["w_qg"][j], W["q_gain"][j:j + 1],
                                               cos_n, sin_n, W["bd"], tq, qb)
                    sel_p = repack(sel_pages, table, selp)
                    wtab = jnp.arange(B * (wl // PAGE), dtype=jnp.int32).reshape(B, wl // PAGE)
                    win_p = repack(win_rows.reshape(B * (wl // PAGE), PAGE, nk2), wtab, winp)
                    o = nsa_attend(qc, qr, gates, cmp_p, sel_p, win_p, T, qb, past_len, w_off, n_sel)
                    return o.reshape(M, N_HEADS, LANES)[:, :, HEAD_DIM:].reshape(M, N_HEADS * HEAD_DIM)

                o = lax.cond(jnp.max(shifts) <= 30.0, fast, general, x3)
                x2 = matmul_res(o, W["w_o_raw"][j], x2, tm)
        if T % tf == 0:
            act, tail = ffn_in(x2, P["norm_ffn"][layer], W["ffn_w_in"][layer], conv0[layer],
                               P["ffn_conv_w"][layer], P["ffn_conv_b"][layer], T, tf, 256)
            conv_states.append(tail)
        else:
            proj = norm_matmul(x2, P["norm_ffn"][layer], W["ffn_w_in"][layer], tm, 512)
            proj3 = proj.reshape(B, T, 2 * D_FF)
            act = ffn_mid(proj3, conv0[layer], P["ffn_conv_w"][layer], P["ffn_conv_b"][layer], tt)
            conv_states.append(proj3[:, T - 2:, :D_FF])
        x2 = matmul_res(act.reshape(M, D_FF), W["ffn_w_out"][layer], x2, tm)

    nk = N_KV * HEAD_DIM
    new_cmp = kvp[:, :, 0:2 * nk].reshape(B, T, 2, N_KV, HEAD_DIM)
    new_sel = kvp[:, :, 2 * nk:4 * nk].reshape(B, T, 2, N_KV, HEAD_DIM)
    new_win = kvp[:, :, 4 * nk:6 * nk].reshape(B, T, 2, N_KV, HEAD_DIM)
    return (x2.reshape(B, T, D), jnp.stack(ret_states), jnp.stack(conv_states),
            new_cmp, new_sel, new_win)


def kernel(x_prompt, x_sample, cache_cmp_kv, cache_sel_kv, cache_win_kv, state_ret, state_conv,
           page_table, norm_mix, norm_ffn, ret_w_in, ret_w_out, ffn_w_in, ffn_conv_w, ffn_conv_b,
           ffn_w_out, kv_norm, kv_w, kv_knorm, cmp_pos, cmp_w1, cmp_w2, nsa_w_qg, nsa_qnorm, nsa_w_o):
    W = _prep_weights(ret_w_in, ret_w_out, ffn_w_in, ffn_w_out, kv_w, kv_knorm, cmp_w1, cmp_w2,
                      nsa_w_qg, nsa_qnorm, nsa_w_o)
    W["pb"] = pos_bias(cmp_pos, cmp_w1)
    P = dict(norm_mix=norm_mix, norm_ffn=norm_ffn, ret_w_in=ret_w_in, ffn_conv_w=ffn_conv_w,
             ffn_conv_b=ffn_conv_b, kv_norm=kv_norm)
    depth = norm_mix.shape[0]
    n_a = ret_w_in.shape[0]
    B, T, _ = x_prompt.shape
    zero_ret = jnp.zeros((n_a, B, RET_HEADS, RET_DK, RET_DV), F32)
    zero_conv = jnp.zeros((depth, B, 2, D_FF), F32)
    y_p, ret_p, conv_p, cmp_p, sel_p, win_p = _trunk(x_prompt, 0, zero_ret, zero_conv, None, W, P)
    win_p = win_p[:, T - min(WINDOW, T):]

    db, ts, _ = x_sample.shape
    past_len = page_table.shape[1] * PAGE
    ctx = (cache_cmp_kv, cache_sel_kv, cache_win_kv, page_table)
    y_s, ret_s, conv_s, cmp_s, sel_s, win_new = _trunk(x_sample, past_len, state_ret, state_conv,
                                                        ctx, W, P)
    all_win = jnp.concatenate([cache_win_kv, win_new], axis=1)
    win_s = all_win[:, all_win.shape[1] - min(WINDOW, past_len + ts):]
    return (y_p, y_s, ret_p, ret_s, conv_p, conv_s, cmp_p, cmp_s, sel_p, sel_s, win_p, win_s)
```

```python
import functools
import math

import jax
import jax.numpy as jnp
import numpy as np
from jax import lax
from jax.experimental import pallas as pl
from jax.experimental.pallas import tpu as pltpu

F32 = jnp.float32
BF16 = jnp.bfloat16

D_MODEL = 1024
PAGE = 128
RET_HEADS = 4
RET_DK = 256
RET_DV = 512
RET_CHUNK = 128
N_HEADS = 16
N_KV = 4
HEAD_DIM = 64
CMP_LEN = 32
CMP_STRIDE = 16
CMP_HID = 128
SEL_BLOCK = 64
SEL_TOPK = 16
WINDOW = 512
Q_BLOCK = 128
D_FF = 2816
ROPE_THETA = 10000.0
EPS = 1e-6
NEG_INF = -1e30
TINY = 1e-30
SEL_FORCE = 1e6
SEL_NEG = -1e6

LANES = 128
KEY_TILE = 512
HALO = 16
VMEM_LIMIT = 48 * 1024 * 1024


def _cparams(sem):
    return pltpu.CompilerParams(dimension_semantics=sem, vmem_limit_bytes=VMEM_LIMIT)


def _nt_dot(a, b):
    return lax.dot_general(a, b, (((1,), (1,)), ((), ())), preferred_element_type=F32)


def _tn_dot(a, b):
    return lax.dot_general(a, b, (((0,), (0,)), ((), ())), preferred_element_type=F32)


def _gelu(x):
    return 0.5 * x * (1.0 + jnp.tanh(math.sqrt(2.0 / math.pi) * (x + 0.044715 * (x * x * x))))


def _rms_rows(x, g):
    r = lax.rsqrt(jnp.mean(x * x, axis=-1, keepdims=True) + EPS)
    return x * r * g


def _head_ms(x, bd):
    x2 = x * x
    hi = x2.astype(BF16)
    lo = (x2 - hi.astype(F32)).astype(BF16)
    s = jnp.dot(hi, bd, preferred_element_type=F32) + jnp.dot(lo, bd, preferred_element_type=F32)
    return s * (1.0 / HEAD_DIM)


def _rope64(x, cos, sin):
    lane = lax.broadcasted_iota(jnp.int32, x.shape, 1)
    sw = jnp.where((lane & 63) < 32, pltpu.roll(x, 96, 1), pltpu.roll(x, 32, 1))
    return x * cos + sw * sin


def _pack_pair(k2, v2, dtype):
    lane = lax.broadcasted_iota(jnp.int32, k2.shape, 1)
    lo = lane < HEAD_DIM
    even = jnp.where(lo, k2, pltpu.roll(v2, HEAD_DIM, 1)).astype(dtype)
    odd = jnp.where(lo, pltpu.roll(k2, HEAD_DIM, 1), v2).astype(dtype)
    return even, odd


def _norm_matmul_kernel(x_ref, g_ref, w_ref, o_ref, h_ref):
    @pl.when(pl.program_id(1) == 0)
    def _():
        h_ref[...] = _rms_rows(x_ref[...], g_ref[...]).astype(BF16)

    o_ref[...] = jnp.dot(h_ref[...], w_ref[...], preferred_element_type=F32).astype(o_ref.dtype)


def norm_matmul(x, g, w, tm, tn):
    M, D = x.shape
    N = w.shape[1]
    return pl.pallas_call(
        _norm_matmul_kernel,
        out_shape=jax.ShapeDtypeStruct((M, N), F32),
        grid=(M // tm, N // tn),
        in_specs=[pl.BlockSpec((tm, D), lambda i, j: (i, 0)),
                  pl.BlockSpec((1, D), lambda i, j: (0, 0)),
                  pl.BlockSpec((D, tn), lambda i, j: (0, j))],
        out_specs=pl.BlockSpec((tm, tn), lambda i, j: (i, j)),
        scratch_shapes=[pltpu.VMEM((tm, D), BF16)],
        compiler_params=_cparams(("parallel", "arbitrary")),
        name="norm_matmul",
    )(x, g.reshape(1, D), w)


def _matmul_res_kernel(a_ref, w_ref, r_ref, o_ref):
    o_ref[...] = r_ref[...] + jnp.dot(a_ref[...].astype(BF16), w_ref[...],
                                      preferred_element_type=F32)


def matmul_res(a, w, res, tm):
    M, K = a.shape
    N = w.shape[1]
    return pl.pallas_call(
        _matmul_res_kernel,
        out_shape=jax.ShapeDtypeStruct((M, N), F32),
        grid=(M // tm,),
        in_specs=[pl.BlockSpec((tm, K), lambda i: (i, 0)),
                  pl.BlockSpec((K, N), lambda i: (0, 0)),
                  pl.BlockSpec((tm, N), lambda i: (i, 0))],
        out_specs=pl.BlockSpec((tm, N), lambda i: (i, 0)),
        compiler_params=_cparams(("parallel",)),
        name="matmul_res",
    )(a, w, res)


def _retention_kernel(lg_ref, gl_ref, q_ref, k_ref, v_ref, g_ref, cos_ref, sin_ref, s0_ref,
                      o_ref, sout_ref, S_ref, *, L, n_chunk):
    h = pl.program_id(1)
    t = pl.program_id(2)
    lg = lg_ref[h]
    gl = gl_ref[h]

    @pl.when(t == 0)
    def _():
        S_ref[...] = s0_ref[0, 0]

    ii = lax.broadcasted_iota(jnp.int32, (L, L), 0)
    jj = lax.broadcasted_iota(jnp.int32, (L, L), 1)
    diff = (ii - jj).astype(F32)
    decay = jnp.where(diff >= 0, jnp.exp(jnp.maximum(diff, 0.0) * lg), 0.0)
    idx = lax.broadcasted_iota(jnp.int32, (L, 1), 0).astype(F32)
    q_dec = jnp.exp((idx + 1.0) * lg)
    k_dec = jnp.exp((L - 1.0 - idx) * lg)
    half = RET_DK // 2

    for c in range(n_chunk):
        rows = pl.ds(c * L, L)
        cos = cos_ref[rows, :]
        sin = sin_ref[rows, :]

        def rope(x):
            x1, x2 = x[:, :half], x[:, half:]
            return jnp.concatenate([x1 * cos - x2 * sin, x2 * cos + x1 * sin], axis=1)

        qr = rope(q_ref[0, rows, :])
        kr = rope(k_ref[0, rows, :]) * (RET_DK ** -0.5)
        qb = qr.astype(BF16)
        vb = v_ref[0, rows, :].astype(BF16)
        sc = _nt_dot(qb, kr.astype(BF16)) * decay
        S = S_ref[...]
        o = jnp.dot(sc.astype(BF16), vb, preferred_element_type=F32)
        o = o + jnp.dot(qb, S.astype(BF16), preferred_element_type=F32) * q_dec
        S_ref[...] = S * gl + _tn_dot((kr * k_dec).astype(BF16), vb)
        on = o * lax.rsqrt(jnp.mean(o * o, axis=-1, keepdims=True) + EPS)
        g = g_ref[0, rows, :]
        o_ref[0, rows, :] = (on * (g * jax.nn.sigmoid(g))).astype(o_ref.dtype)

    @pl.when(t == pl.num_programs(2) - 1)
    def _():
        sout_ref[0, 0] = S_ref[...]


def retention(proj, s0, cos, sin, lg, gl, L, tb):
    B, T, _ = proj.shape
    n_chunk = tb // L
    odt = BF16 if tb % 16 == 0 else F32
    kern = functools.partial(_retention_kernel, L=L, n_chunk=n_chunk)
    grid_spec = pltpu.PrefetchScalarGridSpec(
        num_scalar_prefetch=2,
        grid=(B, RET_HEADS, T // tb),
        in_specs=[
            pl.BlockSpec((1, tb, RET_DK), lambda b, h, t, *_: (b, t, h)),
            pl.BlockSpec((1, tb, RET_DK), lambda b, h, t, *_: (b, t, RET_HEADS + h)),
            pl.BlockSpec((1, tb, RET_DV), lambda b, h, t, *_: (b, t, RET_HEADS + h)),
            pl.BlockSpec((1, tb, RET_DV), lambda b, h, t, *_: (b, t, 2 * RET_HEADS + h)),
            pl.BlockSpec((tb, RET_DK // 2), lambda b, h, t, *_: (t, 0)),
            pl.BlockSpec((tb, RET_DK // 2), lambda b, h, t, *_: (t, 0)),
            pl.BlockSpec((1, 1, RET_DK, RET_DV), lambda b, h, t, *_: (b, h, 0, 0)),
        ],
        out_specs=[
            pl.BlockSpec((1, tb, RET_DV), lambda b, h, t, *_: (b, t, h)),
            pl.BlockSpec((1, 1, RET_DK, RET_DV), lambda b, h, t, *_: (b, h, 0, 0)),
        ],
        scratch_shapes=[pltpu.VMEM((RET_DK, RET_DV), F32)],
    )
    return pl.pallas_call(
        kern,
        out_shape=[jax.ShapeDtypeStruct((B, T, RET_HEADS * RET_DV), odt),
                   jax.ShapeDtypeStruct((B, RET_HEADS, RET_DK, RET_DV), F32)],
        grid_spec=grid_spec,
        compiler_params=_cparams(("parallel", "parallel", "arbitrary")),
        name="retention",
    )(lg, gl, proj, proj, proj, proj, cos, sin, s0)


def _ffn_mid_kernel(u_ref, gt_ref, halo_ref, cw_ref, cb_ref, o_ref):
    u = u_ref[0]
    hl = halo_ref[0, 0]
    row = lax.broadcasted_iota(jnp.int32, u.shape, 0)
    u1 = jnp.where(row == 0, hl[1:2], pltpu.roll(u, 1, 0))
    u2 = jnp.where(row == 0, hl[0:1], jnp.where(row == 1, hl[1:2], pltpu.roll(u, 2, 0)))
    c = cb_ref[...] + cw_ref[0:1] * u2
    c = c + cw_ref[1:2] * u1
    c = c + cw_ref[2:3] * u
    o_ref[0] = (_gelu(c) * gt_ref[0]).astype(o_ref.dtype)


def ffn_mid(proj, buf, conv_w, conv_b, tt):
    B, T, _ = proj.shape
    nt = T // tt
    if nt > 1:
        tails = proj[:, :, :D_FF].reshape(B, nt, tt, D_FF)[:, :-1, tt - 2:, :]
        halo = jnp.concatenate([buf[:, None], tails], axis=1)
    else:
        halo = buf[:, None]
    odt = BF16 if tt % 16 == 0 else F32
    return pl.pallas_call(
        _ffn_mid_kernel,
        out_shape=jax.ShapeDtypeStruct((B, T, D_FF), odt),
        grid=(B, nt),
        in_specs=[pl.BlockSpec((1, tt, D_FF), lambda b, t: (b, t, 0)),
                  pl.BlockSpec((1, tt, D_FF), lambda b, t: (b, t, 1)),
                  pl.BlockSpec((1, 1, 2, D_FF), lambda b, t: (b, t, 0, 0)),
                  pl.BlockSpec((3, D_FF), lambda b, t: (0, 0)),
                  pl.BlockSpec((1, D_FF), lambda b, t: (0, 0))],
        out_specs=pl.BlockSpec((1, tt, D_FF), lambda b, t: (b, t, 0)),
        compiler_params=_cparams(("parallel", "parallel")),
        name="ffn_mid",
    )(proj, proj, halo, conv_w, conv_b.reshape(1, D_FF))


def _ffn_in_kernel(x_ref, xh_ref, g_ref, wu_ref, wg_ref, buf_ref, cw_ref, cb_ref,
                   act_ref, tail_ref, h_ref, hh_ref, *, tiles_per_seq):
    i = pl.program_id(0)

    @pl.when(pl.program_id(1) == 0)
    def _():
        h_ref[...] = _rms_rows(x_ref[...], g_ref[...]).astype(BF16)
        hh_ref[...] = _rms_rows(xh_ref[...], g_ref[...]).astype(BF16)

    h = h_ref[...]
    u = jnp.dot(h, wu_ref[...], preferred_element_type=F32)
    gt = jnp.dot(h, wg_ref[...], preferred_element_type=F32)
    uh = jnp.dot(hh_ref[...], wu_ref[...], preferred_element_type=F32)
    seq_start = (i % tiles_per_seq) == 0
    hl = jnp.where(seq_start, buf_ref[0], uh[HALO - 2:, :])
    row = lax.broadcasted_iota(jnp.int32, u.shape, 0)
    u1 = jnp.where(row == 0, hl[1:2], pltpu.roll(u, 1, 0))
    u2 = jnp.where(row == 0, hl[0:1], jnp.where(row == 1, hl[1:2], pltpu.roll(u, 2, 0)))
    c = cb_ref[...] + cw_ref[0:1] * u2
    c = c + cw_ref[1:2] * u1
    c = c + cw_ref[2:3] * u
    act_ref[...] = (_gelu(c) * gt).astype(act_ref.dtype)
    tail_ref[0] = u[u.shape[0] - 2:, :]


def ffn_in(x, g, w, buf, conv_w, conv_b, T, tm, tn):
    M, D = x.shape
    B = M // T
    n_col = D_FF // tn
    tiles_per_seq = T // tm
    kern = functools.partial(_ffn_in_kernel, tiles_per_seq=tiles_per_seq)
    act, tails = pl.pallas_call(
        kern,
        out_shape=[jax.ShapeDtypeStruct((M, D_FF), BF16),
                   jax.ShapeDtypeStruct((M // tm, 2, D_FF), F32)],
        grid=(M // tm, n_col),
        in_specs=[pl.BlockSpec((tm, D), lambda i, j: (i, 0)),
                  pl.BlockSpec((HALO, D), lambda i, j: (jnp.maximum(i * (tm // HALO) - 1, 0), 0)),
                  pl.BlockSpec((1, D), lambda i, j: (0, 0)),
                  pl.BlockSpec((D, tn), lambda i, j: (0, j)),
                  pl.BlockSpec((D, tn), lambda i, j: (0, n_col + j)),
                  pl.BlockSpec((1, 2, tn), lambda i, j: (i // tiles_per_seq, 0, j)),
                  pl.BlockSpec((3, tn), lambda i, j: (0, j)),
                  pl.BlockSpec((1, tn), lambda i, j: (0, j))],
        out_specs=[pl.BlockSpec((tm, tn), lambda i, j: (i, j)),
                   pl.BlockSpec((1, 2, tn), lambda i, j: (i, 0, j))],
        scratch_shapes=[pltpu.VMEM((tm, D), BF16), pltpu.VMEM((HALO, D), BF16)],
        compiler_params=_cparams(("parallel", "arbitrary")),
        name="ffn_in",
    )(x, x, g.reshape(1, D), w, w, buf, conv_w, conv_b.reshape(1, D_FF))
    return act, tails[tiles_per_seq - 1::tiles_per_seq]


def _kv_kernel(x_ref, g_ref, w_ref, gain_ref, cos_ref, sin_ref, bd_ref,
               kv_ref, selp_ref, winp_ref):
    h = _rms_rows(x_ref[0], g_ref[...]).astype(BF16)
    y = jnp.dot(h, w_ref[...], preferred_element_type=F32)
    cos = cos_ref[...]
    sin = sin_ref[...]
    bd = bd_ref[...]
    nk = N_KV * HEAD_DIM
    kv_ref[0, :, 0:2 * nk] = y[:, 0:2 * nk]
    for br, pack_ref in enumerate((selp_ref, winp_ref)):
        base = 2 * nk * (br + 1)
        kv_ref[0, :, base + nk:base + 2 * nk] = y[:, base + nk:base + 2 * nk]
        for p in range(2):
            kx = y[:, base + LANES * p:base + LANES * (p + 1)]
            kn = kx * lax.rsqrt(_head_ms(kx, bd) + EPS) * gain_ref[br + 1:br + 2, :]
            kr = _rope64(kn, cos, sin)
            kv_ref[0, :, base + LANES * p:base + LANES * (p + 1)] = kr
            vx = y[:, base + nk + LANES * p:base + nk + LANES * (p + 1)]
            even, odd = _pack_pair(kr, vx, pack_ref.dtype)
            pack_ref[0, 2 * p] = even
            pack_ref[0, 2 * p + 1] = odd


def kv_project(x, g, w, gains, cos, sin, bd, tm):
    B, T, D = x.shape
    N = w.shape[1]
    pdt = BF16 if tm % 16 == 0 else F32
    return pl.pallas_call(
        _kv_kernel,
        out_shape=[jax.ShapeDtypeStruct((B, T, N), F32),
                   jax.ShapeDtypeStruct((B, N_KV, T, LANES), pdt),
                   jax.ShapeDtypeStruct((B, N_KV, T, LANES), pdt)],
        grid=(B, T // tm),
        in_specs=[pl.BlockSpec((1, tm, D), lambda b, t: (b, t, 0)),
                  pl.BlockSpec((1, D), lambda b, t: (0, 0)),
                  pl.BlockSpec((D, N), lambda b, t: (0, 0)),
                  pl.BlockSpec((3, LANES), lambda b, t: (0, 0)),
                  pl.BlockSpec((tm, LANES), lambda b, t: (t, 0)),
                  pl.BlockSpec((tm, LANES), lambda b, t: (t, 0)),
                  pl.BlockSpec((LANES, LANES), lambda b, t: (0, 0))],
        out_specs=[pl.BlockSpec((1, tm, N), lambda b, t: (b, t, 0)),
                   pl.BlockSpec((1, N_KV, tm, LANES), lambda b, t: (b, 0, t, 0)),
                   pl.BlockSpec((1, N_KV, tm, LANES), lambda b, t: (b, 0, t, 0))],
        compiler_params=_cparams(("parallel", "parallel")),
        name="kv_project",
    )(x, g.reshape(1, D), w, gains, cos, sin, bd)


def _qg_kernel(x_ref, g_ref, w_ref, gain_ref, cos_ref, sin_ref, bd_ref,
               qc_ref, qr_ref, gate_ref, *, qb, n_qb):
    h = _rms_rows(x_ref[0], g_ref[...]).astype(BF16)
    y = jnp.dot(h, w_ref[...], preferred_element_type=F32)
    cos = cos_ref[...]
    sin = sin_ref[...]
    bd = bd_ref[...]
    nq = N_HEADS * HEAD_DIM
    gate_ref[0] = jax.nn.sigmoid(y[:, nq:nq + LANES])
    lane = lax.broadcasted_iota(jnp.int32, (y.shape[0], LANES), 1)
    lo = lane < HEAD_DIM
    group = N_HEADS // N_KV
    for p in range(N_HEADS // 2):
        qx = y[:, LANES * p:LANES * (p + 1)]
        qn = qx * lax.rsqrt(_head_ms(qx, bd) + EPS) * gain_ref[...]
        qr = _rope64(qn, cos, sin)
        for src, dst in ((qn, qc_ref), (qr, qr_ref)):
            for par in range(2):
                hd = 2 * p + par
                kvh, gi = hd // group, hd % group
                v = src if par == 0 else pltpu.roll(src, HEAD_DIM, 1)
                v = jnp.where(lo, v, 0.0).astype(dst.dtype)
                for j in range(n_qb):
                    dst[0, kvh, j, gi * qb:(gi + 1) * qb, :] = v[j * qb:(j + 1) * qb, :]


def qg_project(x, g, w, gain, cos, sin, bd, tm, qb):
    B, T, D = x.shape
    N = w.shape[1]
    n_qb = tm // qb
    group = N_HEADS // N_KV
    qdt = BF16 if qb % 16 == 0 else F32
    kern = functools.partial(_qg_kernel, qb=qb, n_qb=n_qb)
    qshape = jax.ShapeDtypeStruct((B, N_KV, T // qb, group * qb, LANES), qdt)
    qspec = pl.BlockSpec((1, N_KV, n_qb, group * qb, LANES), lambda b, t: (b, 0, t, 0, 0))
    return pl.pallas_call(
        kern,
        out_shape=[qshape, qshape, jax.ShapeDtypeStruct((B, T, LANES), F32)],
        grid=(B, T // tm),
        in_specs=[pl.BlockSpec((1, tm, D), lambda b, t: (b, t, 0)),
                  pl.BlockSpec((1, D), lambda b, t: (0, 0)),
                  pl.BlockSpec((D, N), lambda b, t: (0, 0)),
                  pl.BlockSpec((1, LANES), lambda b, t: (0, 0)),
                  pl.BlockSpec((tm, LANES), lambda b, t: (t, 0)),
                  pl.BlockSpec((tm, LANES), lambda b, t: (t, 0)),
                  pl.BlockSpec((LANES, LANES), lambda b, t: (0, 0))],
        out_specs=[qspec, qspec, pl.BlockSpec((1, tm, LANES), lambda b, t: (b, t, 0))],
        compiler_params=_cparams(("parallel", "parallel")),
        name="qg_project",
    )(x, g.reshape(1, D), w, gain, cos, sin, bd)


def _pos_bias_kernel(p_ref, w_ref, o_ref):
    o_ref[0] = jnp.dot(p_ref[0].astype(BF16), w_ref[0].astype(BF16), preferred_element_type=F32)


def pos_bias(cmp_pos, cmp_w1):
    K = CMP_LEN * HEAD_DIM
    p = jnp.broadcast_to(cmp_pos.reshape(2, 1, K), (2, 8, K))
    out = pl.pallas_call(
        _pos_bias_kernel,
        out_shape=jax.ShapeDtypeStruct((2, 8, CMP_HID), F32),
        grid=(2,),
        in_specs=[pl.BlockSpec((1, 8, K), lambda c: (c, 0, 0)),
                  pl.BlockSpec((1, K, CMP_HID), lambda c: (c, 0, 0))],
        out_specs=pl.BlockSpec((1, 8, CMP_HID), lambda c: (c, 0, 0)),
        name="pos_bias",
    )(p, cmp_w1)
    return out[:, 0, :]


def _compress_kernel(pt_ref, *refs, n_pg):
    pages = refs[:n_pg]
    w1_ref, pb_ref, w2_ref, gain_ref, out_ref, carry_ref, slab_ref = refs[n_pg:]
    g = pl.program_id(1)

    @pl.when(g == 0)
    def _():
        carry_ref[...] = jnp.zeros_like(carry_ref)

    n = n_pg * (PAGE // CMP_STRIDE)
    lane = lax.broadcasted_iota(jnp.int32, (n, LANES), 1)
    row = lax.broadcasted_iota(jnp.int32, (n, LANES), 0)
    lo = lane < HEAD_DIM
    for pair in range(N_KV // 2):
        res = [jnp.zeros((n, LANES), F32), jnp.zeros((n, LANES), F32)]
        for c in range(2):
            col = c * N_KV * HEAD_DIM + pair * LANES
            for i in range(n_pg):
                slab_ref[PAGE * i:PAGE * (i + 1), :] = pages[i][0, :, col:col + LANES]
            for par in range(2):
                keep = lo if par == 0 else jnp.logical_not(lo)
                acc = jnp.zeros((n, 2 * CMP_HID), F32)
                for s in range(0, CMP_STRIDE, 2):
                    xm = jnp.concatenate(
                        [jnp.where(keep, slab_ref[pl.ds(s + i, n, stride=CMP_STRIDE), :], 0.0)
                         for i in range(2)], axis=1).astype(BF16)
                    acc = acc + jnp.dot(xm, w1_ref[c, s // 2], preferred_element_type=F32)
                p0 = acc[:, :CMP_HID]
                p1 = acc[:, CMP_HID:]
                ci = 2 * (2 * pair + par) + c
                prev = jnp.where(row == 0, carry_ref[ci, 7:8, :], pltpu.roll(p0, 1, 0))
                carry_ref[ci] = p0[n - 8:n, :]
                hid = _gelu(prev + p1 + pb_ref[c:c + 1, :]).astype(BF16)
                res[par] = res[par] + jnp.dot(hid, w2_ref[c], preferred_element_type=F32)
        for par in range(2):
            r = res[par]
            ms = jnp.sum(jnp.where(lo, r * r, 0.0), axis=-1, keepdims=True) * (1.0 / HEAD_DIM)
            kn = r * lax.rsqrt(ms + EPS) * gain_ref[...]
            out_ref[0, 2 * pair + par] = jnp.where(lo, kn, r).astype(out_ref.dtype)


def _page_index(b, g, pt_ref, *, i, n_pg):
    return (pt_ref[b, g * n_pg + i], 0, 0)


def compress(pages_arr, table, w1dup, pb, w2p, gain):
    B, n_pages = table.shape
    n_pg = 16 if n_pages % 16 == 0 else n_pages
    n = n_pg * (PAGE // CMP_STRIDE)
    n_sub = n_pages * (PAGE // CMP_STRIDE)
    width = 2 * N_KV * HEAD_DIM
    in_specs = [pl.BlockSpec((1, PAGE, width), functools.partial(_page_index, i=i, n_pg=n_pg))
                for i in range(n_pg)]
    in_specs += [pl.BlockSpec(w1dup.shape, lambda b, g, pt: (0, 0, 0, 0)),
                 pl.BlockSpec(pb.shape, lambda b, g, pt: (0, 0)),
                 pl.BlockSpec(w2p.shape, lambda b, g, pt: (0, 0, 0)),
                 pl.BlockSpec((1, LANES), lambda b, g, pt: (0, 0))]
    grid_spec = pltpu.PrefetchScalarGridSpec(
        num_scalar_prefetch=1,
        grid=(B, n_pages // n_pg),
        in_specs=in_specs,
        out_specs=pl.BlockSpec((1, N_KV, n, LANES), lambda b, g, pt: (b, 0, g, 0)),
        scratch_shapes=[pltpu.VMEM((2 * N_KV, 8, CMP_HID), F32),
                        pltpu.VMEM((n_pg * PAGE, LANES), F32)],
    )
    return pl.pallas_call(
        functools.partial(_compress_kernel, n_pg=n_pg),
        out_shape=jax.ShapeDtypeStruct((B, N_KV, n_sub, LANES), BF16),
        grid_spec=grid_spec,
        compiler_params=_cparams(("parallel", "arbitrary")),
        name="compress",
    )(table, *([pages_arr] * n_pg), w1dup, pb, w2p, gain)


def _repack_kernel(pt_ref, p0, p1, p2, p3, new_ref, out_ref, *, n_full, t_new):
    t = pl.program_id(1)
    nk = N_KV * HEAD_DIM

    @pl.when(t < n_full)
    def _():
        for i, pg in enumerate((p0, p1, p2, p3)):
            x = pg[0]
            for p in range(2):
                even, odd = _pack_pair(x[:, LANES * p:LANES * (p + 1)],
                                       x[:, nk + LANES * p:nk + LANES * (p + 1)], out_ref.dtype)
                out_ref[0, 2 * p, PAGE * i:PAGE * (i + 1), :] = even
                out_ref[0, 2 * p + 1, PAGE * i:PAGE * (i + 1), :] = odd

    @pl.when(t == n_full)
    def _():
        pad = jnp.zeros((N_KV, KEY_TILE - t_new, LANES), F32)
        out_ref[0] = jnp.concatenate([new_ref[0].astype(F32), pad], axis=1).astype(out_ref.dtype)


def _repack_page_index(b, t, pt_ref, *, i, n_pages):
    return (pt_ref[b, jnp.minimum(4 * t + i, n_pages - 1)], 0, 0)


def repack(pages_arr, table, new_pack):
    B, n_pages = table.shape
    n_full = n_pages // 4
    t_new = new_pack.shape[2]
    width = 2 * N_KV * HEAD_DIM
    in_specs = [pl.BlockSpec((1, PAGE, width),
                             functools.partial(_repack_page_index, i=i, n_pages=n_pages))
                for i in range(4)]
    in_specs.append(pl.BlockSpec((1, N_KV, t_new, LANES), lambda b, t, pt: (b, 0, 0, 0)))
    grid_spec = pltpu.PrefetchScalarGridSpec(
        num_scalar_prefetch=1,
        grid=(B, n_full + 1),
        in_specs=in_specs,
        out_specs=pl.BlockSpec((1, N_KV, KEY_TILE, LANES), lambda b, t, pt: (b, 0, t, 0)),
    )
    return pl.pallas_call(
        functools.partial(_repack_kernel, n_full=n_full, t_new=t_new),
        out_shape=jax.ShapeDtypeStruct((B, N_KV, (n_full + 1) * KEY_TILE, LANES), BF16),
        grid_spec=grid_spec,
        compiler_params=_cparams(("parallel", "arbitrary")),
        name="repack",
    )(table, pages_arr, pages_arr, pages_arr, pages_arr, new_pack)


def _softmax_step(s, mask, kv, m_ref, l_ref, a_ref, k):
    s = jnp.where(mask, s, NEG_INF)
    m_old = m_ref[k]
    m_new = jnp.maximum(m_old, jnp.max(s, axis=-1, keepdims=True))
    alpha = jnp.exp(m_old - m_new)
    p = jnp.where(mask, jnp.exp(s - m_new), 0.0)
    l_ref[k] = alpha * l_ref[k] + jnp.sum(p, axis=-1, keepdims=True)
    a_ref[k] = alpha * a_ref[k] + jnp.dot(p.astype(BF16), kv, preferred_element_type=F32)
    m_ref[k] = m_new


def _nsa_kernel(qi_ref, kt_ref, wt_ref, wf_ref, last_ref,
                qc_ref, qr_ref, gate_ref, cmp_ref, sel_ref, win_ref, wmap_ref,
                o_ref,
                selm_ref, oc_ref, ms_ref, ls_ref, as_ref, mw_ref, lw_ref, aw_ref,
                *, qb, n_sel, n_selp, n_cmp, q0, w_off):
    step = pl.program_id(1)
    qi = qi_ref[step]
    kt = kt_ref[step]
    group = N_HEADS // N_KV
    R = group * qb
    rowq = lax.broadcasted_iota(jnp.int32, (R, 1), 0) & (qb - 1)
    qpos = q0 + qi * qb + rowq
    qpos_q = q0 + qi * qb + lax.broadcasted_iota(jnp.int32, (qb, 1), 0)

    @pl.when(kt == 0)
    def _first():
        for m_ref, l_ref, a_ref in ((ms_ref, ls_ref, as_ref), (mw_ref, lw_ref, aw_ref)):
            m_ref[...] = jnp.full(m_ref.shape, NEG_INF, F32)
            l_ref[...] = jnp.zeros(l_ref.shape, F32)
            a_ref[...] = jnp.zeros(a_ref.shape, F32)
        n_idx = lax.broadcasted_iota(jnp.int32, (1, n_cmp), 1)
        cvalid = (n_idx >= 1) & ((n_idx - 1) * CMP_STRIDE + CMP_LEN - 1 <= qpos)
        blk = lax.broadcasted_iota(jnp.int32, (qb, n_selp), 1)
        blk_f = blk.astype(F32)
        cur = qpos_q >> 6
        forced = (blk == 0) | (blk == cur) | (blk == cur - 1)
        reach = blk * SEL_BLOCK <= qpos_q
        real = blk < n_sel
        for k in range(N_KV):
            ckv = cmp_ref[0, k]
            s = _nt_dot(qc_ref[0, k, 0].astype(BF16), ckv)
            s = jnp.where(cvalid, s, NEG_INF)
            m = jnp.max(s, axis=-1, keepdims=True)
            e = jnp.where(cvalid, jnp.exp(s - m), 0.0)
            p = e / jnp.maximum(jnp.sum(e, axis=-1, keepdims=True), TINY)
            oc_ref[k] = jnp.dot(p.astype(BF16), ckv, preferred_element_type=F32)
            psum = p[0:qb]
            for gi in range(1, group):
                psum = psum + p[gi * qb:(gi + 1) * qb]
            hi = psum.astype(BF16)
            lo = (psum - hi.astype(F32)).astype(BF16)
            imp = _nt_dot(hi, wmap_ref[...]) + _nt_dot(lo, wmap_ref[...])
            v = jnp.where(forced, SEL_FORCE, jnp.where(reach, imp, SEL_NEG))
            v = jnp.where(real, v, -jnp.inf)

            def pick_one(_, carry):
                v, sel = carry
                m = jnp.max(v, axis=-1, keepdims=True)
                first = jnp.min(jnp.where(v == m, blk_f, float(n_selp)), axis=-1, keepdims=True)
                pick = blk_f == first
                sel = jnp.where(pick & (m > 0.5 * SEL_NEG), 1.0, sel)
                return jnp.where(pick, -jnp.inf, v), sel

            _, sel = lax.fori_loop(0, SEL_TOPK, pick_one, (v, jnp.zeros((qb, n_selp), F32)))
            selm_ref[k] = sel

    tok = kt * KEY_TILE + lax.broadcasted_iota(jnp.int32, (1, KEY_TILE), 1)
    blk_of_tok = tok >> 6
    expand = jnp.where(
        lax.broadcasted_iota(jnp.int32, (n_selp, KEY_TILE), 0) == blk_of_tok, 1.0, 0.0).astype(BF16)
    causal = tok <= qpos
    for k in range(N_KV):
        kv = sel_ref[0, k]
        s = _nt_dot(qr_ref[0, k, 0].astype(BF16), kv)
        sm = jnp.dot(selm_ref[k].astype(BF16), expand, preferred_element_type=F32)
        sm = jnp.concatenate([sm] * group, axis=0)
        _softmax_step(s, causal & (sm > 0.5), kv, ms_ref, ls_ref, as_ref, k)

    @pl.when(wf_ref[step] == 1)
    def _window():
        wpos = w_off + wt_ref[step] * KEY_TILE + lax.broadcasted_iota(jnp.int32, (1, KEY_TILE), 1)
        mask = (wpos <= qpos) & (wpos > qpos - WINDOW) & (wpos >= 0)
        for k in range(N_KV):
            kv = win_ref[0, k]
            s = _nt_dot(qr_ref[0, k, 0].astype(BF16), kv)
            _softmax_step(s, mask, kv, mw_ref, lw_ref, aw_ref, k)

    @pl.when(last_ref[step] == 1)
    def _finish():
        gate = gate_ref[0]
        for k in range(N_KV):
            for gi in range(group):
                hd = k * group + gi
                rs = slice(gi * qb, (gi + 1) * qb)
                o_s = as_ref[k, rs, :] / jnp.maximum(ls_ref[k, rs, :], TINY)
                o_w = aw_ref[k, rs, :] / jnp.maximum(lw_ref[k, rs, :], TINY)
                o = gate[:, 3 * hd:3 * hd + 1] * oc_ref[k, rs, :]
                o = o + gate[:, 3 * hd + 1:3 * hd + 2] * o_s
                o = o + gate[:, 3 * hd + 2:3 * hd + 3] * o_w
                o_ref[0, :, LANES * hd:LANES * (hd + 1)] = o.astype(o_ref.dtype)


def _nsa_tables(T, qb, q0, w_off):
    rows = []
    for qi in range(T // qb):
        q_lo = q0 + qi * qb
        q_hi = q_lo + qb - 1
        last_kt = q_hi // KEY_TILE
        w_lo = max(q_lo - WINDOW + 1, w_off)
        wt0 = (w_lo - w_off) // KEY_TILE
        wt1 = (q_hi - w_off) // KEY_TILE
        n_w = wt1 - wt0 + 1
        assert n_w <= last_kt + 1
        for kt in range(last_kt + 1):
            rows.append((qi, kt, wt0 + min(kt, n_w - 1), int(kt < n_w), int(kt == last_kt)))
    tab = np.asarray(rows, np.int32).T
    return [jnp.asarray(tab[i]) for i in range(5)]


def _overlap_map(n_cmp_rows, n_selp):
    m = np.arange(n_cmp_rows)[None, :]
    s = np.arange(n_selp)[:, None]
    c0 = (m - 1) * CMP_STRIDE
    ov = np.minimum(c0 + CMP_LEN, s * SEL_BLOCK + SEL_BLOCK) - np.maximum(c0, s * SEL_BLOCK)
    w = np.maximum(ov, 0).astype(np.float32) / CMP_LEN
    w[:, 0] = 0.0
    return jnp.asarray(w, BF16)


def nsa_attend(qc, qr, gates, cmp_p, sel_p, win_p, T, qb, q0, w_off, n_sel):
    B = qc.shape[0]
    group = N_HEADS // N_KV
    R = group * qb
    n_cmp = cmp_p.shape[2]
    n_selp = -(-n_sel // LANES) * LANES
    tabs = _nsa_tables(T, qb, q0, w_off)
    n_steps = int(tabs[0].shape[0])
    wmap = _overlap_map(n_cmp, n_selp)
    odt = BF16 if qb % 16 == 0 else F32
    kern = functools.partial(_nsa_kernel, qb=qb, n_sel=n_sel, n_selp=n_selp, n_cmp=n_cmp,
                             q0=q0, w_off=w_off)
    qspec = pl.BlockSpec((1, N_KV, 1, R, LANES), lambda b, s, qi, kt, wt, wf, la: (b, 0, qi[s], 0, 0))
    grid_spec = pltpu.PrefetchScalarGridSpec(
        num_scalar_prefetch=5,
        grid=(B, n_steps),
        in_specs=[
            qspec, qspec,
            pl.BlockSpec((1, qb, LANES), lambda b, s, qi, kt, wt, wf, la: (b, qi[s], 0)),
            pl.BlockSpec((1, N_KV, n_cmp, LANES), lambda b, s, qi, kt, wt, wf, la: (b, 0, 0, 0)),
            pl.BlockSpec((1, N_KV, KEY_TILE, LANES), lambda b, s, qi, kt, wt, wf, la: (b, 0, kt[s], 0)),
            pl.BlockSpec((1, N_KV, KEY_TILE, LANES), lambda b, s, qi, kt, wt, wf, la: (b, 0, wt[s], 0)),
            pl.BlockSpec((n_selp, n_cmp), lambda b, s, qi, kt, wt, wf, la: (0, 0)),
        ],
        out_specs=pl.BlockSpec((1, qb, N_HEADS * LANES), lambda b, s, qi, kt, wt, wf, la: (b, qi[s], 0)),
        scratch_shapes=[
            pltpu.VMEM((N_KV, qb, n_selp), F32),
            pltpu.VMEM((N_KV, R, LANES), F32),
            pltpu.VMEM((N_KV, R, 1), F32), pltpu.VMEM((N_KV, R, 1), F32), pltpu.VMEM((N_KV, R, LANES), F32),
            pltpu.VMEM((N_KV, R, 1), F32), pltpu.VMEM((N_KV, R, 1), F32), pltpu.VMEM((N_KV, R, LANES), F32),
        ],
    )
    return pl.pallas_call(
        kern,
        out_shape=jax.ShapeDtypeStruct((B, T, N_HEADS * LANES), odt),
        grid_spec=grid_spec,
        compiler_params=_cparams(("parallel", "arbitrary")),
        name="nsa_attend",
    )(*tabs, qc, qr, gates, cmp_p, sel_p, win_p, wmap)


def _qgt_kernel(x_ref, g_ref, w_ref, gain_ref, cos_ref, sin_ref, bd_ref,
                qc_ref, qr_ref, gate_ref, *, qb, n_qb):
    h = _rms_rows(x_ref[0], g_ref[...]).astype(BF16)
    y = jnp.dot(h, w_ref[...], preferred_element_type=F32)
    cos = cos_ref[...]
    sin = sin_ref[...]
    bd = bd_ref[...]
    nq = N_HEADS * HEAD_DIM
    group = N_HEADS // N_KV
    gate_t = jax.nn.sigmoid(y[:, nq:nq + LANES]).T
    for j in range(n_qb):
        gate_ref[0, j] = gate_t[:, j * qb:(j + 1) * qb]
    pad = jnp.zeros((N_KV, n_qb, HEAD_DIM, group * qb), qc_ref.dtype)
    qc_ref[0, :, :, HEAD_DIM:, :] = pad
    qr_ref[0, :, :, HEAD_DIM:, :] = pad
    for p in range(N_HEADS // 2):
        qx = y[:, LANES * p:LANES * (p + 1)]
        qn = qx * lax.rsqrt(_head_ms(qx, bd) + EPS) * gain_ref[...]
        qr = _rope64(qn, cos, sin)
        for src, dst in ((qn, qc_ref), (qr, qr_ref)):
            st = src.T.astype(dst.dtype)
            for par in range(2):
                kvh, gi = divmod(2 * p + par, group)
                for j in range(n_qb):
                    dst[0, kvh, j, 0:HEAD_DIM, gi * qb:(gi + 1) * qb] = (
                        st[par * HEAD_DIM:(par + 1) * HEAD_DIM, j * qb:(j + 1) * qb])


def qg_project_t(x, g, w, gain, cos, sin, bd, tm, qb):
    B, T, D = x.shape
    N = w.shape[1]
    n_qb = tm // qb
    group = N_HEADS // N_KV
    kern = functools.partial(_qgt_kernel, qb=qb, n_qb=n_qb)
    qshape = jax.ShapeDtypeStruct((B, N_KV, T // qb, LANES, group * qb), BF16)
    qspec = pl.BlockSpec((1, N_KV, n_qb, LANES, group * qb), lambda b, t: (b, 0, t, 0, 0))
    return pl.pallas_call(
        kern,
        out_shape=[qshape, qshape, jax.ShapeDtypeStruct((B, T // qb, LANES, qb), F32)],
        grid=(B, T // tm),
        in_specs=[pl.BlockSpec((1, tm, D), lambda b, t: (b, t, 0)),
                  pl.BlockSpec((1, D), lambda b, t: (0, 0)),
                  pl.BlockSpec((D, N), lambda b, t: (0, 0)),
                  pl.BlockSpec((1, LANES), lambda b, t: (0, 0)),
                  pl.BlockSpec((tm, LANES), lambda b, t: (t, 0)),
                  pl.BlockSpec((tm, LANES), lambda b, t: (t, 0)),
                  pl.BlockSpec((LANES, LANES), lambda b, t: (0, 0))],
        out_specs=[qspec, qspec, pl.BlockSpec((1, n_qb, LANES, qb), lambda b, t: (b, t, 0, 0))],
        compiler_params=_cparams(("parallel", "parallel")),
        name="qg_project_t",
    )(x, g.reshape(1, D), w, gain, cos, sin, bd)


def _nsa_t_kernel(qc_ref, qr_ref, gate_ref, cmp_ref, sel_ref, win_ref, wmap_ref, o_ref,
                  selneg_ref, m_ref, l_ref, acc_ref,
                  *, qb, n_sel, n_selp, n_cmp, q0, w_off, w_rows, l_win):
    qi = pl.program_id(1)
    group = N_HEADS // N_KV
    R = group * qb
    blocks_per_tile = KEY_TILE // SEL_BLOCK
    q_lo = q0 + qi * qb
    qpos_q = q_lo + lax.broadcasted_iota(jnp.int32, (1, qb), 1)
    n_kt = (q_lo + qb - 1) // KEY_TILE + 1
    w_start = pl.multiple_of(jnp.clip(q_lo - WINDOW - w_off, 0, l_win - w_rows), LANES)
    gate = gate_ref[0, 0]

    def lanes4(a):
        return jnp.concatenate([a] * group, axis=1)

    m_idx = lax.broadcasted_iota(jnp.int32, (n_cmp, qb), 0)
    cvalid = (m_idx >= 1) & ((m_idx - 1) * CMP_STRIDE + CMP_LEN - 1 <= qpos_q)
    cbias = lanes4(jnp.where(cvalid, 0.0, NEG_INF))
    any_c = lanes4(qpos_q >= CMP_LEN - 1)
    blk = lax.broadcasted_iota(jnp.int32, (n_selp, qb), 0)
    blk_f = blk.astype(F32)
    cur = qpos_q >> 6
    forced = (blk == 0) | (blk == cur) | (blk == cur - 1)
    reach = blk * SEL_BLOCK <= qpos_q
    real = blk < n_sel
    wpos = w_off + w_start + lax.broadcasted_iota(jnp.int32, (w_rows, qb), 0)
    wbias = lanes4(jnp.where((wpos <= qpos_q) & (wpos > qpos_q - WINDOW) & (wpos >= 0), 0.0, NEG_INF))
    row_t = lax.broadcasted_iota(jnp.int32, (KEY_TILE, qb), 0)

    for k in range(N_KV):
        ckv = cmp_ref[0, k]
        s = jnp.dot(ckv, qc_ref[0, k, 0], preferred_element_type=F32) + cbias
        e = jnp.exp2(s - jnp.max(s, axis=0, keepdims=True))
        den = jnp.maximum(jnp.sum(e, axis=0, keepdims=True), TINY)
        p = e * jnp.where(any_c, 1.0 / den, 0.0)
        oc = _tn_dot(ckv, p.astype(BF16))
        psum = p[:, 0:qb]
        for gi in range(1, group):
            psum = psum + p[:, gi * qb:(gi + 1) * qb]
        hi = psum.astype(BF16)
        lo = (psum - hi.astype(F32)).astype(BF16)
        imp = (jnp.dot(wmap_ref[...], hi, preferred_element_type=F32)
               + jnp.dot(wmap_ref[...], lo, preferred_element_type=F32))
        v = jnp.where(forced, SEL_FORCE, jnp.where(reach, imp, SEL_NEG))
        v = jnp.where(real, v, -jnp.inf)

        def pick_one(_, carry):
            v, sel = carry
            m = jnp.max(v, axis=0, keepdims=True)
            first = jnp.min(jnp.where(v == m, blk_f, float(n_selp)), axis=0, keepdims=True)
            pick = blk_f == first
            sel = jnp.where(pick & (m > 0.5 * SEL_NEG), 0.0, sel)
            return jnp.where(pick, -jnp.inf, v), sel

        _, sel = lax.fori_loop(0, SEL_TOPK, pick_one, (v, jnp.full((n_selp, qb), NEG_INF, F32)))
        selneg_ref[...] = sel

        qr = qr_ref[0, k, 0]
        m_ref[...] = jnp.full(m_ref.shape, NEG_INF, F32)
        l_ref[...] = jnp.zeros(l_ref.shape, F32)
        acc_ref[...] = jnp.zeros(acc_ref.shape, F32)

        def tile(kt, carry):
            start = pl.multiple_of(kt * KEY_TILE, KEY_TILE)
            kv = sel_ref[0, k, pl.ds(start, KEY_TILE), :]
            s = jnp.dot(kv, qr, preferred_element_type=F32)
            pieces = [jnp.broadcast_to(selneg_ref[pl.ds(kt * blocks_per_tile + j, 1), :], (SEL_BLOCK, qb))
                      for j in range(blocks_per_tile)]
            bias = jnp.concatenate(pieces, axis=0) + jnp.where(start + row_t <= qpos_q, 0.0, NEG_INF)
            s = s + lanes4(bias)
            m_old = m_ref[...]
            m_new = jnp.maximum(m_old, jnp.max(s, axis=0, keepdims=True))
            alpha = jnp.exp2(m_old - m_new)
            p = jnp.exp2(s - m_new)
            l_ref[...] = alpha * l_ref[...] + jnp.sum(p, axis=0, keepdims=True)
            acc_ref[...] = alpha * acc_ref[...] + _tn_dot(kv, p.astype(BF16))
            m_ref[...] = m_new
            return carry

        lax.fori_loop(0, n_kt, tile, 0)
        o_s = acc_ref[...] * (1.0 / jnp.maximum(l_ref[...], TINY))

        wkv = win_ref[0, k, pl.ds(w_start, w_rows), :]
        s = jnp.dot(wkv, qr, preferred_element_type=F32) + wbias
        e = jnp.exp2(s - jnp.max(s, axis=0, keepdims=True))
        den = jnp.maximum(jnp.sum(e, axis=0, keepdims=True), TINY)
        o_w = _tn_dot(wkv, e.astype(BF16)) * (1.0 / den)

        for gi in range(group):
            hd = k * group + gi
            sl = slice(gi * qb, (gi + 1) * qb)
            o = gate[3 * hd:3 * hd + 1, :] * oc[:, sl]
            o = o + gate[3 * hd + 1:3 * hd + 2, :] * o_s[:, sl]
            o = o + gate[3 * hd + 2:3 * hd + 3, :] * o_w[:, sl]
            o_ref[0, :, LANES * hd:LANES * (hd + 1)] = o.T.astype(o_ref.dtype)


def nsa_attend_t(qc, qr, gates, cmp_p, sel_p, win_p, T, qb, q0, w_off, n_sel):
    B = qc.shape[0]
    group = N_HEADS // N_KV
    R = group * qb
    n_cmp = cmp_p.shape[2]
    n_selp = -(-n_sel // LANES) * LANES
    l_sel = sel_p.shape[2]
    l_win = win_p.shape[2]
    w_rows = WINDOW + max(qb, LANES)
    assert l_win >= w_rows and l_sel >= ((q0 + T - 1) // KEY_TILE + 1) * KEY_TILE
    wmap = _overlap_map(n_cmp, n_selp)
    kern = functools.partial(_nsa_t_kernel, qb=qb, n_sel=n_sel, n_selp=n_selp, n_cmp=n_cmp,
                             q0=q0, w_off=w_off, w_rows=w_rows, l_win=l_win)
    qspec = pl.BlockSpec((1, N_KV, 1, LANES, R), lambda b, i: (b, 0, i, 0, 0))
    resident = dict(pipeline_mode=pl.Buffered(1))
    return pl.pallas_call(
        kern,
        out_shape=jax.ShapeDtypeStruct((B, T, N_HEADS * LANES), BF16),
        grid=(B, T // qb),
        in_specs=[
            qspec, qspec,
            pl.BlockSpec((1, 1, LANES, qb), lambda b, i: (b, i, 0, 0)),
            pl.BlockSpec((1, N_KV, n_cmp, LANES), lambda b, i: (b, 0, 0, 0)),
            pl.BlockSpec((1, N_KV, l_sel, LANES), lambda b, i: (b, 0, 0, 0), **resident),
            pl.BlockSpec((1, N_KV, l_win, LANES), lambda b, i: (b, 0, 0, 0), **resident),
            pl.BlockSpec((n_selp, n_cmp), lambda b, i: (0, 0)),
        ],
        out_specs=pl.BlockSpec((1, qb, N_HEADS * LANES), lambda b, i: (b, i, 0)),
        scratch_shapes=[
            pltpu.VMEM((n_selp, qb), F32),
            pltpu.VMEM((1, R), F32), pltpu.VMEM((1, R), F32), pltpu.VMEM((LANES, R), F32),
        ],
        compiler_params=_cparams(("parallel", "arbitrary")),
        name="nsa_attend_t",
    )(qc, qr, gates, cmp_p, sel_p, win_p, wmap)


def _nsa_fast_kernel(shift_ref, qc_ref, qr_ref, gate_ref, cmp_ref, sel_ref, win_ref, wmap_ref, hot_ref,
                     o_ref, qaug_ref, oc_ref, l_ref, acc_ref,
                     *, qb, n_sel, n_selp, n_cmp, q0, w_off, w_rows, l_win):
    qi = pl.program_id(1)
    group = N_HEADS // N_KV
    R = group * qb
    q_lo = q0 + qi * qb
    qpos_q = q_lo + lax.broadcasted_iota(jnp.int32, (1, qb), 1)
    n_kt = (q_lo + qb - 1) // KEY_TILE + 1
    w_start = pl.multiple_of(jnp.clip(q_lo - WINDOW - w_off, 0, l_win - w_rows), LANES)
    shift_c = shift_ref[0]
    shift_s = shift_ref[1]
    shift_w = shift_ref[2]

    def lanes4(a):
        return jnp.concatenate([a] * group, axis=1)

    m_idx = lax.broadcasted_iota(jnp.int32, (n_cmp, qb), 0)
    cvalid = (m_idx >= 1) & ((m_idx - 1) * CMP_STRIDE + CMP_LEN - 1 <= qpos_q)
    cbias = lanes4(jnp.where(cvalid, -shift_c, NEG_INF))
    any_c = lanes4(qpos_q >= CMP_LEN - 1)
    blk = lax.broadcasted_iota(jnp.int32, (n_selp, qb), 0)
    blk_f = blk.astype(F32)
    cur = qpos_q >> 6
    forced = (blk == 0) | (blk == cur) | (blk == cur - 1)
    reach = blk * SEL_BLOCK <= qpos_q
    real = blk < n_sel

    imps = []
    for k in range(N_KV):
        ckv = cmp_ref[0, k]
        e = jnp.exp2(jnp.dot(ckv, qc_ref[0, k, 0], preferred_element_type=F32) + cbias)
        den = jnp.maximum(jnp.sum(e, axis=0, keepdims=True), TINY)
        p = e * jnp.where(any_c, 1.0 / den, 0.0)
        oc_ref[k] = _tn_dot(ckv, p.astype(BF16))
        psum = p[:, 0:qb]
        for gi in range(1, group):
            psum = psum + p[:, gi * qb:(gi + 1) * qb]
        hi = psum.astype(BF16)
        lo = (psum - hi.astype(F32)).astype(BF16)
        imp = (jnp.dot(wmap_ref[...], hi, preferred_element_type=F32)
               + jnp.dot(wmap_ref[...], lo, preferred_element_type=F32))
        v = jnp.where(forced, SEL_FORCE, jnp.where(reach, imp, SEL_NEG))
        imps.append(jnp.where(real, v, -jnp.inf))

    def pick_one(_, vs):
        out = []
        for v in vs:
            m = jnp.max(v, axis=0, keepdims=True)
            first = jnp.min(jnp.where(v == m, blk_f, float(n_selp)), axis=0, keepdims=True)
            out.append(jnp.where(blk_f == first, -jnp.inf, v))
        return tuple(out)

    marked = lax.fori_loop(0, SEL_TOPK, pick_one, tuple(imps))
    for k in range(N_KV):
        sel = jnp.where((marked[k] == -jnp.inf) & (imps[k] > 0.5 * SEL_NEG), -shift_s, NEG_INF)
        qaug_ref[k, 0:LANES, :] = qr_ref[0, k, 0]
        qaug_ref[k, LANES:, :] = lanes4(sel).astype(BF16)

    l_ref[...] = jnp.zeros(l_ref.shape, F32)
    acc_ref[...] = jnp.zeros(acc_ref.shape, F32)

    def tile(start, rows, causal_bias):
        hot = hot_ref[pl.ds(start, rows), :]
        for k in range(N_KV):
            kv = sel_ref[0, k, pl.ds(start, rows), :]
            s = jnp.dot(jnp.concatenate([kv, hot], axis=1), qaug_ref[k], preferred_element_type=F32)
            if causal_bias is not None:
                s = s + causal_bias
            p = jnp.exp2(s)
            l_ref[k] += jnp.sum(p.reshape(rows // 8, 8, R), axis=0)
            acc_ref[k] += _tn_dot(kv, p.astype(BF16))

    def full_tile(kt, carry):
        tile(pl.multiple_of(kt * KEY_TILE, KEY_TILE), KEY_TILE, None)
        return carry

    lax.fori_loop(0, n_kt - 1, full_tile, 0)
    d_start = pl.multiple_of((n_kt - 1) * KEY_TILE, KEY_TILE)
    row_t = lax.broadcasted_iota(jnp.int32, (KEY_TILE, qb), 0)
    tile(d_start, KEY_TILE, lanes4(jnp.where(d_start + row_t <= qpos_q, 0.0, NEG_INF)))

    wpos = w_off + w_start + lax.broadcasted_iota(jnp.int32, (w_rows, qb), 0)
    wvalid = (wpos <= qpos_q) & (wpos > qpos_q - WINDOW) & (wpos >= 0)
    wbias = lanes4(jnp.where(wvalid, -shift_w, NEG_INF))
    gate = gate_ref[0, 0]
    for k in range(N_KV):
        wkv = win_ref[0, k, pl.ds(w_start, w_rows), :]
        e = jnp.exp2(jnp.dot(wkv, qr_ref[0, k, 0], preferred_element_type=F32) + wbias)
        den = jnp.maximum(jnp.sum(e, axis=0, keepdims=True), TINY)
        o_w = _tn_dot(wkv, e.astype(BF16)) * (1.0 / den)
        l_s = jnp.maximum(jnp.sum(l_ref[k], axis=0, keepdims=True), TINY)
        o_s = acc_ref[k] * (1.0 / l_s)
        oc = oc_ref[k]
        for gi in range(group):
            hd = k * group + gi
            sl = slice(gi * qb, (gi + 1) * qb)
            o = gate[3 * hd:3 * hd + 1, :] * oc[:, sl]
            o = o + gate[3 * hd + 1:3 * hd + 2, :] * o_s[:, sl]
            o = o + gate[3 * hd + 2:3 * hd + 3, :] * o_w[:, sl]
            o_ref[0, :, LANES * hd:LANES * (hd + 1)] = o.T.astype(o_ref.dtype)


def nsa_attend_fast(shifts, qc, qr, gates, cmp_p, sel_p, win_p, T, qb, q0, w_off, n_sel):
    B = qc.shape[0]
    group = N_HEADS // N_KV
    R = group * qb
    n_cmp = cmp_p.shape[2]
    n_selp = -(-n_sel // LANES) * LANES
    l_sel = sel_p.shape[2]
    l_win = win_p.shape[2]
    w_rows = WINDOW + max(qb, LANES)
    assert l_win >= w_rows and l_sel >= ((q0 + T - 1) // KEY_TILE + 1) * KEY_TILE
    assert n_sel >= SEL_TOPK and q0 % qb == 0 and KEY_TILE % qb == 0
    wmap = _overlap_map(n_cmp, n_selp)
    hot = np.zeros((l_sel, n_selp), np.float32)
    hot[np.arange(l_sel), np.arange(l_sel) // SEL_BLOCK] = 1.0
    hot = jnp.asarray(hot, BF16)
    kern = functools.partial(_nsa_fast_kernel, qb=qb, n_sel=n_sel, n_selp=n_selp, n_cmp=n_cmp,
                             q0=q0, w_off=w_off, w_rows=w_rows, l_win=l_win)
    qspec = pl.BlockSpec((1, N_KV, 1, LANES, R), lambda b, i, sh: (b, 0, i, 0, 0))
    resident = dict(pipeline_mode=pl.Buffered(1))
    grid_spec = pltpu.PrefetchScalarGridSpec(
        num_scalar_prefetch=1,
        grid=(B, T // qb),
        in_specs=[
            qspec, qspec,
            pl.BlockSpec((1, 1, LANES, qb), lambda b, i, sh: (b, i, 0, 0)),
            pl.BlockSpec((1, N_KV, n_cmp, LANES), lambda b, i, sh: (b, 0, 0, 0)),
            pl.BlockSpec((1, N_KV, l_sel, LANES), lambda b, i, sh: (b, 0, 0, 0), **resident),
            pl.BlockSpec((1, N_KV, l_win, LANES), lambda b, i, sh: (b, 0, 0, 0), **resident),
            pl.BlockSpec((n_selp, n_cmp), lambda b, i, sh: (0, 0)),
            pl.BlockSpec((l_sel, n_selp), lambda b, i, sh: (0, 0), **resident),
        ],
        out_specs=pl.BlockSpec((1, qb, N_HEADS * LANES), lambda b, i, sh: (b, i, 0)),
        scratch_shapes=[
            pltpu.VMEM((N_KV, 2 * LANES, R), BF16),
            pltpu.VMEM((N_KV, LANES, R), F32),
            pltpu.VMEM((N_KV, 8, R), F32),
            pltpu.VMEM((N_KV, LANES, R), F32),
        ],
    )
    return pl.pallas_call(
        kern,
        out_shape=jax.ShapeDtypeStruct((B, T, N_HEADS * LANES), BF16),
        grid_spec=grid_spec,
        compiler_params=_cparams(("parallel", "arbitrary")),
        name="nsa_attend_fast",
    )(shifts, qc, qr, gates, cmp_p, sel_p, win_p, wmap, hot)


def _nsa_dec_kernel(shift_ref, pt_ref, *refs, n_pg, qb, n_sel, n_selp, n_cmp, q0, w_off):
    pages = refs[:n_pg]
    (qcp_ref, qbd_ref, gate_ref, cmp_ref, new_ref, win_ref, wmap_ref, fold_ref, hot_ref,
     o_ref, qaug_ref, oc_ref, l_ref, acc_ref) = refs[n_pg:]
    g = pl.program_id(1)
    nk = N_KV * HEAD_DIM
    t_new = new_ref.shape[1]
    lane = lax.broadcasted_iota(jnp.int32, (1, LANES), 1)
    qpos = q0 + (lane & (qb - 1))
    shift_c = shift_ref[0]
    shift_s = shift_ref[1]
    shift_w = shift_ref[2]

    @pl.when(g == 0)
    def _first():
        s = jnp.dot(cmp_ref[0, 0], qcp_ref[0, 0], preferred_element_type=F32)
        for k in range(1, N_KV):
            s = s + jnp.dot(cmp_ref[0, k], qcp_ref[0, k], preferred_element_type=F32)
        m_idx = lax.broadcasted_iota(jnp.int32, (n_cmp, LANES), 0)
        cvalid = (m_idx >= 1) & ((m_idx - 1) * CMP_STRIDE + CMP_LEN - 1 <= qpos)
        e = jnp.exp2(s + jnp.where(cvalid, -shift_c, NEG_INF))
        den = jnp.maximum(jnp.sum(e, axis=0, keepdims=True), TINY)
        p = e * jnp.where(qpos >= CMP_LEN - 1, 1.0 / den, 0.0)
        pb = p.astype(BF16)
        for k in range(N_KV):
            oc_ref[k * HEAD_DIM:(k + 1) * HEAD_DIM, :] = _tn_dot(cmp_ref[0, k], pb)[HEAD_DIM:, :]
        fold = fold_ref[...]
        p_lo = (p - pb.astype(F32)).astype(BF16)
        psum = jnp.dot(pb, fold, preferred_element_type=F32) + jnp.dot(p_lo, fold, preferred_element_type=F32)
        hi = psum.astype(BF16)
        lo = (psum - hi.astype(F32)).astype(BF16)
        imp = (jnp.dot(wmap_ref[...], hi, preferred_element_type=F32)
               + jnp.dot(wmap_ref[...], lo, preferred_element_type=F32))
        blk = lax.broadcasted_iota(jnp.int32, (n_selp, LANES), 0)
        blk_f = blk.astype(F32)
        cur = qpos >> 6
        forced = (blk == 0) | (blk == cur) | (blk == cur - 1)
        v = jnp.where(forced, SEL_FORCE, jnp.where(blk * SEL_BLOCK <= qpos, imp, SEL_NEG))
        v = jnp.where(blk < n_sel, v, -jnp.inf)

        def pick_one(_, carry):
            v, sel = carry
            m = jnp.max(v, axis=0, keepdims=True)
            first = jnp.min(jnp.where(v == m, blk_f, float(n_selp)), axis=0, keepdims=True)
            pick = blk_f == first
            sel = jnp.where(pick & (m > 0.5 * SEL_NEG), -shift_s, sel)
            return jnp.where(pick, -jnp.inf, v), sel

        _, sel = lax.fori_loop(0, SEL_TOPK, pick_one, (v, jnp.full((n_selp, LANES), NEG_INF, F32)))
        qaug_ref[0:nk, :] = qbd_ref[0]
        qaug_ref[nk:, :] = sel.astype(BF16)
        l_ref[...] = jnp.zeros(l_ref.shape, F32)
        acc_ref[...] = jnp.zeros(acc_ref.shape, F32)

    def attend(kx, vx, hot, bias):
        s = jnp.dot(jnp.concatenate([kx, hot], axis=1), qaug_ref[...], preferred_element_type=F32)
        if bias is not None:
            s = s + bias
        p = jnp.exp2(s)
        l_ref[...] += jnp.sum(p.reshape(p.shape[0] // 8, 8, LANES), axis=0)
        acc_ref[...] += _tn_dot(vx, p.astype(BF16))

    per_tile = KEY_TILE // PAGE
    for t in range(n_pg // per_tile):
        x = jnp.concatenate([pages[per_tile * t + i][0] for i in range(per_tile)], axis=0).astype(BF16)
        start = pl.multiple_of((g * (n_pg // per_tile) + t) * KEY_TILE, KEY_TILE)
        attend(x[:, :nk], x[:, nk:], hot_ref[pl.ds(start, KEY_TILE), :], None)

    @pl.when(g == pl.num_programs(1) - 1)
    def _last():
        pad = jnp.zeros((HALO - t_new, 2 * nk), F32)
        row = lax.broadcasted_iota(jnp.int32, (HALO, LANES), 0)
        new_ok = (row < t_new) & (q0 + row <= qpos)
        xn = jnp.concatenate([new_ref[0, :, 2 * nk:4 * nk], pad], axis=0).astype(BF16)
        hot_new = jnp.where(
            lax.broadcasted_iota(jnp.int32, (HALO, n_selp), 1) == q0 // SEL_BLOCK, 1.0, 0.0).astype(BF16)
        attend(xn[:, :nk], xn[:, nk:], hot_new, jnp.where(new_ok, 0.0, NEG_INF))

        qbd = qbd_ref[0]
        xw = win_ref[0].astype(BF16)
        wrow = lax.broadcasted_iota(jnp.int32, (xw.shape[0], LANES), 0)
        wpos = w_off + wrow
        wvalid = (wpos <= qpos) & (wpos > qpos - WINDOW) & (wpos >= 0)
        e_c = jnp.exp2(jnp.dot(xw[:, :nk], qbd, preferred_element_type=F32)
                       + jnp.where(wvalid, -shift_w, NEG_INF))
        xwn = jnp.concatenate([new_ref[0, :, 4 * nk:6 * nk], pad], axis=0).astype(BF16)
        npos = q0 + row
        nvalid = (row < t_new) & (npos <= qpos) & (npos > qpos - WINDOW)
        e_n = jnp.exp2(jnp.dot(xwn[:, :nk], qbd, preferred_element_type=F32)
                       + jnp.where(nvalid, -shift_w, NEG_INF))
        den = jnp.sum(e_c, axis=0, keepdims=True) + jnp.sum(e_n, axis=0, keepdims=True)
        o_w = _tn_dot(xw[:, nk:], e_c.astype(BF16)) + _tn_dot(xwn[:, nk:], e_n.astype(BF16))
        o_w = o_w * (1.0 / jnp.maximum(den, TINY))
        l_s = jnp.maximum(jnp.sum(l_ref[...], axis=0, keepdims=True), TINY)
        o_s = acc_ref[...] * (1.0 / l_s)
        gate = gate_ref[0]
        o_ref[0] = gate[0:1, :] * oc_ref[...] + gate[1:2, :] * o_s + gate[2:3, :] * o_w


def _dec_page_index(b, g, sh, pt, *, i, n_pg):
    return (pt[b, g * n_pg + i], 0, 0)


def nsa_attend_dec(shifts, table, qc, qr, gates, cmp_p, sel_pages, kvp, cache_win, qb, q0, n_sel):
    B, n_pages = table.shape
    group = N_HEADS // N_KV
    assert N_KV * group * qb == LANES
    nk = N_KV * HEAD_DIM
    n_pg = 16 if n_pages % 16 == 0 else n_pages
    n_cmp = cmp_p.shape[2]
    n_selp = -(-n_sel // LANES) * LANES
    wl = cache_win.shape[1]
    eye = jnp.eye(N_KV, dtype=F32)

    def spread(q):
        qt = q[:, :, 0, :, :HEAD_DIM].astype(F32).transpose(0, 1, 3, 2)
        return jnp.einsum("bkdr,kj->bkdjr", qt, eye).reshape(B, N_KV, HEAD_DIM, LANES)

    qc_pad = jnp.pad(spread(qc), ((0, 0), (0, 0), (0, LANES - HEAD_DIM), (0, 0))).astype(BF16)
    q_bd = spread(qr).reshape(B, nk, LANES).astype(BF16)
    gate_l = gates[:, :, :3 * N_HEADS].reshape(B, qb, N_KV, group, 3).transpose(0, 4, 2, 3, 1)
    gate_l = gate_l.reshape(B, 3, LANES)
    wmap = _overlap_map(n_cmp, n_selp)
    lane = np.arange(LANES)
    fold = jnp.asarray((lane[:, None] // (group * qb) == lane[None, :] // (group * qb))
                       & (lane[:, None] % qb == lane[None, :] % qb), BF16)
    l_past = n_pages * PAGE
    hot = np.zeros((l_past, n_selp), np.float32)
    hot[np.arange(l_past), np.arange(l_past) // SEL_BLOCK] = 1.0
    hot = jnp.asarray(hot, BF16)
    kern = functools.partial(_nsa_dec_kernel, n_pg=n_pg, qb=qb, n_sel=n_sel, n_selp=n_selp,
                             n_cmp=n_cmp, q0=q0, w_off=q0 - wl)
    in_specs = [pl.BlockSpec((1, PAGE, 2 * nk), functools.partial(_dec_page_index, i=i, n_pg=n_pg))
                for i in range(n_pg)]
    in_specs += [
        pl.BlockSpec((1, N_KV, LANES, LANES), lambda b, g, sh, pt: (b, 0, 0, 0)),
        pl.BlockSpec((1, nk, LANES), lambda b, g, sh, pt: (b, 0, 0)),
        pl.BlockSpec((1, 3, LANES), lambda b, g, sh, pt: (b, 0, 0)),
        pl.BlockSpec((1, N_KV, n_cmp, LANES), lambda b, g, sh, pt: (b, 0, 0, 0)),
        pl.BlockSpec((1, qb, 6 * nk), lambda b, g, sh, pt: (b, 0, 0)),
        pl.BlockSpec((1, wl, 2 * nk), lambda b, g, sh, pt: (b, 0, 0)),
        pl.BlockSpec((n_selp, n_cmp), lambda b, g, sh, pt: (0, 0)),
        pl.BlockSpec((LANES, LANES), lambda b, g, sh, pt: (0, 0)),
        pl.BlockSpec((l_past, n_selp), lambda b, g, sh, pt: (0, 0), pipeline_mode=pl.Buffered(1)),
    ]
    grid_spec = pltpu.PrefetchScalarGridSpec(
        num_scalar_prefetch=2,
        grid=(B, n_pages // n_pg),
        in_specs=in_specs,
        out_specs=pl.BlockSpec((1, nk, LANES), lambda b, g, sh, pt: (b, 0, 0)),
        scratch_shapes=[
            pltpu.VMEM((nk + n_selp, LANES), BF16),
            pltpu.VMEM((nk, LANES), F32),
            pltpu.VMEM((8, LANES), F32),
            pltpu.VMEM((nk, LANES), F32),
        ],
    )
    o_t = pl.pallas_call(
        kern,
        out_shape=jax.ShapeDtypeStruct((B, nk, LANES), F32),
        grid_spec=grid_spec,
        compiler_params=_cparams(("parallel", "arbitrary")),
        name="nsa_attend_dec",
    )(shifts, table, *([sel_pages] * n_pg), qc_pad, q_bd, gate_l, cmp_p, kvp, cache_win, wmap, fold, hot)
    o6 = o_t.reshape(B, N_KV, HEAD_DIM, N_KV, group, qb)
    o5 = jnp.einsum("bkdkgq->bqkgd", o6)
    return o5.reshape(B * qb, N_HEADS * HEAD_DIM)


def _rope_tables(pos, half):
    inv = jnp.exp(-math.log(ROPE_THETA) * jnp.arange(half, dtype=F32) / half)
    ang = pos.astype(F32)[:, None] * inv[None, :]
    return jnp.cos(ang), jnp.sin(ang)


def _prep_weights(ret_w_in, ret_w_out, ffn_w_in, ffn_w_out, kv_w, kv_knorm, cmp_w1, cmp_w2,
                  nsa_w_qg, nsa_qnorm, nsa_w_o):
    n_b = nsa_w_qg.shape[0]
    nq = N_HEADS * HEAD_DIM
    qg_pad = nq + LANES - nsa_w_qg.shape[2]
    w_qg = jnp.pad(nsa_w_qg, ((0, 0), (0, 0), (0, qg_pad))).astype(BF16)
    w_o = jnp.pad(nsa_w_o.reshape(n_b, N_HEADS, 1, HEAD_DIM, D_MODEL),
                  ((0, 0), (0, 0), (1, 0), (0, 0), (0, 0))).reshape(n_b, N_HEADS * LANES, D_MODEL)
    R = CMP_LEN // CMP_STRIDE
    w1 = cmp_w1.reshape(2, R, CMP_STRIDE, HEAD_DIM, CMP_HID).transpose(0, 2, 3, 1, 4)
    w1 = w1.reshape(2, CMP_STRIDE, HEAD_DIM, R * CMP_HID)
    w1dup = jnp.concatenate([w1, w1], axis=2).astype(BF16)
    w1dup = w1dup.reshape(2, CMP_STRIDE // 2, 2 * LANES, R * CMP_HID)
    z = jnp.zeros((CMP_HID, HEAD_DIM), F32)
    w2p = jnp.stack([jnp.concatenate([cmp_w2[0], z], axis=1),
                     jnp.concatenate([z, cmp_w2[1]], axis=1)]).astype(BF16)
    ones = jnp.ones((HEAD_DIM,), F32)
    return dict(
        ret_w_in=ret_w_in.astype(BF16), ret_w_out=ret_w_out.astype(BF16),
        ffn_w_in=ffn_w_in.astype(BF16), ffn_w_out=ffn_w_out.astype(BF16),
        kv_w=kv_w.astype(BF16), w_qg=w_qg, w_o=w_o.astype(BF16), w_o_raw=nsa_w_o.astype(BF16),
        kv_gain=jnp.tile(kv_knorm, (1, 2)),
        cmp_gain=jnp.concatenate([kv_knorm[0], ones]).reshape(1, LANES),
        q_gain=jnp.tile(nsa_qnorm, (1, 2)) * (HEAD_DIM ** -0.5),
        q_gain2=jnp.tile(nsa_qnorm, (1, 2)) * (HEAD_DIM ** -0.5 * math.log2(math.e)),
        score_bound=(1.05 * HEAD_DIM ** 0.5 * math.log2(math.e))
        * jnp.max(jnp.abs(nsa_qnorm), axis=1)[:, None] * jnp.max(jnp.abs(kv_knorm), axis=1)[None, :],
        w1dup=w1dup, w2p=w2p,
        bd=jnp.asarray(np.kron(np.eye(2), np.ones((HEAD_DIM, HEAD_DIM))), BF16),
    )


def _trunk(x, past_len, ret_s0, conv0, ctx, W, P):
    B, T, D = x.shape
    M = B * T
    depth = P["norm_mix"].shape[0]
    n_a = P["ret_w_in"].shape[0]
    pos = past_len + jnp.arange(T)
    cos_r, sin_r = _rope_tables(pos, RET_DK // 2)
    c32, s32 = _rope_tables(pos, HEAD_DIM // 2)
    cos_n = jnp.tile(c32, (1, 4))
    sin_n = jnp.concatenate([-s32, s32, -s32, s32], axis=1)
    lg = jnp.log1p(-jnp.exp2(-5.0 - jnp.arange(RET_HEADS, dtype=F32)))
    L = RET_CHUNK if T % RET_CHUNK == 0 else T
    gl = jnp.exp(L * lg)
    tm = min(512, M)
    tf = 1024 if T % 1024 == 0 else 512
    tb = min(512, T)
    tq = min(512, T)
    qb = Q_BLOCK if T % Q_BLOCK == 0 else T
    tt = min(256, T)

    x2 = x.reshape(M, D)
    ret_states, conv_states = [], []
    for layer in range(depth):
        if layer == n_a:
            kvp, selp, winp = kv_project(x2.reshape(B, T, D), P["kv_norm"], W["kv_w"], W["kv_gain"],
                                         cos_n, sin_n, W["bd"], tq)
            nk2 = 2 * N_KV * HEAD_DIM
            if ctx is None:
                table = jnp.arange(M // PAGE, dtype=jnp.int32).reshape(B, T // PAGE)
                cmp_p = compress(kvp.reshape(M // PAGE, PAGE, 3 * nk2), table,
                                 W["w1dup"], W["pb"], W["w2p"], W["cmp_gain"])
                sel_p, win_p = selp, winp
                w_off = 0
            else:
                cache_cmp, cache_sel, cache_win, table = ctx
                n_pool = cache_cmp.shape[0]
                cmp_p = compress(cache_cmp.reshape(n_pool, PAGE, nk2), table,
                                 W["w1dup"], W["pb"], W["w2p"], W["cmp_gain"])
                sel_pages = lax.optimization_barrier(cache_sel.reshape(n_pool, PAGE, nk2))
                wl = cache_win.shape[1]
                win_rows = cache_win.reshape(B, wl, nk2)
                w_off = past_len - wl
            n_sel = -(-(past_len + T) // SEL_BLOCK)
        h_norm = P["norm_mix"][layer]
        if layer < n_a:
            proj = norm_matmul(x2, h_norm, W["ret_w_in"][layer], min(1024, M), 1024)
            og, s_new = retention(proj.reshape(B, T, -1), ret_s0[layer], cos_r, sin_r, lg, gl, L, tb)
            ret_states.append(s_new)
            x2 = matmul_res(og.reshape(M, -1), W["ret_w_out"][layer], x2, tm)
        else:
            j = layer - n_a
            if qb % LANES == 0:
                qc, qr, gates = qg_project_t(x2.reshape(B, T, D), h_norm, W["w_qg"][j],
                                             W["q_gain2"][j:j + 1], cos_n, sin_n, W["bd"], tq, qb)
                shifts = W["score_bound"][j]
                args = (qc, qr, gates, cmp_p, sel_p, win_p)
                o = lax.cond(
                    jnp.max(shifts) <= 30.0,
                    lambda a: nsa_attend_fast(shifts, *a, T, qb, past_len, w_off, n_sel),
                    lambda a: nsa_attend_t(*a, T, qb, past_len, w_off, n_sel),
                    args)
                x2 = matmul_res(o.reshape(M, -1), W["w_o"][j], x2, tm)
            else:
                shifts = W["score_bound"][j]
                x3 = x2.reshape(B, T, D)

                def fast(x3, j=j, h_norm=h_norm, shifts=shifts):
                    qc, qr, gates = qg_project(x3, h_norm, W["w_qg"][j], W["q_gain2"][j:j + 1],
                                               cos_n, sin_n, W["bd"], tq, qb)
                    return nsa_attend_dec(shifts, table, qc, qr, gates, cmp_p, sel_pages, kvp,
                                          win_rows, qb, past_len, n_sel)

                def general(x3, j=j, h_norm=h_norm):
                    qc, qr, gates = qg_project(x3, h_norm, W["w_qg"][j], W["q_gain"][j:j + 1],
                                               cos_n, sin_n, W["bd"], tq, qb)
                    sel_p = repack(sel_pages, table, selp)
                    wtab = jnp.arange(B * (wl // PAGE), dtype=jnp.int32).reshape(B, wl // PAGE)
                    win_p = repack(win_rows.reshape(B * (wl // PAGE), PAGE, nk2), wtab, winp)
                    o = nsa_attend(qc, qr, gates, cmp_p, sel_p, win_p, T, qb, past_len, w_off, n_sel)
                    return o.reshape(M, N_HEADS, LANES)[:, :, HEAD_DIM:].reshape(M, N_HEADS * HEAD_DIM)

                o = lax.cond(jnp.max(shifts) <= 30.0, fast, general, x3)
                x2 = matmul_res(o, W["w_o_raw"][j], x2, tm)
        if T % tf == 0:
            act, tail = ffn_in(x2, P["norm_ffn"][layer], W["ffn_w_in"][layer], conv0[layer],
                               P["ffn_conv_w"][layer], P["ffn_conv_b"][layer], T, tf, 256)
            conv_states.append(tail)
        else:
            proj = norm_matmul(x2, P["norm_ffn"][layer], W["ffn_w_in"][layer], tm, 512)
            proj3 = proj.reshape(B, T, 2 * D_FF)
            act = ffn_mid(proj3, conv0[layer], P["ffn_conv_w"][layer], P["ffn_conv_b"][layer], tt)
            conv_states.append(proj3[:, T - 2:, :D_FF])
        x2 = matmul_res(act.reshape(M, D_FF), W["ffn_w_out"][layer], x2, tm)

    nk = N_KV * HEAD_DIM
    new_cmp = kvp[:, :, 0:2 * nk].reshape(B, T, 2, N_KV, HEAD_DIM)
    new_sel = kvp[:, :, 2 * nk:4 * nk].reshape(B, T, 2, N_KV, HEAD_DIM)
    new_win = kvp[:, :, 4 * nk:6 * nk].reshape(B, T, 2, N_KV, HEAD_DIM)
    return (x2.reshape(B, T, D), jnp.stack(ret_states), jnp.stack(conv_states),
            new_cmp, new_sel, new_win)


def kernel(x_prompt, x_sample, cache_cmp_kv, cache_sel_kv, cache_win_kv, state_ret, state_conv,
           page_table, norm_mix, norm_ffn, ret_w_in, ret_w_out, ffn_w_in, ffn_conv_w, ffn_conv_b,
           ffn_w_out, kv_norm, kv_w, kv_knorm, cmp_pos, cmp_w1, cmp_w2, nsa_w_qg, nsa_qnorm, nsa_w_o):
    W = _prep_weights(ret_w_in, ret_w_out, ffn_w_in, ffn_w_out, kv_w, kv_knorm, cmp_w1, cmp_w2,
                      nsa_w_qg, nsa_qnorm, nsa_w_o)
    W["pb"] = pos_bias(cmp_pos, cmp_w1)
    P = dict(norm_mix=norm_mix, norm_ffn=norm_ffn, ret_w_in=ret_w_in, ffn_conv_w=ffn_conv_w,
             ffn_conv_b=ffn_conv_b, kv_norm=kv_norm)
    depth = norm_mix.shape[0]
    n_a = ret_w_in.shape[0]
    B, T, _ = x_prompt.shape
    zero_ret = jnp.zeros((n_a, B, RET_HEADS, RET_DK, RET_DV), F32)
    zero_conv = jnp.zeros((depth, B, 2, D_FF), F32)
    y_p, ret_p, conv_p, cmp_p, sel_p, win_p = _trunk(x_prompt, 0, zero_ret, zero_conv, None, W, P)
    win_p = win_p[:, T - min(WINDOW, T):]

    db, ts, _ = x_sample.shape
    past_len = page_table.shape[1] * PAGE
    ctx = (cache_cmp_kv, cache_sel_kv, cache_win_kv, page_table)
    y_s, ret_s, conv_s, cmp_s, sel_s, win_new = _trunk(x_sample, past_len, state_ret, state_conv,
                                                        ctx, W, P)
    all_win = jnp.concatenate([cache_win_kv, win_new], axis=1)
    win_s = all_win[:, all_win.shape[1] - min(WINDOW, past_len + ts):]
    return (y_p, y_s, ret_p, ret_s, conv_p, conv_s, cmp_p, cmp_s, sel_p, sel_s, win_p, win_s)
```

```python
import functools
import math

import jax
import jax.numpy as jnp
import numpy as np
from jax import lax
from jax.experimental import pallas as pl
from jax.experimental.pallas import tpu as pltpu

F32 = jnp.float32
BF16 = jnp.bfloat16

D_MODEL = 1024
PAGE = 128
RET_HEADS = 4
RET_DK = 256
RET_DV = 512
RET_CHUNK = 128
N_HEADS = 16
N_KV = 4
HEAD_DIM = 64
CMP_LEN = 32
CMP_STRIDE = 16
CMP_HID = 128
SEL_BLOCK = 64
SEL_TOPK = 16
WINDOW = 512
Q_BLOCK = 128
D_FF = 2816
ROPE_THETA = 10000.0
EPS = 1e-6
NEG_INF = -1e30
TINY = 1e-30
SEL_FORCE = 1e6
SEL_NEG = -1e6

LANES = 128
KEY_TILE = 512
HALO = 16
VMEM_LIMIT = 48 * 1024 * 1024


def _cparams(sem):
    return pltpu.CompilerParams(dimension_semantics=sem, vmem_limit_bytes=VMEM_LIMIT)


def _nt_dot(a, b):
    return lax.dot_general(a, b, (((1,), (1,)), ((), ())), preferred_element_type=F32)


def _tn_dot(a, b):
    return lax.dot_general(a, b, (((0,), (0,)), ((), ())), preferred_element_type=F32)


def _gelu(x):
    return 0.5 * x * (1.0 + jnp.tanh(math.sqrt(2.0 / math.pi) * (x + 0.044715 * (x * x * x))))


def _rms_rows(x, g):
    r = lax.rsqrt(jnp.mean(x * x, axis=-1, keepdims=True) + EPS)
    return x * r * g


def _head_ms(x, bd):
    x2 = x * x
    hi = x2.astype(BF16)
    lo = (x2 - hi.astype(F32)).astype(BF16)
    s = jnp.dot(hi, bd, preferred_element_type=F32) + jnp.dot(lo, bd, preferred_element_type=F32)
    return s * (1.0 / HEAD_DIM)


def _rope64(x, cos, sin):
    lane = lax.broadcasted_iota(jnp.int32, x.shape, 1)
    sw = jnp.where((lane & 63) < 32, pltpu.roll(x, 96, 1), pltpu.roll(x, 32, 1))
    return x * cos + sw * sin


def _pack_pair(k2, v2, dtype):
    lane = lax.broadcasted_iota(jnp.int32, k2.shape, 1)
    lo = lane < HEAD_DIM
    even = jnp.where(lo, k2, pltpu.roll(v2, HEAD_DIM, 1)).astype(dtype)
    odd = jnp.where(lo, pltpu.roll(k2, HEAD_DIM, 1), v2).astype(dtype)
    return even, odd


def _norm_matmul_kernel(x_ref, g_ref, w_ref, o_ref, h_ref):
    @pl.when(pl.program_id(1) == 0)
    def _():
        h_ref[...] = _rms_rows(x_ref[...], g_ref[...]).astype(BF16)

    o_ref[...] = jnp.dot(h_ref[...], w_ref[...], preferred_element_type=F32).astype(o_ref.dtype)


def norm_matmul(x, g, w, tm, tn):
    M, D = x.shape
    N = w.shape[1]
    return pl.pallas_call(
        _norm_matmul_kernel,
        out_shape=jax.ShapeDtypeStruct((M, N), F32),
        grid=(M // tm, N // tn),
        in_specs=[pl.BlockSpec((tm, D), lambda i, j: (i, 0)),
                  pl.BlockSpec((1, D), lambda i, j: (0, 0)),
                  pl.BlockSpec((D, tn), lambda i, j: (0, j))],
        out_specs=pl.BlockSpec((tm, tn), lambda i, j: (i, j)),
        scratch_shapes=[pltpu.VMEM((tm, D), BF16)],
        compiler_params=_cparams(("parallel", "arbitrary")),
        name="norm_matmul",
    )(x, g.reshape(1, D), w)


def _matmul_res_kernel(a_ref, w_ref, r_ref, o_ref):
    o_ref[...] = r_ref[...] + jnp.dot(a_ref[...].astype(BF16), w_ref[...],
                                      preferred_element_type=F32)


def matmul_res(a, w, res, tm):
    M, K = a.shape
    N = w.shape[1]
    return pl.pallas_call(
        _matmul_res_kernel,
        out_shape=jax.ShapeDtypeStruct((M, N), F32),
        grid=(M // tm,),
        in_specs=[pl.BlockSpec((tm, K), lambda i: (i, 0)),
                  pl.BlockSpec((K, N), lambda i: (0, 0)),
                  pl.BlockSpec((tm, N), lambda i: (i, 0))],
        out_specs=pl.BlockSpec((tm, N), lambda i: (i, 0)),
        compiler_params=_cparams(("parallel",)),
        name="matmul_res",
    )(a, w, res)


def _retention_kernel(lg_ref, gl_ref, q_ref, k_ref, v_ref, g_ref, cos_ref, sin_ref, s0_ref,
                      o_ref, sout_ref, S_ref, *, L, n_chunk):
    h = pl.program_id(1)
    t = pl.program_id(2)
    lg = lg_ref[h]
    gl = gl_ref[h]

    @pl.when(t == 0)
    def _():
        S_ref[...] = s0_ref[0, 0]

    ii = lax.broadcasted_iota(jnp.int32, (L, L), 0)
    jj = lax.broadcasted_iota(jnp.int32, (L, L), 1)
    diff = (ii - jj).astype(F32)
    decay = jnp.where(diff >= 0, jnp.exp(jnp.maximum(diff, 0.0) * lg), 0.0)
    idx = lax.broadcasted_iota(jnp.int32, (L, 1), 0).astype(F32)
    q_dec = jnp.exp((idx + 1.0) * lg)
    k_dec = jnp.exp((L - 1.0 - idx) * lg)
    half = RET_DK // 2

    for c in range(n_chunk):
        rows = pl.ds(c * L, L)
        cos = cos_ref[rows, :]
        sin = sin_ref[rows, :]

        def rope(x):
            x1, x2 = x[:, :half], x[:, half:]
            return jnp.concatenate([x1 * cos - x2 * sin, x2 * cos + x1 * sin], axis=1)

        qr = rope(q_ref[0, rows, :])
        kr = rope(k_ref[0, rows, :]) * (RET_DK ** -0.5)
        qb = qr.astype(BF16)
        vb = v_ref[0, rows, :].astype(BF16)
        sc = _nt_dot(qb, kr.astype(BF16)) * decay
        S = S_ref[...]
        o = jnp.dot(sc.astype(BF16), vb, preferred_element_type=F32)
        o = o + jnp.dot(qb, S.astype(BF16), preferred_element_type=F32) * q_dec
        S_ref[...] = S * gl + _tn_dot((kr * k_dec).astype(BF16), vb)
        on = o * lax.rsqrt(jnp.mean(o * o, axis=-1, keepdims=True) + EPS)
        g = g_ref[0, rows, :]
        o_ref[0, rows, :] = (on * (g * jax.nn.sigmoid(g))).astype(o_ref.dtype)

    @pl.when(t == pl.num_programs(2) - 1)
    def _():
        sout_ref[0, 0] = S_ref[...]


def retention(proj, s0, cos, sin, lg, gl, L, tb):
    B, T, _ = proj.shape
    n_chunk = tb // L
    odt = BF16 if tb % 16 == 0 else F32
    kern = functools.partial(_retention_kernel, L=L, n_chunk=n_chunk)
    grid_spec = pltpu.PrefetchScalarGridSpec(
        num_scalar_prefetch=2,
        grid=(B, RET_HEADS, T // tb),
        in_specs=[
            pl.BlockSpec((1, tb, RET_DK), lambda b, h, t, *_: (b, t, h)),
            pl.BlockSpec((1, tb, RET_DK), lambda b, h, t, *_: (b, t, RET_HEADS + h)),
            pl.BlockSpec((1, tb, RET_DV), lambda b, h, t, *_: (b, t, RET_HEADS + h)),
            pl.BlockSpec((1, tb, RET_DV), lambda b, h, t, *_: (b, t, 2 * RET_HEADS + h)),
            pl.BlockSpec((tb, RET_DK // 2), lambda b, h, t, *_: (t, 0)),
            pl.BlockSpec((tb, RET_DK // 2), lambda b, h, t, *_: (t, 0)),
            pl.BlockSpec((1, 1, RET_DK, RET_DV), lambda b, h, t, *_: (b, h, 0, 0)),
        ],
        out_specs=[
            pl.BlockSpec((1, tb, RET_DV), lambda b, h, t, *_: (b, t, h)),
            pl.BlockSpec((1, 1, RET_DK, RET_DV), lambda b, h, t, *_: (b, h, 0, 0)),
        ],
        scratch_shapes=[pltpu.VMEM((RET_DK, RET_DV), F32)],
    )
    return pl.pallas_call(
        kern,
        out_shape=[jax.ShapeDtypeStruct((B, T, RET_HEADS * RET_DV), odt),
                   jax.ShapeDtypeStruct((B, RET_HEADS, RET_DK, RET_DV), F32)],
        grid_spec=grid_spec,
        compiler_params=_cparams(("parallel", "parallel", "arbitrary")),
        name="retention",
    )(lg, gl, proj, proj, proj, proj, cos, sin, s0)


def _ffn_mid_kernel(u_ref, gt_ref, halo_ref, cw_ref, cb_ref, o_ref):
    u = u_ref[0]
    hl = halo_ref[0, 0]
    row = lax.broadcasted_iota(jnp.int32, u.shape, 0)
    u1 = jnp.where(row == 0, hl[1:2], pltpu.roll(u, 1, 0))
    u2 = jnp.where(row == 0, hl[0:1], jnp.where(row == 1, hl[1:2], pltpu.roll(u, 2, 0)))
    c = cb_ref[...] + cw_ref[0:1] * u2
    c = c + cw_ref[1:2] * u1
    c = c + cw_ref[2:3] * u
    o_ref[0] = (_gelu(c) * gt_ref[0]).astype(o_ref.dtype)


def ffn_mid(proj, buf, conv_w, conv_b, tt):
    B, T, _ = proj.shape
    nt = T // tt
    if nt > 1:
        tails = proj[:, :, :D_FF].reshape(B, nt, tt, D_FF)[:, :-1, tt - 2:, :]
        halo = jnp.concatenate([buf[:, None], tails], axis=1)
    else:
        halo = buf[:, None]
    odt = BF16 if tt % 16 == 0 else F32
    return pl.pallas_call(
        _ffn_mid_kernel,
        out_shape=jax.ShapeDtypeStruct((B, T, D_FF), odt),
        grid=(B, nt),
        in_specs=[pl.BlockSpec((1, tt, D_FF), lambda b, t: (b, t, 0)),
                  pl.BlockSpec((1, tt, D_FF), lambda b, t: (b, t, 1)),
                  pl.BlockSpec((1, 1, 2, D_FF), lambda b, t: (b, t, 0, 0)),
                  pl.BlockSpec((3, D_FF), lambda b, t: (0, 0)),
                  pl.BlockSpec((1, D_FF), lambda b, t: (0, 0))],
        out_specs=pl.BlockSpec((1, tt, D_FF), lambda b, t: (b, t, 0)),
        compiler_params=_cparams(("parallel", "parallel")),
        name="ffn_mid",
    )(proj, proj, halo, conv_w, conv_b.reshape(1, D_FF))


def _ffn_in_kernel(x_ref, xh_ref, g_ref, wu_ref, wg_ref, buf_ref, cw_ref, cb_ref,
                   act_ref, tail_ref, h_ref, hh_ref, *, tiles_per_seq):
    i = pl.program_id(0)

    @pl.when(pl.program_id(1) == 0)
    def _():
        h_ref[...] = _rms_rows(x_ref[...], g_ref[...]).astype(BF16)
        hh_ref[...] = _rms_rows(xh_ref[...], g_ref[...]).astype(BF16)

    h = h_ref[...]
    u = jnp.dot(h, wu_ref[...], preferred_element_type=F32)
    gt = jnp.dot(h, wg_ref[...], preferred_element_type=F32)
    uh = jnp.dot(hh_ref[...], wu_ref[...], preferred_element_type=F32)
    seq_start = (i % tiles_per_seq) == 0
    hl = jnp.where(seq_start, buf_ref[0], uh[HALO - 2:, :])
    row = lax.broadcasted_iota(jnp.int32, u.shape, 0)
    u1 = jnp.where(row == 0, hl[1:2], pltpu.roll(u, 1, 0))
    u2 = jnp.where(row == 0, hl[0:1], jnp.where(row == 1, hl[1:2], pltpu.roll(u, 2, 0)))
    c = cb_ref[...] + cw_ref[0:1] * u2
    c = c + cw_ref[1:2] * u1
    c = c + cw_ref[2:3] * u
    act_ref[...] = (_gelu(c) * gt).astype(act_ref.dtype)
    tail_ref[0] = u[u.shape[0] - 2:, :]


def ffn_in(x, g, w, buf, conv_w, conv_b, T, tm, tn):
    M, D = x.shape
    B = M // T
    n_col = D_FF // tn
    tiles_per_seq = T // tm
    kern = functools.partial(_ffn_in_kernel, tiles_per_seq=tiles_per_seq)
    act, tails = pl.pallas_call(
        kern,
        out_shape=[jax.ShapeDtypeStruct((M, D_FF), BF16),
                   jax.ShapeDtypeStruct((M // tm, 2, D_FF), F32)],
        grid=(M // tm, n_col),
        in_specs=[pl.BlockSpec((tm, D), lambda i, j: (i, 0)),
                  pl.BlockSpec((HALO, D), lambda i, j: (jnp.maximum(i * (tm // HALO) - 1, 0), 0)),
                  pl.BlockSpec((1, D), lambda i, j: (0, 0)),
                  pl.BlockSpec((D, tn), lambda i, j: (0, j)),
                  pl.BlockSpec((D, tn), lambda i, j: (0, n_col + j)),
                  pl.BlockSpec((1, 2, tn), lambda i, j: (i // tiles_per_seq, 0, j)),
                  pl.BlockSpec((3, tn), lambda i, j: (0, j)),
                  pl.BlockSpec((1, tn), lambda i, j: (0, j))],
        out_specs=[pl.BlockSpec((tm, tn), lambda i, j: (i, j)),
                   pl.BlockSpec((1, 2, tn), lambda i, j: (i, 0, j))],
        scratch_shapes=[pltpu.VMEM((tm, D), BF16), pltpu.VMEM((HALO, D), BF16)],
        compiler_params=_cparams(("parallel", "arbitrary")),
        name="ffn_in",
    )(x, x, g.reshape(1, D), w, w, buf, conv_w, conv_b.reshape(1, D_FF))
    return act, tails[tiles_per_seq - 1::tiles_per_seq]


def _kv_kernel(x_ref, g_ref, w_ref, gain_ref, cos_ref, sin_ref, bd_ref,
               kv_ref, selp_ref, winp_ref):
    h = _rms_rows(x_ref[0], g_ref[...]).astype(BF16)
    y = jnp.dot(h, w_ref[...], preferred_element_type=F32)
    cos = cos_ref[...]
    sin = sin_ref[...]
    bd = bd_ref[...]
    nk = N_KV * HEAD_DIM
    kv_ref[0, :, 0:2 * nk] = y[:, 0:2 * nk]
    for br, pack_ref in enumerate((selp_ref, winp_ref)):
        base = 2 * nk * (br + 1)
        kv_ref[0, :, base + nk:base + 2 * nk] = y[:, base + nk:base + 2 * nk]
        for p in range(2):
            kx = y[:, base + LANES * p:base + LANES * (p + 1)]
            kn = kx * lax.rsqrt(_head_ms(kx, bd) + EPS) * gain_ref[br + 1:br + 2, :]
            kr = _rope64(kn, cos, sin)
            kv_ref[0, :, base + LANES * p:base + LANES * (p + 1)] = kr
            vx = y[:, base + nk + LANES * p:base + nk + LANES * (p + 1)]
            even, odd = _pack_pair(kr, vx, pack_ref.dtype)
            pack_ref[0, 2 * p] = even
            pack_ref[0, 2 * p + 1] = odd


def kv_project(x, g, w, gains, cos, sin, bd, tm):
    B, T, D = x.shape
    N = w.shape[1]
    pdt = BF16 if tm % 16 == 0 else F32
    return pl.pallas_call(
        _kv_kernel,
        out_shape=[jax.ShapeDtypeStruct((B, T, N), F32),
                   jax.ShapeDtypeStruct((B, N_KV, T, LANES), pdt),
                   jax.ShapeDtypeStruct((B, N_KV, T, LANES), pdt)],
        grid=(B, T // tm),
        in_specs=[pl.BlockSpec((1, tm, D), lambda b, t: (b, t, 0)),
                  pl.BlockSpec((1, D), lambda b, t: (0, 0)),
                  pl.BlockSpec((D, N), lambda b, t: (0, 0)),
                  pl.BlockSpec((3, LANES), lambda b, t: (0, 0)),
                  pl.BlockSpec((tm, LANES), lambda b, t: (t, 0)),
                  pl.BlockSpec((tm, LANES), lambda b, t: (t, 0)),
                  pl.BlockSpec((LANES, LANES), lambda b, t: (0, 0))],
        out_specs=[pl.BlockSpec((1, tm, N), lambda b, t: (b, t, 0)),
                   pl.BlockSpec((1, N_KV, tm, LANES), lambda b, t: (b, 0, t, 0)),
                   pl.BlockSpec((1, N_KV, tm, LANES), lambda b, t: (b, 0, t, 0))],
        compiler_params=_cparams(("parallel", "parallel")),
        name="kv_project",
    )(x, g.reshape(1, D), w, gains, cos, sin, bd)


def _qg_kernel(x_ref, g_ref, w_ref, gain_ref, cos_ref, sin_ref, bd_ref,
               qc_ref, qr_ref, gate_ref, *, qb, n_qb):
    h = _rms_rows(x_ref[0], g_ref[...]).astype(BF16)
    y = jnp.dot(h, w_ref[...], preferred_element_type=F32)
    cos = cos_ref[...]
    sin = sin_ref[...]
    bd = bd_ref[...]
    nq = N_HEADS * HEAD_DIM
    gate_ref[0] = jax.nn.sigmoid(y[:, nq:nq + LANES])
    lane = lax.broadcasted_iota(jnp.int32, (y.shape[0], LANES), 1)
    lo = lane < HEAD_DIM
    group = N_HEADS // N_KV
    for p in range(N_HEADS // 2):
        qx = y[:, LANES * p:LANES * (p + 1)]
        qn = qx * lax.rsqrt(_head_ms(qx, bd) + EPS) * gain_ref[...]
        qr = _rope64(qn, cos, sin)
        for src, dst in ((qn, qc_ref), (qr, qr_ref)):
            for par in range(2):
                hd = 2 * p + par
                kvh, gi = hd // group, hd % group
                v = src if par == 0 else pltpu.roll(src, HEAD_DIM, 1)
                v = jnp.where(lo, v, 0.0).astype(dst.dtype)
                for j in range(n_qb):
                    dst[0, kvh, j, gi * qb:(gi + 1) * qb, :] = v[j * qb:(j + 1) * qb, :]


def qg_project(x, g, w, gain, cos, sin, bd, tm, qb):
    B, T, D = x.shape
    N = w.shape[1]
    n_qb = tm // qb
    group = N_HEADS // N_KV
    qdt = BF16 if qb % 16 == 0 else F32
    kern = functools.partial(_qg_kernel, qb=qb, n_qb=n_qb)
    qshape = jax.ShapeDtypeStruct((B, N_KV, T // qb, group * qb, LANES), qdt)
    qspec = pl.BlockSpec((1, N_KV, n_qb, group * qb, LANES), lambda b, t: (b, 0, t, 0, 0))
    return pl.pallas_call(
        kern,
        out_shape=[qshape, qshape, jax.ShapeDtypeStruct((B, T, LANES), F32)],
        grid=(B, T // tm),
        in_specs=[pl.BlockSpec((1, tm, D), lambda b, t: (b, t, 0)),
                  pl.BlockSpec((1, D), lambda b, t: (0, 0)),
                  pl.BlockSpec((D, N), lambda b, t: (0, 0)),
                  pl.BlockSpec((1, LANES), lambda b, t: (0, 0)),
                  pl.BlockSpec((tm, LANES), lambda b, t: (t, 0)),
                  pl.BlockSpec((tm, LANES), lambda b, t: (t, 0)),
                  pl.BlockSpec((LANES, LANES), lambda b, t: (0, 0))],
        out_specs=[qspec, qspec, pl.BlockSpec((1, tm, LANES), lambda b, t: (b, t, 0))],
        compiler_params=_cparams(("parallel", "parallel")),
        name="qg_project",
    )(x, g.reshape(1, D), w, gain, cos, sin, bd)


def _pos_bias_kernel(p_ref, w_ref, o_ref):
    o_ref[0] = jnp.dot(p_ref[0].astype(BF16), w_ref[0].astype(BF16), preferred_element_type=F32)


def pos_bias(cmp_pos, cmp_w1):
    K = CMP_LEN * HEAD_DIM
    p = jnp.broadcast_to(cmp_pos.reshape(2, 1, K), (2, 8, K))
    out = pl.pallas_call(
        _pos_bias_kernel,
        out_shape=jax.ShapeDtypeStruct((2, 8, CMP_HID), F32),
        grid=(2,),
        in_specs=[pl.BlockSpec((1, 8, K), lambda c: (c, 0, 0)),
                  pl.BlockSpec((1, K, CMP_HID), lambda c: (c, 0, 0))],
        out_specs=pl.BlockSpec((1, 8, CMP_HID), lambda c: (c, 0, 0)),
        name="pos_bias",
    )(p, cmp_w1)
    return out[:, 0, :]


def _compress_kernel(pt_ref, *refs, n_pg):
    pages = refs[:n_pg]
    w1_ref, pb_ref, w2_ref, gain_ref, out_ref, carry_ref, slab_ref = refs[n_pg:]
    g = pl.program_id(1)

    @pl.when(g == 0)
    def _():
        carry_ref[...] = jnp.zeros_like(carry_ref)

    n = n_pg * (PAGE // CMP_STRIDE)
    lane = lax.broadcasted_iota(jnp.int32, (n, LANES), 1)
    row = lax.broadcasted_iota(jnp.int32, (n, LANES), 0)
    lo = lane < HEAD_DIM
    for pair in range(N_KV // 2):
        res = [jnp.zeros((n, LANES), F32), jnp.zeros((n, LANES), F32)]
        for c in range(2):
            col = c * N_KV * HEAD_DIM + pair * LANES
            for i in range(n_pg):
                slab_ref[PAGE * i:PAGE * (i + 1), :] = pages[i][0, :, col:col + LANES]
            for par in range(2):
                keep = lo if par == 0 else jnp.logical_not(lo)
                acc = jnp.zeros((n, 2 * CMP_HID), F32)
                for s in range(0, CMP_STRIDE, 2):
                    xm = jnp.concatenate(
                        [jnp.where(keep, slab_ref[pl.ds(s + i, n, stride=CMP_STRIDE), :], 0.0)
                         for i in range(2)], axis=1).astype(BF16)
                    acc = acc + jnp.dot(xm, w1_ref[c, s // 2], preferred_element_type=F32)
                p0 = acc[:, :CMP_HID]
                p1 = acc[:, CMP_HID:]
                ci = 2 * (2 * pair + par) + c
                prev = jnp.where(row == 0, carry_ref[ci, 7:8, :], pltpu.roll(p0, 1, 0))
                carry_ref[ci] = p0[n - 8:n, :]
                hid = _gelu(prev + p1 + pb_ref[c:c + 1, :]).astype(BF16)
                res[par] = res[par] + jnp.dot(hid, w2_ref[c], preferred_element_type=F32)
        for par in range(2):
            r = res[par]
            ms = jnp.sum(jnp.where(lo, r * r, 0.0), axis=-1, keepdims=True) * (1.0 / HEAD_DIM)
            kn = r * lax.rsqrt(ms + EPS) * gain_ref[...]
            out_ref[0, 2 * pair + par] = jnp.where(lo, kn, r).astype(out_ref.dtype)


def _page_index(b, g, pt_ref, *, i, n_pg):
    return (pt_ref[b, g * n_pg + i], 0, 0)


def compress(pages_arr, table, w1dup, pb, w2p, gain):
    B, n_pages = table.shape
    n_pg = 16 if n_pages % 16 == 0 else n_pages
    n = n_pg * (PAGE // CMP_STRIDE)
    n_sub = n_pages * (PAGE // CMP_STRIDE)
    width = 2 * N_KV * HEAD_DIM
    in_specs = [pl.BlockSpec((1, PAGE, width), functools.partial(_page_index, i=i, n_pg=n_pg))
                for i in range(n_pg)]
    in_specs += [pl.BlockSpec(w1dup.shape, lambda b, g, pt: (0, 0, 0, 0)),
                 pl.BlockSpec(pb.shape, lambda b, g, pt: (0, 0)),
                 pl.BlockSpec(w2p.shape, lambda b, g, pt: (0, 0, 0)),
                 pl.BlockSpec((1, LANES), lambda b, g, pt: (0, 0))]
    grid_spec = pltpu.PrefetchScalarGridSpec(
        num_scalar_prefetch=1,
        grid=(B, n_pages // n_pg),
        in_specs=in_specs,
        out_specs=pl.BlockSpec((1, N_KV, n, LANES), lambda b, g, pt: (b, 0, g, 0)),
        scratch_shapes=[pltpu.VMEM((2 * N_KV, 8, CMP_HID), F32),
                        pltpu.VMEM((n_pg * PAGE, LANES), F32)],
    )
    return pl.pallas_call(
        functools.partial(_compress_kernel, n_pg=n_pg),
        out_shape=jax.ShapeDtypeStruct((B, N_KV, n_sub, LANES), BF16),
        grid_spec=grid_spec,
        compiler_params=_cparams(("parallel", "arbitrary")),
        name="compress",
    )(table, *([pages_arr] * n_pg), w1dup, pb, w2p, gain)


def _repack_kernel(pt_ref, p0, p1, p2, p3, new_ref, out_ref, *, n_full, t_new):
    t = pl.program_id(1)
    nk = N_KV * HEAD_DIM

    @pl.when(t < n_full)
    def _():
        for i, pg in enumerate((p0, p1, p2, p3)):
            x = pg[0]
            for p in range(2):
                even, odd = _pack_pair(x[:, LANES * p:LANES * (p + 1)],
                                       x[:, nk + LANES * p:nk + LANES * (p + 1)], out_ref.dtype)
                out_ref[0, 2 * p, PAGE * i:PAGE * (i + 1), :] = even
                out_ref[0, 2 * p + 1, PAGE * i:PAGE * (i + 1), :] = odd

    @pl.when(t == n_full)
    def _():
        pad = jnp.zeros((N_KV, KEY_TILE - t_new, LANES), F32)
        out_ref[0] = jnp.concatenate([new_ref[0].astype(F32), pad], axis=1).astype(out_ref.dtype)


def _repack_page_index(b, t, pt_ref, *, i, n_pages):
    return (pt_ref[b, jnp.minimum(4 * t + i, n_pages - 1)], 0, 0)


def repack(pages_arr, table, new_pack):
    B, n_pages = table.shape
    n_full = n_pages // 4
    t_new = new_pack.shape[2]
    width = 2 * N_KV * HEAD_DIM
    in_specs = [pl.BlockSpec((1, PAGE, width),
                             functools.partial(_repack_page_index, i=i, n_pages=n_pages))
                for i in range(4)]
    in_specs.append(pl.BlockSpec((1, N_KV, t_new, LANES), lambda b, t, pt: (b, 0, 0, 0)))
    grid_spec = pltpu.PrefetchScalarGridSpec(
        num_scalar_prefetch=1,
        grid=(B, n_full + 1),
        in_specs=in_specs,
        out_specs=pl.BlockSpec((1, N_KV, KEY_TILE, LANES), lambda b, t, pt: (b, 0, t, 0)),
    )
    return pl.pallas_call(
        functools.partial(_repack_kernel, n_full=n_full, t_new=t_new),
        out_shape=jax.ShapeDtypeStruct((B, N_KV, (n_full + 1) * KEY_TILE, LANES), BF16),
        grid_spec=grid_spec,
        compiler_params=_cparams(("parallel", "arbitrary")),
        name="repack",
    )(table, pages_arr, pages_arr, pages_arr, pages_arr, new_pack)


def _softmax_step(s, mask, kv, m_ref, l_ref, a_ref, k):
    s = jnp.where(mask, s, NEG_INF)
    m_old = m_ref[k]
    m_new = jnp.maximum(m_old, jnp.max(s, axis=-1, keepdims=True))
    alpha = jnp.exp(m_old - m_new)
    p = jnp.where(mask, jnp.exp(s - m_new), 0.0)
    l_ref[k] = alpha * l_ref[k] + jnp.sum(p, axis=-1, keepdims=True)
    a_ref[k] = alpha * a_ref[k] + jnp.dot(p.astype(BF16), kv, preferred_element_type=F32)
    m_ref[k] = m_new


def _nsa_kernel(qi_ref, kt_ref, wt_ref, wf_ref, last_ref,
                qc_ref, qr_ref, gate_ref, cmp_ref, sel_ref, win_ref, wmap_ref,
                o_ref,
                selm_ref, oc_ref, ms_ref, ls_ref, as_ref, mw_ref, lw_ref, aw_ref,
                *, qb, n_sel, n_selp, n_cmp, q0, w_off):
    step = pl.program_id(1)
    qi = qi_ref[step]
    kt = kt_ref[step]
    group = N_HEADS // N_KV
    R = group * qb
    rowq = lax.broadcasted_iota(jnp.int32, (R, 1), 0) & (qb - 1)
    qpos = q0 + qi * qb + rowq
    qpos_q = q0 + qi * qb + lax.broadcasted_iota(jnp.int32, (qb, 1), 0)

    @pl.when(kt == 0)
    def _first():
        for m_ref, l_ref, a_ref in ((ms_ref, ls_ref, as_ref), (mw_ref, lw_ref, aw_ref)):
            m_ref[...] = jnp.full(m_ref.shape, NEG_INF, F32)
            l_ref[...] = jnp.zeros(l_ref.shape, F32)
            a_ref[...] = jnp.zeros(a_ref.shape, F32)
        n_idx = lax.broadcasted_iota(jnp.int32, (1, n_cmp), 1)
        cvalid = (n_idx >= 1) & ((n_idx - 1) * CMP_STRIDE + CMP_LEN - 1 <= qpos)
        blk = lax.broadcasted_iota(jnp.int32, (qb, n_selp), 1)
        blk_f = blk.astype(F32)
        cur = qpos_q >> 6
        forced = (blk == 0) | (blk == cur) | (blk == cur - 1)
        reach = blk * SEL_BLOCK <= qpos_q
        real = blk < n_sel
        for k in range(N_KV):
            ckv = cmp_ref[0, k]
            s = _nt_dot(qc_ref[0, k, 0].astype(BF16), ckv)
            s = jnp.where(cvalid, s, NEG_INF)
            m = jnp.max(s, axis=-1, keepdims=True)
            e = jnp.where(cvalid, jnp.exp(s - m), 0.0)
            p = e / jnp.maximum(jnp.sum(e, axis=-1, keepdims=True), TINY)
            oc_ref[k] = jnp.dot(p.astype(BF16), ckv, preferred_element_type=F32)
            psum = p[0:qb]
            for gi in range(1, group):
                psum = psum + p[gi * qb:(gi + 1) * qb]
            hi = psum.astype(BF16)
            lo = (psum - hi.astype(F32)).astype(BF16)
            imp = _nt_dot(hi, wmap_ref[...]) + _nt_dot(lo, wmap_ref[...])
            v = jnp.where(forced, SEL_FORCE, jnp.where(reach, imp, SEL_NEG))
            v = jnp.where(real, v, -jnp.inf)

            def pick_one(_, carry):
                v, sel = carry
                m = jnp.max(v, axis=-1, keepdims=True)
                first = jnp.min(jnp.where(v == m, blk_f, float(n_selp)), axis=-1, keepdims=True)
                pick = blk_f == first
                sel = jnp.where(pick & (m > 0.5 * SEL_NEG), 1.0, sel)
                return jnp.where(pick, -jnp.inf, v), sel

            _, sel = lax.fori_loop(0, SEL_TOPK, pick_one, (v, jnp.zeros((qb, n_selp), F32)))
            selm_ref[k] = sel

    tok = kt * KEY_TILE + lax.broadcasted_iota(jnp.int32, (1, KEY_TILE), 1)
    blk_of_tok = tok >> 6
    expand = jnp.where(
        lax.broadcasted_iota(jnp.int32, (n_selp, KEY_TILE), 0) == blk_of_tok, 1.0, 0.0).astype(BF16)
    causal = tok <= qpos
    for k in range(N_KV):
        kv = sel_ref[0, k]
        s = _nt_dot(qr_ref[0, k, 0].astype(BF16), kv)
        sm = jnp.dot(selm_ref[k].astype(BF16), expand, preferred_element_type=F32)
        sm = jnp.concatenate([sm] * group, axis=0)
        _softmax_step(s, causal & (sm > 0.5), kv, ms_ref, ls_ref, as_ref, k)

    @pl.when(wf_ref[step] == 1)
    def _window():
        wpos = w_off + wt_ref[step] * KEY_TILE + lax.broadcasted_iota(jnp.int32, (1, KEY_TILE), 1)
        mask = (wpos <= qpos) & (wpos > qpos - WINDOW) & (wpos >= 0)
        for k in range(N_KV):
            kv = win_ref[0, k]
            s = _nt_dot(qr_ref[0, k, 0].astype(BF16), kv)
            _softmax_step(s, mask, kv, mw_ref, lw_ref, aw_ref, k)

    @pl.when(last_ref[step] == 1)
    def _finish():
        gate = gate_ref[0]
        for k in range(N_KV):
            for gi in range(group):
                hd = k * group + gi
                rs = slice(gi * qb, (gi + 1) * qb)
                o_s = as_ref[k, rs, :] / jnp.maximum(ls_ref[k, rs, :], TINY)
                o_w = aw_ref[k, rs, :] / jnp.maximum(lw_ref[k, rs, :], TINY)
                o = gate[:, 3 * hd:3 * hd + 1] * oc_ref[k, rs, :]
                o = o + gate[:, 3 * hd + 1:3 * hd + 2] * o_s
                o = o + gate[:, 3 * hd + 2:3 * hd + 3] * o_w
                o_ref[0, :, LANES * hd:LANES * (hd + 1)] = o.astype(o_ref.dtype)


def _nsa_tables(T, qb, q0, w_off):
    rows = []
    for qi in range(T // qb):
        q_lo = q0 + qi * qb
        q_hi = q_lo + qb - 1
        last_kt = q_hi // KEY_TILE
        w_lo = max(q_lo - WINDOW + 1, w_off)
        wt0 = (w_lo - w_off) // KEY_TILE
        wt1 = (q_hi - w_off) // KEY_TILE
        n_w = wt1 - wt0 + 1
        assert n_w <= last_kt + 1
        for kt in range(last_kt + 1):
            rows.append((qi, kt, wt0 + min(kt, n_w - 1), int(kt < n_w), int(kt == last_kt)))
    tab = np.asarray(rows, np.int32).T
    return [jnp.asarray(tab[i]) for i in range(5)]


def _overlap_map(n_cmp_rows, n_selp):
    m = np.arange(n_cmp_rows)[None, :]
    s = np.arange(n_selp)[:, None]
    c0 = (m - 1) * CMP_STRIDE
    ov = np.minimum(c0 + CMP_LEN, s * SEL_BLOCK + SEL_BLOCK) - np.maximum(c0, s * SEL_BLOCK)
    w = np.maximum(ov, 0).astype(np.float32) / CMP_LEN
    w[:, 0] = 0.0
    return jnp.asarray(w, BF16)


def nsa_attend(qc, qr, gates, cmp_p, sel_p, win_p, T, qb, q0, w_off, n_sel):
    B = qc.shape[0]
    group = N_HEADS // N_KV
    R = group * qb
    n_cmp = cmp_p.shape[2]
    n_selp = -(-n_sel // LANES) * LANES
    tabs = _nsa_tables(T, qb, q0, w_off)
    n_steps = int(tabs[0].shape[0])
    wmap = _overlap_map(n_cmp, n_selp)
    odt = BF16 if qb % 16 == 0 else F32
    kern = functools.partial(_nsa_kernel, qb=qb, n_sel=n_sel, n_selp=n_selp, n_cmp=n_cmp,
                             q0=q0, w_off=w_off)
    qspec = pl.BlockSpec((1, N_KV, 1, R, LANES), lambda b, s, qi, kt, wt, wf, la: (b, 0, qi[s], 0, 0))
    grid_spec = pltpu.PrefetchScalarGridSpec(
        num_scalar_prefetch=5,
        grid=(B, n_steps),
        in_specs=[
            qspec, qspec,
            pl.BlockSpec((1, qb, LANES), lambda b, s, qi, kt, wt, wf, la: (b, qi[s], 0)),
            pl.BlockSpec((1, N_KV, n_cmp, LANES), lambda b, s, qi, kt, wt, wf, la: (b, 0, 0, 0)),
            pl.BlockSpec((1, N_KV, KEY_TILE, LANES), lambda b, s, qi, kt, wt, wf, la: (b, 0, kt[s], 0)),
            pl.BlockSpec((1, N_KV, KEY_TILE, LANES), lambda b, s, qi, kt, wt, wf, la: (b, 0, wt[s], 0)),
            pl.BlockSpec((n_selp, n_cmp), lambda b, s, qi, kt, wt, wf, la: (0, 0)),
        ],
        out_specs=pl.BlockSpec((1, qb, N_HEADS * LANES), lambda b, s, qi, kt, wt, wf, la: (b, qi[s], 0)),
        scratch_shapes=[
            pltpu.VMEM((N_KV, qb, n_selp), F32),
            pltpu.VMEM((N_KV, R, LANES), F32),
            pltpu.VMEM((N_KV, R, 1), F32), pltpu.VMEM((N_KV, R, 1), F32), pltpu.VMEM((N_KV, R, LANES), F32),
            pltpu.VMEM((N_KV, R, 1), F32), pltpu.VMEM((N_KV, R, 1), F32), pltpu.VMEM((N_KV, R, LANES), F32),
        ],
    )
    return pl.pallas_call(
        kern,
        out_shape=jax.ShapeDtypeStruct((B, T, N_HEADS * LANES), odt),
        grid_spec=grid_spec,
        compiler_params=_cparams(("parallel", "arbitrary")),
        name="nsa_attend",
    )(*tabs, qc, qr, gates, cmp_p, sel_p, win_p, wmap)


def _qgt_kernel(x_ref, g_ref, w_ref, gain_ref, cos_ref, sin_ref, bd_ref,
                qc_ref, qr_ref, gate_ref, *, qb, n_qb):
    h = _rms_rows(x_ref[0], g_ref[...]).astype(BF16)
    y = jnp.dot(h, w_ref[...], preferred_element_type=F32)
    cos = cos_ref[...]
    sin = sin_ref[...]
    bd = bd_ref[...]
    nq = N_HEADS * HEAD_DIM
    group = N_HEADS // N_KV
    gate_t = jax.nn.sigmoid(y[:, nq:nq + LANES]).T
    for j in range(n_qb):
        gate_ref[0, j] = gate_t[:, j * qb:(j + 1) * qb]
    pad = jnp.zeros((N_KV, n_qb, HEAD_DIM, group * qb), qc_ref.dtype)
    qc_ref[0, :, :, HEAD_DIM:, :] = pad
    qr_ref[0, :, :, HEAD_DIM:, :] = pad
    for p in range(N_HEADS // 2):
        qx = y[:, LANES * p:LANES * (p + 1)]
        qn = qx * lax.rsqrt(_head_ms(qx, bd) + EPS) * gain_ref[...]
        qr = _rope64(qn, cos, sin)
        for src, dst in ((qn, qc_ref), (qr, qr_ref)):
            st = src.T.astype(dst.dtype)
            for par in range(2):
                kvh, gi = divmod(2 * p + par, group)
                for j in range(n_qb):
                    dst[0, kvh, j, 0:HEAD_DIM, gi * qb:(gi + 1) * qb] = (
                        st[par * HEAD_DIM:(par + 1) * HEAD_DIM, j * qb:(j + 1) * qb])


def qg_project_t(x, g, w, gain, cos, sin, bd, tm, qb):
    B, T, D = x.shape
    N = w.shape[1]
    n_qb = tm // qb
    group = N_HEADS // N_KV
    kern = functools.partial(_qgt_kernel, qb=qb, n_qb=n_qb)
    qshape = jax.ShapeDtypeStruct((B, N_KV, T // qb, LANES, group * qb), BF16)
    qspec = pl.BlockSpec((1, N_KV, n_qb, LANES, group * qb), lambda b, t: (b, 0, t, 0, 0))
    return pl.pallas_call(
        kern,
        out_shape=[qshape, qshape, jax.ShapeDtypeStruct((B, T // qb, LANES, qb), F32)],
        grid=(B, T // tm),
        in_specs=[pl.BlockSpec((1, tm, D), lambda b, t: (b, t, 0)),
                  pl.BlockSpec((1, D), lambda b, t: (0, 0)),
                  pl.BlockSpec((D, N), lambda b, t: (0, 0)),
                  pl.BlockSpec((1, LANES), lambda b, t: (0, 0)),
                  pl.BlockSpec((tm, LANES), lambda b, t: (t, 0)),
                  pl.BlockSpec((tm, LANES), lambda b, t: (t, 0)),
                  pl.BlockSpec((LANES, LANES), lambda b, t: (0, 0))],
        out_specs=[qspec, qspec, pl.BlockSpec((1, n_qb, LANES, qb), lambda b, t: (b, t, 0, 0))],
        compiler_params=_cparams(("parallel", "parallel")),
        name="qg_project_t",
    )(x, g.reshape(1, D), w, gain, cos, sin, bd)


def _nsa_t_kernel(qc_ref, qr_ref, gate_ref, cmp_ref, sel_ref, win_ref, wmap_ref, o_ref,
                  selneg_ref, m_ref, l_ref, acc_ref,
                  *, qb, n_sel, n_selp, n_cmp, q0, w_off, w_rows, l_win):
    qi = pl.program_id(1)
    group = N_HEADS // N_KV
    R = group * qb
    blocks_per_tile = KEY_TILE // SEL_BLOCK
    q_lo = q0 + qi * qb
    qpos_q = q_lo + lax.broadcasted_iota(jnp.int32, (1, qb), 1)
    n_kt = (q_lo + qb - 1) // KEY_TILE + 1
    w_start = pl.multiple_of(jnp.clip(q_lo - WINDOW - w_off, 0, l_win - w_rows), LANES)
    gate = gate_ref[0, 0]

    def lanes4(a):
        return jnp.concatenate([a] * group, axis=1)

    m_idx = lax.broadcasted_iota(jnp.int32, (n_cmp, qb), 0)
    cvalid = (m_idx >= 1) & ((m_idx - 1) * CMP_STRIDE + CMP_LEN - 1 <= qpos_q)
    cbias = lanes4(jnp.where(cvalid, 0.0, NEG_INF))
    any_c = lanes4(qpos_q >= CMP_LEN - 1)
    blk = lax.broadcasted_iota(jnp.int32, (n_selp, qb), 0)
    blk_f = blk.astype(F32)
    cur = qpos_q >> 6
    forced = (blk == 0) | (blk == cur) | (blk == cur - 1)
    reach = blk * SEL_BLOCK <= qpos_q
    real = blk < n_sel
    wpos = w_off + w_start + lax.broadcasted_iota(jnp.int32, (w_rows, qb), 0)
    wbias = lanes4(jnp.where((wpos <= qpos_q) & (wpos > qpos_q - WINDOW) & (wpos >= 0), 0.0, NEG_INF))
    row_t = lax.broadcasted_iota(jnp.int32, (KEY_TILE, qb), 0)

    for k in range(N_KV):
        ckv = cmp_ref[0, k]
        s = jnp.dot(ckv, qc_ref[0, k, 0], preferred_element_type=F32) + cbias
        e = jnp.exp2(s - jnp.max(s, axis=0, keepdims=True))
        den = jnp.maximum(jnp.sum(e, axis=0, keepdims=True), TINY)
        p = e * jnp.where(any_c, 1.0 / den, 0.0)
        oc = _tn_dot(ckv, p.astype(BF16))
        psum = p[:, 0:qb]
        for gi in range(1, group):
            psum = psum + p[:, gi * qb:(gi + 1) * qb]
        hi = psum.astype(BF16)
        lo = (psum - hi.astype(F32)).astype(BF16)
        imp = (jnp.dot(wmap_ref[...], hi, preferred_element_type=F32)
               + jnp.dot(wmap_ref[...], lo, preferred_element_type=F32))
        v = jnp.where(forced, SEL_FORCE, jnp.where(reach, imp, SEL_NEG))
        v = jnp.where(real, v, -jnp.inf)

        def pick_one(_, carry):
            v, sel = carry
            m = jnp.max(v, axis=0, keepdims=True)
            first = jnp.min(jnp.where(v == m, blk_f, float(n_selp)), axis=0, keepdims=True)
            pick = blk_f == first
            sel = jnp.where(pick & (m > 0.5 * SEL_NEG), 0.0, sel)
            return jnp.where(pick, -jnp.inf, v), sel

        _, sel = lax.fori_loop(0, SEL_TOPK, pick_one, (v, jnp.full((n_selp, qb), NEG_INF, F32)))
        selneg_ref[...] = sel

        qr = qr_ref[0, k, 0]
        m_ref[...] = jnp.full(m_ref.shape, NEG_INF, F32)
        l_ref[...] = jnp.zeros(l_ref.shape, F32)
        acc_ref[...] = jnp.zeros(acc_ref.shape, F32)

        def tile(kt, carry):
            start = pl.multiple_of(kt * KEY_TILE, KEY_TILE)
            kv = sel_ref[0, k, pl.ds(start, KEY_TILE), :]
            s = jnp.dot(kv, qr, preferred_element_type=F32)
            pieces = [jnp.broadcast_to(selneg_ref[pl.ds(kt * blocks_per_tile + j, 1), :], (SEL_BLOCK, qb))
                      for j in range(blocks_per_tile)]
            bias = jnp.concatenate(pieces, axis=0) + jnp.where(start + row_t <= qpos_q, 0.0, NEG_INF)
            s = s + lanes4(bias)
            m_old = m_ref[...]
            m_new = jnp.maximum(m_old, jnp.max(s, axis=0, keepdims=True))
            alpha = jnp.exp2(m_old - m_new)
            p = jnp.exp2(s - m_new)
            l_ref[...] = alpha * l_ref[...] + jnp.sum(p, axis=0, keepdims=True)
            acc_ref[...] = alpha * acc_ref[...] + _tn_dot(kv, p.astype(BF16))
            m_ref[...] = m_new
            return carry

        lax.fori_loop(0, n_kt, tile, 0)
        o_s = acc_ref[...] * (1.0 / jnp.maximum(l_ref[...], TINY))

        wkv = win_ref[0, k, pl.ds(w_start, w_rows), :]
        s = jnp.dot(wkv, qr, preferred_element_type=F32) + wbias
        e = jnp.exp2(s - jnp.max(s, axis=0, keepdims=True))
        den = jnp.maximum(jnp.sum(e, axis=0, keepdims=True), TINY)
        o_w = _tn_dot(wkv, e.astype(BF16)) * (1.0 / den)

        for gi in range(group):
            hd = k * group + gi
            sl = slice(gi * qb, (gi + 1) * qb)
            o = gate[3 * hd:3 * hd + 1, :] * oc[:, sl]
            o = o + gate[3 * hd + 1:3 * hd + 2, :] * o_s[:, sl]
            o = o + gate[3 * hd + 2:3 * hd + 3, :] * o_w[:, sl]
            o_ref[0, :, LANES * hd:LANES * (hd + 1)] = o.T.astype(o_ref.dtype)


def nsa_attend_t(qc, qr, gates, cmp_p, sel_p, win_p, T, qb, q0, w_off, n_sel):
    B = qc.shape[0]
    group = N_HEADS // N_KV
    R = group * qb
    n_cmp = cmp_p.shape[2]
    n_selp = -(-n_sel // LANES) * LANES
    l_sel = sel_p.shape[2]
    l_win = win_p.shape[2]
    w_rows = WINDOW + max(qb, LANES)
    assert l_win >= w_rows and l_sel >= ((q0 + T - 1) // KEY_TILE + 1) * KEY_TILE
    wmap = _overlap_map(n_cmp, n_selp)
    kern = functools.partial(_nsa_t_kernel, qb=qb, n_sel=n_sel, n_selp=n_selp, n_cmp=n_cmp,
                             q0=q0, w_off=w_off, w_rows=w_rows, l_win=l_win)
    qspec = pl.BlockSpec((1, N_KV, 1, LANES, R), lambda b, i: (b, 0, i, 0, 0))
    resident = dict(pipeline_mode=pl.Buffered(1))
    return pl.pallas_call(
        kern,
        out_shape=jax.ShapeDtypeStruct((B, T, N_HEADS * LANES), BF16),
        grid=(B, T // qb),
        in_specs=[
            qspec, qspec,
            pl.BlockSpec((1, 1, LANES, qb), lambda b, i: (b, i, 0, 0)),
            pl.BlockSpec((1, N_KV, n_cmp, LANES), lambda b, i: (b, 0, 0, 0)),
            pl.BlockSpec((1, N_KV, l_sel, LANES), lambda b, i: (b, 0, 0, 0), **resident),
            pl.BlockSpec((1, N_KV, l_win, LANES), lambda b, i: (b, 0, 0, 0), **resident),
            pl.BlockSpec((n_selp, n_cmp), lambda b, i: (0, 0)),
        ],
        out_specs=pl.BlockSpec((1, qb, N_HEADS * LANES), lambda b, i: (b, i, 0)),
        scratch_shapes=[
            pltpu.VMEM((n_selp, qb), F32),
            pltpu.VMEM((1, R), F32), pltpu.VMEM((1, R), F32), pltpu.VMEM((LANES, R), F32),
        ],
        compiler_params=_cparams(("parallel", "arbitrary")),
        name="nsa_attend_t",
    )(qc, qr, gates, cmp_p, sel_p, win_p, wmap)


def _nsa_fast_kernel(shift_ref, qc_ref, qr_ref, gate_ref, cmp_ref, sel_ref, win_ref, wmap_ref, hot_ref,
                     o_ref, qaug_ref, oc_ref, l_ref, acc_ref,
                     *, qb, n_sel, n_selp, n_cmp, q0, w_off, w_rows, l_win):
    qi = pl.program_id(1)
    group = N_HEADS // N_KV
    R = group * qb
    q_lo = q0 + qi * qb
    qpos_q = q_lo + lax.broadcasted_iota(jnp.int32, (1, qb), 1)
    n_kt = (q_lo + qb - 1) // KEY_TILE + 1
    w_start = pl.multiple_of(jnp.clip(q_lo - WINDOW - w_off, 0, l_win - w_rows), LANES)
    shift_c = shift_ref[0]
    shift_s = shift_ref[1]
    shift_w = shift_ref[2]

    def lanes4(a):
        return jnp.concatenate([a] * group, axis=1)

    m_idx = lax.broadcasted_iota(jnp.int32, (n_cmp, qb), 0)
    cvalid = (m_idx >= 1) & ((m_idx - 1) * CMP_STRIDE + CMP_LEN - 1 <= qpos_q)
    cbias = lanes4(jnp.where(cvalid, -shift_c, NEG_INF))
    any_c = lanes4(qpos_q >= CMP_LEN - 1)
    blk = lax.broadcasted_iota(jnp.int32, (n_selp, qb), 0)
    blk_f = blk.astype(F32)
    cur = qpos_q >> 6
    forced = (blk == 0) | (blk == cur) | (blk == cur - 1)
    reach = blk * SEL_BLOCK <= qpos_q
    real = blk < n_sel

    imps = []
    for k in range(N_KV):
        ckv = cmp_ref[0, k]
        e = jnp.exp2(jnp.dot(ckv, qc_ref[0, k, 0], preferred_element_type=F32) + cbias)
        den = jnp.maximum(jnp.sum(e, axis=0, keepdims=True), TINY)
        p = e * jnp.where(any_c, 1.0 / den, 0.0)
        oc_ref[k] = _tn_dot(ckv, p.astype(BF16))
        psum = p[:, 0:qb]
        for gi in range(1, group):
            psum = psum + p[:, gi * qb:(gi + 1) * qb]
        hi = psum.astype(BF16)
        lo = (psum - hi.astype(F32)).astype(BF16)
        imp = (jnp.dot(wmap_ref[...], hi, preferred_element_type=F32)
               + jnp.dot(wmap_ref[...], lo, preferred_element_type=F32))
        v = jnp.where(forced, SEL_FORCE, jnp.where(reach, imp, SEL_NEG))
        imps.append(jnp.where(real, v, -jnp.inf))

    def pick_one(_, vs):
        out = []
        for v in vs:
            m = jnp.max(v, axis=0, keepdims=True)
            first = jnp.min(jnp.where(v == m, blk_f, float(n_selp)), axis=0, keepdims=True)
            out.append(jnp.where(blk_f == first, -jnp.inf, v))
        return tuple(out)

    marked = lax.fori_loop(0, SEL_TOPK, pick_one, tuple(imps))
    for k in range(N_KV):
        sel = jnp.where((marked[k] == -jnp.inf) & (imps[k] > 0.5 * SEL_NEG), -shift_s, NEG_INF)
        qaug_ref[k, 0:LANES, :] = qr_ref[0, k, 0]
        qaug_ref[k, LANES:, :] = lanes4(sel).astype(BF16)

    l_ref[...] = jnp.zeros(l_ref.shape, F32)
    acc_ref[...] = jnp.zeros(acc_ref.shape, F32)

    def tile(start, rows, causal_bias):
        hot = hot_ref[pl.ds(start, rows), :]
        for k in range(N_KV):
            kv = sel_ref[0, k, pl.ds(start, rows), :]
            s = jnp.dot(jnp.concatenate([kv, hot], axis=1), qaug_ref[k], preferred_element_type=F32)
            if causal_bias is not None:
                s = s + causal_bias
            p = jnp.exp2(s)
            l_ref[k] += jnp.sum(p.reshape(rows // 8, 8, R), axis=0)
            acc_ref[k] += _tn_dot(kv, p.astype(BF16))

    def double_tile(i, carry):
        tile(pl.multiple_of(i * 2 * KEY_TILE, 2 * KEY_TILE), 2 * KEY_TILE, None)
        return carry

    n_below = n_kt - 1
    lax.fori_loop(0, n_below // 2, double_tile, 0)

    @pl.when(n_below % 2 == 1)
    def _():
        tile(pl.multiple_of((n_below - 1) * KEY_TILE, KEY_TILE), KEY_TILE, None)

    d_start = pl.multiple_of(n_below * KEY_TILE, KEY_TILE)
    row_t = lax.broadcasted_iota(jnp.int32, (KEY_TILE, qb), 0)
    tile(d_start, KEY_TILE, lanes4(jnp.where(d_start + row_t <= qpos_q, 0.0, NEG_INF)))

    wpos = w_off + w_start + lax.broadcasted_iota(jnp.int32, (w_rows, qb), 0)
    wvalid = (wpos <= qpos_q) & (wpos > qpos_q - WINDOW) & (wpos >= 0)
    wbias = lanes4(jnp.where(wvalid, -shift_w, NEG_INF))
    gate = gate_ref[0, 0]
    for k in range(N_KV):
        wkv = win_ref[0, k, pl.ds(w_start, w_rows), :]
        e = jnp.exp2(jnp.dot(wkv, qr_ref[0, k, 0], preferred_element_type=F32) + wbias)
        den = jnp.maximum(jnp.sum(e, axis=0, keepdims=True), TINY)
        o_w = _tn_dot(wkv, e.astype(BF16)) * (1.0 / den)
        l_s = jnp.maximum(jnp.sum(l_ref[k], axis=0, keepdims=True), TINY)
        o_s = acc_ref[k] * (1.0 / l_s)
        oc = oc_ref[k]
        for gi in range(group):
            hd = k * group + gi
            sl = slice(gi * qb, (gi + 1) * qb)
            o = gate[3 * hd:3 * hd + 1, :] * oc[:, sl]
            o = o + gate[3 * hd + 1:3 * hd + 2, :] * o_s[:, sl]
            o = o + gate[3 * hd + 2:3 * hd + 3, :] * o_w[:, sl]
            o_ref[0, :, LANES * hd:LANES * (hd + 1)] = o.T.astype(o_ref.dtype)


def nsa_attend_fast(shifts, qc, qr, gates, cmp_p, sel_p, win_p, T, qb, q0, w_off, n_sel):
    B = qc.shape[0]
    group = N_HEADS // N_KV
    R = group * qb
    n_cmp = cmp_p.shape[2]
    n_selp = -(-n_sel // LANES) * LANES
    l_sel = sel_p.shape[2]
    l_win = win_p.shape[2]
    w_rows = WINDOW + max(qb, LANES)
    assert l_win >= w_rows and l_sel >= ((q0 + T - 1) // KEY_TILE + 1) * KEY_TILE
    assert n_sel >= SEL_TOPK and q0 % qb == 0 and KEY_TILE % qb == 0
    wmap = _overlap_map(n_cmp, n_selp)
    hot = np.zeros((l_sel, n_selp), np.float32)
    hot[np.arange(l_sel), np.arange(l_sel) // SEL_BLOCK] = 1.0
    hot = jnp.asarray(hot, BF16)
    kern = functools.partial(_nsa_fast_kernel, qb=qb, n_sel=n_sel, n_selp=n_selp, n_cmp=n_cmp,
                             q0=q0, w_off=w_off, w_rows=w_rows, l_win=l_win)
    qspec = pl.BlockSpec((1, N_KV, 1, LANES, R), lambda b, i, sh: (b, 0, i, 0, 0))
    resident = dict(pipeline_mode=pl.Buffered(1))
    grid_spec = pltpu.PrefetchScalarGridSpec(
        num_scalar_prefetch=1,
        grid=(B, T // qb),
        in_specs=[
            qspec, qspec,
            pl.BlockSpec((1, 1, LANES, qb), lambda b, i, sh: (b, i, 0, 0)),
            pl.BlockSpec((1, N_KV, n_cmp, LANES), lambda b, i, sh: (b, 0, 0, 0)),
            pl.BlockSpec((1, N_KV, l_sel, LANES), lambda b, i, sh: (b, 0, 0, 0), **resident),
            pl.BlockSpec((1, N_KV, l_win, LANES), lambda b, i, sh: (b, 0, 0, 0), **resident),
            pl.BlockSpec((n_selp, n_cmp), lambda b, i, sh: (0, 0)),
            pl.BlockSpec((l_sel, n_selp), lambda b, i, sh: (0, 0), **resident),
        ],
        out_specs=pl.BlockSpec((1, qb, N_HEADS * LANES), lambda b, i, sh: (b, i, 0)),
        scratch_shapes=[
            pltpu.VMEM((N_KV, 2 * LANES, R), BF16),
            pltpu.VMEM((N_KV, LANES, R), F32),
            pltpu.VMEM((N_KV, 8, R), F32),
            pltpu.VMEM((N_KV, LANES, R), F32),
        ],
    )
    return pl.pallas_call(
        kern,
        out_shape=jax.ShapeDtypeStruct((B, T, N_HEADS * LANES), BF16),
        grid_spec=grid_spec,
        compiler_params=_cparams(("parallel", "arbitrary")),
        name="nsa_attend_fast",
    )(shifts, qc, qr, gates, cmp_p, sel_p, win_p, wmap, hot)


def _nsa_dec_kernel(pt_ref, *refs, n_pg, qb, n_sel, n_selp, n_cmp, q0, w_off):
    pages = refs[:n_pg]
    (qcp_ref, qbd_ref, gate_ref, cmp_ref, new_ref, win_ref, wmap_ref, fold_ref, hot_ref,
     o_ref, qaug_ref, oc_ref, m_ref, l_ref, acc_ref) = refs[n_pg:]
    g = pl.program_id(1)
    nk = N_KV * HEAD_DIM
    t_new = new_ref.shape[1]
    lane = lax.broadcasted_iota(jnp.int32, (1, LANES), 1)
    qpos = q0 + (lane & (qb - 1))

    @pl.when(g == 0)
    def _first():
        s = jnp.dot(cmp_ref[0, 0], qcp_ref[0, 0], preferred_element_type=F32)
        for k in range(1, N_KV):
            s = s + jnp.dot(cmp_ref[0, k], qcp_ref[0, k], preferred_element_type=F32)
        m_idx = lax.broadcasted_iota(jnp.int32, (n_cmp, LANES), 0)
        cvalid = (m_idx >= 1) & ((m_idx - 1) * CMP_STRIDE + CMP_LEN - 1 <= qpos)
        s = s + jnp.where(cvalid, 0.0, NEG_INF)
        e = jnp.exp2(s - jnp.max(s, axis=0, keepdims=True))
        den = jnp.maximum(jnp.sum(e, axis=0, keepdims=True), TINY)
        p = e * jnp.where(qpos >= CMP_LEN - 1, 1.0 / den, 0.0)
        pb = p.astype(BF16)
        for k in range(N_KV):
            oc_ref[k * HEAD_DIM:(k + 1) * HEAD_DIM, :] = _tn_dot(cmp_ref[0, k], pb)[HEAD_DIM:, :]
        fold = fold_ref[...]
        p_lo = (p - pb.astype(F32)).astype(BF16)
        psum = jnp.dot(pb, fold, preferred_element_type=F32) + jnp.dot(p_lo, fold, preferred_element_type=F32)
        hi = psum.astype(BF16)
        lo = (psum - hi.astype(F32)).astype(BF16)
        imp = (jnp.dot(wmap_ref[...], hi, preferred_element_type=F32)
               + jnp.dot(wmap_ref[...], lo, preferred_element_type=F32))
        blk = lax.broadcasted_iota(jnp.int32, (n_selp, LANES), 0)
        blk_f = blk.astype(F32)
        cur = qpos >> 6
        forced = (blk == 0) | (blk == cur) | (blk == cur - 1)
        v = jnp.where(forced, SEL_FORCE, jnp.where(blk * SEL_BLOCK <= qpos, imp, SEL_NEG))
        v = jnp.where(blk < n_sel, v, -jnp.inf)

        def pick_one(_, carry):
            v, sel = carry
            m = jnp.max(v, axis=0, keepdims=True)
            first = jnp.min(jnp.where(v == m, blk_f, float(n_selp)), axis=0, keepdims=True)
            pick = blk_f == first
            sel = jnp.where(pick & (m > 0.5 * SEL_NEG), 0.0, sel)
            return jnp.where(pick, -jnp.inf, v), sel

        _, sel = lax.fori_loop(0, SEL_TOPK, pick_one, (v, jnp.full((n_selp, LANES), NEG_INF, F32)))
        qaug_ref[0:nk, :] = qbd_ref[0]
        qaug_ref[nk:, :] = sel.astype(BF16)
        m_ref[...] = jnp.full(m_ref.shape, NEG_INF, F32)
        l_ref[...] = jnp.zeros(l_ref.shape, F32)
        acc_ref[...] = jnp.zeros(acc_ref.shape, F32)

    def attend(kx, vx, hot, bias):
        s = jnp.dot(jnp.concatenate([kx, hot], axis=1), qaug_ref[...], preferred_element_type=F32)
        if bias is not None:
            s = s + bias
        m_old = m_ref[...]
        m_new = jnp.maximum(m_old, jnp.max(s, axis=0, keepdims=True))
        alpha = jnp.exp2(m_old - m_new)
        p = jnp.exp2(s - m_new)
        l_ref[...] = alpha * l_ref[...] + jnp.sum(p.reshape(p.shape[0] // 8, 8, LANES), axis=0)
        acc_ref[...] = alpha * acc_ref[...] + _tn_dot(vx, p.astype(BF16))
        m_ref[...] = m_new

    per_tile = KEY_TILE // PAGE
    for t in range(n_pg // per_tile):
        x = jnp.concatenate([pages[per_tile * t + i][0] for i in range(per_tile)], axis=0).astype(BF16)
        start = pl.multiple_of((g * (n_pg // per_tile) + t) * KEY_TILE, KEY_TILE)
        attend(x[:, :nk], x[:, nk:], hot_ref[pl.ds(start, KEY_TILE), :], None)

    @pl.when(g == pl.num_programs(1) - 1)
    def _last():
        pad = jnp.zeros((HALO - t_new, 2 * nk), F32)
        row = lax.broadcasted_iota(jnp.int32, (HALO, LANES), 0)
        new_ok = (row < t_new) & (q0 + row <= qpos)
        xn = jnp.concatenate([new_ref[0, :, 2 * nk:4 * nk], pad], axis=0).astype(BF16)
        hot_new = jnp.where(
            lax.broadcasted_iota(jnp.int32, (HALO, n_selp), 1) == q0 // SEL_BLOCK, 1.0, 0.0).astype(BF16)
        attend(xn[:, :nk], xn[:, nk:], hot_new, jnp.where(new_ok, 0.0, NEG_INF))

        qbd = qbd_ref[0]
        xw = win_ref[0].astype(BF16)
        wrow = lax.broadcasted_iota(jnp.int32, (xw.shape[0], LANES), 0)
        wpos = w_off + wrow
        wvalid = (wpos <= qpos) & (wpos > qpos - WINDOW) & (wpos >= 0)
        s_c = jnp.dot(xw[:, :nk], qbd, preferred_element_type=F32) + jnp.where(wvalid, 0.0, NEG_INF)
        xwn = jnp.concatenate([new_ref[0, :, 4 * nk:6 * nk], pad], axis=0).astype(BF16)
        npos = q0 + row
        nvalid = (row < t_new) & (npos <= qpos) & (npos > qpos - WINDOW)
        s_n = jnp.dot(xwn[:, :nk], qbd, preferred_element_type=F32) + jnp.where(nvalid, 0.0, NEG_INF)
        m_w = jnp.maximum(jnp.max(s_c, axis=0, keepdims=True), jnp.max(s_n, axis=0, keepdims=True))
        e_c = jnp.exp2(s_c - m_w)
        e_n = jnp.exp2(s_n - m_w)
        den = jnp.sum(e_c, axis=0, keepdims=True) + jnp.sum(e_n, axis=0, keepdims=True)
        o_w = _tn_dot(xw[:, nk:], e_c.astype(BF16)) + _tn_dot(xwn[:, nk:], e_n.astype(BF16))
        o_w = o_w * (1.0 / jnp.maximum(den, TINY))
        l_s = jnp.maximum(jnp.sum(l_ref[...], axis=0, keepdims=True), TINY)
        o_s = acc_ref[...] * (1.0 / l_s)
        gate = gate_ref[0]
        o_ref[0] = gate[0:1, :] * oc_ref[...] + gate[1:2, :] * o_s + gate[2:3, :] * o_w


def _dec_page_index(b, g, pt, *, i, n_pg):
    return (pt[b, g * n_pg + i], 0, 0)


def nsa_attend_dec(table, qc, qr, gates, cmp_p, sel_pages, kvp, cache_win, qb, q0, n_sel):
    B, n_pages = table.shape
    group = N_HEADS // N_KV
    assert N_KV * group * qb == LANES
    nk = N_KV * HEAD_DIM
    n_pg = 16 if n_pages % 16 == 0 else n_pages
    n_cmp = cmp_p.shape[2]
    n_selp = -(-n_sel // LANES) * LANES
    wl = cache_win.shape[1]
    eye = jnp.eye(N_KV, dtype=F32)

    def spread(q):
        qt = q[:, :, 0, :, :HEAD_DIM].astype(F32).transpose(0, 1, 3, 2)
        return jnp.einsum("bkdr,kj->bkdjr", qt, eye).reshape(B, N_KV, HEAD_DIM, LANES)

    qc_pad = jnp.pad(spread(qc), ((0, 0), (0, 0), (0, LANES - HEAD_DIM), (0, 0))).astype(BF16)
    q_bd = spread(qr).reshape(B, nk, LANES).astype(BF16)
    gate_l = gates[:, :, :3 * N_HEADS].reshape(B, qb, N_KV, group, 3).transpose(0, 4, 2, 3, 1)
    gate_l = gate_l.reshape(B, 3, LANES)
    wmap = _overlap_map(n_cmp, n_selp)
    lane = np.arange(LANES)
    fold = jnp.asarray((lane[:, None] // (group * qb) == lane[None, :] // (group * qb))
                       & (lane[:, None] % qb == lane[None, :] % qb), BF16)
    l_past = n_pages * PAGE
    hot = np.zeros((l_past, n_selp), np.float32)
    hot[np.arange(l_past), np.arange(l_past) // SEL_BLOCK] = 1.0
    hot = jnp.asarray(hot, BF16)
    kern = functools.partial(_nsa_dec_kernel, n_pg=n_pg, qb=qb, n_sel=n_sel, n_selp=n_selp,
                             n_cmp=n_cmp, q0=q0, w_off=q0 - wl)
    in_specs = [pl.BlockSpec((1, PAGE, 2 * nk), functools.partial(_dec_page_index, i=i, n_pg=n_pg))
                for i in range(n_pg)]
    in_specs += [
        pl.BlockSpec((1, N_KV, LANES, LANES), lambda b, g, pt: (b, 0, 0, 0)),
        pl.BlockSpec((1, nk, LANES), lambda b, g, pt: (b, 0, 0)),
        pl.BlockSpec((1, 3, LANES), lambda b, g, pt: (b, 0, 0)),
        pl.BlockSpec((1, N_KV, n_cmp, LANES), lambda b, g, pt: (b, 0, 0, 0)),
        pl.BlockSpec((1, qb, 6 * nk), lambda b, g, pt: (b, 0, 0)),
        pl.BlockSpec((1, wl, 2 * nk), lambda b, g, pt: (b, 0, 0)),
        pl.BlockSpec((n_selp, n_cmp), lambda b, g, pt: (0, 0)),
        pl.BlockSpec((LANES, LANES), lambda b, g, pt: (0, 0)),
        pl.BlockSpec((l_past, n_selp), lambda b, g, pt: (0, 0), pipeline_mode=pl.Buffered(1)),
    ]
    grid_spec = pltpu.PrefetchScalarGridSpec(
        num_scalar_prefetch=1,
        grid=(B, n_pages // n_pg),
        in_specs=in_specs,
        out_specs=pl.BlockSpec((1, nk, LANES), lambda b, g, pt: (b, 0, 0)),
        scratch_shapes=[
            pltpu.VMEM((nk + n_selp, LANES), BF16),
            pltpu.VMEM((nk, LANES), F32),
            pltpu.VMEM((1, LANES), F32),
            pltpu.VMEM((8, LANES), F32),
            pltpu.VMEM((nk, LANES), F32),
        ],
    )
    o_t = pl.pallas_call(
        kern,
        out_shape=jax.ShapeDtypeStruct((B, nk, LANES), F32),
        grid_spec=grid_spec,
        compiler_params=_cparams(("parallel", "arbitrary")),
        name="nsa_attend_dec",
    )(table, *([sel_pages] * n_pg), qc_pad, q_bd, gate_l, cmp_p, kvp, cache_win, wmap, fold, hot)
    o6 = o_t.reshape(B, N_KV, HEAD_DIM, N_KV, group, qb)
    o5 = jnp.einsum("bkdkgq->bqkgd", o6)
    return o5.reshape(B * qb, N_HEADS * HEAD_DIM)


def _rope_tables(pos, half):
    inv = jnp.exp(-math.log(ROPE_THETA) * jnp.arange(half, dtype=F32) / half)
    ang = pos.astype(F32)[:, None] * inv[None, :]
    return jnp.cos(ang), jnp.sin(ang)


def _prep_weights(ret_w_in, ret_w_out, ffn_w_in, ffn_w_out, kv_w, kv_knorm, cmp_w1, cmp_w2,
                  nsa_w_qg, nsa_qnorm, nsa_w_o):
    n_b = nsa_w_qg.shape[0]
    nq = N_HEADS * HEAD_DIM
    qg_pad = nq + LANES - nsa_w_qg.shape[2]
    w_qg = jnp.pad(nsa_w_qg, ((0, 0), (0, 0), (0, qg_pad))).astype(BF16)
    w_o = jnp.pad(nsa_w_o.reshape(n_b, N_HEADS, 1, HEAD_DIM, D_MODEL),
                  ((0, 0), (0, 0), (1, 0), (0, 0), (0, 0))).reshape(n_b, N_HEADS * LANES, D_MODEL)
    R = CMP_LEN // CMP_STRIDE
    w1 = cmp_w1.reshape(2, R, CMP_STRIDE, HEAD_DIM, CMP_HID).transpose(0, 2, 3, 1, 4)
    w1 = w1.reshape(2, CMP_STRIDE, HEAD_DIM, R * CMP_HID)
    w1dup = jnp.concatenate([w1, w1], axis=2).astype(BF16)
    w1dup = w1dup.reshape(2, CMP_STRIDE // 2, 2 * LANES, R * CMP_HID)
    z = jnp.zeros((CMP_HID, HEAD_DIM), F32)
    w2p = jnp.stack([jnp.concatenate([cmp_w2[0], z], axis=1),
                     jnp.concatenate([z, cmp_w2[1]], axis=1)]).astype(BF16)
    ones = jnp.ones((HEAD_DIM,), F32)
    return dict(
        ret_w_in=ret_w_in.astype(BF16), ret_w_out=ret_w_out.astype(BF16),
        ffn_w_in=ffn_w_in.astype(BF16), ffn_w_out=ffn_w_out.astype(BF16),
        kv_w=kv_w.astype(BF16), w_qg=w_qg, w_o=w_o.astype(BF16), w_o_raw=nsa_w_o.astype(BF16),
        kv_gain=jnp.tile(kv_knorm, (1, 2)),
        cmp_gain=jnp.concatenate([kv_knorm[0], ones]).reshape(1, LANES),
        q_gain=jnp.tile(nsa_qnorm, (1, 2)) * (HEAD_DIM ** -0.5),
        q_gain2=jnp.tile(nsa_qnorm, (1, 2)) * (HEAD_DIM ** -0.5 * math.log2(math.e)),
        score_bound=(1.05 * HEAD_DIM ** 0.5 * math.log2(math.e))
        * jnp.max(jnp.abs(nsa_qnorm), axis=1)[:, None] * jnp.max(jnp.abs(kv_knorm), axis=1)[None, :],
        w1dup=w1dup, w2p=w2p,
        bd=jnp.asarray(np.kron(np.eye(2), np.ones((HEAD_DIM, HEAD_DIM))), BF16),
    )


def _trunk(x, past_len, ret_s0, conv0, ctx, W, P):
    B, T, D = x.shape
    M = B * T
    depth = P["norm_mix"].shape[0]
    n_a = P["ret_w_in"].shape[0]
    pos = past_len + jnp.arange(T)
    cos_r, sin_r = _rope_tables(pos, RET_DK // 2)
    c32, s32 = _rope_tables(pos, HEAD_DIM // 2)
    cos_n = jnp.tile(c32, (1, 4))
    sin_n = jnp.concatenate([-s32, s32, -s32, s32], axis=1)
    lg = jnp.log1p(-jnp.exp2(-5.0 - jnp.arange(RET_HEADS, dtype=F32)))
    L = RET_CHUNK if T % RET_CHUNK == 0 else T
    gl = jnp.exp(L * lg)
    tm = min(512, M)
    tf = 1024 if T % 1024 == 0 else 512
    tb = min(512, T)
    tq = min(512, T)
    qb = Q_BLOCK if T % Q_BLOCK == 0 else T
    tt = min(256, T)

    x2 = x.reshape(M, D)
    ret_states, conv_states = [], []
    for layer in range(depth):
        if layer == n_a:
            kvp, selp, winp = kv_project(x2.reshape(B, T, D), P["kv_norm"], W["kv_w"], W["kv_gain"],
                                         cos_n, sin_n, W["bd"], tq)
            nk2 = 2 * N_KV * HEAD_DIM
            if ctx is None:
                table = jnp.arange(M // PAGE, dtype=jnp.int32).reshape(B, T // PAGE)
                cmp_p = compress(kvp.reshape(M // PAGE, PAGE, 3 * nk2), table,
                                 W["w1dup"], W["pb"], W["w2p"], W["cmp_gain"])
                sel_p, win_p = selp, winp
                w_off = 0
            else:
                cache_cmp, cache_sel, cache_win, table = ctx
                n_pool = cache_cmp.shape[0]
                cmp_p = compress(cache_cmp.reshape(n_pool, PAGE, nk2), table,
                                 W["w1dup"], W["pb"], W["w2p"], W["cmp_gain"])
                sel_pages = lax.optimization_barrier(cache_sel.reshape(n_pool, PAGE, nk2))
                wl = cache_win.shape[1]
                win_rows = cache_win.reshape(B, wl, nk2)
                w_off = past_len - wl
            n_sel = -(-(past_len + T) // SEL_BLOCK)
        h_norm = P["norm_mix"][layer]
        if layer < n_a:
            proj = norm_matmul(x2, h_norm, W["ret_w_in"][layer], min(1024, M), 1024)
            og, s_new = retention(proj.reshape(B, T, -1), ret_s0[layer], cos_r, sin_r, lg, gl, L, tb)
            ret_states.append(s_new)
            x2 = matmul_res(og.reshape(M, -1), W["ret_w_out"][layer], x2, tm)
        else:
            j = layer - n_a
            if qb % LANES == 0:
                qc, qr, gates = qg_project_t(x2.reshape(B, T, D), h_norm, W["w_qg"][j],
                                             W["q_gain2"][j:j + 1], cos_n, sin_n, W["bd"], tq, qb)
                shifts = W["score_bound"][j]
                args = (qc, qr, gates, cmp_p, sel_p, win_p)
                o = lax.cond(
                    jnp.max(shifts) <= 30.0,
                    lambda a: nsa_attend_fast(shifts, *a, T, qb, past_len, w_off, n_sel),
                    lambda a: nsa_attend_t(*a, T, qb, past_len, w_off, n_sel),
                    args)
                x2 = matmul_res(o.reshape(M, -1), W["w_o"][j], x2, tm)
            else:
                qc, qr, gates = qg_project(x2.reshape(B, T, D), h_norm, W["w_qg"][j],
                                           W["q_gain2"][j:j + 1], cos_n, sin_n, W["bd"], tq, qb)
                o = nsa_attend_dec(table, qc, qr, gates, cmp_p, sel_pages, kvp, win_rows,
                                   qb, past_len, n_sel)
                x2 = matmul_res(o, W["w_o_raw"][j], x2, tm)
        if T % tf == 0:
            act, tail = ffn_in(x2, P["norm_ffn"][layer], W["ffn_w_in"][layer], conv0[layer],
                               P["ffn_conv_w"][layer], P["ffn_conv_b"][layer], T, tf, 256)
            conv_states.append(tail)
        else:
            proj = norm_matmul(x2, P["norm_ffn"][layer], W["ffn_w_in"][layer], tm, 512)
            proj3 = proj.reshape(B, T, 2 * D_FF)
            act = ffn_mid(proj3, conv0[layer], P["ffn_conv_w"][layer], P["ffn_conv_b"][layer], tt)
            conv_states.append(proj3[:, T - 2:, :D_FF])
        x2 = matmul_res(act.reshape(M, D_FF), W["ffn_w_out"][layer], x2, tm)

    nk = N_KV * HEAD_DIM
    new_cmp = kvp[:, :, 0:2 * nk].reshape(B, T, 2, N_KV, HEAD_DIM)
    new_sel = kvp[:, :, 2 * nk:4 * nk].reshape(B, T, 2, N_KV, HEAD_DIM)
    new_win = kvp[:, :, 4 * nk:6 * nk].reshape(B, T, 2, N_KV, HEAD_DIM)
    return (x2.reshape(B, T, D), jnp.stack(ret_states), jnp.stack(conv_states),
            new_cmp, new_sel, new_win)


def kernel(x_prompt, x_sample, cache_cmp_kv, cache_sel_kv, cache_win_kv, state_ret, state_conv,
           page_table, norm_mix, norm_ffn, ret_w_in, ret_w_out, ffn_w_in, ffn_conv_w, ffn_conv_b,
           ffn_w_out, kv_norm, kv_w, kv_knorm, cmp_pos, cmp_w1, cmp_w2, nsa_w_qg, nsa_qnorm, nsa_w_o):
    W = _prep_weights(ret_w_in, ret_w_out, ffn_w_in, ffn_w_out, kv_w, kv_knorm, cmp_w1, cmp_w2,
                      nsa_w_qg, nsa_qnorm, nsa_w_o)
    W["pb"] = pos_bias(cmp_pos, cmp_w1)
    P = dict(norm_mix=norm_mix, norm_ffn=norm_ffn, ret_w_in=ret_w_in, ffn_conv_w=ffn_conv_w,
             ffn_conv_b=ffn_conv_b, kv_norm=kv_norm)
    depth = norm_mix.shape[0]
    n_a = ret_w_in.shape[0]
    B, T, _ = x_prompt.shape
    zero_ret = jnp.zeros((n_a, B, RET_HEADS, RET_DK, RET_DV), F32)
    zero_conv = jnp.zeros((depth, B, 2, D_FF), F32)
    y_p, ret_p, conv_p, cmp_p, sel_p, win_p = _trunk(x_prompt, 0, zero_ret, zero_conv, None, W, P)
    win_p = win_p[:, T - min(WINDOW, T):]

    db, ts, _ = x_sample.shape
    past_len = page_table.shape[1] * PAGE
    ctx = (cache_cmp_kv, cache_sel_kv, cache_win_kv, page_table)
    y_s, ret_s, conv_s, cmp_s, sel_s, win_new = _trunk(x_sample, past_len, state_ret, state_conv,
                                                        ctx, W, P)
    all_win = jnp.concatenate([cache_win_kv, win_new], axis=1)
    win_s = all_win[:, all_win.shape[1] - min(WINDOW, past_len + ts):]
    return (y_p, y_s, ret_p, ret_s, conv_p, conv_s, cmp_p, cmp_s, sel_p, sel_s, win_p, win_s)
```

```python
import functools
import math

import jax
import jax.numpy as jnp
import numpy as np
from jax import lax
from jax.experimental import pallas as pl
from jax.experimental.pallas import tpu as pltpu

F32 = jnp.float32
BF16 = jnp.bfloat16

D_MODEL = 1024
PAGE = 128
RET_HEADS = 4
RET_DK = 256
RET_DV = 512
RET_CHUNK = 128
N_HEADS = 16
N_KV = 4
HEAD_DIM = 64
CMP_LEN = 32
CMP_STRIDE = 16
CMP_HID = 128
SEL_BLOCK = 64
SEL_TOPK = 16
WINDOW = 512
Q_BLOCK = 128
D_FF = 2816
ROPE_THETA = 10000.0
EPS = 1e-6
NEG_INF = -1e30
TINY = 1e-30
SEL_FORCE = 1e6
SEL_NEG = -1e6

LANES = 128
KEY_TILE = 512
HALO = 16
VMEM_LIMIT = 48 * 1024 * 1024


def _cparams(sem):
    return pltpu.CompilerParams(dimension_semantics=sem, vmem_limit_bytes=VMEM_LIMIT)


def _nt_dot(a, b):
    return lax.dot_general(a, b, (((1,), (1,)), ((), ())), preferred_element_type=F32)


def _tn_dot(a, b):
    return lax.dot_general(a, b, (((0,), (0,)), ((), ())), preferred_element_type=F32)


def _gelu(x):
    return 0.5 * x * (1.0 + jnp.tanh(math.sqrt(2.0 / math.pi) * (x + 0.044715 * (x * x * x))))


def _rms_rows(x, g):
    r = lax.rsqrt(jnp.mean(x * x, axis=-1, keepdims=True) + EPS)
    return x * r * g


def _head_ms(x, bd):
    x2 = x * x
    hi = x2.astype(BF16)
    lo = (x2 - hi.astype(F32)).astype(BF16)
    s = jnp.dot(hi, bd, preferred_element_type=F32) + jnp.dot(lo, bd, preferred_element_type=F32)
    return s * (1.0 / HEAD_DIM)


def _rope64(x, cos, sin):
    lane = lax.broadcasted_iota(jnp.int32, x.shape, 1)
    sw = jnp.where((lane & 63) < 32, pltpu.roll(x, 96, 1), pltpu.roll(x, 32, 1))
    return x * cos + sw * sin


def _pack_pair(k2, v2, dtype):
    lane = lax.broadcasted_iota(jnp.int32, k2.shape, 1)
    lo = lane < HEAD_DIM
    even = jnp.where(lo, k2, pltpu.roll(v2, HEAD_DIM, 1)).astype(dtype)
    odd = jnp.where(lo, pltpu.roll(k2, HEAD_DIM, 1), v2).astype(dtype)
    return even, odd


def _norm_matmul_kernel(x_ref, g_ref, w_ref, o_ref, h_ref):
    @pl.when(pl.program_id(1) == 0)
    def _():
        h_ref[...] = _rms_rows(x_ref[...], g_ref[...]).astype(BF16)

    o_ref[...] = jnp.dot(h_ref[...], w_ref[...], preferred_element_type=F32).astype(o_ref.dtype)


def norm_matmul(x, g, w, tm, tn):
    M, D = x.shape
    N = w.shape[1]
    return pl.pallas_call(
        _norm_matmul_kernel,
        out_shape=jax.ShapeDtypeStruct((M, N), F32),
        grid=(M // tm, N // tn),
        in_specs=[pl.BlockSpec((tm, D), lambda i, j: (i, 0)),
                  pl.BlockSpec((1, D), lambda i, j: (0, 0)),
                  pl.BlockSpec((D, tn), lambda i, j: (0, j))],
        out_specs=pl.BlockSpec((tm, tn), lambda i, j: (i, j)),
        scratch_shapes=[pltpu.VMEM((tm, D), BF16)],
        compiler_params=_cparams(("parallel", "arbitrary")),
        name="norm_matmul",
    )(x, g.reshape(1, D), w)


def _matmul_res_kernel(a_ref, w_ref, r_ref, o_ref):
    o_ref[...] = r_ref[...] + jnp.dot(a_ref[...].astype(BF16), w_ref[...],
                                      preferred_element_type=F32)


def matmul_res(a, w, res, tm):
    M, K = a.shape
    N = w.shape[1]
    return pl.pallas_call(
        _matmul_res_kernel,
        out_shape=jax.ShapeDtypeStruct((M, N), F32),
        grid=(M // tm,),
        in_specs=[pl.BlockSpec((tm, K), lambda i: (i, 0)),
                  pl.BlockSpec((K, N), lambda i: (0, 0)),
                  pl.BlockSpec((tm, N), lambda i: (i, 0))],
        out_specs=pl.BlockSpec((tm, N), lambda i: (i, 0)),
        compiler_params=_cparams(("parallel",)),
        name="matmul_res",
    )(a, w, res)


def _retention_kernel(lg_ref, gl_ref, q_ref, k_ref, v_ref, g_ref, cos_ref, sin_ref, s0_ref,
                      o_ref, sout_ref, S_ref, *, L, n_chunk):
    h = pl.program_id(1)
    t = pl.program_id(2)
    lg = lg_ref[h]
    gl = gl_ref[h]

    @pl.when(t == 0)
    def _():
        S_ref[...] = s0_ref[0, 0]

    ii = lax.broadcasted_iota(jnp.int32, (L, L), 0)
    jj = lax.broadcasted_iota(jnp.int32, (L, L), 1)
    diff = (ii - jj).astype(F32)
    decay = jnp.where(diff >= 0, jnp.exp(jnp.maximum(diff, 0.0) * lg), 0.0)
    idx = lax.broadcasted_iota(jnp.int32, (L, 1), 0).astype(F32)
    q_dec = jnp.exp((idx + 1.0) * lg)
    k_dec = jnp.exp((L - 1.0 - idx) * lg)
    half = RET_DK // 2

    for c in range(n_chunk):
        rows = pl.ds(c * L, L)
        cos = cos_ref[rows, :]
        sin = sin_ref[rows, :]

        def rope(x):
            x1, x2 = x[:, :half], x[:, half:]
            return jnp.concatenate([x1 * cos - x2 * sin, x2 * cos + x1 * sin], axis=1)

        qr = rope(q_ref[0, rows, :])
        kr = rope(k_ref[0, rows, :]) * (RET_DK ** -0.5)
        qb = qr.astype(BF16)
        vb = v_ref[0, rows, :].astype(BF16)
        sc = _nt_dot(qb, kr.astype(BF16)) * decay
        S = S_ref[...]
        o = jnp.dot(sc.astype(BF16), vb, preferred_element_type=F32)
        o = o + jnp.dot(qb, S.astype(BF16), preferred_element_type=F32) * q_dec
        S_ref[...] = S * gl + _tn_dot((kr * k_dec).astype(BF16), vb)
        on = o * lax.rsqrt(jnp.mean(o * o, axis=-1, keepdims=True) + EPS)
        g = g_ref[0, rows, :]
        o_ref[0, rows, :] = (on * (g * jax.nn.sigmoid(g))).astype(o_ref.dtype)

    @pl.when(t == pl.num_programs(2) - 1)
    def _():
        sout_ref[0, 0] = S_ref[...]


def retention(proj, s0, cos, sin, lg, gl, L, tb):
    B, T, _ = proj.shape
    n_chunk = tb // L
    odt = BF16 if tb % 16 == 0 else F32
    kern = functools.partial(_retention_kernel, L=L, n_chunk=n_chunk)
    grid_spec = pltpu.PrefetchScalarGridSpec(
        num_scalar_prefetch=2,
        grid=(B, RET_HEADS, T // tb),
        in_specs=[
            pl.BlockSpec((1, tb, RET_DK), lambda b, h, t, *_: (b, t, h)),
            pl.BlockSpec((1, tb, RET_DK), lambda b, h, t, *_: (b, t, RET_HEADS + h)),
            pl.BlockSpec((1, tb, RET_DV), lambda b, h, t, *_: (b, t, RET_HEADS + h)),
            pl.BlockSpec((1, tb, RET_DV), lambda b, h, t, *_: (b, t, 2 * RET_HEADS + h)),
            pl.BlockSpec((tb, RET_DK // 2), lambda b, h, t, *_: (t, 0)),
            pl.BlockSpec((tb, RET_DK // 2), lambda b, h, t, *_: (t, 0)),
            pl.BlockSpec((1, 1, RET_DK, RET_DV), lambda b, h, t, *_: (b, h, 0, 0)),
        ],
        out_specs=[
            pl.BlockSpec((1, tb, RET_DV), lambda b, h, t, *_: (b, t, h)),
            pl.BlockSpec((1, 1, RET_DK, RET_DV), lambda b, h, t, *_: (b, h, 0, 0)),
        ],
        scratch_shapes=[pltpu.VMEM((RET_DK, RET_DV), F32)],
    )
    return pl.pallas_call(
        kern,
        out_shape=[jax.ShapeDtypeStruct((B, T, RET_HEADS * RET_DV), odt),
                   jax.ShapeDtypeStruct((B, RET_HEADS, RET_DK, RET_DV), F32)],
        grid_spec=grid_spec,
        compiler_params=_cparams(("parallel", "parallel", "arbitrary")),
        name="retention",
    )(lg, gl, proj, proj, proj, proj, cos, sin, s0)


def _ffn_mid_kernel(u_ref, gt_ref, halo_ref, cw_ref, cb_ref, o_ref):
    u = u_ref[0]
    hl = halo_ref[0, 0]
    row = lax.broadcasted_iota(jnp.int32, u.shape, 0)
    u1 = jnp.where(row == 0, hl[1:2], pltpu.roll(u, 1, 0))
    u2 = jnp.where(row == 0, hl[0:1], jnp.where(row == 1, hl[1:2], pltpu.roll(u, 2, 0)))
    c = cb_ref[...] + cw_ref[0:1] * u2
    c = c + cw_ref[1:2] * u1
    c = c + cw_ref[2:3] * u
    o_ref[0] = (_gelu(c) * gt_ref[0]).astype(o_ref.dtype)


def ffn_mid(proj, buf, conv_w, conv_b, tt):
    B, T, _ = proj.shape
    nt = T // tt
    if nt > 1:
        tails = proj[:, :, :D_FF].reshape(B, nt, tt, D_FF)[:, :-1, tt - 2:, :]
        halo = jnp.concatenate([buf[:, None], tails], axis=1)
    else:
        halo = buf[:, None]
    odt = BF16 if tt % 16 == 0 else F32
    return pl.pallas_call(
        _ffn_mid_kernel,
        out_shape=jax.ShapeDtypeStruct((B, T, D_FF), odt),
        grid=(B, nt),
        in_specs=[pl.BlockSpec((1, tt, D_FF), lambda b, t: (b, t, 0)),
                  pl.BlockSpec((1, tt, D_FF), lambda b, t: (b, t, 1)),
                  pl.BlockSpec((1, 1, 2, D_FF), lambda b, t: (b, t, 0, 0)),
                  pl.BlockSpec((3, D_FF), lambda b, t: (0, 0)),
                  pl.BlockSpec((1, D_FF), lambda b, t: (0, 0))],
        out_specs=pl.BlockSpec((1, tt, D_FF), lambda b, t: (b, t, 0)),
        compiler_params=_cparams(("parallel", "parallel")),
        name="ffn_mid",
    )(proj, proj, halo, conv_w, conv_b.reshape(1, D_FF))


def _ffn_in_kernel(x_ref, xh_ref, g_ref, w_ref, buf_ref, cw_ref, cb_ref,
                   act_ref, tail_ref, *, tiles_per_seq, tn):
    seq_start = (pl.program_id(0) % tiles_per_seq) == 0
    h = _rms_rows(x_ref[...], g_ref[...]).astype(BF16)
    hh = _rms_rows(xh_ref[...], g_ref[...]).astype(BF16)
    tm = h.shape[0]
    row = lax.broadcasted_iota(jnp.int32, (tm, tn), 0)
    for j in range(D_FF // tn):
        cols = slice(j * tn, (j + 1) * tn)
        wu = w_ref[:, cols]
        u = jnp.dot(h, wu, preferred_element_type=F32)
        gt = jnp.dot(h, w_ref[:, D_FF + j * tn:D_FF + (j + 1) * tn], preferred_element_type=F32)
        uh = jnp.dot(hh, wu, preferred_element_type=F32)
        hl = jnp.where(seq_start, buf_ref[0, :, cols], uh[HALO - 2:, :])
        u1 = jnp.where(row == 0, hl[1:2], pltpu.roll(u, 1, 0))
        u2 = jnp.where(row == 0, hl[0:1], jnp.where(row == 1, hl[1:2], pltpu.roll(u, 2, 0)))
        c = cb_ref[:, cols] + cw_ref[0:1, cols] * u2
        c = c + cw_ref[1:2, cols] * u1
        c = c + cw_ref[2:3, cols] * u
        act_ref[:, cols] = (_gelu(c) * gt).astype(act_ref.dtype)
        tail_ref[0, :, cols] = u[tm - 2:, :]


def ffn_in(x, g, w, buf, conv_w, conv_b, T, tm, tn):
    M, D = x.shape
    tiles_per_seq = T // tm
    kern = functools.partial(_ffn_in_kernel, tiles_per_seq=tiles_per_seq, tn=tn)
    act, tails = pl.pallas_call(
        kern,
        out_shape=[jax.ShapeDtypeStruct((M, D_FF), BF16),
                   jax.ShapeDtypeStruct((M // tm, 2, D_FF), F32)],
        grid=(M // tm,),
        in_specs=[pl.BlockSpec((tm, D), lambda i: (i, 0)),
                  pl.BlockSpec((HALO, D), lambda i: (jnp.maximum(i * (tm // HALO) - 1, 0), 0)),
                  pl.BlockSpec((1, D), lambda i: (0, 0)),
                  pl.BlockSpec((D, 2 * D_FF), lambda i: (0, 0), pipeline_mode=pl.Buffered(1)),
                  pl.BlockSpec((1, 2, D_FF), lambda i: (i // tiles_per_seq, 0, 0)),
                  pl.BlockSpec((3, D_FF), lambda i: (0, 0)),
                  pl.BlockSpec((1, D_FF), lambda i: (0, 0))],
        out_specs=[pl.BlockSpec((tm, D_FF), lambda i: (i, 0)),
                   pl.BlockSpec((1, 2, D_FF), lambda i: (i, 0, 0))],
        compiler_params=_cparams(("parallel",)),
        name="ffn_in",
    )(x, x, g.reshape(1, D), w, buf, conv_w, conv_b.reshape(1, D_FF))
    return act, tails[tiles_per_seq - 1::tiles_per_seq]


def _kv_kernel(x_ref, g_ref, w_ref, gain_ref, cos_ref, sin_ref, bd_ref,
               kv_ref, selp_ref, winp_ref):
    h = _rms_rows(x_ref[0], g_ref[...]).astype(BF16)
    y = jnp.dot(h, w_ref[...], preferred_element_type=F32)
    cos = cos_ref[...]
    sin = sin_ref[...]
    bd = bd_ref[...]
    nk = N_KV * HEAD_DIM
    kv_ref[0, :, 0:2 * nk] = y[:, 0:2 * nk]
    for br, pack_ref in enumerate((selp_ref, winp_ref)):
        base = 2 * nk * (br + 1)
        kv_ref[0, :, base + nk:base + 2 * nk] = y[:, base + nk:base + 2 * nk]
        for p in range(2):
            kx = y[:, base + LANES * p:base + LANES * (p + 1)]
            kn = kx * lax.rsqrt(_head_ms(kx, bd) + EPS) * gain_ref[br + 1:br + 2, :]
            kr = _rope64(kn, cos, sin)
            kv_ref[0, :, base + LANES * p:base + LANES * (p + 1)] = kr
            vx = y[:, base + nk + LANES * p:base + nk + LANES * (p + 1)]
            even, odd = _pack_pair(kr, vx, pack_ref.dtype)
            pack_ref[0, 2 * p] = even
            pack_ref[0, 2 * p + 1] = odd


def kv_project(x, g, w, gains, cos, sin, bd, tm):
    B, T, D = x.shape
    N = w.shape[1]
    pdt = BF16 if tm % 16 == 0 else F32
    return pl.pallas_call(
        _kv_kernel,
        out_shape=[jax.ShapeDtypeStruct((B, T, N), F32),
                   jax.ShapeDtypeStruct((B, N_KV, T, LANES), pdt),
                   jax.ShapeDtypeStruct((B, N_KV, T, LANES), pdt)],
        grid=(B, T // tm),
        in_specs=[pl.BlockSpec((1, tm, D), lambda b, t: (b, t, 0)),
                  pl.BlockSpec((1, D), lambda b, t: (0, 0)),
                  pl.BlockSpec((D, N), lambda b, t: (0, 0)),
                  pl.BlockSpec((3, LANES), lambda b, t: (0, 0)),
                  pl.BlockSpec((tm, LANES), lambda b, t: (t, 0)),
                  pl.BlockSpec((tm, LANES), lambda b, t: (t, 0)),
                  pl.BlockSpec((LANES, LANES), lambda b, t: (0, 0))],
        out_specs=[pl.BlockSpec((1, tm, N), lambda b, t: (b, t, 0)),
                   pl.BlockSpec((1, N_KV, tm, LANES), lambda b, t: (b, 0, t, 0)),
                   pl.BlockSpec((1, N_KV, tm, LANES), lambda b, t: (b, 0, t, 0))],
        compiler_params=_cparams(("parallel", "parallel")),
        name="kv_project",
    )(x, g.reshape(1, D), w, gains, cos, sin, bd)


def _qg_kernel(x_ref, g_ref, w_ref, gain_ref, cos_ref, sin_ref, bd_ref,
               qc_ref, qr_ref, gate_ref, *, qb, n_qb):
    h = _rms_rows(x_ref[0], g_ref[...]).astype(BF16)
    y = jnp.dot(h, w_ref[...], preferred_element_type=F32)
    cos = cos_ref[...]
    sin = sin_ref[...]
    bd = bd_ref[...]
    nq = N_HEADS * HEAD_DIM
    gate_ref[0] = jax.nn.sigmoid(y[:, nq:nq + LANES])
    lane = lax.broadcasted_iota(jnp.int32, (y.shape[0], LANES), 1)
    lo = lane < HEAD_DIM
    group = N_HEADS // N_KV
    for p in range(N_HEADS // 2):
        qx = y[:, LANES * p:LANES * (p + 1)]
        qn = qx * lax.rsqrt(_head_ms(qx, bd) + EPS) * gain_ref[...]
        qr = _rope64(qn, cos, sin)
        for src, dst in ((qn, qc_ref), (qr, qr_ref)):
            for par in range(2):
                hd = 2 * p + par
                kvh, gi = hd // group, hd % group
                v = src if par == 0 else pltpu.roll(src, HEAD_DIM, 1)
                v = jnp.where(lo, v, 0.0).astype(dst.dtype)
                for j in range(n_qb):
                    dst[0, kvh, j, gi * qb:(gi + 1) * qb, :] = v[j * qb:(j + 1) * qb, :]


def qg_project(x, g, w, gain, cos, sin, bd, tm, qb):
    B, T, D = x.shape
    N = w.shape[1]
    n_qb = tm // qb
    group = N_HEADS // N_KV
    qdt = BF16 if qb % 16 == 0 else F32
    kern = functools.partial(_qg_kernel, qb=qb, n_qb=n_qb)
    qshape = jax.ShapeDtypeStruct((B, N_KV, T // qb, group * qb, LANES), qdt)
    qspec = pl.BlockSpec((1, N_KV, n_qb, group * qb, LANES), lambda b, t: (b, 0, t, 0, 0))
    return pl.pallas_call(
        kern,
        out_shape=[qshape, qshape, jax.ShapeDtypeStruct((B, T, LANES), F32)],
        grid=(B, T // tm),
        in_specs=[pl.BlockSpec((1, tm, D), lambda b, t: (b, t, 0)),
                  pl.BlockSpec((1, D), lambda b, t: (0, 0)),
                  pl.BlockSpec((D, N), lambda b, t: (0, 0)),
                  pl.BlockSpec((1, LANES), lambda b, t: (0, 0)),
                  pl.BlockSpec((tm, LANES), lambda b, t: (t, 0)),
                  pl.BlockSpec((tm, LANES), lambda b, t: (t, 0)),
                  pl.BlockSpec((LANES, LANES), lambda b, t: (0, 0))],
        out_specs=[qspec, qspec, pl.BlockSpec((1, tm, LANES), lambda b, t: (b, t, 0))],
        compiler_params=_cparams(("parallel", "parallel")),
        name="qg_project",
    )(x, g.reshape(1, D), w, gain, cos, sin, bd)


def _pos_bias_kernel(p_ref, w_ref, o_ref):
    o_ref[0] = jnp.dot(p_ref[0].astype(BF16), w_ref[0].astype(BF16), preferred_element_type=F32)


def pos_bias(cmp_pos, cmp_w1):
    K = CMP_LEN * HEAD_DIM
    p = jnp.broadcast_to(cmp_pos.reshape(2, 1, K), (2, 8, K))
    out = pl.pallas_call(
        _pos_bias_kernel,
        out_shape=jax.ShapeDtypeStruct((2, 8, CMP_HID), F32),
        grid=(2,),
        in_specs=[pl.BlockSpec((1, 8, K), lambda c: (c, 0, 0)),
                  pl.BlockSpec((1, K, CMP_HID), lambda c: (c, 0, 0))],
        out_specs=pl.BlockSpec((1, 8, CMP_HID), lambda c: (c, 0, 0)),
        name="pos_bias",
    )(p, cmp_w1)
    return out[:, 0, :]


def _compress_kernel(pt_ref, *refs, n_pg):
    pages = refs[:n_pg]
    w1_ref, pb_ref, w2_ref, gain_ref, out_ref, carry_ref, slab_ref = refs[n_pg:]
    g = pl.program_id(1)

    @pl.when(g == 0)
    def _():
        carry_ref[...] = jnp.zeros_like(carry_ref)

    n = n_pg * (PAGE // CMP_STRIDE)
    lane = lax.broadcasted_iota(jnp.int32, (n, LANES), 1)
    row = lax.broadcasted_iota(jnp.int32, (n, LANES), 0)
    lo = lane < HEAD_DIM
    for pair in range(N_KV // 2):
        res = [jnp.zeros((n, LANES), F32), jnp.zeros((n, LANES), F32)]
        for c in range(2):
            col = c * N_KV * HEAD_DIM + pair * LANES
            for i in range(n_pg):
                slab_ref[PAGE * i:PAGE * (i + 1), :] = pages[i][0, :, col:col + LANES]
            for par in range(2):
                keep = lo if par == 0 else jnp.logical_not(lo)
                acc = jnp.zeros((n, 2 * CMP_HID), F32)
                for s in range(0, CMP_STRIDE, 2):
                    xm = jnp.concatenate(
                        [jnp.where(keep, slab_ref[pl.ds(s + i, n, stride=CMP_STRIDE), :], 0.0)
                         for i in range(2)], axis=1).astype(BF16)
                    acc = acc + jnp.dot(xm, w1_ref[c, s // 2], preferred_element_type=F32)
                p0 = acc[:, :CMP_HID]
                p1 = acc[:, CMP_HID:]
                ci = 2 * (2 * pair + par) + c
                prev = jnp.where(row == 0, carry_ref[ci, 7:8, :], pltpu.roll(p0, 1, 0))
                carry_ref[ci] = p0[n - 8:n, :]
                hid = _gelu(prev + p1 + pb_ref[c:c + 1, :]).astype(BF16)
                res[par] = res[par] + jnp.dot(hid, w2_ref[c], preferred_element_type=F32)
        for par in range(2):
            r = res[par]
            ms = jnp.sum(jnp.where(lo, r * r, 0.0), axis=-1, keepdims=True) * (1.0 / HEAD_DIM)
            kn = r * lax.rsqrt(ms + EPS) * gain_ref[...]
            out_ref[0, 2 * pair + par] = jnp.where(lo, kn, r).astype(out_ref.dtype)


def _page_index(b, g, pt_ref, *, i, n_pg):
    return (pt_ref[b, g * n_pg + i], 0, 0)


def compress(pages_arr, table, w1dup, pb, w2p, gain):
    B, n_pages = table.shape
    n_pg = 16 if n_pages % 16 == 0 else n_pages
    n = n_pg * (PAGE // CMP_STRIDE)
    n_sub = n_pages * (PAGE // CMP_STRIDE)
    width = 2 * N_KV * HEAD_DIM
    in_specs = [pl.BlockSpec((1, PAGE, width), functools.partial(_page_index, i=i, n_pg=n_pg))
                for i in range(n_pg)]
    in_specs += [pl.BlockSpec(w1dup.shape, lambda b, g, pt: (0, 0, 0, 0)),
                 pl.BlockSpec(pb.shape, lambda b, g, pt: (0, 0)),
                 pl.BlockSpec(w2p.shape, lambda b, g, pt: (0, 0, 0)),
                 pl.BlockSpec((1, LANES), lambda b, g, pt: (0, 0))]
    grid_spec = pltpu.PrefetchScalarGridSpec(
        num_scalar_prefetch=1,
        grid=(B, n_pages // n_pg),
        in_specs=in_specs,
        out_specs=pl.BlockSpec((1, N_KV, n, LANES), lambda b, g, pt: (b, 0, g, 0)),
        scratch_shapes=[pltpu.VMEM((2 * N_KV, 8, CMP_HID), F32),
                        pltpu.VMEM((n_pg * PAGE, LANES), F32)],
    )
    return pl.pallas_call(
        functools.partial(_compress_kernel, n_pg=n_pg),
        out_shape=jax.ShapeDtypeStruct((B, N_KV, n_sub, LANES), BF16),
        grid_spec=grid_spec,
        compiler_params=_cparams(("parallel", "arbitrary")),
        name="compress",
    )(table, *([pages_arr] * n_pg), w1dup, pb, w2p, gain)


def _repack_kernel(pt_ref, p0, p1, p2, p3, new_ref, out_ref, *, n_full, t_new):
    t = pl.program_id(1)
    nk = N_KV * HEAD_DIM

    @pl.when(t < n_full)
    def _():
        for i, pg in enumerate((p0, p1, p2, p3)):
            x = pg[0]
            for p in range(2):
                even, odd = _pack_pair(x[:, LANES * p:LANES * (p + 1)],
                                       x[:, nk + LANES * p:nk + LANES * (p + 1)], out_ref.dtype)
                out_ref[0, 2 * p, PAGE * i:PAGE * (i + 1), :] = even
                out_ref[0, 2 * p + 1, PAGE * i:PAGE * (i + 1), :] = odd

    @pl.when(t == n_full)
    def _():
        pad = jnp.zeros((N_KV, KEY_TILE - t_new, LANES), F32)
        out_ref[0] = jnp.concatenate([new_ref[0].astype(F32), pad], axis=1).astype(out_ref.dtype)


def _repack_page_index(b, t, pt_ref, *, i, n_pages):
    return (pt_ref[b, jnp.minimum(4 * t + i, n_pages - 1)], 0, 0)


def repack(pages_arr, table, new_pack):
    B, n_pages = table.shape
    n_full = n_pages // 4
    t_new = new_pack.shape[2]
    width = 2 * N_KV * HEAD_DIM
    in_specs = [pl.BlockSpec((1, PAGE, width),
                             functools.partial(_repack_page_index, i=i, n_pages=n_pages))
                for i in range(4)]
    in_specs.append(pl.BlockSpec((1, N_KV, t_new, LANES), lambda b, t, pt: (b, 0, 0, 0)))
    grid_spec = pltpu.PrefetchScalarGridSpec(
        num_scalar_prefetch=1,
        grid=(B, n_full + 1),
        in_specs=in_specs,
        out_specs=pl.BlockSpec((1, N_KV, KEY_TILE, LANES), lambda b, t, pt: (b, 0, t, 0)),
    )
    return pl.pallas_call(
        functools.partial(_repack_kernel, n_full=n_full, t_new=t_new),
        out_shape=jax.ShapeDtypeStruct((B, N_KV, (n_full + 1) * KEY_TILE, LANES), BF16),
        grid_spec=grid_spec,
        compiler_params=_cparams(("parallel", "arbitrary")),
        name="repack",
    )(table, pages_arr, pages_arr, pages_arr, pages_arr, new_pack)


def _softmax_step(s, mask, kv, m_ref, l_ref, a_ref, k):
    s = jnp.where(mask, s, NEG_INF)
    m_old = m_ref[k]
    m_new = jnp.maximum(m_old, jnp.max(s, axis=-1, keepdims=True))
    alpha = jnp.exp(m_old - m_new)
    p = jnp.where(mask, jnp.exp(s - m_new), 0.0)
    l_ref[k] = alpha * l_ref[k] + jnp.sum(p, axis=-1, keepdims=True)
    a_ref[k] = alpha * a_ref[k] + jnp.dot(p.astype(BF16), kv, preferred_element_type=F32)
    m_ref[k] = m_new


def _nsa_kernel(qi_ref, kt_ref, wt_ref, wf_ref, last_ref,
                qc_ref, qr_ref, gate_ref, cmp_ref, sel_ref, win_ref, wmap_ref,
                o_ref,
                selm_ref, oc_ref, ms_ref, ls_ref, as_ref, mw_ref, lw_ref, aw_ref,
                *, qb, n_sel, n_selp, n_cmp, q0, w_off):
    step = pl.program_id(1)
    qi = qi_ref[step]
    kt = kt_ref[step]
    group = N_HEADS // N_KV
    R = group * qb
    rowq = lax.broadcasted_iota(jnp.int32, (R, 1), 0) & (qb - 1)
    qpos = q0 + qi * qb + rowq
    qpos_q = q0 + qi * qb + lax.broadcasted_iota(jnp.int32, (qb, 1), 0)

    @pl.when(kt == 0)
    def _first():
        for m_ref, l_ref, a_ref in ((ms_ref, ls_ref, as_ref), (mw_ref, lw_ref, aw_ref)):
            m_ref[...] = jnp.full(m_ref.shape, NEG_INF, F32)
            l_ref[...] = jnp.zeros(l_ref.shape, F32)
            a_ref[...] = jnp.zeros(a_ref.shape, F32)
        n_idx = lax.broadcasted_iota(jnp.int32, (1, n_cmp), 1)
        cvalid = (n_idx >= 1) & ((n_idx - 1) * CMP_STRIDE + CMP_LEN - 1 <= qpos)
        blk = lax.broadcasted_iota(jnp.int32, (qb, n_selp), 1)
        blk_f = blk.astype(F32)
        cur = qpos_q >> 6
        forced = (blk == 0) | (blk == cur) | (blk == cur - 1)
        reach = blk * SEL_BLOCK <= qpos_q
        real = blk < n_sel
        for k in range(N_KV):
            ckv = cmp_ref[0, k]
            s = _nt_dot(qc_ref[0, k, 0].astype(BF16), ckv)
            s = jnp.where(cvalid, s, NEG_INF)
            m = jnp.max(s, axis=-1, keepdims=True)
            e = jnp.where(cvalid, jnp.exp(s - m), 0.0)
            p = e / jnp.maximum(jnp.sum(e, axis=-1, keepdims=True), TINY)
            oc_ref[k] = jnp.dot(p.astype(BF16), ckv, preferred_element_type=F32)
            psum = p[0:qb]
            for gi in range(1, group):
                psum = psum + p[gi * qb:(gi + 1) * qb]
            hi = psum.astype(BF16)
            lo = (psum - hi.astype(F32)).astype(BF16)
            imp = _nt_dot(hi, wmap_ref[...]) + _nt_dot(lo, wmap_ref[...])
            v = jnp.where(forced, SEL_FORCE, jnp.where(reach, imp, SEL_NEG))
            v = jnp.where(real, v, -jnp.inf)

            def pick_one(_, carry):
                v, sel = carry
                m = jnp.max(v, axis=-1, keepdims=True)
                first = jnp.min(jnp.where(v == m, blk_f, float(n_selp)), axis=-1, keepdims=True)
                pick = blk_f == first
                sel = jnp.where(pick & (m > 0.5 * SEL_NEG), 1.0, sel)
                return jnp.where(pick, -jnp.inf, v), sel

            _, sel = lax.fori_loop(0, SEL_TOPK, pick_one, (v, jnp.zeros((qb, n_selp), F32)))
            selm_ref[k] = sel

    tok = kt * KEY_TILE + lax.broadcasted_iota(jnp.int32, (1, KEY_TILE), 1)
    blk_of_tok = tok >> 6
    expand = jnp.where(
        lax.broadcasted_iota(jnp.int32, (n_selp, KEY_TILE), 0) == blk_of_tok, 1.0, 0.0).astype(BF16)
    causal = tok <= qpos
    for k in range(N_KV):
        kv = sel_ref[0, k]
        s = _nt_dot(qr_ref[0, k, 0].astype(BF16), kv)
        sm = jnp.dot(selm_ref[k].astype(BF16), expand, preferred_element_type=F32)
        sm = jnp.concatenate([sm] * group, axis=0)
        _softmax_step(s, causal & (sm > 0.5), kv, ms_ref, ls_ref, as_ref, k)

    @pl.when(wf_ref[step] == 1)
    def _window():
        wpos = w_off + wt_ref[step] * KEY_TILE + lax.broadcasted_iota(jnp.int32, (1, KEY_TILE), 1)
        mask = (wpos <= qpos) & (wpos > qpos - WINDOW) & (wpos >= 0)
        for k in range(N_KV):
            kv = win_ref[0, k]
            s = _nt_dot(qr_ref[0, k, 0].astype(BF16), kv)
            _softmax_step(s, mask, kv, mw_ref, lw_ref, aw_ref, k)

    @pl.when(last_ref[step] == 1)
    def _finish():
        gate = gate_ref[0]
        for k in range(N_KV):
            for gi in range(group):
                hd = k * group + gi
                rs = slice(gi * qb, (gi + 1) * qb)
                o_s = as_ref[k, rs, :] / jnp.maximum(ls_ref[k, rs, :], TINY)
                o_w = aw_ref[k, rs, :] / jnp.maximum(lw_ref[k, rs, :], TINY)
                o = gate[:, 3 * hd:3 * hd + 1] * oc_ref[k, rs, :]
                o = o + gate[:, 3 * hd + 1:3 * hd + 2] * o_s
                o = o + gate[:, 3 * hd + 2:3 * hd + 3] * o_w
                o_ref[0, :, LANES * hd:LANES * (hd + 1)] = o.astype(o_ref.dtype)


def _nsa_tables(T, qb, q0, w_off):
    rows = []
    for qi in range(T // qb):
        q_lo = q0 + qi * qb
        q_hi = q_lo + qb - 1
        last_kt = q_hi // KEY_TILE
        w_lo = max(q_lo - WINDOW + 1, w_off)
        wt0 = (w_lo - w_off) // KEY_TILE
        wt1 = (q_hi - w_off) // KEY_TILE
        n_w = wt1 - wt0 + 1
        assert n_w <= last_kt + 1
        for kt in range(last_kt + 1):
            rows.append((qi, kt, wt0 + min(kt, n_w - 1), int(kt < n_w), int(kt == last_kt)))
    tab = np.asarray(rows, np.int32).T
    return [jnp.asarray(tab[i]) for i in range(5)]


def _overlap_map(n_cmp_rows, n_selp):
    m = np.arange(n_cmp_rows)[None, :]
    s = np.arange(n_selp)[:, None]
    c0 = (m - 1) * CMP_STRIDE
    ov = np.minimum(c0 + CMP_LEN, s * SEL_BLOCK + SEL_BLOCK) - np.maximum(c0, s * SEL_BLOCK)
    w = np.maximum(ov, 0).astype(np.float32) / CMP_LEN
    w[:, 0] = 0.0
    return jnp.asarray(w, BF16)


def nsa_attend(qc, qr, gates, cmp_p, sel_p, win_p, T, qb, q0, w_off, n_sel):
    B = qc.shape[0]
    group = N_HEADS // N_KV
    R = group * qb
    n_cmp = cmp_p.shape[2]
    n_selp = -(-n_sel // LANES) * LANES
    tabs = _nsa_tables(T, qb, q0, w_off)
    n_steps = int(tabs[0].shape[0])
    wmap = _overlap_map(n_cmp, n_selp)
    odt = BF16 if qb % 16 == 0 else F32
    kern = functools.partial(_nsa_kernel, qb=qb, n_sel=n_sel, n_selp=n_selp, n_cmp=n_cmp,
                             q0=q0, w_off=w_off)
    qspec = pl.BlockSpec((1, N_KV, 1, R, LANES), lambda b, s, qi, kt, wt, wf, la: (b, 0, qi[s], 0, 0))
    grid_spec = pltpu.PrefetchScalarGridSpec(
        num_scalar_prefetch=5,
        grid=(B, n_steps),
        in_specs=[
            qspec, qspec,
            pl.BlockSpec((1, qb, LANES), lambda b, s, qi, kt, wt, wf, la: (b, qi[s], 0)),
            pl.BlockSpec((1, N_KV, n_cmp, LANES), lambda b, s, qi, kt, wt, wf, la: (b, 0, 0, 0)),
            pl.BlockSpec((1, N_KV, KEY_TILE, LANES), lambda b, s, qi, kt, wt, wf, la: (b, 0, kt[s], 0)),
            pl.BlockSpec((1, N_KV, KEY_TILE, LANES), lambda b, s, qi, kt, wt, wf, la: (b, 0, wt[s], 0)),
            pl.BlockSpec((n_selp, n_cmp), lambda b, s, qi, kt, wt, wf, la: (0, 0)),
        ],
        out_specs=pl.BlockSpec((1, qb, N_HEADS * LANES), lambda b, s, qi, kt, wt, wf, la: (b, qi[s], 0)),
        scratch_shapes=[
            pltpu.VMEM((N_KV, qb, n_selp), F32),
            pltpu.VMEM((N_KV, R, LANES), F32),
            pltpu.VMEM((N_KV, R, 1), F32), pltpu.VMEM((N_KV, R, 1), F32), pltpu.VMEM((N_KV, R, LANES), F32),
            pltpu.VMEM((N_KV, R, 1), F32), pltpu.VMEM((N_KV, R, 1), F32), pltpu.VMEM((N_KV, R, LANES), F32),
        ],
    )
    return pl.pallas_call(
        kern,
        out_shape=jax.ShapeDtypeStruct((B, T, N_HEADS * LANES), odt),
        grid_spec=grid_spec,
        compiler_params=_cparams(("parallel", "arbitrary")),
        name="nsa_attend",
    )(*tabs, qc, qr, gates, cmp_p, sel_p, win_p, wmap)


def _qgt_kernel(x_ref, g_ref, w_ref, gain_ref, cos_ref, sin_ref, bd_ref,
                qc_ref, qr_ref, gate_ref, *, qb, n_qb):
    h = _rms_rows(x_ref[0], g_ref[...]).astype(BF16)
    y = jnp.dot(h, w_ref[...], preferred_element_type=F32)
    cos = cos_ref[...]
    sin = sin_ref[...]
    bd = bd_ref[...]
    nq = N_HEADS * HEAD_DIM
    group = N_HEADS // N_KV
    gate_t = jax.nn.sigmoid(y[:, nq:nq + LANES]).T
    for j in range(n_qb):
        gate_ref[0, j] = gate_t[:, j * qb:(j + 1) * qb]
    pad = jnp.zeros((N_KV, n_qb, HEAD_DIM, group * qb), qc_ref.dtype)
    qc_ref[0, :, :, HEAD_DIM:, :] = pad
    qr_ref[0, :, :, HEAD_DIM:, :] = pad
    for p in range(N_HEADS // 2):
        qx = y[:, LANES * p:LANES * (p + 1)]
        qn = qx * lax.rsqrt(_head_ms(qx, bd) + EPS) * gain_ref[...]
        qr = _rope64(qn, cos, sin)
        for src, dst in ((qn, qc_ref), (qr, qr_ref)):
            st = src.T.astype(dst.dtype)
            for par in range(2):
                kvh, gi = divmod(2 * p + par, group)
                for j in range(n_qb):
                    dst[0, kvh, j, 0:HEAD_DIM, gi * qb:(gi + 1) * qb] = (
                        st[par * HEAD_DIM:(par + 1) * HEAD_DIM, j * qb:(j + 1) * qb])


def qg_project_t(x, g, w, gain, cos, sin, bd, tm, qb):
    B, T, D = x.shape
    N = w.shape[1]
    n_qb = tm // qb
    group = N_HEADS // N_KV
    kern = functools.partial(_qgt_kernel, qb=qb, n_qb=n_qb)
    qshape = jax.ShapeDtypeStruct((B, N_KV, T // qb, LANES, group * qb), BF16)
    qspec = pl.BlockSpec((1, N_KV, n_qb, LANES, group * qb), lambda b, t: (b, 0, t, 0, 0))
    return pl.pallas_call(
        kern,
        out_shape=[qshape, qshape, jax.ShapeDtypeStruct((B, T // qb, LANES, qb), F32)],
        grid=(B, T // tm),
        in_specs=[pl.BlockSpec((1, tm, D), lambda b, t: (b, t, 0)),
                  pl.BlockSpec((1, D), lambda b, t: (0, 0)),
                  pl.BlockSpec((D, N), lambda b, t: (0, 0)),
                  pl.BlockSpec((1, LANES), lambda b, t: (0, 0)),
                  pl.BlockSpec((tm, LANES), lambda b, t: (t, 0)),
                  pl.BlockSpec((tm, LANES), lambda b, t: (t, 0)),
                  pl.BlockSpec((LANES, LANES), lambda b, t: (0, 0))],
        out_specs=[qspec, qspec, pl.BlockSpec((1, n_qb, LANES, qb), lambda b, t: (b, t, 0, 0))],
        compiler_params=_cparams(("parallel", "parallel")),
        name="qg_project_t",
    )(x, g.reshape(1, D), w, gain, cos, sin, bd)


def _nsa_t_kernel(qc_ref, qr_ref, gate_ref, cmp_ref, sel_ref, win_ref, wmap_ref, o_ref,
                  selneg_ref, m_ref, l_ref, acc_ref,
                  *, qb, n_sel, n_selp, n_cmp, q0, w_off, w_rows, l_win):
    qi = pl.program_id(1)
    group = N_HEADS // N_KV
    R = group * qb
    blocks_per_tile = KEY_TILE // SEL_BLOCK
    q_lo = q0 + qi * qb
    qpos_q = q_lo + lax.broadcasted_iota(jnp.int32, (1, qb), 1)
    n_kt = (q_lo + qb - 1) // KEY_TILE + 1
    w_start = pl.multiple_of(jnp.clip(q_lo - WINDOW - w_off, 0, l_win - w_rows), LANES)
    gate = gate_ref[0, 0]

    def lanes4(a):
        return jnp.concatenate([a] * group, axis=1)

    m_idx = lax.broadcasted_iota(jnp.int32, (n_cmp, qb), 0)
    cvalid = (m_idx >= 1) & ((m_idx - 1) * CMP_STRIDE + CMP_LEN - 1 <= qpos_q)
    cbias = lanes4(jnp.where(cvalid, 0.0, NEG_INF))
    any_c = lanes4(qpos_q >= CMP_LEN - 1)
    blk = lax.broadcasted_iota(jnp.int32, (n_selp, qb), 0)
    blk_f = blk.astype(F32)
    cur = qpos_q >> 6
    forced = (blk == 0) | (blk == cur) | (blk == cur - 1)
    reach = blk * SEL_BLOCK <= qpos_q
    real = blk < n_sel
    wpos = w_off + w_start + lax.broadcasted_iota(jnp.int32, (w_rows, qb), 0)
    wbias = lanes4(jnp.where((wpos <= qpos_q) & (wpos > qpos_q - WINDOW) & (wpos >= 0), 0.0, NEG_INF))
    row_t = lax.broadcasted_iota(jnp.int32, (KEY_TILE, qb), 0)

    for k in range(N_KV):
        ckv = cmp_ref[0, k]
        s = jnp.dot(ckv, qc_ref[0, k, 0], preferred_element_type=F32) + cbias
        e = jnp.exp2(s - jnp.max(s, axis=0, keepdims=True))
        den = jnp.maximum(jnp.sum(e, axis=0, keepdims=True), TINY)
        p = e * jnp.where(any_c, 1.0 / den, 0.0)
        oc = _tn_dot(ckv, p.astype(BF16))
        psum = p[:, 0:qb]
        for gi in range(1, group):
            psum = psum + p[:, gi * qb:(gi + 1) * qb]
        hi = psum.astype(BF16)
        lo = (psum - hi.astype(F32)).astype(BF16)
        imp = (jnp.dot(wmap_ref[...], hi, preferred_element_type=F32)
               + jnp.dot(wmap_ref[...], lo, preferred_element_type=F32))
        v = jnp.where(forced, SEL_FORCE, jnp.where(reach, imp, SEL_NEG))
        v = jnp.where(real, v, -jnp.inf)

        def pick_one(_, carry):
            v, sel = carry
            m = jnp.max(v, axis=0, keepdims=True)
            first = jnp.min(jnp.where(v == m, blk_f, float(n_selp)), axis=0, keepdims=True)
            pick = blk_f == first
            sel = jnp.where(pick & (m > 0.5 * SEL_NEG), 0.0, sel)
            return jnp.where(pick, -jnp.inf, v), sel

        _, sel = lax.fori_loop(0, SEL_TOPK, pick_one, (v, jnp.full((n_selp, qb), NEG_INF, F32)))
        selneg_ref[...] = sel

        qr = qr_ref[0, k, 0]
        m_ref[...] = jnp.full(m_ref.shape, NEG_INF, F32)
        l_ref[...] = jnp.zeros(l_ref.shape, F32)
        acc_ref[...] = jnp.zeros(acc_ref.shape, F32)

        def tile(kt, carry):
            start = pl.multiple_of(kt * KEY_TILE, KEY_TILE)
            kv = sel_ref[0, k, pl.ds(start, KEY_TILE), :]
            s = jnp.dot(kv, qr, preferred_element_type=F32)
            pieces = [jnp.broadcast_to(selneg_ref[pl.ds(kt * blocks_per_tile + j, 1), :], (SEL_BLOCK, qb))
                      for j in range(blocks_per_tile)]
            bias = jnp.concatenate(pieces, axis=0) + jnp.where(start + row_t <= qpos_q, 0.0, NEG_INF)
            s = s + lanes4(bias)
            m_old = m_ref[...]
            m_new = jnp.maximum(m_old, jnp.max(s, axis=0, keepdims=True))
            alpha = jnp.exp2(m_old - m_new)
            p = jnp.exp2(s - m_new)
            l_ref[...] = alpha * l_ref[...] + jnp.sum(p, axis=0, keepdims=True)
            acc_ref[...] = alpha * acc_ref[...] + _tn_dot(kv, p.astype(BF16))
            m_ref[...] = m_new
            return carry

        lax.fori_loop(0, n_kt, tile, 0)
        o_s = acc_ref[...] * (1.0 / jnp.maximum(l_ref[...], TINY))

        wkv = win_ref[0, k, pl.ds(w_start, w_rows), :]
        s = jnp.dot(wkv, qr, preferred_element_type=F32) + wbias
        e = jnp.exp2(s - jnp.max(s, axis=0, keepdims=True))
        den = jnp.maximum(jnp.sum(e, axis=0, keepdims=True), TINY)
        o_w = _tn_dot(wkv, e.astype(BF16)) * (1.0 / den)

        for gi in range(group):
            hd = k * group + gi
            sl = slice(gi * qb, (gi + 1) * qb)
            o = gate[3 * hd:3 * hd + 1, :] * oc[:, sl]
            o = o + gate[3 * hd + 1:3 * hd + 2, :] * o_s[:, sl]
            o = o + gate[3 * hd + 2:3 * hd + 3, :] * o_w[:, sl]
            o_ref[0, :, LANES * hd:LANES * (hd + 1)] = o.T.astype(o_ref.dtype)


def nsa_attend_t(qc, qr, gates, cmp_p, sel_p, win_p, T, qb, q0, w_off, n_sel):
    B = qc.shape[0]
    group = N_HEADS // N_KV
    R = group * qb
    n_cmp = cmp_p.shape[2]
    n_selp = -(-n_sel // LANES) * LANES
    l_sel = sel_p.shape[2]
    l_win = win_p.shape[2]
    w_rows = WINDOW + max(qb, LANES)
    assert l_win >= w_rows and l_sel >= ((q0 + T - 1) // KEY_TILE + 1) * KEY_TILE
    wmap = _overlap_map(n_cmp, n_selp)
    kern = functools.partial(_nsa_t_kernel, qb=qb, n_sel=n_sel, n_selp=n_selp, n_cmp=n_cmp,
                             q0=q0, w_off=w_off, w_rows=w_rows, l_win=l_win)
    qspec = pl.BlockSpec((1, N_KV, 1, LANES, R), lambda b, i: (b, 0, i, 0, 0))
    resident = dict(pipeline_mode=pl.Buffered(1))
    return pl.pallas_call(
        kern,
        out_shape=jax.ShapeDtypeStruct((B, T, N_HEADS * LANES), BF16),
        grid=(B, T // qb),
        in_specs=[
            qspec, qspec,
            pl.BlockSpec((1, 1, LANES, qb), lambda b, i: (b, i, 0, 0)),
            pl.BlockSpec((1, N_KV, n_cmp, LANES), lambda b, i: (b, 0, 0, 0)),
            pl.BlockSpec((1, N_KV, l_sel, LANES), lambda b, i: (b, 0, 0, 0), **resident),
            pl.BlockSpec((1, N_KV, l_win, LANES), lambda b, i: (b, 0, 0, 0), **resident),
            pl.BlockSpec((n_selp, n_cmp), lambda b, i: (0, 0)),
        ],
        out_specs=pl.BlockSpec((1, qb, N_HEADS * LANES), lambda b, i: (b, i, 0)),
        scratch_shapes=[
            pltpu.VMEM((n_selp, qb), F32),
            pltpu.VMEM((1, R), F32), pltpu.VMEM((1, R), F32), pltpu.VMEM((LANES, R), F32),
        ],
        compiler_params=_cparams(("parallel", "arbitrary")),
        name="nsa_attend_t",
    )(qc, qr, gates, cmp_p, sel_p, win_p, wmap)


def _nsa_fast_kernel(shift_ref, qc_ref, qr_ref, gate_ref, cmp_ref, sel_ref, win_ref, wmap_ref, hot_ref,
                     o_ref, qaug_ref, oc_ref, l_ref, acc_ref,
                     *, qb, n_sel, n_selp, n_cmp, q0, w_off, w_rows, l_win):
    qi = pl.program_id(1)
    group = N_HEADS // N_KV
    R = group * qb
    q_lo = q0 + qi * qb
    qpos_q = q_lo + lax.broadcasted_iota(jnp.int32, (1, qb), 1)
    n_kt = (q_lo + qb - 1) // KEY_TILE + 1
    w_start = pl.multiple_of(jnp.clip(q_lo - WINDOW - w_off, 0, l_win - w_rows), LANES)
    shift_c = shift_ref[0]
    shift_s = shift_ref[1]
    shift_w = shift_ref[2]

    def lanes4(a):
        return jnp.concatenate([a] * group, axis=1)

    m_idx = lax.broadcasted_iota(jnp.int32, (n_cmp, qb), 0)
    cvalid = (m_idx >= 1) & ((m_idx - 1) * CMP_STRIDE + CMP_LEN - 1 <= qpos_q)
    cbias = lanes4(jnp.where(cvalid, -shift_c, NEG_INF))
    any_c = lanes4(qpos_q >= CMP_LEN - 1)
    blk = lax.broadcasted_iota(jnp.int32, (n_selp, qb), 0)
    blk_f = blk.astype(F32)
    cur = qpos_q >> 6
    forced = (blk == 0) | (blk == cur) | (blk == cur - 1)
    reach = blk * SEL_BLOCK <= qpos_q
    real = blk < n_sel

    imps = []
    for k in range(N_KV):
        ckv = cmp_ref[0, k]
        e = jnp.exp2(jnp.dot(ckv, qc_ref[0, k, 0], preferred_element_type=F32) + cbias)
        den = jnp.maximum(jnp.sum(e, axis=0, keepdims=True), TINY)
        p = e * jnp.where(any_c, 1.0 / den, 0.0)
        oc_ref[k] = _tn_dot(ckv, p.astype(BF16))
        psum = p[:, 0:qb]
        for gi in range(1, group):
            psum = psum + p[:, gi * qb:(gi + 1) * qb]
        hi = psum.astype(BF16)
        lo = (psum - hi.astype(F32)).astype(BF16)
        imp = (jnp.dot(wmap_ref[...], hi, preferred_element_type=F32)
               + jnp.dot(wmap_ref[...], lo, preferred_element_type=F32))
        v = jnp.where(forced, SEL_FORCE, jnp.where(reach, imp, SEL_NEG))
        imps.append(jnp.where(real, v, -jnp.inf))

    def pick_one(_, vs):
        out = []
        for v in vs:
            m = jnp.max(v, axis=0, keepdims=True)
            first = jnp.min(jnp.where(v == m, blk_f, float(n_selp)), axis=0, keepdims=True)
            out.append(jnp.where(blk_f == first, -jnp.inf, v))
        return tuple(out)

    marked = lax.fori_loop(0, SEL_TOPK, pick_one, tuple(imps))
    for k in range(N_KV):
        sel = jnp.where((marked[k] == -jnp.inf) & (imps[k] > 0.5 * SEL_NEG), -shift_s, NEG_INF)
        qaug_ref[k, 0:LANES, :] = qr_ref[0, k, 0]
        qaug_ref[k, LANES:, :] = lanes4(sel).astype(BF16)

    l_ref[...] = jnp.zeros(l_ref.shape, F32)
    acc_ref[...] = jnp.zeros(acc_ref.shape, F32)

    def tile(start, rows, causal_bias):
        hot = hot_ref[pl.ds(start, rows), :]
        for k in range(N_KV):
            kv = sel_ref[0, k, pl.ds(start, rows), :]
            s = jnp.dot(jnp.concatenate([kv, hot], axis=1), qaug_ref[k], preferred_element_type=F32)
            if causal_bias is not None:
                s = s + causal_bias
            p = jnp.exp2(s)
            l_ref[k] += jnp.sum(p.reshape(rows // 8, 8, R), axis=0)
            acc_ref[k] += _tn_dot(kv, p.astype(BF16))

    def double_tile(i, carry):
        tile(pl.multiple_of(i * 2 * KEY_TILE, 2 * KEY_TILE), 2 * KEY_TILE, None)
        return carry

    n_below = n_kt - 1
    lax.fori_loop(0, n_below // 2, double_tile, 0)

    @pl.when(n_below % 2 == 1)
    def _():
        tile(pl.multiple_of((n_below - 1) * KEY_TILE, KEY_TILE), KEY_TILE, None)

    d_start = pl.multiple_of(n_below * KEY_TILE, KEY_TILE)
    row_t = lax.broadcasted_iota(jnp.int32, (KEY_TILE, qb), 0)
    tile(d_start, KEY_TILE, lanes4(jnp.where(d_start + row_t <= qpos_q, 0.0, NEG_INF)))

    wpos = w_off + w_start + lax.broadcasted_iota(jnp.int32, (w_rows, qb), 0)
    wvalid = (wpos <= qpos_q) & (wpos > qpos_q - WINDOW) & (wpos >= 0)
    wbias = lanes4(jnp.where(wvalid, -shift_w, NEG_INF))
    gate = gate_ref[0, 0]
    for k in range(N_KV):
        wkv = win_ref[0, k, pl.ds(w_start, w_rows), :]
        e = jnp.exp2(jnp.dot(wkv, qr_ref[0, k, 0], preferred_element_type=F32) + wbias)
        den = jnp.maximum(jnp.sum(e, axis=0, keepdims=True), TINY)
        o_w = _tn_dot(wkv, e.astype(BF16)) * (1.0 / den)
        l_s = jnp.maximum(jnp.sum(l_ref[k], axis=0, keepdims=True), TINY)
        o_s = acc_ref[k] * (1.0 / l_s)
        oc = oc_ref[k]
        for gi in range(group):
            hd = k * group + gi
            sl = slice(gi * qb, (gi + 1) * qb)
            o = gate[3 * hd:3 * hd + 1, :] * oc[:, sl]
            o = o + gate[3 * hd + 1:3 * hd + 2, :] * o_s[:, sl]
            o = o + gate[3 * hd + 2:3 * hd + 3, :] * o_w[:, sl]
            o_ref[0, :, LANES * hd:LANES * (hd + 1)] = o.T.astype(o_ref.dtype)


def nsa_attend_fast(shifts, qc, qr, gates, cmp_p, sel_p, win_p, T, qb, q0, w_off, n_sel):
    B = qc.shape[0]
    group = N_HEADS // N_KV
    R = group * qb
    n_cmp = cmp_p.shape[2]
    n_selp = -(-n_sel // LANES) * LANES
    l_sel = sel_p.shape[2]
    l_win = win_p.shape[2]
    w_rows = WINDOW + max(qb, LANES)
    assert l_win >= w_rows and l_sel >= ((q0 + T - 1) // KEY_TILE + 1) * KEY_TILE
    assert n_sel >= SEL_TOPK and q0 % qb == 0 and KEY_TILE % qb == 0
    wmap = _overlap_map(n_cmp, n_selp)
    hot = np.zeros((l_sel, n_selp), np.float32)
    hot[np.arange(l_sel), np.arange(l_sel) // SEL_BLOCK] = 1.0
    hot = jnp.asarray(hot, BF16)
    kern = functools.partial(_nsa_fast_kernel, qb=qb, n_sel=n_sel, n_selp=n_selp, n_cmp=n_cmp,
                             q0=q0, w_off=w_off, w_rows=w_rows, l_win=l_win)
    qspec = pl.BlockSpec((1, N_KV, 1, LANES, R), lambda b, i, sh: (b, 0, i, 0, 0))
    resident = dict(pipeline_mode=pl.Buffered(1))
    grid_spec = pltpu.PrefetchScalarGridSpec(
        num_scalar_prefetch=1,
        grid=(B, T // qb),
        in_specs=[
            qspec, qspec,
            pl.BlockSpec((1, 1, LANES, qb), lambda b, i, sh: (b, i, 0, 0)),
            pl.BlockSpec((1, N_KV, n_cmp, LANES), lambda b, i, sh: (b, 0, 0, 0)),
            pl.BlockSpec((1, N_KV, l_sel, LANES), lambda b, i, sh: (b, 0, 0, 0), **resident),
            pl.BlockSpec((1, N_KV, l_win, LANES), lambda b, i, sh: (b, 0, 0, 0), **resident),
            pl.BlockSpec((n_selp, n_cmp), lambda b, i, sh: (0, 0)),
            pl.BlockSpec((l_sel, n_selp), lambda b, i, sh: (0, 0), **resident),
        ],
        out_specs=pl.BlockSpec((1, qb, N_HEADS * LANES), lambda b, i, sh: (b, i, 0)),
        scratch_shapes=[
            pltpu.VMEM((N_KV, 2 * LANES, R), BF16),
            pltpu.VMEM((N_KV, LANES, R), F32),
            pltpu.VMEM((N_KV, 8, R), F32),
            pltpu.VMEM((N_KV, LANES, R), F32),
        ],
    )
    return pl.pallas_call(
        kern,
        out_shape=jax.ShapeDtypeStruct((B, T, N_HEADS * LANES), BF16),
        grid_spec=grid_spec,
        compiler_params=_cparams(("parallel", "arbitrary")),
        name="nsa_attend_fast",
    )(shifts, qc, qr, gates, cmp_p, sel_p, win_p, wmap, hot)


def _nsa_dec_kernel(pt_ref, *refs, n_pg, qb, n_sel, n_selp, n_cmp, q0, w_off):
    pages = refs[:n_pg]
    (qcp_ref, qbd_ref, gate_ref, cmp_ref, new_ref, win_ref, wmap_ref, fold_ref, hot_ref,
     o_ref, qaug_ref, oc_ref, m_ref, l_ref, acc_ref) = refs[n_pg:]
    g = pl.program_id(1)
    nk = N_KV * HEAD_DIM
    t_new = new_ref.shape[1]
    lane = lax.broadcasted_iota(jnp.int32, (1, LANES), 1)
    qpos = q0 + (lane & (qb - 1))

    @pl.when(g == 0)
    def _first():
        s = jnp.dot(cmp_ref[0, 0], qcp_ref[0, 0], preferred_element_type=F32)
        for k in range(1, N_KV):
            s = s + jnp.dot(cmp_ref[0, k], qcp_ref[0, k], preferred_element_type=F32)
        m_idx = lax.broadcasted_iota(jnp.int32, (n_cmp, LANES), 0)
        cvalid = (m_idx >= 1) & ((m_idx - 1) * CMP_STRIDE + CMP_LEN - 1 <= qpos)
        s = s + jnp.where(cvalid, 0.0, NEG_INF)
        e = jnp.exp2(s - jnp.max(s, axis=0, keepdims=True))
        den = jnp.maximum(jnp.sum(e, axis=0, keepdims=True), TINY)
        p = e * jnp.where(qpos >= CMP_LEN - 1, 1.0 / den, 0.0)
        pb = p.astype(BF16)
        for k in range(N_KV):
            oc_ref[k * HEAD_DIM:(k + 1) * HEAD_DIM, :] = _tn_dot(cmp_ref[0, k], pb)[HEAD_DIM:, :]
        fold = fold_ref[...]
        p_lo = (p - pb.astype(F32)).astype(BF16)
        psum = jnp.dot(pb, fold, preferred_element_type=F32) + jnp.dot(p_lo, fold, preferred_element_type=F32)
        hi = psum.astype(BF16)
        lo = (psum - hi.astype(F32)).astype(BF16)
        imp = (jnp.dot(wmap_ref[...], hi, preferred_element_type=F32)
               + jnp.dot(wmap_ref[...], lo, preferred_element_type=F32))
        blk = lax.broadcasted_iota(jnp.int32, (n_selp, LANES), 0)
        blk_f = blk.astype(F32)
        cur = qpos >> 6
        forced = (blk == 0) | (blk == cur) | (blk == cur - 1)
        v = jnp.where(forced, SEL_FORCE, jnp.where(blk * SEL_BLOCK <= qpos, imp, SEL_NEG))
        v = jnp.where(blk < n_sel, v, -jnp.inf)

        def pick_one(_, carry):
            v, sel = carry
            m = jnp.max(v, axis=0, keepdims=True)
            first = jnp.min(jnp.where(v == m, blk_f, float(n_selp)), axis=0, keepdims=True)
            pick = blk_f == first
            sel = jnp.where(pick & (m > 0.5 * SEL_NEG), 0.0, sel)
            return jnp.where(pick, -jnp.inf, v), sel

        _, sel = lax.fori_loop(0, SEL_TOPK, pick_one, (v, jnp.full((n_selp, LANES), NEG_INF, F32)))
        qaug_ref[0:nk, :] = qbd_ref[0]
        qaug_ref[nk:, :] = sel.astype(BF16)
        m_ref[...] = jnp.full(m_ref.shape, NEG_INF, F32)
        l_ref[...] = jnp.zeros(l_ref.shape, F32)
        acc_ref[...] = jnp.zeros(acc_ref.shape, F32)

    def attend(kx, vx, hot, bias):
        s = jnp.dot(jnp.concatenate([kx, hot], axis=1), qaug_ref[...], preferred_element_type=F32)
        if bias is not None:
            s = s + bias
        m_old = m_ref[...]
        m_new = jnp.maximum(m_old, jnp.max(s, axis=0, keepdims=True))
        alpha = jnp.exp2(m_old - m_new)
        p = jnp.exp2(s - m_new)
        l_ref[...] = alpha * l_ref[...] + jnp.sum(p.reshape(p.shape[0] // 8, 8, LANES), axis=0)
        acc_ref[...] = alpha * acc_ref[...] + _tn_dot(vx, p.astype(BF16))
        m_ref[...] = m_new

    per_tile = KEY_TILE // PAGE
    for t in range(n_pg // per_tile):
        x = jnp.concatenate([pages[per_tile * t + i][0] for i in range(per_tile)], axis=0).astype(BF16)
        start = pl.multiple_of((g * (n_pg // per_tile) + t) * KEY_TILE, KEY_TILE)
        attend(x[:, :nk], x[:, nk:], hot_ref[pl.ds(start, KEY_TILE), :], None)

    @pl.when(g == pl.num_programs(1) - 1)
    def _last():
        pad = jnp.zeros((HALO - t_new, 2 * nk), F32)
        row = lax.broadcasted_iota(jnp.int32, (HALO, LANES), 0)
        new_ok = (row < t_new) & (q0 + row <= qpos)
        xn = jnp.concatenate([new_ref[0, :, 2 * nk:4 * nk], pad], axis=0).astype(BF16)
        hot_new = jnp.where(
            lax.broadcasted_iota(jnp.int32, (HALO, n_selp), 1) == q0 // SEL_BLOCK, 1.0, 0.0).astype(BF16)
        attend(xn[:, :nk], xn[:, nk:], hot_new, jnp.where(new_ok, 0.0, NEG_INF))

        qbd = qbd_ref[0]
        xw = win_ref[0].astype(BF16)
        wrow = lax.broadcasted_iota(jnp.int32, (xw.shape[0], LANES), 0)
        wpos = w_off + wrow
        wvalid = (wpos <= qpos) & (wpos > qpos - WINDOW) & (wpos >= 0)
        s_c = jnp.dot(xw[:, :nk], qbd, preferred_element_type=F32) + jnp.where(wvalid, 0.0, NEG_INF)
        xwn = jnp.concatenate([new_ref[0, :, 4 * nk:6 * nk], pad], axis=0).astype(BF16)
        npos = q0 + row
        nvalid = (row < t_new) & (npos <= qpos) & (npos > qpos - WINDOW)
        s_n = jnp.dot(xwn[:, :nk], qbd, preferred_element_type=F32) + jnp.where(nvalid, 0.0, NEG_INF)
        m_w = jnp.maximum(jnp.max(s_c, axis=0, keepdims=True), jnp.max(s_n, axis=0, keepdims=True))
        e_c = jnp.exp2(s_c - m_w)
        e_n = jnp.exp2(s_n - m_w)
        den = jnp.sum(e_c, axis=0, keepdims=True) + jnp.sum(e_n, axis=0, keepdims=True)
        o_w = _tn_dot(xw[:, nk:], e_c.astype(BF16)) + _tn_dot(xwn[:, nk:], e_n.astype(BF16))
        o_w = o_w * (1.0 / jnp.maximum(den, TINY))
        l_s = jnp.maximum(jnp.sum(l_ref[...], axis=0, keepdims=True), TINY)
        o_s = acc_ref[...] * (1.0 / l_s)
        gate = gate_ref[0]
        o_ref[0] = gate[0:1, :] * oc_ref[...] + gate[1:2, :] * o_s + gate[2:3, :] * o_w


def _dec_page_index(b, g, pt, *, i, n_pg):
    return (pt[b, g * n_pg + i], 0, 0)


def nsa_attend_dec(table, qc, qr, gates, cmp_p, sel_pages, kvp, cache_win, qb, q0, n_sel):
    B, n_pages = table.shape
    group = N_HEADS // N_KV
    assert N_KV * group * qb == LANES
    nk = N_KV * HEAD_DIM
    n_pg = 16 if n_pages % 16 == 0 else n_pages
    n_cmp = cmp_p.shape[2]
    n_selp = -(-n_sel // LANES) * LANES
    wl = cache_win.shape[1]
    eye = jnp.eye(N_KV, dtype=F32)

    def spread(q):
        qt = q[:, :, 0, :, :HEAD_DIM].astype(F32).transpose(0, 1, 3, 2)
        return jnp.einsum("bkdr,kj->bkdjr", qt, eye).reshape(B, N_KV, HEAD_DIM, LANES)

    qc_pad = jnp.pad(spread(qc), ((0, 0), (0, 0), (0, LANES - HEAD_DIM), (0, 0))).astype(BF16)
    q_bd = spread(qr).reshape(B, nk, LANES).astype(BF16)
    gate_l = gates[:, :, :3 * N_HEADS].reshape(B, qb, N_KV, group, 3).transpose(0, 4, 2, 3, 1)
    gate_l = gate_l.reshape(B, 3, LANES)
    wmap = _overlap_map(n_cmp, n_selp)
    lane = np.arange(LANES)
    fold = jnp.asarray((lane[:, None] // (group * qb) == lane[None, :] // (group * qb))
                       & (lane[:, None] % qb == lane[None, :] % qb), BF16)
    l_past = n_pages * PAGE
    hot = np.zeros((l_past, n_selp), np.float32)
    hot[np.arange(l_past), np.arange(l_past) // SEL_BLOCK] = 1.0
    hot = jnp.asarray(hot, BF16)
    kern = functools.partial(_nsa_dec_kernel, n_pg=n_pg, qb=qb, n_sel=n_sel, n_selp=n_selp,
                             n_cmp=n_cmp, q0=q0, w_off=q0 - wl)
    in_specs = [pl.BlockSpec((1, PAGE, 2 * nk), functools.partial(_dec_page_index, i=i, n_pg=n_pg))
                for i in range(n_pg)]
    in_specs += [
        pl.BlockSpec((1, N_KV, LANES, LANES), lambda b, g, pt: (b, 0, 0, 0)),
        pl.BlockSpec((1, nk, LANES), lambda b, g, pt: (b, 0, 0)),
        pl.BlockSpec((1, 3, LANES), lambda b, g, pt: (b, 0, 0)),
        pl.BlockSpec((1, N_KV, n_cmp, LANES), lambda b, g, pt: (b, 0, 0, 0)),
        pl.BlockSpec((1, qb, 6 * nk), lambda b, g, pt: (b, 0, 0)),
        pl.BlockSpec((1, wl, 2 * nk), lambda b, g, pt: (b, 0, 0)),
        pl.BlockSpec((n_selp, n_cmp), lambda b, g, pt: (0, 0)),
        pl.BlockSpec((LANES, LANES), lambda b, g, pt: (0, 0)),
        pl.BlockSpec((l_past, n_selp), lambda b, g, pt: (0, 0), pipeline_mode=pl.Buffered(1)),
    ]
    grid_spec = pltpu.PrefetchScalarGridSpec(
        num_scalar_prefetch=1,
        grid=(B, n_pages // n_pg),
        in_specs=in_specs,
        out_specs=pl.BlockSpec((1, nk, LANES), lambda b, g, pt: (b, 0, 0)),
        scratch_shapes=[
            pltpu.VMEM((nk + n_selp, LANES), BF16),
            pltpu.VMEM((nk, LANES), F32),
            pltpu.VMEM((1, LANES), F32),
            pltpu.VMEM((8, LANES), F32),
            pltpu.VMEM((nk, LANES), F32),
        ],
    )
    o_t = pl.pallas_call(
        kern,
        out_shape=jax.ShapeDtypeStruct((B, nk, LANES), F32),
        grid_spec=grid_spec,
        compiler_params=_cparams(("parallel", "arbitrary")),
        name="nsa_attend_dec",
    )(table, *([sel_pages] * n_pg), qc_pad, q_bd, gate_l, cmp_p, kvp, cache_win, wmap, fold, hot)
    o6 = o_t.reshape(B, N_KV, HEAD_DIM, N_KV, group, qb)
    o5 = jnp.einsum("bkdkgq->bqkgd", o6)
    return o5.reshape(B * qb, N_HEADS * HEAD_DIM)


def _rope_tables(pos, half):
    inv = jnp.exp(-math.log(ROPE_THETA) * jnp.arange(half, dtype=F32) / half)
    ang = pos.astype(F32)[:, None] * inv[None, :]
    return jnp.cos(ang), jnp.sin(ang)


def _prep_weights(ret_w_in, ret_w_out, ffn_w_in, ffn_w_out, kv_w, kv_knorm, cmp_w1, cmp_w2,
                  nsa_w_qg, nsa_qnorm, nsa_w_o):
    n_b = nsa_w_qg.shape[0]
    nq = N_HEADS * HEAD_DIM
    qg_pad = nq + LANES - nsa_w_qg.shape[2]
    w_qg = jnp.pad(nsa_w_qg, ((0, 0), (0, 0), (0, qg_pad))).astype(BF16)
    w_o = jnp.pad(nsa_w_o.reshape(n_b, N_HEADS, 1, HEAD_DIM, D_MODEL),
                  ((0, 0), (0, 0), (1, 0), (0, 0), (0, 0))).reshape(n_b, N_HEADS * LANES, D_MODEL)
    R = CMP_LEN // CMP_STRIDE
    w1 = cmp_w1.reshape(2, R, CMP_STRIDE, HEAD_DIM, CMP_HID).transpose(0, 2, 3, 1, 4)
    w1 = w1.reshape(2, CMP_STRIDE, HEAD_DIM, R * CMP_HID)
    w1dup = jnp.concatenate([w1, w1], axis=2).astype(BF16)
    w1dup = w1dup.reshape(2, CMP_STRIDE // 2, 2 * LANES, R * CMP_HID)
    z = jnp.zeros((CMP_HID, HEAD_DIM), F32)
    w2p = jnp.stack([jnp.concatenate([cmp_w2[0], z], axis=1),
                     jnp.concatenate([z, cmp_w2[1]], axis=1)]).astype(BF16)
    ones = jnp.ones((HEAD_DIM,), F32)
    return dict(
        ret_w_in=ret_w_in.astype(BF16), ret_w_out=ret_w_out.astype(BF16),
        ffn_w_in=ffn_w_in.astype(BF16), ffn_w_out=ffn_w_out.astype(BF16),
        kv_w=kv_w.astype(BF16), w_qg=w_qg, w_o=w_o.astype(BF16), w_o_raw=nsa_w_o.astype(BF16),
        kv_gain=jnp.tile(kv_knorm, (1, 2)),
        cmp_gain=jnp.concatenate([kv_knorm[0], ones]).reshape(1, LANES),
        q_gain=jnp.tile(nsa_qnorm, (1, 2)) * (HEAD_DIM ** -0.5),
        q_gain2=jnp.tile(nsa_qnorm, (1, 2)) * (HEAD_DIM ** -0.5 * math.log2(math.e)),
        score_bound=(1.05 * HEAD_DIM ** 0.5 * math.log2(math.e))
        * jnp.max(jnp.abs(nsa_qnorm), axis=1)[:, None] * jnp.max(jnp.abs(kv_knorm), axis=1)[None, :],
        w1dup=w1dup, w2p=w2p,
        bd=jnp.asarray(np.kron(np.eye(2), np.ones((HEAD_DIM, HEAD_DIM))), BF16),
    )


def _trunk(x, past_len, ret_s0, conv0, ctx, W, P):
    B, T, D = x.shape
    M = B * T
    depth = P["norm_mix"].shape[0]
    n_a = P["ret_w_in"].shape[0]
    pos = past_len + jnp.arange(T)
    cos_r, sin_r = _rope_tables(pos, RET_DK // 2)
    c32, s32 = _rope_tables(pos, HEAD_DIM // 2)
    cos_n = jnp.tile(c32, (1, 4))
    sin_n = jnp.concatenate([-s32, s32, -s32, s32], axis=1)
    lg = jnp.log1p(-jnp.exp2(-5.0 - jnp.arange(RET_HEADS, dtype=F32)))
    L = RET_CHUNK if T % RET_CHUNK == 0 else T
    gl = jnp.exp(L * lg)
    tm = min(512, M)
    tf = 1024 if T % 1024 == 0 else 512
    tb = min(512, T)
    tq = min(512, T)
    qb = next((c for c in (2 * Q_BLOCK, Q_BLOCK) if T % c == 0), T)
    tt = min(256, T)

    x2 = x.reshape(M, D)
    ret_states, conv_states = [], []
    for layer in range(depth):
        if layer == n_a:
            kvp, selp, winp = kv_project(x2.reshape(B, T, D), P["kv_norm"], W["kv_w"], W["kv_gain"],
                                         cos_n, sin_n, W["bd"], tq)
            nk2 = 2 * N_KV * HEAD_DIM
            if ctx is None:
                table = jnp.arange(M // PAGE, dtype=jnp.int32).reshape(B, T // PAGE)
                cmp_p = compress(kvp.reshape(M // PAGE, PAGE, 3 * nk2), table,
                                 W["w1dup"], W["pb"], W["w2p"], W["cmp_gain"])
                sel_p, win_p = selp, winp
                w_off = 0
            else:
                cache_cmp, cache_sel, cache_win, table = ctx
                n_pool = cache_cmp.shape[0]
                cmp_p = compress(cache_cmp.reshape(n_pool, PAGE, nk2), table,
                                 W["w1dup"], W["pb"], W["w2p"], W["cmp_gain"])
                sel_pages = lax.optimization_barrier(cache_sel.reshape(n_pool, PAGE, nk2))
                wl = cache_win.shape[1]
                win_rows = cache_win.reshape(B, wl, nk2)
                w_off = past_len - wl
            n_sel = -(-(past_len + T) // SEL_BLOCK)
        h_norm = P["norm_mix"][layer]
        if layer < n_a:
            proj = norm_matmul(x2, h_norm, W["ret_w_in"][layer], min(1024, M), 1024)
            og, s_new = retention(proj.reshape(B, T, -1), ret_s0[layer], cos_r, sin_r, lg, gl, L, tb)
            ret_states.append(s_new)
            x2 = matmul_res(og.reshape(M, -1), W["ret_w_out"][layer], x2, tm)
        else:
            j = layer - n_a
            if qb % LANES == 0:
                qc, qr, gates = qg_project_t(x2.reshape(B, T, D), h_norm, W["w_qg"][j],
                                             W["q_gain2"][j:j + 1], cos_n, sin_n, W["bd"], tq, qb)
                shifts = W["score_bound"][j]
                args = (qc, qr, gates, cmp_p, sel_p, win_p)
                o = lax.cond(
                    jnp.max(shifts) <= 30.0,
                    lambda a: nsa_attend_fast(shifts, *a, T, qb, past_len, w_off, n_sel),
                    lambda a: nsa_attend_t(*a, T, qb, past_len, w_off, n_sel),
                    args)
                x2 = matmul_res(o.reshape(M, -1), W["w_o"][j], x2, tm)
            else:
                qc, qr, gates = qg_project(x2.reshape(B, T, D), h_norm, W["w_qg"][j],
                                           W["q_gain2"][j:j + 1], cos_n, sin_n, W["bd"], tq, qb)
                o = nsa_attend_dec(table, qc, qr, gates, cmp_p, sel_pages, kvp, win_rows,
                                   qb, past_len, n_sel)
                x2 = matmul_res(o, W["w_o_raw"][j], x2, tm)
        if T % tf == 0:
            act, tail = ffn_in(x2, P["norm_ffn"][layer], W["ffn_w_in"][layer], conv0[layer],
                               P["ffn_conv_w"][layer], P["ffn_conv_b"][layer], T, tf, 256)
            conv_states.append(tail)
        else:
            proj = norm_matmul(x2, P["norm_ffn"][layer], W["ffn_w_in"][layer], tm, 512)
            proj3 = proj.reshape(B, T, 2 * D_FF)
            act = ffn_mid(proj3, conv0[layer], P["ffn_conv_w"][layer], P["ffn_conv_b"][layer], tt)
            conv_states.append(proj3[:, T - 2:, :D_FF])
        x2 = matmul_res(act.reshape(M, D_FF), W["ffn_w_out"][layer], x2, tm)

    nk = N_KV * HEAD_DIM
    new_cmp = kvp[:, :, 0:2 * nk].reshape(B, T, 2, N_KV, HEAD_DIM)
    new_sel = kvp[:, :, 2 * nk:4 * nk].reshape(B, T, 2, N_KV, HEAD_DIM)
    new_win = kvp[:, :, 4 * nk:6 * nk].reshape(B, T, 2, N_KV, HEAD_DIM)
    return (x2.reshape(B, T, D), jnp.stack(ret_states), jnp.stack(conv_states),
            new_cmp, new_sel, new_win)


def kernel(x_prompt, x_sample, cache_cmp_kv, cache_sel_kv, cache_win_kv, state_ret, state_conv,
           page_table, norm_mix, norm_ffn, ret_w_in, ret_w_out, ffn_w_in, ffn_conv_w, ffn_conv_b,
           ffn_w_out, kv_norm, kv_w, kv_knorm, cmp_pos, cmp_w1, cmp_w2, nsa_w_qg, nsa_qnorm, nsa_w_o):
    W = _prep_weights(ret_w_in, ret_w_out, ffn_w_in, ffn_w_out, kv_w, kv_knorm, cmp_w1, cmp_w2,
                      nsa_w_qg, nsa_qnorm, nsa_w_o)
    W["pb"] = pos_bias(cmp_pos, cmp_w1)
    P = dict(norm_mix=norm_mix, norm_ffn=norm_ffn, ret_w_in=ret_w_in, ffn_conv_w=ffn_conv_w,
             ffn_conv_b=ffn_conv_b, kv_norm=kv_norm)
    depth = norm_mix.shape[0]
    n_a = ret_w_in.shape[0]
    B, T, _ = x_prompt.shape
    zero_ret = jnp.zeros((n_a, B, RET_HEADS, RET_DK, RET_DV), F32)
    zero_conv = jnp.zeros((depth, B, 2, D_FF), F32)
    y_p, ret_p, conv_p, cmp_p, sel_p, win_p = _trunk(x_prompt, 0, zero_ret, zero_conv, None, W, P)
    win_p = win_p[:, T - min(WINDOW, T):]

    db, ts, _ = x_sample.shape
    past_len = page_table.shape[1] * PAGE
    ctx = (cache_cmp_kv, cache_sel_kv, cache_win_kv, page_table)
    y_s, ret_s, conv_s, cmp_s, sel_s, win_new = _trunk(x_sample, past_len, state_ret, state_conv,
                                                        ctx, W, P)
    all_win = jnp.concatenate([cache_win_kv, win_new], axis=1)
    win_s = all_win[:, all_win.shape[1] - min(WINDOW, past_len + ts):]
    return (y_p, y_s, ret_p, ret_s, conv_p, conv_s, cmp_p, cmp_s, sel_p, sel_s, win_p, win_s)
```

```python
import functools
import math

import jax
import jax.numpy as jnp
import numpy as np
from jax import lax
from jax.experimental import pallas as pl
from jax.experimental.pallas import tpu as pltpu

F32 = jnp.float32
BF16 = jnp.bfloat16

D_MODEL = 1024
PAGE = 128
RET_HEADS = 4
RET_DK = 256
RET_DV = 512
RET_CHUNK = 128
N_HEADS = 16
N_KV = 4
HEAD_DIM = 64
CMP_LEN = 32
CMP_STRIDE = 16
CMP_HID = 128
SEL_BLOCK = 64
SEL_TOPK = 16
WINDOW = 512
Q_BLOCK = 128
D_FF = 2816
ROPE_THETA = 10000.0
EPS = 1e-6
NEG_INF = -1e30
TINY = 1e-30
SEL_FORCE = 1e6
SEL_NEG = -1e6

LANES = 128
KEY_TILE = 512
HALO = 16
VMEM_LIMIT = 48 * 1024 * 1024


def _cparams(sem):
    return pltpu.CompilerParams(dimension_semantics=sem, vmem_limit_bytes=VMEM_LIMIT)


def _nt_dot(a, b):
    return lax.dot_general(a, b, (((1,), (1,)), ((), ())), preferred_element_type=F32)


def _tn_dot(a, b):
    return lax.dot_general(a, b, (((0,), (0,)), ((), ())), preferred_element_type=F32)


def _gelu(x):
    return 0.5 * x * (1.0 + jnp.tanh(math.sqrt(2.0 / math.pi) * (x + 0.044715 * (x * x * x))))


def _rms_rows(x, g):
    r = lax.rsqrt(jnp.mean(x * x, axis=-1, keepdims=True) + EPS)
    return x * r * g


def _head_ms(x, bd):
    x2 = x * x
    hi = x2.astype(BF16)
    lo = (x2 - hi.astype(F32)).astype(BF16)
    s = jnp.dot(hi, bd, preferred_element_type=F32) + jnp.dot(lo, bd, preferred_element_type=F32)
    return s * (1.0 / HEAD_DIM)


def _rope64(x, cos, sin):
    lane = lax.broadcasted_iota(jnp.int32, x.shape, 1)
    sw = jnp.where((lane & 63) < 32, pltpu.roll(x, 96, 1), pltpu.roll(x, 32, 1))
    return x * cos + sw * sin


def _pack_pair(k2, v2, dtype):
    lane = lax.broadcasted_iota(jnp.int32, k2.shape, 1)
    lo = lane < HEAD_DIM
    even = jnp.where(lo, k2, pltpu.roll(v2, HEAD_DIM, 1)).astype(dtype)
    odd = jnp.where(lo, pltpu.roll(k2, HEAD_DIM, 1), v2).astype(dtype)
    return even, odd


def _norm_matmul_kernel(x_ref, g_ref, w_ref, o_ref, h_ref):
    @pl.when(pl.program_id(1) == 0)
    def _():
        h_ref[...] = _rms_rows(x_ref[...], g_ref[...]).astype(BF16)

    o_ref[...] = jnp.dot(h_ref[...], w_ref[...], preferred_element_type=F32).astype(o_ref.dtype)


def norm_matmul(x, g, w, tm, tn):
    M, D = x.shape
    N = w.shape[1]
    return pl.pallas_call(
        _norm_matmul_kernel,
        out_shape=jax.ShapeDtypeStruct((M, N), F32),
        grid=(M // tm, N // tn),
        in_specs=[pl.BlockSpec((tm, D), lambda i, j: (i, 0)),
                  pl.BlockSpec((1, D), lambda i, j: (0, 0)),
                  pl.BlockSpec((D, tn), lambda i, j: (0, j))],
        out_specs=pl.BlockSpec((tm, tn), lambda i, j: (i, j)),
        scratch_shapes=[pltpu.VMEM((tm, D), BF16)],
        compiler_params=_cparams(("parallel", "arbitrary")),
        name="norm_matmul",
    )(x, g.reshape(1, D), w)


def _matmul_res_kernel(a_ref, w_ref, r_ref, o_ref):
    o_ref[...] = r_ref[...] + jnp.dot(a_ref[...].astype(BF16), w_ref[...],
                                      preferred_element_type=F32)


def matmul_res(a, w, res, tm):
    M, K = a.shape
    N = w.shape[1]
    return pl.pallas_call(
        _matmul_res_kernel,
        out_shape=jax.ShapeDtypeStruct((M, N), F32),
        grid=(M // tm,),
        in_specs=[pl.BlockSpec((tm, K), lambda i: (i, 0)),
                  pl.BlockSpec((K, N), lambda i: (0, 0)),
                  pl.BlockSpec((tm, N), lambda i: (i, 0))],
        out_specs=pl.BlockSpec((tm, N), lambda i: (i, 0)),
        compiler_params=_cparams(("parallel",)),
        name="matmul_res",
    )(a, w, res)


def _retention_kernel(lg_ref, gl_ref, q_ref, k_ref, v_ref, g_ref, cos_ref, sin_ref, s0_ref,
                      o_ref, sout_ref, S_ref, *, L, n_chunk):
    h = pl.program_id(1)
    t = pl.program_id(2)
    lg = lg_ref[h]
    gl = gl_ref[h]

    @pl.when(t == 0)
    def _():
        S_ref[...] = s0_ref[0, 0]

    ii = lax.broadcasted_iota(jnp.int32, (L, L), 0)
    jj = lax.broadcasted_iota(jnp.int32, (L, L), 1)
    diff = (ii - jj).astype(F32)
    decay = jnp.where(diff >= 0, jnp.exp(jnp.maximum(diff, 0.0) * lg), 0.0)
    idx = lax.broadcasted_iota(jnp.int32, (L, 1), 0).astype(F32)
    q_dec = jnp.exp((idx + 1.0) * lg)
    k_dec = jnp.exp((L - 1.0 - idx) * lg)
    half = RET_DK // 2

    for c in range(n_chunk):
        rows = pl.ds(c * L, L)
        cos = cos_ref[rows, :]
        sin = sin_ref[rows, :]

        def rope(x):
            x1, x2 = x[:, :half], x[:, half:]
            return jnp.concatenate([x1 * cos - x2 * sin, x2 * cos + x1 * sin], axis=1)

        qr = rope(q_ref[0, rows, :])
        kr = rope(k_ref[0, rows, :]) * (RET_DK ** -0.5)
        qb = qr.astype(BF16)
        vb = v_ref[0, rows, :].astype(BF16)
        sc = _nt_dot(qb, kr.astype(BF16)) * decay
        S = S_ref[...]
        o = jnp.dot(sc.astype(BF16), vb, preferred_element_type=F32)
        o = o + jnp.dot(qb, S.astype(BF16), preferred_element_type=F32) * q_dec
        S_ref[...] = S * gl + _tn_dot((kr * k_dec).astype(BF16), vb)
        on = o * lax.rsqrt(jnp.mean(o * o, axis=-1, keepdims=True) + EPS)
        g = g_ref[0, rows, :]
        o_ref[0, rows, :] = (on * (g * jax.nn.sigmoid(g))).astype(o_ref.dtype)

    @pl.when(t == pl.num_programs(2) - 1)
    def _():
        sout_ref[0, 0] = S_ref[...]


def retention(proj, s0, cos, sin, lg, gl, L, tb):
    B, T, _ = proj.shape
    n_chunk = tb // L
    odt = BF16 if tb % 16 == 0 else F32
    kern = functools.partial(_retention_kernel, L=L, n_chunk=n_chunk)
    grid_spec = pltpu.PrefetchScalarGridSpec(
        num_scalar_prefetch=2,
        grid=(B, RET_HEADS, T // tb),
        in_specs=[
            pl.BlockSpec((1, tb, RET_DK), lambda b, h, t, *_: (b, t, h)),
            pl.BlockSpec((1, tb, RET_DK), lambda b, h, t, *_: (b, t, RET_HEADS + h)),
            pl.BlockSpec((1, tb, RET_DV), lambda b, h, t, *_: (b, t, RET_HEADS + h)),
            pl.BlockSpec((1, tb, RET_DV), lambda b, h, t, *_: (b, t, 2 * RET_HEADS + h)),
            pl.BlockSpec((tb, RET_DK // 2), lambda b, h, t, *_: (t, 0)),
            pl.BlockSpec((tb, RET_DK // 2), lambda b, h, t, *_: (t, 0)),
            pl.BlockSpec((1, 1, RET_DK, RET_DV), lambda b, h, t, *_: (b, h, 0, 0)),
        ],
        out_specs=[
            pl.BlockSpec((1, tb, RET_DV), lambda b, h, t, *_: (b, t, h)),
            pl.BlockSpec((1, 1, RET_DK, RET_DV), lambda b, h, t, *_: (b, h, 0, 0)),
        ],
        scratch_shapes=[pltpu.VMEM((RET_DK, RET_DV), F32)],
    )
    return pl.pallas_call(
        kern,
        out_shape=[jax.ShapeDtypeStruct((B, T, RET_HEADS * RET_DV), odt),
                   jax.ShapeDtypeStruct((B, RET_HEADS, RET_DK, RET_DV), F32)],
        grid_spec=grid_spec,
        compiler_params=_cparams(("parallel", "parallel", "arbitrary")),
        name="retention",
    )(lg, gl, proj, proj, proj, proj, cos, sin, s0)


def _ffn_mid_kernel(u_ref, gt_ref, halo_ref, cw_ref, cb_ref, o_ref):
    u = u_ref[0]
    hl = halo_ref[0, 0]
    row = lax.broadcasted_iota(jnp.int32, u.shape, 0)
    u1 = jnp.where(row == 0, hl[1:2], pltpu.roll(u, 1, 0))
    u2 = jnp.where(row == 0, hl[0:1], jnp.where(row == 1, hl[1:2], pltpu.roll(u, 2, 0)))
    c = cb_ref[...] + cw_ref[0:1] * u2
    c = c + cw_ref[1:2] * u1
    c = c + cw_ref[2:3] * u
    o_ref[0] = (_gelu(c) * gt_ref[0]).astype(o_ref.dtype)


def ffn_mid(proj, buf, conv_w, conv_b, tt):
    B, T, _ = proj.shape
    nt = T // tt
    if nt > 1:
        tails = proj[:, :, :D_FF].reshape(B, nt, tt, D_FF)[:, :-1, tt - 2:, :]
        halo = jnp.concatenate([buf[:, None], tails], axis=1)
    else:
        halo = buf[:, None]
    odt = BF16 if tt % 16 == 0 else F32
    return pl.pallas_call(
        _ffn_mid_kernel,
        out_shape=jax.ShapeDtypeStruct((B, T, D_FF), odt),
        grid=(B, nt),
        in_specs=[pl.BlockSpec((1, tt, D_FF), lambda b, t: (b, t, 0)),
                  pl.BlockSpec((1, tt, D_FF), lambda b, t: (b, t, 1)),
                  pl.BlockSpec((1, 1, 2, D_FF), lambda b, t: (b, t, 0, 0)),
                  pl.BlockSpec((3, D_FF), lambda b, t: (0, 0)),
                  pl.BlockSpec((1, D_FF), lambda b, t: (0, 0))],
        out_specs=pl.BlockSpec((1, tt, D_FF), lambda b, t: (b, t, 0)),
        compiler_params=_cparams(("parallel", "parallel")),
        name="ffn_mid",
    )(proj, proj, halo, conv_w, conv_b.reshape(1, D_FF))


def _ffn_in_kernel(x_ref, xh_ref, g_ref, w_ref, buf_ref, cw_ref, cb_ref,
                   act_ref, tail_ref, *, tiles_per_seq, tn):
    seq_start = (pl.program_id(0) % tiles_per_seq) == 0
    h = _rms_rows(x_ref[...], g_ref[...]).astype(BF16)
    hh = _rms_rows(xh_ref[...], g_ref[...]).astype(BF16)
    tm = h.shape[0]
    row = lax.broadcasted_iota(jnp.int32, (tm, tn), 0)
    for j in range(D_FF // tn):
        cols = slice(j * tn, (j + 1) * tn)
        wu = w_ref[:, cols]
        u = jnp.dot(h, wu, preferred_element_type=F32)
        gt = jnp.dot(h, w_ref[:, D_FF + j * tn:D_FF + (j + 1) * tn], preferred_element_type=F32)
        uh = jnp.dot(hh, wu, preferred_element_type=F32)
        hl = jnp.where(seq_start, buf_ref[0, :, cols], uh[HALO - 2:, :])
        u1 = jnp.where(row == 0, hl[1:2], pltpu.roll(u, 1, 0))
        u2 = jnp.where(row == 0, hl[0:1], jnp.where(row == 1, hl[1:2], pltpu.roll(u, 2, 0)))
        c = cb_ref[:, cols] + cw_ref[0:1, cols] * u2
        c = c + cw_ref[1:2, cols] * u1
        c = c + cw_ref[2:3, cols] * u
        act_ref[:, cols] = (_gelu(c) * gt).astype(act_ref.dtype)
        tail_ref[0, :, cols] = u[tm - 2:, :]


def ffn_in(x, g, w, buf, conv_w, conv_b, T, tm, tn):
    M, D = x.shape
    tiles_per_seq = T // tm
    kern = functools.partial(_ffn_in_kernel, tiles_per_seq=tiles_per_seq, tn=tn)
    act, tails = pl.pallas_call(
        kern,
        out_shape=[jax.ShapeDtypeStruct((M, D_FF), BF16),
                   jax.ShapeDtypeStruct((M // tm, 2, D_FF), F32)],
        grid=(M // tm,),
        in_specs=[pl.BlockSpec((tm, D), lambda i: (i, 0)),
                  pl.BlockSpec((HALO, D), lambda i: (jnp.maximum(i * (tm // HALO) - 1, 0), 0)),
                  pl.BlockSpec((1, D), lambda i: (0, 0)),
                  pl.BlockSpec((D, 2 * D_FF), lambda i: (0, 0), pipeline_mode=pl.Buffered(1)),
                  pl.BlockSpec((1, 2, D_FF), lambda i: (i // tiles_per_seq, 0, 0)),
                  pl.BlockSpec((3, D_FF), lambda i: (0, 0)),
                  pl.BlockSpec((1, D_FF), lambda i: (0, 0))],
        out_specs=[pl.BlockSpec((tm, D_FF), lambda i: (i, 0)),
                   pl.BlockSpec((1, 2, D_FF), lambda i: (i, 0, 0))],
        compiler_params=_cparams(("parallel",)),
        name="ffn_in",
    )(x, x, g.reshape(1, D), w, buf, conv_w, conv_b.reshape(1, D_FF))
    return act, tails[tiles_per_seq - 1::tiles_per_seq]


def _kv_kernel(x_ref, g_ref, w_ref, gain_ref, cos_ref, sin_ref, bd_ref,
               kv_ref, selp_ref, winp_ref):
    h = _rms_rows(x_ref[0], g_ref[...]).astype(BF16)
    y = jnp.dot(h, w_ref[...], preferred_element_type=F32)
    cos = cos_ref[...]
    sin = sin_ref[...]
    bd = bd_ref[...]
    nk = N_KV * HEAD_DIM
    kv_ref[0, :, 0:2 * nk] = y[:, 0:2 * nk]
    for br, pack_ref in enumerate((selp_ref, winp_ref)):
        base = 2 * nk * (br + 1)
        kv_ref[0, :, base + nk:base + 2 * nk] = y[:, base + nk:base + 2 * nk]
        for p in range(2):
            kx = y[:, base + LANES * p:base + LANES * (p + 1)]
            kn = kx * lax.rsqrt(_head_ms(kx, bd) + EPS) * gain_ref[br + 1:br + 2, :]
            kr = _rope64(kn, cos, sin)
            kv_ref[0, :, base + LANES * p:base + LANES * (p + 1)] = kr
            vx = y[:, base + nk + LANES * p:base + nk + LANES * (p + 1)]
            even, odd = _pack_pair(kr, vx, pack_ref.dtype)
            pack_ref[0, 2 * p] = even
            pack_ref[0, 2 * p + 1] = odd


def kv_project(x, g, w, gains, cos, sin, bd, tm):
    B, T, D = x.shape
    N = w.shape[1]
    pdt = BF16 if tm % 16 == 0 else F32
    return pl.pallas_call(
        _kv_kernel,
        out_shape=[jax.ShapeDtypeStruct((B, T, N), F32),
                   jax.ShapeDtypeStruct((B, N_KV, T, LANES), pdt),
                   jax.ShapeDtypeStruct((B, N_KV, T, LANES), pdt)],
        grid=(B, T // tm),
        in_specs=[pl.BlockSpec((1, tm, D), lambda b, t: (b, t, 0)),
                  pl.BlockSpec((1, D), lambda b, t: (0, 0)),
                  pl.BlockSpec((D, N), lambda b, t: (0, 0)),
                  pl.BlockSpec((3, LANES), lambda b, t: (0, 0)),
                  pl.BlockSpec((tm, LANES), lambda b, t: (t, 0)),
                  pl.BlockSpec((tm, LANES), lambda b, t: (t, 0)),
                  pl.BlockSpec((LANES, LANES), lambda b, t: (0, 0))],
        out_specs=[pl.BlockSpec((1, tm, N), lambda b, t: (b, t, 0)),
                   pl.BlockSpec((1, N_KV, tm, LANES), lambda b, t: (b, 0, t, 0)),
                   pl.BlockSpec((1, N_KV, tm, LANES), lambda b, t: (b, 0, t, 0))],
        compiler_params=_cparams(("parallel", "parallel")),
        name="kv_project",
    )(x, g.reshape(1, D), w, gains, cos, sin, bd)


def _qg_kernel(x_ref, g_ref, w_ref, gain_ref, cos_ref, sin_ref, bd_ref,
               qc_ref, qr_ref, gate_ref, *, qb, n_qb):
    h = _rms_rows(x_ref[0], g_ref[...]).astype(BF16)
    y = jnp.dot(h, w_ref[...], preferred_element_type=F32)
    cos = cos_ref[...]
    sin = sin_ref[...]
    bd = bd_ref[...]
    nq = N_HEADS * HEAD_DIM
    gate_ref[0] = jax.nn.sigmoid(y[:, nq:nq + LANES])
    lane = lax.broadcasted_iota(jnp.int32, (y.shape[0], LANES), 1)
    lo = lane < HEAD_DIM
    group = N_HEADS // N_KV
    for p in range(N_HEADS // 2):
        qx = y[:, LANES * p:LANES * (p + 1)]
        qn = qx * lax.rsqrt(_head_ms(qx, bd) + EPS) * gain_ref[...]
        qr = _rope64(qn, cos, sin)
        for src, dst in ((qn, qc_ref), (qr, qr_ref)):
            for par in range(2):
                hd = 2 * p + par
                kvh, gi = hd // group, hd % group
                v = src if par == 0 else pltpu.roll(src, HEAD_DIM, 1)
                v = jnp.where(lo, v, 0.0).astype(dst.dtype)
                for j in range(n_qb):
                    dst[0, kvh, j, gi * qb:(gi + 1) * qb, :] = v[j * qb:(j + 1) * qb, :]


def qg_project(x, g, w, gain, cos, sin, bd, tm, qb):
    B, T, D = x.shape
    N = w.shape[1]
    n_qb = tm // qb
    group = N_HEADS // N_KV
    qdt = BF16 if qb % 16 == 0 else F32
    kern = functools.partial(_qg_kernel, qb=qb, n_qb=n_qb)
    qshape = jax.ShapeDtypeStruct((B, N_KV, T // qb, group * qb, LANES), qdt)
    qspec = pl.BlockSpec((1, N_KV, n_qb, group * qb, LANES), lambda b, t: (b, 0, t, 0, 0))
    return pl.pallas_call(
        kern,
        out_shape=[qshape, qshape, jax.ShapeDtypeStruct((B, T, LANES), F32)],
        grid=(B, T // tm),
        in_specs=[pl.BlockSpec((1, tm, D), lambda b, t: (b, t, 0)),
                  pl.BlockSpec((1, D), lambda b, t: (0, 0)),
                  pl.BlockSpec((D, N), lambda b, t: (0, 0)),
                  pl.BlockSpec((1, LANES), lambda b, t: (0, 0)),
                  pl.BlockSpec((tm, LANES), lambda b, t: (t, 0)),
                  pl.BlockSpec((tm, LANES), lambda b, t: (t, 0)),
                  pl.BlockSpec((LANES, LANES), lambda b, t: (0, 0))],
        out_specs=[qspec, qspec, pl.BlockSpec((1, tm, LANES), lambda b, t: (b, t, 0))],
        compiler_params=_cparams(("parallel", "parallel")),
        name="qg_project",
    )(x, g.reshape(1, D), w, gain, cos, sin, bd)


def _pos_bias_kernel(p_ref, w_ref, o_ref):
    o_ref[0] = jnp.dot(p_ref[0].astype(BF16), w_ref[0].astype(BF16), preferred_element_type=F32)


def pos_bias(cmp_pos, cmp_w1):
    K = CMP_LEN * HEAD_DIM
    p = jnp.broadcast_to(cmp_pos.reshape(2, 1, K), (2, 8, K))
    out = pl.pallas_call(
        _pos_bias_kernel,
        out_shape=jax.ShapeDtypeStruct((2, 8, CMP_HID), F32),
        grid=(2,),
        in_specs=[pl.BlockSpec((1, 8, K), lambda c: (c, 0, 0)),
                  pl.BlockSpec((1, K, CMP_HID), lambda c: (c, 0, 0))],
        out_specs=pl.BlockSpec((1, 8, CMP_HID), lambda c: (c, 0, 0)),
        name="pos_bias",
    )(p, cmp_w1)
    return out[:, 0, :]


def _compress_kernel(pt_ref, *refs, n_pg):
    pages = refs[:n_pg]
    w1_ref, pb_ref, w2_ref, gain_ref, out_ref, carry_ref, slab_ref = refs[n_pg:]
    g = pl.program_id(1)

    @pl.when(g == 0)
    def _():
        carry_ref[...] = jnp.zeros_like(carry_ref)

    n = n_pg * (PAGE // CMP_STRIDE)
    lo = lax.broadcasted_iota(jnp.int32, (n, LANES), 1) < HEAD_DIM
    row = lax.broadcasted_iota(jnp.int32, (N_KV * n, LANES), 0)
    res = jnp.zeros((N_KV * n, LANES), F32)
    for c in range(2):
        for pair in range(N_KV // 2):
            col = c * N_KV * HEAD_DIM + pair * LANES
            for i in range(n_pg):
                slab_ref[pair, PAGE * i:PAGE * (i + 1), :] = pages[i][0, :, col:col + LANES]
        acc = jnp.zeros((N_KV * n, 2 * CMP_HID), F32)
        for s in range(0, CMP_STRIDE, 2):
            parts = []
            for k in range(N_KV):
                pair, par = divmod(k, 2)
                keep = lo if par == 0 else jnp.logical_not(lo)
                parts.append(jnp.concatenate(
                    [jnp.where(keep, slab_ref[pair, pl.ds(s + i, n, stride=CMP_STRIDE), :], 0.0)
                     for i in range(2)], axis=1))
            xm = jnp.concatenate(parts, axis=0).astype(BF16)
            acc = acc + jnp.dot(xm, w1_ref[c, s // 2], preferred_element_type=F32)
        p0 = acc[:, :CMP_HID]
        p1 = acc[:, CMP_HID:]
        prev = pltpu.roll(p0, 1, 0)
        for k in range(N_KV):
            prev = jnp.where(row == k * n, carry_ref[2 * k + c, 7:8, :], prev)
        for k in range(N_KV):
            carry_ref[2 * k + c] = p0[(k + 1) * n - 8:(k + 1) * n, :]
        hid = _gelu(prev + p1 + pb_ref[c:c + 1, :]).astype(BF16)
        res = res + jnp.dot(hid, w2_ref[c], preferred_element_type=F32)
    for k in range(N_KV):
        r = res[k * n:(k + 1) * n, :]
        ms = jnp.sum(jnp.where(lo, r * r, 0.0), axis=-1, keepdims=True) * (1.0 / HEAD_DIM)
        kn = r * lax.rsqrt(ms + EPS) * gain_ref[...]
        out_ref[0, k] = jnp.where(lo, kn, r).astype(out_ref.dtype)


def _page_index(b, g, pt_ref, *, i, n_pg):
    return (pt_ref[b, g * n_pg + i], 0, 0)


def compress(pages_arr, table, w1dup, pb, w2p, gain):
    B, n_pages = table.shape
    n_pg = 16 if n_pages % 16 == 0 else n_pages
    n = n_pg * (PAGE // CMP_STRIDE)
    n_sub = n_pages * (PAGE // CMP_STRIDE)
    width = 2 * N_KV * HEAD_DIM
    in_specs = [pl.BlockSpec((1, PAGE, width), functools.partial(_page_index, i=i, n_pg=n_pg))
                for i in range(n_pg)]
    in_specs += [pl.BlockSpec(w1dup.shape, lambda b, g, pt: (0, 0, 0, 0)),
                 pl.BlockSpec(pb.shape, lambda b, g, pt: (0, 0)),
                 pl.BlockSpec(w2p.shape, lambda b, g, pt: (0, 0, 0)),
                 pl.BlockSpec((1, LANES), lambda b, g, pt: (0, 0))]
    grid_spec = pltpu.PrefetchScalarGridSpec(
        num_scalar_prefetch=1,
        grid=(B, n_pages // n_pg),
        in_specs=in_specs,
        out_specs=pl.BlockSpec((1, N_KV, n, LANES), lambda b, g, pt: (b, 0, g, 0)),
        scratch_shapes=[pltpu.VMEM((2 * N_KV, 8, CMP_HID), F32),
                        pltpu.VMEM((N_KV // 2, n_pg * PAGE, LANES), F32)],
    )
    return pl.pallas_call(
        functools.partial(_compress_kernel, n_pg=n_pg),
        out_shape=jax.ShapeDtypeStruct((B, N_KV, n_sub, LANES), BF16),
        grid_spec=grid_spec,
        compiler_params=_cparams(("parallel", "arbitrary")),
        name="compress",
    )(table, *([pages_arr] * n_pg), w1dup, pb, w2p, gain)


def _repack_kernel(pt_ref, p0, p1, p2, p3, new_ref, out_ref, *, n_full, t_new):
    t = pl.program_id(1)
    nk = N_KV * HEAD_DIM

    @pl.when(t < n_full)
    def _():
        for i, pg in enumerate((p0, p1, p2, p3)):
            x = pg[0]
            for p in range(2):
                even, odd = _pack_pair(x[:, LANES * p:LANES * (p + 1)],
                                       x[:, nk + LANES * p:nk + LANES * (p + 1)], out_ref.dtype)
                out_ref[0, 2 * p, PAGE * i:PAGE * (i + 1), :] = even
                out_ref[0, 2 * p + 1, PAGE * i:PAGE * (i + 1), :] = odd

    @pl.when(t == n_full)
    def _():
        pad = jnp.zeros((N_KV, KEY_TILE - t_new, LANES), F32)
        out_ref[0] = jnp.concatenate([new_ref[0].astype(F32), pad], axis=1).astype(out_ref.dtype)


def _repack_page_index(b, t, pt_ref, *, i, n_pages):
    return (pt_ref[b, jnp.minimum(4 * t + i, n_pages - 1)], 0, 0)


def repack(pages_arr, table, new_pack):
    B, n_pages = table.shape
    n_full = n_pages // 4
    t_new = new_pack.shape[2]
    width = 2 * N_KV * HEAD_DIM
    in_specs = [pl.BlockSpec((1, PAGE, width),
                             functools.partial(_repack_page_index, i=i, n_pages=n_pages))
                for i in range(4)]
    in_specs.append(pl.BlockSpec((1, N_KV, t_new, LANES), lambda b, t, pt: (b, 0, 0, 0)))
    grid_spec = pltpu.PrefetchScalarGridSpec(
        num_scalar_prefetch=1,
        grid=(B, n_full + 1),
        in_specs=in_specs,
        out_specs=pl.BlockSpec((1, N_KV, KEY_TILE, LANES), lambda b, t, pt: (b, 0, t, 0)),
    )
    return pl.pallas_call(
        functools.partial(_repack_kernel, n_full=n_full, t_new=t_new),
        out_shape=jax.ShapeDtypeStruct((B, N_KV, (n_full + 1) * KEY_TILE, LANES), BF16),
        grid_spec=grid_spec,
        compiler_params=_cparams(("parallel", "arbitrary")),
        name="repack",
    )(table, pages_arr, pages_arr, pages_arr, pages_arr, new_pack)


def _softmax_step(s, mask, kv, m_ref, l_ref, a_ref, k):
    s = jnp.where(mask, s, NEG_INF)
    m_old = m_ref[k]
    m_new = jnp.maximum(m_old, jnp.max(s, axis=-1, keepdims=True))
    alpha = jnp.exp(m_old - m_new)
    p = jnp.where(mask, jnp.exp(s - m_new), 0.0)
    l_ref[k] = alpha * l_ref[k] + jnp.sum(p, axis=-1, keepdims=True)
    a_ref[k] = alpha * a_ref[k] + jnp.dot(p.astype(BF16), kv, preferred_element_type=F32)
    m_ref[k] = m_new


def _nsa_kernel(qi_ref, kt_ref, wt_ref, wf_ref, last_ref,
                qc_ref, qr_ref, gate_ref, cmp_ref, sel_ref, win_ref, wmap_ref,
                o_ref,
                selm_ref, oc_ref, ms_ref, ls_ref, as_ref, mw_ref, lw_ref, aw_ref,
                *, qb, n_sel, n_selp, n_cmp, q0, w_off):
    step = pl.program_id(1)
    qi = qi_ref[step]
    kt = kt_ref[step]
    group = N_HEADS // N_KV
    R = group * qb
    rowq = lax.broadcasted_iota(jnp.int32, (R, 1), 0) & (qb - 1)
    qpos = q0 + qi * qb + rowq
    qpos_q = q0 + qi * qb + lax.broadcasted_iota(jnp.int32, (qb, 1), 0)

    @pl.when(kt == 0)
    def _first():
        for m_ref, l_ref, a_ref in ((ms_ref, ls_ref, as_ref), (mw_ref, lw_ref, aw_ref)):
            m_ref[...] = jnp.full(m_ref.shape, NEG_INF, F32)
            l_ref[...] = jnp.zeros(l_ref.shape, F32)
            a_ref[...] = jnp.zeros(a_ref.shape, F32)
        n_idx = lax.broadcasted_iota(jnp.int32, (1, n_cmp), 1)
        cvalid = (n_idx >= 1) & ((n_idx - 1) * CMP_STRIDE + CMP_LEN - 1 <= qpos)
        blk = lax.broadcasted_iota(jnp.int32, (qb, n_selp), 1)
        blk_f = blk.astype(F32)
        cur = qpos_q >> 6
        forced = (blk == 0) | (blk == cur) | (blk == cur - 1)
        reach = blk * SEL_BLOCK <= qpos_q
        real = blk < n_sel
        for k in range(N_KV):
            ckv = cmp_ref[0, k]
            s = _nt_dot(qc_ref[0, k, 0].astype(BF16), ckv)
            s = jnp.where(cvalid, s, NEG_INF)
            m = jnp.max(s, axis=-1, keepdims=True)
            e = jnp.where(cvalid, jnp.exp(s - m), 0.0)
            p = e / jnp.maximum(jnp.sum(e, axis=-1, keepdims=True), TINY)
            oc_ref[k] = jnp.dot(p.astype(BF16), ckv, preferred_element_type=F32)
            psum = p[0:qb]
            for gi in range(1, group):
                psum = psum + p[gi * qb:(gi + 1) * qb]
            hi = psum.astype(BF16)
            lo = (psum - hi.astype(F32)).astype(BF16)
            imp = _nt_dot(hi, wmap_ref[...]) + _nt_dot(lo, wmap_ref[...])
            v = jnp.where(forced, SEL_FORCE, jnp.where(reach, imp, SEL_NEG))
            v = jnp.where(real, v, -jnp.inf)

            def pick_one(_, carry):
                v, sel = carry
                m = jnp.max(v, axis=-1, keepdims=True)
                first = jnp.min(jnp.where(v == m, blk_f, float(n_selp)), axis=-1, keepdims=True)
                pick = blk_f == first
                sel = jnp.where(pick & (m > 0.5 * SEL_NEG), 1.0, sel)
                return jnp.where(pick, -jnp.inf, v), sel

            _, sel = lax.fori_loop(0, SEL_TOPK, pick_one, (v, jnp.zeros((qb, n_selp), F32)))
            selm_ref[k] = sel

    tok = kt * KEY_TILE + lax.broadcasted_iota(jnp.int32, (1, KEY_TILE), 1)
    blk_of_tok = tok >> 6
    expand = jnp.where(
        lax.broadcasted_iota(jnp.int32, (n_selp, KEY_TILE), 0) == blk_of_tok, 1.0, 0.0).astype(BF16)
    causal = tok <= qpos
    for k in range(N_KV):
        kv = sel_ref[0, k]
        s = _nt_dot(qr_ref[0, k, 0].astype(BF16), kv)
        sm = jnp.dot(selm_ref[k].astype(BF16), expand, preferred_element_type=F32)
        sm = jnp.concatenate([sm] * group, axis=0)
        _softmax_step(s, causal & (sm > 0.5), kv, ms_ref, ls_ref, as_ref, k)

    @pl.when(wf_ref[step] == 1)
    def _window():
        wpos = w_off + wt_ref[step] * KEY_TILE + lax.broadcasted_iota(jnp.int32, (1, KEY_TILE), 1)
        mask = (wpos <= qpos) & (wpos > qpos - WINDOW) & (wpos >= 0)
        for k in range(N_KV):
            kv = win_ref[0, k]
            s = _nt_dot(qr_ref[0, k, 0].astype(BF16), kv)
            _softmax_step(s, mask, kv, mw_ref, lw_ref, aw_ref, k)

    @pl.when(last_ref[step] == 1)
    def _finish():
        gate = gate_ref[0]
        for k in range(N_KV):
            for gi in range(group):
                hd = k * group + gi
                rs = slice(gi * qb, (gi + 1) * qb)
                o_s = as_ref[k, rs, :] / jnp.maximum(ls_ref[k, rs, :], TINY)
                o_w = aw_ref[k, rs, :] / jnp.maximum(lw_ref[k, rs, :], TINY)
                o = gate[:, 3 * hd:3 * hd + 1] * oc_ref[k, rs, :]
                o = o + gate[:, 3 * hd + 1:3 * hd + 2] * o_s
                o = o + gate[:, 3 * hd + 2:3 * hd + 3] * o_w
                o_ref[0, :, LANES * hd:LANES * (hd + 1)] = o.astype(o_ref.dtype)


def _nsa_tables(T, qb, q0, w_off):
    rows = []
    for qi in range(T // qb):
        q_lo = q0 + qi * qb
        q_hi = q_lo + qb - 1
        last_kt = q_hi // KEY_TILE
        w_lo = max(q_lo - WINDOW + 1, w_off)
        wt0 = (w_lo - w_off) // KEY_TILE
        wt1 = (q_hi - w_off) // KEY_TILE
        n_w = wt1 - wt0 + 1
        assert n_w <= last_kt + 1
        for kt in range(last_kt + 1):
            rows.append((qi, kt, wt0 + min(kt, n_w - 1), int(kt < n_w), int(kt == last_kt)))
    tab = np.asarray(rows, np.int32).T
    return [jnp.asarray(tab[i]) for i in range(5)]


def _overlap_map(n_cmp_rows, n_selp):
    m = np.arange(n_cmp_rows)[None, :]
    s = np.arange(n_selp)[:, None]
    c0 = (m - 1) * CMP_STRIDE
    ov = np.minimum(c0 + CMP_LEN, s * SEL_BLOCK + SEL_BLOCK) - np.maximum(c0, s * SEL_BLOCK)
    w = np.maximum(ov, 0).astype(np.float32) / CMP_LEN
    w[:, 0] = 0.0
    return jnp.asarray(w, BF16)


def nsa_attend(qc, qr, gates, cmp_p, sel_p, win_p, T, qb, q0, w_off, n_sel):
    B = qc.shape[0]
    group = N_HEADS // N_KV
    R = group * qb
    n_cmp = cmp_p.shape[2]
    n_selp = -(-n_sel // LANES) * LANES
    tabs = _nsa_tables(T, qb, q0, w_off)
    n_steps = int(tabs[0].shape[0])
    wmap = _overlap_map(n_cmp, n_selp)
    odt = BF16 if qb % 16 == 0 else F32
    kern = functools.partial(_nsa_kernel, qb=qb, n_sel=n_sel, n_selp=n_selp, n_cmp=n_cmp,
                             q0=q0, w_off=w_off)
    qspec = pl.BlockSpec((1, N_KV, 1, R, LANES), lambda b, s, qi, kt, wt, wf, la: (b, 0, qi[s], 0, 0))
    grid_spec = pltpu.PrefetchScalarGridSpec(
        num_scalar_prefetch=5,
        grid=(B, n_steps),
        in_specs=[
            qspec, qspec,
            pl.BlockSpec((1, qb, LANES), lambda b, s, qi, kt, wt, wf, la: (b, qi[s], 0)),
            pl.BlockSpec((1, N_KV, n_cmp, LANES), lambda b, s, qi, kt, wt, wf, la: (b, 0, 0, 0)),
            pl.BlockSpec((1, N_KV, KEY_TILE, LANES), lambda b, s, qi, kt, wt, wf, la: (b, 0, kt[s], 0)),
            pl.BlockSpec((1, N_KV, KEY_TILE, LANES), lambda b, s, qi, kt, wt, wf, la: (b, 0, wt[s], 0)),
            pl.BlockSpec((n_selp, n_cmp), lambda b, s, qi, kt, wt, wf, la: (0, 0)),
        ],
        out_specs=pl.BlockSpec((1, qb, N_HEADS * LANES), lambda b, s, qi, kt, wt, wf, la: (b, qi[s], 0)),
        scratch_shapes=[
            pltpu.VMEM((N_KV, qb, n_selp), F32),
            pltpu.VMEM((N_KV, R, LANES), F32),
            pltpu.VMEM((N_KV, R, 1), F32), pltpu.VMEM((N_KV, R, 1), F32), pltpu.VMEM((N_KV, R, LANES), F32),
            pltpu.VMEM((N_KV, R, 1), F32), pltpu.VMEM((N_KV, R, 1), F32), pltpu.VMEM((N_KV, R, LANES), F32),
        ],
    )
    return pl.pallas_call(
        kern,
        out_shape=jax.ShapeDtypeStruct((B, T, N_HEADS * LANES), odt),
        grid_spec=grid_spec,
        compiler_params=_cparams(("parallel", "arbitrary")),
        name="nsa_attend",
    )(*tabs, qc, qr, gates, cmp_p, sel_p, win_p, wmap)


def _qgt_kernel(x_ref, g_ref, w_ref, gain_ref, cos_ref, sin_ref, bd_ref,
                qc_ref, qr_ref, gate_ref, *, qb, n_qb):
    h = _rms_rows(x_ref[0], g_ref[...]).astype(BF16)
    y = jnp.dot(h, w_ref[...], preferred_element_type=F32)
    cos = cos_ref[...]
    sin = sin_ref[...]
    bd = bd_ref[...]
    nq = N_HEADS * HEAD_DIM
    group = N_HEADS // N_KV
    gate_t = jax.nn.sigmoid(y[:, nq:nq + LANES]).T
    for j in range(n_qb):
        gate_ref[0, j] = gate_t[:, j * qb:(j + 1) * qb]
    pad = jnp.zeros((N_KV, n_qb, HEAD_DIM, group * qb), qc_ref.dtype)
    qc_ref[0, :, :, HEAD_DIM:, :] = pad
    qr_ref[0, :, :, HEAD_DIM:, :] = pad
    for p in range(N_HEADS // 2):
        qx = y[:, LANES * p:LANES * (p + 1)]
        qn = qx * lax.rsqrt(_head_ms(qx, bd) + EPS) * gain_ref[...]
        qr = _rope64(qn, cos, sin)
        for src, dst in ((qn, qc_ref), (qr, qr_ref)):
            st = src.T.astype(dst.dtype)
            for par in range(2):
                kvh, gi = divmod(2 * p + par, group)
                for j in range(n_qb):
                    dst[0, kvh, j, 0:HEAD_DIM, gi * qb:(gi + 1) * qb] = (
                        st[par * HEAD_DIM:(par + 1) * HEAD_DIM, j * qb:(j + 1) * qb])


def qg_project_t(x, g, w, gain, cos, sin, bd, tm, qb):
    B, T, D = x.shape
    N = w.shape[1]
    n_qb = tm // qb
    group = N_HEADS // N_KV
    kern = functools.partial(_qgt_kernel, qb=qb, n_qb=n_qb)
    qshape = jax.ShapeDtypeStruct((B, N_KV, T // qb, LANES, group * qb), BF16)
    qspec = pl.BlockSpec((1, N_KV, n_qb, LANES, group * qb), lambda b, t: (b, 0, t, 0, 0))
    return pl.pallas_call(
        kern,
        out_shape=[qshape, qshape, jax.ShapeDtypeStruct((B, T // qb, LANES, qb), F32)],
        grid=(B, T // tm),
        in_specs=[pl.BlockSpec((1, tm, D), lambda b, t: (b, t, 0)),
                  pl.BlockSpec((1, D), lambda b, t: (0, 0)),
                  pl.BlockSpec((D, N), lambda b, t: (0, 0)),
                  pl.BlockSpec((1, LANES), lambda b, t: (0, 0)),
                  pl.BlockSpec((tm, LANES), lambda b, t: (t, 0)),
                  pl.BlockSpec((tm, LANES), lambda b, t: (t, 0)),
                  pl.BlockSpec((LANES, LANES), lambda b, t: (0, 0))],
        out_specs=[qspec, qspec, pl.BlockSpec((1, n_qb, LANES, qb), lambda b, t: (b, t, 0, 0))],
        compiler_params=_cparams(("parallel", "parallel")),
        name="qg_project_t",
    )(x, g.reshape(1, D), w, gain, cos, sin, bd)


def _nsa_t_kernel(qc_ref, qr_ref, gate_ref, cmp_ref, sel_ref, win_ref, wmap_ref, o_ref,
                  selneg_ref, m_ref, l_ref, acc_ref,
                  *, qb, n_sel, n_selp, n_cmp, q0, w_off, w_rows, l_win):
    qi = pl.program_id(1)
    group = N_HEADS // N_KV
    R = group * qb
    blocks_per_tile = KEY_TILE // SEL_BLOCK
    q_lo = q0 + qi * qb
    qpos_q = q_lo + lax.broadcasted_iota(jnp.int32, (1, qb), 1)
    n_kt = (q_lo + qb - 1) // KEY_TILE + 1
    w_start = pl.multiple_of(jnp.clip(q_lo - WINDOW - w_off, 0, l_win - w_rows), LANES)
    gate = gate_ref[0, 0]

    def lanes4(a):
        return jnp.concatenate([a] * group, axis=1)

    m_idx = lax.broadcasted_iota(jnp.int32, (n_cmp, qb), 0)
    cvalid = (m_idx >= 1) & ((m_idx - 1) * CMP_STRIDE + CMP_LEN - 1 <= qpos_q)
    cbias = lanes4(jnp.where(cvalid, 0.0, NEG_INF))
    any_c = lanes4(qpos_q >= CMP_LEN - 1)
    blk = lax.broadcasted_iota(jnp.int32, (n_selp, qb), 0)
    blk_f = blk.astype(F32)
    cur = qpos_q >> 6
    forced = (blk == 0) | (blk == cur) | (blk == cur - 1)
    reach = blk * SEL_BLOCK <= qpos_q
    real = blk < n_sel
    wpos = w_off + w_start + lax.broadcasted_iota(jnp.int32, (w_rows, qb), 0)
    wbias = lanes4(jnp.where((wpos <= qpos_q) & (wpos > qpos_q - WINDOW) & (wpos >= 0), 0.0, NEG_INF))
    row_t = lax.broadcasted_iota(jnp.int32, (KEY_TILE, qb), 0)

    for k in range(N_KV):
        ckv = cmp_ref[0, k]
        s = jnp.dot(ckv, qc_ref[0, k, 0], preferred_element_type=F32) + cbias
        e = jnp.exp2(s - jnp.max(s, axis=0, keepdims=True))
        den = jnp.maximum(jnp.sum(e, axis=0, keepdims=True), TINY)
        p = e * jnp.where(any_c, 1.0 / den, 0.0)
        oc = _tn_dot(ckv, p.astype(BF16))
        psum = p[:, 0:qb]
        for gi in range(1, group):
            psum = psum + p[:, gi * qb:(gi + 1) * qb]
        hi = psum.astype(BF16)
        lo = (psum - hi.astype(F32)).astype(BF16)
        imp = (jnp.dot(wmap_ref[...], hi, preferred_element_type=F32)
               + jnp.dot(wmap_ref[...], lo, preferred_element_type=F32))
        v = jnp.where(forced, SEL_FORCE, jnp.where(reach, imp, SEL_NEG))
        v = jnp.where(real, v, -jnp.inf)

        def pick_one(_, carry):
            v, sel = carry
            m = jnp.max(v, axis=0, keepdims=True)
            first = jnp.min(jnp.where(v == m, blk_f, float(n_selp)), axis=0, keepdims=True)
            pick = blk_f == first
            sel = jnp.where(pick & (m > 0.5 * SEL_NEG), 0.0, sel)
            return jnp.where(pick, -jnp.inf, v), sel

        _, sel = lax.fori_loop(0, SEL_TOPK, pick_one, (v, jnp.full((n_selp, qb), NEG_INF, F32)))
        selneg_ref[...] = sel

        qr = qr_ref[0, k, 0]
        m_ref[...] = jnp.full(m_ref.shape, NEG_INF, F32)
        l_ref[...] = jnp.zeros(l_ref.shape, F32)
        acc_ref[...] = jnp.zeros(acc_ref.shape, F32)

        def tile(kt, carry):
            start = pl.multiple_of(kt * KEY_TILE, KEY_TILE)
            kv = sel_ref[0, k, pl.ds(start, KEY_TILE), :]
            s = jnp.dot(kv, qr, preferred_element_type=F32)
            pieces = [jnp.broadcast_to(selneg_ref[pl.ds(kt * blocks_per_tile + j, 1), :], (SEL_BLOCK, qb))
                      for j in range(blocks_per_tile)]
            bias = jnp.concatenate(pieces, axis=0) + jnp.where(start + row_t <= qpos_q, 0.0, NEG_INF)
            s = s + lanes4(bias)
            m_old = m_ref[...]
            m_new = jnp.maximum(m_old, jnp.max(s, axis=0, keepdims=True))
            alpha = jnp.exp2(m_old - m_new)
            p = jnp.exp2(s - m_new)
            l_ref[...] = alpha * l_ref[...] + jnp.sum(p, axis=0, keepdims=True)
            acc_ref[...] = alpha * acc_ref[...] + _tn_dot(kv, p.astype(BF16))
            m_ref[...] = m_new
            return carry

        lax.fori_loop(0, n_kt, tile, 0)
        o_s = acc_ref[...] * (1.0 / jnp.maximum(l_ref[...], TINY))

        wkv = win_ref[0, k, pl.ds(w_start, w_rows), :]
        s = jnp.dot(wkv, qr, preferred_element_type=F32) + wbias
        e = jnp.exp2(s - jnp.max(s, axis=0, keepdims=True))
        den = jnp.maximum(jnp.sum(e, axis=0, keepdims=True), TINY)
        o_w = _tn_dot(wkv, e.astype(BF16)) * (1.0 / den)

        for gi in range(group):
            hd = k * group + gi
            sl = slice(gi * qb, (gi + 1) * qb)
            o = gate[3 * hd:3 * hd + 1, :] * oc[:, sl]
            o = o + gate[3 * hd + 1:3 * hd + 2, :] * o_s[:, sl]
            o = o + gate[3 * hd + 2:3 * hd + 3, :] * o_w[:, sl]
            o_ref[0, :, LANES * hd:LANES * (hd + 1)] = o.T.astype(o_ref.dtype)


def nsa_attend_t(qc, qr, gates, cmp_p, sel_p, win_p, T, qb, q0, w_off, n_sel):
    B = qc.shape[0]
    group = N_HEADS // N_KV
    R = group * qb
    n_cmp = cmp_p.shape[2]
    n_selp = -(-n_sel // LANES) * LANES
    l_sel = sel_p.shape[2]
    l_win = win_p.shape[2]
    w_rows = WINDOW + max(qb, LANES)
    assert l_win >= w_rows and l_sel >= ((q0 + T - 1) // KEY_TILE + 1) * KEY_TILE
    wmap = _overlap_map(n_cmp, n_selp)
    kern = functools.partial(_nsa_t_kernel, qb=qb, n_sel=n_sel, n_selp=n_selp, n_cmp=n_cmp,
                             q0=q0, w_off=w_off, w_rows=w_rows, l_win=l_win)
    qspec = pl.BlockSpec((1, N_KV, 1, LANES, R), lambda b, i: (b, 0, i, 0, 0))
    resident = dict(pipeline_mode=pl.Buffered(1))
    return pl.pallas_call(
        kern,
        out_shape=jax.ShapeDtypeStruct((B, T, N_HEADS * LANES), BF16),
        grid=(B, T // qb),
        in_specs=[
            qspec, qspec,
            pl.BlockSpec((1, 1, LANES, qb), lambda b, i: (b, i, 0, 0)),
            pl.BlockSpec((1, N_KV, n_cmp, LANES), lambda b, i: (b, 0, 0, 0)),
            pl.BlockSpec((1, N_KV, l_sel, LANES), lambda b, i: (b, 0, 0, 0), **resident),
            pl.BlockSpec((1, N_KV, l_win, LANES), lambda b, i: (b, 0, 0, 0), **resident),
            pl.BlockSpec((n_selp, n_cmp), lambda b, i: (0, 0)),
        ],
        out_specs=pl.BlockSpec((1, qb, N_HEADS * LANES), lambda b, i: (b, i, 0)),
        scratch_shapes=[
            pltpu.VMEM((n_selp, qb), F32),
            pltpu.VMEM((1, R), F32), pltpu.VMEM((1, R), F32), pltpu.VMEM((LANES, R), F32),
        ],
        compiler_params=_cparams(("parallel", "arbitrary")),
        name="nsa_attend_t",
    )(qc, qr, gates, cmp_p, sel_p, win_p, wmap)


def _nsa_fast_kernel(shift_ref, qc_ref, qr_ref, gate_ref, cmp_ref, sel_ref, win_ref, wmap_ref, hot_ref,
                     o_ref, qaug_ref, oc_ref, ow_ref, l_ref, acc_ref,
                     *, qb, n_sel, n_selp, n_cmp, q0, w_off, w_rows, l_win):
    qi = pl.program_id(1)
    group = N_HEADS // N_KV
    R = group * qb
    q_lo = q0 + qi * qb
    qpos_q = q_lo + lax.broadcasted_iota(jnp.int32, (1, qb), 1)
    n_kt = (q_lo + qb - 1) // KEY_TILE + 1
    w_start = pl.multiple_of(jnp.clip(q_lo - WINDOW - w_off, 0, l_win - w_rows), LANES)
    shift_c = shift_ref[0]
    shift_s = shift_ref[1]
    shift_w = shift_ref[2]

    def lanes4(a):
        return jnp.concatenate([a] * group, axis=1)

    m_idx = lax.broadcasted_iota(jnp.int32, (n_cmp, qb), 0)
    cvalid = (m_idx >= 1) & ((m_idx - 1) * CMP_STRIDE + CMP_LEN - 1 <= qpos_q)
    cbias = lanes4(jnp.where(cvalid, -shift_c, NEG_INF))
    any_c = lanes4(qpos_q >= CMP_LEN - 1)
    blk = lax.broadcasted_iota(jnp.int32, (n_selp, qb), 0)
    blk_f = blk.astype(F32)
    cur = qpos_q >> 6
    forced = (blk == 0) | (blk == cur) | (blk == cur - 1)
    reach = blk * SEL_BLOCK <= qpos_q
    real = blk < n_sel

    imps = []
    for k in range(N_KV):
        ckv = cmp_ref[0, k]
        e = jnp.exp2(jnp.dot(ckv, qc_ref[0, k, 0], preferred_element_type=F32) + cbias)
        den = jnp.maximum(jnp.sum(e, axis=0, keepdims=True), TINY)
        p = e * jnp.where(any_c, 1.0 / den, 0.0)
        oc_ref[k] = _tn_dot(ckv, p.astype(BF16))
        psum = p[:, 0:qb]
        for gi in range(1, group):
            psum = psum + p[:, gi * qb:(gi + 1) * qb]
        hi = psum.astype(BF16)
        lo = (psum - hi.astype(F32)).astype(BF16)
        imp = (jnp.dot(wmap_ref[...], hi, preferred_element_type=F32)
               + jnp.dot(wmap_ref[...], lo, preferred_element_type=F32))
        v = jnp.where(forced, SEL_FORCE, jnp.where(reach, imp, SEL_NEG))
        imps.append(jnp.where(real, v, -jnp.inf))

    def pick_one(_, vs):
        out = []
        for v in vs:
            m = jnp.max(v, axis=0, keepdims=True)
            first = jnp.min(jnp.where(v == m, blk_f, float(n_selp)), axis=0, keepdims=True)
            out.append(jnp.where(blk_f == first, -jnp.inf, v))
        return tuple(out)

    wpos = w_off + w_start + lax.broadcasted_iota(jnp.int32, (w_rows, qb), 0)
    wvalid = (wpos <= qpos_q) & (wpos > qpos_q - WINDOW) & (wpos >= 0)
    wbias = lanes4(jnp.where(wvalid, -shift_w, NEG_INF))
    for k in range(N_KV):
        wkv = win_ref[0, k, pl.ds(w_start, w_rows), :]
        e = jnp.exp2(jnp.dot(wkv, qr_ref[0, k, 0], preferred_element_type=F32) + wbias)
        den = jnp.maximum(jnp.sum(e, axis=0, keepdims=True), TINY)
        ow_ref[k] = _tn_dot(wkv, e.astype(BF16)) * (1.0 / den)

    marked = lax.fori_loop(0, SEL_TOPK, pick_one, tuple(imps), unroll=True)
    for k in range(N_KV):
        sel = jnp.where((marked[k] == -jnp.inf) & (imps[k] > 0.5 * SEL_NEG), -shift_s, NEG_INF)
        qaug_ref[k, 0:LANES, :] = qr_ref[0, k, 0]
        qaug_ref[k, LANES:, :] = lanes4(sel).astype(BF16)

    l_ref[...] = jnp.zeros(l_ref.shape, F32)
    acc_ref[...] = jnp.zeros(acc_ref.shape, F32)

    def tile(start, rows, causal_bias):
        hot = hot_ref[pl.ds(start, rows), :]
        for k in range(N_KV):
            kv = sel_ref[0, k, pl.ds(start, rows), :]
            s = jnp.dot(jnp.concatenate([kv, hot], axis=1), qaug_ref[k], preferred_element_type=F32)
            if causal_bias is not None:
                s = s + causal_bias
            p = jnp.exp2(s)
            l_ref[k] += jnp.sum(p.reshape(rows // 8, 8, R), axis=0)
            acc_ref[k] += _tn_dot(kv, p.astype(BF16))

    def double_tile(i, carry):
        tile(pl.multiple_of(i * 2 * KEY_TILE, 2 * KEY_TILE), 2 * KEY_TILE, None)
        return carry

    n_below = n_kt - 1
    lax.fori_loop(0, n_below // 2, double_tile, 0)

    @pl.when(n_below % 2 == 1)
    def _():
        tile(pl.multiple_of((n_below - 1) * KEY_TILE, KEY_TILE), KEY_TILE, None)

    d_start = pl.multiple_of(n_below * KEY_TILE, KEY_TILE)
    row_t = lax.broadcasted_iota(jnp.int32, (KEY_TILE, qb), 0)
    tile(d_start, KEY_TILE, lanes4(jnp.where(d_start + row_t <= qpos_q, 0.0, NEG_INF)))

    gate = gate_ref[0, 0]
    for k in range(N_KV):
        o_w = ow_ref[k]
        l_s = jnp.maximum(jnp.sum(l_ref[k], axis=0, keepdims=True), TINY)
        o_s = acc_ref[k] * (1.0 / l_s)
        oc = oc_ref[k]
        for gi in range(group):
            hd = k * group + gi
            sl = slice(gi * qb, (gi + 1) * qb)
            o = gate[3 * hd:3 * hd + 1, :] * oc[:, sl]
            o = o + gate[3 * hd + 1:3 * hd + 2, :] * o_s[:, sl]
            o = o + gate[3 * hd + 2:3 * hd + 3, :] * o_w[:, sl]
            o_ref[0, :, LANES * hd:LANES * (hd + 1)] = o.T.astype(o_ref.dtype)


def nsa_attend_fast(shifts, qc, qr, gates, cmp_p, sel_p, win_p, T, qb, q0, w_off, n_sel):
    B = qc.shape[0]
    group = N_HEADS // N_KV
    R = group * qb
    n_cmp = cmp_p.shape[2]
    n_selp = -(-n_sel // LANES) * LANES
    l_sel = sel_p.shape[2]
    l_win = win_p.shape[2]
    w_rows = WINDOW + max(qb, LANES)
    assert l_win >= w_rows and l_sel >= ((q0 + T - 1) // KEY_TILE + 1) * KEY_TILE
    assert n_sel >= SEL_TOPK and q0 % qb == 0 and KEY_TILE % qb == 0
    wmap = _overlap_map(n_cmp, n_selp)
    hot = np.zeros((l_sel, n_selp), np.float32)
    hot[np.arange(l_sel), np.arange(l_sel) // SEL_BLOCK] = 1.0
    hot = jnp.asarray(hot, BF16)
    kern = functools.partial(_nsa_fast_kernel, qb=qb, n_sel=n_sel, n_selp=n_selp, n_cmp=n_cmp,
                             q0=q0, w_off=w_off, w_rows=w_rows, l_win=l_win)
    qspec = pl.BlockSpec((1, N_KV, 1, LANES, R), lambda b, i, sh: (b, 0, i, 0, 0))
    resident = dict(pipeline_mode=pl.Buffered(1))
    grid_spec = pltpu.PrefetchScalarGridSpec(
        num_scalar_prefetch=1,
        grid=(B, T // qb),
        in_specs=[
            qspec, qspec,
            pl.BlockSpec((1, 1, LANES, qb), lambda b, i, sh: (b, i, 0, 0)),
            pl.BlockSpec((1, N_KV, n_cmp, LANES), lambda b, i, sh: (b, 0, 0, 0)),
            pl.BlockSpec((1, N_KV, l_sel, LANES), lambda b, i, sh: (b, 0, 0, 0), **resident),
            pl.BlockSpec((1, N_KV, l_win, LANES), lambda b, i, sh: (b, 0, 0, 0), **resident),
            pl.BlockSpec((n_selp, n_cmp), lambda b, i, sh: (0, 0)),
            pl.BlockSpec((l_sel, n_selp), lambda b, i, sh: (0, 0), **resident),
        ],
        out_specs=pl.BlockSpec((1, qb, N_HEADS * LANES), lambda b, i, sh: (b, i, 0)),
        scratch_shapes=[
            pltpu.VMEM((N_KV, 2 * LANES, R), BF16),
            pltpu.VMEM((N_KV, LANES, R), F32),
            pltpu.VMEM((N_KV, LANES, R), F32),
            pltpu.VMEM((N_KV, 8, R), F32),
            pltpu.VMEM((N_KV, LANES, R), F32),
        ],
    )
    return pl.pallas_call(
        kern,
        out_shape=jax.ShapeDtypeStruct((B, T, N_HEADS * LANES), BF16),
        grid_spec=grid_spec,
        compiler_params=_cparams(("parallel", "arbitrary")),
        name="nsa_attend_fast",
    )(shifts, qc, qr, gates, cmp_p, sel_p, win_p, wmap, hot)


def _nsa_dec_kernel(pt_ref, *refs, n_pg, qb, n_sel, n_selp, n_cmp, q0, w_off):
    pages = refs[:n_pg]
    (qcp_ref, qbd_ref, gate_ref, cmp_ref, new_ref, win_ref, wmap_ref, fold_ref, hot_ref,
     o_ref, qaug_ref, oc_ref, m_ref, l_ref, acc_ref) = refs[n_pg:]
    g = pl.program_id(1)
    nk = N_KV * HEAD_DIM
    t_new = new_ref.shape[1]
    lane = lax.broadcasted_iota(jnp.int32, (1, LANES), 1)
    qpos = q0 + (lane & (qb - 1))

    @pl.when(g == 0)
    def _first():
        s = jnp.dot(cmp_ref[0, 0], qcp_ref[0, 0], preferred_element_type=F32)
        for k in range(1, N_KV):
            s = s + jnp.dot(cmp_ref[0, k], qcp_ref[0, k], preferred_element_type=F32)
        m_idx = lax.broadcasted_iota(jnp.int32, (n_cmp, LANES), 0)
        cvalid = (m_idx >= 1) & ((m_idx - 1) * CMP_STRIDE + CMP_LEN - 1 <= qpos)
        s = s + jnp.where(cvalid, 0.0, NEG_INF)
        e = jnp.exp2(s - jnp.max(s, axis=0, keepdims=True))
        den = jnp.maximum(jnp.sum(e, axis=0, keepdims=True), TINY)
        p = e * jnp.where(qpos >= CMP_LEN - 1, 1.0 / den, 0.0)
        pb = p.astype(BF16)
        for k in range(N_KV):
            oc_ref[k * HEAD_DIM:(k + 1) * HEAD_DIM, :] = _tn_dot(cmp_ref[0, k], pb)[HEAD_DIM:, :]
        fold = fold_ref[...]
        p_lo = (p - pb.astype(F32)).astype(BF16)
        psum = jnp.dot(pb, fold, preferred_element_type=F32) + jnp.dot(p_lo, fold, preferred_element_type=F32)
        hi = psum.astype(BF16)
        lo = (psum - hi.astype(F32)).astype(BF16)
        imp = (jnp.dot(wmap_ref[...], hi, preferred_element_type=F32)
               + jnp.dot(wmap_ref[...], lo, preferred_element_type=F32))
        blk = lax.broadcasted_iota(jnp.int32, (n_selp, LANES), 0)
        blk_f = blk.astype(F32)
        cur = qpos >> 6
        forced = (blk == 0) | (blk == cur) | (blk == cur - 1)
        v = jnp.where(forced, SEL_FORCE, jnp.where(blk * SEL_BLOCK <= qpos, imp, SEL_NEG))
        v = jnp.where(blk < n_sel, v, -jnp.inf)

        def pick_one(_, carry):
            v, sel = carry
            m = jnp.max(v, axis=0, keepdims=True)
            first = jnp.min(jnp.where(v == m, blk_f, float(n_selp)), axis=0, keepdims=True)
            pick = blk_f == first
            sel = jnp.where(pick & (m > 0.5 * SEL_NEG), 0.0, sel)
            return jnp.where(pick, -jnp.inf, v), sel

        _, sel = lax.fori_loop(0, SEL_TOPK, pick_one, (v, jnp.full((n_selp, LANES), NEG_INF, F32)))
        qaug_ref[0:nk, :] = qbd_ref[0]
        qaug_ref[nk:, :] = sel.astype(BF16)
        m_ref[...] = jnp.full(m_ref.shape, NEG_INF, F32)
        l_ref[...] = jnp.zeros(l_ref.shape, F32)
        acc_ref[...] = jnp.zeros(acc_ref.shape, F32)

    def attend(kx, vx, hot, bias):
        s = jnp.dot(jnp.concatenate([kx, hot], axis=1), qaug_ref[...], preferred_element_type=F32)
        if bias is not None:
            s = s + bias
        m_old = m_ref[...]
        m_new = jnp.maximum(m_old, jnp.max(s, axis=0, keepdims=True))
        alpha = jnp.exp2(m_old - m_new)
        p = jnp.exp2(s - m_new)
        l_ref[...] = alpha * l_ref[...] + jnp.sum(p.reshape(p.shape[0] // 8, 8, LANES), axis=0)
        acc_ref[...] = alpha * acc_ref[...] + _tn_dot(vx, p.astype(BF16))
        m_ref[...] = m_new

    x = jnp.concatenate([pages[i][0] for i in range(n_pg)], axis=0).astype(BF16)
    start = pl.multiple_of(g * (n_pg * PAGE), n_pg * PAGE)
    attend(x[:, :nk], x[:, nk:], hot_ref[pl.ds(start, n_pg * PAGE), :], None)

    @pl.when(g == pl.num_programs(1) - 1)
    def _last():
        pad = jnp.zeros((HALO - t_new, 2 * nk), F32)
        row = lax.broadcasted_iota(jnp.int32, (HALO, LANES), 0)
        new_ok = (row < t_new) & (q0 + row <= qpos)
        xn = jnp.concatenate([new_ref[0, :, 2 * nk:4 * nk], pad], axis=0).astype(BF16)
        hot_new = jnp.where(
            lax.broadcasted_iota(jnp.int32, (HALO, n_selp), 1) == q0 // SEL_BLOCK, 1.0, 0.0).astype(BF16)
        attend(xn[:, :nk], xn[:, nk:], hot_new, jnp.where(new_ok, 0.0, NEG_INF))

        qbd = qbd_ref[0]
        xw = win_ref[0].astype(BF16)
        wrow = lax.broadcasted_iota(jnp.int32, (xw.shape[0], LANES), 0)
        wpos = w_off + wrow
        wvalid = (wpos <= qpos) & (wpos > qpos - WINDOW) & (wpos >= 0)
        s_c = jnp.dot(xw[:, :nk], qbd, preferred_element_type=F32) + jnp.where(wvalid, 0.0, NEG_INF)
        xwn = jnp.concatenate([new_ref[0, :, 4 * nk:6 * nk], pad], axis=0).astype(BF16)
        npos = q0 + row
        nvalid = (row < t_new) & (npos <= qpos) & (npos > qpos - WINDOW)
        s_n = jnp.dot(xwn[:, :nk], qbd, preferred_element_type=F32) + jnp.where(nvalid, 0.0, NEG_INF)
        m_w = jnp.maximum(jnp.max(s_c, axis=0, keepdims=True), jnp.max(s_n, axis=0, keepdims=True))
        e_c = jnp.exp2(s_c - m_w)
        e_n = jnp.exp2(s_n - m_w)
        den = jnp.sum(e_c, axis=0, keepdims=True) + jnp.sum(e_n, axis=0, keepdims=True)
        o_w = _tn_dot(xw[:, nk:], e_c.astype(BF16)) + _tn_dot(xwn[:, nk:], e_n.astype(BF16))
        o_w = o_w * (1.0 / jnp.maximum(den, TINY))
        l_s = jnp.maximum(jnp.sum(l_ref[...], axis=0, keepdims=True), TINY)
        o_s = acc_ref[...] * (1.0 / l_s)
        gate = gate_ref[0]
        o_ref[0] = gate[0:1, :] * oc_ref[...] + gate[1:2, :] * o_s + gate[2:3, :] * o_w


def _dec_page_index(b, g, pt, *, i, n_pg):
    return (pt[b, g * n_pg + i], 0, 0)


def nsa_attend_dec(table, qc, qr, gates, cmp_p, sel_pages, kvp, cache_win, qb, q0, n_sel):
    B, n_pages = table.shape
    group = N_HEADS // N_KV
    assert N_KV * group * qb == LANES
    nk = N_KV * HEAD_DIM
    n_pg = 16 if n_pages % 16 == 0 else n_pages
    n_cmp = cmp_p.shape[2]
    n_selp = -(-n_sel // LANES) * LANES
    wl = cache_win.shape[1]
    eye = jnp.eye(N_KV, dtype=F32)

    def spread(q):
        qt = q[:, :, 0, :, :HEAD_DIM].astype(F32).transpose(0, 1, 3, 2)
        return jnp.einsum("bkdr,kj->bkdjr", qt, eye).reshape(B, N_KV, HEAD_DIM, LANES)

    qc_pad = jnp.pad(spread(qc), ((0, 0), (0, 0), (0, LANES - HEAD_DIM), (0, 0))).astype(BF16)
    q_bd = spread(qr).reshape(B, nk, LANES).astype(BF16)
    gate_l = gates[:, :, :3 * N_HEADS].reshape(B, qb, N_KV, group, 3).transpose(0, 4, 2, 3, 1)
    gate_l = gate_l.reshape(B, 3, LANES)
    wmap = _overlap_map(n_cmp, n_selp)
    lane = np.arange(LANES)
    fold = jnp.asarray((lane[:, None] // (group * qb) == lane[None, :] // (group * qb))
                       & (lane[:, None] % qb == lane[None, :] % qb), BF16)
    l_past = n_pages * PAGE
    hot = np.zeros((l_past, n_selp), np.float32)
    hot[np.arange(l_past), np.arange(l_past) // SEL_BLOCK] = 1.0
    hot = jnp.asarray(hot, BF16)
    kern = functools.partial(_nsa_dec_kernel, n_pg=n_pg, qb=qb, n_sel=n_sel, n_selp=n_selp,
                             n_cmp=n_cmp, q0=q0, w_off=q0 - wl)
    in_specs = [pl.BlockSpec((1, PAGE, 2 * nk), functools.partial(_dec_page_index, i=i, n_pg=n_pg))
                for i in range(n_pg)]
    in_specs += [
        pl.BlockSpec((1, N_KV, LANES, LANES), lambda b, g, pt: (b, 0, 0, 0)),
        pl.BlockSpec((1, nk, LANES), lambda b, g, pt: (b, 0, 0)),
        pl.BlockSpec((1, 3, LANES), lambda b, g, pt: (b, 0, 0)),
        pl.BlockSpec((1, N_KV, n_cmp, LANES), lambda b, g, pt: (b, 0, 0, 0)),
        pl.BlockSpec((1, qb, 6 * nk), lambda b, g, pt: (b, 0, 0)),
        pl.BlockSpec((1, wl, 2 * nk), lambda b, g, pt: (b, 0, 0)),
        pl.BlockSpec((n_selp, n_cmp), lambda b, g, pt: (0, 0)),
        pl.BlockSpec((LANES, LANES), lambda b, g, pt: (0, 0)),
        pl.BlockSpec((l_past, n_selp), lambda b, g, pt: (0, 0), pipeline_mode=pl.Buffered(1)),
    ]
    grid_spec = pltpu.PrefetchScalarGridSpec(
        num_scalar_prefetch=1,
        grid=(B, n_pages // n_pg),
        in_specs=in_specs,
        out_specs=pl.BlockSpec((1, nk, LANES), lambda b, g, pt: (b, 0, 0)),
        scratch_shapes=[
            pltpu.VMEM((nk + n_selp, LANES), BF16),
            pltpu.VMEM((nk, LANES), F32),
            pltpu.VMEM((1, LANES), F32),
            pltpu.VMEM((8, LANES), F32),
            pltpu.VMEM((nk, LANES), F32),
        ],
    )
    o_t = pl.pallas_call(
        kern,
        out_shape=jax.ShapeDtypeStruct((B, nk, LANES), F32),
        grid_spec=grid_spec,
        compiler_params=_cparams(("parallel", "arbitrary")),
        name="nsa_attend_dec",
    )(table, *([sel_pages] * n_pg), qc_pad, q_bd, gate_l, cmp_p, kvp, cache_win, wmap, fold, hot)
    o6 = o_t.reshape(B, N_KV, HEAD_DIM, N_KV, group, qb)
    o5 = jnp.einsum("bkdkgq->bqkgd", o6)
    return o5.reshape(B * qb, N_HEADS * HEAD_DIM)


def _rope_tables(pos, half):
    inv = jnp.exp(-math.log(ROPE_THETA) * jnp.arange(half, dtype=F32) / half)
    ang = pos.astype(F32)[:, None] * inv[None, :]
    return jnp.cos(ang), jnp.sin(ang)


def _prep_weights(ret_w_in, ret_w_out, ffn_w_in, ffn_w_out, kv_w, kv_knorm, cmp_w1, cmp_w2,
                  nsa_w_qg, nsa_qnorm, nsa_w_o):
    n_b = nsa_w_qg.shape[0]
    nq = N_HEADS * HEAD_DIM
    qg_pad = nq + LANES - nsa_w_qg.shape[2]
    w_qg = jnp.pad(nsa_w_qg, ((0, 0), (0, 0), (0, qg_pad))).astype(BF16)
    w_o = jnp.pad(nsa_w_o.reshape(n_b, N_HEADS, 1, HEAD_DIM, D_MODEL),
                  ((0, 0), (0, 0), (1, 0), (0, 0), (0, 0))).reshape(n_b, N_HEADS * LANES, D_MODEL)
    R = CMP_LEN // CMP_STRIDE
    w1 = cmp_w1.reshape(2, R, CMP_STRIDE, HEAD_DIM, CMP_HID).transpose(0, 2, 3, 1, 4)
    w1 = w1.reshape(2, CMP_STRIDE, HEAD_DIM, R * CMP_HID)
    w1dup = jnp.concatenate([w1, w1], axis=2).astype(BF16)
    w1dup = w1dup.reshape(2, CMP_STRIDE // 2, 2 * LANES, R * CMP_HID)
    z = jnp.zeros((CMP_HID, HEAD_DIM), F32)
    w2p = jnp.stack([jnp.concatenate([cmp_w2[0], z], axis=1),
                     jnp.concatenate([z, cmp_w2[1]], axis=1)]).astype(BF16)
    ones = jnp.ones((HEAD_DIM,), F32)
    return dict(
        ret_w_in=ret_w_in.astype(BF16), ret_w_out=ret_w_out.astype(BF16),
        ffn_w_in=ffn_w_in.astype(BF16), ffn_w_out=ffn_w_out.astype(BF16),
        kv_w=kv_w.astype(BF16), w_qg=w_qg, w_o=w_o.astype(BF16), w_o_raw=nsa_w_o.astype(BF16),
        kv_gain=jnp.tile(kv_knorm, (1, 2)),
        cmp_gain=jnp.concatenate([kv_knorm[0], ones]).reshape(1, LANES),
        q_gain=jnp.tile(nsa_qnorm, (1, 2)) * (HEAD_DIM ** -0.5),
        q_gain2=jnp.tile(nsa_qnorm, (1, 2)) * (HEAD_DIM ** -0.5 * math.log2(math.e)),
        score_bound=(1.05 * HEAD_DIM ** 0.5 * math.log2(math.e))
        * jnp.max(jnp.abs(nsa_qnorm), axis=1)[:, None] * jnp.max(jnp.abs(kv_knorm), axis=1)[None, :],
        w1dup=w1dup, w2p=w2p,
        bd=jnp.asarray(np.kron(np.eye(2), np.ones((HEAD_DIM, HEAD_DIM))), BF16),
    )


def _trunk(x, past_len, ret_s0, conv0, ctx, W, P):
    B, T, D = x.shape
    M = B * T
    depth = P["norm_mix"].shape[0]
    n_a = P["ret_w_in"].shape[0]
    pos = past_len + jnp.arange(T)
    cos_r, sin_r = _rope_tables(pos, RET_DK // 2)
    c32, s32 = _rope_tables(pos, HEAD_DIM // 2)
    cos_n = jnp.tile(c32, (1, 4))
    sin_n = jnp.concatenate([-s32, s32, -s32, s32], axis=1)
    lg = jnp.log1p(-jnp.exp2(-5.0 - jnp.arange(RET_HEADS, dtype=F32)))
    L = RET_CHUNK if T % RET_CHUNK == 0 else T
    gl = jnp.exp(L * lg)
    tm = min(512, M)
    tf = 1024 if T % 1024 == 0 else 512
    tb = min(512, T)
    tq = min(512, T)
    qb = next((c for c in (2 * Q_BLOCK, Q_BLOCK) if T % c == 0), T)
    tt = min(256, T)

    x2 = x.reshape(M, D)
    ret_states, conv_states = [], []
    for layer in range(depth):
        if layer == n_a:
            kvp, selp, winp = kv_project(x2.reshape(B, T, D), P["kv_norm"], W["kv_w"], W["kv_gain"],
                                         cos_n, sin_n, W["bd"], tq)
            nk2 = 2 * N_KV * HEAD_DIM
            if ctx is None:
                table = jnp.arange(M // PAGE, dtype=jnp.int32).reshape(B, T // PAGE)
                cmp_p = compress(kvp.reshape(M // PAGE, PAGE, 3 * nk2), table,
                                 W["w1dup"], W["pb"], W["w2p"], W["cmp_gain"])
                sel_p, win_p = selp, winp
                w_off = 0
            else:
                cache_cmp, cache_sel, cache_win, table = ctx
                n_pool = cache_cmp.shape[0]
                cmp_p = compress(cache_cmp.reshape(n_pool, PAGE, nk2), table,
                                 W["w1dup"], W["pb"], W["w2p"], W["cmp_gain"])
                sel_pages = lax.optimization_barrier(cache_sel.reshape(n_pool, PAGE, nk2))
                wl = cache_win.shape[1]
                win_rows = cache_win.reshape(B, wl, nk2)
                w_off = past_len - wl
            n_sel = -(-(past_len + T) // SEL_BLOCK)
        h_norm = P["norm_mix"][layer]
        if layer < n_a:
            proj = norm_matmul(x2, h_norm, W["ret_w_in"][layer], min(1024, M), 1024)
            og, s_new = retention(proj.reshape(B, T, -1), ret_s0[layer], cos_r, sin_r, lg, gl, L, tb)
            ret_states.append(s_new)
            x2 = matmul_res(og.reshape(M, -1), W["ret_w_out"][layer], x2, tm)
        else:
            j = layer - n_a
            if qb % LANES == 0:
                qc, qr, gates = qg_project_t(x2.reshape(B, T, D), h_norm, W["w_qg"][j],
                                             W["q_gain2"][j:j + 1], cos_n, sin_n, W["bd"], tq, qb)
                shifts = W["score_bound"][j]
                args = (qc, qr, gates, cmp_p, sel_p, win_p)
                o = lax.cond(
                    jnp.max(shifts) <= 30.0,
                    lambda a: nsa_attend_fast(shifts, *a, T, qb, past_len, w_off, n_sel),
                    lambda a: nsa_attend_t(*a, T, qb, past_len, w_off, n_sel),
                    args)
                x2 = matmul_res(o.reshape(M, -1), W["w_o"][j], x2, tm)
            else:
                qc, qr, gates = qg_project(x2.reshape(B, T, D), h_norm, W["w_qg"][j],
                                           W["q_gain2"][j:j + 1], cos_n, sin_n, W["bd"], tq, qb)
                o = nsa_attend_dec(table, qc, qr, gates, cmp_p, sel_pages, kvp, win_rows,
                                   qb, past_len, n_sel)
                x2 = matmul_res(o, W["w_o_raw"][j], x2, tm)
        if T % tf == 0:
            act, tail = ffn_in(x2, P["norm_ffn"][layer], W["ffn_w_in"][layer], conv0[layer],
                               P["ffn_conv_w"][layer], P["ffn_conv_b"][layer], T, tf, 256)
            conv_states.append(tail)
        else:
            proj = norm_matmul(x2, P["norm_ffn"][layer], W["ffn_w_in"][layer], tm, 512)
            proj3 = proj.reshape(B, T, 2 * D_FF)
            act = ffn_mid(proj3, conv0[layer], P["ffn_conv_w"][layer], P["ffn_conv_b"][layer], tt)
            conv_states.append(proj3[:, T - 2:, :D_FF])
        x2 = matmul_res(act.reshape(M, D_FF), W["ffn_w_out"][layer], x2, tm)

    nk = N_KV * HEAD_DIM
    new_cmp = kvp[:, :, 0:2 * nk].reshape(B, T, 2, N_KV, HEAD_DIM)
    new_sel = kvp[:, :, 2 * nk:4 * nk].reshape(B, T, 2, N_KV, HEAD_DIM)
    new_win = kvp[:, :, 4 * nk:6 * nk].reshape(B, T, 2, N_KV, HEAD_DIM)
    return (x2.reshape(B, T, D), jnp.stack(ret_states), jnp.stack(conv_states),
            new_cmp, new_sel, new_win)


def kernel(x_prompt, x_sample, cache_cmp_kv, cache_sel_kv, cache_win_kv, state_ret, state_conv,
           page_table, norm_mix, norm_ffn, ret_w_in, ret_w_out, ffn_w_in, ffn_conv_w, ffn_conv_b,
           ffn_w_out, kv_norm, kv_w, kv_knorm, cmp_pos, cmp_w1, cmp_w2, nsa_w_qg, nsa_qnorm, nsa_w_o):
    W = _prep_weights(ret_w_in, ret_w_out, ffn_w_in, ffn_w_out, kv_w, kv_knorm, cmp_w1, cmp_w2,
                      nsa_w_qg, nsa_qnorm, nsa_w_o)
    W["pb"] = pos_bias(cmp_pos, cmp_w1)
    P = dict(norm_mix=norm_mix, norm_ffn=norm_ffn, ret_w_in=ret_w_in, ffn_conv_w=ffn_conv_w,
             ffn_conv_b=ffn_conv_b, kv_norm=kv_norm)
    depth = norm_mix.shape[0]
    n_a = ret_w_in.shape[0]
    B, T, _ = x_prompt.shape
    zero_ret = jnp.zeros((n_a, B, RET_HEADS, RET_DK, RET_DV), F32)
    zero_conv = jnp.zeros((depth, B, 2, D_FF), F32)
    y_p, ret_p, conv_p, cmp_p, sel_p, win_p = _trunk(x_prompt, 0, zero_ret, zero_conv, None, W, P)
    win_p = win_p[:, T - min(WINDOW, T):]

    db, ts, _ = x_sample.shape
    past_len = page_table.shape[1] * PAGE
    ctx = (cache_cmp_kv, cache_sel_kv, cache_win_kv, page_table)
    y_s, ret_s, conv_s, cmp_s, sel_s, win_new = _trunk(x_sample, past_len, state_ret, state_conv,
                                                        ctx, W, P)
    all_win = jnp.concatenate([cache_win_kv, win_new], axis=1)
    win_s = all_win[:, all_win.shape[1] - min(WINDOW, past_len + ts):]
    return (y_p, y_s, ret_p, ret_s, conv_p, conv_s, cmp_p, cmp_s, sel_p, sel_s, win_p, win_s)
```

```python
import functools
import math

import jax
import jax.numpy as jnp
import numpy as np
from jax import lax
from jax.experimental import pallas as pl
from jax.experimental.pallas import tpu as pltpu

F32 = jnp.float32
BF16 = jnp.bfloat16

D_MODEL = 1024
PAGE = 128
RET_HEADS = 4
RET_DK = 256
RET_DV = 512
RET_CHUNK = 128
N_HEADS = 16
N_KV = 4
HEAD_DIM = 64
CMP_LEN = 32
CMP_STRIDE = 16
CMP_HID = 128
SEL_BLOCK = 64
SEL_TOPK = 16
WINDOW = 512
Q_BLOCK = 128
D_FF = 2816
ROPE_THETA = 10000.0
EPS = 1e-6
NEG_INF = -1e30
TINY = 1e-30
SEL_FORCE = 1e6
SEL_NEG = -1e6

LANES = 128
KEY_TILE = 512
HALO = 16
VMEM_LIMIT = 48 * 1024 * 1024


def _cparams(sem):
    return pltpu.CompilerParams(dimension_semantics=sem, vmem_limit_bytes=VMEM_LIMIT)


def _nt_dot(a, b):
    return lax.dot_general(a, b, (((1,), (1,)), ((), ())), preferred_element_type=F32)


def _tn_dot(a, b):
    return lax.dot_general(a, b, (((0,), (0,)), ((), ())), preferred_element_type=F32)


def _gelu(x):
    return 0.5 * x * (1.0 + jnp.tanh(math.sqrt(2.0 / math.pi) * (x + 0.044715 * (x * x * x))))


def _rms_rows(x, g):
    r = lax.rsqrt(jnp.mean(x * x, axis=-1, keepdims=True) + EPS)
    return x * r * g


def _head_ms(x, bd):
    x2 = x * x
    hi = x2.astype(BF16)
    lo = (x2 - hi.astype(F32)).astype(BF16)
    s = jnp.dot(hi, bd, preferred_element_type=F32) + jnp.dot(lo, bd, preferred_element_type=F32)
    return s * (1.0 / HEAD_DIM)


def _rope64(x, cos, sin):
    lane = lax.broadcasted_iota(jnp.int32, x.shape, 1)
    sw = jnp.where((lane & 63) < 32, pltpu.roll(x, 96, 1), pltpu.roll(x, 32, 1))
    return x * cos + sw * sin


def _pack_pair(k2, v2, dtype):
    lane = lax.broadcasted_iota(jnp.int32, k2.shape, 1)
    lo = lane < HEAD_DIM
    even = jnp.where(lo, k2, pltpu.roll(v2, HEAD_DIM, 1)).astype(dtype)
    odd = jnp.where(lo, pltpu.roll(k2, HEAD_DIM, 1), v2).astype(dtype)
    return even, odd


def _norm_matmul_kernel(x_ref, g_ref, w_ref, o_ref, h_ref):
    @pl.when(pl.program_id(1) == 0)
    def _():
        h_ref[...] = _rms_rows(x_ref[...], g_ref[...]).astype(BF16)

    o_ref[...] = jnp.dot(h_ref[...], w_ref[...], preferred_element_type=F32).astype(o_ref.dtype)


def norm_matmul(x, g, w, tm, tn):
    M, D = x.shape
    N = w.shape[1]
    return pl.pallas_call(
        _norm_matmul_kernel,
        out_shape=jax.ShapeDtypeStruct((M, N), F32),
        grid=(M // tm, N // tn),
        in_specs=[pl.BlockSpec((tm, D), lambda i, j: (i, 0)),
                  pl.BlockSpec((1, D), lambda i, j: (0, 0)),
                  pl.BlockSpec((D, tn), lambda i, j: (0, j))],
        out_specs=pl.BlockSpec((tm, tn), lambda i, j: (i, j)),
        scratch_shapes=[pltpu.VMEM((tm, D), BF16)],
        compiler_params=_cparams(("parallel", "arbitrary")),
        name="norm_matmul",
    )(x, g.reshape(1, D), w)


def _matmul_res_kernel(a_ref, w_ref, r_ref, o_ref):
    o_ref[...] = r_ref[...] + jnp.dot(a_ref[...].astype(BF16), w_ref[...],
                                      preferred_element_type=F32)


def matmul_res(a, w, res, tm):
    M, K = a.shape
    N = w.shape[1]
    return pl.pallas_call(
        _matmul_res_kernel,
        out_shape=jax.ShapeDtypeStruct((M, N), F32),
        grid=(M // tm,),
        in_specs=[pl.BlockSpec((tm, K), lambda i: (i, 0)),
                  pl.BlockSpec((K, N), lambda i: (0, 0)),
                  pl.BlockSpec((tm, N), lambda i: (i, 0))],
        out_specs=pl.BlockSpec((tm, N), lambda i: (i, 0)),
        compiler_params=_cparams(("parallel",)),
        name="matmul_res",
    )(a, w, res)


def _retention_kernel(lg_ref, gl_ref, q_ref, k_ref, v_ref, g_ref, cos_ref, sin_ref, s0_ref,
                      o_ref, sout_ref, S_ref, *, L, n_chunk):
    h = pl.program_id(1)
    t = pl.program_id(2)
    lg = lg_ref[h]
    gl = gl_ref[h]

    @pl.when(t == 0)
    def _():
        S_ref[...] = s0_ref[0, 0]

    ii = lax.broadcasted_iota(jnp.int32, (L, L), 0)
    jj = lax.broadcasted_iota(jnp.int32, (L, L), 1)
    diff = (ii - jj).astype(F32)
    decay = jnp.where(diff >= 0, jnp.exp(jnp.maximum(diff, 0.0) * lg), 0.0)
    idx = lax.broadcasted_iota(jnp.int32, (L, 1), 0).astype(F32)
    q_dec = jnp.exp((idx + 1.0) * lg)
    k_dec = jnp.exp((L - 1.0 - idx) * lg)
    half = RET_DK // 2

    for c in range(n_chunk):
        rows = pl.ds(c * L, L)
        cos = cos_ref[rows, :]
        sin = sin_ref[rows, :]

        def rope(x):
            x1, x2 = x[:, :half], x[:, half:]
            return jnp.concatenate([x1 * cos - x2 * sin, x2 * cos + x1 * sin], axis=1)

        qr = rope(q_ref[0, rows, :])
        kr = rope(k_ref[0, rows, :]) * (RET_DK ** -0.5)
        qb = qr.astype(BF16)
        vb = v_ref[0, rows, :].astype(BF16)
        sc = _nt_dot(qb, kr.astype(BF16)) * decay
        S = S_ref[...]
        o = jnp.dot(sc.astype(BF16), vb, preferred_element_type=F32)
        o = o + jnp.dot(qb, S.astype(BF16), preferred_element_type=F32) * q_dec
        S_ref[...] = S * gl + _tn_dot((kr * k_dec).astype(BF16), vb)
        on = o * lax.rsqrt(jnp.mean(o * o, axis=-1, keepdims=True) + EPS)
        g = g_ref[0, rows, :]
        o_ref[0, rows, :] = (on * (g * jax.nn.sigmoid(g))).astype(o_ref.dtype)

    @pl.when(t == pl.num_programs(2) - 1)
    def _():
        sout_ref[0, 0] = S_ref[...]


def retention(proj, s0, cos, sin, lg, gl, L, tb):
    B, T, _ = proj.shape
    n_chunk = tb // L
    odt = BF16 if tb % 16 == 0 else F32
    kern = functools.partial(_retention_kernel, L=L, n_chunk=n_chunk)
    grid_spec = pltpu.PrefetchScalarGridSpec(
        num_scalar_prefetch=2,
        grid=(B, RET_HEADS, T // tb),
        in_specs=[
            pl.BlockSpec((1, tb, RET_DK), lambda b, h, t, *_: (b, t, h)),
            pl.BlockSpec((1, tb, RET_DK), lambda b, h, t, *_: (b, t, RET_HEADS + h)),
            pl.BlockSpec((1, tb, RET_DV), lambda b, h, t, *_: (b, t, RET_HEADS + h)),
            pl.BlockSpec((1, tb, RET_DV), lambda b, h, t, *_: (b, t, 2 * RET_HEADS + h)),
            pl.BlockSpec((tb, RET_DK // 2), lambda b, h, t, *_: (t, 0)),
            pl.BlockSpec((tb, RET_DK // 2), lambda b, h, t, *_: (t, 0)),
            pl.BlockSpec((1, 1, RET_DK, RET_DV), lambda b, h, t, *_: (b, h, 0, 0)),
        ],
        out_specs=[
            pl.BlockSpec((1, tb, RET_DV), lambda b, h, t, *_: (b, t, h)),
            pl.BlockSpec((1, 1, RET_DK, RET_DV), lambda b, h, t, *_: (b, h, 0, 0)),
        ],
        scratch_shapes=[pltpu.VMEM((RET_DK, RET_DV), F32)],
    )
    return pl.pallas_call(
        kern,
        out_shape=[jax.ShapeDtypeStruct((B, T, RET_HEADS * RET_DV), odt),
                   jax.ShapeDtypeStruct((B, RET_HEADS, RET_DK, RET_DV), F32)],
        grid_spec=grid_spec,
        compiler_params=_cparams(("parallel", "parallel", "arbitrary")),
        name="retention",
    )(lg, gl, proj, proj, proj, proj, cos, sin, s0)


def _ffn_mid_kernel(u_ref, gt_ref, halo_ref, cw_ref, cb_ref, o_ref):
    u = u_ref[0]
    hl = halo_ref[0, 0]
    row = lax.broadcasted_iota(jnp.int32, u.shape, 0)
    u1 = jnp.where(row == 0, hl[1:2], pltpu.roll(u, 1, 0))
    u2 = jnp.where(row == 0, hl[0:1], jnp.where(row == 1, hl[1:2], pltpu.roll(u, 2, 0)))
    c = cb_ref[...] + cw_ref[0:1] * u2
    c = c + cw_ref[1:2] * u1
    c = c + cw_ref[2:3] * u
    o_ref[0] = (_gelu(c) * gt_ref[0]).astype(o_ref.dtype)


def ffn_mid(proj, buf, conv_w, conv_b, tt):
    B, T, _ = proj.shape
    nt = T // tt
    if nt > 1:
        tails = proj[:, :, :D_FF].reshape(B, nt, tt, D_FF)[:, :-1, tt - 2:, :]
        halo = jnp.concatenate([buf[:, None], tails], axis=1)
    else:
        halo = buf[:, None]
    odt = BF16 if tt % 16 == 0 else F32
    return pl.pallas_call(
        _ffn_mid_kernel,
        out_shape=jax.ShapeDtypeStruct((B, T, D_FF), odt),
        grid=(B, nt),
        in_specs=[pl.BlockSpec((1, tt, D_FF), lambda b, t: (b, t, 0)),
                  pl.BlockSpec((1, tt, D_FF), lambda b, t: (b, t, 1)),
                  pl.BlockSpec((1, 1, 2, D_FF), lambda b, t: (b, t, 0, 0)),
                  pl.BlockSpec((3, D_FF), lambda b, t: (0, 0)),
                  pl.BlockSpec((1, D_FF), lambda b, t: (0, 0))],
        out_specs=pl.BlockSpec((1, tt, D_FF), lambda b, t: (b, t, 0)),
        compiler_params=_cparams(("parallel", "parallel")),
        name="ffn_mid",
    )(proj, proj, halo, conv_w, conv_b.reshape(1, D_FF))


def _ffn_in_kernel(x_ref, xh_ref, g_ref, w_ref, buf_ref, cw_ref, cb_ref,
                   act_ref, tail_ref, *, tiles_per_seq, tn):
    seq_start = (pl.program_id(0) % tiles_per_seq) == 0
    h = _rms_rows(x_ref[...], g_ref[...]).astype(BF16)
    hh = _rms_rows(xh_ref[...], g_ref[...]).astype(BF16)
    tm = h.shape[0]
    row = lax.broadcasted_iota(jnp.int32, (tm, tn), 0)
    for j in range(D_FF // tn):
        cols = slice(j * tn, (j + 1) * tn)
        wu = w_ref[:, cols]
        u = jnp.dot(h, wu, preferred_element_type=F32)
        gt = jnp.dot(h, w_ref[:, D_FF + j * tn:D_FF + (j + 1) * tn], preferred_element_type=F32)
        uh = jnp.dot(hh, wu, preferred_element_type=F32)
        hl = jnp.where(seq_start, buf_ref[0, :, cols], uh[HALO - 2:, :])
        u1 = jnp.where(row == 0, hl[1:2], pltpu.roll(u, 1, 0))
        u2 = jnp.where(row == 0, hl[0:1], jnp.where(row == 1, hl[1:2], pltpu.roll(u, 2, 0)))
        c = cb_ref[:, cols] + cw_ref[0:1, cols] * u2
        c = c + cw_ref[1:2, cols] * u1
        c = c + cw_ref[2:3, cols] * u
        act_ref[:, cols] = (_gelu(c) * gt).astype(act_ref.dtype)
        tail_ref[0, :, cols] = u[tm - 2:, :]


def ffn_in(x, g, w, buf, conv_w, conv_b, T, tm, tn):
    M, D = x.shape
    tiles_per_seq = T // tm
    kern = functools.partial(_ffn_in_kernel, tiles_per_seq=tiles_per_seq, tn=tn)
    act, tails = pl.pallas_call(
        kern,
        out_shape=[jax.ShapeDtypeStruct((M, D_FF), BF16),
                   jax.ShapeDtypeStruct((M // tm, 2, D_FF), F32)],
        grid=(M // tm,),
        in_specs=[pl.BlockSpec((tm, D), lambda i: (i, 0)),
                  pl.BlockSpec((HALO, D), lambda i: (jnp.maximum(i * (tm // HALO) - 1, 0), 0)),
                  pl.BlockSpec((1, D), lambda i: (0, 0)),
                  pl.BlockSpec((D, 2 * D_FF), lambda i: (0, 0), pipeline_mode=pl.Buffered(1)),
                  pl.BlockSpec((1, 2, D_FF), lambda i: (i // tiles_per_seq, 0, 0)),
                  pl.BlockSpec((3, D_FF), lambda i: (0, 0)),
                  pl.BlockSpec((1, D_FF), lambda i: (0, 0))],
        out_specs=[pl.BlockSpec((tm, D_FF), lambda i: (i, 0)),
                   pl.BlockSpec((1, 2, D_FF), lambda i: (i, 0, 0))],
        compiler_params=_cparams(("parallel",)),
        name="ffn_in",
    )(x, x, g.reshape(1, D), w, buf, conv_w, conv_b.reshape(1, D_FF))
    return act, tails[tiles_per_seq - 1::tiles_per_seq]


def _kv_kernel(x_ref, g_ref, w_ref, gain_ref, cos_ref, sin_ref, bd_ref,
               kv_ref, selp_ref, winp_ref):
    h = _rms_rows(x_ref[0], g_ref[...]).astype(BF16)
    y = jnp.dot(h, w_ref[...], preferred_element_type=F32)
    cos = cos_ref[...]
    sin = sin_ref[...]
    bd = bd_ref[...]
    nk = N_KV * HEAD_DIM
    kv_ref[0, :, 0:2 * nk] = y[:, 0:2 * nk]
    for br, pack_ref in enumerate((selp_ref, winp_ref)):
        base = 2 * nk * (br + 1)
        kv_ref[0, :, base + nk:base + 2 * nk] = y[:, base + nk:base + 2 * nk]
        for p in range(2):
            kx = y[:, base + LANES * p:base + LANES * (p + 1)]
            kn = kx * lax.rsqrt(_head_ms(kx, bd) + EPS) * gain_ref[br + 1:br + 2, :]
            kr = _rope64(kn, cos, sin)
            kv_ref[0, :, base + LANES * p:base + LANES * (p + 1)] = kr
            vx = y[:, base + nk + LANES * p:base + nk + LANES * (p + 1)]
            even, odd = _pack_pair(kr, vx, pack_ref.dtype)
            pack_ref[0, 2 * p] = even
            pack_ref[0, 2 * p + 1] = odd


def kv_project(x, g, w, gains, cos, sin, bd, tm):
    B, T, D = x.shape
    N = w.shape[1]
    pdt = BF16 if tm % 16 == 0 else F32
    return pl.pallas_call(
        _kv_kernel,
        out_shape=[jax.ShapeDtypeStruct((B, T, N), F32),
                   jax.ShapeDtypeStruct((B, N_KV, T, LANES), pdt),
                   jax.ShapeDtypeStruct((B, N_KV, T, LANES), pdt)],
        grid=(B, T // tm),
        in_specs=[pl.BlockSpec((1, tm, D), lambda b, t: (b, t, 0)),
                  pl.BlockSpec((1, D), lambda b, t: (0, 0)),
                  pl.BlockSpec((D, N), lambda b, t: (0, 0)),
                  pl.BlockSpec((3, LANES), lambda b, t: (0, 0)),
                  pl.BlockSpec((tm, LANES), lambda b, t: (t, 0)),
                  pl.BlockSpec((tm, LANES), lambda b, t: (t, 0)),
                  pl.BlockSpec((LANES, LANES), lambda b, t: (0, 0))],
        out_specs=[pl.BlockSpec((1, tm, N), lambda b, t: (b, t, 0)),
                   pl.BlockSpec((1, N_KV, tm, LANES), lambda b, t: (b, 0, t, 0)),
                   pl.BlockSpec((1, N_KV, tm, LANES), lambda b, t: (b, 0, t, 0))],
        compiler_params=_cparams(("parallel", "parallel")),
        name="kv_project",
    )(x, g.reshape(1, D), w, gains, cos, sin, bd)


def _qg_kernel(x_ref, g_ref, w_ref, gain_ref, cos_ref, sin_ref, bd_ref,
               qc_ref, qr_ref, gate_ref, *, qb, n_qb):
    h = _rms_rows(x_ref[0], g_ref[...]).astype(BF16)
    y = jnp.dot(h, w_ref[...], preferred_element_type=F32)
    cos = cos_ref[...]
    sin = sin_ref[...]
    bd = bd_ref[...]
    nq = N_HEADS * HEAD_DIM
    gate_ref[0] = jax.nn.sigmoid(y[:, nq:nq + LANES])
    lane = lax.broadcasted_iota(jnp.int32, (y.shape[0], LANES), 1)
    lo = lane < HEAD_DIM
    group = N_HEADS // N_KV
    for p in range(N_HEADS // 2):
        qx = y[:, LANES * p:LANES * (p + 1)]
        qn = qx * lax.rsqrt(_head_ms(qx, bd) + EPS) * gain_ref[...]
        qr = _rope64(qn, cos, sin)
        for src, dst in ((qn, qc_ref), (qr, qr_ref)):
            for par in range(2):
                hd = 2 * p + par
                kvh, gi = hd // group, hd % group
                v = src if par == 0 else pltpu.roll(src, HEAD_DIM, 1)
                v = jnp.where(lo, v, 0.0).astype(dst.dtype)
                for j in range(n_qb):
                    dst[0, kvh, j, gi * qb:(gi + 1) * qb, :] = v[j * qb:(j + 1) * qb, :]


def qg_project(x, g, w, gain, cos, sin, bd, tm, qb):
    B, T, D = x.shape
    N = w.shape[1]
    n_qb = tm // qb
    group = N_HEADS // N_KV
    qdt = BF16 if qb % 16 == 0 else F32
    kern = functools.partial(_qg_kernel, qb=qb, n_qb=n_qb)
    qshape = jax.ShapeDtypeStruct((B, N_KV, T // qb, group * qb, LANES), qdt)
    qspec = pl.BlockSpec((1, N_KV, n_qb, group * qb, LANES), lambda b, t: (b, 0, t, 0, 0))
    return pl.pallas_call(
        kern,
        out_shape=[qshape, qshape, jax.ShapeDtypeStruct((B, T, LANES), F32)],
        grid=(B, T // tm),
        in_specs=[pl.BlockSpec((1, tm, D), lambda b, t: (b, t, 0)),
                  pl.BlockSpec((1, D), lambda b, t: (0, 0)),
                  pl.BlockSpec((D, N), lambda b, t: (0, 0)),
                  pl.BlockSpec((1, LANES), lambda b, t: (0, 0)),
                  pl.BlockSpec((tm, LANES), lambda b, t: (t, 0)),
                  pl.BlockSpec((tm, LANES), lambda b, t: (t, 0)),
                  pl.BlockSpec((LANES, LANES), lambda b, t: (0, 0))],
        out_specs=[qspec, qspec, pl.BlockSpec((1, tm, LANES), lambda b, t: (b, t, 0))],
        compiler_params=_cparams(("parallel", "parallel")),
        name="qg_project",
    )(x, g.reshape(1, D), w, gain, cos, sin, bd)


def _pos_bias_kernel(p_ref, w_ref, o_ref):
    o_ref[0] = jnp.dot(p_ref[0].astype(BF16), w_ref[0].astype(BF16), preferred_element_type=F32)


def pos_bias(cmp_pos, cmp_w1):
    K = CMP_LEN * HEAD_DIM
    p = jnp.broadcast_to(cmp_pos.reshape(2, 1, K), (2, 8, K))
    out = pl.pallas_call(
        _pos_bias_kernel,
        out_shape=jax.ShapeDtypeStruct((2, 8, CMP_HID), F32),
        grid=(2,),
        in_specs=[pl.BlockSpec((1, 8, K), lambda c: (c, 0, 0)),
                  pl.BlockSpec((1, K, CMP_HID), lambda c: (c, 0, 0))],
        out_specs=pl.BlockSpec((1, 8, CMP_HID), lambda c: (c, 0, 0)),
        name="pos_bias",
    )(p, cmp_w1)
    return out[:, 0, :]


def _compress_kernel(pt_ref, *refs, n_pg):
    pages = refs[:n_pg]
    w1_ref, pb_ref, w2_ref, gain_ref, out_ref, carry_ref, slab_ref = refs[n_pg:]
    g = pl.program_id(1)

    @pl.when(g == 0)
    def _():
        carry_ref[...] = jnp.zeros_like(carry_ref)

    n = n_pg * (PAGE // CMP_STRIDE)
    lo = lax.broadcasted_iota(jnp.int32, (n, LANES), 1) < HEAD_DIM
    row = lax.broadcasted_iota(jnp.int32, (N_KV * n, LANES), 0)
    res = jnp.zeros((N_KV * n, LANES), F32)
    for c in range(2):
        for pair in range(N_KV // 2):
            col = c * N_KV * HEAD_DIM + pair * LANES
            for i in range(n_pg):
                slab_ref[pair, PAGE * i:PAGE * (i + 1), :] = pages[i][0, :, col:col + LANES]
        acc = jnp.zeros((N_KV * n, 2 * CMP_HID), F32)
        for s in range(0, CMP_STRIDE, 2):
            parts = []
            for k in range(N_KV):
                pair, par = divmod(k, 2)
                keep = lo if par == 0 else jnp.logical_not(lo)
                parts.append(jnp.concatenate(
                    [jnp.where(keep, slab_ref[pair, pl.ds(s + i, n, stride=CMP_STRIDE), :], 0.0)
                     for i in range(2)], axis=1))
            xm = jnp.concatenate(parts, axis=0).astype(BF16)
            acc = acc + jnp.dot(xm, w1_ref[c, s // 2], preferred_element_type=F32)
        p0 = acc[:, :CMP_HID]
        p1 = acc[:, CMP_HID:]
        prev = pltpu.roll(p0, 1, 0)
        for k in range(N_KV):
            prev = jnp.where(row == k * n, carry_ref[2 * k + c, 7:8, :], prev)
        for k in range(N_KV):
            carry_ref[2 * k + c] = p0[(k + 1) * n - 8:(k + 1) * n, :]
        hid = _gelu(prev + p1 + pb_ref[c:c + 1, :]).astype(BF16)
        res = res + jnp.dot(hid, w2_ref[c], preferred_element_type=F32)
    for k in range(N_KV):
        r = res[k * n:(k + 1) * n, :]
        ms = jnp.sum(jnp.where(lo, r * r, 0.0), axis=-1, keepdims=True) * (1.0 / HEAD_DIM)
        kn = r * lax.rsqrt(ms + EPS) * gain_ref[...]
        out_ref[0, k] = jnp.where(lo, kn, r).astype(out_ref.dtype)


def _page_index(b, g, pt_ref, *, i, n_pg):
    return (pt_ref[b, g * n_pg + i], 0, 0)


def compress(pages_arr, table, w1dup, pb, w2p, gain):
    B, n_pages = table.shape
    n_pg = 16 if n_pages % 16 == 0 else n_pages
    n = n_pg * (PAGE // CMP_STRIDE)
    n_sub = n_pages * (PAGE // CMP_STRIDE)
    width = 2 * N_KV * HEAD_DIM
    in_specs = [pl.BlockSpec((1, PAGE, width), functools.partial(_page_index, i=i, n_pg=n_pg))
                for i in range(n_pg)]
    in_specs += [pl.BlockSpec(w1dup.shape, lambda b, g, pt: (0, 0, 0, 0)),
                 pl.BlockSpec(pb.shape, lambda b, g, pt: (0, 0)),
                 pl.BlockSpec(w2p.shape, lambda b, g, pt: (0, 0, 0)),
                 pl.BlockSpec((1, LANES), lambda b, g, pt: (0, 0))]
    grid_spec = pltpu.PrefetchScalarGridSpec(
        num_scalar_prefetch=1,
        grid=(B, n_pages // n_pg),
        in_specs=in_specs,
        out_specs=pl.BlockSpec((1, N_KV, n, LANES), lambda b, g, pt: (b, 0, g, 0)),
        scratch_shapes=[pltpu.VMEM((2 * N_KV, 8, CMP_HID), F32),
                        pltpu.VMEM((N_KV // 2, n_pg * PAGE, LANES), F32)],
    )
    return pl.pallas_call(
        functools.partial(_compress_kernel, n_pg=n_pg),
        out_shape=jax.ShapeDtypeStruct((B, N_KV, n_sub, LANES), BF16),
        grid_spec=grid_spec,
        compiler_params=_cparams(("parallel", "arbitrary")),
        name="compress",
    )(table, *([pages_arr] * n_pg), w1dup, pb, w2p, gain)


def _overlap_map(n_cmp_rows, n_selp):
    m = np.arange(n_cmp_rows)[None, :]
    s = np.arange(n_selp)[:, None]
    c0 = (m - 1) * CMP_STRIDE
    ov = np.minimum(c0 + CMP_LEN, s * SEL_BLOCK + SEL_BLOCK) - np.maximum(c0, s * SEL_BLOCK)
    w = np.maximum(ov, 0).astype(np.float32) / CMP_LEN
    w[:, 0] = 0.0
    return jnp.asarray(w, BF16)


def _qgt_kernel(x_ref, g_ref, w_ref, gain_ref, cos_ref, sin_ref, bd_ref,
                qc_ref, qr_ref, gate_ref, *, qb, n_qb):
    h = _rms_rows(x_ref[0], g_ref[...]).astype(BF16)
    y = jnp.dot(h, w_ref[...], preferred_element_type=F32)
    cos = cos_ref[...]
    sin = sin_ref[...]
    bd = bd_ref[...]
    nq = N_HEADS * HEAD_DIM
    group = N_HEADS // N_KV
    gate_t = jax.nn.sigmoid(y[:, nq:nq + LANES]).T
    for j in range(n_qb):
        gate_ref[0, j] = gate_t[:, j * qb:(j + 1) * qb]
    pad = jnp.zeros((N_KV, n_qb, HEAD_DIM, group * qb), qc_ref.dtype)
    qc_ref[0, :, :, HEAD_DIM:, :] = pad
    qr_ref[0, :, :, HEAD_DIM:, :] = pad
    for p in range(N_HEADS // 2):
        qx = y[:, LANES * p:LANES * (p + 1)]
        qn = qx * lax.rsqrt(_head_ms(qx, bd) + EPS) * gain_ref[...]
        qr = _rope64(qn, cos, sin)
        for src, dst in ((qn, qc_ref), (qr, qr_ref)):
            st = src.T.astype(dst.dtype)
            for par in range(2):
                kvh, gi = divmod(2 * p + par, group)
                for j in range(n_qb):
                    dst[0, kvh, j, 0:HEAD_DIM, gi * qb:(gi + 1) * qb] = (
                        st[par * HEAD_DIM:(par + 1) * HEAD_DIM, j * qb:(j + 1) * qb])


def qg_project_t(x, g, w, gain, cos, sin, bd, tm, qb):
    B, T, D = x.shape
    N = w.shape[1]
    n_qb = tm // qb
    group = N_HEADS // N_KV
    kern = functools.partial(_qgt_kernel, qb=qb, n_qb=n_qb)
    qshape = jax.ShapeDtypeStruct((B, N_KV, T // qb, LANES, group * qb), BF16)
    qspec = pl.BlockSpec((1, N_KV, n_qb, LANES, group * qb), lambda b, t: (b, 0, t, 0, 0))
    return pl.pallas_call(
        kern,
        out_shape=[qshape, qshape, jax.ShapeDtypeStruct((B, T // qb, LANES, qb), F32)],
        grid=(B, T // tm),
        in_specs=[pl.BlockSpec((1, tm, D), lambda b, t: (b, t, 0)),
                  pl.BlockSpec((1, D), lambda b, t: (0, 0)),
                  pl.BlockSpec((D, N), lambda b, t: (0, 0)),
                  pl.BlockSpec((1, LANES), lambda b, t: (0, 0)),
                  pl.BlockSpec((tm, LANES), lambda b, t: (t, 0)),
                  pl.BlockSpec((tm, LANES), lambda b, t: (t, 0)),
                  pl.BlockSpec((LANES, LANES), lambda b, t: (0, 0))],
        out_specs=[qspec, qspec, pl.BlockSpec((1, n_qb, LANES, qb), lambda b, t: (b, t, 0, 0))],
        compiler_params=_cparams(("parallel", "parallel")),
        name="qg_project_t",
    )(x, g.reshape(1, D), w, gain, cos, sin, bd)


def _nsa_t_kernel(qc_ref, qr_ref, gate_ref, cmp_ref, sel_ref, win_ref, wmap_ref, o_ref,
                  selneg_ref, m_ref, l_ref, acc_ref,
                  *, qb, n_sel, n_selp, n_cmp, q0, w_off, w_rows, l_win):
    qi = pl.program_id(1)
    group = N_HEADS // N_KV
    R = group * qb
    blocks_per_tile = KEY_TILE // SEL_BLOCK
    q_lo = q0 + qi * qb
    qpos_q = q_lo + lax.broadcasted_iota(jnp.int32, (1, qb), 1)
    n_kt = (q_lo + qb - 1) // KEY_TILE + 1
    w_start = pl.multiple_of(jnp.clip(q_lo - WINDOW - w_off, 0, l_win - w_rows), LANES)
    gate = gate_ref[0, 0]

    def lanes4(a):
        return jnp.concatenate([a] * group, axis=1)

    m_idx = lax.broadcasted_iota(jnp.int32, (n_cmp, qb), 0)
    cvalid = (m_idx >= 1) & ((m_idx - 1) * CMP_STRIDE + CMP_LEN - 1 <= qpos_q)
    cbias = lanes4(jnp.where(cvalid, 0.0, NEG_INF))
    any_c = lanes4(qpos_q >= CMP_LEN - 1)
    blk = lax.broadcasted_iota(jnp.int32, (n_selp, qb), 0)
    blk_f = blk.astype(F32)
    cur = qpos_q >> 6
    forced = (blk == 0) | (blk == cur) | (blk == cur - 1)
    reach = blk * SEL_BLOCK <= qpos_q
    real = blk < n_sel
    wpos = w_off + w_start + lax.broadcasted_iota(jnp.int32, (w_rows, qb), 0)
    wbias = lanes4(jnp.where((wpos <= qpos_q) & (wpos > qpos_q - WINDOW) & (wpos >= 0), 0.0, NEG_INF))
    row_t = lax.broadcasted_iota(jnp.int32, (KEY_TILE, qb), 0)

    for k in range(N_KV):
        ckv = cmp_ref[0, k]
        s = jnp.dot(ckv, qc_ref[0, k, 0], preferred_element_type=F32) + cbias
        e = jnp.exp2(s - jnp.max(s, axis=0, keepdims=True))
        den = jnp.maximum(jnp.sum(e, axis=0, keepdims=True), TINY)
        p = e * jnp.where(any_c, 1.0 / den, 0.0)
        oc = _tn_dot(ckv, p.astype(BF16))
        psum = p[:, 0:qb]
        for gi in range(1, group):
            psum = psum + p[:, gi * qb:(gi + 1) * qb]
        hi = psum.astype(BF16)
        lo = (psum - hi.astype(F32)).astype(BF16)
        imp = (jnp.dot(wmap_ref[...], hi, preferred_element_type=F32)
               + jnp.dot(wmap_ref[...], lo, preferred_element_type=F32))
        v = jnp.where(forced, SEL_FORCE, jnp.where(reach, imp, SEL_NEG))
        v = jnp.where(real, v, -jnp.inf)

        def pick_one(_, carry):
            v, sel = carry
            m = jnp.max(v, axis=0, keepdims=True)
            first = jnp.min(jnp.where(v == m, blk_f, float(n_selp)), axis=0, keepdims=True)
            pick = blk_f == first
            sel = jnp.where(pick & (m > 0.5 * SEL_NEG), 0.0, sel)
            return jnp.where(pick, -jnp.inf, v), sel

        _, sel = lax.fori_loop(0, SEL_TOPK, pick_one, (v, jnp.full((n_selp, qb), NEG_INF, F32)))
        selneg_ref[...] = sel

        qr = qr_ref[0, k, 0]
        m_ref[...] = jnp.full(m_ref.shape, NEG_INF, F32)
        l_ref[...] = jnp.zeros(l_ref.shape, F32)
        acc_ref[...] = jnp.zeros(acc_ref.shape, F32)

        def tile(kt, carry):
            start = pl.multiple_of(kt * KEY_TILE, KEY_TILE)
            kv = sel_ref[0, k, pl.ds(start, KEY_TILE), :]
            s = jnp.dot(kv, qr, preferred_element_type=F32)
            pieces = [jnp.broadcast_to(selneg_ref[pl.ds(kt * blocks_per_tile + j, 1), :], (SEL_BLOCK, qb))
                      for j in range(blocks_per_tile)]
            bias = jnp.concatenate(pieces, axis=0) + jnp.where(start + row_t <= qpos_q, 0.0, NEG_INF)
            s = s + lanes4(bias)
            m_old = m_ref[...]
            m_new = jnp.maximum(m_old, jnp.max(s, axis=0, keepdims=True))
            alpha = jnp.exp2(m_old - m_new)
            p = jnp.exp2(s - m_new)
            l_ref[...] = alpha * l_ref[...] + jnp.sum(p, axis=0, keepdims=True)
            acc_ref[...] = alpha * acc_ref[...] + _tn_dot(kv, p.astype(BF16))
            m_ref[...] = m_new
            return carry

        lax.fori_loop(0, n_kt, tile, 0)
        o_s = acc_ref[...] * (1.0 / jnp.maximum(l_ref[...], TINY))

        wkv = win_ref[0, k, pl.ds(w_start, w_rows), :]
        s = jnp.dot(wkv, qr, preferred_element_type=F32) + wbias
        e = jnp.exp2(s - jnp.max(s, axis=0, keepdims=True))
        den = jnp.maximum(jnp.sum(e, axis=0, keepdims=True), TINY)
        o_w = _tn_dot(wkv, e.astype(BF16)) * (1.0 / den)

        for gi in range(group):
            hd = k * group + gi
            sl = slice(gi * qb, (gi + 1) * qb)
            o = gate[3 * hd:3 * hd + 1, :] * oc[:, sl]
            o = o + gate[3 * hd + 1:3 * hd + 2, :] * o_s[:, sl]
            o = o + gate[3 * hd + 2:3 * hd + 3, :] * o_w[:, sl]
            o_ref[0, :, LANES * hd:LANES * (hd + 1)] = o.T.astype(o_ref.dtype)


def nsa_attend_t(qc, qr, gates, cmp_p, sel_p, win_p, T, qb, q0, w_off, n_sel):
    B = qc.shape[0]
    group = N_HEADS // N_KV
    R = group * qb
    n_cmp = cmp_p.shape[2]
    n_selp = -(-n_sel // LANES) * LANES
    l_sel = sel_p.shape[2]
    l_win = win_p.shape[2]
    w_rows = WINDOW + max(qb, LANES)
    assert l_win >= w_rows and l_sel >= ((q0 + T - 1) // KEY_TILE + 1) * KEY_TILE
    wmap = _overlap_map(n_cmp, n_selp)
    kern = functools.partial(_nsa_t_kernel, qb=qb, n_sel=n_sel, n_selp=n_selp, n_cmp=n_cmp,
                             q0=q0, w_off=w_off, w_rows=w_rows, l_win=l_win)
    qspec = pl.BlockSpec((1, N_KV, 1, LANES, R), lambda b, i: (b, 0, i, 0, 0))
    resident = dict(pipeline_mode=pl.Buffered(1))
    return pl.pallas_call(
        kern,
        out_shape=jax.ShapeDtypeStruct((B, T, N_HEADS * LANES), BF16),
        grid=(B, T // qb),
        in_specs=[
            qspec, qspec,
            pl.BlockSpec((1, 1, LANES, qb), lambda b, i: (b, i, 0, 0)),
            pl.BlockSpec((1, N_KV, n_cmp, LANES), lambda b, i: (b, 0, 0, 0)),
            pl.BlockSpec((1, N_KV, l_sel, LANES), lambda b, i: (b, 0, 0, 0), **resident),
            pl.BlockSpec((1, N_KV, l_win, LANES), lambda b, i: (b, 0, 0, 0), **resident),
            pl.BlockSpec((n_selp, n_cmp), lambda b, i: (0, 0)),
        ],
        out_specs=pl.BlockSpec((1, qb, N_HEADS * LANES), lambda b, i: (b, i, 0)),
        scratch_shapes=[
            pltpu.VMEM((n_selp, qb), F32),
            pltpu.VMEM((1, R), F32), pltpu.VMEM((1, R), F32), pltpu.VMEM((LANES, R), F32),
        ],
        compiler_params=_cparams(("parallel", "arbitrary")),
        name="nsa_attend_t",
    )(qc, qr, gates, cmp_p, sel_p, win_p, wmap)


def _nsa_fast_kernel(shift_ref, qc_ref, qr_ref, gate_ref, cmp_ref, sel_ref, win_ref, wmap_ref, hot_ref,
                     o_ref, qaug_ref, oc_ref, ow_ref, l_ref, acc_ref,
                     *, qb, n_sel, n_selp, n_cmp, q0, w_off, w_rows, l_win):
    qi = pl.program_id(1)
    group = N_HEADS // N_KV
    R = group * qb
    q_lo = q0 + qi * qb
    qpos_q = q_lo + lax.broadcasted_iota(jnp.int32, (1, qb), 1)
    n_kt = (q_lo + qb - 1) // KEY_TILE + 1
    w_start = pl.multiple_of(jnp.clip(q_lo - WINDOW - w_off, 0, l_win - w_rows), LANES)
    shift_c = shift_ref[0]
    shift_s = shift_ref[1]
    shift_w = shift_ref[2]

    def lanes4(a):
        return jnp.concatenate([a] * group, axis=1)

    m_idx = lax.broadcasted_iota(jnp.int32, (n_cmp, qb), 0)
    cvalid = (m_idx >= 1) & ((m_idx - 1) * CMP_STRIDE + CMP_LEN - 1 <= qpos_q)
    cbias = lanes4(jnp.where(cvalid, -shift_c, NEG_INF))
    any_c = lanes4(qpos_q >= CMP_LEN - 1)
    blk = lax.broadcasted_iota(jnp.int32, (n_selp, qb), 0)
    blk_f = blk.astype(F32)
    cur = qpos_q >> 6
    forced = (blk == 0) | (blk == cur) | (blk == cur - 1)
    reach = blk * SEL_BLOCK <= qpos_q
    real = blk < n_sel

    imps = []
    for k in range(N_KV):
        ckv = cmp_ref[0, k]
        e = jnp.exp2(jnp.dot(ckv, qc_ref[0, k, 0], preferred_element_type=F32) + cbias)
        den = jnp.maximum(jnp.sum(e, axis=0, keepdims=True), TINY)
        p = e * jnp.where(any_c, 1.0 / den, 0.0)
        oc_ref[k] = _tn_dot(ckv, p.astype(BF16))
        psum = p[:, 0:qb]
        for gi in range(1, group):
            psum = psum + p[:, gi * qb:(gi + 1) * qb]
        hi = psum.astype(BF16)
        lo = (psum - hi.astype(F32)).astype(BF16)
        imp = (jnp.dot(wmap_ref[...], hi, preferred_element_type=F32)
               + jnp.dot(wmap_ref[...], lo, preferred_element_type=F32))
        v = jnp.where(forced, SEL_FORCE, jnp.where(reach, imp, SEL_NEG))
        imps.append(jnp.where(real, v, -jnp.inf))

    def pick_one(_, vs):
        out = []
        for v in vs:
            m = jnp.max(v, axis=0, keepdims=True)
            first = jnp.min(jnp.where(v == m, blk_f, float(n_selp)), axis=0, keepdims=True)
            out.append(jnp.where(blk_f == first, -jnp.inf, v))
        return tuple(out)

    wpos = w_off + w_start + lax.broadcasted_iota(jnp.int32, (w_rows, qb), 0)
    wvalid = (wpos <= qpos_q) & (wpos > qpos_q - WINDOW) & (wpos >= 0)
    wbias = lanes4(jnp.where(wvalid, -shift_w, NEG_INF))
    for k in range(N_KV):
        wkv = win_ref[0, k, pl.ds(w_start, w_rows), :]
        e = jnp.exp2(jnp.dot(wkv, qr_ref[0, k, 0], preferred_element_type=F32) + wbias)
        den = jnp.maximum(jnp.sum(e, axis=0, keepdims=True), TINY)
        ow_ref[k] = _tn_dot(wkv, e.astype(BF16)) * (1.0 / den)

    marked = lax.fori_loop(0, SEL_TOPK, pick_one, tuple(imps), unroll=True)
    for k in range(N_KV):
        sel = jnp.where((marked[k] == -jnp.inf) & (imps[k] > 0.5 * SEL_NEG), -shift_s, NEG_INF)
        qaug_ref[k, 0:LANES, :] = qr_ref[0, k, 0]
        qaug_ref[k, LANES:, :] = lanes4(sel).astype(BF16)

    l_ref[...] = jnp.zeros(l_ref.shape, F32)
    acc_ref[...] = jnp.zeros(acc_ref.shape, F32)

    def tile(start, rows, causal_bias):
        hot = hot_ref[pl.ds(start, rows), :]
        for k in range(N_KV):
            kv = sel_ref[0, k, pl.ds(start, rows), :]
            s = jnp.dot(jnp.concatenate([kv, hot], axis=1), qaug_ref[k], preferred_element_type=F32)
            if causal_bias is not None:
                s = s + causal_bias
            p = jnp.exp2(s)
            l_ref[k] += jnp.sum(p.reshape(rows // 8, 8, R), axis=0)
            acc_ref[k] += _tn_dot(kv, p.astype(BF16))

    def double_tile(i, carry):
        tile(pl.multiple_of(i * 2 * KEY_TILE, 2 * KEY_TILE), 2 * KEY_TILE, None)
        return carry

    n_below = n_kt - 1
    lax.fori_loop(0, n_below // 2, double_tile, 0)

    @pl.when(n_below % 2 == 1)
    def _():
        tile(pl.multiple_of((n_below - 1) * KEY_TILE, KEY_TILE), KEY_TILE, None)

    d_start = pl.multiple_of(n_below * KEY_TILE, KEY_TILE)
    for n_sub in range(1, KEY_TILE // qb + 1):
        rows = n_sub * qb

        @pl.when(q_lo - d_start == rows - qb)
        def _(rows=rows):
            row_t = lax.broadcasted_iota(jnp.int32, (rows, qb), 0)
            tile(d_start, rows, lanes4(jnp.where(d_start + row_t <= qpos_q, 0.0, NEG_INF)))

    gate = gate_ref[0, 0]
    for k in range(N_KV):
        o_w = ow_ref[k]
        l_s = jnp.maximum(jnp.sum(l_ref[k], axis=0, keepdims=True), TINY)
        o_s = acc_ref[k] * (1.0 / l_s)
        oc = oc_ref[k]
        for gi in range(group):
            hd = k * group + gi
            sl = slice(gi * qb, (gi + 1) * qb)
            o = gate[3 * hd:3 * hd + 1, :] * oc[:, sl]
            o = o + gate[3 * hd + 1:3 * hd + 2, :] * o_s[:, sl]
            o = o + gate[3 * hd + 2:3 * hd + 3, :] * o_w[:, sl]
            o_ref[0, :, LANES * hd:LANES * (hd + 1)] = o.T.astype(o_ref.dtype)


def nsa_attend_fast(shifts, qc, qr, gates, cmp_p, sel_p, win_p, T, qb, q0, w_off, n_sel):
    B = qc.shape[0]
    group = N_HEADS // N_KV
    R = group * qb
    n_cmp = cmp_p.shape[2]
    n_selp = -(-n_sel // LANES) * LANES
    l_sel = sel_p.shape[2]
    l_win = win_p.shape[2]
    w_rows = WINDOW + max(qb, LANES)
    assert l_win >= w_rows and l_sel >= ((q0 + T - 1) // KEY_TILE + 1) * KEY_TILE
    assert n_sel >= SEL_TOPK and q0 % qb == 0 and KEY_TILE % qb == 0
    wmap = _overlap_map(n_cmp, n_selp)
    hot = np.zeros((l_sel, n_selp), np.float32)
    hot[np.arange(l_sel), np.arange(l_sel) // SEL_BLOCK] = 1.0
    hot = jnp.asarray(hot, BF16)
    kern = functools.partial(_nsa_fast_kernel, qb=qb, n_sel=n_sel, n_selp=n_selp, n_cmp=n_cmp,
                             q0=q0, w_off=w_off, w_rows=w_rows, l_win=l_win)
    qspec = pl.BlockSpec((1, N_KV, 1, LANES, R), lambda b, i, sh: (b, 0, i, 0, 0))
    resident = dict(pipeline_mode=pl.Buffered(1))
    grid_spec = pltpu.PrefetchScalarGridSpec(
        num_scalar_prefetch=1,
        grid=(B, T // qb),
        in_specs=[
            qspec, qspec,
            pl.BlockSpec((1, 1, LANES, qb), lambda b, i, sh: (b, i, 0, 0)),
            pl.BlockSpec((1, N_KV, n_cmp, LANES), lambda b, i, sh: (b, 0, 0, 0)),
            pl.BlockSpec((1, N_KV, l_sel, LANES), lambda b, i, sh: (b, 0, 0, 0), **resident),
            pl.BlockSpec((1, N_KV, l_win, LANES), lambda b, i, sh: (b, 0, 0, 0), **resident),
            pl.BlockSpec((n_selp, n_cmp), lambda b, i, sh: (0, 0)),
            pl.BlockSpec((l_sel, n_selp), lambda b, i, sh: (0, 0), **resident),
        ],
        out_specs=pl.BlockSpec((1, qb, N_HEADS * LANES), lambda b, i, sh: (b, i, 0)),
        scratch_shapes=[
            pltpu.VMEM((N_KV, 2 * LANES, R), BF16),
            pltpu.VMEM((N_KV, LANES, R), F32),
            pltpu.VMEM((N_KV, LANES, R), F32),
            pltpu.VMEM((N_KV, 8, R), F32),
            pltpu.VMEM((N_KV, LANES, R), F32),
        ],
    )
    return pl.pallas_call(
        kern,
        out_shape=jax.ShapeDtypeStruct((B, T, N_HEADS * LANES), BF16),
        grid_spec=grid_spec,
        compiler_params=_cparams(("parallel", "arbitrary")),
        name="nsa_attend_fast",
    )(shifts, qc, qr, gates, cmp_p, sel_p, win_p, wmap, hot)


def _nsa_dec_kernel(pt_ref, *refs, n_pg, qb, n_sel, n_selp, n_cmp, q0, w_off):
    pages = refs[:n_pg]
    (qcp_ref, qbd_ref, gate_ref, cmp_ref, new_ref, win_ref, wmap_ref, fold_ref, hot_ref,
     o_ref, qaug_ref, oc_ref, m_ref, l_ref, acc_ref) = refs[n_pg:]
    g = pl.program_id(1)
    nk = N_KV * HEAD_DIM
    t_new = new_ref.shape[1]
    lane = lax.broadcasted_iota(jnp.int32, (1, LANES), 1)
    qpos = q0 + (lane & (qb - 1))

    @pl.when(g == 0)
    def _first():
        s = jnp.dot(cmp_ref[0, 0], qcp_ref[0, 0], preferred_element_type=F32)
        for k in range(1, N_KV):
            s = s + jnp.dot(cmp_ref[0, k], qcp_ref[0, k], preferred_element_type=F32)
        m_idx = lax.broadcasted_iota(jnp.int32, (n_cmp, LANES), 0)
        cvalid = (m_idx >= 1) & ((m_idx - 1) * CMP_STRIDE + CMP_LEN - 1 <= qpos)
        s = s + jnp.where(cvalid, 0.0, NEG_INF)
        e = jnp.exp2(s - jnp.max(s, axis=0, keepdims=True))
        den = jnp.maximum(jnp.sum(e, axis=0, keepdims=True), TINY)
        p = e * jnp.where(qpos >= CMP_LEN - 1, 1.0 / den, 0.0)
        pb = p.astype(BF16)
        for k in range(N_KV):
            oc_ref[k * HEAD_DIM:(k + 1) * HEAD_DIM, :] = _tn_dot(cmp_ref[0, k], pb)[HEAD_DIM:, :]
        fold = fold_ref[...]
        p_lo = (p - pb.astype(F32)).astype(BF16)
        psum = jnp.dot(pb, fold, preferred_element_type=F32) + jnp.dot(p_lo, fold, preferred_element_type=F32)
        hi = psum.astype(BF16)
        lo = (psum - hi.astype(F32)).astype(BF16)
        imp = (jnp.dot(wmap_ref[...], hi, preferred_element_type=F32)
               + jnp.dot(wmap_ref[...], lo, preferred_element_type=F32))
        blk = lax.broadcasted_iota(jnp.int32, (n_selp, LANES), 0)
        blk_f = blk.astype(F32)
        cur = qpos >> 6
        forced = (blk == 0) | (blk == cur) | (blk == cur - 1)
        v = jnp.where(forced, SEL_FORCE, jnp.where(blk * SEL_BLOCK <= qpos, imp, SEL_NEG))
        v = jnp.where(blk < n_sel, v, -jnp.inf)

        def pick_one(_, carry):
            v, sel = carry
            m = jnp.max(v, axis=0, keepdims=True)
            first = jnp.min(jnp.where(v == m, blk_f, float(n_selp)), axis=0, keepdims=True)
            pick = blk_f == first
            sel = jnp.where(pick & (m > 0.5 * SEL_NEG), 0.0, sel)
            return jnp.where(pick, -jnp.inf, v), sel

        _, sel = lax.fori_loop(0, SEL_TOPK, pick_one, (v, jnp.full((n_selp, LANES), NEG_INF, F32)))
        qaug_ref[0:nk, :] = qbd_ref[0]
        qaug_ref[nk:, :] = sel.astype(BF16)
        m_ref[...] = jnp.full(m_ref.shape, NEG_INF, F32)
        l_ref[...] = jnp.zeros(l_ref.shape, F32)
        acc_ref[...] = jnp.zeros(acc_ref.shape, F32)

    def attend(kx, vx, hot, bias):
        s = jnp.dot(jnp.concatenate([kx, hot], axis=1), qaug_ref[...], preferred_element_type=F32)
        if bias is not None:
            s = s + bias
        m_old = m_ref[...]
        m_new = jnp.maximum(m_old, jnp.max(s, axis=0, keepdims=True))
        alpha = jnp.exp2(m_old - m_new)
        p = jnp.exp2(s - m_new)
        l_ref[...] = alpha * l_ref[...] + jnp.sum(p.reshape(p.shape[0] // 8, 8, LANES), axis=0)
        acc_ref[...] = alpha * acc_ref[...] + _tn_dot(vx, p.astype(BF16))
        m_ref[...] = m_new

    x = jnp.concatenate([pages[i][0] for i in range(n_pg)], axis=0).astype(BF16)
    start = pl.multiple_of(g * (n_pg * PAGE), n_pg * PAGE)
    attend(x[:, :nk], x[:, nk:], hot_ref[pl.ds(start, n_pg * PAGE), :], None)

    @pl.when(g == pl.num_programs(1) - 1)
    def _last():
        pad = jnp.zeros((HALO - t_new, 2 * nk), F32)
        row = lax.broadcasted_iota(jnp.int32, (HALO, LANES), 0)
        new_ok = (row < t_new) & (q0 + row <= qpos)
        xn = jnp.concatenate([new_ref[0, :, 2 * nk:4 * nk], pad], axis=0).astype(BF16)
        hot_new = jnp.where(
            lax.broadcasted_iota(jnp.int32, (HALO, n_selp), 1) == q0 // SEL_BLOCK, 1.0, 0.0).astype(BF16)
        attend(xn[:, :nk], xn[:, nk:], hot_new, jnp.where(new_ok, 0.0, NEG_INF))

        qbd = qbd_ref[0]
        xw = win_ref[0].astype(BF16)
        wrow = lax.broadcasted_iota(jnp.int32, (xw.shape[0], LANES), 0)
        wpos = w_off + wrow
        wvalid = (wpos <= qpos) & (wpos > qpos - WINDOW) & (wpos >= 0)
        s_c = jnp.dot(xw[:, :nk], qbd, preferred_element_type=F32) + jnp.where(wvalid, 0.0, NEG_INF)
        xwn = jnp.concatenate([new_ref[0, :, 4 * nk:6 * nk], pad], axis=0).astype(BF16)
        npos = q0 + row
        nvalid = (row < t_new) & (npos <= qpos) & (npos > qpos - WINDOW)
        s_n = jnp.dot(xwn[:, :nk], qbd, preferred_element_type=F32) + jnp.where(nvalid, 0.0, NEG_INF)
        m_w = jnp.maximum(jnp.max(s_c, axis=0, keepdims=True), jnp.max(s_n, axis=0, keepdims=True))
        e_c = jnp.exp2(s_c - m_w)
        e_n = jnp.exp2(s_n - m_w)
        den = jnp.sum(e_c, axis=0, keepdims=True) + jnp.sum(e_n, axis=0, keepdims=True)
        o_w = _tn_dot(xw[:, nk:], e_c.astype(BF16)) + _tn_dot(xwn[:, nk:], e_n.astype(BF16))
        o_w = o_w * (1.0 / jnp.maximum(den, TINY))
        l_s = jnp.maximum(jnp.sum(l_ref[...], axis=0, keepdims=True), TINY)
        o_s = acc_ref[...] * (1.0 / l_s)
        gate = gate_ref[0]
        o_ref[0] = gate[0:1, :] * oc_ref[...] + gate[1:2, :] * o_s + gate[2:3, :] * o_w


def _dec_page_index(b, g, pt, *, i, n_pg):
    return (pt[b, g * n_pg + i], 0, 0)


def nsa_attend_dec(table, qc, qr, gates, cmp_p, sel_pages, kvp, cache_win, qb, q0, n_sel):
    B, n_pages = table.shape
    group = N_HEADS // N_KV
    assert N_KV * group * qb == LANES
    nk = N_KV * HEAD_DIM
    n_pg = 16 if n_pages % 16 == 0 else n_pages
    n_cmp = cmp_p.shape[2]
    n_selp = -(-n_sel // LANES) * LANES
    wl = cache_win.shape[1]
    eye = jnp.eye(N_KV, dtype=F32)

    def spread(q):
        qt = q[:, :, 0, :, :HEAD_DIM].astype(F32).transpose(0, 1, 3, 2)
        return jnp.einsum("bkdr,kj->bkdjr", qt, eye).reshape(B, N_KV, HEAD_DIM, LANES)

    qc_pad = jnp.pad(spread(qc), ((0, 0), (0, 0), (0, LANES - HEAD_DIM), (0, 0))).astype(BF16)
    q_bd = spread(qr).reshape(B, nk, LANES).astype(BF16)
    gate_l = gates[:, :, :3 * N_HEADS].reshape(B, qb, N_KV, group, 3).transpose(0, 4, 2, 3, 1)
    gate_l = gate_l.reshape(B, 3, LANES)
    wmap = _overlap_map(n_cmp, n_selp)
    lane = np.arange(LANES)
    fold = jnp.asarray((lane[:, None] // (group * qb) == lane[None, :] // (group * qb))
                       & (lane[:, None] % qb == lane[None, :] % qb), BF16)
    l_past = n_pages * PAGE
    hot = np.zeros((l_past, n_selp), np.float32)
    hot[np.arange(l_past), np.arange(l_past) // SEL_BLOCK] = 1.0
    hot = jnp.asarray(hot, BF16)
    kern = functools.partial(_nsa_dec_kernel, n_pg=n_pg, qb=qb, n_sel=n_sel, n_selp=n_selp,
                             n_cmp=n_cmp, q0=q0, w_off=q0 - wl)
    in_specs = [pl.BlockSpec((1, PAGE, 2 * nk), functools.partial(_dec_page_index, i=i, n_pg=n_pg))
                for i in range(n_pg)]
    in_specs += [
        pl.BlockSpec((1, N_KV, LANES, LANES), lambda b, g, pt: (b, 0, 0, 0)),
        pl.BlockSpec((1, nk, LANES), lambda b, g, pt: (b, 0, 0)),
        pl.BlockSpec((1, 3, LANES), lambda b, g, pt: (b, 0, 0)),
        pl.BlockSpec((1, N_KV, n_cmp, LANES), lambda b, g, pt: (b, 0, 0, 0)),
        pl.BlockSpec((1, qb, 6 * nk), lambda b, g, pt: (b, 0, 0)),
        pl.BlockSpec((1, wl, 2 * nk), lambda b, g, pt: (b, 0, 0)),
        pl.BlockSpec((n_selp, n_cmp), lambda b, g, pt: (0, 0)),
        pl.BlockSpec((LANES, LANES), lambda b, g, pt: (0, 0)),
        pl.BlockSpec((l_past, n_selp), lambda b, g, pt: (0, 0), pipeline_mode=pl.Buffered(1)),
    ]
    grid_spec = pltpu.PrefetchScalarGridSpec(
        num_scalar_prefetch=1,
        grid=(B, n_pages // n_pg),
        in_specs=in_specs,
        out_specs=pl.BlockSpec((1, nk, LANES), lambda b, g, pt: (b, 0, 0)),
        scratch_shapes=[
            pltpu.VMEM((nk + n_selp, LANES), BF16),
            pltpu.VMEM((nk, LANES), F32),
            pltpu.VMEM((1, LANES), F32),
            pltpu.VMEM((8, LANES), F32),
            pltpu.VMEM((nk, LANES), F32),
        ],
    )
    o_t = pl.pallas_call(
        kern,
        out_shape=jax.ShapeDtypeStruct((B, nk, LANES), F32),
        grid_spec=grid_spec,
        compiler_params=_cparams(("parallel", "arbitrary")),
        name="nsa_attend_dec",
    )(table, *([sel_pages] * n_pg), qc_pad, q_bd, gate_l, cmp_p, kvp, cache_win, wmap, fold, hot)
    o6 = o_t.reshape(B, N_KV, HEAD_DIM, N_KV, group, qb)
    o5 = jnp.einsum("bkdkgq->bqkgd", o6)
    return o5.reshape(B * qb, N_HEADS * HEAD_DIM)


def _rope_tables(pos, half):
    inv = jnp.exp(-math.log(ROPE_THETA) * jnp.arange(half, dtype=F32) / half)
    ang = pos.astype(F32)[:, None] * inv[None, :]
    return jnp.cos(ang), jnp.sin(ang)


def _prep_weights(ret_w_in, ret_w_out, ffn_w_in, ffn_w_out, kv_w, kv_knorm, cmp_w1, cmp_w2,
                  nsa_w_qg, nsa_qnorm, nsa_w_o):
    n_b = nsa_w_qg.shape[0]
    nq = N_HEADS * HEAD_DIM
    qg_pad = nq + LANES - nsa_w_qg.shape[2]
    w_qg = jnp.pad(nsa_w_qg, ((0, 0), (0, 0), (0, qg_pad))).astype(BF16)
    w_o = jnp.pad(nsa_w_o.reshape(n_b, N_HEADS, 1, HEAD_DIM, D_MODEL),
                  ((0, 0), (0, 0), (1, 0), (0, 0), (0, 0))).reshape(n_b, N_HEADS * LANES, D_MODEL)
    R = CMP_LEN // CMP_STRIDE
    w1 = cmp_w1.reshape(2, R, CMP_STRIDE, HEAD_DIM, CMP_HID).transpose(0, 2, 3, 1, 4)
    w1 = w1.reshape(2, CMP_STRIDE, HEAD_DIM, R * CMP_HID)
    w1dup = jnp.concatenate([w1, w1], axis=2).astype(BF16)
    w1dup = w1dup.reshape(2, CMP_STRIDE // 2, 2 * LANES, R * CMP_HID)
    z = jnp.zeros((CMP_HID, HEAD_DIM), F32)
    w2p = jnp.stack([jnp.concatenate([cmp_w2[0], z], axis=1),
                     jnp.concatenate([z, cmp_w2[1]], axis=1)]).astype(BF16)
    ones = jnp.ones((HEAD_DIM,), F32)
    return dict(
        ret_w_in=ret_w_in.astype(BF16), ret_w_out=ret_w_out.astype(BF16),
        ffn_w_in=ffn_w_in.astype(BF16), ffn_w_out=ffn_w_out.astype(BF16),
        kv_w=kv_w.astype(BF16), w_qg=w_qg, w_o=w_o.astype(BF16), w_o_raw=nsa_w_o.astype(BF16),
        kv_gain=jnp.tile(kv_knorm, (1, 2)),
        cmp_gain=jnp.concatenate([kv_knorm[0], ones]).reshape(1, LANES),
        q_gain2=jnp.tile(nsa_qnorm, (1, 2)) * (HEAD_DIM ** -0.5 * math.log2(math.e)),
        score_bound=(1.05 * HEAD_DIM ** 0.5 * math.log2(math.e))
        * jnp.max(jnp.abs(nsa_qnorm), axis=1)[:, None] * jnp.max(jnp.abs(kv_knorm), axis=1)[None, :],
        w1dup=w1dup, w2p=w2p,
        bd=jnp.asarray(np.kron(np.eye(2), np.ones((HEAD_DIM, HEAD_DIM))), BF16),
    )


def _trunk(x, past_len, ret_s0, conv0, ctx, W, P):
    B, T, D = x.shape
    M = B * T
    depth = P["norm_mix"].shape[0]
    n_a = P["ret_w_in"].shape[0]
    pos = past_len + jnp.arange(T)
    cos_r, sin_r = _rope_tables(pos, RET_DK // 2)
    c32, s32 = _rope_tables(pos, HEAD_DIM // 2)
    cos_n = jnp.tile(c32, (1, 4))
    sin_n = jnp.concatenate([-s32, s32, -s32, s32], axis=1)
    lg = jnp.log1p(-jnp.exp2(-5.0 - jnp.arange(RET_HEADS, dtype=F32)))
    L = RET_CHUNK if T % RET_CHUNK == 0 else T
    gl = jnp.exp(L * lg)
    tm = min(512, M)
    tf = 1024 if T % 1024 == 0 else 512
    tb = min(512, T)
    tq = min(512, T)
    qb = next((c for c in (2 * Q_BLOCK, Q_BLOCK) if T % c == 0), T)
    tt = min(256, T)

    x2 = x.reshape(M, D)
    ret_states, conv_states = [], []
    for layer in range(depth):
        if layer == n_a:
            kvp, selp, winp = kv_project(x2.reshape(B, T, D), P["kv_norm"], W["kv_w"], W["kv_gain"],
                                         cos_n, sin_n, W["bd"], tq)
            nk2 = 2 * N_KV * HEAD_DIM
            if ctx is None:
                table = jnp.arange(M // PAGE, dtype=jnp.int32).reshape(B, T // PAGE)
                cmp_p = compress(kvp.reshape(M // PAGE, PAGE, 3 * nk2), table,
                                 W["w1dup"], W["pb"], W["w2p"], W["cmp_gain"])
                sel_p, win_p = selp, winp
                w_off = 0
            else:
                cache_cmp, cache_sel, cache_win, table = ctx
                n_pool = cache_cmp.shape[0]
                cmp_p = compress(cache_cmp.reshape(n_pool, PAGE, nk2), table,
                                 W["w1dup"], W["pb"], W["w2p"], W["cmp_gain"])
                sel_pages = cache_sel.reshape(n_pool, PAGE, nk2)
                win_rows = cache_win.reshape(B, cache_win.shape[1], nk2)
            n_sel = -(-(past_len + T) // SEL_BLOCK)
        h_norm = P["norm_mix"][layer]
        if layer < n_a:
            proj = norm_matmul(x2, h_norm, W["ret_w_in"][layer], min(1024, M), 1024)
            og, s_new = retention(proj.reshape(B, T, -1), ret_s0[layer], cos_r, sin_r, lg, gl, L, tb)
            ret_states.append(s_new)
            x2 = matmul_res(og.reshape(M, -1), W["ret_w_out"][layer], x2, tm)
        else:
            j = layer - n_a
            if qb % LANES == 0:
                qc, qr, gates = qg_project_t(x2.reshape(B, T, D), h_norm, W["w_qg"][j],
                                             W["q_gain2"][j:j + 1], cos_n, sin_n, W["bd"], tq, qb)
                shifts = W["score_bound"][j]
                args = (qc, qr, gates, cmp_p, sel_p, win_p)
                o = lax.cond(
                    jnp.max(shifts) <= 30.0,
                    lambda a: nsa_attend_fast(shifts, *a, T, qb, past_len, w_off, n_sel),
                    lambda a: nsa_attend_t(*a, T, qb, past_len, w_off, n_sel),
                    args)
                x2 = matmul_res(o.reshape(M, -1), W["w_o"][j], x2, tm)
            else:
                qc, qr, gates = qg_project(x2.reshape(B, T, D), h_norm, W["w_qg"][j],
                                           W["q_gain2"][j:j + 1], cos_n, sin_n, W["bd"], tq, qb)
                o = nsa_attend_dec(table, qc, qr, gates, cmp_p, sel_pages, kvp, win_rows,
                                   qb, past_len, n_sel)
                x2 = matmul_res(o, W["w_o_raw"][j], x2, tm)
        if T % tf == 0:
            act, tail = ffn_in(x2, P["norm_ffn"][layer], W["ffn_w_in"][layer], conv0[layer],
                               P["ffn_conv_w"][layer], P["ffn_conv_b"][layer], T, tf, 256)
            conv_states.append(tail)
        else:
            proj = norm_matmul(x2, P["norm_ffn"][layer], W["ffn_w_in"][layer], tm, 512)
            proj3 = proj.reshape(B, T, 2 * D_FF)
            act = ffn_mid(proj3, conv0[layer], P["ffn_conv_w"][layer], P["ffn_conv_b"][layer], tt)
            conv_states.append(proj3[:, T - 2:, :D_FF])
        x2 = matmul_res(act.reshape(M, D_FF), W["ffn_w_out"][layer], x2, tm)

    nk = N_KV * HEAD_DIM
    new_cmp = kvp[:, :, 0:2 * nk].reshape(B, T, 2, N_KV, HEAD_DIM)
    new_sel = kvp[:, :, 2 * nk:4 * nk].reshape(B, T, 2, N_KV, HEAD_DIM)
    new_win = kvp[:, :, 4 * nk:6 * nk].reshape(B, T, 2, N_KV, HEAD_DIM)
    return (x2.reshape(B, T, D), jnp.stack(ret_states), jnp.stack(conv_states),
            new_cmp, new_sel, new_win)


def kernel(x_prompt, x_sample, cache_cmp_kv, cache_sel_kv, cache_win_kv, state_ret, state_conv,
           page_table, norm_mix, norm_ffn, ret_w_in, ret_w_out, ffn_w_in, ffn_conv_w, ffn_conv_b,
           ffn_w_out, kv_norm, kv_w, kv_knorm, cmp_pos, cmp_w1, cmp_w2, nsa_w_qg, nsa_qnorm, nsa_w_o):
    W = _prep_weights(ret_w_in, ret_w_out, ffn_w_in, ffn_w_out, kv_w, kv_knorm, cmp_w1, cmp_w2,
                      nsa_w_qg, nsa_qnorm, nsa_w_o)
    W["pb"] = pos_bias(cmp_pos, cmp_w1)
    P = dict(norm_mix=norm_mix, norm_ffn=norm_ffn, ret_w_in=ret_w_in, ffn_conv_w=ffn_conv_w,
             ffn_conv_b=ffn_conv_b, kv_norm=kv_norm)
    depth = norm_mix.shape[0]
    n_a = ret_w_in.shape[0]
    B, T, _ = x_prompt.shape
    zero_ret = jnp.zeros((n_a, B, RET_HEADS, RET_DK, RET_DV), F32)
    zero_conv = jnp.zeros((depth, B, 2, D_FF), F32)
    y_p, ret_p, conv_p, cmp_p, sel_p, win_p = _trunk(x_prompt, 0, zero_ret, zero_conv, None, W, P)
    win_p = win_p[:, T - min(WINDOW, T):]

    db, ts, _ = x_sample.shape
    past_len = page_table.shape[1] * PAGE
    ctx = (cache_cmp_kv, cache_sel_kv, cache_win_kv, page_table)
    y_s, ret_s, conv_s, cmp_s, sel_s, win_new = _trunk(x_sample, past_len, state_ret, state_conv,
                                                        ctx, W, P)
    all_win = jnp.concatenate([cache_win_kv, win_new], axis=1)
    win_s = all_win[:, all_win.shape[1] - min(WINDOW, past_len + ts):]
    return (y_p, y_s, ret_p, ret_s, conv_p, conv_s, cmp_p, cmp_s, sel_p, sel_s, win_p, win_s)
```

```python
import functools
import math

import jax
import jax.numpy as jnp
import numpy as np
from jax import lax
from jax.experimental import pallas as pl
from jax.experimental.pallas import tpu as pltpu

F32 = jnp.float32
BF16 = jnp.bfloat16

D_MODEL = 1024
PAGE = 128
RET_HEADS = 4
RET_DK = 256
RET_DV = 512
RET_CHUNK = 128
N_HEADS = 16
N_KV = 4
HEAD_DIM = 64
CMP_LEN = 32
CMP_STRIDE = 16
CMP_HID = 128
SEL_BLOCK = 64
SEL_TOPK = 16
WINDOW = 512
Q_BLOCK = 128
D_FF = 2816
ROPE_THETA = 10000.0
EPS = 1e-6
NEG_INF = -1e30
TINY = 1e-30
SEL_FORCE = 1e6
SEL_NEG = -1e6

LANES = 128
KEY_TILE = 512
HALO = 16
VMEM_LIMIT = 48 * 1024 * 1024


def _cparams(sem):
    return pltpu.CompilerParams(dimension_semantics=sem, vmem_limit_bytes=VMEM_LIMIT)


def _nt_dot(a, b):
    return lax.dot_general(a, b, (((1,), (1,)), ((), ())), preferred_element_type=F32)


def _tn_dot(a, b):
    return lax.dot_general(a, b, (((0,), (0,)), ((), ())), preferred_element_type=F32)


def _gelu(x):
    return 0.5 * x * (1.0 + jnp.tanh(math.sqrt(2.0 / math.pi) * (x + 0.044715 * (x * x * x))))


def _rms_rows(x, g):
    r = lax.rsqrt(jnp.mean(x * x, axis=-1, keepdims=True) + EPS)
    return x * r * g


def _head_ms(x, bd):
    x2 = x * x
    hi = x2.astype(BF16)
    lo = (x2 - hi.astype(F32)).astype(BF16)
    s = jnp.dot(hi, bd, preferred_element_type=F32) + jnp.dot(lo, bd, preferred_element_type=F32)
    return s * (1.0 / HEAD_DIM)


def _rope64(x, cos, sin):
    lane = lax.broadcasted_iota(jnp.int32, x.shape, 1)
    sw = jnp.where((lane & 63) < 32, pltpu.roll(x, 96, 1), pltpu.roll(x, 32, 1))
    return x * cos + sw * sin


def _pack_pair(k2, v2, dtype):
    lane = lax.broadcasted_iota(jnp.int32, k2.shape, 1)
    lo = lane < HEAD_DIM
    even = jnp.where(lo, k2, pltpu.roll(v2, HEAD_DIM, 1)).astype(dtype)
    odd = jnp.where(lo, pltpu.roll(k2, HEAD_DIM, 1), v2).astype(dtype)
    return even, odd


def _norm_matmul_kernel(x_ref, g_ref, w_ref, o_ref, h_ref):
    @pl.when(pl.program_id(1) == 0)
    def _():
        h_ref[...] = _rms_rows(x_ref[...], g_ref[...]).astype(BF16)

    o_ref[...] = jnp.dot(h_ref[...], w_ref[...], preferred_element_type=F32).astype(o_ref.dtype)


def norm_matmul(x, g, w, tm, tn):
    M, D = x.shape
    N = w.shape[1]
    return pl.pallas_call(
        _norm_matmul_kernel,
        out_shape=jax.ShapeDtypeStruct((M, N), F32),
        grid=(M // tm, N // tn),
        in_specs=[pl.BlockSpec((tm, D), lambda i, j: (i, 0)),
                  pl.BlockSpec((1, D), lambda i, j: (0, 0)),
                  pl.BlockSpec((D, tn), lambda i, j: (0, j))],
        out_specs=pl.BlockSpec((tm, tn), lambda i, j: (i, j)),
        scratch_shapes=[pltpu.VMEM((tm, D), BF16)],
        compiler_params=_cparams(("parallel", "arbitrary")),
        name="norm_matmul",
    )(x, g.reshape(1, D), w)


def _matmul_res_kernel(a_ref, w_ref, r_ref, o_ref):
    o_ref[...] = r_ref[...] + jnp.dot(a_ref[...].astype(BF16), w_ref[...],
                                      preferred_element_type=F32)


def matmul_res(a, w, res, tm):
    M, K = a.shape
    N = w.shape[1]
    return pl.pallas_call(
        _matmul_res_kernel,
        out_shape=jax.ShapeDtypeStruct((M, N), F32),
        grid=(M // tm,),
        in_specs=[pl.BlockSpec((tm, K), lambda i: (i, 0)),
                  pl.BlockSpec((K, N), lambda i: (0, 0)),
                  pl.BlockSpec((tm, N), lambda i: (i, 0))],
        out_specs=pl.BlockSpec((tm, N), lambda i: (i, 0)),
        compiler_params=_cparams(("parallel",)),
        name="matmul_res",
    )(a, w, res)


def _retention_kernel(lg_ref, gl_ref, q_ref, k_ref, v_ref, g_ref, cos_ref, sin_ref, s0_ref,
                      o_ref, sout_ref, S_ref, *, L, n_chunk):
    h = pl.program_id(1)
    t = pl.program_id(2)
    lg = lg_ref[h]
    gl = gl_ref[h]

    @pl.when(t == 0)
    def _():
        S_ref[...] = s0_ref[0, 0]

    ii = lax.broadcasted_iota(jnp.int32, (L, L), 0)
    jj = lax.broadcasted_iota(jnp.int32, (L, L), 1)
    diff = (ii - jj).astype(F32)
    decay = jnp.where(diff >= 0, jnp.exp(jnp.maximum(diff, 0.0) * lg), 0.0)
    idx = lax.broadcasted_iota(jnp.int32, (L, 1), 0).astype(F32)
    q_dec = jnp.exp((idx + 1.0) * lg)
    k_dec = jnp.exp((L - 1.0 - idx) * lg)
    half = RET_DK // 2

    for c in range(n_chunk):
        rows = pl.ds(c * L, L)
        cos = cos_ref[rows, :]
        sin = sin_ref[rows, :]

        def rope(x):
            x1, x2 = x[:, :half], x[:, half:]
            return jnp.concatenate([x1 * cos - x2 * sin, x2 * cos + x1 * sin], axis=1)

        qr = rope(q_ref[0, rows, :])
        kr = rope(k_ref[0, rows, :]) * (RET_DK ** -0.5)
        qb = qr.astype(BF16)
        vb = v_ref[0, rows, :].astype(BF16)
        sc = _nt_dot(qb, kr.astype(BF16)) * decay
        S = S_ref[...]
        o = jnp.dot(sc.astype(BF16), vb, preferred_element_type=F32)
        o = o + jnp.dot(qb, S.astype(BF16), preferred_element_type=F32) * q_dec
        S_ref[...] = S * gl + _tn_dot((kr * k_dec).astype(BF16), vb)
        on = o * lax.rsqrt(jnp.mean(o * o, axis=-1, keepdims=True) + EPS)
        g = g_ref[0, rows, :]
        o_ref[0, rows, :] = (on * (g * jax.nn.sigmoid(g))).astype(o_ref.dtype)

    @pl.when(t == pl.num_programs(2) - 1)
    def _():
        sout_ref[0, 0] = S_ref[...]


def retention(proj, s0, cos, sin, lg, gl, L, tb):
    B, T, _ = proj.shape
    n_chunk = tb // L
    odt = BF16 if tb % 16 == 0 else F32
    kern = functools.partial(_retention_kernel, L=L, n_chunk=n_chunk)
    grid_spec = pltpu.PrefetchScalarGridSpec(
        num_scalar_prefetch=2,
        grid=(B, RET_HEADS, T // tb),
        in_specs=[
            pl.BlockSpec((1, tb, RET_DK), lambda b, h, t, *_: (b, t, h)),
            pl.BlockSpec((1, tb, RET_DK), lambda b, h, t, *_: (b, t, RET_HEADS + h)),
            pl.BlockSpec((1, tb, RET_DV), lambda b, h, t, *_: (b, t, RET_HEADS + h)),
            pl.BlockSpec((1, tb, RET_DV), lambda b, h, t, *_: (b, t, 2 * RET_HEADS + h)),
            pl.BlockSpec((tb, RET_DK // 2), lambda b, h, t, *_: (t, 0)),
            pl.BlockSpec((tb, RET_DK // 2), lambda b, h, t, *_: (t, 0)),
            pl.BlockSpec((1, 1, RET_DK, RET_DV), lambda b, h, t, *_: (b, h, 0, 0)),
        ],
        out_specs=[
            pl.BlockSpec((1, tb, RET_DV), lambda b, h, t, *_: (b, t, h)),
            pl.BlockSpec((1, 1, RET_DK, RET_DV), lambda b, h, t, *_: (b, h, 0, 0)),
        ],
        scratch_shapes=[pltpu.VMEM((RET_DK, RET_DV), F32)],
    )
    return pl.pallas_call(
        kern,
        out_shape=[jax.ShapeDtypeStruct((B, T, RET_HEADS * RET_DV), odt),
                   jax.ShapeDtypeStruct((B, RET_HEADS, RET_DK, RET_DV), F32)],
        grid_spec=grid_spec,
        compiler_params=_cparams(("parallel", "parallel", "arbitrary")),
        name="retention",
    )(lg, gl, proj, proj, proj, proj, cos, sin, s0)


def _ffn_mid_kernel(u_ref, gt_ref, halo_ref, cw_ref, cb_ref, o_ref):
    u = u_ref[0]
    hl = halo_ref[0, 0]
    row = lax.broadcasted_iota(jnp.int32, u.shape, 0)
    u1 = jnp.where(row == 0, hl[1:2], pltpu.roll(u, 1, 0))
    u2 = jnp.where(row == 0, hl[0:1], jnp.where(row == 1, hl[1:2], pltpu.roll(u, 2, 0)))
    c = cb_ref[...] + cw_ref[0:1] * u2
    c = c + cw_ref[1:2] * u1
    c = c + cw_ref[2:3] * u
    o_ref[0] = (_gelu(c) * gt_ref[0]).astype(o_ref.dtype)


def ffn_mid(proj, buf, conv_w, conv_b, tt):
    B, T, _ = proj.shape
    nt = T // tt
    if nt > 1:
        tails = proj[:, :, :D_FF].reshape(B, nt, tt, D_FF)[:, :-1, tt - 2:, :]
        halo = jnp.concatenate([buf[:, None], tails], axis=1)
    else:
        halo = buf[:, None]
    odt = BF16 if tt % 16 == 0 else F32
    return pl.pallas_call(
        _ffn_mid_kernel,
        out_shape=jax.ShapeDtypeStruct((B, T, D_FF), odt),
        grid=(B, nt),
        in_specs=[pl.BlockSpec((1, tt, D_FF), lambda b, t: (b, t, 0)),
                  pl.BlockSpec((1, tt, D_FF), lambda b, t: (b, t, 1)),
                  pl.BlockSpec((1, 1, 2, D_FF), lambda b, t: (b, t, 0, 0)),
                  pl.BlockSpec((3, D_FF), lambda b, t: (0, 0)),
                  pl.BlockSpec((1, D_FF), lambda b, t: (0, 0))],
        out_specs=pl.BlockSpec((1, tt, D_FF), lambda b, t: (b, t, 0)),
        compiler_params=_cparams(("parallel", "parallel")),
        name="ffn_mid",
    )(proj, proj, halo, conv_w, conv_b.reshape(1, D_FF))


def _ffn_in_kernel(x_ref, xh_ref, g_ref, w_ref, buf_ref, cw_ref, cb_ref,
                   act_ref, tail_ref, *, tiles_per_seq, tn):
    seq_start = (pl.program_id(0) % tiles_per_seq) == 0
    h = _rms_rows(x_ref[...], g_ref[...]).astype(BF16)
    hh = _rms_rows(xh_ref[...], g_ref[...]).astype(BF16)
    tm = h.shape[0]
    row = lax.broadcasted_iota(jnp.int32, (tm, tn), 0)
    for j in range(D_FF // tn):
        cols = slice(j * tn, (j + 1) * tn)
        wu = w_ref[:, cols]
        u = jnp.dot(h, wu, preferred_element_type=F32)
        gt = jnp.dot(h, w_ref[:, D_FF + j * tn:D_FF + (j + 1) * tn], preferred_element_type=F32)
        uh = jnp.dot(hh, wu, preferred_element_type=F32)
        hl = jnp.where(seq_start, buf_ref[0, :, cols], uh[HALO - 2:, :])
        u1 = jnp.where(row == 0, hl[1:2], pltpu.roll(u, 1, 0))
        u2 = jnp.where(row == 0, hl[0:1], jnp.where(row == 1, hl[1:2], pltpu.roll(u, 2, 0)))
        c = cb_ref[:, cols] + cw_ref[0:1, cols] * u2
        c = c + cw_ref[1:2, cols] * u1
        c = c + cw_ref[2:3, cols] * u
        act_ref[:, cols] = (_gelu(c) * gt).astype(act_ref.dtype)
        tail_ref[0, :, cols] = u[tm - 2:, :]


def ffn_in(x, g, w, buf, conv_w, conv_b, T, tm, tn):
    M, D = x.shape
    tiles_per_seq = T // tm
    kern = functools.partial(_ffn_in_kernel, tiles_per_seq=tiles_per_seq, tn=tn)
    act, tails = pl.pallas_call(
        kern,
        out_shape=[jax.ShapeDtypeStruct((M, D_FF), BF16),
                   jax.ShapeDtypeStruct((M // tm, 2, D_FF), F32)],
        grid=(M // tm,),
        in_specs=[pl.BlockSpec((tm, D), lambda i: (i, 0)),
                  pl.BlockSpec((HALO, D), lambda i: (jnp.maximum(i * (tm // HALO) - 1, 0), 0)),
                  pl.BlockSpec((1, D), lambda i: (0, 0)),
                  pl.BlockSpec((D, 2 * D_FF), lambda i: (0, 0), pipeline_mode=pl.Buffered(1)),
                  pl.BlockSpec((1, 2, D_FF), lambda i: (i // tiles_per_seq, 0, 0)),
                  pl.BlockSpec((3, D_FF), lambda i: (0, 0)),
                  pl.BlockSpec((1, D_FF), lambda i: (0, 0))],
        out_specs=[pl.BlockSpec((tm, D_FF), lambda i: (i, 0)),
                   pl.BlockSpec((1, 2, D_FF), lambda i: (i, 0, 0))],
        compiler_params=_cparams(("parallel",)),
        name="ffn_in",
    )(x, x, g.reshape(1, D), w, buf, conv_w, conv_b.reshape(1, D_FF))
    return act, tails[tiles_per_seq - 1::tiles_per_seq]


def _kv_kernel(x_ref, g_ref, w_ref, gain_ref, cos_ref, sin_ref, bd_ref,
               kv_ref, selp_ref, winp_ref):
    h = _rms_rows(x_ref[0], g_ref[...]).astype(BF16)
    y = jnp.dot(h, w_ref[...], preferred_element_type=F32)
    cos = cos_ref[...]
    sin = sin_ref[...]
    bd = bd_ref[...]
    nk = N_KV * HEAD_DIM
    kv_ref[0, :, 0:2 * nk] = y[:, 0:2 * nk]
    for br, pack_ref in enumerate((selp_ref, winp_ref)):
        base = 2 * nk * (br + 1)
        kv_ref[0, :, base + nk:base + 2 * nk] = y[:, base + nk:base + 2 * nk]
        for p in range(2):
            kx = y[:, base + LANES * p:base + LANES * (p + 1)]
            kn = kx * lax.rsqrt(_head_ms(kx, bd) + EPS) * gain_ref[br + 1:br + 2, :]
            kr = _rope64(kn, cos, sin)
            kv_ref[0, :, base + LANES * p:base + LANES * (p + 1)] = kr
            vx = y[:, base + nk + LANES * p:base + nk + LANES * (p + 1)]
            even, odd = _pack_pair(kr, vx, pack_ref.dtype)
            pack_ref[0, 2 * p] = even
            pack_ref[0, 2 * p + 1] = odd


def kv_project(x, g, w, gains, cos, sin, bd, tm):
    B, T, D = x.shape
    N = w.shape[1]
    pdt = BF16 if tm % 16 == 0 else F32
    return pl.pallas_call(
        _kv_kernel,
        out_shape=[jax.ShapeDtypeStruct((B, T, N), F32),
                   jax.ShapeDtypeStruct((B, N_KV, T, LANES), pdt),
                   jax.ShapeDtypeStruct((B, N_KV, T, LANES), pdt)],
        grid=(B, T // tm),
        in_specs=[pl.BlockSpec((1, tm, D), lambda b, t: (b, t, 0)),
                  pl.BlockSpec((1, D), lambda b, t: (0, 0)),
                  pl.BlockSpec((D, N), lambda b, t: (0, 0)),
                  pl.BlockSpec((3, LANES), lambda b, t: (0, 0)),
                  pl.BlockSpec((tm, LANES), lambda b, t: (t, 0)),
                  pl.BlockSpec((tm, LANES), lambda b, t: (t, 0)),
                  pl.BlockSpec((LANES, LANES), lambda b, t: (0, 0))],
        out_specs=[pl.BlockSpec((1, tm, N), lambda b, t: (b, t, 0)),
                   pl.BlockSpec((1, N_KV, tm, LANES), lambda b, t: (b, 0, t, 0)),
                   pl.BlockSpec((1, N_KV, tm, LANES), lambda b, t: (b, 0, t, 0))],
        compiler_params=_cparams(("parallel", "parallel")),
        name="kv_project",
    )(x, g.reshape(1, D), w, gains, cos, sin, bd)


def _qg_kernel(x_ref, g_ref, w_ref, gain_ref, cos_ref, sin_ref, bd_ref,
               qc_ref, qr_ref, gate_ref, *, qb, n_qb):
    h = _rms_rows(x_ref[0], g_ref[...]).astype(BF16)
    y = jnp.dot(h, w_ref[...], preferred_element_type=F32)
    cos = cos_ref[...]
    sin = sin_ref[...]
    bd = bd_ref[...]
    nq = N_HEADS * HEAD_DIM
    gate_ref[0] = jax.nn.sigmoid(y[:, nq:nq + LANES])
    lane = lax.broadcasted_iota(jnp.int32, (y.shape[0], LANES), 1)
    lo = lane < HEAD_DIM
    group = N_HEADS // N_KV
    for p in range(N_HEADS // 2):
        qx = y[:, LANES * p:LANES * (p + 1)]
        qn = qx * lax.rsqrt(_head_ms(qx, bd) + EPS) * gain_ref[...]
        qr = _rope64(qn, cos, sin)
        for src, dst in ((qn, qc_ref), (qr, qr_ref)):
            for par in range(2):
                hd = 2 * p + par
                kvh, gi = hd // group, hd % group
                v = src if par == 0 else pltpu.roll(src, HEAD_DIM, 1)
                v = jnp.where(lo, v, 0.0).astype(dst.dtype)
                for j in range(n_qb):
                    dst[0, kvh, j, gi * qb:(gi + 1) * qb, :] = v[j * qb:(j + 1) * qb, :]


def qg_project(x, g, w, gain, cos, sin, bd, tm, qb):
    B, T, D = x.shape
    N = w.shape[1]
    n_qb = tm // qb
    group = N_HEADS // N_KV
    qdt = BF16 if qb % 16 == 0 else F32
    kern = functools.partial(_qg_kernel, qb=qb, n_qb=n_qb)
    qshape = jax.ShapeDtypeStruct((B, N_KV, T // qb, group * qb, LANES), qdt)
    qspec = pl.BlockSpec((1, N_KV, n_qb, group * qb, LANES), lambda b, t: (b, 0, t, 0, 0))
    return pl.pallas_call(
        kern,
        out_shape=[qshape, qshape, jax.ShapeDtypeStruct((B, T, LANES), F32)],
        grid=(B, T // tm),
        in_specs=[pl.BlockSpec((1, tm, D), lambda b, t: (b, t, 0)),
                  pl.BlockSpec((1, D), lambda b, t: (0, 0)),
                  pl.BlockSpec((D, N), lambda b, t: (0, 0)),
                  pl.BlockSpec((1, LANES), lambda b, t: (0, 0)),
                  pl.BlockSpec((tm, LANES), lambda b, t: (t, 0)),
                  pl.BlockSpec((tm, LANES), lambda b, t: (t, 0)),
                  pl.BlockSpec((LANES, LANES), lambda b, t: (0, 0))],
        out_specs=[qspec, qspec, pl.BlockSpec((1, tm, LANES), lambda b, t: (b, t, 0))],
        compiler_params=_cparams(("parallel", "parallel")),
        name="qg_project",
    )(x, g.reshape(1, D), w, gain, cos, sin, bd)


def _pos_bias_kernel(p_ref, w_ref, o_ref):
    o_ref[0] = jnp.dot(p_ref[0].astype(BF16), w_ref[0].astype(BF16), preferred_element_type=F32)


def pos_bias(cmp_pos, cmp_w1):
    K = CMP_LEN * HEAD_DIM
    p = jnp.broadcast_to(cmp_pos.reshape(2, 1, K), (2, 8, K))
    out = pl.pallas_call(
        _pos_bias_kernel,
        out_shape=jax.ShapeDtypeStruct((2, 8, CMP_HID), F32),
        grid=(2,),
        in_specs=[pl.BlockSpec((1, 8, K), lambda c: (c, 0, 0)),
                  pl.BlockSpec((1, K, CMP_HID), lambda c: (c, 0, 0))],
        out_specs=pl.BlockSpec((1, 8, CMP_HID), lambda c: (c, 0, 0)),
        name="pos_bias",
    )(p, cmp_w1)
    return out[:, 0, :]


def _compress_kernel(pt_ref, *refs, n_pg):
    pages = refs[:n_pg]
    w1_ref, pb_ref, w2_ref, gain_ref, out_ref, carry_ref, slab_ref = refs[n_pg:]
    g = pl.program_id(1)

    @pl.when(g == 0)
    def _():
        carry_ref[...] = jnp.zeros_like(carry_ref)

    n = n_pg * (PAGE // CMP_STRIDE)
    lo = lax.broadcasted_iota(jnp.int32, (n, LANES), 1) < HEAD_DIM
    row = lax.broadcasted_iota(jnp.int32, (N_KV * n, LANES), 0)
    res = jnp.zeros((N_KV * n, LANES), F32)
    for c in range(2):
        for pair in range(N_KV // 2):
            col = c * N_KV * HEAD_DIM + pair * LANES
            for i in range(n_pg):
                slab_ref[pair, PAGE * i:PAGE * (i + 1), :] = pages[i][0, :, col:col + LANES]
        acc = jnp.zeros((N_KV * n, 2 * CMP_HID), F32)
        for s in range(0, CMP_STRIDE, 2):
            parts = []
            for k in range(N_KV):
                pair, par = divmod(k, 2)
                keep = lo if par == 0 else jnp.logical_not(lo)
                parts.append(jnp.concatenate(
                    [jnp.where(keep, slab_ref[pair, pl.ds(s + i, n, stride=CMP_STRIDE), :], 0.0)
                     for i in range(2)], axis=1))
            xm = jnp.concatenate(parts, axis=0).astype(BF16)
            acc = acc + jnp.dot(xm, w1_ref[c, s // 2], preferred_element_type=F32)
        p0 = acc[:, :CMP_HID]
        p1 = acc[:, CMP_HID:]
        prev = pltpu.roll(p0, 1, 0)
        for k in range(N_KV):
            prev = jnp.where(row == k * n, carry_ref[2 * k + c, 7:8, :], prev)
        for k in range(N_KV):
            carry_ref[2 * k + c] = p0[(k + 1) * n - 8:(k + 1) * n, :]
        hid = _gelu(prev + p1 + pb_ref[c:c + 1, :]).astype(BF16)
        res = res + jnp.dot(hid, w2_ref[c], preferred_element_type=F32)
    for k in range(N_KV):
        r = res[k * n:(k + 1) * n, :]
        ms = jnp.sum(jnp.where(lo, r * r, 0.0), axis=-1, keepdims=True) * (1.0 / HEAD_DIM)
        kn = r * lax.rsqrt(ms + EPS) * gain_ref[...]
        out_ref[0, k] = jnp.where(lo, kn, r).astype(out_ref.dtype)


def _page_index(b, g, pt_ref, *, i, n_pg):
    return (pt_ref[b, g * n_pg + i], 0, 0)


def compress(pages_arr, table, w1dup, pb, w2p, gain):
    B, n_pages = table.shape
    n_pg = 16 if n_pages % 16 == 0 else n_pages
    n = n_pg * (PAGE // CMP_STRIDE)
    n_sub = n_pages * (PAGE // CMP_STRIDE)
    width = 2 * N_KV * HEAD_DIM
    in_specs = [pl.BlockSpec((1, PAGE, width), functools.partial(_page_index, i=i, n_pg=n_pg))
                for i in range(n_pg)]
    in_specs += [pl.BlockSpec(w1dup.shape, lambda b, g, pt: (0, 0, 0, 0)),
                 pl.BlockSpec(pb.shape, lambda b, g, pt: (0, 0)),
                 pl.BlockSpec(w2p.shape, lambda b, g, pt: (0, 0, 0)),
                 pl.BlockSpec((1, LANES), lambda b, g, pt: (0, 0))]
    grid_spec = pltpu.PrefetchScalarGridSpec(
        num_scalar_prefetch=1,
        grid=(B, n_pages // n_pg),
        in_specs=in_specs,
        out_specs=pl.BlockSpec((1, N_KV, n, LANES), lambda b, g, pt: (b, 0, g, 0)),
        scratch_shapes=[pltpu.VMEM((2 * N_KV, 8, CMP_HID), F32),
                        pltpu.VMEM((N_KV // 2, n_pg * PAGE, LANES), F32)],
    )
    return pl.pallas_call(
        functools.partial(_compress_kernel, n_pg=n_pg),
        out_shape=jax.ShapeDtypeStruct((B, N_KV, n_sub, LANES), BF16),
        grid_spec=grid_spec,
        compiler_params=_cparams(("parallel", "arbitrary")),
        name="compress",
    )(table, *([pages_arr] * n_pg), w1dup, pb, w2p, gain)


def _overlap_map(n_cmp_rows, n_selp):
    m = np.arange(n_cmp_rows)[None, :]
    s = np.arange(n_selp)[:, None]
    c0 = (m - 1) * CMP_STRIDE
    ov = np.minimum(c0 + CMP_LEN, s * SEL_BLOCK + SEL_BLOCK) - np.maximum(c0, s * SEL_BLOCK)
    w = np.maximum(ov, 0).astype(np.float32) / CMP_LEN
    w[:, 0] = 0.0
    return jnp.asarray(w, BF16)


def _qgt_kernel(x_ref, g_ref, w_ref, gain_ref, cos_ref, sin_ref, bd_ref,
                qc_ref, qr_ref, gate_ref, *, qb, n_qb):
    h = _rms_rows(x_ref[0], g_ref[...]).astype(BF16)
    y = jnp.dot(h, w_ref[...], preferred_element_type=F32)
    cos = cos_ref[...]
    sin = sin_ref[...]
    bd = bd_ref[...]
    nq = N_HEADS * HEAD_DIM
    group = N_HEADS // N_KV
    gate_t = jax.nn.sigmoid(y[:, nq:nq + LANES]).T
    for j in range(n_qb):
        gate_ref[0, j] = gate_t[:, j * qb:(j + 1) * qb]
    pad = jnp.zeros((N_KV, n_qb, HEAD_DIM, group * qb), qc_ref.dtype)
    qc_ref[0, :, :, HEAD_DIM:, :] = pad
    qr_ref[0, :, :, HEAD_DIM:, :] = pad
    for p in range(N_HEADS // 2):
        qx = y[:, LANES * p:LANES * (p + 1)]
        qn = qx * lax.rsqrt(_head_ms(qx, bd) + EPS) * gain_ref[...]
        qr = _rope64(qn, cos, sin)
        for src, dst in ((qn, qc_ref), (qr, qr_ref)):
            st = src.T.astype(dst.dtype)
            for par in range(2):
                kvh, gi = divmod(2 * p + par, group)
                for j in range(n_qb):
                    dst[0, kvh, j, 0:HEAD_DIM, gi * qb:(gi + 1) * qb] = (
                        st[par * HEAD_DIM:(par + 1) * HEAD_DIM, j * qb:(j + 1) * qb])


def qg_project_t(x, g, w, gain, cos, sin, bd, tm, qb):
    B, T, D = x.shape
    N = w.shape[1]
    n_qb = tm // qb
    group = N_HEADS // N_KV
    kern = functools.partial(_qgt_kernel, qb=qb, n_qb=n_qb)
    qshape = jax.ShapeDtypeStruct((B, N_KV, T // qb, LANES, group * qb), BF16)
    qspec = pl.BlockSpec((1, N_KV, n_qb, LANES, group * qb), lambda b, t: (b, 0, t, 0, 0))
    return pl.pallas_call(
        kern,
        out_shape=[qshape, qshape, jax.ShapeDtypeStruct((B, T // qb, LANES, qb), F32)],
        grid=(B, T // tm),
        in_specs=[pl.BlockSpec((1, tm, D), lambda b, t: (b, t, 0)),
                  pl.BlockSpec((1, D), lambda b, t: (0, 0)),
                  pl.BlockSpec((D, N), lambda b, t: (0, 0)),
                  pl.BlockSpec((1, LANES), lambda b, t: (0, 0)),
                  pl.BlockSpec((tm, LANES), lambda b, t: (t, 0)),
                  pl.BlockSpec((tm, LANES), lambda b, t: (t, 0)),
                  pl.BlockSpec((LANES, LANES), lambda b, t: (0, 0))],
        out_specs=[qspec, qspec, pl.BlockSpec((1, n_qb, LANES, qb), lambda b, t: (b, t, 0, 0))],
        compiler_params=_cparams(("parallel", "parallel")),
        name="qg_project_t",
    )(x, g.reshape(1, D), w, gain, cos, sin, bd)


def _nsa_t_kernel(qc_ref, qr_ref, gate_ref, cmp_ref, sel_ref, win_ref, wmap_ref, o_ref,
                  selneg_ref, m_ref, l_ref, acc_ref,
                  *, qb, n_sel, n_selp, n_cmp, q0, w_off, w_rows, l_win):
    qi = pl.program_id(1)
    group = N_HEADS // N_KV
    R = group * qb
    blocks_per_tile = KEY_TILE // SEL_BLOCK
    q_lo = q0 + qi * qb
    qpos_q = q_lo + lax.broadcasted_iota(jnp.int32, (1, qb), 1)
    n_kt = (q_lo + qb - 1) // KEY_TILE + 1
    w_start = pl.multiple_of(jnp.clip(q_lo - WINDOW - w_off, 0, l_win - w_rows), LANES)
    gate = gate_ref[0, 0]

    def lanes4(a):
        return jnp.concatenate([a] * group, axis=1)

    m_idx = lax.broadcasted_iota(jnp.int32, (n_cmp, qb), 0)
    cvalid = (m_idx >= 1) & ((m_idx - 1) * CMP_STRIDE + CMP_LEN - 1 <= qpos_q)
    cbias = lanes4(jnp.where(cvalid, 0.0, NEG_INF))
    any_c = lanes4(qpos_q >= CMP_LEN - 1)
    blk = lax.broadcasted_iota(jnp.int32, (n_selp, qb), 0)
    blk_f = blk.astype(F32)
    cur = qpos_q >> 6
    forced = (blk == 0) | (blk == cur) | (blk == cur - 1)
    reach = blk * SEL_BLOCK <= qpos_q
    real = blk < n_sel
    wpos = w_off + w_start + lax.broadcasted_iota(jnp.int32, (w_rows, qb), 0)
    wbias = lanes4(jnp.where((wpos <= qpos_q) & (wpos > qpos_q - WINDOW) & (wpos >= 0), 0.0, NEG_INF))
    row_t = lax.broadcasted_iota(jnp.int32, (KEY_TILE, qb), 0)

    for k in range(N_KV):
        ckv = cmp_ref[0, k]
        s = jnp.dot(ckv, qc_ref[0, k, 0], preferred_element_type=F32) + cbias
        e = jnp.exp2(s - jnp.max(s, axis=0, keepdims=True))
        den = jnp.maximum(jnp.sum(e, axis=0, keepdims=True), TINY)
        p = e * jnp.where(any_c, 1.0 / den, 0.0)
        oc = _tn_dot(ckv, p.astype(BF16))
        psum = p[:, 0:qb]
        for gi in range(1, group):
            psum = psum + p[:, gi * qb:(gi + 1) * qb]
        hi = psum.astype(BF16)
        lo = (psum - hi.astype(F32)).astype(BF16)
        imp = (jnp.dot(wmap_ref[...], hi, preferred_element_type=F32)
               + jnp.dot(wmap_ref[...], lo, preferred_element_type=F32))
        v = jnp.where(forced, SEL_FORCE, jnp.where(reach, imp, SEL_NEG))
        v = jnp.where(real, v, -jnp.inf)

        def pick_one(_, carry):
            v, sel = carry
            m = jnp.max(v, axis=0, keepdims=True)
            first = jnp.min(jnp.where(v == m, blk_f, float(n_selp)), axis=0, keepdims=True)
            pick = blk_f == first
            sel = jnp.where(pick & (m > 0.5 * SEL_NEG), 0.0, sel)
            return jnp.where(pick, -jnp.inf, v), sel

        _, sel = lax.fori_loop(0, SEL_TOPK, pick_one, (v, jnp.full((n_selp, qb), NEG_INF, F32)))
        selneg_ref[...] = sel

        qr = qr_ref[0, k, 0]
        m_ref[...] = jnp.full(m_ref.shape, NEG_INF, F32)
        l_ref[...] = jnp.zeros(l_ref.shape, F32)
        acc_ref[...] = jnp.zeros(acc_ref.shape, F32)

        def tile(kt, carry):
            start = pl.multiple_of(kt * KEY_TILE, KEY_TILE)
            kv = sel_ref[0, k, pl.ds(start, KEY_TILE), :]
            s = jnp.dot(kv, qr, preferred_element_type=F32)
            pieces = [jnp.broadcast_to(selneg_ref[pl.ds(kt * blocks_per_tile + j, 1), :], (SEL_BLOCK, qb))
                      for j in range(blocks_per_tile)]
            bias = jnp.concatenate(pieces, axis=0) + jnp.where(start + row_t <= qpos_q, 0.0, NEG_INF)
            s = s + lanes4(bias)
            m_old = m_ref[...]
            m_new = jnp.maximum(m_old, jnp.max(s, axis=0, keepdims=True))
            alpha = jnp.exp2(m_old - m_new)
            p = jnp.exp2(s - m_new)
            l_ref[...] = alpha * l_ref[...] + jnp.sum(p, axis=0, keepdims=True)
            acc_ref[...] = alpha * acc_ref[...] + _tn_dot(kv, p.astype(BF16))
            m_ref[...] = m_new
            return carry

        lax.fori_loop(0, n_kt, tile, 0)
        o_s = acc_ref[...] * (1.0 / jnp.maximum(l_ref[...], TINY))

        wkv = win_ref[0, k, pl.ds(w_start, w_rows), :]
        s = jnp.dot(wkv, qr, preferred_element_type=F32) + wbias
        e = jnp.exp2(s - jnp.max(s, axis=0, keepdims=True))
        den = jnp.maximum(jnp.sum(e, axis=0, keepdims=True), TINY)
        o_w = _tn_dot(wkv, e.astype(BF16)) * (1.0 / den)

        for gi in range(group):
            hd = k * group + gi
            sl = slice(gi * qb, (gi + 1) * qb)
            o = gate[3 * hd:3 * hd + 1, :] * oc[:, sl]
            o = o + gate[3 * hd + 1:3 * hd + 2, :] * o_s[:, sl]
            o = o + gate[3 * hd + 2:3 * hd + 3, :] * o_w[:, sl]
            o_ref[0, :, LANES * hd:LANES * (hd + 1)] = o.T.astype(o_ref.dtype)


def nsa_attend_t(qc, qr, gates, cmp_p, sel_p, win_p, T, qb, q0, w_off, n_sel):
    B = qc.shape[0]
    group = N_HEADS // N_KV
    R = group * qb
    n_cmp = cmp_p.shape[2]
    n_selp = -(-n_sel // LANES) * LANES
    l_sel = sel_p.shape[2]
    l_win = win_p.shape[2]
    w_rows = WINDOW + max(qb, LANES)
    assert l_win >= w_rows and l_sel >= ((q0 + T - 1) // KEY_TILE + 1) * KEY_TILE
    wmap = _overlap_map(n_cmp, n_selp)
    kern = functools.partial(_nsa_t_kernel, qb=qb, n_sel=n_sel, n_selp=n_selp, n_cmp=n_cmp,
                             q0=q0, w_off=w_off, w_rows=w_rows, l_win=l_win)
    qspec = pl.BlockSpec((1, N_KV, 1, LANES, R), lambda b, i: (b, 0, i, 0, 0))
    resident = dict(pipeline_mode=pl.Buffered(1))
    return pl.pallas_call(
        kern,
        out_shape=jax.ShapeDtypeStruct((B, T, N_HEADS * LANES), BF16),
        grid=(B, T // qb),
        in_specs=[
            qspec, qspec,
            pl.BlockSpec((1, 1, LANES, qb), lambda b, i: (b, i, 0, 0)),
            pl.BlockSpec((1, N_KV, n_cmp, LANES), lambda b, i: (b, 0, 0, 0)),
            pl.BlockSpec((1, N_KV, l_sel, LANES), lambda b, i: (b, 0, 0, 0), **resident),
            pl.BlockSpec((1, N_KV, l_win, LANES), lambda b, i: (b, 0, 0, 0), **resident),
            pl.BlockSpec((n_selp, n_cmp), lambda b, i: (0, 0)),
        ],
        out_specs=pl.BlockSpec((1, qb, N_HEADS * LANES), lambda b, i: (b, i, 0)),
        scratch_shapes=[
            pltpu.VMEM((n_selp, qb), F32),
            pltpu.VMEM((1, R), F32), pltpu.VMEM((1, R), F32), pltpu.VMEM((LANES, R), F32),
        ],
        compiler_params=_cparams(("parallel", "arbitrary")),
        name="nsa_attend_t",
    )(qc, qr, gates, cmp_p, sel_p, win_p, wmap)


def _nsa_fast_kernel(shift_ref, qc_ref, qr_ref, gate_ref, cmp_ref, sel_ref, win_ref, wmap_ref, hot_ref,
                     o_ref, qaug_ref, oc_ref, ow_ref, l_ref, acc_ref,
                     *, qb, n_sel, n_selp, n_cmp, q0, w_off, w_rows, l_win):
    qi = pl.program_id(1)
    group = N_HEADS // N_KV
    R = group * qb
    q_lo = q0 + qi * qb
    qpos_q = q_lo + lax.broadcasted_iota(jnp.int32, (1, qb), 1)
    n_kt = (q_lo + qb - 1) // KEY_TILE + 1
    w_start = pl.multiple_of(jnp.clip(q_lo - WINDOW - w_off, 0, l_win - w_rows), LANES)
    shift_c = shift_ref[0]
    shift_s = shift_ref[1]
    shift_w = shift_ref[2]

    def lanes4(a):
        return jnp.concatenate([a] * group, axis=1)

    m_idx = lax.broadcasted_iota(jnp.int32, (n_cmp, qb), 0)
    cvalid = (m_idx >= 1) & ((m_idx - 1) * CMP_STRIDE + CMP_LEN - 1 <= qpos_q)
    cbias = lanes4(jnp.where(cvalid, -shift_c, NEG_INF))
    any_c = lanes4(qpos_q >= CMP_LEN - 1)
    blk = lax.broadcasted_iota(jnp.int32, (n_selp, qb), 0)
    blk_f = blk.astype(F32)
    cur = qpos_q >> 6
    forced = (blk == 0) | (blk == cur) | (blk == cur - 1)
    reach = blk * SEL_BLOCK <= qpos_q
    real = blk < n_sel

    imps = []
    for k in range(N_KV):
        ckv = cmp_ref[0, k]
        e = jnp.exp2(jnp.dot(ckv, qc_ref[0, k, 0], preferred_element_type=F32) + cbias)
        den = jnp.maximum(jnp.sum(e, axis=0, keepdims=True), TINY)
        p = e * jnp.where(any_c, 1.0 / den, 0.0)
        oc_ref[k] = _tn_dot(ckv[:, HEAD_DIM:], p.astype(BF16))
        psum = p[:, 0:qb]
        for gi in range(1, group):
            psum = psum + p[:, gi * qb:(gi + 1) * qb]
        hi = psum.astype(BF16)
        lo = (psum - hi.astype(F32)).astype(BF16)
        imp = (jnp.dot(wmap_ref[...], hi, preferred_element_type=F32)
               + jnp.dot(wmap_ref[...], lo, preferred_element_type=F32))
        v = jnp.where(forced, SEL_FORCE, jnp.where(reach, imp, SEL_NEG))
        imps.append(jnp.where(real, v, -jnp.inf))

    def pick_one(_, vs):
        out = []
        for v in vs:
            m = jnp.max(v, axis=0, keepdims=True)
            first = jnp.min(jnp.where(v == m, blk_f, float(n_selp)), axis=0, keepdims=True)
            out.append(jnp.where(blk_f == first, -jnp.inf, v))
        return tuple(out)

    wpos = w_off + w_start + lax.broadcasted_iota(jnp.int32, (w_rows, qb), 0)
    wvalid = (wpos <= qpos_q) & (wpos > qpos_q - WINDOW) & (wpos >= 0)
    wbias = lanes4(jnp.where(wvalid, -shift_w, NEG_INF))
    for k in range(N_KV):
        wkv = win_ref[0, k, pl.ds(w_start, w_rows), :]
        e = jnp.exp2(jnp.dot(wkv, qr_ref[0, k, 0], preferred_element_type=F32) + wbias)
        den = jnp.maximum(jnp.sum(e, axis=0, keepdims=True), TINY)
        ow_ref[k] = _tn_dot(wkv[:, HEAD_DIM:], e.astype(BF16)) * (1.0 / den)

    n_forced = 3
    marked = lax.fori_loop(0, SEL_TOPK - n_forced, pick_one,
                           tuple(jnp.where(forced, -jnp.inf, v) for v in imps), unroll=True)
    for k in range(N_KV):
        sel = jnp.where((marked[k] == -jnp.inf) & (imps[k] > 0.5 * SEL_NEG), -shift_s, NEG_INF)
        qaug_ref[k, 0:LANES, :] = qr_ref[0, k, 0]
        qaug_ref[k, LANES:, :] = lanes4(sel).astype(BF16)

    l_ref[...] = jnp.zeros(l_ref.shape, F32)
    acc_ref[...] = jnp.zeros(acc_ref.shape, F32)

    def tile(start, rows, causal_bias):
        hot = hot_ref[pl.ds(start, rows), :]
        for k in range(N_KV):
            kv = sel_ref[0, k, pl.ds(start, rows), :]
            s = jnp.dot(jnp.concatenate([kv, hot], axis=1), qaug_ref[k], preferred_element_type=F32)
            if causal_bias is not None:
                s = s + causal_bias
            p = jnp.exp2(s)
            l_ref[k] += jnp.sum(p.reshape(rows // 8, 8, R), axis=0)
            acc_ref[k] += _tn_dot(kv[:, HEAD_DIM:], p.astype(BF16))

    def double_tile(i, carry):
        tile(pl.multiple_of(i * 2 * KEY_TILE, 2 * KEY_TILE), 2 * KEY_TILE, None)
        return carry

    n_below = n_kt - 1
    lax.fori_loop(0, n_below // 2, double_tile, 0)

    @pl.when(n_below % 2 == 1)
    def _():
        tile(pl.multiple_of((n_below - 1) * KEY_TILE, KEY_TILE), KEY_TILE, None)

    d_start = pl.multiple_of(n_below * KEY_TILE, KEY_TILE)
    row_t = lax.broadcasted_iota(jnp.int32, (KEY_TILE, qb), 0)
    tile(d_start, KEY_TILE, lanes4(jnp.where(d_start + row_t <= qpos_q, 0.0, NEG_INF)))

    gate = gate_ref[0, 0]
    for k in range(N_KV):
        o_w = ow_ref[k]
        l_s = jnp.maximum(jnp.sum(l_ref[k], axis=0, keepdims=True), TINY)
        o_s = acc_ref[k] * (1.0 / l_s)
        oc = oc_ref[k]
        heads = []
        for gi in range(group):
            hd = k * group + gi
            sl = slice(gi * qb, (gi + 1) * qb)
            o = gate[3 * hd:3 * hd + 1, :] * oc[:, sl]
            o = o + gate[3 * hd + 1:3 * hd + 2, :] * o_s[:, sl]
            heads.append(o + gate[3 * hd + 2:3 * hd + 3, :] * o_w[:, sl])
        for pr in range(group // 2):
            o2 = jnp.concatenate(heads[2 * pr:2 * pr + 2], axis=0)
            col = (k * group + 2 * pr) * HEAD_DIM
            o_ref[0, :, col:col + LANES] = o2.T.astype(o_ref.dtype)


def nsa_attend_fast(shifts, qc, qr, gates, cmp_p, sel_p, win_p, T, qb, q0, w_off, n_sel):
    B = qc.shape[0]
    group = N_HEADS // N_KV
    R = group * qb
    n_cmp = cmp_p.shape[2]
    n_selp = -(-n_sel // LANES) * LANES
    l_sel = sel_p.shape[2]
    l_win = win_p.shape[2]
    w_rows = WINDOW + max(qb, LANES)
    assert l_win >= w_rows and l_sel >= ((q0 + T - 1) // KEY_TILE + 1) * KEY_TILE
    assert n_sel >= SEL_TOPK and q0 % qb == 0 and KEY_TILE % qb == 0
    wmap = _overlap_map(n_cmp, n_selp)
    hot = np.zeros((l_sel, n_selp), np.float32)
    hot[np.arange(l_sel), np.arange(l_sel) // SEL_BLOCK] = 1.0
    hot = jnp.asarray(hot, BF16)
    kern = functools.partial(_nsa_fast_kernel, qb=qb, n_sel=n_sel, n_selp=n_selp, n_cmp=n_cmp,
                             q0=q0, w_off=w_off, w_rows=w_rows, l_win=l_win)
    qspec = pl.BlockSpec((1, N_KV, 1, LANES, R), lambda b, i, sh: (b, 0, i, 0, 0))
    resident = dict(pipeline_mode=pl.Buffered(1))
    grid_spec = pltpu.PrefetchScalarGridSpec(
        num_scalar_prefetch=1,
        grid=(B, T // qb),
        in_specs=[
            qspec, qspec,
            pl.BlockSpec((1, 1, LANES, qb), lambda b, i, sh: (b, i, 0, 0)),
            pl.BlockSpec((1, N_KV, n_cmp, LANES), lambda b, i, sh: (b, 0, 0, 0)),
            pl.BlockSpec((1, N_KV, l_sel, LANES), lambda b, i, sh: (b, 0, 0, 0), **resident),
            pl.BlockSpec((1, N_KV, l_win, LANES), lambda b, i, sh: (b, 0, 0, 0), **resident),
            pl.BlockSpec((n_selp, n_cmp), lambda b, i, sh: (0, 0)),
            pl.BlockSpec((l_sel, n_selp), lambda b, i, sh: (0, 0), **resident),
        ],
        out_specs=pl.BlockSpec((1, qb, N_HEADS * HEAD_DIM), lambda b, i, sh: (b, i, 0)),
        scratch_shapes=[
            pltpu.VMEM((N_KV, 2 * LANES, R), BF16),
            pltpu.VMEM((N_KV, HEAD_DIM, R), F32),
            pltpu.VMEM((N_KV, HEAD_DIM, R), F32),
            pltpu.VMEM((N_KV, 8, R), F32),
            pltpu.VMEM((N_KV, HEAD_DIM, R), F32),
        ],
    )
    return pl.pallas_call(
        kern,
        out_shape=jax.ShapeDtypeStruct((B, T, N_HEADS * HEAD_DIM), BF16),
        grid_spec=grid_spec,
        compiler_params=_cparams(("parallel", "arbitrary")),
        name="nsa_attend_fast",
    )(shifts, qc, qr, gates, cmp_p, sel_p, win_p, wmap, hot)


def _nsa_dec_kernel(pt_ref, *refs, n_pg, qb, n_sel, n_selp, n_cmp, q0, w_off):
    pages = refs[:n_pg]
    (qcp_ref, qbd_ref, gate_ref, cmp_ref, new_ref, win_ref, wmap_ref, fold_ref, hot_ref,
     o_ref, qaug_ref, oc_ref, m_ref, l_ref, acc_ref) = refs[n_pg:]
    g = pl.program_id(1)
    nk = N_KV * HEAD_DIM
    t_new = new_ref.shape[1]
    lane = lax.broadcasted_iota(jnp.int32, (1, LANES), 1)
    qpos = q0 + (lane & (qb - 1))

    @pl.when(g == 0)
    def _first():
        s = jnp.dot(cmp_ref[0, 0], qcp_ref[0, 0], preferred_element_type=F32)
        for k in range(1, N_KV):
            s = s + jnp.dot(cmp_ref[0, k], qcp_ref[0, k], preferred_element_type=F32)
        m_idx = lax.broadcasted_iota(jnp.int32, (n_cmp, LANES), 0)
        cvalid = (m_idx >= 1) & ((m_idx - 1) * CMP_STRIDE + CMP_LEN - 1 <= qpos)
        s = s + jnp.where(cvalid, 0.0, NEG_INF)
        e = jnp.exp2(s - jnp.max(s, axis=0, keepdims=True))
        den = jnp.maximum(jnp.sum(e, axis=0, keepdims=True), TINY)
        p = e * jnp.where(qpos >= CMP_LEN - 1, 1.0 / den, 0.0)
        pb = p.astype(BF16)
        for k in range(N_KV):
            oc_ref[k * HEAD_DIM:(k + 1) * HEAD_DIM, :] = _tn_dot(cmp_ref[0, k], pb)[HEAD_DIM:, :]
        fold = fold_ref[...]
        p_lo = (p - pb.astype(F32)).astype(BF16)
        psum = jnp.dot(pb, fold, preferred_element_type=F32) + jnp.dot(p_lo, fold, preferred_element_type=F32)
        hi = psum.astype(BF16)
        lo = (psum - hi.astype(F32)).astype(BF16)
        imp = (jnp.dot(wmap_ref[...], hi, preferred_element_type=F32)
               + jnp.dot(wmap_ref[...], lo, preferred_element_type=F32))
        blk = lax.broadcasted_iota(jnp.int32, (n_selp, LANES), 0)
        blk_f = blk.astype(F32)
        cur = qpos >> 6
        forced = (blk == 0) | (blk == cur) | (blk == cur - 1)
        v = jnp.where(forced, SEL_FORCE, jnp.where(blk * SEL_BLOCK <= qpos, imp, SEL_NEG))
        v = jnp.where(blk < n_sel, v, -jnp.inf)

        def pick_one(_, carry):
            v, sel = carry
            m = jnp.max(v, axis=0, keepdims=True)
            first = jnp.min(jnp.where(v == m, blk_f, float(n_selp)), axis=0, keepdims=True)
            pick = blk_f == first
            sel = jnp.where(pick & (m > 0.5 * SEL_NEG), 0.0, sel)
            return jnp.where(pick, -jnp.inf, v), sel

        _, sel = lax.fori_loop(0, SEL_TOPK, pick_one, (v, jnp.full((n_selp, LANES), NEG_INF, F32)))
        qaug_ref[0:nk, :] = qbd_ref[0]
        qaug_ref[nk:, :] = sel.astype(BF16)
        m_ref[...] = jnp.full(m_ref.shape, NEG_INF, F32)
        l_ref[...] = jnp.zeros(l_ref.shape, F32)
        acc_ref[...] = jnp.zeros(acc_ref.shape, F32)

    def attend(kx, vx, hot, bias):
        s = jnp.dot(jnp.concatenate([kx, hot], axis=1), qaug_ref[...], preferred_element_type=F32)
        if bias is not None:
            s = s + bias
        m_old = m_ref[...]
        m_new = jnp.maximum(m_old, jnp.max(s, axis=0, keepdims=True))
        alpha = jnp.exp2(m_old - m_new)
        p = jnp.exp2(s - m_new)
        l_ref[...] = alpha * l_ref[...] + jnp.sum(p.reshape(p.shape[0] // 8, 8, LANES), axis=0)
        acc_ref[...] = alpha * acc_ref[...] + _tn_dot(vx, p.astype(BF16))
        m_ref[...] = m_new

    x = jnp.concatenate([pages[i][0] for i in range(n_pg)], axis=0).astype(BF16)
    start = pl.multiple_of(g * (n_pg * PAGE), n_pg * PAGE)
    attend(x[:, :nk], x[:, nk:], hot_ref[pl.ds(start, n_pg * PAGE), :], None)

    @pl.when(g == pl.num_programs(1) - 1)
    def _last():
        pad = jnp.zeros((HALO - t_new, 2 * nk), F32)
        row = lax.broadcasted_iota(jnp.int32, (HALO, LANES), 0)
        new_ok = (row < t_new) & (q0 + row <= qpos)
        xn = jnp.concatenate([new_ref[0, :, 2 * nk:4 * nk], pad], axis=0).astype(BF16)
        hot_new = jnp.where(
            lax.broadcasted_iota(jnp.int32, (HALO, n_selp), 1) == q0 // SEL_BLOCK, 1.0, 0.0).astype(BF16)
        attend(xn[:, :nk], xn[:, nk:], hot_new, jnp.where(new_ok, 0.0, NEG_INF))

        qbd = qbd_ref[0]
        xw = win_ref[0].astype(BF16)
        wrow = lax.broadcasted_iota(jnp.int32, (xw.shape[0], LANES), 0)
        wpos = w_off + wrow
        wvalid = (wpos <= qpos) & (wpos > qpos - WINDOW) & (wpos >= 0)
        s_c = jnp.dot(xw[:, :nk], qbd, preferred_element_type=F32) + jnp.where(wvalid, 0.0, NEG_INF)
        xwn = jnp.concatenate([new_ref[0, :, 4 * nk:6 * nk], pad], axis=0).astype(BF16)
        npos = q0 + row
        nvalid = (row < t_new) & (npos <= qpos) & (npos > qpos - WINDOW)
        s_n = jnp.dot(xwn[:, :nk], qbd, preferred_element_type=F32) + jnp.where(nvalid, 0.0, NEG_INF)
        m_w = jnp.maximum(jnp.max(s_c, axis=0, keepdims=True), jnp.max(s_n, axis=0, keepdims=True))
        e_c = jnp.exp2(s_c - m_w)
        e_n = jnp.exp2(s_n - m_w)
        den = jnp.sum(e_c, axis=0, keepdims=True) + jnp.sum(e_n, axis=0, keepdims=True)
        o_w = _tn_dot(xw[:, nk:], e_c.astype(BF16)) + _tn_dot(xwn[:, nk:], e_n.astype(BF16))
        o_w = o_w * (1.0 / jnp.maximum(den, TINY))
        l_s = jnp.maximum(jnp.sum(l_ref[...], axis=0, keepdims=True), TINY)
        o_s = acc_ref[...] * (1.0 / l_s)
        gate = gate_ref[0]
        o_ref[0] = gate[0:1, :] * oc_ref[...] + gate[1:2, :] * o_s + gate[2:3, :] * o_w


def _dec_page_index(b, g, pt, *, i, n_pg):
    return (pt[b, g * n_pg + i], 0, 0)


def nsa_attend_dec(table, qc, qr, gates, cmp_p, sel_pages, kvp, cache_win, qb, q0, n_sel):
    B, n_pages = table.shape
    group = N_HEADS // N_KV
    assert N_KV * group * qb == LANES
    nk = N_KV * HEAD_DIM
    n_pg = 16 if n_pages % 16 == 0 else n_pages
    n_cmp = cmp_p.shape[2]
    n_selp = -(-n_sel // LANES) * LANES
    wl = cache_win.shape[1]
    eye = jnp.eye(N_KV, dtype=F32)

    def spread(q):
        qt = q[:, :, 0, :, :HEAD_DIM].astype(F32).transpose(0, 1, 3, 2)
        return jnp.einsum("bkdr,kj->bkdjr", qt, eye).reshape(B, N_KV, HEAD_DIM, LANES)

    qc_pad = jnp.pad(spread(qc), ((0, 0), (0, 0), (0, LANES - HEAD_DIM), (0, 0))).astype(BF16)
    q_bd = spread(qr).reshape(B, nk, LANES).astype(BF16)
    gate_l = gates[:, :, :3 * N_HEADS].reshape(B, qb, N_KV, group, 3).transpose(0, 4, 2, 3, 1)
    gate_l = gate_l.reshape(B, 3, LANES)
    wmap = _overlap_map(n_cmp, n_selp)
    lane = np.arange(LANES)
    fold = jnp.asarray((lane[:, None] // (group * qb) == lane[None, :] // (group * qb))
                       & (lane[:, None] % qb == lane[None, :] % qb), BF16)
    l_past = n_pages * PAGE
    hot = np.zeros((l_past, n_selp), np.float32)
    hot[np.arange(l_past), np.arange(l_past) // SEL_BLOCK] = 1.0
    hot = jnp.asarray(hot, BF16)
    kern = functools.partial(_nsa_dec_kernel, n_pg=n_pg, qb=qb, n_sel=n_sel, n_selp=n_selp,
                             n_cmp=n_cmp, q0=q0, w_off=q0 - wl)
    in_specs = [pl.BlockSpec((1, PAGE, 2 * nk), functools.partial(_dec_page_index, i=i, n_pg=n_pg))
                for i in range(n_pg)]
    in_specs += [
        pl.BlockSpec((1, N_KV, LANES, LANES), lambda b, g, pt: (b, 0, 0, 0)),
        pl.BlockSpec((1, nk, LANES), lambda b, g, pt: (b, 0, 0)),
        pl.BlockSpec((1, 3, LANES), lambda b, g, pt: (b, 0, 0)),
        pl.BlockSpec((1, N_KV, n_cmp, LANES), lambda b, g, pt: (b, 0, 0, 0)),
        pl.BlockSpec((1, qb, 6 * nk), lambda b, g, pt: (b, 0, 0)),
        pl.BlockSpec((1, wl, 2 * nk), lambda b, g, pt: (b, 0, 0)),
        pl.BlockSpec((n_selp, n_cmp), lambda b, g, pt: (0, 0)),
        pl.BlockSpec((LANES, LANES), lambda b, g, pt: (0, 0)),
        pl.BlockSpec((l_past, n_selp), lambda b, g, pt: (0, 0), pipeline_mode=pl.Buffered(1)),
    ]
    grid_spec = pltpu.PrefetchScalarGridSpec(
        num_scalar_prefetch=1,
        grid=(B, n_pages // n_pg),
        in_specs=in_specs,
        out_specs=pl.BlockSpec((1, nk, LANES), lambda b, g, pt: (b, 0, 0)),
        scratch_shapes=[
            pltpu.VMEM((nk + n_selp, LANES), BF16),
            pltpu.VMEM((nk, LANES), F32),
            pltpu.VMEM((1, LANES), F32),
            pltpu.VMEM((8, LANES), F32),
            pltpu.VMEM((nk, LANES), F32),
        ],
    )
    o_t = pl.pallas_call(
        kern,
        out_shape=jax.ShapeDtypeStruct((B, nk, LANES), F32),
        grid_spec=grid_spec,
        compiler_params=_cparams(("parallel", "arbitrary")),
        name="nsa_attend_dec",
    )(table, *([sel_pages] * n_pg), qc_pad, q_bd, gate_l, cmp_p, kvp, cache_win, wmap, fold, hot)
    o6 = o_t.reshape(B, N_KV, HEAD_DIM, N_KV, group, qb)
    o5 = jnp.einsum("bkdkgq->bqkgd", o6)
    return o5.reshape(B * qb, N_HEADS * HEAD_DIM)


def _rope_tables(pos, half):
    inv = jnp.exp(-math.log(ROPE_THETA) * jnp.arange(half, dtype=F32) / half)
    ang = pos.astype(F32)[:, None] * inv[None, :]
    return jnp.cos(ang), jnp.sin(ang)


def _prep_weights(ret_w_in, ret_w_out, ffn_w_in, ffn_w_out, kv_w, kv_knorm, cmp_w1, cmp_w2,
                  nsa_w_qg, nsa_qnorm, nsa_w_o):
    nq = N_HEADS * HEAD_DIM
    qg_pad = nq + LANES - nsa_w_qg.shape[2]
    w_qg = jnp.pad(nsa_w_qg, ((0, 0), (0, 0), (0, qg_pad))).astype(BF16)
    R = CMP_LEN // CMP_STRIDE
    w1 = cmp_w1.reshape(2, R, CMP_STRIDE, HEAD_DIM, CMP_HID).transpose(0, 2, 3, 1, 4)
    w1 = w1.reshape(2, CMP_STRIDE, HEAD_DIM, R * CMP_HID)
    w1dup = jnp.concatenate([w1, w1], axis=2).astype(BF16)
    w1dup = w1dup.reshape(2, CMP_STRIDE // 2, 2 * LANES, R * CMP_HID)
    z = jnp.zeros((CMP_HID, HEAD_DIM), F32)
    w2p = jnp.stack([jnp.concatenate([cmp_w2[0], z], axis=1),
                     jnp.concatenate([z, cmp_w2[1]], axis=1)]).astype(BF16)
    ones = jnp.ones((HEAD_DIM,), F32)
    return dict(
        ret_w_in=ret_w_in.astype(BF16), ret_w_out=ret_w_out.astype(BF16),
        ffn_w_in=ffn_w_in.astype(BF16), ffn_w_out=ffn_w_out.astype(BF16),
        kv_w=kv_w.astype(BF16), w_qg=w_qg, w_o_raw=nsa_w_o.astype(BF16),
        kv_gain=jnp.tile(kv_knorm, (1, 2)),
        cmp_gain=jnp.concatenate([kv_knorm[0], ones]).reshape(1, LANES),
        q_gain2=jnp.tile(nsa_qnorm, (1, 2)) * (HEAD_DIM ** -0.5 * math.log2(math.e)),
        score_bound=(1.05 * HEAD_DIM ** 0.5 * math.log2(math.e))
        * jnp.max(jnp.abs(nsa_qnorm), axis=1)[:, None] * jnp.max(jnp.abs(kv_knorm), axis=1)[None, :],
        w1dup=w1dup, w2p=w2p,
        bd=jnp.asarray(np.kron(np.eye(2), np.ones((HEAD_DIM, HEAD_DIM))), BF16),
    )


def _trunk(x, past_len, ret_s0, conv0, ctx, W, P):
    B, T, D = x.shape
    M = B * T
    depth = P["norm_mix"].shape[0]
    n_a = P["ret_w_in"].shape[0]
    pos = past_len + jnp.arange(T)
    cos_r, sin_r = _rope_tables(pos, RET_DK // 2)
    c32, s32 = _rope_tables(pos, HEAD_DIM // 2)
    cos_n = jnp.tile(c32, (1, 4))
    sin_n = jnp.concatenate([-s32, s32, -s32, s32], axis=1)
    lg = jnp.log1p(-jnp.exp2(-5.0 - jnp.arange(RET_HEADS, dtype=F32)))
    L = RET_CHUNK if T % RET_CHUNK == 0 else T
    gl = jnp.exp(L * lg)
    tm = min(512, M)
    tf = 1024 if T % 1024 == 0 else 512
    tb = min(512, T)
    tq = min(512, T)
    qb = next((c for c in (2 * Q_BLOCK, Q_BLOCK) if T % c == 0), T)
    tt = min(256, T)

    x2 = x.reshape(M, D)
    ret_states, conv_states = [], []
    for layer in range(depth):
        if layer == n_a:
            kvp, selp, winp = kv_project(x2.reshape(B, T, D), P["kv_norm"], W["kv_w"], W["kv_gain"],
                                         cos_n, sin_n, W["bd"], tq)
            nk2 = 2 * N_KV * HEAD_DIM
            if ctx is None:
                table = jnp.arange(M // PAGE, dtype=jnp.int32).reshape(B, T // PAGE)
                cmp_p = compress(kvp.reshape(M // PAGE, PAGE, 3 * nk2), table,
                                 W["w1dup"], W["pb"], W["w2p"], W["cmp_gain"])
                sel_p, win_p = selp, winp
                w_off = 0
            else:
                cache_cmp, cache_sel, cache_win, table = ctx
                n_pool = cache_cmp.shape[0]
                cmp_p = compress(cache_cmp.reshape(n_pool, PAGE, nk2), table,
                                 W["w1dup"], W["pb"], W["w2p"], W["cmp_gain"])
                sel_pages = cache_sel.reshape(n_pool, PAGE, nk2)
                win_rows = cache_win.reshape(B, cache_win.shape[1], nk2)
            n_sel = -(-(past_len + T) // SEL_BLOCK)
        h_norm = P["norm_mix"][layer]
        if layer < n_a:
            proj = norm_matmul(x2, h_norm, W["ret_w_in"][layer], min(1024, M), 1024)
            og, s_new = retention(proj.reshape(B, T, -1), ret_s0[layer], cos_r, sin_r, lg, gl, L, tb)
            ret_states.append(s_new)
            x2 = matmul_res(og.reshape(M, -1), W["ret_w_out"][layer], x2, tm)
        else:
            j = layer - n_a
            if qb % LANES == 0:
                qc, qr, gates = qg_project_t(x2.reshape(B, T, D), h_norm, W["w_qg"][j],
                                             W["q_gain2"][j:j + 1], cos_n, sin_n, W["bd"], tq, qb)
                shifts = W["score_bound"][j]
                args = (qc, qr, gates, cmp_p, sel_p, win_p)
                o = lax.cond(
                    jnp.max(shifts) <= 30.0,
                    lambda a: nsa_attend_fast(shifts, *a, T, qb, past_len, w_off, n_sel),
                    lambda a: nsa_attend_t(*a, T, qb, past_len, w_off, n_sel).reshape(
                        B, T, N_HEADS, LANES)[..., HEAD_DIM:].reshape(B, T, N_HEADS * HEAD_DIM),
                    args)
                x2 = matmul_res(o.reshape(M, -1), W["w_o_raw"][j], x2, tm)
            else:
                qc, qr, gates = qg_project(x2.reshape(B, T, D), h_norm, W["w_qg"][j],
                                           W["q_gain2"][j:j + 1], cos_n, sin_n, W["bd"], tq, qb)
                o = nsa_attend_dec(table, qc, qr, gates, cmp_p, sel_pages, kvp, win_rows,
                                   qb, past_len, n_sel)
                x2 = matmul_res(o, W["w_o_raw"][j], x2, tm)
        if T % tf == 0:
            act, tail = ffn_in(x2, P["norm_ffn"][layer], W["ffn_w_in"][layer], conv0[layer],
                               P["ffn_conv_w"][layer], P["ffn_conv_b"][layer], T, tf, 256)
            conv_states.append(tail)
        else:
            proj = norm_matmul(x2, P["norm_ffn"][layer], W["ffn_w_in"][layer], tm, 512)
            proj3 = proj.reshape(B, T, 2 * D_FF)
            act = ffn_mid(proj3, conv0[layer], P["ffn_conv_w"][layer], P["ffn_conv_b"][layer], tt)
            conv_states.append(proj3[:, T - 2:, :D_FF])
        x2 = matmul_res(act.reshape(M, D_FF), W["ffn_w_out"][layer], x2, tm)

    nk = N_KV * HEAD_DIM
    new_cmp = kvp[:, :, 0:2 * nk].reshape(B, T, 2, N_KV, HEAD_DIM)
    new_sel = kvp[:, :, 2 * nk:4 * nk].reshape(B, T, 2, N_KV, HEAD_DIM)
    new_win = kvp[:, :, 4 * nk:6 * nk].reshape(B, T, 2, N_KV, HEAD_DIM)
    return (x2.reshape(B, T, D), jnp.stack(ret_states), jnp.stack(conv_states),
            new_cmp, new_sel, new_win)


def kernel(x_prompt, x_sample, cache_cmp_kv, cache_sel_kv, cache_win_kv, state_ret, state_conv,
           page_table, norm_mix, norm_ffn, ret_w_in, ret_w_out, ffn_w_in, ffn_conv_w, ffn_conv_b,
           ffn_w_out, kv_norm, kv_w, kv_knorm, cmp_pos, cmp_w1, cmp_w2, nsa_w_qg, nsa_qnorm, nsa_w_o):
    W = _prep_weights(ret_w_in, ret_w_out, ffn_w_in, ffn_w_out, kv_w, kv_knorm, cmp_w1, cmp_w2,
                      nsa_w_qg, nsa_qnorm, nsa_w_o)
    W["pb"] = pos_bias(cmp_pos, cmp_w1)
    P = dict(norm_mix=norm_mix, norm_ffn=norm_ffn, ret_w_in=ret_w_in, ffn_conv_w=ffn_conv_w,
             ffn_conv_b=ffn_conv_b, kv_norm=kv_norm)
    depth = norm_mix.shape[0]
    n_a = ret_w_in.shape[0]
    B, T, _ = x_prompt.shape
    zero_ret = jnp.zeros((n_a, B, RET_HEADS, RET_DK, RET_DV), F32)
    zero_conv = jnp.zeros((depth, B, 2, D_FF), F32)
    y_p, ret_p, conv_p, cmp_p, sel_p, win_p = _trunk(x_prompt, 0, zero_ret, zero_conv, None, W, P)
    win_p = win_p[:, T - min(WINDOW, T):]

    db, ts, _ = x_sample.shape
    past_len = page_table.shape[1] * PAGE
    ctx = (cache_cmp_kv, cache_sel_kv, cache_win_kv, page_table)
    y_s, ret_s, conv_s, cmp_s, sel_s, win_new = _trunk(x_sample, past_len, state_ret, state_conv,
                                                        ctx, W, P)
    all_win = jnp.concatenate([cache_win_kv, win_new], axis=1)
    win_s = all_win[:, all_win.shape[1] - min(WINDOW, past_len + ts):]
    return (y_p, y_s, ret_p, ret_s, conv_p, conv_s, cmp_p, cmp_s, sel_p, sel_s, win_p, win_s)
```

```python
import functools
import math

import jax
import jax.numpy as jnp
import numpy as np
from jax import lax
from jax.experimental import pallas as pl
from jax.experimental.pallas import tpu as pltpu

F32 = jnp.float32
BF16 = jnp.bfloat16

D_MODEL = 1024
PAGE = 128
RET_HEADS = 4
RET_DK = 256
RET_DV = 512
RET_CHUNK = 128
N_HEADS = 16
N_KV = 4
HEAD_DIM = 64
CMP_LEN = 32
CMP_STRIDE = 16
CMP_HID = 128
SEL_BLOCK = 64
SEL_TOPK = 16
WINDOW = 512
Q_BLOCK = 128
D_FF = 2816
ROPE_THETA = 10000.0
EPS = 1e-6
NEG_INF = -1e30
TINY = 1e-30
SEL_FORCE = 1e6
SEL_NEG = -1e6

LANES = 128
KEY_TILE = 512
HALO = 16
VMEM_LIMIT = 48 * 1024 * 1024


def _cparams(sem):
    return pltpu.CompilerParams(dimension_semantics=sem, vmem_limit_bytes=VMEM_LIMIT)


def _nt_dot(a, b):
    return lax.dot_general(a, b, (((1,), (1,)), ((), ())), preferred_element_type=F32)


def _tn_dot(a, b):
    return lax.dot_general(a, b, (((0,), (0,)), ((), ())), preferred_element_type=F32)


def _gelu(x):
    return 0.5 * x * (1.0 + jnp.tanh(math.sqrt(2.0 / math.pi) * (x + 0.044715 * (x * x * x))))


def _rms_rows(x, g):
    r = lax.rsqrt(jnp.mean(x * x, axis=-1, keepdims=True) + EPS)
    return x * r * g


def _head_ms(x, bd):
    x2 = x * x
    hi = x2.astype(BF16)
    lo = (x2 - hi.astype(F32)).astype(BF16)
    s = jnp.dot(hi, bd, preferred_element_type=F32) + jnp.dot(lo, bd, preferred_element_type=F32)
    return s * (1.0 / HEAD_DIM)


def _rope64(x, cos, sin):
    lane = lax.broadcasted_iota(jnp.int32, x.shape, 1)
    sw = jnp.where((lane & 63) < 32, pltpu.roll(x, 96, 1), pltpu.roll(x, 32, 1))
    return x * cos + sw * sin


def _pack_pair(k2, v2, dtype):
    lane = lax.broadcasted_iota(jnp.int32, k2.shape, 1)
    lo = lane < HEAD_DIM
    even = jnp.where(lo, k2, pltpu.roll(v2, HEAD_DIM, 1)).astype(dtype)
    odd = jnp.where(lo, pltpu.roll(k2, HEAD_DIM, 1), v2).astype(dtype)
    return even, odd


def _norm_matmul_kernel(x_ref, g_ref, w_ref, o_ref, h_ref):
    @pl.when(pl.program_id(1) == 0)
    def _():
        h_ref[...] = _rms_rows(x_ref[...], g_ref[...]).astype(BF16)

    o_ref[...] = jnp.dot(h_ref[...], w_ref[...], preferred_element_type=F32).astype(o_ref.dtype)


def norm_matmul(x, g, w, tm, tn):
    M, D = x.shape
    N = w.shape[1]
    return pl.pallas_call(
        _norm_matmul_kernel,
        out_shape=jax.ShapeDtypeStruct((M, N), F32),
        grid=(M // tm, N // tn),
        in_specs=[pl.BlockSpec((tm, D), lambda i, j: (i, 0)),
                  pl.BlockSpec((1, D), lambda i, j: (0, 0)),
                  pl.BlockSpec((D, tn), lambda i, j: (0, j))],
        out_specs=pl.BlockSpec((tm, tn), lambda i, j: (i, j)),
        scratch_shapes=[pltpu.VMEM((tm, D), BF16)],
        compiler_params=_cparams(("parallel", "arbitrary")),
        name="norm_matmul",
    )(x, g.reshape(1, D), w)


def _matmul_res_kernel(a_ref, w_ref, r_ref, o_ref):
    o_ref[...] = r_ref[...] + jnp.dot(a_ref[...].astype(BF16), w_ref[...],
                                      preferred_element_type=F32)


def matmul_res(a, w, res, tm):
    M, K = a.shape
    N = w.shape[1]
    return pl.pallas_call(
        _matmul_res_kernel,
        out_shape=jax.ShapeDtypeStruct((M, N), F32),
        grid=(M // tm,),
        in_specs=[pl.BlockSpec((tm, K), lambda i: (i, 0)),
                  pl.BlockSpec((K, N), lambda i: (0, 0)),
                  pl.BlockSpec((tm, N), lambda i: (i, 0))],
        out_specs=pl.BlockSpec((tm, N), lambda i: (i, 0)),
        compiler_params=_cparams(("parallel",)),
        name="matmul_res",
    )(a, w, res)


def _retention_kernel(lg_ref, gl_ref, q_ref, k_ref, v_ref, g_ref, cos_ref, sin_ref, s0_ref,
                      o_ref, sout_ref, S_ref, *, L, n_chunk):
    h = pl.program_id(1)
    t = pl.program_id(2)
    lg = lg_ref[h]
    gl = gl_ref[h]

    @pl.when(t == 0)
    def _():
        S_ref[...] = s0_ref[0, 0]

    ii = lax.broadcasted_iota(jnp.int32, (L, L), 0)
    jj = lax.broadcasted_iota(jnp.int32, (L, L), 1)
    diff = (ii - jj).astype(F32)
    decay = jnp.where(diff >= 0, jnp.exp(jnp.maximum(diff, 0.0) * lg), 0.0)
    idx = lax.broadcasted_iota(jnp.int32, (L, 1), 0).astype(F32)
    q_dec = jnp.exp((idx + 1.0) * lg)
    k_dec = jnp.exp((L - 1.0 - idx) * lg)
    half = RET_DK // 2

    for c in range(n_chunk):
        rows = pl.ds(c * L, L)
        cos = cos_ref[rows, :]
        sin = sin_ref[rows, :]

        def rope(x):
            x1, x2 = x[:, :half], x[:, half:]
            return jnp.concatenate([x1 * cos - x2 * sin, x2 * cos + x1 * sin], axis=1)

        qr = rope(q_ref[0, rows, :])
        kr = rope(k_ref[0, rows, :]) * (RET_DK ** -0.5)
        qb = qr.astype(BF16)
        vb = v_ref[0, rows, :].astype(BF16)
        sc = _nt_dot(qb, kr.astype(BF16)) * decay
        S = S_ref[...]
        o = jnp.dot(sc.astype(BF16), vb, preferred_element_type=F32)
        o = o + jnp.dot(qb, S.astype(BF16), preferred_element_type=F32) * q_dec
        S_ref[...] = S * gl + _tn_dot((kr * k_dec).astype(BF16), vb)
        on = o * lax.rsqrt(jnp.mean(o * o, axis=-1, keepdims=True) + EPS)
        g = g_ref[0, rows, :]
        o_ref[0, rows, :] = (on * (g * jax.nn.sigmoid(g))).astype(o_ref.dtype)

    @pl.when(t == pl.num_programs(2) - 1)
    def _():
        sout_ref[0, 0] = S_ref[...]


def retention(proj, s0, cos, sin, lg, gl, L, tb):
    B, T, _ = proj.shape
    n_chunk = tb // L
    odt = BF16 if tb % 16 == 0 else F32
    kern = functools.partial(_retention_kernel, L=L, n_chunk=n_chunk)
    grid_spec = pltpu.PrefetchScalarGridSpec(
        num_scalar_prefetch=2,
        grid=(B, RET_HEADS, T // tb),
        in_specs=[
            pl.BlockSpec((1, tb, RET_DK), lambda b, h, t, *_: (b, t, h)),
            pl.BlockSpec((1, tb, RET_DK), lambda b, h, t, *_: (b, t, RET_HEADS + h)),
            pl.BlockSpec((1, tb, RET_DV), lambda b, h, t, *_: (b, t, RET_HEADS + h)),
            pl.BlockSpec((1, tb, RET_DV), lambda b, h, t, *_: (b, t, 2 * RET_HEADS + h)),
            pl.BlockSpec((tb, RET_DK // 2), lambda b, h, t, *_: (t, 0)),
            pl.BlockSpec((tb, RET_DK // 2), lambda b, h, t, *_: (t, 0)),
            pl.BlockSpec((1, 1, RET_DK, RET_DV), lambda b, h, t, *_: (b, h, 0, 0)),
        ],
        out_specs=[
            pl.BlockSpec((1, tb, RET_DV), lambda b, h, t, *_: (b, t, h)),
            pl.BlockSpec((1, 1, RET_DK, RET_DV), lambda b, h, t, *_: (b, h, 0, 0)),
        ],
        scratch_shapes=[pltpu.VMEM((RET_DK, RET_DV), F32)],
    )
    return pl.pallas_call(
        kern,
        out_shape=[jax.ShapeDtypeStruct((B, T, RET_HEADS * RET_DV), odt),
                   jax.ShapeDtypeStruct((B, RET_HEADS, RET_DK, RET_DV), F32)],
        grid_spec=grid_spec,
        compiler_params=_cparams(("parallel", "parallel", "arbitrary")),
        name="retention",
    )(lg, gl, proj, proj, proj, proj, cos, sin, s0)


def _ffn_mid_kernel(u_ref, gt_ref, halo_ref, cw_ref, cb_ref, o_ref):
    u = u_ref[0]
    hl = halo_ref[0, 0]
    row = lax.broadcasted_iota(jnp.int32, u.shape, 0)
    u1 = jnp.where(row == 0, hl[1:2], pltpu.roll(u, 1, 0))
    u2 = jnp.where(row == 0, hl[0:1], jnp.where(row == 1, hl[1:2], pltpu.roll(u, 2, 0)))
    c = cb_ref[...] + cw_ref[0:1] * u2
    c = c + cw_ref[1:2] * u1
    c = c + cw_ref[2:3] * u
    o_ref[0] = (_gelu(c) * gt_ref[0]).astype(o_ref.dtype)


def ffn_mid(proj, buf, conv_w, conv_b, tt):
    B, T, _ = proj.shape
    nt = T // tt
    if nt > 1:
        tails = proj[:, :, :D_FF].reshape(B, nt, tt, D_FF)[:, :-1, tt - 2:, :]
        halo = jnp.concatenate([buf[:, None], tails], axis=1)
    else:
        halo = buf[:, None]
    odt = BF16 if tt % 16 == 0 else F32
    return pl.pallas_call(
        _ffn_mid_kernel,
        out_shape=jax.ShapeDtypeStruct((B, T, D_FF), odt),
        grid=(B, nt),
        in_specs=[pl.BlockSpec((1, tt, D_FF), lambda b, t: (b, t, 0)),
                  pl.BlockSpec((1, tt, D_FF), lambda b, t: (b, t, 1)),
                  pl.BlockSpec((1, 1, 2, D_FF), lambda b, t: (b, t, 0, 0)),
                  pl.BlockSpec((3, D_FF), lambda b, t: (0, 0)),
                  pl.BlockSpec((1, D_FF), lambda b, t: (0, 0))],
        out_specs=pl.BlockSpec((1, tt, D_FF), lambda b, t: (b, t, 0)),
        compiler_params=_cparams(("parallel", "parallel")),
        name="ffn_mid",
    )(proj, proj, halo, conv_w, conv_b.reshape(1, D_FF))


def _ffn_in_kernel(x_ref, xh_ref, g_ref, w_ref, buf_ref, cw_ref, cb_ref,
                   act_ref, tail_ref, *, tiles_per_seq, tn):
    seq_start = (pl.program_id(0) % tiles_per_seq) == 0
    h = _rms_rows(x_ref[...], g_ref[...]).astype(BF16)
    h_ext = jnp.concatenate([_rms_rows(xh_ref[...], g_ref[...]).astype(BF16), h], axis=0)
    tm = h.shape[0]
    row = lax.broadcasted_iota(jnp.int32, (tm, tn), 0)
    for j in range(D_FF // tn):
        cols = slice(j * tn, (j + 1) * tn)
        u_ext = jnp.dot(h_ext, w_ref[:, cols], preferred_element_type=F32)
        u = u_ext[HALO:, :]
        gt = jnp.dot(h, w_ref[:, D_FF + j * tn:D_FF + (j + 1) * tn], preferred_element_type=F32)
        hl = jnp.where(seq_start, buf_ref[0, :, cols], u_ext[HALO - 2:HALO, :])
        u1 = jnp.where(row == 0, hl[1:2], pltpu.roll(u, 1, 0))
        u2 = jnp.where(row == 0, hl[0:1], jnp.where(row == 1, hl[1:2], pltpu.roll(u, 2, 0)))
        c = cb_ref[:, cols] + cw_ref[0:1, cols] * u2
        c = c + cw_ref[1:2, cols] * u1
        c = c + cw_ref[2:3, cols] * u
        act_ref[:, cols] = (_gelu(c) * gt).astype(act_ref.dtype)
        tail_ref[0, :, cols] = u[tm - 2:, :]


def ffn_in(x, g, w, buf, conv_w, conv_b, T, tm, tn):
    M, D = x.shape
    tiles_per_seq = T // tm
    kern = functools.partial(_ffn_in_kernel, tiles_per_seq=tiles_per_seq, tn=tn)
    act, tails = pl.pallas_call(
        kern,
        out_shape=[jax.ShapeDtypeStruct((M, D_FF), BF16),
                   jax.ShapeDtypeStruct((M // tm, 2, D_FF), F32)],
        grid=(M // tm,),
        in_specs=[pl.BlockSpec((tm, D), lambda i: (i, 0)),
                  pl.BlockSpec((HALO, D), lambda i: (jnp.maximum(i * (tm // HALO) - 1, 0), 0)),
                  pl.BlockSpec((1, D), lambda i: (0, 0)),
                  pl.BlockSpec((D, 2 * D_FF), lambda i: (0, 0), pipeline_mode=pl.Buffered(1)),
                  pl.BlockSpec((1, 2, D_FF), lambda i: (i // tiles_per_seq, 0, 0)),
                  pl.BlockSpec((3, D_FF), lambda i: (0, 0)),
                  pl.BlockSpec((1, D_FF), lambda i: (0, 0))],
        out_specs=[pl.BlockSpec((tm, D_FF), lambda i: (i, 0)),
                   pl.BlockSpec((1, 2, D_FF), lambda i: (i, 0, 0))],
        compiler_params=_cparams(("parallel",)),
        name="ffn_in",
    )(x, x, g.reshape(1, D), w, buf, conv_w, conv_b.reshape(1, D_FF))
    return act, tails[tiles_per_seq - 1::tiles_per_seq]


def _kv_kernel(x_ref, g_ref, w_ref, gain_ref, cos_ref, sin_ref, bd_ref,
               kv_ref, selp_ref, winp_ref):
    h = _rms_rows(x_ref[0], g_ref[...]).astype(BF16)
    y = jnp.dot(h, w_ref[...], preferred_element_type=F32)
    cos = cos_ref[...]
    sin = sin_ref[...]
    bd = bd_ref[...]
    nk = N_KV * HEAD_DIM
    kv_ref[0, :, 0:2 * nk] = y[:, 0:2 * nk]
    for br, pack_ref in enumerate((selp_ref, winp_ref)):
        base = 2 * nk * (br + 1)
        kv_ref[0, :, base + nk:base + 2 * nk] = y[:, base + nk:base + 2 * nk]
        for p in range(2):
            kx = y[:, base + LANES * p:base + LANES * (p + 1)]
            kn = kx * lax.rsqrt(_head_ms(kx, bd) + EPS) * gain_ref[br + 1:br + 2, :]
            kr = _rope64(kn, cos, sin)
            kv_ref[0, :, base + LANES * p:base + LANES * (p + 1)] = kr
            vx = y[:, base + nk + LANES * p:base + nk + LANES * (p + 1)]
            even, odd = _pack_pair(kr, vx, pack_ref.dtype)
            pack_ref[0, 2 * p] = even
            pack_ref[0, 2 * p + 1] = odd


def kv_project(x, g, w, gains, cos, sin, bd, tm):
    B, T, D = x.shape
    N = w.shape[1]
    pdt = BF16 if tm % 16 == 0 else F32
    return pl.pallas_call(
        _kv_kernel,
        out_shape=[jax.ShapeDtypeStruct((B, T, N), F32),
                   jax.ShapeDtypeStruct((B, N_KV, T, LANES), pdt),
                   jax.ShapeDtypeStruct((B, N_KV, T, LANES), pdt)],
        grid=(B, T // tm),
        in_specs=[pl.BlockSpec((1, tm, D), lambda b, t: (b, t, 0)),
                  pl.BlockSpec((1, D), lambda b, t: (0, 0)),
                  pl.BlockSpec((D, N), lambda b, t: (0, 0)),
                  pl.BlockSpec((3, LANES), lambda b, t: (0, 0)),
                  pl.BlockSpec((tm, LANES), lambda b, t: (t, 0)),
                  pl.BlockSpec((tm, LANES), lambda b, t: (t, 0)),
                  pl.BlockSpec((LANES, LANES), lambda b, t: (0, 0))],
        out_specs=[pl.BlockSpec((1, tm, N), lambda b, t: (b, t, 0)),
                   pl.BlockSpec((1, N_KV, tm, LANES), lambda b, t: (b, 0, t, 0)),
                   pl.BlockSpec((1, N_KV, tm, LANES), lambda b, t: (b, 0, t, 0))],
        compiler_params=_cparams(("parallel", "parallel")),
        name="kv_project",
    )(x, g.reshape(1, D), w, gains, cos, sin, bd)


def _qg_kernel(x_ref, g_ref, w_ref, gain_ref, cos_ref, sin_ref, bd_ref,
               qc_ref, qr_ref, gate_ref, *, qb, n_qb):
    h = _rms_rows(x_ref[0], g_ref[...]).astype(BF16)
    y = jnp.dot(h, w_ref[...], preferred_element_type=F32)
    cos = cos_ref[...]
    sin = sin_ref[...]
    bd = bd_ref[...]
    nq = N_HEADS * HEAD_DIM
    gate_ref[0] = jax.nn.sigmoid(y[:, nq:nq + LANES])
    lane = lax.broadcasted_iota(jnp.int32, (y.shape[0], LANES), 1)
    lo = lane < HEAD_DIM
    group = N_HEADS // N_KV
    for p in range(N_HEADS // 2):
        qx = y[:, LANES * p:LANES * (p + 1)]
        qn = qx * lax.rsqrt(_head_ms(qx, bd) + EPS) * gain_ref[...]
        qr = _rope64(qn, cos, sin)
        for src, dst in ((qn, qc_ref), (qr, qr_ref)):
            for par in range(2):
                hd = 2 * p + par
                kvh, gi = hd // group, hd % group
                v = src if par == 0 else pltpu.roll(src, HEAD_DIM, 1)
                v = jnp.where(lo, v, 0.0).astype(dst.dtype)
                for j in range(n_qb):
                    dst[0, kvh, j, gi * qb:(gi + 1) * qb, :] = v[j * qb:(j + 1) * qb, :]


def qg_project(x, g, w, gain, cos, sin, bd, tm, qb):
    B, T, D = x.shape
    N = w.shape[1]
    n_qb = tm // qb
    group = N_HEADS // N_KV
    qdt = BF16 if qb % 16 == 0 else F32
    kern = functools.partial(_qg_kernel, qb=qb, n_qb=n_qb)
    qshape = jax.ShapeDtypeStruct((B, N_KV, T // qb, group * qb, LANES), qdt)
    qspec = pl.BlockSpec((1, N_KV, n_qb, group * qb, LANES), lambda b, t: (b, 0, t, 0, 0))
    return pl.pallas_call(
        kern,
        out_shape=[qshape, qshape, jax.ShapeDtypeStruct((B, T, LANES), F32)],
        grid=(B, T // tm),
        in_specs=[pl.BlockSpec((1, tm, D), lambda b, t: (b, t, 0)),
                  pl.BlockSpec((1, D), lambda b, t: (0, 0)),
                  pl.BlockSpec((D, N), lambda b, t: (0, 0)),
                  pl.BlockSpec((1, LANES), lambda b, t: (0, 0)),
                  pl.BlockSpec((tm, LANES), lambda b, t: (t, 0)),
                  pl.BlockSpec((tm, LANES), lambda b, t: (t, 0)),
                  pl.BlockSpec((LANES, LANES), lambda b, t: (0, 0))],
        out_specs=[qspec, qspec, pl.BlockSpec((1, tm, LANES), lambda b, t: (b, t, 0))],
        compiler_params=_cparams(("parallel", "parallel")),
        name="qg_project",
    )(x, g.reshape(1, D), w, gain, cos, sin, bd)


def _pos_bias_kernel(p_ref, w_ref, o_ref):
    o_ref[0] = jnp.dot(p_ref[0].astype(BF16), w_ref[0].astype(BF16), preferred_element_type=F32)


def pos_bias(cmp_pos, cmp_w1):
    K = CMP_LEN * HEAD_DIM
    p = jnp.broadcast_to(cmp_pos.reshape(2, 1, K), (2, 8, K))
    out = pl.pallas_call(
        _pos_bias_kernel,
        out_shape=jax.ShapeDtypeStruct((2, 8, CMP_HID), F32),
        grid=(2,),
        in_specs=[pl.BlockSpec((1, 8, K), lambda c: (c, 0, 0)),
                  pl.BlockSpec((1, K, CMP_HID), lambda c: (c, 0, 0))],
        out_specs=pl.BlockSpec((1, 8, CMP_HID), lambda c: (c, 0, 0)),
        name="pos_bias",
    )(p, cmp_w1)
    return out[:, 0, :]


def _compress_kernel(pt_ref, *refs, n_pg):
    pages = refs[:n_pg]
    w1_ref, pb_ref, w2_ref, gain_ref, out_ref, carry_ref, slab_ref = refs[n_pg:]
    g = pl.program_id(1)

    @pl.when(g == 0)
    def _():
        carry_ref[...] = jnp.zeros_like(carry_ref)

    n = n_pg * (PAGE // CMP_STRIDE)
    lo = lax.broadcasted_iota(jnp.int32, (n, LANES), 1) < HEAD_DIM
    row = lax.broadcasted_iota(jnp.int32, (N_KV * n, LANES), 0)
    res = jnp.zeros((N_KV * n, LANES), F32)
    for c in range(2):
        for pair in range(N_KV // 2):
            col = c * N_KV * HEAD_DIM + pair * LANES
            for i in range(n_pg):
                slab_ref[pair, PAGE * i:PAGE * (i + 1), :] = pages[i][0, :, col:col + LANES]
        acc = jnp.zeros((N_KV * n, 2 * CMP_HID), F32)
        for s in range(0, CMP_STRIDE, 2):
            parts = []
            for k in range(N_KV):
                pair, par = divmod(k, 2)
                keep = lo if par == 0 else jnp.logical_not(lo)
                parts.append(jnp.concatenate(
                    [jnp.where(keep, slab_ref[pair, pl.ds(s + i, n, stride=CMP_STRIDE), :], 0.0)
                     for i in range(2)], axis=1))
            xm = jnp.concatenate(parts, axis=0).astype(BF16)
            acc = acc + jnp.dot(xm, w1_ref[c, s // 2], preferred_element_type=F32)
        p0 = acc[:, :CMP_HID]
        p1 = acc[:, CMP_HID:]
        prev = pltpu.roll(p0, 1, 0)
        for k in range(N_KV):
            prev = jnp.where(row == k * n, carry_ref[2 * k + c, 7:8, :], prev)
        for k in range(N_KV):
            carry_ref[2 * k + c] = p0[(k + 1) * n - 8:(k + 1) * n, :]
        hid = _gelu(prev + p1 + pb_ref[c:c + 1, :]).astype(BF16)
        res = res + jnp.dot(hid, w2_ref[c], preferred_element_type=F32)
    for k in range(N_KV):
        r = res[k * n:(k + 1) * n, :]
        ms = jnp.sum(jnp.where(lo, r * r, 0.0), axis=-1, keepdims=True) * (1.0 / HEAD_DIM)
        kn = r * lax.rsqrt(ms + EPS) * gain_ref[...]
        out_ref[0, k] = jnp.where(lo, kn, r).astype(out_ref.dtype)


def _page_index(b, g, pt_ref, *, i, n_pg):
    return (pt_ref[b, g * n_pg + i], 0, 0)


def compress(pages_arr, table, w1dup, pb, w2p, gain):
    B, n_pages = table.shape
    n_pg = 16 if n_pages % 16 == 0 else n_pages
    n = n_pg * (PAGE // CMP_STRIDE)
    n_sub = n_pages * (PAGE // CMP_STRIDE)
    width = 2 * N_KV * HEAD_DIM
    in_specs = [pl.BlockSpec((1, PAGE, width), functools.partial(_page_index, i=i, n_pg=n_pg))
                for i in range(n_pg)]
    in_specs += [pl.BlockSpec(w1dup.shape, lambda b, g, pt: (0, 0, 0, 0)),
                 pl.BlockSpec(pb.shape, lambda b, g, pt: (0, 0)),
                 pl.BlockSpec(w2p.shape, lambda b, g, pt: (0, 0, 0)),
                 pl.BlockSpec((1, LANES), lambda b, g, pt: (0, 0))]
    grid_spec = pltpu.PrefetchScalarGridSpec(
        num_scalar_prefetch=1,
        grid=(B, n_pages // n_pg),
        in_specs=in_specs,
        out_specs=pl.BlockSpec((1, N_KV, n, LANES), lambda b, g, pt: (b, 0, g, 0)),
        scratch_shapes=[pltpu.VMEM((2 * N_KV, 8, CMP_HID), F32),
                        pltpu.VMEM((N_KV // 2, n_pg * PAGE, LANES), F32)],
    )
    return pl.pallas_call(
        functools.partial(_compress_kernel, n_pg=n_pg),
        out_shape=jax.ShapeDtypeStruct((B, N_KV, n_sub, LANES), BF16),
        grid_spec=grid_spec,
        compiler_params=_cparams(("parallel", "arbitrary")),
        name="compress",
    )(table, *([pages_arr] * n_pg), w1dup, pb, w2p, gain)


def _overlap_map(n_cmp_rows, n_selp):
    m = np.arange(n_cmp_rows)[None, :]
    s = np.arange(n_selp)[:, None]
    c0 = (m - 1) * CMP_STRIDE
    ov = np.minimum(c0 + CMP_LEN, s * SEL_BLOCK + SEL_BLOCK) - np.maximum(c0, s * SEL_BLOCK)
    w = np.maximum(ov, 0).astype(np.float32) / CMP_LEN
    w[:, 0] = 0.0
    return jnp.asarray(w, BF16)


def _qgt_kernel(x_ref, g_ref, w_ref, gain_ref, cos_ref, sin_ref, bd_ref,
                qc_ref, qr_ref, gate_ref, *, qb, n_qb):
    h = _rms_rows(x_ref[0], g_ref[...]).astype(BF16)
    y = jnp.dot(h, w_ref[...], preferred_element_type=F32)
    cos = cos_ref[...]
    sin = sin_ref[...]
    bd = bd_ref[...]
    nq = N_HEADS * HEAD_DIM
    group = N_HEADS // N_KV
    gate_t = jax.nn.sigmoid(y[:, nq:nq + LANES]).T
    for j in range(n_qb):
        gate_ref[0, j] = gate_t[:, j * qb:(j + 1) * qb]
    pad = jnp.zeros((N_KV, n_qb, HEAD_DIM, group * qb), qc_ref.dtype)
    qc_ref[0, :, :, HEAD_DIM:, :] = pad
    qr_ref[0, :, :, HEAD_DIM:, :] = pad
    for p in range(N_HEADS // 2):
        qx = y[:, LANES * p:LANES * (p + 1)]
        qn = qx * lax.rsqrt(_head_ms(qx, bd) + EPS) * gain_ref[...]
        qr = _rope64(qn, cos, sin)
        for src, dst in ((qn, qc_ref), (qr, qr_ref)):
            st = src.T.astype(dst.dtype)
            for par in range(2):
                kvh, gi = divmod(2 * p + par, group)
                for j in range(n_qb):
                    dst[0, kvh, j, 0:HEAD_DIM, gi * qb:(gi + 1) * qb] = (
                        st[par * HEAD_DIM:(par + 1) * HEAD_DIM, j * qb:(j + 1) * qb])


def qg_project_t(x, g, w, gain, cos, sin, bd, tm, qb):
    B, T, D = x.shape
    N = w.shape[1]
    n_qb = tm // qb
    group = N_HEADS // N_KV
    kern = functools.partial(_qgt_kernel, qb=qb, n_qb=n_qb)
    qshape = jax.ShapeDtypeStruct((B, N_KV, T // qb, LANES, group * qb), BF16)
    qspec = pl.BlockSpec((1, N_KV, n_qb, LANES, group * qb), lambda b, t: (b, 0, t, 0, 0))
    return pl.pallas_call(
        kern,
        out_shape=[qshape, qshape, jax.ShapeDtypeStruct((B, T // qb, LANES, qb), F32)],
        grid=(B, T // tm),
        in_specs=[pl.BlockSpec((1, tm, D), lambda b, t: (b, t, 0)),
                  pl.BlockSpec((1, D), lambda b, t: (0, 0)),
                  pl.BlockSpec((D, N), lambda b, t: (0, 0)),
                  pl.BlockSpec((1, LANES), lambda b, t: (0, 0)),
                  pl.BlockSpec((tm, LANES), lambda b, t: (t, 0)),
                  pl.BlockSpec((tm, LANES), lambda b, t: (t, 0)),
                  pl.BlockSpec((LANES, LANES), lambda b, t: (0, 0))],
        out_specs=[qspec, qspec, pl.BlockSpec((1, n_qb, LANES, qb), lambda b, t: (b, t, 0, 0))],
        compiler_params=_cparams(("parallel", "parallel")),
        name="qg_project_t",
    )(x, g.reshape(1, D), w, gain, cos, sin, bd)


def _nsa_t_kernel(qc_ref, qr_ref, gate_ref, cmp_ref, sel_ref, win_ref, wmap_ref, o_ref,
                  selneg_ref, m_ref, l_ref, acc_ref,
                  *, qb, n_sel, n_selp, n_cmp, q0, w_off, w_rows, l_win):
    qi = pl.program_id(1)
    group = N_HEADS // N_KV
    R = group * qb
    blocks_per_tile = KEY_TILE // SEL_BLOCK
    q_lo = q0 + qi * qb
    qpos_q = q_lo + lax.broadcasted_iota(jnp.int32, (1, qb), 1)
    n_kt = (q_lo + qb - 1) // KEY_TILE + 1
    w_start = pl.multiple_of(jnp.clip(q_lo - WINDOW - w_off, 0, l_win - w_rows), LANES)
    gate = gate_ref[0, 0]

    def lanes4(a):
        return jnp.concatenate([a] * group, axis=1)

    m_idx = lax.broadcasted_iota(jnp.int32, (n_cmp, qb), 0)
    cvalid = (m_idx >= 1) & ((m_idx - 1) * CMP_STRIDE + CMP_LEN - 1 <= qpos_q)
    cbias = lanes4(jnp.where(cvalid, 0.0, NEG_INF))
    any_c = lanes4(qpos_q >= CMP_LEN - 1)
    blk = lax.broadcasted_iota(jnp.int32, (n_selp, qb), 0)
    blk_f = blk.astype(F32)
    cur = qpos_q >> 6
    forced = (blk == 0) | (blk == cur) | (blk == cur - 1)
    reach = blk * SEL_BLOCK <= qpos_q
    real = blk < n_sel
    wpos = w_off + w_start + lax.broadcasted_iota(jnp.int32, (w_rows, qb), 0)
    wbias = lanes4(jnp.where((wpos <= qpos_q) & (wpos > qpos_q - WINDOW) & (wpos >= 0), 0.0, NEG_INF))
    row_t = lax.broadcasted_iota(jnp.int32, (KEY_TILE, qb), 0)

    for k in range(N_KV):
        ckv = cmp_ref[0, k]
        s = jnp.dot(ckv, qc_ref[0, k, 0], preferred_element_type=F32) + cbias
        e = jnp.exp2(s - jnp.max(s, axis=0, keepdims=True))
        den = jnp.maximum(jnp.sum(e, axis=0, keepdims=True), TINY)
        p = e * jnp.where(any_c, 1.0 / den, 0.0)
        oc = _tn_dot(ckv, p.astype(BF16))
        psum = p[:, 0:qb]
        for gi in range(1, group):
            psum = psum + p[:, gi * qb:(gi + 1) * qb]
        hi = psum.astype(BF16)
        lo = (psum - hi.astype(F32)).astype(BF16)
        imp = (jnp.dot(wmap_ref[...], hi, preferred_element_type=F32)
               + jnp.dot(wmap_ref[...], lo, preferred_element_type=F32))
        v = jnp.where(forced, SEL_FORCE, jnp.where(reach, imp, SEL_NEG))
        v = jnp.where(real, v, -jnp.inf)

        def pick_one(_, carry):
            v, sel = carry
            m = jnp.max(v, axis=0, keepdims=True)
            first = jnp.min(jnp.where(v == m, blk_f, float(n_selp)), axis=0, keepdims=True)
            pick = blk_f == first
            sel = jnp.where(pick & (m > 0.5 * SEL_NEG), 0.0, sel)
            return jnp.where(pick, -jnp.inf, v), sel

        _, sel = lax.fori_loop(0, SEL_TOPK, pick_one, (v, jnp.full((n_selp, qb), NEG_INF, F32)))
        selneg_ref[...] = sel

        qr = qr_ref[0, k, 0]
        m_ref[...] = jnp.full(m_ref.shape, NEG_INF, F32)
        l_ref[...] = jnp.zeros(l_ref.shape, F32)
        acc_ref[...] = jnp.zeros(acc_ref.shape, F32)

        def tile(kt, carry):
            start = pl.multiple_of(kt * KEY_TILE, KEY_TILE)
            kv = sel_ref[0, k, pl.ds(start, KEY_TILE), :]
            s = jnp.dot(kv, qr, preferred_element_type=F32)
            pieces = [jnp.broadcast_to(selneg_ref[pl.ds(kt * blocks_per_tile + j, 1), :], (SEL_BLOCK, qb))
                      for j in range(blocks_per_tile)]
            bias = jnp.concatenate(pieces, axis=0) + jnp.where(start + row_t <= qpos_q, 0.0, NEG_INF)
            s = s + lanes4(bias)
            m_old = m_ref[...]
            m_new = jnp.maximum(m_old, jnp.max(s, axis=0, keepdims=True))
            alpha = jnp.exp2(m_old - m_new)
            p = jnp.exp2(s - m_new)
            l_ref[...] = alpha * l_ref[...] + jnp.sum(p, axis=0, keepdims=True)
            acc_ref[...] = alpha * acc_ref[...] + _tn_dot(kv, p.astype(BF16))
            m_ref[...] = m_new
            return carry

        lax.fori_loop(0, n_kt, tile, 0)
        o_s = acc_ref[...] * (1.0 / jnp.maximum(l_ref[...], TINY))

        wkv = win_ref[0, k, pl.ds(w_start, w_rows), :]
        s = jnp.dot(wkv, qr, preferred_element_type=F32) + wbias
        e = jnp.exp2(s - jnp.max(s, axis=0, keepdims=True))
        den = jnp.maximum(jnp.sum(e, axis=0, keepdims=True), TINY)
        o_w = _tn_dot(wkv, e.astype(BF16)) * (1.0 / den)

        for gi in range(group):
            hd = k * group + gi
            sl = slice(gi * qb, (gi + 1) * qb)
            o = gate[3 * hd:3 * hd + 1, :] * oc[:, sl]
            o = o + gate[3 * hd + 1:3 * hd + 2, :] * o_s[:, sl]
            o = o + gate[3 * hd + 2:3 * hd + 3, :] * o_w[:, sl]
            o_ref[0, :, LANES * hd:LANES * (hd + 1)] = o.T.astype(o_ref.dtype)


def nsa_attend_t(qc, qr, gates, cmp_p, sel_p, win_p, T, qb, q0, w_off, n_sel):
    B = qc.shape[0]
    group = N_HEADS // N_KV
    R = group * qb
    n_cmp = cmp_p.shape[2]
    n_selp = -(-n_sel // LANES) * LANES
    l_sel = sel_p.shape[2]
    l_win = win_p.shape[2]
    w_rows = WINDOW + max(qb, LANES)
    assert l_win >= w_rows and l_sel >= ((q0 + T - 1) // KEY_TILE + 1) * KEY_TILE
    wmap = _overlap_map(n_cmp, n_selp)
    kern = functools.partial(_nsa_t_kernel, qb=qb, n_sel=n_sel, n_selp=n_selp, n_cmp=n_cmp,
                             q0=q0, w_off=w_off, w_rows=w_rows, l_win=l_win)
    qspec = pl.BlockSpec((1, N_KV, 1, LANES, R), lambda b, i: (b, 0, i, 0, 0))
    resident = dict(pipeline_mode=pl.Buffered(1))
    return pl.pallas_call(
        kern,
        out_shape=jax.ShapeDtypeStruct((B, T, N_HEADS * LANES), BF16),
        grid=(B, T // qb),
        in_specs=[
            qspec, qspec,
            pl.BlockSpec((1, 1, LANES, qb), lambda b, i: (b, i, 0, 0)),
            pl.BlockSpec((1, N_KV, n_cmp, LANES), lambda b, i: (b, 0, 0, 0)),
            pl.BlockSpec((1, N_KV, l_sel, LANES), lambda b, i: (b, 0, 0, 0), **resident),
            pl.BlockSpec((1, N_KV, l_win, LANES), lambda b, i: (b, 0, 0, 0), **resident),
            pl.BlockSpec((n_selp, n_cmp), lambda b, i: (0, 0)),
        ],
        out_specs=pl.BlockSpec((1, qb, N_HEADS * LANES), lambda b, i: (b, i, 0)),
        scratch_shapes=[
            pltpu.VMEM((n_selp, qb), F32),
            pltpu.VMEM((1, R), F32), pltpu.VMEM((1, R), F32), pltpu.VMEM((LANES, R), F32),
        ],
        compiler_params=_cparams(("parallel", "arbitrary")),
        name="nsa_attend_t",
    )(qc, qr, gates, cmp_p, sel_p, win_p, wmap)


def _nsa_fast_kernel(shift_ref, qc_ref, qr_ref, gate_ref, cmp_ref, sel_ref, win_ref, wmap_ref, hot_ref,
                     o_ref, qaug_ref, oc_ref, ow_ref, l_ref, acc_ref,
                     *, qb, n_sel, n_selp, n_cmp, q0, w_off, w_rows, l_win):
    qi = pl.program_id(1)
    group = N_HEADS // N_KV
    R = group * qb
    q_lo = q0 + qi * qb
    qpos_q = q_lo + lax.broadcasted_iota(jnp.int32, (1, qb), 1)
    n_kt = (q_lo + qb - 1) // KEY_TILE + 1
    w_start = pl.multiple_of(jnp.clip(q_lo - WINDOW - w_off, 0, l_win - w_rows), LANES)
    shift_c = shift_ref[0]
    shift_s = shift_ref[1]
    shift_w = shift_ref[2]

    def lanes4(a):
        return jnp.concatenate([a] * group, axis=1)

    m_idx = lax.broadcasted_iota(jnp.int32, (n_cmp, qb), 0)
    cvalid = (m_idx >= 1) & ((m_idx - 1) * CMP_STRIDE + CMP_LEN - 1 <= qpos_q)
    cbias = lanes4(jnp.where(cvalid, -shift_c, NEG_INF))
    any_c = lanes4(qpos_q >= CMP_LEN - 1)
    blk = lax.broadcasted_iota(jnp.int32, (n_selp, qb), 0)
    blk_f = blk.astype(F32)
    cur = qpos_q >> 6
    forced = (blk == 0) | (blk == cur) | (blk == cur - 1)
    reach = blk * SEL_BLOCK <= qpos_q
    real = blk < n_sel

    imps = []
    for k in range(N_KV):
        ckv = cmp_ref[0, k]
        e = jnp.exp2(jnp.dot(ckv, qc_ref[0, k, 0], preferred_element_type=F32) + cbias)
        den = jnp.maximum(jnp.sum(e, axis=0, keepdims=True), TINY)
        p = e * jnp.where(any_c, 1.0 / den, 0.0)
        oc_ref[k] = _tn_dot(ckv, p.astype(BF16))
        psum = p[:, 0:qb]
        for gi in range(1, group):
            psum = psum + p[:, gi * qb:(gi + 1) * qb]
        hi = psum.astype(BF16)
        lo = (psum - hi.astype(F32)).astype(BF16)
        imp = (jnp.dot(wmap_ref[...], hi, preferred_element_type=F32)
               + jnp.dot(wmap_ref[...], lo, preferred_element_type=F32))
        v = jnp.where(forced, SEL_FORCE, jnp.where(reach, imp, SEL_NEG))
        imps.append(jnp.where(real, v, -jnp.inf))

    def pick_one(_, vs):
        out = []
        for v in vs:
            m = jnp.max(v, axis=0, keepdims=True)
            first = jnp.min(jnp.where(v == m, blk_f, float(n_selp)), axis=0, keepdims=True)
            out.append(jnp.where(blk_f == first, -jnp.inf, v))
        return tuple(out)

    wpos = w_off + w_start + lax.broadcasted_iota(jnp.int32, (w_rows, qb), 0)
    wvalid = (wpos <= qpos_q) & (wpos > qpos_q - WINDOW) & (wpos >= 0)
    wbias = lanes4(jnp.where(wvalid, -shift_w, NEG_INF))
    for k in range(N_KV):
        wkv = win_ref[0, k, pl.ds(w_start, w_rows), :]
        e = jnp.exp2(jnp.dot(wkv, qr_ref[0, k, 0], preferred_element_type=F32) + wbias)
        den = jnp.maximum(jnp.sum(e, axis=0, keepdims=True), TINY)
        ow_ref[k] = _tn_dot(wkv, e.astype(BF16)) * (1.0 / den)

    n_forced = 3
    marked = lax.fori_loop(0, SEL_TOPK - n_forced, pick_one,
                           tuple(jnp.where(forced, -jnp.inf, v) for v in imps), unroll=True)
    for k in range(N_KV):
        sel = jnp.where((marked[k] == -jnp.inf) & (imps[k] > 0.5 * SEL_NEG), -shift_s, NEG_INF)
        qaug_ref[k, 0:LANES, :] = qr_ref[0, k, 0]
        qaug_ref[k, LANES:, :] = lanes4(sel).astype(BF16)

    l_ref[...] = jnp.zeros(l_ref.shape, F32)
    acc_ref[...] = jnp.zeros(acc_ref.shape, F32)

    def tile(start, rows, causal_bias):
        hot = hot_ref[pl.ds(start, rows), :]
        for k in range(N_KV):
            kv = sel_ref[0, k, pl.ds(start, rows), :]
            s = jnp.dot(jnp.concatenate([kv, hot], axis=1), qaug_ref[k], preferred_element_type=F32)
            if causal_bias is not None:
                s = s + causal_bias
            p = jnp.exp2(s)
            l_ref[k] += jnp.sum(p.reshape(rows // 8, 8, R), axis=0)
            acc_ref[k] += _tn_dot(kv, p.astype(BF16))

    def double_tile(i, carry):
        tile(pl.multiple_of(i * 2 * KEY_TILE, 2 * KEY_TILE), 2 * KEY_TILE, None)
        return carry

    n_below = n_kt - 1
    lax.fori_loop(0, n_below // 2, double_tile, 0)

    @pl.when(n_below % 2 == 1)
    def _():
        tile(pl.multiple_of((n_below - 1) * KEY_TILE, KEY_TILE), KEY_TILE, None)

    d_start = pl.multiple_of(n_below * KEY_TILE, KEY_TILE)
    row_t = lax.broadcasted_iota(jnp.int32, (KEY_TILE, qb), 0)
    tile(d_start, KEY_TILE, lanes4(jnp.where(d_start + row_t <= qpos_q, 0.0, NEG_INF)))

    gate = gate_ref[0, 0]
    for k in range(N_KV):
        o_w = ow_ref[k, HEAD_DIM:, :]
        l_s = jnp.maximum(jnp.sum(l_ref[k], axis=0, keepdims=True), TINY)
        o_s = acc_ref[k, HEAD_DIM:, :] * (1.0 / l_s)
        oc = oc_ref[k, HEAD_DIM:, :]
        heads = []
        for gi in range(group):
            hd = k * group + gi
            sl = slice(gi * qb, (gi + 1) * qb)
            o = gate[3 * hd:3 * hd + 1, :] * oc[:, sl]
            o = o + gate[3 * hd + 1:3 * hd + 2, :] * o_s[:, sl]
            heads.append(o + gate[3 * hd + 2:3 * hd + 3, :] * o_w[:, sl])
        for pr in range(group // 2):
            o2 = jnp.concatenate(heads[2 * pr:2 * pr + 2], axis=0)
            col = (k * group + 2 * pr) * HEAD_DIM
            o_ref[0, :, col:col + LANES] = o2.T.astype(o_ref.dtype)


def nsa_attend_fast(shifts, qc, qr, gates, cmp_p, sel_p, win_p, T, qb, q0, w_off, n_sel):
    B = qc.shape[0]
    group = N_HEADS // N_KV
    R = group * qb
    n_cmp = cmp_p.shape[2]
    n_selp = -(-n_sel // LANES) * LANES
    l_sel = sel_p.shape[2]
    l_win = win_p.shape[2]
    w_rows = WINDOW + max(qb, LANES)
    assert l_win >= w_rows and l_sel >= ((q0 + T - 1) // KEY_TILE + 1) * KEY_TILE
    assert n_sel >= SEL_TOPK and q0 % qb == 0 and KEY_TILE % qb == 0
    wmap = _overlap_map(n_cmp, n_selp)
    hot = np.zeros((l_sel, n_selp), np.float32)
    hot[np.arange(l_sel), np.arange(l_sel) // SEL_BLOCK] = 1.0
    hot = jnp.asarray(hot, BF16)
    kern = functools.partial(_nsa_fast_kernel, qb=qb, n_sel=n_sel, n_selp=n_selp, n_cmp=n_cmp,
                             q0=q0, w_off=w_off, w_rows=w_rows, l_win=l_win)
    qspec = pl.BlockSpec((1, N_KV, 1, LANES, R), lambda b, i, sh: (b, 0, i, 0, 0))
    resident = dict(pipeline_mode=pl.Buffered(1))
    grid_spec = pltpu.PrefetchScalarGridSpec(
        num_scalar_prefetch=1,
        grid=(B, T // qb),
        in_specs=[
            qspec, qspec,
            pl.BlockSpec((1, 1, LANES, qb), lambda b, i, sh: (b, i, 0, 0)),
            pl.BlockSpec((1, N_KV, n_cmp, LANES), lambda b, i, sh: (b, 0, 0, 0)),
            pl.BlockSpec((1, N_KV, l_sel, LANES), lambda b, i, sh: (b, 0, 0, 0), **resident),
            pl.BlockSpec((1, N_KV, l_win, LANES), lambda b, i, sh: (b, 0, 0, 0), **resident),
            pl.BlockSpec((n_selp, n_cmp), lambda b, i, sh: (0, 0)),
            pl.BlockSpec((l_sel, n_selp), lambda b, i, sh: (0, 0), **resident),
        ],
        out_specs=pl.BlockSpec((1, qb, N_HEADS * HEAD_DIM), lambda b, i, sh: (b, i, 0)),
        scratch_shapes=[
            pltpu.VMEM((N_KV, 2 * LANES, R), BF16),
            pltpu.VMEM((N_KV, LANES, R), F32),
            pltpu.VMEM((N_KV, LANES, R), F32),
            pltpu.VMEM((N_KV, 8, R), F32),
            pltpu.VMEM((N_KV, LANES, R), F32),
        ],
    )
    return pl.pallas_call(
        kern,
        out_shape=jax.ShapeDtypeStruct((B, T, N_HEADS * HEAD_DIM), BF16),
        grid_spec=grid_spec,
        compiler_params=_cparams(("parallel", "arbitrary")),
        name="nsa_attend_fast",
    )(shifts, qc, qr, gates, cmp_p, sel_p, win_p, wmap, hot)


def _nsa_dec_kernel(pt_ref, *refs, n_pg, qb, n_sel, n_selp, n_cmp, q0, w_off):
    pages = refs[:n_pg]
    (qcp_ref, qbd_ref, gate_ref, cmp_ref, new_ref, win_ref, wmap_ref, fold_ref, hot_ref,
     o_ref, qaug_ref, oc_ref, m_ref, l_ref, acc_ref) = refs[n_pg:]
    g = pl.program_id(1)
    nk = N_KV * HEAD_DIM
    t_new = new_ref.shape[1]
    lane = lax.broadcasted_iota(jnp.int32, (1, LANES), 1)
    qpos = q0 + (lane & (qb - 1))

    @pl.when(g == 0)
    def _first():
        s = jnp.dot(cmp_ref[0, 0], qcp_ref[0, 0], preferred_element_type=F32)
        for k in range(1, N_KV):
            s = s + jnp.dot(cmp_ref[0, k], qcp_ref[0, k], preferred_element_type=F32)
        m_idx = lax.broadcasted_iota(jnp.int32, (n_cmp, LANES), 0)
        cvalid = (m_idx >= 1) & ((m_idx - 1) * CMP_STRIDE + CMP_LEN - 1 <= qpos)
        s = s + jnp.where(cvalid, 0.0, NEG_INF)
        e = jnp.exp2(s - jnp.max(s, axis=0, keepdims=True))
        den = jnp.maximum(jnp.sum(e, axis=0, keepdims=True), TINY)
        p = e * jnp.where(qpos >= CMP_LEN - 1, 1.0 / den, 0.0)
        pb = p.astype(BF16)
        for k in range(N_KV):
            oc_ref[k * HEAD_DIM:(k + 1) * HEAD_DIM, :] = _tn_dot(cmp_ref[0, k], pb)[HEAD_DIM:, :]
        fold = fold_ref[...]
        p_lo = (p - pb.astype(F32)).astype(BF16)
        psum = jnp.dot(pb, fold, preferred_element_type=F32) + jnp.dot(p_lo, fold, preferred_element_type=F32)
        hi = psum.astype(BF16)
        lo = (psum - hi.astype(F32)).astype(BF16)
        imp = (jnp.dot(wmap_ref[...], hi, preferred_element_type=F32)
               + jnp.dot(wmap_ref[...], lo, preferred_element_type=F32))
        blk = lax.broadcasted_iota(jnp.int32, (n_selp, LANES), 0)
        blk_f = blk.astype(F32)
        cur = qpos >> 6
        forced = (blk == 0) | (blk == cur) | (blk == cur - 1)
        v = jnp.where(forced, SEL_FORCE, jnp.where(blk * SEL_BLOCK <= qpos, imp, SEL_NEG))
        v = jnp.where(blk < n_sel, v, -jnp.inf)

        def pick_one(_, carry):
            v, sel = carry
            m = jnp.max(v, axis=0, keepdims=True)
            first = jnp.min(jnp.where(v == m, blk_f, float(n_selp)), axis=0, keepdims=True)
            pick = blk_f == first
            sel = jnp.where(pick & (m > 0.5 * SEL_NEG), 0.0, sel)
            return jnp.where(pick, -jnp.inf, v), sel

        _, sel = lax.fori_loop(0, SEL_TOPK, pick_one, (v, jnp.full((n_selp, LANES), NEG_INF, F32)))
        qaug_ref[0:nk, :] = qbd_ref[0]
        qaug_ref[nk:, :] = sel.astype(BF16)
        m_ref[...] = jnp.full(m_ref.shape, NEG_INF, F32)
        l_ref[...] = jnp.zeros(l_ref.shape, F32)
        acc_ref[...] = jnp.zeros(acc_ref.shape, F32)

    def attend(kx, vx, hot, bias):
        s = jnp.dot(jnp.concatenate([kx, hot], axis=1), qaug_ref[...], preferred_element_type=F32)
        if bias is not None:
            s = s + bias
        m_old = m_ref[...]
        m_new = jnp.maximum(m_old, jnp.max(s, axis=0, keepdims=True))
        alpha = jnp.exp2(m_old - m_new)
        p = jnp.exp2(s - m_new)
        l_ref[...] = alpha * l_ref[...] + jnp.sum(p.reshape(p.shape[0] // 8, 8, LANES), axis=0)
        acc_ref[...] = alpha * acc_ref[...] + _tn_dot(vx, p.astype(BF16))
        m_ref[...] = m_new

    x = jnp.concatenate([pages[i][0] for i in range(n_pg)], axis=0).astype(BF16)
    start = pl.multiple_of(g * (n_pg * PAGE), n_pg * PAGE)
    attend(x[:, :nk], x[:, nk:], hot_ref[pl.ds(start, n_pg * PAGE), :], None)

    @pl.when(g == pl.num_programs(1) - 1)
    def _last():
        pad = jnp.zeros((HALO - t_new, 2 * nk), F32)
        row = lax.broadcasted_iota(jnp.int32, (HALO, LANES), 0)
        new_ok = (row < t_new) & (q0 + row <= qpos)
        xn = jnp.concatenate([new_ref[0, :, 2 * nk:4 * nk], pad], axis=0).astype(BF16)
        hot_new = jnp.where(
            lax.broadcasted_iota(jnp.int32, (HALO, n_selp), 1) == q0 // SEL_BLOCK, 1.0, 0.0).astype(BF16)
        attend(xn[:, :nk], xn[:, nk:], hot_new, jnp.where(new_ok, 0.0, NEG_INF))

        qbd = qbd_ref[0]
        xw = win_ref[0].astype(BF16)
        wrow = lax.broadcasted_iota(jnp.int32, (xw.shape[0], LANES), 0)
        wpos = w_off + wrow
        wvalid = (wpos <= qpos) & (wpos > qpos - WINDOW) & (wpos >= 0)
        s_c = jnp.dot(xw[:, :nk], qbd, preferred_element_type=F32) + jnp.where(wvalid, 0.0, NEG_INF)
        xwn = jnp.concatenate([new_ref[0, :, 4 * nk:6 * nk], pad], axis=0).astype(BF16)
        npos = q0 + row
        nvalid = (row < t_new) & (npos <= qpos) & (npos > qpos - WINDOW)
        s_n = jnp.dot(xwn[:, :nk], qbd, preferred_element_type=F32) + jnp.where(nvalid, 0.0, NEG_INF)
        m_w = jnp.maximum(jnp.max(s_c, axis=0, keepdims=True), jnp.max(s_n, axis=0, keepdims=True))
        e_c = jnp.exp2(s_c - m_w)
        e_n = jnp.exp2(s_n - m_w)
        den = jnp.sum(e_c, axis=0, keepdims=True) + jnp.sum(e_n, axis=0, keepdims=True)
        o_w = _tn_dot(xw[:, nk:], e_c.astype(BF16)) + _tn_dot(xwn[:, nk:], e_n.astype(BF16))
        o_w = o_w * (1.0 / jnp.maximum(den, TINY))
        l_s = jnp.maximum(jnp.sum(l_ref[...], axis=0, keepdims=True), TINY)
        o_s = acc_ref[...] * (1.0 / l_s)
        gate = gate_ref[0]
        o_ref[0] = gate[0:1, :] * oc_ref[...] + gate[1:2, :] * o_s + gate[2:3, :] * o_w


def _dec_page_index(b, g, pt, *, i, n_pg):
    return (pt[b, g * n_pg + i], 0, 0)


def nsa_attend_dec(table, qc, qr, gates, cmp_p, sel_pages, kvp, cache_win, qb, q0, n_sel):
    B, n_pages = table.shape
    group = N_HEADS // N_KV
    assert N_KV * group * qb == LANES
    nk = N_KV * HEAD_DIM
    n_pg = 16 if n_pages % 16 == 0 else n_pages
    n_cmp = cmp_p.shape[2]
    n_selp = -(-n_sel // LANES) * LANES
    wl = cache_win.shape[1]
    eye = jnp.eye(N_KV, dtype=F32)

    def spread(q):
        qt = q[:, :, 0, :, :HEAD_DIM].astype(F32).transpose(0, 1, 3, 2)
        return jnp.einsum("bkdr,kj->bkdjr", qt, eye).reshape(B, N_KV, HEAD_DIM, LANES)

    qc_pad = jnp.pad(spread(qc), ((0, 0), (0, 0), (0, LANES - HEAD_DIM), (0, 0))).astype(BF16)
    q_bd = spread(qr).reshape(B, nk, LANES).astype(BF16)
    gate_l = gates[:, :, :3 * N_HEADS].reshape(B, qb, N_KV, group, 3).transpose(0, 4, 2, 3, 1)
    gate_l = gate_l.reshape(B, 3, LANES)
    wmap = _overlap_map(n_cmp, n_selp)
    lane = np.arange(LANES)
    fold = jnp.asarray((lane[:, None] // (group * qb) == lane[None, :] // (group * qb))
                       & (lane[:, None] % qb == lane[None, :] % qb), BF16)
    l_past = n_pages * PAGE
    hot = np.zeros((l_past, n_selp), np.float32)
    hot[np.arange(l_past), np.arange(l_past) // SEL_BLOCK] = 1.0
    hot = jnp.asarray(hot, BF16)
    kern = functools.partial(_nsa_dec_kernel, n_pg=n_pg, qb=qb, n_sel=n_sel, n_selp=n_selp,
                             n_cmp=n_cmp, q0=q0, w_off=q0 - wl)
    in_specs = [pl.BlockSpec((1, PAGE, 2 * nk), functools.partial(_dec_page_index, i=i, n_pg=n_pg))
                for i in range(n_pg)]
    in_specs += [
        pl.BlockSpec((1, N_KV, LANES, LANES), lambda b, g, pt: (b, 0, 0, 0)),
        pl.BlockSpec((1, nk, LANES), lambda b, g, pt: (b, 0, 0)),
        pl.BlockSpec((1, 3, LANES), lambda b, g, pt: (b, 0, 0)),
        pl.BlockSpec((1, N_KV, n_cmp, LANES), lambda b, g, pt: (b, 0, 0, 0)),
        pl.BlockSpec((1, qb, 6 * nk), lambda b, g, pt: (b, 0, 0)),
        pl.BlockSpec((1, wl, 2 * nk), lambda b, g, pt: (b, 0, 0)),
        pl.BlockSpec((n_selp, n_cmp), lambda b, g, pt: (0, 0)),
        pl.BlockSpec((LANES, LANES), lambda b, g, pt: (0, 0)),
        pl.BlockSpec((l_past, n_selp), lambda b, g, pt: (0, 0), pipeline_mode=pl.Buffered(1)),
    ]
    grid_spec = pltpu.PrefetchScalarGridSpec(
        num_scalar_prefetch=1,
        grid=(B, n_pages // n_pg),
        in_specs=in_specs,
        out_specs=pl.BlockSpec((1, nk, LANES), lambda b, g, pt: (b, 0, 0)),
        scratch_shapes=[
            pltpu.VMEM((nk + n_selp, LANES), BF16),
            pltpu.VMEM((nk, LANES), F32),
            pltpu.VMEM((1, LANES), F32),
            pltpu.VMEM((8, LANES), F32),
            pltpu.VMEM((nk, LANES), F32),
        ],
    )
    o_t = pl.pallas_call(
        kern,
        out_shape=jax.ShapeDtypeStruct((B, nk, LANES), F32),
        grid_spec=grid_spec,
        compiler_params=_cparams(("parallel", "arbitrary")),
        name="nsa_attend_dec",
    )(table, *([sel_pages] * n_pg), qc_pad, q_bd, gate_l, cmp_p, kvp, cache_win, wmap, fold, hot)
    o6 = o_t.reshape(B, N_KV, HEAD_DIM, N_KV, group, qb)
    o5 = jnp.einsum("bkdkgq->bqkgd", o6)
    return o5.reshape(B * qb, N_HEADS * HEAD_DIM)


def _rope_tables(pos, half):
    inv = jnp.exp(-math.log(ROPE_THETA) * jnp.arange(half, dtype=F32) / half)
    ang = pos.astype(F32)[:, None] * inv[None, :]
    return jnp.cos(ang), jnp.sin(ang)


def _prep_weights(ret_w_in, ret_w_out, ffn_w_in, ffn_w_out, kv_w, kv_knorm, cmp_w1, cmp_w2,
                  nsa_w_qg, nsa_qnorm, nsa_w_o):
    nq = N_HEADS * HEAD_DIM
    qg_pad = nq + LANES - nsa_w_qg.shape[2]
    w_qg = jnp.pad(nsa_w_qg, ((0, 0), (0, 0), (0, qg_pad))).astype(BF16)
    R = CMP_LEN // CMP_STRIDE
    w1 = cmp_w1.reshape(2, R, CMP_STRIDE, HEAD_DIM, CMP_HID).transpose(0, 2, 3, 1, 4)
    w1 = w1.reshape(2, CMP_STRIDE, HEAD_DIM, R * CMP_HID)
    w1dup = jnp.concatenate([w1, w1], axis=2).astype(BF16)
    w1dup = w1dup.reshape(2, CMP_STRIDE // 2, 2 * LANES, R * CMP_HID)
    z = jnp.zeros((CMP_HID, HEAD_DIM), F32)
    w2p = jnp.stack([jnp.concatenate([cmp_w2[0], z], axis=1),
                     jnp.concatenate([z, cmp_w2[1]], axis=1)]).astype(BF16)
    ones = jnp.ones((HEAD_DIM,), F32)
    return dict(
        ret_w_in=ret_w_in.astype(BF16), ret_w_out=ret_w_out.astype(BF16),
        ffn_w_in=ffn_w_in.astype(BF16), ffn_w_out=ffn_w_out.astype(BF16),
        kv_w=kv_w.astype(BF16), w_qg=w_qg, w_o_raw=nsa_w_o.astype(BF16),
        kv_gain=jnp.tile(kv_knorm, (1, 2)),
        cmp_gain=jnp.concatenate([kv_knorm[0], ones]).reshape(1, LANES),
        q_gain2=jnp.tile(nsa_qnorm, (1, 2)) * (HEAD_DIM ** -0.5 * math.log2(math.e)),
        score_bound=(1.05 * HEAD_DIM ** 0.5 * math.log2(math.e))
        * jnp.max(jnp.abs(nsa_qnorm), axis=1)[:, None] * jnp.max(jnp.abs(kv_knorm), axis=1)[None, :],
        w1dup=w1dup, w2p=w2p,
        bd=jnp.asarray(np.kron(np.eye(2), np.ones((HEAD_DIM, HEAD_DIM))), BF16),
    )


def _trunk(x, past_len, ret_s0, conv0, ctx, W, P):
    B, T, D = x.shape
    M = B * T
    depth = P["norm_mix"].shape[0]
    n_a = P["ret_w_in"].shape[0]
    pos = past_len + jnp.arange(T)
    cos_r, sin_r = _rope_tables(pos, RET_DK // 2)
    c32, s32 = _rope_tables(pos, HEAD_DIM // 2)
    cos_n = jnp.tile(c32, (1, 4))
    sin_n = jnp.concatenate([-s32, s32, -s32, s32], axis=1)
    lg = jnp.log1p(-jnp.exp2(-5.0 - jnp.arange(RET_HEADS, dtype=F32)))
    L = RET_CHUNK if T % RET_CHUNK == 0 else T
    gl = jnp.exp(L * lg)
    tm = min(512, M)
    tf = 1024 if T % 1024 == 0 else 512
    tb = min(512, T)
    tq = min(512, T)
    qb = next((c for c in (2 * Q_BLOCK, Q_BLOCK) if T % c == 0), T)
    tt = min(256, T)

    x2 = x.reshape(M, D)
    ret_states, conv_states = [], []
    for layer in range(depth):
        if layer == n_a:
            kvp, selp, winp = kv_project(x2.reshape(B, T, D), P["kv_norm"], W["kv_w"], W["kv_gain"],
                                         cos_n, sin_n, W["bd"], tq)
            nk2 = 2 * N_KV * HEAD_DIM
            if ctx is None:
                table = jnp.arange(M // PAGE, dtype=jnp.int32).reshape(B, T // PAGE)
                cmp_p = compress(kvp.reshape(M // PAGE, PAGE, 3 * nk2), table,
                                 W["w1dup"], W["pb"], W["w2p"], W["cmp_gain"])
                sel_p, win_p = selp, winp
                w_off = 0
            else:
                cache_cmp, cache_sel, cache_win, table = ctx
                n_pool = cache_cmp.shape[0]
                cmp_p = compress(cache_cmp.reshape(n_pool, PAGE, nk2), table,
                                 W["w1dup"], W["pb"], W["w2p"], W["cmp_gain"])
                sel_pages = cache_sel.reshape(n_pool, PAGE, nk2)
                win_rows = cache_win.reshape(B, cache_win.shape[1], nk2)
            n_sel = -(-(past_len + T) // SEL_BLOCK)
        h_norm = P["norm_mix"][layer]
        if layer < n_a:
            proj = norm_matmul(x2, h_norm, W["ret_w_in"][layer], min(1024, M), 1024)
            og, s_new = retention(proj.reshape(B, T, -1), ret_s0[layer], cos_r, sin_r, lg, gl, L, tb)
            ret_states.append(s_new)
            x2 = matmul_res(og.reshape(M, -1), W["ret_w_out"][layer], x2, tm)
        else:
            j = layer - n_a
            if qb % LANES == 0:
                qc, qr, gates = qg_project_t(x2.reshape(B, T, D), h_norm, W["w_qg"][j],
                                             W["q_gain2"][j:j + 1], cos_n, sin_n, W["bd"], tq, qb)
                shifts = W["score_bound"][j]
                args = (qc, qr, gates, cmp_p, sel_p, win_p)
                o = lax.cond(
                    jnp.max(shifts) <= 30.0,
                    lambda a: nsa_attend_fast(shifts, *a, T, qb, past_len, w_off, n_sel),
                    lambda a: nsa_attend_t(*a, T, qb, past_len, w_off, n_sel).reshape(
                        B, T, N_HEADS, LANES)[..., HEAD_DIM:].reshape(B, T, N_HEADS * HEAD_DIM),
                    args)
                x2 = matmul_res(o.reshape(M, -1), W["w_o_raw"][j], x2, tm)
            else:
                qc, qr, gates = qg_project(x2.reshape(B, T, D), h_norm, W["w_qg"][j],
                                           W["q_gain2"][j:j + 1], cos_n, sin_n, W["bd"], tq, qb)
                o = nsa_attend_dec(table, qc, qr, gates, cmp_p, sel_pages, kvp, win_rows,
                                   qb, past_len, n_sel)
                x2 = matmul_res(o, W["w_o_raw"][j], x2, tm)
        if T % tf == 0:
            act, tail = ffn_in(x2, P["norm_ffn"][layer], W["ffn_w_in"][layer], conv0[layer],
                               P["ffn_conv_w"][layer], P["ffn_conv_b"][layer], T, tf, 256)
            conv_states.append(tail)
        else:
            proj = norm_matmul(x2, P["norm_ffn"][layer], W["ffn_w_in"][layer], tm, 512)
            proj3 = proj.reshape(B, T, 2 * D_FF)
            act = ffn_mid(proj3, conv0[layer], P["ffn_conv_w"][layer], P["ffn_conv_b"][layer], tt)
            conv_states.append(proj3[:, T - 2:, :D_FF])
        x2 = matmul_res(act.reshape(M, D_FF), W["ffn_w_out"][layer], x2, tm)

    nk = N_KV * HEAD_DIM
    new_cmp = kvp[:, :, 0:2 * nk].reshape(B, T, 2, N_KV, HEAD_DIM)
    new_sel = kvp[:, :, 2 * nk:4 * nk].reshape(B, T, 2, N_KV, HEAD_DIM)
    new_win = kvp[:, :, 4 * nk:6 * nk].reshape(B, T, 2, N_KV, HEAD_DIM)
    return (x2.reshape(B, T, D), jnp.stack(ret_states), jnp.stack(conv_states),
            new_cmp, new_sel, new_win)


def kernel(x_prompt, x_sample, cache_cmp_kv, cache_sel_kv, cache_win_kv, state_ret, state_conv,
           page_table, norm_mix, norm_ffn, ret_w_in, ret_w_out, ffn_w_in, ffn_conv_w, ffn_conv_b,
           ffn_w_out, kv_norm, kv_w, kv_knorm, cmp_pos, cmp_w1, cmp_w2, nsa_w_qg, nsa_qnorm, nsa_w_o):
    W = _prep_weights(ret_w_in, ret_w_out, ffn_w_in, ffn_w_out, kv_w, kv_knorm, cmp_w1, cmp_w2,
                      nsa_w_qg, nsa_qnorm, nsa_w_o)
    W["pb"] = pos_bias(cmp_pos, cmp_w1)
    P = dict(norm_mix=norm_mix, norm_ffn=norm_ffn, ret_w_in=ret_w_in, ffn_conv_w=ffn_conv_w,
             ffn_conv_b=ffn_conv_b, kv_norm=kv_norm)
    depth = norm_mix.shape[0]
    n_a = ret_w_in.shape[0]
    B, T, _ = x_prompt.shape
    zero_ret = jnp.zeros((n_a, B, RET_HEADS, RET_DK, RET_DV), F32)
    zero_conv = jnp.zeros((depth, B, 2, D_FF), F32)
    y_p, ret_p, conv_p, cmp_p, sel_p, win_p = _trunk(x_prompt, 0, zero_ret, zero_conv, None, W, P)
    win_p = win_p[:, T - min(WINDOW, T):]

    db, ts, _ = x_sample.shape
    past_len = page_table.shape[1] * PAGE
    ctx = (cache_cmp_kv, cache_sel_kv, cache_win_kv, page_table)
    y_s, ret_s, conv_s, cmp_s, sel_s, win_new = _trunk(x_sample, past_len, state_ret, state_conv,
                                                        ctx, W, P)
    all_win = jnp.concatenate([cache_win_kv, win_new], axis=1)
    win_s = all_win[:, all_win.shape[1] - min(WINDOW, past_len + ts):]
    return (y_p, y_s, ret_p, ret_s, conv_p, conv_s, cmp_p, cmp_s, sel_p, sel_s, win_p, win_s)
```

```python
import functools
import math

import jax
import jax.numpy as jnp
import numpy as np
from jax import lax
from jax.experimental import pallas as pl
from jax.experimental.pallas import tpu as pltpu

F32 = jnp.float32
BF16 = jnp.bfloat16

D_MODEL = 1024
PAGE = 128
RET_HEADS = 4
RET_DK = 256
RET_DV = 512
RET_CHUNK = 128
N_HEADS = 16
N_KV = 4
HEAD_DIM = 64
CMP_LEN = 32
CMP_STRIDE = 16
CMP_HID = 128
SEL_BLOCK = 64
SEL_TOPK = 16
WINDOW = 512
Q_BLOCK = 128
D_FF = 2816
ROPE_THETA = 10000.0
EPS = 1e-6
NEG_INF = -1e30
TINY = 1e-30
SEL_FORCE = 1e6
SEL_NEG = -1e6

LANES = 128
KEY_TILE = 512
HALO = 16
VMEM_LIMIT = 48 * 1024 * 1024


def _cparams(sem):
    return pltpu.CompilerParams(dimension_semantics=sem, vmem_limit_bytes=VMEM_LIMIT)


def _nt_dot(a, b):
    return lax.dot_general(a, b, (((1,), (1,)), ((), ())), preferred_element_type=F32)


def _tn_dot(a, b):
    return lax.dot_general(a, b, (((0,), (0,)), ((), ())), preferred_element_type=F32)


def _gelu(x):
    return 0.5 * x * (1.0 + jnp.tanh(math.sqrt(2.0 / math.pi) * (x + 0.044715 * (x * x * x))))


def _rms_rows(x, g):
    r = lax.rsqrt(jnp.mean(x * x, axis=-1, keepdims=True) + EPS)
    return x * r * g


def _head_ms(x, bd):
    x2 = x * x
    hi = x2.astype(BF16)
    lo = (x2 - hi.astype(F32)).astype(BF16)
    s = jnp.dot(hi, bd, preferred_element_type=F32) + jnp.dot(lo, bd, preferred_element_type=F32)
    return s * (1.0 / HEAD_DIM)


def _rope64(x, cos, sin):
    lane = lax.broadcasted_iota(jnp.int32, x.shape, 1)
    sw = jnp.where((lane & 63) < 32, pltpu.roll(x, 96, 1), pltpu.roll(x, 32, 1))
    return x * cos + sw * sin


def _pack_pair(k2, v2, dtype):
    lane = lax.broadcasted_iota(jnp.int32, k2.shape, 1)
    lo = lane < HEAD_DIM
    even = jnp.where(lo, k2, pltpu.roll(v2, HEAD_DIM, 1)).astype(dtype)
    odd = jnp.where(lo, pltpu.roll(k2, HEAD_DIM, 1), v2).astype(dtype)
    return even, odd


def _norm_matmul_kernel(x_ref, g_ref, w_ref, o_ref, h_ref):
    @pl.when(pl.program_id(1) == 0)
    def _():
        h_ref[...] = _rms_rows(x_ref[...], g_ref[...]).astype(BF16)

    o_ref[...] = jnp.dot(h_ref[...], w_ref[...], preferred_element_type=F32).astype(o_ref.dtype)


def norm_matmul(x, g, w, tm, tn):
    M, D = x.shape
    N = w.shape[1]
    return pl.pallas_call(
        _norm_matmul_kernel,
        out_shape=jax.ShapeDtypeStruct((M, N), F32),
        grid=(M // tm, N // tn),
        in_specs=[pl.BlockSpec((tm, D), lambda i, j: (i, 0)),
                  pl.BlockSpec((1, D), lambda i, j: (0, 0)),
                  pl.BlockSpec((D, tn), lambda i, j: (0, j))],
        out_specs=pl.BlockSpec((tm, tn), lambda i, j: (i, j)),
        scratch_shapes=[pltpu.VMEM((tm, D), BF16)],
        compiler_params=_cparams(("parallel", "arbitrary")),
        name="norm_matmul",
    )(x, g.reshape(1, D), w)


def _matmul_res_kernel(a_ref, w_ref, r_ref, o_ref):
    o_ref[...] = r_ref[...] + jnp.dot(a_ref[...].astype(BF16), w_ref[...],
                                      preferred_element_type=F32)


def matmul_res(a, w, res, tm):
    M, K = a.shape
    N = w.shape[1]
    return pl.pallas_call(
        _matmul_res_kernel,
        out_shape=jax.ShapeDtypeStruct((M, N), F32),
        grid=(M // tm,),
        in_specs=[pl.BlockSpec((tm, K), lambda i: (i, 0)),
                  pl.BlockSpec((K, N), lambda i: (0, 0)),
                  pl.BlockSpec((tm, N), lambda i: (i, 0))],
        out_specs=pl.BlockSpec((tm, N), lambda i: (i, 0)),
        compiler_params=_cparams(("parallel",)),
        name="matmul_res",
    )(a, w, res)


def _retention_kernel(lg_ref, gl_ref, q_ref, k_ref, v_ref, g_ref, cos_ref, sin_ref, s0_ref,
                      o_ref, sout_ref, S_ref, *, L, n_chunk):
    h = pl.program_id(1)
    t = pl.program_id(2)
    lg = lg_ref[h]
    gl = gl_ref[h]

    @pl.when(t == 0)
    def _():
        S_ref[...] = s0_ref[0, 0]

    ii = lax.broadcasted_iota(jnp.int32, (L, L), 0)
    jj = lax.broadcasted_iota(jnp.int32, (L, L), 1)
    diff = (ii - jj).astype(F32)
    decay = jnp.where(diff >= 0, jnp.exp(jnp.maximum(diff, 0.0) * lg), 0.0)
    idx = lax.broadcasted_iota(jnp.int32, (L, 1), 0).astype(F32)
    q_dec = jnp.exp((idx + 1.0) * lg)
    k_dec = jnp.exp((L - 1.0 - idx) * lg)
    half = RET_DK // 2

    for c in range(n_chunk):
        rows = pl.ds(c * L, L)
        cos = cos_ref[rows, :]
        sin = sin_ref[rows, :]

        def rope(x):
            x1, x2 = x[:, :half], x[:, half:]
            return jnp.concatenate([x1 * cos - x2 * sin, x2 * cos + x1 * sin], axis=1)

        qr = rope(q_ref[0, rows, :])
        kr = rope(k_ref[0, rows, :]) * (RET_DK ** -0.5)
        qb = qr.astype(BF16)
        vb = v_ref[0, rows, :].astype(BF16)
        sc = _nt_dot(qb, kr.astype(BF16)) * decay
        S = S_ref[...]
        o = jnp.dot(sc.astype(BF16), vb, preferred_element_type=F32)
        o = o + jnp.dot(qb, S.astype(BF16), preferred_element_type=F32) * q_dec
        S_ref[...] = S * gl + _tn_dot((kr * k_dec).astype(BF16), vb)
        on = o * lax.rsqrt(jnp.mean(o * o, axis=-1, keepdims=True) + EPS)
        g = g_ref[0, rows, :]
        o_ref[0, rows, :] = (on * (g * jax.nn.sigmoid(g))).astype(o_ref.dtype)

    @pl.when(t == pl.num_programs(2) - 1)
    def _():
        sout_ref[0, 0] = S_ref[...]


def retention(proj, s0, cos, sin, lg, gl, L, tb):
    B, T, _ = proj.shape
    n_chunk = tb // L
    odt = BF16 if tb % 16 == 0 else F32
    kern = functools.partial(_retention_kernel, L=L, n_chunk=n_chunk)
    grid_spec = pltpu.PrefetchScalarGridSpec(
        num_scalar_prefetch=2,
        grid=(B, RET_HEADS, T // tb),
        in_specs=[
            pl.BlockSpec((1, tb, RET_DK), lambda b, h, t, *_: (b, t, h)),
            pl.BlockSpec((1, tb, RET_DK), lambda b, h, t, *_: (b, t, RET_HEADS + h)),
            pl.BlockSpec((1, tb, RET_DV), lambda b, h, t, *_: (b, t, RET_HEADS + h)),
            pl.BlockSpec((1, tb, RET_DV), lambda b, h, t, *_: (b, t, 2 * RET_HEADS + h)),
            pl.BlockSpec((tb, RET_DK // 2), lambda b, h, t, *_: (t, 0)),
            pl.BlockSpec((tb, RET_DK // 2), lambda b, h, t, *_: (t, 0)),
            pl.BlockSpec((1, 1, RET_DK, RET_DV), lambda b, h, t, *_: (b, h, 0, 0)),
        ],
        out_specs=[
            pl.BlockSpec((1, tb, RET_DV), lambda b, h, t, *_: (b, t, h)),
            pl.BlockSpec((1, 1, RET_DK, RET_DV), lambda b, h, t, *_: (b, h, 0, 0)),
        ],
        scratch_shapes=[pltpu.VMEM((RET_DK, RET_DV), F32)],
    )
    return pl.pallas_call(
        kern,
        out_shape=[jax.ShapeDtypeStruct((B, T, RET_HEADS * RET_DV), odt),
                   jax.ShapeDtypeStruct((B, RET_HEADS, RET_DK, RET_DV), F32)],
        grid_spec=grid_spec,
        compiler_params=_cparams(("parallel", "parallel", "arbitrary")),
        name="retention",
    )(lg, gl, proj, proj, proj, proj, cos, sin, s0)


def _ffn_mid_kernel(u_ref, gt_ref, halo_ref, cw_ref, cb_ref, o_ref):
    u = u_ref[0]
    hl = halo_ref[0, 0]
    row = lax.broadcasted_iota(jnp.int32, u.shape, 0)
    u1 = jnp.where(row == 0, hl[1:2], pltpu.roll(u, 1, 0))
    u2 = jnp.where(row == 0, hl[0:1], jnp.where(row == 1, hl[1:2], pltpu.roll(u, 2, 0)))
    c = cb_ref[...] + cw_ref[0:1] * u2
    c = c + cw_ref[1:2] * u1
    c = c + cw_ref[2:3] * u
    o_ref[0] = (_gelu(c) * gt_ref[0]).astype(o_ref.dtype)


def ffn_mid(proj, buf, conv_w, conv_b, tt):
    B, T, _ = proj.shape
    nt = T // tt
    if nt > 1:
        tails = proj[:, :, :D_FF].reshape(B, nt, tt, D_FF)[:, :-1, tt - 2:, :]
        halo = jnp.concatenate([buf[:, None], tails], axis=1)
    else:
        halo = buf[:, None]
    odt = BF16 if tt % 16 == 0 else F32
    return pl.pallas_call(
        _ffn_mid_kernel,
        out_shape=jax.ShapeDtypeStruct((B, T, D_FF), odt),
        grid=(B, nt),
        in_specs=[pl.BlockSpec((1, tt, D_FF), lambda b, t: (b, t, 0)),
                  pl.BlockSpec((1, tt, D_FF), lambda b, t: (b, t, 1)),
                  pl.BlockSpec((1, 1, 2, D_FF), lambda b, t: (b, t, 0, 0)),
                  pl.BlockSpec((3, D_FF), lambda b, t: (0, 0)),
                  pl.BlockSpec((1, D_FF), lambda b, t: (0, 0))],
        out_specs=pl.BlockSpec((1, tt, D_FF), lambda b, t: (b, t, 0)),
        compiler_params=_cparams(("parallel", "parallel")),
        name="ffn_mid",
    )(proj, proj, halo, conv_w, conv_b.reshape(1, D_FF))


def _ffn_in_kernel(x_ref, xh_ref, g_ref, w_ref, buf_ref, cw_ref, cb_ref,
                   act_ref, tail_ref, *, tiles_per_seq, tn):
    seq_start = (pl.program_id(0) % tiles_per_seq) == 0
    h = _rms_rows(x_ref[...], g_ref[...]).astype(BF16)
    hh = _rms_rows(xh_ref[...], g_ref[...]).astype(BF16)
    tm = h.shape[0]
    row = lax.broadcasted_iota(jnp.int32, (tm, tn), 0)
    for j in range(D_FF // tn):
        cols = slice(j * tn, (j + 1) * tn)
        wu = w_ref[:, cols]
        u = jnp.dot(h, wu, preferred_element_type=F32)
        gt = jnp.dot(h, w_ref[:, D_FF + j * tn:D_FF + (j + 1) * tn], preferred_element_type=F32)
        uh = jnp.dot(hh, wu, preferred_element_type=F32)
        hl = jnp.where(seq_start, buf_ref[0, :, cols], uh[HALO - 2:, :])
        u1 = jnp.where(row == 0, hl[1:2], pltpu.roll(u, 1, 0))
        u2 = jnp.where(row == 0, hl[0:1], jnp.where(row == 1, hl[1:2], pltpu.roll(u, 2, 0)))
        c = cb_ref[:, cols] + cw_ref[0:1, cols] * u2
        c = c + cw_ref[1:2, cols] * u1
        c = c + cw_ref[2:3, cols] * u
        act_ref[:, cols] = (_gelu(c) * gt).astype(act_ref.dtype)
        tail_ref[0, :, cols] = u[tm - 2:, :]


def ffn_in(x, g, w, buf, conv_w, conv_b, T, tm, tn):
    M, D = x.shape
    tiles_per_seq = T // tm
    kern = functools.partial(_ffn_in_kernel, tiles_per_seq=tiles_per_seq, tn=tn)
    act, tails = pl.pallas_call(
        kern,
        out_shape=[jax.ShapeDtypeStruct((M, D_FF), BF16),
                   jax.ShapeDtypeStruct((M // tm, 2, D_FF), F32)],
        grid=(M // tm,),
        in_specs=[pl.BlockSpec((tm, D), lambda i: (i, 0)),
                  pl.BlockSpec((HALO, D), lambda i: (jnp.maximum(i * (tm // HALO) - 1, 0), 0)),
                  pl.BlockSpec((1, D), lambda i: (0, 0)),
                  pl.BlockSpec((D, 2 * D_FF), lambda i: (0, 0), pipeline_mode=pl.Buffered(1)),
                  pl.BlockSpec((1, 2, D_FF), lambda i: (i // tiles_per_seq, 0, 0)),
                  pl.BlockSpec((3, D_FF), lambda i: (0, 0)),
                  pl.BlockSpec((1, D_FF), lambda i: (0, 0))],
        out_specs=[pl.BlockSpec((tm, D_FF), lambda i: (i, 0)),
                   pl.BlockSpec((1, 2, D_FF), lambda i: (i, 0, 0))],
        compiler_params=_cparams(("parallel",)),
        name="ffn_in",
    )(x, x, g.reshape(1, D), w, buf, conv_w, conv_b.reshape(1, D_FF))
    return act, tails[tiles_per_seq - 1::tiles_per_seq]


def _kv_kernel(x_ref, g_ref, w_ref, gain_ref, cos_ref, sin_ref, bd_ref,
               kv_ref, selp_ref, winp_ref):
    h = _rms_rows(x_ref[0], g_ref[...]).astype(BF16)
    y = jnp.dot(h, w_ref[...], preferred_element_type=F32)
    cos = cos_ref[...]
    sin = sin_ref[...]
    bd = bd_ref[...]
    nk = N_KV * HEAD_DIM
    kv_ref[0, :, 0:2 * nk] = y[:, 0:2 * nk]
    for br, pack_ref in enumerate((selp_ref, winp_ref)):
        base = 2 * nk * (br + 1)
        kv_ref[0, :, base + nk:base + 2 * nk] = y[:, base + nk:base + 2 * nk]
        for p in range(2):
            kx = y[:, base + LANES * p:base + LANES * (p + 1)]
            kn = kx * lax.rsqrt(_head_ms(kx, bd) + EPS) * gain_ref[br + 1:br + 2, :]
            kr = _rope64(kn, cos, sin)
            kv_ref[0, :, base + LANES * p:base + LANES * (p + 1)] = kr
            vx = y[:, base + nk + LANES * p:base + nk + LANES * (p + 1)]
            even, odd = _pack_pair(kr, vx, pack_ref.dtype)
            pack_ref[0, 2 * p] = even
            pack_ref[0, 2 * p + 1] = odd


def kv_project(x, g, w, gains, cos, sin, bd, tm):
    B, T, D = x.shape
    N = w.shape[1]
    pdt = BF16 if tm % 16 == 0 else F32
    return pl.pallas_call(
        _kv_kernel,
        out_shape=[jax.ShapeDtypeStruct((B, T, N), F32),
                   jax.ShapeDtypeStruct((B, N_KV, T, LANES), pdt),
                   jax.ShapeDtypeStruct((B, N_KV, T, LANES), pdt)],
        grid=(B, T // tm),
        in_specs=[pl.BlockSpec((1, tm, D), lambda b, t: (b, t, 0)),
                  pl.BlockSpec((1, D), lambda b, t: (0, 0)),
                  pl.BlockSpec((D, N), lambda b, t: (0, 0)),
                  pl.BlockSpec((3, LANES), lambda b, t: (0, 0)),
                  pl.BlockSpec((tm, LANES), lambda b, t: (t, 0)),
                  pl.BlockSpec((tm, LANES), lambda b, t: (t, 0)),
                  pl.BlockSpec((LANES, LANES), lambda b, t: (0, 0))],
        out_specs=[pl.BlockSpec((1, tm, N), lambda b, t: (b, t, 0)),
                   pl.BlockSpec((1, N_KV, tm, LANES), lambda b, t: (b, 0, t, 0)),
                   pl.BlockSpec((1, N_KV, tm, LANES), lambda b, t: (b, 0, t, 0))],
        compiler_params=_cparams(("parallel", "parallel")),
        name="kv_project",
    )(x, g.reshape(1, D), w, gains, cos, sin, bd)


def _qg_kernel(x_ref, g_ref, w_ref, gain_ref, cos_ref, sin_ref, bd_ref,
               qc_ref, qr_ref, gate_ref, *, qb, n_qb):
    h = _rms_rows(x_ref[0], g_ref[...]).astype(BF16)
    y = jnp.dot(h, w_ref[...], preferred_element_type=F32)
    cos = cos_ref[...]
    sin = sin_ref[...]
    bd = bd_ref[...]
    nq = N_HEADS * HEAD_DIM
    gate_ref[0] = jax.nn.sigmoid(y[:, nq:nq + LANES])
    lane = lax.broadcasted_iota(jnp.int32, (y.shape[0], LANES), 1)
    lo = lane < HEAD_DIM
    group = N_HEADS // N_KV
    for p in range(N_HEADS // 2):
        qx = y[:, LANES * p:LANES * (p + 1)]
        qn = qx * lax.rsqrt(_head_ms(qx, bd) + EPS) * gain_ref[...]
        qr = _rope64(qn, cos, sin)
        for src, dst in ((qn, qc_ref), (qr, qr_ref)):
            for par in range(2):
                hd = 2 * p + par
                kvh, gi = hd // group, hd % group
                v = src if par == 0 else pltpu.roll(src, HEAD_DIM, 1)
                v = jnp.where(lo, v, 0.0).astype(dst.dtype)
                for j in range(n_qb):
                    dst[0, kvh, j, gi * qb:(gi + 1) * qb, :] = v[j * qb:(j + 1) * qb, :]


def qg_project(x, g, w, gain, cos, sin, bd, tm, qb):
    B, T, D = x.shape
    N = w.shape[1]
    n_qb = tm // qb
    group = N_HEADS // N_KV
    qdt = BF16 if qb % 16 == 0 else F32
    kern = functools.partial(_qg_kernel, qb=qb, n_qb=n_qb)
    qshape = jax.ShapeDtypeStruct((B, N_KV, T // qb, group * qb, LANES), qdt)
    qspec = pl.BlockSpec((1, N_KV, n_qb, group * qb, LANES), lambda b, t: (b, 0, t, 0, 0))
    return pl.pallas_call(
        kern,
        out_shape=[qshape, qshape, jax.ShapeDtypeStruct((B, T, LANES), F32)],
        grid=(B, T // tm),
        in_specs=[pl.BlockSpec((1, tm, D), lambda b, t: (b, t, 0)),
                  pl.BlockSpec((1, D), lambda b, t: (0, 0)),
                  pl.BlockSpec((D, N), lambda b, t: (0, 0)),
                  pl.BlockSpec((1, LANES), lambda b, t: (0, 0)),
                  pl.BlockSpec((tm, LANES), lambda b, t: (t, 0)),
                  pl.BlockSpec((tm, LANES), lambda b, t: (t, 0)),
                  pl.BlockSpec((LANES, LANES), lambda b, t: (0, 0))],
        out_specs=[qspec, qspec, pl.BlockSpec((1, tm, LANES), lambda b, t: (b, t, 0))],
        compiler_params=_cparams(("parallel", "parallel")),
        name="qg_project",
    )(x, g.reshape(1, D), w, gain, cos, sin, bd)


def _pos_bias_kernel(p_ref, w_ref, o_ref):
    o_ref[0] = jnp.dot(p_ref[0].astype(BF16), w_ref[0].astype(BF16), preferred_element_type=F32)


def pos_bias(cmp_pos, cmp_w1):
    K = CMP_LEN * HEAD_DIM
    p = jnp.broadcast_to(cmp_pos.reshape(2, 1, K), (2, 8, K))
    out = pl.pallas_call(
        _pos_bias_kernel,
        out_shape=jax.ShapeDtypeStruct((2, 8, CMP_HID), F32),
        grid=(2,),
        in_specs=[pl.BlockSpec((1, 8, K), lambda c: (c, 0, 0)),
                  pl.BlockSpec((1, K, CMP_HID), lambda c: (c, 0, 0))],
        out_specs=pl.BlockSpec((1, 8, CMP_HID), lambda c: (c, 0, 0)),
        name="pos_bias",
    )(p, cmp_w1)
    return out[:, 0, :]


def _compress_kernel(pt_ref, *refs, n_pg):
    pages = refs[:n_pg]
    w1_ref, pb_ref, w2_ref, gain_ref, out_ref, carry_ref, slab_ref = refs[n_pg:]
    g = pl.program_id(1)

    @pl.when(g == 0)
    def _():
        carry_ref[...] = jnp.zeros_like(carry_ref)

    n = n_pg * (PAGE // CMP_STRIDE)
    lo = lax.broadcasted_iota(jnp.int32, (n, LANES), 1) < HEAD_DIM
    row = lax.broadcasted_iota(jnp.int32, (N_KV * n, LANES), 0)
    res = jnp.zeros((N_KV * n, LANES), F32)
    for c in range(2):
        for pair in range(N_KV // 2):
            col = c * N_KV * HEAD_DIM + pair * LANES
            for i in range(n_pg):
                slab_ref[pair, PAGE * i:PAGE * (i + 1), :] = pages[i][0, :, col:col + LANES]
        acc = jnp.zeros((N_KV * n, 2 * CMP_HID), F32)
        for s in range(0, CMP_STRIDE, 2):
            parts = []
            for k in range(N_KV):
                pair, par = divmod(k, 2)
                keep = lo if par == 0 else jnp.logical_not(lo)
                parts.append(jnp.concatenate(
                    [jnp.where(keep, slab_ref[pair, pl.ds(s + i, n, stride=CMP_STRIDE), :], 0.0)
                     for i in range(2)], axis=1))
            xm = jnp.concatenate(parts, axis=0).astype(BF16)
            acc = acc + jnp.dot(xm, w1_ref[c, s // 2], preferred_element_type=F32)
        p0 = acc[:, :CMP_HID]
        p1 = acc[:, CMP_HID:]
        prev = pltpu.roll(p0, 1, 0)
        for k in range(N_KV):
            prev = jnp.where(row == k * n, carry_ref[2 * k + c, 7:8, :], prev)
        for k in range(N_KV):
            carry_ref[2 * k + c] = p0[(k + 1) * n - 8:(k + 1) * n, :]
        hid = _gelu(prev + p1 + pb_ref[c:c + 1, :]).astype(BF16)
        res = res + jnp.dot(hid, w2_ref[c], preferred_element_type=F32)
    for k in range(N_KV):
        r = res[k * n:(k + 1) * n, :]
        ms = jnp.sum(jnp.where(lo, r * r, 0.0), axis=-1, keepdims=True) * (1.0 / HEAD_DIM)
        kn = r * lax.rsqrt(ms + EPS) * gain_ref[...]
        out_ref[0, k] = jnp.where(lo, kn, r).astype(out_ref.dtype)


def _page_index(b, g, pt_ref, *, i, n_pg):
    return (pt_ref[b, g * n_pg + i], 0, 0)


def compress(pages_arr, table, w1dup, pb, w2p, gain):
    B, n_pages = table.shape
    n_pg = 16 if n_pages % 16 == 0 else n_pages
    n = n_pg * (PAGE // CMP_STRIDE)
    n_sub = n_pages * (PAGE // CMP_STRIDE)
    width = 2 * N_KV * HEAD_DIM
    in_specs = [pl.BlockSpec((1, PAGE, width), functools.partial(_page_index, i=i, n_pg=n_pg))
                for i in range(n_pg)]
    in_specs += [pl.BlockSpec(w1dup.shape, lambda b, g, pt: (0, 0, 0, 0)),
                 pl.BlockSpec(pb.shape, lambda b, g, pt: (0, 0)),
                 pl.BlockSpec(w2p.shape, lambda b, g, pt: (0, 0, 0)),
                 pl.BlockSpec((1, LANES), lambda b, g, pt: (0, 0))]
    grid_spec = pltpu.PrefetchScalarGridSpec(
        num_scalar_prefetch=1,
        grid=(B, n_pages // n_pg),
        in_specs=in_specs,
        out_specs=pl.BlockSpec((1, N_KV, n, LANES), lambda b, g, pt: (b, 0, g, 0)),
        scratch_shapes=[pltpu.VMEM((2 * N_KV, 8, CMP_HID), F32),
                        pltpu.VMEM((N_KV // 2, n_pg * PAGE, LANES), F32)],
    )
    return pl.pallas_call(
        functools.partial(_compress_kernel, n_pg=n_pg),
        out_shape=jax.ShapeDtypeStruct((B, N_KV, n_sub, LANES), BF16),
        grid_spec=grid_spec,
        compiler_params=_cparams(("parallel", "arbitrary")),
        name="compress",
    )(table, *([pages_arr] * n_pg), w1dup, pb, w2p, gain)


def _overlap_map(n_cmp_rows, n_selp):
    m = np.arange(n_cmp_rows)[None, :]
    s = np.arange(n_selp)[:, None]
    c0 = (m - 1) * CMP_STRIDE
    ov = np.minimum(c0 + CMP_LEN, s * SEL_BLOCK + SEL_BLOCK) - np.maximum(c0, s * SEL_BLOCK)
    w = np.maximum(ov, 0).astype(np.float32) / CMP_LEN
    w[:, 0] = 0.0
    return jnp.asarray(w, BF16)


def _qgt_kernel(x_ref, g_ref, w_ref, gain_ref, cos_ref, sin_ref, bd_ref,
                qc_ref, qr_ref, gate_ref, *, qb, n_qb):
    h = _rms_rows(x_ref[0], g_ref[...]).astype(BF16)
    y = jnp.dot(h, w_ref[...], preferred_element_type=F32)
    cos = cos_ref[...]
    sin = sin_ref[...]
    bd = bd_ref[...]
    nq = N_HEADS * HEAD_DIM
    group = N_HEADS // N_KV
    gate_t = jax.nn.sigmoid(y[:, nq:nq + LANES]).T
    for j in range(n_qb):
        gate_ref[0, j] = gate_t[:, j * qb:(j + 1) * qb]
    pad = jnp.zeros((N_KV, n_qb, HEAD_DIM, group * qb), qc_ref.dtype)
    qc_ref[0, :, :, HEAD_DIM:, :] = pad
    qr_ref[0, :, :, HEAD_DIM:, :] = pad
    for p in range(N_HEADS // 2):
        qx = y[:, LANES * p:LANES * (p + 1)]
        qn = qx * lax.rsqrt(_head_ms(qx, bd) + EPS) * gain_ref[...]
        qr = _rope64(qn, cos, sin)
        for src, dst in ((qn, qc_ref), (qr, qr_ref)):
            st = src.T.astype(dst.dtype)
            for par in range(2):
                kvh, gi = divmod(2 * p + par, group)
                for j in range(n_qb):
                    dst[0, kvh, j, 0:HEAD_DIM, gi * qb:(gi + 1) * qb] = (
                        st[par * HEAD_DIM:(par + 1) * HEAD_DIM, j * qb:(j + 1) * qb])


def qg_project_t(x, g, w, gain, cos, sin, bd, tm, qb):
    B, T, D = x.shape
    N = w.shape[1]
    n_qb = tm // qb
    group = N_HEADS // N_KV
    kern = functools.partial(_qgt_kernel, qb=qb, n_qb=n_qb)
    qshape = jax.ShapeDtypeStruct((B, N_KV, T // qb, LANES, group * qb), BF16)
    qspec = pl.BlockSpec((1, N_KV, n_qb, LANES, group * qb), lambda b, t: (b, 0, t, 0, 0))
    return pl.pallas_call(
        kern,
        out_shape=[qshape, qshape, jax.ShapeDtypeStruct((B, T // qb, LANES, qb), F32)],
        grid=(B, T // tm),
        in_specs=[pl.BlockSpec((1, tm, D), lambda b, t: (b, t, 0)),
                  pl.BlockSpec((1, D), lambda b, t: (0, 0)),
                  pl.BlockSpec((D, N), lambda b, t: (0, 0)),
                  pl.BlockSpec((1, LANES), lambda b, t: (0, 0)),
                  pl.BlockSpec((tm, LANES), lambda b, t: (t, 0)),
                  pl.BlockSpec((tm, LANES), lambda b, t: (t, 0)),
                  pl.BlockSpec((LANES, LANES), lambda b, t: (0, 0))],
        out_specs=[qspec, qspec, pl.BlockSpec((1, n_qb, LANES, qb), lambda b, t: (b, t, 0, 0))],
        compiler_params=_cparams(("parallel", "parallel")),
        name="qg_project_t",
    )(x, g.reshape(1, D), w, gain, cos, sin, bd)


def _nsa_t_kernel(qc_ref, qr_ref, gate_ref, cmp_ref, sel_ref, win_ref, wmap_ref, o_ref,
                  selneg_ref, m_ref, l_ref, acc_ref,
                  *, qb, n_sel, n_selp, n_cmp, q0, w_off, w_rows, l_win):
    qi = pl.program_id(1)
    group = N_HEADS // N_KV
    R = group * qb
    blocks_per_tile = KEY_TILE // SEL_BLOCK
    q_lo = q0 + qi * qb
    qpos_q = q_lo + lax.broadcasted_iota(jnp.int32, (1, qb), 1)
    n_kt = (q_lo + qb - 1) // KEY_TILE + 1
    w_start = pl.multiple_of(jnp.clip(q_lo - WINDOW - w_off, 0, l_win - w_rows), LANES)
    gate = gate_ref[0, 0]

    def lanes4(a):
        return jnp.concatenate([a] * group, axis=1)

    m_idx = lax.broadcasted_iota(jnp.int32, (n_cmp, qb), 0)
    cvalid = (m_idx >= 1) & ((m_idx - 1) * CMP_STRIDE + CMP_LEN - 1 <= qpos_q)
    cbias = lanes4(jnp.where(cvalid, 0.0, NEG_INF))
    any_c = lanes4(qpos_q >= CMP_LEN - 1)
    blk = lax.broadcasted_iota(jnp.int32, (n_selp, qb), 0)
    blk_f = blk.astype(F32)
    cur = qpos_q >> 6
    forced = (blk == 0) | (blk == cur) | (blk == cur - 1)
    reach = blk * SEL_BLOCK <= qpos_q
    real = blk < n_sel
    wpos = w_off + w_start + lax.broadcasted_iota(jnp.int32, (w_rows, qb), 0)
    wbias = lanes4(jnp.where((wpos <= qpos_q) & (wpos > qpos_q - WINDOW) & (wpos >= 0), 0.0, NEG_INF))
    row_t = lax.broadcasted_iota(jnp.int32, (KEY_TILE, qb), 0)

    for k in range(N_KV):
        ckv = cmp_ref[0, k]
        s = jnp.dot(ckv, qc_ref[0, k, 0], preferred_element_type=F32) + cbias
        e = jnp.exp2(s - jnp.max(s, axis=0, keepdims=True))
        den = jnp.maximum(jnp.sum(e, axis=0, keepdims=True), TINY)
        p = e * jnp.where(any_c, 1.0 / den, 0.0)
        oc = _tn_dot(ckv, p.astype(BF16))
        psum = p[:, 0:qb]
        for gi in range(1, group):
            psum = psum + p[:, gi * qb:(gi + 1) * qb]
        hi = psum.astype(BF16)
        lo = (psum - hi.astype(F32)).astype(BF16)
        imp = (jnp.dot(wmap_ref[...], hi, preferred_element_type=F32)
               + jnp.dot(wmap_ref[...], lo, preferred_element_type=F32))
        v = jnp.where(forced, SEL_FORCE, jnp.where(reach, imp, SEL_NEG))
        v = jnp.where(real, v, -jnp.inf)

        def pick_one(_, carry):
            v, sel = carry
            m = jnp.max(v, axis=0, keepdims=True)
            first = jnp.min(jnp.where(v == m, blk_f, float(n_selp)), axis=0, keepdims=True)
            pick = blk_f == first
            sel = jnp.where(pick & (m > 0.5 * SEL_NEG), 0.0, sel)
            return jnp.where(pick, -jnp.inf, v), sel

        _, sel = lax.fori_loop(0, SEL_TOPK, pick_one, (v, jnp.full((n_selp, qb), NEG_INF, F32)))
        selneg_ref[...] = sel

        qr = qr_ref[0, k, 0]
        m_ref[...] = jnp.full(m_ref.shape, NEG_INF, F32)
        l_ref[...] = jnp.zeros(l_ref.shape, F32)
        acc_ref[...] = jnp.zeros(acc_ref.shape, F32)

        def tile(kt, carry):
            start = pl.multiple_of(kt * KEY_TILE, KEY_TILE)
            kv = sel_ref[0, k, pl.ds(start, KEY_TILE), :]
            s = jnp.dot(kv, qr, preferred_element_type=F32)
            pieces = [jnp.broadcast_to(selneg_ref[pl.ds(kt * blocks_per_tile + j, 1), :], (SEL_BLOCK, qb))
                      for j in range(blocks_per_tile)]
            bias = jnp.concatenate(pieces, axis=0) + jnp.where(start + row_t <= qpos_q, 0.0, NEG_INF)
            s = s + lanes4(bias)
            m_old = m_ref[...]
            m_new = jnp.maximum(m_old, jnp.max(s, axis=0, keepdims=True))
            alpha = jnp.exp2(m_old - m_new)
            p = jnp.exp2(s - m_new)
            l_ref[...] = alpha * l_ref[...] + jnp.sum(p, axis=0, keepdims=True)
            acc_ref[...] = alpha * acc_ref[...] + _tn_dot(kv, p.astype(BF16))
            m_ref[...] = m_new
            return carry

        lax.fori_loop(0, n_kt, tile, 0)
        o_s = acc_ref[...] * (1.0 / jnp.maximum(l_ref[...], TINY))

        wkv = win_ref[0, k, pl.ds(w_start, w_rows), :]
        s = jnp.dot(wkv, qr, preferred_element_type=F32) + wbias
        e = jnp.exp2(s - jnp.max(s, axis=0, keepdims=True))
        den = jnp.maximum(jnp.sum(e, axis=0, keepdims=True), TINY)
        o_w = _tn_dot(wkv, e.astype(BF16)) * (1.0 / den)

        for gi in range(group):
            hd = k * group + gi
            sl = slice(gi * qb, (gi + 1) * qb)
            o = gate[3 * hd:3 * hd + 1, :] * oc[:, sl]
            o = o + gate[3 * hd + 1:3 * hd + 2, :] * o_s[:, sl]
            o = o + gate[3 * hd + 2:3 * hd + 3, :] * o_w[:, sl]
            o_ref[0, :, LANES * hd:LANES * (hd + 1)] = o.T.astype(o_ref.dtype)


def nsa_attend_t(qc, qr, gates, cmp_p, sel_p, win_p, T, qb, q0, w_off, n_sel):
    B = qc.shape[0]
    group = N_HEADS // N_KV
    R = group * qb
    n_cmp = cmp_p.shape[2]
    n_selp = -(-n_sel // LANES) * LANES
    l_sel = sel_p.shape[2]
    l_win = win_p.shape[2]
    w_rows = WINDOW + max(qb, LANES)
    assert l_win >= w_rows and l_sel >= ((q0 + T - 1) // KEY_TILE + 1) * KEY_TILE
    wmap = _overlap_map(n_cmp, n_selp)
    kern = functools.partial(_nsa_t_kernel, qb=qb, n_sel=n_sel, n_selp=n_selp, n_cmp=n_cmp,
                             q0=q0, w_off=w_off, w_rows=w_rows, l_win=l_win)
    qspec = pl.BlockSpec((1, N_KV, 1, LANES, R), lambda b, i: (b, 0, i, 0, 0))
    resident = dict(pipeline_mode=pl.Buffered(1))
    return pl.pallas_call(
        kern,
        out_shape=jax.ShapeDtypeStruct((B, T, N_HEADS * LANES), BF16),
        grid=(B, T // qb),
        in_specs=[
            qspec, qspec,
            pl.BlockSpec((1, 1, LANES, qb), lambda b, i: (b, i, 0, 0)),
            pl.BlockSpec((1, N_KV, n_cmp, LANES), lambda b, i: (b, 0, 0, 0)),
            pl.BlockSpec((1, N_KV, l_sel, LANES), lambda b, i: (b, 0, 0, 0), **resident),
            pl.BlockSpec((1, N_KV, l_win, LANES), lambda b, i: (b, 0, 0, 0), **resident),
            pl.BlockSpec((n_selp, n_cmp), lambda b, i: (0, 0)),
        ],
        out_specs=pl.BlockSpec((1, qb, N_HEADS * LANES), lambda b, i: (b, i, 0)),
        scratch_shapes=[
            pltpu.VMEM((n_selp, qb), F32),
            pltpu.VMEM((1, R), F32), pltpu.VMEM((1, R), F32), pltpu.VMEM((LANES, R), F32),
        ],
        compiler_params=_cparams(("parallel", "arbitrary")),
        name="nsa_attend_t",
    )(qc, qr, gates, cmp_p, sel_p, win_p, wmap)


def _nsa_fast_kernel(shift_ref, qc_ref, qr_ref, gate_ref, cmp_ref, sel_ref, win_ref, wmap_ref, hot_ref,
                     o_ref, qaug_ref, oc_ref, ow_ref, l_ref, acc_ref,
                     *, qb, n_sel, n_selp, n_cmp, q0, w_off, w_rows, l_win):
    qi = pl.program_id(1)
    group = N_HEADS // N_KV
    R = group * qb
    q_lo = q0 + qi * qb
    qpos_q = q_lo + lax.broadcasted_iota(jnp.int32, (1, qb), 1)
    n_kt = (q_lo + qb - 1) // KEY_TILE + 1
    w_start = pl.multiple_of(jnp.clip(q_lo - WINDOW - w_off, 0, l_win - w_rows), LANES)
    shift_c = shift_ref[0]
    shift_s = shift_ref[1]
    shift_w = shift_ref[2]

    def lanes4(a):
        return jnp.concatenate([a] * group, axis=1)

    m_idx = lax.broadcasted_iota(jnp.int32, (n_cmp, qb), 0)
    cvalid = (m_idx >= 1) & ((m_idx - 1) * CMP_STRIDE + CMP_LEN - 1 <= qpos_q)
    cbias = lanes4(jnp.where(cvalid, -shift_c, NEG_INF))
    any_c = lanes4(qpos_q >= CMP_LEN - 1)
    blk = lax.broadcasted_iota(jnp.int32, (n_selp, qb), 0)
    blk_f = blk.astype(F32)
    cur = qpos_q >> 6
    forced = (blk == 0) | (blk == cur) | (blk == cur - 1)
    reach = blk * SEL_BLOCK <= qpos_q
    real = blk < n_sel

    imps = []
    for k in range(N_KV):
        ckv = cmp_ref[0, k]
        e = jnp.exp2(jnp.dot(ckv, qc_ref[0, k, 0], preferred_element_type=F32) + cbias)
        den = jnp.maximum(jnp.sum(e, axis=0, keepdims=True), TINY)
        p = e * jnp.where(any_c, 1.0 / den, 0.0)
        oc_ref[k] = _tn_dot(ckv, p.astype(BF16))
        psum = p[:, 0:qb]
        for gi in range(1, group):
            psum = psum + p[:, gi * qb:(gi + 1) * qb]
        hi = psum.astype(BF16)
        lo = (psum - hi.astype(F32)).astype(BF16)
        imp = (jnp.dot(wmap_ref[...], hi, preferred_element_type=F32)
               + jnp.dot(wmap_ref[...], lo, preferred_element_type=F32))
        v = jnp.where(forced, SEL_FORCE, jnp.where(reach, imp, SEL_NEG))
        imps.append(jnp.where(real, v, -jnp.inf))

    def pick_one(_, vs):
        out = []
        for v in vs:
            m = jnp.max(v, axis=0, keepdims=True)
            first = jnp.min(jnp.where(v == m, blk_f, float(n_selp)), axis=0, keepdims=True)
            out.append(jnp.where(blk_f == first, -jnp.inf, v))
        return tuple(out)

    wpos = w_off + w_start + lax.broadcasted_iota(jnp.int32, (w_rows, qb), 0)
    wvalid = (wpos <= qpos_q) & (wpos > qpos_q - WINDOW) & (wpos >= 0)
    wbias = lanes4(jnp.where(wvalid, -shift_w, NEG_INF))
    for k in range(N_KV):
        wkv = win_ref[0, k, pl.ds(w_start, w_rows), :]
        e = jnp.exp2(jnp.dot(wkv, qr_ref[0, k, 0], preferred_element_type=F32) + wbias)
        den = jnp.maximum(jnp.sum(e, axis=0, keepdims=True), TINY)
        ow_ref[k] = _tn_dot(wkv, e.astype(BF16)) * (1.0 / den)

    n_forced = 3
    marked = lax.fori_loop(0, SEL_TOPK - n_forced, pick_one,
                           tuple(jnp.where(forced, -jnp.inf, v) for v in imps), unroll=True)
    for k in range(N_KV):
        sel = jnp.where((marked[k] == -jnp.inf) & (imps[k] > 0.5 * SEL_NEG), -shift_s, NEG_INF)
        qaug_ref[k, 0:LANES, :] = qr_ref[0, k, 0]
        qaug_ref[k, LANES:, :] = lanes4(sel).astype(BF16)

    l_ref[...] = jnp.zeros(l_ref.shape, F32)
    acc_ref[...] = jnp.zeros(acc_ref.shape, F32)

    def tile(start, rows, causal_bias):
        hot = hot_ref[pl.ds(start, rows), :]
        for k in range(N_KV):
            kv = sel_ref[0, k, pl.ds(start, rows), :]
            s = jnp.dot(jnp.concatenate([kv, hot], axis=1), qaug_ref[k], preferred_element_type=F32)
            if causal_bias is not None:
                s = s + causal_bias
            p = jnp.exp2(s)
            l_ref[k] += jnp.sum(p.reshape(rows // 8, 8, R), axis=0)
            acc_ref[k] += _tn_dot(kv, p.astype(BF16))

    def double_tile(i, carry):
        tile(pl.multiple_of(i * 2 * KEY_TILE, 2 * KEY_TILE), 2 * KEY_TILE, None)
        return carry

    n_below = n_kt - 1
    lax.fori_loop(0, n_below // 2, double_tile, 0)

    @pl.when(n_below % 2 == 1)
    def _():
        tile(pl.multiple_of((n_below - 1) * KEY_TILE, KEY_TILE), KEY_TILE, None)

    d_start = pl.multiple_of(n_below * KEY_TILE, KEY_TILE)
    row_t = lax.broadcasted_iota(jnp.int32, (KEY_TILE, qb), 0)
    tile(d_start, KEY_TILE, lanes4(jnp.where(d_start + row_t <= qpos_q, 0.0, NEG_INF)))

    gate = gate_ref[0, 0]
    for k in range(N_KV):
        o_w = ow_ref[k, HEAD_DIM:, :]
        l_s = jnp.maximum(jnp.sum(l_ref[k], axis=0, keepdims=True), TINY)
        o_s = acc_ref[k, HEAD_DIM:, :] * (1.0 / l_s)
        oc = oc_ref[k, HEAD_DIM:, :]
        heads = []
        for gi in range(group):
            hd = k * group + gi
            sl = slice(gi * qb, (gi + 1) * qb)
            o = gate[3 * hd:3 * hd + 1, :] * oc[:, sl]
            o = o + gate[3 * hd + 1:3 * hd + 2, :] * o_s[:, sl]
            heads.append(o + gate[3 * hd + 2:3 * hd + 3, :] * o_w[:, sl])
        for pr in range(group // 2):
            o2 = jnp.concatenate(heads[2 * pr:2 * pr + 2], axis=0)
            col = (k * group + 2 * pr) * HEAD_DIM
            o_ref[0, :, col:col + LANES] = o2.T.astype(o_ref.dtype)


def nsa_attend_fast(shifts, qc, qr, gates, cmp_p, sel_p, win_p, T, qb, q0, w_off, n_sel):
    B = qc.shape[0]
    group = N_HEADS // N_KV
    R = group * qb
    n_cmp = cmp_p.shape[2]
    n_selp = -(-n_sel // LANES) * LANES
    l_sel = sel_p.shape[2]
    l_win = win_p.shape[2]
    w_rows = WINDOW + max(qb, LANES)
    assert l_win >= w_rows and l_sel >= ((q0 + T - 1) // KEY_TILE + 1) * KEY_TILE
    assert n_sel >= SEL_TOPK and q0 % qb == 0 and KEY_TILE % qb == 0
    wmap = _overlap_map(n_cmp, n_selp)
    hot = np.zeros((l_sel, n_selp), np.float32)
    hot[np.arange(l_sel), np.arange(l_sel) // SEL_BLOCK] = 1.0
    hot = jnp.asarray(hot, BF16)
    kern = functools.partial(_nsa_fast_kernel, qb=qb, n_sel=n_sel, n_selp=n_selp, n_cmp=n_cmp,
                             q0=q0, w_off=w_off, w_rows=w_rows, l_win=l_win)
    qspec = pl.BlockSpec((1, N_KV, 1, LANES, R), lambda b, i, sh: (b, 0, i, 0, 0))
    resident = dict(pipeline_mode=pl.Buffered(1))
    grid_spec = pltpu.PrefetchScalarGridSpec(
        num_scalar_prefetch=1,
        grid=(B, T // qb),
        in_specs=[
            qspec, qspec,
            pl.BlockSpec((1, 1, LANES, qb), lambda b, i, sh: (b, i, 0, 0)),
            pl.BlockSpec((1, N_KV, n_cmp, LANES), lambda b, i, sh: (b, 0, 0, 0)),
            pl.BlockSpec((1, N_KV, l_sel, LANES), lambda b, i, sh: (b, 0, 0, 0), **resident),
            pl.BlockSpec((1, N_KV, l_win, LANES), lambda b, i, sh: (b, 0, 0, 0), **resident),
            pl.BlockSpec((n_selp, n_cmp), lambda b, i, sh: (0, 0)),
            pl.BlockSpec((l_sel, n_selp), lambda b, i, sh: (0, 0), **resident),
        ],
        out_specs=pl.BlockSpec((1, qb, N_HEADS * HEAD_DIM), lambda b, i, sh: (b, i, 0)),
        scratch_shapes=[
            pltpu.VMEM((N_KV, 2 * LANES, R), BF16),
            pltpu.VMEM((N_KV, LANES, R), F32),
            pltpu.VMEM((N_KV, LANES, R), F32),
            pltpu.VMEM((N_KV, 8, R), F32),
            pltpu.VMEM((N_KV, LANES, R), F32),
        ],
    )
    return pl.pallas_call(
        kern,
        out_shape=jax.ShapeDtypeStruct((B, T, N_HEADS * HEAD_DIM), BF16),
        grid_spec=grid_spec,
        compiler_params=_cparams(("parallel", "arbitrary")),
        name="nsa_attend_fast",
    )(shifts, qc, qr, gates, cmp_p, sel_p, win_p, wmap, hot)


def _nsa_dec_kernel(pt_ref, *refs, n_pg, qb, n_sel, n_selp, n_cmp, q0, w_off):
    pages = refs[:n_pg]
    (qcp_ref, qbd_ref, gate_ref, cmp_ref, new_ref, win_ref, wmap_ref, fold_ref, hot_ref,
     o_ref, qaug_ref, oc_ref, m_ref, l_ref, acc_ref) = refs[n_pg:]
    g = pl.program_id(1)
    nk = N_KV * HEAD_DIM
    t_new = new_ref.shape[1]
    lane = lax.broadcasted_iota(jnp.int32, (1, LANES), 1)
    qpos = q0 + (lane & (qb - 1))

    @pl.when(g == 0)
    def _first():
        s = jnp.dot(cmp_ref[0, 0], qcp_ref[0, 0], preferred_element_type=F32)
        for k in range(1, N_KV):
            s = s + jnp.dot(cmp_ref[0, k], qcp_ref[0, k], preferred_element_type=F32)
        m_idx = lax.broadcasted_iota(jnp.int32, (n_cmp, LANES), 0)
        cvalid = (m_idx >= 1) & ((m_idx - 1) * CMP_STRIDE + CMP_LEN - 1 <= qpos)
        s = s + jnp.where(cvalid, 0.0, NEG_INF)
        e = jnp.exp2(s - jnp.max(s, axis=0, keepdims=True))
        den = jnp.maximum(jnp.sum(e, axis=0, keepdims=True), TINY)
        p = e * jnp.where(qpos >= CMP_LEN - 1, 1.0 / den, 0.0)
        pb = p.astype(BF16)
        for k in range(N_KV):
            oc_ref[k * HEAD_DIM:(k + 1) * HEAD_DIM, :] = _tn_dot(cmp_ref[0, k], pb)[HEAD_DIM:, :]
        fold = fold_ref[...]
        p_lo = (p - pb.astype(F32)).astype(BF16)
        psum = jnp.dot(pb, fold, preferred_element_type=F32) + jnp.dot(p_lo, fold, preferred_element_type=F32)
        hi = psum.astype(BF16)
        lo = (psum - hi.astype(F32)).astype(BF16)
        imp = (jnp.dot(wmap_ref[...], hi, preferred_element_type=F32)
               + jnp.dot(wmap_ref[...], lo, preferred_element_type=F32))
        blk = lax.broadcasted_iota(jnp.int32, (n_selp, LANES), 0)
        blk_f = blk.astype(F32)
        cur = qpos >> 6
        forced = (blk == 0) | (blk == cur) | (blk == cur - 1)
        v = jnp.where(forced, SEL_FORCE, jnp.where(blk * SEL_BLOCK <= qpos, imp, SEL_NEG))
        v = jnp.where(blk < n_sel, v, -jnp.inf)

        def pick_one(_, carry):
            v, sel = carry
            m = jnp.max(v, axis=0, keepdims=True)
            first = jnp.min(jnp.where(v == m, blk_f, float(n_selp)), axis=0, keepdims=True)
            pick = blk_f == first
            sel = jnp.where(pick & (m > 0.5 * SEL_NEG), 0.0, sel)
            return jnp.where(pick, -jnp.inf, v), sel

        _, sel = lax.fori_loop(0, SEL_TOPK, pick_one, (v, jnp.full((n_selp, LANES), NEG_INF, F32)))
        qaug_ref[0:nk, :] = qbd_ref[0]
        qaug_ref[nk:, :] = sel.astype(BF16)
        m_ref[...] = jnp.full(m_ref.shape, NEG_INF, F32)
        l_ref[...] = jnp.zeros(l_ref.shape, F32)
        acc_ref[...] = jnp.zeros(acc_ref.shape, F32)

    def attend(kx, vx, hot, bias):
        s = jnp.dot(jnp.concatenate([kx, hot], axis=1), qaug_ref[...], preferred_element_type=F32)
        if bias is not None:
            s = s + bias
        m_old = m_ref[...]
        m_new = jnp.maximum(m_old, jnp.max(s, axis=0, keepdims=True))
        alpha = jnp.exp2(m_old - m_new)
        p = jnp.exp2(s - m_new)
        l_ref[...] = alpha * l_ref[...] + jnp.sum(p.reshape(p.shape[0] // 8, 8, LANES), axis=0)
        acc_ref[...] = alpha * acc_ref[...] + _tn_dot(vx, p.astype(BF16))
        m_ref[...] = m_new

    x = jnp.concatenate([pages[i][0] for i in range(n_pg)], axis=0).astype(BF16)
    start = pl.multiple_of(g * (n_pg * PAGE), n_pg * PAGE)
    attend(x[:, :nk], x[:, nk:], hot_ref[pl.ds(start, n_pg * PAGE), :], None)

    @pl.when(g == pl.num_programs(1) - 1)
    def _last():
        pad = jnp.zeros((HALO - t_new, 2 * nk), F32)
        row = lax.broadcasted_iota(jnp.int32, (HALO, LANES), 0)
        new_ok = (row < t_new) & (q0 + row <= qpos)
        xn = jnp.concatenate([new_ref[0, :, 2 * nk:4 * nk], pad], axis=0).astype(BF16)
        hot_new = jnp.where(
            lax.broadcasted_iota(jnp.int32, (HALO, n_selp), 1) == q0 // SEL_BLOCK, 1.0, 0.0).astype(BF16)
        attend(xn[:, :nk], xn[:, nk:], hot_new, jnp.where(new_ok, 0.0, NEG_INF))

        qbd = qbd_ref[0]
        xw = win_ref[0].astype(BF16)
        wrow = lax.broadcasted_iota(jnp.int32, (xw.shape[0], LANES), 0)
        wpos = w_off + wrow
        wvalid = (wpos <= qpos) & (wpos > qpos - WINDOW) & (wpos >= 0)
        s_c = jnp.dot(xw[:, :nk], qbd, preferred_element_type=F32) + jnp.where(wvalid, 0.0, NEG_INF)
        xwn = jnp.concatenate([new_ref[0, :, 4 * nk:6 * nk], pad], axis=0).astype(BF16)
        npos = q0 + row
        nvalid = (row < t_new) & (npos <= qpos) & (npos > qpos - WINDOW)
        s_n = jnp.dot(xwn[:, :nk], qbd, preferred_element_type=F32) + jnp.where(nvalid, 0.0, NEG_INF)
        m_w = jnp.maximum(jnp.max(s_c, axis=0, keepdims=True), jnp.max(s_n, axis=0, keepdims=True))
        e_c = jnp.exp2(s_c - m_w)
        e_n = jnp.exp2(s_n - m_w)
        den = jnp.sum(e_c, axis=0, keepdims=True) + jnp.sum(e_n, axis=0, keepdims=True)
        o_w = _tn_dot(xw[:, nk:], e_c.astype(BF16)) + _tn_dot(xwn[:, nk:], e_n.astype(BF16))
        o_w = o_w * (1.0 / jnp.maximum(den, TINY))
        l_s = jnp.maximum(jnp.sum(l_ref[...], axis=0, keepdims=True), TINY)
        o_s = acc_ref[...] * (1.0 / l_s)
        gate = gate_ref[0]
        o_ref[0] = gate[0:1, :] * oc_ref[...] + gate[1:2, :] * o_s + gate[2:3, :] * o_w


def _dec_page_index(b, g, pt, *, i, n_pg):
    return (pt[b, g * n_pg + i], 0, 0)


def nsa_attend_dec(table, qc, qr, gates, cmp_p, sel_pages, kvp, cache_win, qb, q0, n_sel):
    B, n_pages = table.shape
    group = N_HEADS // N_KV
    assert N_KV * group * qb == LANES
    nk = N_KV * HEAD_DIM
    n_pg = 16 if n_pages % 16 == 0 else n_pages
    n_cmp = cmp_p.shape[2]
    n_selp = -(-n_sel // LANES) * LANES
    wl = cache_win.shape[1]
    eye = jnp.eye(N_KV, dtype=F32)

    def spread(q):
        qt = q[:, :, 0, :, :HEAD_DIM].astype(F32).transpose(0, 1, 3, 2)
        return jnp.einsum("bkdr,kj->bkdjr", qt, eye).reshape(B, N_KV, HEAD_DIM, LANES)

    qc_pad = jnp.pad(spread(qc), ((0, 0), (0, 0), (0, LANES - HEAD_DIM), (0, 0))).astype(BF16)
    q_bd = spread(qr).reshape(B, nk, LANES).astype(BF16)
    gate_l = gates[:, :, :3 * N_HEADS].reshape(B, qb, N_KV, group, 3).transpose(0, 4, 2, 3, 1)
    gate_l = gate_l.reshape(B, 3, LANES)
    wmap = _overlap_map(n_cmp, n_selp)
    lane = np.arange(LANES)
    fold = jnp.asarray((lane[:, None] // (group * qb) == lane[None, :] // (group * qb))
                       & (lane[:, None] % qb == lane[None, :] % qb), BF16)
    l_past = n_pages * PAGE
    hot = np.zeros((l_past, n_selp), np.float32)
    hot[np.arange(l_past), np.arange(l_past) // SEL_BLOCK] = 1.0
    hot = jnp.asarray(hot, BF16)
    kern = functools.partial(_nsa_dec_kernel, n_pg=n_pg, qb=qb, n_sel=n_sel, n_selp=n_selp,
                             n_cmp=n_cmp, q0=q0, w_off=q0 - wl)
    in_specs = [pl.BlockSpec((1, PAGE, 2 * nk), functools.partial(_dec_page_index, i=i, n_pg=n_pg))
                for i in range(n_pg)]
    in_specs += [
        pl.BlockSpec((1, N_KV, LANES, LANES), lambda b, g, pt: (b, 0, 0, 0)),
        pl.BlockSpec((1, nk, LANES), lambda b, g, pt: (b, 0, 0)),
        pl.BlockSpec((1, 3, LANES), lambda b, g, pt: (b, 0, 0)),
        pl.BlockSpec((1, N_KV, n_cmp, LANES), lambda b, g, pt: (b, 0, 0, 0)),
        pl.BlockSpec((1, qb, 6 * nk), lambda b, g, pt: (b, 0, 0)),
        pl.BlockSpec((1, wl, 2 * nk), lambda b, g, pt: (b, 0, 0)),
        pl.BlockSpec((n_selp, n_cmp), lambda b, g, pt: (0, 0)),
        pl.BlockSpec((LANES, LANES), lambda b, g, pt: (0, 0)),
        pl.BlockSpec((l_past, n_selp), lambda b, g, pt: (0, 0), pipeline_mode=pl.Buffered(1)),
    ]
    grid_spec = pltpu.PrefetchScalarGridSpec(
        num_scalar_prefetch=1,
        grid=(B, n_pages // n_pg),
        in_specs=in_specs,
        out_specs=pl.BlockSpec((1, nk, LANES), lambda b, g, pt: (b, 0, 0)),
        scratch_shapes=[
            pltpu.VMEM((nk + n_selp, LANES), BF16),
            pltpu.VMEM((nk, LANES), F32),
            pltpu.VMEM((1, LANES), F32),
            pltpu.VMEM((8, LANES), F32),
            pltpu.VMEM((nk, LANES), F32),
        ],
    )
    o_t = pl.pallas_call(
        kern,
        out_shape=jax.ShapeDtypeStruct((B, nk, LANES), F32),
        grid_spec=grid_spec,
        compiler_params=_cparams(("parallel", "arbitrary")),
        name="nsa_attend_dec",
    )(table, *([sel_pages] * n_pg), qc_pad, q_bd, gate_l, cmp_p, kvp, cache_win, wmap, fold, hot)
    o6 = o_t.reshape(B, N_KV, HEAD_DIM, N_KV, group, qb)
    o5 = jnp.einsum("bkdkgq->bqkgd", o6)
    return o5.reshape(B * qb, N_HEADS * HEAD_DIM)


def _rope_tables(pos, half):
    inv = jnp.exp(-math.log(ROPE_THETA) * jnp.arange(half, dtype=F32) / half)
    ang = pos.astype(F32)[:, None] * inv[None, :]
    return jnp.cos(ang), jnp.sin(ang)


def _prep_weights(ret_w_in, ret_w_out, ffn_w_in, ffn_w_out, kv_w, kv_knorm, cmp_w1, cmp_w2,
                  nsa_w_qg, nsa_qnorm, nsa_w_o):
    nq = N_HEADS * HEAD_DIM
    qg_pad = nq + LANES - nsa_w_qg.shape[2]
    w_qg = jnp.pad(nsa_w_qg, ((0, 0), (0, 0), (0, qg_pad))).astype(BF16)
    R = CMP_LEN // CMP_STRIDE
    w1 = cmp_w1.reshape(2, R, CMP_STRIDE, HEAD_DIM, CMP_HID).transpose(0, 2, 3, 1, 4)
    w1 = w1.reshape(2, CMP_STRIDE, HEAD_DIM, R * CMP_HID)
    w1dup = jnp.concatenate([w1, w1], axis=2).astype(BF16)
    w1dup = w1dup.reshape(2, CMP_STRIDE // 2, 2 * LANES, R * CMP_HID)
    z = jnp.zeros((CMP_HID, HEAD_DIM), F32)
    w2p = jnp.stack([jnp.concatenate([cmp_w2[0], z], axis=1),
                     jnp.concatenate([z, cmp_w2[1]], axis=1)]).astype(BF16)
    ones = jnp.ones((HEAD_DIM,), F32)
    return dict(
        ret_w_in=ret_w_in.astype(BF16), ret_w_out=ret_w_out.astype(BF16),
        ffn_w_in=ffn_w_in.astype(BF16), ffn_w_out=ffn_w_out.astype(BF16),
        kv_w=kv_w.astype(BF16), w_qg=w_qg, w_o_raw=nsa_w_o.astype(BF16),
        kv_gain=jnp.tile(kv_knorm, (1, 2)),
        cmp_gain=jnp.concatenate([kv_knorm[0], ones]).reshape(1, LANES),
        q_gain2=jnp.tile(nsa_qnorm, (1, 2)) * (HEAD_DIM ** -0.5 * math.log2(math.e)),
        score_bound=(1.05 * HEAD_DIM ** 0.5 * math.log2(math.e))
        * jnp.max(jnp.abs(nsa_qnorm), axis=1)[:, None] * jnp.max(jnp.abs(kv_knorm), axis=1)[None, :],
        w1dup=w1dup, w2p=w2p,
        bd=jnp.asarray(np.kron(np.eye(2), np.ones((HEAD_DIM, HEAD_DIM))), BF16),
    )


def _trunk(x, past_len, ret_s0, conv0, ctx, W, P):
    B, T, D = x.shape
    M = B * T
    depth = P["norm_mix"].shape[0]
    n_a = P["ret_w_in"].shape[0]
    pos = past_len + jnp.arange(T)
    cos_r, sin_r = _rope_tables(pos, RET_DK // 2)
    c32, s32 = _rope_tables(pos, HEAD_DIM // 2)
    cos_n = jnp.tile(c32, (1, 4))
    sin_n = jnp.concatenate([-s32, s32, -s32, s32], axis=1)
    lg = jnp.log1p(-jnp.exp2(-5.0 - jnp.arange(RET_HEADS, dtype=F32)))
    L = RET_CHUNK if T % RET_CHUNK == 0 else T
    gl = jnp.exp(L * lg)
    tm = min(512, M)
    tf = 1024 if T % 1024 == 0 else 512
    tb = min(512, T)
    tq = min(512, T)
    qb = next((c for c in (2 * Q_BLOCK, Q_BLOCK) if T % c == 0), T)
    tt = min(256, T)

    x2 = x.reshape(M, D)
    ret_states, conv_states = [], []
    for layer in range(depth):
        if layer == n_a:
            kvp, selp, winp = kv_project(x2.reshape(B, T, D), P["kv_norm"], W["kv_w"], W["kv_gain"],
                                         cos_n, sin_n, W["bd"], tq)
            nk2 = 2 * N_KV * HEAD_DIM
            if ctx is None:
                table = jnp.arange(M // PAGE, dtype=jnp.int32).reshape(B, T // PAGE)
                cmp_p = compress(kvp.reshape(M // PAGE, PAGE, 3 * nk2), table,
                                 W["w1dup"], W["pb"], W["w2p"], W["cmp_gain"])
                sel_p, win_p = selp, winp
                w_off = 0
            else:
                cache_cmp, cache_sel, cache_win, table = ctx
                n_pool = cache_cmp.shape[0]
                cmp_p = compress(cache_cmp.reshape(n_pool, PAGE, nk2), table,
                                 W["w1dup"], W["pb"], W["w2p"], W["cmp_gain"])
                sel_pages = cache_sel.reshape(n_pool, PAGE, nk2)
                win_rows = cache_win.reshape(B, cache_win.shape[1], nk2)
            n_sel = -(-(past_len + T) // SEL_BLOCK)
        h_norm = P["norm_mix"][layer]
        if layer < n_a:
            proj = norm_matmul(x2, h_norm, W["ret_w_in"][layer], min(1024, M), 1024)
            og, s_new = retention(proj.reshape(B, T, -1), ret_s0[layer], cos_r, sin_r, lg, gl, L, tb)
            ret_states.append(s_new)
            x2 = matmul_res(og.reshape(M, -1), W["ret_w_out"][layer], x2, tm)
        else:
            j = layer - n_a
            if qb % LANES == 0:
                qc, qr, gates = qg_project_t(x2.reshape(B, T, D), h_norm, W["w_qg"][j],
                                             W["q_gain2"][j:j + 1], cos_n, sin_n, W["bd"], tq, qb)
                shifts = W["score_bound"][j]
                args = (qc, qr, gates, cmp_p, sel_p, win_p)
                o = lax.cond(
                    jnp.max(shifts) <= 30.0,
                    lambda a: nsa_attend_fast(shifts, *a, T, qb, past_len, w_off, n_sel),
                    lambda a: nsa_attend_t(*a, T, qb, past_len, w_off, n_sel).reshape(
                        B, T, N_HEADS, LANES)[..., HEAD_DIM:].reshape(B, T, N_HEADS * HEAD_DIM),
                    args)
                x2 = matmul_res(o.reshape(M, -1), W["w_o_raw"][j], x2, tm)
            else:
                qc, qr, gates = qg_project(x2.reshape(B, T, D), h_norm, W["w_qg"][j],
                                           W["q_gain2"][j:j + 1], cos_n, sin_n, W["bd"], tq, qb)
                o = nsa_attend_dec(table, qc, qr, gates, cmp_p, sel_pages, kvp, win_rows,
                                   qb, past_len, n_sel)
                x2 = matmul_res(o, W["w_o_raw"][j], x2, tm)
        if T % tf == 0:
            act, tail = ffn_in(x2, P["norm_ffn"][layer], W["ffn_w_in"][layer], conv0[layer],
                               P["ffn_conv_w"][layer], P["ffn_conv_b"][layer], T, tf, 256)
            conv_states.append(tail)
        else:
            proj = norm_matmul(x2, P["norm_ffn"][layer], W["ffn_w_in"][layer], tm, 512)
            proj3 = proj.reshape(B, T, 2 * D_FF)
            act = ffn_mid(proj3, conv0[layer], P["ffn_conv_w"][layer], P["ffn_conv_b"][layer], tt)
            conv_states.append(proj3[:, T - 2:, :D_FF])
        x2 = matmul_res(act.reshape(M, D_FF), W["ffn_w_out"][layer], x2, min(1024, M))

    nk = N_KV * HEAD_DIM
    new_cmp = kvp[:, :, 0:2 * nk].reshape(B, T, 2, N_KV, HEAD_DIM)
    new_sel = kvp[:, :, 2 * nk:4 * nk].reshape(B, T, 2, N_KV, HEAD_DIM)
    new_win = kvp[:, :, 4 * nk:6 * nk].reshape(B, T, 2, N_KV, HEAD_DIM)
    return (x2.reshape(B, T, D), jnp.stack(ret_states), jnp.stack(conv_states),
            new_cmp, new_sel, new_win)


def kernel(x_prompt, x_sample, cache_cmp_kv, cache_sel_kv, cache_win_kv, state_ret, state_conv,
           page_table, norm_mix, norm_ffn, ret_w_in, ret_w_out, ffn_w_in, ffn_conv_w, ffn_conv_b,
           ffn_w_out, kv_norm, kv_w, kv_knorm, cmp_pos, cmp_w1, cmp_w2, nsa_w_qg, nsa_qnorm, nsa_w_o):
    W = _prep_weights(ret_w_in, ret_w_out, ffn_w_in, ffn_w_out, kv_w, kv_knorm, cmp_w1, cmp_w2,
                      nsa_w_qg, nsa_qnorm, nsa_w_o)
    W["pb"] = pos_bias(cmp_pos, cmp_w1)
    P = dict(norm_mix=norm_mix, norm_ffn=norm_ffn, ret_w_in=ret_w_in, ffn_conv_w=ffn_conv_w,
             ffn_conv_b=ffn_conv_b, kv_norm=kv_norm)
    depth = norm_mix.shape[0]
    n_a = ret_w_in.shape[0]
    B, T, _ = x_prompt.shape
    zero_ret = jnp.zeros((n_a, B, RET_HEADS, RET_DK, RET_DV), F32)
    zero_conv = jnp.zeros((depth, B, 2, D_FF), F32)
    y_p, ret_p, conv_p, cmp_p, sel_p, win_p = _trunk(x_prompt, 0, zero_ret, zero_conv, None, W, P)
    win_p = win_p[:, T - min(WINDOW, T):]

    db, ts, _ = x_sample.shape
    past_len = page_table.shape[1] * PAGE
    ctx = (cache_cmp_kv, cache_sel_kv, cache_win_kv, page_table)
    y_s, ret_s, conv_s, cmp_s, sel_s, win_new = _trunk(x_sample, past_len, state_ret, state_conv,
                                                        ctx, W, P)
    all_win = jnp.concatenate([cache_win_kv, win_new], axis=1)
    win_s = all_win[:, all_win.shape[1] - min(WINDOW, past_len + ts):]
    return (y_p, y_s, ret_p, ret_s, conv_p, conv_s, cmp_p, cmp_s, sel_p, sel_s, win_p, win_s)
```

```python
import functools
import math

import jax
import jax.numpy as jnp
import numpy as np
from jax import lax
from jax.experimental import pallas as pl
from jax.experimental.pallas import tpu as pltpu

F32 = jnp.float32
BF16 = jnp.bfloat16

D_MODEL = 1024
PAGE = 128
RET_HEADS = 4
RET_DK = 256
RET_DV = 512
RET_CHUNK = 128
N_HEADS = 16
N_KV = 4
HEAD_DIM = 64
CMP_LEN = 32
CMP_STRIDE = 16
CMP_HID = 128
SEL_BLOCK = 64
SEL_TOPK = 16
WINDOW = 512
Q_BLOCK = 128
D_FF = 2816
ROPE_THETA = 10000.0
EPS = 1e-6
NEG_INF = -1e30
TINY = 1e-30
SEL_FORCE = 1e6
SEL_NEG = -1e6

LANES = 128
KEY_TILE = 512
HALO = 16
VMEM_LIMIT = 48 * 1024 * 1024


def _cparams(sem):
    return pltpu.CompilerParams(dimension_semantics=sem, vmem_limit_bytes=VMEM_LIMIT)


def _nt_dot(a, b):
    return lax.dot_general(a, b, (((1,), (1,)), ((), ())), preferred_element_type=F32)


def _tn_dot(a, b):
    return lax.dot_general(a, b, (((0,), (0,)), ((), ())), preferred_element_type=F32)


def _gelu(x):
    return 0.5 * x * (1.0 + jnp.tanh(math.sqrt(2.0 / math.pi) * (x + 0.044715 * (x * x * x))))


def _rms_rows(x, g):
    r = lax.rsqrt(jnp.mean(x * x, axis=-1, keepdims=True) + EPS)
    return x * r * g


def _head_ms(x, bd):
    x2 = x * x
    hi = x2.astype(BF16)
    lo = (x2 - hi.astype(F32)).astype(BF16)
    s = jnp.dot(hi, bd, preferred_element_type=F32) + jnp.dot(lo, bd, preferred_element_type=F32)
    return s * (1.0 / HEAD_DIM)


def _rope64(x, cos, sin):
    lane = lax.broadcasted_iota(jnp.int32, x.shape, 1)
    sw = jnp.where((lane & 63) < 32, pltpu.roll(x, 96, 1), pltpu.roll(x, 32, 1))
    return x * cos + sw * sin


def _pack_pair(k2, v2, dtype):
    lane = lax.broadcasted_iota(jnp.int32, k2.shape, 1)
    lo = lane < HEAD_DIM
    even = jnp.where(lo, k2, pltpu.roll(v2, HEAD_DIM, 1)).astype(dtype)
    odd = jnp.where(lo, pltpu.roll(k2, HEAD_DIM, 1), v2).astype(dtype)
    return even, odd


def _norm_matmul_kernel(x_ref, g_ref, w_ref, o_ref, h_ref):
    @pl.when(pl.program_id(1) == 0)
    def _():
        h_ref[...] = _rms_rows(x_ref[...], g_ref[...]).astype(BF16)

    o_ref[...] = jnp.dot(h_ref[...], w_ref[...], preferred_element_type=F32).astype(o_ref.dtype)


def norm_matmul(x, g, w, tm, tn):
    M, D = x.shape
    N = w.shape[1]
    return pl.pallas_call(
        _norm_matmul_kernel,
        out_shape=jax.ShapeDtypeStruct((M, N), F32),
        grid=(M // tm, N // tn),
        in_specs=[pl.BlockSpec((tm, D), lambda i, j: (i, 0)),
                  pl.BlockSpec((1, D), lambda i, j: (0, 0)),
                  pl.BlockSpec((D, tn), lambda i, j: (0, j))],
        out_specs=pl.BlockSpec((tm, tn), lambda i, j: (i, j)),
        scratch_shapes=[pltpu.VMEM((tm, D), BF16)],
        compiler_params=_cparams(("parallel", "arbitrary")),
        name="norm_matmul",
    )(x, g.reshape(1, D), w)


def _matmul_res_kernel(a_ref, w_ref, r_ref, o_ref):
    o_ref[...] = r_ref[...] + jnp.dot(a_ref[...].astype(BF16), w_ref[...],
                                      preferred_element_type=F32)


def matmul_res(a, w, res, tm):
    M, K = a.shape
    N = w.shape[1]
    return pl.pallas_call(
        _matmul_res_kernel,
        out_shape=jax.ShapeDtypeStruct((M, N), F32),
        grid=(M // tm,),
        in_specs=[pl.BlockSpec((tm, K), lambda i: (i, 0)),
                  pl.BlockSpec((K, N), lambda i: (0, 0)),
                  pl.BlockSpec((tm, N), lambda i: (i, 0))],
        out_specs=pl.BlockSpec((tm, N), lambda i: (i, 0)),
        compiler_params=_cparams(("parallel",)),
        name="matmul_res",
    )(a, w, res)


def _retention_kernel(lg_ref, gl_ref, q_ref, k_ref, v_ref, g_ref, cos_ref, sin_ref, s0_ref,
                      o_ref, sout_ref, S_ref, *, L, n_chunk):
    h = pl.program_id(1)
    t = pl.program_id(2)
    lg = lg_ref[h]
    gl = gl_ref[h]

    @pl.when(t == 0)
    def _():
        S_ref[...] = s0_ref[0, 0]

    ii = lax.broadcasted_iota(jnp.int32, (L, L), 0)
    jj = lax.broadcasted_iota(jnp.int32, (L, L), 1)
    diff = (ii - jj).astype(F32)
    decay = jnp.where(diff >= 0, jnp.exp(jnp.maximum(diff, 0.0) * lg), 0.0)
    idx = lax.broadcasted_iota(jnp.int32, (L, 1), 0).astype(F32)
    q_dec = jnp.exp((idx + 1.0) * lg)
    k_dec = jnp.exp((L - 1.0 - idx) * lg)
    half = RET_DK // 2

    for c in range(n_chunk):
        rows = pl.ds(c * L, L)
        cos = cos_ref[rows, :]
        sin = sin_ref[rows, :]

        def rope(x):
            x1, x2 = x[:, :half], x[:, half:]
            return jnp.concatenate([x1 * cos - x2 * sin, x2 * cos + x1 * sin], axis=1)

        qr = rope(q_ref[0, rows, :])
        kr = rope(k_ref[0, rows, :]) * (RET_DK ** -0.5)
        qb = qr.astype(BF16)
        vb = v_ref[0, rows, :].astype(BF16)
        sc = _nt_dot(qb, kr.astype(BF16)) * decay
        S = S_ref[...]
        o = jnp.dot(sc.astype(BF16), vb, preferred_element_type=F32)
        o = o + jnp.dot(qb, S.astype(BF16), preferred_element_type=F32) * q_dec
        S_ref[...] = S * gl + _tn_dot((kr * k_dec).astype(BF16), vb)
        on = o * lax.rsqrt(jnp.mean(o * o, axis=-1, keepdims=True) + EPS)
        g = g_ref[0, rows, :]
        o_ref[0, rows, :] = (on * (g * jax.nn.sigmoid(g))).astype(o_ref.dtype)

    @pl.when(t == pl.num_programs(2) - 1)
    def _():
        sout_ref[0, 0] = S_ref[...]


def retention(proj, s0, cos, sin, lg, gl, L, tb):
    B, T, _ = proj.shape
    n_chunk = tb // L
    odt = BF16 if tb % 16 == 0 else F32
    kern = functools.partial(_retention_kernel, L=L, n_chunk=n_chunk)
    grid_spec = pltpu.PrefetchScalarGridSpec(
        num_scalar_prefetch=2,
        grid=(B, RET_HEADS, T // tb),
        in_specs=[
            pl.BlockSpec((1, tb, RET_DK), lambda b, h, t, *_: (b, t, h)),
            pl.BlockSpec((1, tb, RET_DK), lambda b, h, t, *_: (b, t, RET_HEADS + h)),
            pl.BlockSpec((1, tb, RET_DV), lambda b, h, t, *_: (b, t, RET_HEADS + h)),
            pl.BlockSpec((1, tb, RET_DV), lambda b, h, t, *_: (b, t, 2 * RET_HEADS + h)),
            pl.BlockSpec((tb, RET_DK // 2), lambda b, h, t, *_: (t, 0)),
            pl.BlockSpec((tb, RET_DK // 2), lambda b, h, t, *_: (t, 0)),
            pl.BlockSpec((1, 1, RET_DK, RET_DV), lambda b, h, t, *_: (b, h, 0, 0)),
        ],
        out_specs=[
            pl.BlockSpec((1, tb, RET_DV), lambda b, h, t, *_: (b, t, h)),
            pl.BlockSpec((1, 1, RET_DK, RET_DV), lambda b, h, t, *_: (b, h, 0, 0)),
        ],
        scratch_shapes=[pltpu.VMEM((RET_DK, RET_DV), F32)],
    )
    return pl.pallas_call(
        kern,
        out_shape=[jax.ShapeDtypeStruct((B, T, RET_HEADS * RET_DV), odt),
                   jax.ShapeDtypeStruct((B, RET_HEADS, RET_DK, RET_DV), F32)],
        grid_spec=grid_spec,
        compiler_params=_cparams(("parallel", "parallel", "arbitrary")),
        name="retention",
    )(lg, gl, proj, proj, proj, proj, cos, sin, s0)


def _ffn_mid_kernel(u_ref, gt_ref, halo_ref, cw_ref, cb_ref, o_ref):
    u = u_ref[0]
    hl = halo_ref[0, 0]
    row = lax.broadcasted_iota(jnp.int32, u.shape, 0)
    u1 = jnp.where(row == 0, hl[1:2], pltpu.roll(u, 1, 0))
    u2 = jnp.where(row == 0, hl[0:1], jnp.where(row == 1, hl[1:2], pltpu.roll(u, 2, 0)))
    c = cb_ref[...] + cw_ref[0:1] * u2
    c = c + cw_ref[1:2] * u1
    c = c + cw_ref[2:3] * u
    o_ref[0] = (_gelu(c) * gt_ref[0]).astype(o_ref.dtype)


def ffn_mid(proj, buf, conv_w, conv_b, tt):
    B, T, _ = proj.shape
    nt = T // tt
    if nt > 1:
        tails = proj[:, :, :D_FF].reshape(B, nt, tt, D_FF)[:, :-1, tt - 2:, :]
        halo = jnp.concatenate([buf[:, None], tails], axis=1)
    else:
        halo = buf[:, None]
    odt = BF16 if tt % 16 == 0 else F32
    return pl.pallas_call(
        _ffn_mid_kernel,
        out_shape=jax.ShapeDtypeStruct((B, T, D_FF), odt),
        grid=(B, nt),
        in_specs=[pl.BlockSpec((1, tt, D_FF), lambda b, t: (b, t, 0)),
                  pl.BlockSpec((1, tt, D_FF), lambda b, t: (b, t, 1)),
                  pl.BlockSpec((1, 1, 2, D_FF), lambda b, t: (b, t, 0, 0)),
                  pl.BlockSpec((3, D_FF), lambda b, t: (0, 0)),
                  pl.BlockSpec((1, D_FF), lambda b, t: (0, 0))],
        out_specs=pl.BlockSpec((1, tt, D_FF), lambda b, t: (b, t, 0)),
        compiler_params=_cparams(("parallel", "parallel")),
        name="ffn_mid",
    )(proj, proj, halo, conv_w, conv_b.reshape(1, D_FF))


def _ffn_in_kernel(x_ref, xh_ref, g_ref, w_ref, buf_ref, cw_ref, cb_ref,
                   act_ref, tail_ref, *, tiles_per_seq, tn):
    seq_start = (pl.program_id(0) % tiles_per_seq) == 0
    h = _rms_rows(x_ref[...], g_ref[...]).astype(BF16)
    hh = _rms_rows(xh_ref[...], g_ref[...]).astype(BF16)
    tm = h.shape[0]
    row = lax.broadcasted_iota(jnp.int32, (tm, tn), 0)
    for j in range(D_FF // tn):
        cols = slice(j * tn, (j + 1) * tn)
        wu = w_ref[:, cols]
        u = jnp.dot(h, wu, preferred_element_type=F32)
        gt = jnp.dot(h, w_ref[:, D_FF + j * tn:D_FF + (j + 1) * tn], preferred_element_type=F32)
        uh = jnp.dot(hh, wu, preferred_element_type=F32)
        hl = jnp.where(seq_start, buf_ref[0, :, cols], uh[HALO - 2:, :])
        u1 = jnp.where(row == 0, hl[1:2], pltpu.roll(u, 1, 0))
        u2 = jnp.where(row == 0, hl[0:1], jnp.where(row == 1, hl[1:2], pltpu.roll(u, 2, 0)))
        c = cb_ref[:, cols] + cw_ref[0:1, cols] * u2
        c = c + cw_ref[1:2, cols] * u1
        c = c + cw_ref[2:3, cols] * u
        act_ref[:, cols] = (_gelu(c) * gt).astype(act_ref.dtype)
        tail_ref[0, :, cols] = u[tm - 2:, :]


def ffn_in(x, g, w, buf, conv_w, conv_b, T, tm, tn):
    M, D = x.shape
    tiles_per_seq = T // tm
    kern = functools.partial(_ffn_in_kernel, tiles_per_seq=tiles_per_seq, tn=tn)
    act, tails = pl.pallas_call(
        kern,
        out_shape=[jax.ShapeDtypeStruct((M, D_FF), BF16),
                   jax.ShapeDtypeStruct((M // tm, 2, D_FF), F32)],
        grid=(M // tm,),
        in_specs=[pl.BlockSpec((tm, D), lambda i: (i, 0)),
                  pl.BlockSpec((HALO, D), lambda i: (jnp.maximum(i * (tm // HALO) - 1, 0), 0)),
                  pl.BlockSpec((1, D), lambda i: (0, 0)),
                  pl.BlockSpec((D, 2 * D_FF), lambda i: (0, 0), pipeline_mode=pl.Buffered(1)),
                  pl.BlockSpec((1, 2, D_FF), lambda i: (i // tiles_per_seq, 0, 0)),
                  pl.BlockSpec((3, D_FF), lambda i: (0, 0)),
                  pl.BlockSpec((1, D_FF), lambda i: (0, 0))],
        out_specs=[pl.BlockSpec((tm, D_FF), lambda i: (i, 0)),
                   pl.BlockSpec((1, 2, D_FF), lambda i: (i, 0, 0))],
        compiler_params=_cparams(("parallel",)),
        name="ffn_in",
    )(x, x, g.reshape(1, D), w, buf, conv_w, conv_b.reshape(1, D_FF))
    return act, tails[tiles_per_seq - 1::tiles_per_seq]


def _kv_kernel(x_ref, g_ref, w_ref, gain_ref, cos_ref, sin_ref, bd_ref,
               kv_ref, selp_ref, winp_ref):
    h = _rms_rows(x_ref[0], g_ref[...]).astype(BF16)
    y = jnp.dot(h, w_ref[...], preferred_element_type=F32)
    cos = cos_ref[...]
    sin = sin_ref[...]
    bd = bd_ref[...]
    nk = N_KV * HEAD_DIM
    kv_ref[0, :, 0:2 * nk] = y[:, 0:2 * nk]
    for br, pack_ref in enumerate((selp_ref, winp_ref)):
        base = 2 * nk * (br + 1)
        kv_ref[0, :, base + nk:base + 2 * nk] = y[:, base + nk:base + 2 * nk]
        for p in range(2):
            kx = y[:, base + LANES * p:base + LANES * (p + 1)]
            kn = kx * lax.rsqrt(_head_ms(kx, bd) + EPS) * gain_ref[br + 1:br + 2, :]
            kr = _rope64(kn, cos, sin)
            kv_ref[0, :, base + LANES * p:base + LANES * (p + 1)] = kr
            vx = y[:, base + nk + LANES * p:base + nk + LANES * (p + 1)]
            even, odd = _pack_pair(kr, vx, pack_ref.dtype)
            pack_ref[0, 2 * p] = even
            pack_ref[0, 2 * p + 1] = odd


def kv_project(x, g, w, gains, cos, sin, bd, tm):
    B, T, D = x.shape
    N = w.shape[1]
    pdt = BF16 if tm % 16 == 0 else F32
    return pl.pallas_call(
        _kv_kernel,
        out_shape=[jax.ShapeDtypeStruct((B, T, N), F32),
                   jax.ShapeDtypeStruct((B, N_KV, T, LANES), pdt),
                   jax.ShapeDtypeStruct((B, N_KV, T, LANES), pdt)],
        grid=(B, T // tm),
        in_specs=[pl.BlockSpec((1, tm, D), lambda b, t: (b, t, 0)),
                  pl.BlockSpec((1, D), lambda b, t: (0, 0)),
                  pl.BlockSpec((D, N), lambda b, t: (0, 0)),
                  pl.BlockSpec((3, LANES), lambda b, t: (0, 0)),
                  pl.BlockSpec((tm, LANES), lambda b, t: (t, 0)),
                  pl.BlockSpec((tm, LANES), lambda b, t: (t, 0)),
                  pl.BlockSpec((LANES, LANES), lambda b, t: (0, 0))],
        out_specs=[pl.BlockSpec((1, tm, N), lambda b, t: (b, t, 0)),
                   pl.BlockSpec((1, N_KV, tm, LANES), lambda b, t: (b, 0, t, 0)),
                   pl.BlockSpec((1, N_KV, tm, LANES), lambda b, t: (b, 0, t, 0))],
        compiler_params=_cparams(("parallel", "parallel")),
        name="kv_project",
    )(x, g.reshape(1, D), w, gains, cos, sin, bd)


def _qg_kernel(x_ref, g_ref, w_ref, gain_ref, cos_ref, sin_ref, bd_ref,
               qc_ref, qr_ref, gate_ref, *, qb, n_qb):
    h = _rms_rows(x_ref[0], g_ref[...]).astype(BF16)
    y = jnp.dot(h, w_ref[...], preferred_element_type=F32)
    cos = cos_ref[...]
    sin = sin_ref[...]
    bd = bd_ref[...]
    nq = N_HEADS * HEAD_DIM
    gate_ref[0] = jax.nn.sigmoid(y[:, nq:nq + LANES])
    lane = lax.broadcasted_iota(jnp.int32, (y.shape[0], LANES), 1)
    lo = lane < HEAD_DIM
    group = N_HEADS // N_KV
    for p in range(N_HEADS // 2):
        qx = y[:, LANES * p:LANES * (p + 1)]
        qn = qx * lax.rsqrt(_head_ms(qx, bd) + EPS) * gain_ref[...]
        qr = _rope64(qn, cos, sin)
        for src, dst in ((qn, qc_ref), (qr, qr_ref)):
            for par in range(2):
                hd = 2 * p + par
                kvh, gi = hd // group, hd % group
                v = src if par == 0 else pltpu.roll(src, HEAD_DIM, 1)
                v = jnp.where(lo, v, 0.0).astype(dst.dtype)
                for j in range(n_qb):
                    dst[0, kvh, j, gi * qb:(gi + 1) * qb, :] = v[j * qb:(j + 1) * qb, :]


def qg_project(x, g, w, gain, cos, sin, bd, tm, qb):
    B, T, D = x.shape
    N = w.shape[1]
    n_qb = tm // qb
    group = N_HEADS // N_KV
    qdt = BF16 if qb % 16 == 0 else F32
    kern = functools.partial(_qg_kernel, qb=qb, n_qb=n_qb)
    qshape = jax.ShapeDtypeStruct((B, N_KV, T // qb, group * qb, LANES), qdt)
    qspec = pl.BlockSpec((1, N_KV, n_qb, group * qb, LANES), lambda b, t: (b, 0, t, 0, 0))
    return pl.pallas_call(
        kern,
        out_shape=[qshape, qshape, jax.ShapeDtypeStruct((B, T, LANES), F32)],
        grid=(B, T // tm),
        in_specs=[pl.BlockSpec((1, tm, D), lambda b, t: (b, t, 0)),
                  pl.BlockSpec((1, D), lambda b, t: (0, 0)),
                  pl.BlockSpec((D, N), lambda b, t: (0, 0)),
                  pl.BlockSpec((1, LANES), lambda b, t: (0, 0)),
                  pl.BlockSpec((tm, LANES), lambda b, t: (t, 0)),
                  pl.BlockSpec((tm, LANES), lambda b, t: (t, 0)),
                  pl.BlockSpec((LANES, LANES), lambda b, t: (0, 0))],
        out_specs=[qspec, qspec, pl.BlockSpec((1, tm, LANES), lambda b, t: (b, t, 0))],
        compiler_params=_cparams(("parallel", "parallel")),
        name="qg_project",
    )(x, g.reshape(1, D), w, gain, cos, sin, bd)


def _pos_bias_kernel(p_ref, w_ref, o_ref):
    o_ref[0] = jnp.dot(p_ref[0].astype(BF16), w_ref[0].astype(BF16), preferred_element_type=F32)


def pos_bias(cmp_pos, cmp_w1):
    K = CMP_LEN * HEAD_DIM
    p = jnp.broadcast_to(cmp_pos.reshape(2, 1, K), (2, 8, K))
    out = pl.pallas_call(
        _pos_bias_kernel,
        out_shape=jax.ShapeDtypeStruct((2, 8, CMP_HID), F32),
        grid=(2,),
        in_specs=[pl.BlockSpec((1, 8, K), lambda c: (c, 0, 0)),
                  pl.BlockSpec((1, K, CMP_HID), lambda c: (c, 0, 0))],
        out_specs=pl.BlockSpec((1, 8, CMP_HID), lambda c: (c, 0, 0)),
        name="pos_bias",
    )(p, cmp_w1)
    return out[:, 0, :]


def _compress_kernel(pt_ref, *refs, n_pg):
    pages = refs[:n_pg]
    w1_ref, pb_ref, w2_ref, gain_ref, out_ref, carry_ref, slab_ref = refs[n_pg:]
    g = pl.program_id(1)

    @pl.when(g == 0)
    def _():
        carry_ref[...] = jnp.zeros_like(carry_ref)

    n = n_pg * (PAGE // CMP_STRIDE)
    lo = lax.broadcasted_iota(jnp.int32, (n, LANES), 1) < HEAD_DIM
    row = lax.broadcasted_iota(jnp.int32, (N_KV * n, LANES), 0)
    res = jnp.zeros((N_KV * n, LANES), F32)
    for c in range(2):
        for pair in range(N_KV // 2):
            col = c * N_KV * HEAD_DIM + pair * LANES
            for i in range(n_pg):
                slab_ref[pair, PAGE * i:PAGE * (i + 1), :] = pages[i][0, :, col:col + LANES]
        acc = jnp.zeros((N_KV * n, 2 * CMP_HID), F32)
        for s in range(0, CMP_STRIDE, 2):
            parts = []
            for k in range(N_KV):
                pair, par = divmod(k, 2)
                keep = lo if par == 0 else jnp.logical_not(lo)
                parts.append(jnp.concatenate(
                    [jnp.where(keep, slab_ref[pair, pl.ds(s + i, n, stride=CMP_STRIDE), :], 0.0)
                     for i in range(2)], axis=1))
            xm = jnp.concatenate(parts, axis=0).astype(BF16)
            acc = acc + jnp.dot(xm, w1_ref[c, s // 2], preferred_element_type=F32)
        p0 = acc[:, :CMP_HID]
        p1 = acc[:, CMP_HID:]
        prev = pltpu.roll(p0, 1, 0)
        for k in range(N_KV):
            prev = jnp.where(row == k * n, carry_ref[2 * k + c, 7:8, :], prev)
        for k in range(N_KV):
            carry_ref[2 * k + c] = p0[(k + 1) * n - 8:(k + 1) * n, :]
        hid = _gelu(prev + p1 + pb_ref[c:c + 1, :]).astype(BF16)
        res = res + jnp.dot(hid, w2_ref[c], preferred_element_type=F32)
    for k in range(N_KV):
        r = res[k * n:(k + 1) * n, :]
        ms = jnp.sum(jnp.where(lo, r * r, 0.0), axis=-1, keepdims=True) * (1.0 / HEAD_DIM)
        kn = r * lax.rsqrt(ms + EPS) * gain_ref[...]
        out_ref[0, k] = jnp.where(lo, kn, r).astype(out_ref.dtype)


def _page_index(b, g, pt_ref, *, i, n_pg):
    return (pt_ref[b, g * n_pg + i], 0, 0)


def compress(pages_arr, table, w1dup, pb, w2p, gain):
    B, n_pages = table.shape
    n_pg = 16 if n_pages % 16 == 0 else n_pages
    n = n_pg * (PAGE // CMP_STRIDE)
    n_sub = n_pages * (PAGE // CMP_STRIDE)
    width = 2 * N_KV * HEAD_DIM
    in_specs = [pl.BlockSpec((1, PAGE, width), functools.partial(_page_index, i=i, n_pg=n_pg))
                for i in range(n_pg)]
    in_specs += [pl.BlockSpec(w1dup.shape, lambda b, g, pt: (0, 0, 0, 0)),
                 pl.BlockSpec(pb.shape, lambda b, g, pt: (0, 0)),
                 pl.BlockSpec(w2p.shape, lambda b, g, pt: (0, 0, 0)),
                 pl.BlockSpec((1, LANES), lambda b, g, pt: (0, 0))]
    grid_spec = pltpu.PrefetchScalarGridSpec(
        num_scalar_prefetch=1,
        grid=(B, n_pages // n_pg),
        in_specs=in_specs,
        out_specs=pl.BlockSpec((1, N_KV, n, LANES), lambda b, g, pt: (b, 0, g, 0)),
        scratch_shapes=[pltpu.VMEM((2 * N_KV, 8, CMP_HID), F32),
                        pltpu.VMEM((N_KV // 2, n_pg * PAGE, LANES), F32)],
    )
    return pl.pallas_call(
        functools.partial(_compress_kernel, n_pg=n_pg),
        out_shape=jax.ShapeDtypeStruct((B, N_KV, n_sub, LANES), BF16),
        grid_spec=grid_spec,
        compiler_params=_cparams(("parallel", "arbitrary")),
        name="compress",
    )(table, *([pages_arr] * n_pg), w1dup, pb, w2p, gain)


def _overlap_map(n_cmp_rows, n_selp):
    m = np.arange(n_cmp_rows)[None, :]
    s = np.arange(n_selp)[:, None]
    c0 = (m - 1) * CMP_STRIDE
    ov = np.minimum(c0 + CMP_LEN, s * SEL_BLOCK + SEL_BLOCK) - np.maximum(c0, s * SEL_BLOCK)
    w = np.maximum(ov, 0).astype(np.float32) / CMP_LEN
    w[:, 0] = 0.0
    return jnp.asarray(w, BF16)


def _qgt_kernel(x_ref, g_ref, w_ref, gain_ref, cos_ref, sin_ref, bd_ref,
                qc_ref, qr_ref, gate_ref, *, qb, n_qb):
    h = _rms_rows(x_ref[0], g_ref[...]).astype(BF16)
    y = jnp.dot(h, w_ref[...], preferred_element_type=F32)
    cos = cos_ref[...]
    sin = sin_ref[...]
    bd = bd_ref[...]
    nq = N_HEADS * HEAD_DIM
    group = N_HEADS // N_KV
    gate_t = jax.nn.sigmoid(y[:, nq:nq + LANES]).T
    for j in range(n_qb):
        gate_ref[0, j] = gate_t[:, j * qb:(j + 1) * qb]
    pad = jnp.zeros((N_KV, n_qb, HEAD_DIM, group * qb), qc_ref.dtype)
    qc_ref[0, :, :, HEAD_DIM:, :] = pad
    qr_ref[0, :, :, HEAD_DIM:, :] = pad
    for p in range(N_HEADS // 2):
        qx = y[:, LANES * p:LANES * (p + 1)]
        qn = qx * lax.rsqrt(_head_ms(qx, bd) + EPS) * gain_ref[...]
        qr = _rope64(qn, cos, sin)
        for src, dst in ((qn, qc_ref), (qr, qr_ref)):
            st = src.T.astype(dst.dtype)
            for par in range(2):
                kvh, gi = divmod(2 * p + par, group)
                for j in range(n_qb):
                    dst[0, kvh, j, 0:HEAD_DIM, gi * qb:(gi + 1) * qb] = (
                        st[par * HEAD_DIM:(par + 1) * HEAD_DIM, j * qb:(j + 1) * qb])


def qg_project_t(x, g, w, gain, cos, sin, bd, tm, qb):
    B, T, D = x.shape
    N = w.shape[1]
    n_qb = tm // qb
    group = N_HEADS // N_KV
    kern = functools.partial(_qgt_kernel, qb=qb, n_qb=n_qb)
    qshape = jax.ShapeDtypeStruct((B, N_KV, T // qb, LANES, group * qb), BF16)
    qspec = pl.BlockSpec((1, N_KV, n_qb, LANES, group * qb), lambda b, t: (b, 0, t, 0, 0))
    return pl.pallas_call(
        kern,
        out_shape=[qshape, qshape, jax.ShapeDtypeStruct((B, T // qb, LANES, qb), F32)],
        grid=(B, T // tm),
        in_specs=[pl.BlockSpec((1, tm, D), lambda b, t: (b, t, 0)),
                  pl.BlockSpec((1, D), lambda b, t: (0, 0)),
                  pl.BlockSpec((D, N), lambda b, t: (0, 0)),
                  pl.BlockSpec((1, LANES), lambda b, t: (0, 0)),
                  pl.BlockSpec((tm, LANES), lambda b, t: (t, 0)),
                  pl.BlockSpec((tm, LANES), lambda b, t: (t, 0)),
                  pl.BlockSpec((LANES, LANES), lambda b, t: (0, 0))],
        out_specs=[qspec, qspec, pl.BlockSpec((1, n_qb, LANES, qb), lambda b, t: (b, t, 0, 0))],
        compiler_params=_cparams(("parallel", "parallel")),
        name="qg_project_t",
    )(x, g.reshape(1, D), w, gain, cos, sin, bd)


def _nsa_t_kernel(qc_ref, qr_ref, gate_ref, cmp_ref, sel_ref, win_ref, wmap_ref, o_ref,
                  selneg_ref, m_ref, l_ref, acc_ref,
                  *, qb, n_sel, n_selp, n_cmp, q0, w_off, w_rows, l_win):
    qi = pl.program_id(1)
    group = N_HEADS // N_KV
    R = group * qb
    blocks_per_tile = KEY_TILE // SEL_BLOCK
    q_lo = q0 + qi * qb
    qpos_q = q_lo + lax.broadcasted_iota(jnp.int32, (1, qb), 1)
    n_kt = (q_lo + qb - 1) // KEY_TILE + 1
    w_start = pl.multiple_of(jnp.clip(q_lo - WINDOW - w_off, 0, l_win - w_rows), LANES)
    gate = gate_ref[0, 0]

    def lanes4(a):
        return jnp.concatenate([a] * group, axis=1)

    m_idx = lax.broadcasted_iota(jnp.int32, (n_cmp, qb), 0)
    cvalid = (m_idx >= 1) & ((m_idx - 1) * CMP_STRIDE + CMP_LEN - 1 <= qpos_q)
    cbias = lanes4(jnp.where(cvalid, 0.0, NEG_INF))
    any_c = lanes4(qpos_q >= CMP_LEN - 1)
    blk = lax.broadcasted_iota(jnp.int32, (n_selp, qb), 0)
    blk_f = blk.astype(F32)
    cur = qpos_q >> 6
    forced = (blk == 0) | (blk == cur) | (blk == cur - 1)
    reach = blk * SEL_BLOCK <= qpos_q
    real = blk < n_sel
    wpos = w_off + w_start + lax.broadcasted_iota(jnp.int32, (w_rows, qb), 0)
    wbias = lanes4(jnp.where((wpos <= qpos_q) & (wpos > qpos_q - WINDOW) & (wpos >= 0), 0.0, NEG_INF))
    row_t = lax.broadcasted_iota(jnp.int32, (KEY_TILE, qb), 0)

    for k in range(N_KV):
        ckv = cmp_ref[0, k]
        s = jnp.dot(ckv, qc_ref[0, k, 0], preferred_element_type=F32) + cbias
        e = jnp.exp2(s - jnp.max(s, axis=0, keepdims=True))
        den = jnp.maximum(jnp.sum(e, axis=0, keepdims=True), TINY)
        p = e * jnp.where(any_c, 1.0 / den, 0.0)
        oc = _tn_dot(ckv, p.astype(BF16))
        psum = p[:, 0:qb]
        for gi in range(1, group):
            psum = psum + p[:, gi * qb:(gi + 1) * qb]
        hi = psum.astype(BF16)
        lo = (psum - hi.astype(F32)).astype(BF16)
        imp = (jnp.dot(wmap_ref[...], hi, preferred_element_type=F32)
               + jnp.dot(wmap_ref[...], lo, preferred_element_type=F32))
        v = jnp.where(forced, SEL_FORCE, jnp.where(reach, imp, SEL_NEG))
        v = jnp.where(real, v, -jnp.inf)

        def pick_one(_, carry):
            v, sel = carry
            m = jnp.max(v, axis=0, keepdims=True)
            first = jnp.min(jnp.where(v == m, blk_f, float(n_selp)), axis=0, keepdims=True)
            pick = blk_f == first
            sel = jnp.where(pick & (m > 0.5 * SEL_NEG), 0.0, sel)
            return jnp.where(pick, -jnp.inf, v), sel

        _, sel = lax.fori_loop(0, SEL_TOPK, pick_one, (v, jnp.full((n_selp, qb), NEG_INF, F32)))
        selneg_ref[...] = sel

        qr = qr_ref[0, k, 0]
        m_ref[...] = jnp.full(m_ref.shape, NEG_INF, F32)
        l_ref[...] = jnp.zeros(l_ref.shape, F32)
        acc_ref[...] = jnp.zeros(acc_ref.shape, F32)

        def tile(kt, carry):
            start = pl.multiple_of(kt * KEY_TILE, KEY_TILE)
            kv = sel_ref[0, k, pl.ds(start, KEY_TILE), :]
            s = jnp.dot(kv, qr, preferred_element_type=F32)
            pieces = [jnp.broadcast_to(selneg_ref[pl.ds(kt * blocks_per_tile + j, 1), :], (SEL_BLOCK, qb))
                      for j in range(blocks_per_tile)]
            bias = jnp.concatenate(pieces, axis=0) + jnp.where(start + row_t <= qpos_q, 0.0, NEG_INF)
            s = s + lanes4(bias)
            m_old = m_ref[...]
            m_new = jnp.maximum(m_old, jnp.max(s, axis=0, keepdims=True))
            alpha = jnp.exp2(m_old - m_new)
            p = jnp.exp2(s - m_new)
            l_ref[...] = alpha * l_ref[...] + jnp.sum(p, axis=0, keepdims=True)
            acc_ref[...] = alpha * acc_ref[...] + _tn_dot(kv, p.astype(BF16))
            m_ref[...] = m_new
            return carry

        lax.fori_loop(0, n_kt, tile, 0)
        o_s = acc_ref[...] * (1.0 / jnp.maximum(l_ref[...], TINY))

        wkv = win_ref[0, k, pl.ds(w_start, w_rows), :]
        s = jnp.dot(wkv, qr, preferred_element_type=F32) + wbias
        e = jnp.exp2(s - jnp.max(s, axis=0, keepdims=True))
        den = jnp.maximum(jnp.sum(e, axis=0, keepdims=True), TINY)
        o_w = _tn_dot(wkv, e.astype(BF16)) * (1.0 / den)

        for gi in range(group):
            hd = k * group + gi
            sl = slice(gi * qb, (gi + 1) * qb)
            o = gate[3 * hd:3 * hd + 1, :] * oc[:, sl]
            o = o + gate[3 * hd + 1:3 * hd + 2, :] * o_s[:, sl]
            o = o + gate[3 * hd + 2:3 * hd + 3, :] * o_w[:, sl]
            o_ref[0, :, LANES * hd:LANES * (hd + 1)] = o.T.astype(o_ref.dtype)


def nsa_attend_t(qc, qr, gates, cmp_p, sel_p, win_p, T, qb, q0, w_off, n_sel):
    B = qc.shape[0]
    group = N_HEADS // N_KV
    R = group * qb
    n_cmp = cmp_p.shape[2]
    n_selp = -(-n_sel // LANES) * LANES
    l_sel = sel_p.shape[2]
    l_win = win_p.shape[2]
    w_rows = WINDOW + max(qb, LANES)
    assert l_win >= w_rows and l_sel >= ((q0 + T - 1) // KEY_TILE + 1) * KEY_TILE
    wmap = _overlap_map(n_cmp, n_selp)
    kern = functools.partial(_nsa_t_kernel, qb=qb, n_sel=n_sel, n_selp=n_selp, n_cmp=n_cmp,
                             q0=q0, w_off=w_off, w_rows=w_rows, l_win=l_win)
    qspec = pl.BlockSpec((1, N_KV, 1, LANES, R), lambda b, i: (b, 0, i, 0, 0))
    resident = dict(pipeline_mode=pl.Buffered(1))
    return pl.pallas_call(
        kern,
        out_shape=jax.ShapeDtypeStruct((B, T, N_HEADS * LANES), BF16),
        grid=(B, T // qb),
        in_specs=[
            qspec, qspec,
            pl.BlockSpec((1, 1, LANES, qb), lambda b, i: (b, i, 0, 0)),
            pl.BlockSpec((1, N_KV, n_cmp, LANES), lambda b, i: (b, 0, 0, 0)),
            pl.BlockSpec((1, N_KV, l_sel, LANES), lambda b, i: (b, 0, 0, 0), **resident),
            pl.BlockSpec((1, N_KV, l_win, LANES), lambda b, i: (b, 0, 0, 0), **resident),
            pl.BlockSpec((n_selp, n_cmp), lambda b, i: (0, 0)),
        ],
        out_specs=pl.BlockSpec((1, qb, N_HEADS * LANES), lambda b, i: (b, i, 0)),
        scratch_shapes=[
            pltpu.VMEM((n_selp, qb), F32),
            pltpu.VMEM((1, R), F32), pltpu.VMEM((1, R), F32), pltpu.VMEM((LANES, R), F32),
        ],
        compiler_params=_cparams(("parallel", "arbitrary")),
        name="nsa_attend_t",
    )(qc, qr, gates, cmp_p, sel_p, win_p, wmap)


def _nsa_fast_kernel(shift_ref, qc_ref, qr_ref, gate_ref, cmp_ref, sel_ref, win_ref, wmap_ref, hot_ref,
                     o_ref, qaug_ref, oc_ref, ow_ref, l_ref, acc_ref,
                     *, qb, n_sel, n_selp, n_cmp, q0, w_off, w_rows, l_win):
    qi = pl.program_id(1)
    group = N_HEADS // N_KV
    R = group * qb
    q_lo = q0 + qi * qb
    qpos_q = q_lo + lax.broadcasted_iota(jnp.int32, (1, qb), 1)
    n_kt = (q_lo + qb - 1) // KEY_TILE + 1
    w_start = pl.multiple_of(jnp.clip(q_lo - WINDOW - w_off, 0, l_win - w_rows), LANES)
    shift_c = shift_ref[0]
    shift_s = shift_ref[1]
    shift_w = shift_ref[2]

    def lanes4(a):
        return jnp.concatenate([a] * group, axis=1)

    m_idx = lax.broadcasted_iota(jnp.int32, (n_cmp, qb), 0)
    cvalid = (m_idx >= 1) & ((m_idx - 1) * CMP_STRIDE + CMP_LEN - 1 <= qpos_q)
    cbias = lanes4(jnp.where(cvalid, -shift_c, NEG_INF))
    any_c = lanes4(qpos_q >= CMP_LEN - 1)
    blk = lax.broadcasted_iota(jnp.int32, (n_selp, qb), 0)
    blk_f = blk.astype(F32)
    cur = qpos_q >> 6
    forced = (blk == 0) | (blk == cur) | (blk == cur - 1)
    reach = blk * SEL_BLOCK <= qpos_q
    real = blk < n_sel

    imps = []
    for k in range(N_KV):
        ckv = cmp_ref[0, k]
        e = jnp.exp2(jnp.dot(ckv, qc_ref[0, k, 0], preferred_element_type=F32) + cbias)
        den = jnp.maximum(jnp.sum(e, axis=0, keepdims=True), TINY)
        p = e * jnp.where(any_c, 1.0 / den, 0.0)
        oc_ref[k] = _tn_dot(ckv, p.astype(BF16))
        psum = p[:, 0:qb]
        for gi in range(1, group):
            psum = psum + p[:, gi * qb:(gi + 1) * qb]
        hi = psum.astype(BF16)
        lo = (psum - hi.astype(F32)).astype(BF16)
        imp = (jnp.dot(wmap_ref[...], hi, preferred_element_type=F32)
               + jnp.dot(wmap_ref[...], lo, preferred_element_type=F32))
        v = jnp.where(forced, SEL_FORCE, jnp.where(reach, imp, SEL_NEG))
        imps.append(jnp.where(real, v, -jnp.inf))

    def pick_one(_, vs):
        out = []
        for v in vs:
            m = jnp.max(v, axis=0, keepdims=True)
            first = jnp.min(jnp.where(v == m, blk_f, float(n_selp)), axis=0, keepdims=True)
            out.append(jnp.where(blk_f == first, -jnp.inf, v))
        return tuple(out)

    wpos = w_off + w_start + lax.broadcasted_iota(jnp.int32, (w_rows, qb), 0)
    wvalid = (wpos <= qpos_q) & (wpos > qpos_q - WINDOW) & (wpos >= 0)
    wbias = lanes4(jnp.where(wvalid, -shift_w, NEG_INF))
    for k in range(N_KV):
        wkv = win_ref[0, k, pl.ds(w_start, w_rows), :]
        e = jnp.exp2(jnp.dot(wkv, qr_ref[0, k, 0], preferred_element_type=F32) + wbias)
        den = jnp.maximum(jnp.sum(e, axis=0, keepdims=True), TINY)
        ow_ref[k] = _tn_dot(wkv, e.astype(BF16)) * (1.0 / den)

    n_forced = 3
    marked = lax.fori_loop(0, SEL_TOPK - n_forced, pick_one,
                           tuple(jnp.where(forced, -jnp.inf, v) for v in imps), unroll=True)
    for k in range(N_KV):
        sel = jnp.where((marked[k] == -jnp.inf) & (imps[k] > 0.5 * SEL_NEG), -shift_s, NEG_INF)
        qaug_ref[k, 0:LANES, :] = qr_ref[0, k, 0]
        qaug_ref[k, LANES:, :] = lanes4(sel).astype(BF16)

    l_ref[...] = jnp.zeros(l_ref.shape, F32)
    acc_ref[...] = jnp.zeros(acc_ref.shape, F32)

    def tile(start, rows, causal_bias):
        hot = hot_ref[pl.ds(start, rows), :]
        for k in range(N_KV):
            kv = sel_ref[0, k, pl.ds(start, rows), :]
            s = jnp.dot(jnp.concatenate([kv, hot], axis=1), qaug_ref[k], preferred_element_type=F32)
            if causal_bias is not None:
                s = s + causal_bias
            p = jnp.exp2(s)
            l_ref[k] += jnp.sum(p.reshape(rows // 8, 8, R), axis=0)
            acc_ref[k] += _tn_dot(kv, p.astype(BF16))

    def double_tile(i, carry):
        tile(pl.multiple_of(i * 2 * KEY_TILE, 2 * KEY_TILE), 2 * KEY_TILE, None)
        return carry

    n_below = n_kt - 1
    lax.fori_loop(0, n_below // 2, double_tile, 0)

    @pl.when(n_below % 2 == 1)
    def _():
        tile(pl.multiple_of((n_below - 1) * KEY_TILE, KEY_TILE), KEY_TILE, None)

    d_start = pl.multiple_of(n_below * KEY_TILE, KEY_TILE)
    row_t = lax.broadcasted_iota(jnp.int32, (KEY_TILE, qb), 0)
    tile(d_start, KEY_TILE, lanes4(jnp.where(d_start + row_t <= qpos_q, 0.0, NEG_INF)))

    gate = gate_ref[0, 0]
    for k in range(N_KV):
        o_w = ow_ref[k, HEAD_DIM:, :]
        l_s = jnp.maximum(jnp.sum(l_ref[k], axis=0, keepdims=True), TINY)
        o_s = acc_ref[k, HEAD_DIM:, :] * (1.0 / l_s)
        oc = oc_ref[k, HEAD_DIM:, :]
        heads = []
        for gi in range(group):
            hd = k * group + gi
            sl = slice(gi * qb, (gi + 1) * qb)
            o = gate[3 * hd:3 * hd + 1, :] * oc[:, sl]
            o = o + gate[3 * hd + 1:3 * hd + 2, :] * o_s[:, sl]
            heads.append(o + gate[3 * hd + 2:3 * hd + 3, :] * o_w[:, sl])
        for pr in range(group // 2):
            o2 = jnp.concatenate(heads[2 * pr:2 * pr + 2], axis=0)
            col = (k * group + 2 * pr) * HEAD_DIM
            o_ref[0, :, col:col + LANES] = o2.T.astype(o_ref.dtype)


def nsa_attend_fast(shifts, qc, qr, gates, cmp_p, sel_p, win_p, T, qb, q0, w_off, n_sel):
    B = qc.shape[0]
    group = N_HEADS // N_KV
    R = group * qb
    n_cmp = cmp_p.shape[2]
    n_selp = -(-n_sel // LANES) * LANES
    l_sel = sel_p.shape[2]
    l_win = win_p.shape[2]
    w_rows = WINDOW + max(qb, LANES)
    assert l_win >= w_rows and l_sel >= ((q0 + T - 1) // KEY_TILE + 1) * KEY_TILE
    assert n_sel >= SEL_TOPK and q0 % qb == 0 and KEY_TILE % qb == 0
    wmap = _overlap_map(n_cmp, n_selp)
    hot = np.zeros((l_sel, n_selp), np.float32)
    hot[np.arange(l_sel), np.arange(l_sel) // SEL_BLOCK] = 1.0
    hot = jnp.asarray(hot, BF16)
    kern = functools.partial(_nsa_fast_kernel, qb=qb, n_sel=n_sel, n_selp=n_selp, n_cmp=n_cmp,
                             q0=q0, w_off=w_off, w_rows=w_rows, l_win=l_win)
    qspec = pl.BlockSpec((1, N_KV, 1, LANES, R), lambda b, i, sh: (b, 0, i, 0, 0))
    resident = dict(pipeline_mode=pl.Buffered(1))
    grid_spec = pltpu.PrefetchScalarGridSpec(
        num_scalar_prefetch=1,
        grid=(B, T // qb),
        in_specs=[
            qspec, qspec,
            pl.BlockSpec((1, 1, LANES, qb), lambda b, i, sh: (b, i, 0, 0)),
            pl.BlockSpec((1, N_KV, n_cmp, LANES), lambda b, i, sh: (b, 0, 0, 0)),
            pl.BlockSpec((1, N_KV, l_sel, LANES), lambda b, i, sh: (b, 0, 0, 0), **resident),
            pl.BlockSpec((1, N_KV, l_win, LANES), lambda b, i, sh: (b, 0, 0, 0), **resident),
            pl.BlockSpec((n_selp, n_cmp), lambda b, i, sh: (0, 0)),
            pl.BlockSpec((l_sel, n_selp), lambda b, i, sh: (0, 0), **resident),
        ],
        out_specs=pl.BlockSpec((1, qb, N_HEADS * HEAD_DIM), lambda b, i, sh: (b, i, 0)),
        scratch_shapes=[
            pltpu.VMEM((N_KV, 2 * LANES, R), BF16),
            pltpu.VMEM((N_KV, LANES, R), F32),
            pltpu.VMEM((N_KV, LANES, R), F32),
            pltpu.VMEM((N_KV, 8, R), F32),
            pltpu.VMEM((N_KV, LANES, R), F32),
        ],
    )
    return pl.pallas_call(
        kern,
        out_shape=jax.ShapeDtypeStruct((B, T, N_HEADS * HEAD_DIM), BF16),
        grid_spec=grid_spec,
        compiler_params=_cparams(("parallel", "arbitrary")),
        name="nsa_attend_fast",
    )(shifts, qc, qr, gates, cmp_p, sel_p, win_p, wmap, hot)


def _nsa_dec_kernel(pt_ref, *refs, n_pg, qb, n_sel, n_selp, n_cmp, q0, w_off):
    pages = refs[:n_pg]
    (qcp_ref, qbd_ref, gate_ref, cmp_ref, new_ref, win_ref, wmap_ref, fold_ref, hot_ref,
     o_ref, qaug_ref, oc_ref, m_ref, l_ref, acc_ref) = refs[n_pg:]
    g = pl.program_id(1)
    nk = N_KV * HEAD_DIM
    t_new = new_ref.shape[1]
    lane = lax.broadcasted_iota(jnp.int32, (1, LANES), 1)
    qpos = q0 + (lane & (qb - 1))

    @pl.when(g == 0)
    def _first():
        s = jnp.dot(cmp_ref[0, 0], qcp_ref[0, 0], preferred_element_type=F32)
        for k in range(1, N_KV):
            s = s + jnp.dot(cmp_ref[0, k], qcp_ref[0, k], preferred_element_type=F32)
        m_idx = lax.broadcasted_iota(jnp.int32, (n_cmp, LANES), 0)
        cvalid = (m_idx >= 1) & ((m_idx - 1) * CMP_STRIDE + CMP_LEN - 1 <= qpos)
        s = s + jnp.where(cvalid, 0.0, NEG_INF)
        e = jnp.exp2(s - jnp.max(s, axis=0, keepdims=True))
        den = jnp.maximum(jnp.sum(e, axis=0, keepdims=True), TINY)
        p = e * jnp.where(qpos >= CMP_LEN - 1, 1.0 / den, 0.0)
        pb = p.astype(BF16)
        for k in range(N_KV):
            oc_ref[k * HEAD_DIM:(k + 1) * HEAD_DIM, :] = _tn_dot(cmp_ref[0, k], pb)[HEAD_DIM:, :]
        fold = fold_ref[...]
        p_lo = (p - pb.astype(F32)).astype(BF16)
        psum = jnp.dot(pb, fold, preferred_element_type=F32) + jnp.dot(p_lo, fold, preferred_element_type=F32)
        hi = psum.astype(BF16)
        lo = (psum - hi.astype(F32)).astype(BF16)
        imp = (jnp.dot(wmap_ref[...], hi, preferred_element_type=F32)
               + jnp.dot(wmap_ref[...], lo, preferred_element_type=F32))
        blk = lax.broadcasted_iota(jnp.int32, (n_selp, LANES), 0)
        blk_f = blk.astype(F32)
        cur = qpos >> 6
        forced = (blk == 0) | (blk == cur) | (blk == cur - 1)
        v = jnp.where(forced, SEL_FORCE, jnp.where(blk * SEL_BLOCK <= qpos, imp, SEL_NEG))
        v = jnp.where(blk < n_sel, v, -jnp.inf)

        def pick_one(_, carry):
            v, sel = carry
            m = jnp.max(v, axis=0, keepdims=True)
            first = jnp.min(jnp.where(v == m, blk_f, float(n_selp)), axis=0, keepdims=True)
            pick = blk_f == first
            sel = jnp.where(pick & (m > 0.5 * SEL_NEG), 0.0, sel)
            return jnp.where(pick, -jnp.inf, v), sel

        _, sel = lax.fori_loop(0, SEL_TOPK, pick_one, (v, jnp.full((n_selp, LANES), NEG_INF, F32)))
        qaug_ref[0:nk, :] = qbd_ref[0]
        qaug_ref[nk:, :] = sel.astype(BF16)
        m_ref[...] = jnp.full(m_ref.shape, NEG_INF, F32)
        l_ref[...] = jnp.zeros(l_ref.shape, F32)
        acc_ref[...] = jnp.zeros(acc_ref.shape, F32)

    def attend(kx, vx, hot, bias):
        s = jnp.dot(jnp.concatenate([kx, hot], axis=1), qaug_ref[...], preferred_element_type=F32)
        if bias is not None:
            s = s + bias
        m_old = m_ref[...]
        m_new = jnp.maximum(m_old, jnp.max(s, axis=0, keepdims=True))
        alpha = jnp.exp2(m_old - m_new)
        p = jnp.exp2(s - m_new)
        l_ref[...] = alpha * l_ref[...] + jnp.sum(p.reshape(p.shape[0] // 8, 8, LANES), axis=0)
        acc_ref[...] = alpha * acc_ref[...] + _tn_dot(vx, p.astype(BF16))
        m_ref[...] = m_new

    x = jnp.concatenate([pages[i][0] for i in range(n_pg)], axis=0).astype(BF16)
    start = pl.multiple_of(g * (n_pg * PAGE), n_pg * PAGE)
    attend(x[:, :nk], x[:, nk:], hot_ref[pl.ds(start, n_pg * PAGE), :], None)

    @pl.when(g == pl.num_programs(1) - 1)
    def _last():
        pad = jnp.zeros((HALO - t_new, 2 * nk), F32)
        row = lax.broadcasted_iota(jnp.int32, (HALO, LANES), 0)
        new_ok = (row < t_new) & (q0 + row <= qpos)
        xn = jnp.concatenate([new_ref[0, :, 2 * nk:4 * nk], pad], axis=0).astype(BF16)
        hot_new = jnp.where(
            lax.broadcasted_iota(jnp.int32, (HALO, n_selp), 1) == q0 // SEL_BLOCK, 1.0, 0.0).astype(BF16)
        attend(xn[:, :nk], xn[:, nk:], hot_new, jnp.where(new_ok, 0.0, NEG_INF))

        qbd = qbd_ref[0]
        xw = win_ref[0].astype(BF16)
        wrow = lax.broadcasted_iota(jnp.int32, (xw.shape[0], LANES), 0)
        wpos = w_off + wrow
        wvalid = (wpos <= qpos) & (wpos > qpos - WINDOW) & (wpos >= 0)
        s_c = jnp.dot(xw[:, :nk], qbd, preferred_element_type=F32) + jnp.where(wvalid, 0.0, NEG_INF)
        xwn = jnp.concatenate([new_ref[0, :, 4 * nk:6 * nk], pad], axis=0).astype(BF16)
        npos = q0 + row
        nvalid = (row < t_new) & (npos <= qpos) & (npos > qpos - WINDOW)
        s_n = jnp.dot(xwn[:, :nk], qbd, preferred_element_type=F32) + jnp.where(nvalid, 0.0, NEG_INF)
        m_w = jnp.maximum(jnp.max(s_c, axis=0, keepdims=True), jnp.max(s_n, axis=0, keepdims=True))
        e_c = jnp.exp2(s_c - m_w)
        e_n = jnp.exp2(s_n - m_w)
        den = jnp.sum(e_c, axis=0, keepdims=True) + jnp.sum(e_n, axis=0, keepdims=True)
        o_w = _tn_dot(xw[:, nk:], e_c.astype(BF16)) + _tn_dot(xwn[:, nk:], e_n.astype(BF16))
        o_w = o_w * (1.0 / jnp.maximum(den, TINY))
        l_s = jnp.maximum(jnp.sum(l_ref[...], axis=0, keepdims=True), TINY)
        o_s = acc_ref[...] * (1.0 / l_s)
        gate = gate_ref[0]
        o_ref[0] = gate[0:1, :] * oc_ref[...] + gate[1:2, :] * o_s + gate[2:3, :] * o_w


def _dec_page_index(b, g, pt, *, i, n_pg):
    return (pt[b, g * n_pg + i], 0, 0)


def nsa_attend_dec(table, qc, qr, gates, cmp_p, sel_pages, kvp, cache_win, qb, q0, n_sel):
    B, n_pages = table.shape
    group = N_HEADS // N_KV
    assert N_KV * group * qb == LANES
    nk = N_KV * HEAD_DIM
    n_pg = 16 if n_pages % 16 == 0 else n_pages
    n_cmp = cmp_p.shape[2]
    n_selp = -(-n_sel // LANES) * LANES
    wl = cache_win.shape[1]
    eye = jnp.eye(N_KV, dtype=F32)

    def spread(q):
        qt = q[:, :, 0, :, :HEAD_DIM].astype(F32).transpose(0, 1, 3, 2)
        return jnp.einsum("bkdr,kj->bkdjr", qt, eye).reshape(B, N_KV, HEAD_DIM, LANES)

    qc_pad = jnp.pad(spread(qc), ((0, 0), (0, 0), (0, LANES - HEAD_DIM), (0, 0))).astype(BF16)
    q_bd = spread(qr).reshape(B, nk, LANES).astype(BF16)
    gate_l = gates[:, :, :3 * N_HEADS].reshape(B, qb, N_KV, group, 3).transpose(0, 4, 2, 3, 1)
    gate_l = gate_l.reshape(B, 3, LANES)
    wmap = _overlap_map(n_cmp, n_selp)
    lane = np.arange(LANES)
    fold = jnp.asarray((lane[:, None] // (group * qb) == lane[None, :] // (group * qb))
                       & (lane[:, None] % qb == lane[None, :] % qb), BF16)
    l_past = n_pages * PAGE
    hot = np.zeros((l_past, n_selp), np.float32)
    hot[np.arange(l_past), np.arange(l_past) // SEL_BLOCK] = 1.0
    hot = jnp.asarray(hot, BF16)
    kern = functools.partial(_nsa_dec_kernel, n_pg=n_pg, qb=qb, n_sel=n_sel, n_selp=n_selp,
                             n_cmp=n_cmp, q0=q0, w_off=q0 - wl)
    in_specs = [pl.BlockSpec((1, PAGE, 2 * nk), functools.partial(_dec_page_index, i=i, n_pg=n_pg))
                for i in range(n_pg)]
    in_specs += [
        pl.BlockSpec((1, N_KV, LANES, LANES), lambda b, g, pt: (b, 0, 0, 0)),
        pl.BlockSpec((1, nk, LANES), lambda b, g, pt: (b, 0, 0)),
        pl.BlockSpec((1, 3, LANES), lambda b, g, pt: (b, 0, 0)),
        pl.BlockSpec((1, N_KV, n_cmp, LANES), lambda b, g, pt: (b, 0, 0, 0)),
        pl.BlockSpec((1, qb, 6 * nk), lambda b, g, pt: (b, 0, 0)),
        pl.BlockSpec((1, wl, 2 * nk), lambda b, g, pt: (b, 0, 0)),
        pl.BlockSpec((n_selp, n_cmp), lambda b, g, pt: (0, 0)),
        pl.BlockSpec((LANES, LANES), lambda b, g, pt: (0, 0)),
        pl.BlockSpec((l_past, n_selp), lambda b, g, pt: (0, 0), pipeline_mode=pl.Buffered(1)),
    ]
    grid_spec = pltpu.PrefetchScalarGridSpec(
        num_scalar_prefetch=1,
        grid=(B, n_pages // n_pg),
        in_specs=in_specs,
        out_specs=pl.BlockSpec((1, nk, LANES), lambda b, g, pt: (b, 0, 0)),
        scratch_shapes=[
            pltpu.VMEM((nk + n_selp, LANES), BF16),
            pltpu.VMEM((nk, LANES), F32),
            pltpu.VMEM((1, LANES), F32),
            pltpu.VMEM((8, LANES), F32),
            pltpu.VMEM((nk, LANES), F32),
        ],
    )
    o_t = pl.pallas_call(
        kern,
        out_shape=jax.ShapeDtypeStruct((B, nk, LANES), F32),
        grid_spec=grid_spec,
        compiler_params=_cparams(("parallel", "arbitrary")),
        name="nsa_attend_dec",
    )(table, *([sel_pages] * n_pg), qc_pad, q_bd, gate_l, cmp_p, kvp, cache_win, wmap, fold, hot)
    o6 = o_t.reshape(B, N_KV, HEAD_DIM, N_KV, group, qb)
    o5 = jnp.einsum("bkdkgq->bqkgd", o6)
    return o5.reshape(B * qb, N_HEADS * HEAD_DIM)


def _rope_tables(pos, half):
    inv = jnp.exp(-math.log(ROPE_THETA) * jnp.arange(half, dtype=F32) / half)
    ang = pos.astype(F32)[:, None] * inv[None, :]
    return jnp.cos(ang), jnp.sin(ang)


def _prep_weights(ret_w_in, ret_w_out, ffn_w_in, ffn_w_out, kv_w, kv_knorm, cmp_w1, cmp_w2,
                  nsa_w_qg, nsa_qnorm, nsa_w_o):
    nq = N_HEADS * HEAD_DIM
    qg_pad = nq + LANES - nsa_w_qg.shape[2]
    w_qg = jnp.pad(nsa_w_qg, ((0, 0), (0, 0), (0, qg_pad))).astype(BF16)
    R = CMP_LEN // CMP_STRIDE
    w1 = cmp_w1.reshape(2, R, CMP_STRIDE, HEAD_DIM, CMP_HID).transpose(0, 2, 3, 1, 4)
    w1 = w1.reshape(2, CMP_STRIDE, HEAD_DIM, R * CMP_HID)
    w1dup = jnp.concatenate([w1, w1], axis=2).astype(BF16)
    w1dup = w1dup.reshape(2, CMP_STRIDE // 2, 2 * LANES, R * CMP_HID)
    z = jnp.zeros((CMP_HID, HEAD_DIM), F32)
    w2p = jnp.stack([jnp.concatenate([cmp_w2[0], z], axis=1),
                     jnp.concatenate([z, cmp_w2[1]], axis=1)]).astype(BF16)
    ones = jnp.ones((HEAD_DIM,), F32)
    return dict(
        ret_w_in=ret_w_in.astype(BF16), ret_w_out=ret_w_out.astype(BF16),
        ffn_w_in=ffn_w_in.astype(BF16), ffn_w_out=ffn_w_out.astype(BF16),
        kv_w=kv_w.astype(BF16), w_qg=w_qg, w_o_raw=nsa_w_o.astype(BF16),
        kv_gain=jnp.tile(kv_knorm, (1, 2)),
        cmp_gain=jnp.concatenate([kv_knorm[0], ones]).reshape(1, LANES),
        q_gain2=jnp.tile(nsa_qnorm, (1, 2)) * (HEAD_DIM ** -0.5 * math.log2(math.e)),
        score_bound=(1.05 * HEAD_DIM ** 0.5 * math.log2(math.e))
        * jnp.max(jnp.abs(nsa_qnorm), axis=1)[:, None] * jnp.max(jnp.abs(kv_knorm), axis=1)[None, :],
        w1dup=w1dup, w2p=w2p,
        bd=jnp.asarray(np.kron(np.eye(2), np.ones((HEAD_DIM, HEAD_DIM))), BF16),
    )


def _trunk(x, past_len, ret_s0, conv0, ctx, W, P):
    B, T, D = x.shape
    M = B * T
    depth = P["norm_mix"].shape[0]
    n_a = P["ret_w_in"].shape[0]
    pos = past_len + jnp.arange(T)
    cos_r, sin_r = _rope_tables(pos, RET_DK // 2)
    c32, s32 = _rope_tables(pos, HEAD_DIM // 2)
    cos_n = jnp.tile(c32, (1, 4))
    sin_n = jnp.concatenate([-s32, s32, -s32, s32], axis=1)
    lg = jnp.log1p(-jnp.exp2(-5.0 - jnp.arange(RET_HEADS, dtype=F32)))
    L = next((c for c in (2 * RET_CHUNK, RET_CHUNK) if T % c == 0), T)
    gl = jnp.exp(L * lg)
    tm = min(512, M)
    tf = 1024 if T % 1024 == 0 else 512
    tb = min(512, T)
    tq = min(512, T)
    qb = next((c for c in (2 * Q_BLOCK, Q_BLOCK) if T % c == 0), T)
    tt = min(256, T)

    x2 = x.reshape(M, D)
    ret_states, conv_states = [], []
    for layer in range(depth):
        if layer == n_a:
            kvp, selp, winp = kv_project(x2.reshape(B, T, D), P["kv_norm"], W["kv_w"], W["kv_gain"],
                                         cos_n, sin_n, W["bd"], tq)
            nk2 = 2 * N_KV * HEAD_DIM
            if ctx is None:
                table = jnp.arange(M // PAGE, dtype=jnp.int32).reshape(B, T // PAGE)
                cmp_p = compress(kvp.reshape(M // PAGE, PAGE, 3 * nk2), table,
                                 W["w1dup"], W["pb"], W["w2p"], W["cmp_gain"])
                sel_p, win_p = selp, winp
                w_off = 0
            else:
                cache_cmp, cache_sel, cache_win, table = ctx
                n_pool = cache_cmp.shape[0]
                cmp_p = compress(cache_cmp.reshape(n_pool, PAGE, nk2), table,
                                 W["w1dup"], W["pb"], W["w2p"], W["cmp_gain"])
                sel_pages = cache_sel.reshape(n_pool, PAGE, nk2)
                win_rows = cache_win.reshape(B, cache_win.shape[1], nk2)
            n_sel = -(-(past_len + T) // SEL_BLOCK)
        h_norm = P["norm_mix"][layer]
        if layer < n_a:
            proj = norm_matmul(x2, h_norm, W["ret_w_in"][layer], min(1024, M), 1024)
            og, s_new = retention(proj.reshape(B, T, -1), ret_s0[layer], cos_r, sin_r, lg, gl, L, tb)
            ret_states.append(s_new)
            x2 = matmul_res(og.reshape(M, -1), W["ret_w_out"][layer], x2, tm)
        else:
            j = layer - n_a
            if qb % LANES == 0:
                qc, qr, gates = qg_project_t(x2.reshape(B, T, D), h_norm, W["w_qg"][j],
                                             W["q_gain2"][j:j + 1], cos_n, sin_n, W["bd"], tq, qb)
                shifts = W["score_bound"][j]
                args = (qc, qr, gates, cmp_p, sel_p, win_p)
                o = lax.cond(
                    jnp.max(shifts) <= 30.0,
                    lambda a: nsa_attend_fast(shifts, *a, T, qb, past_len, w_off, n_sel),
                    lambda a: nsa_attend_t(*a, T, qb, past_len, w_off, n_sel).reshape(
                        B, T, N_HEADS, LANES)[..., HEAD_DIM:].reshape(B, T, N_HEADS * HEAD_DIM),
                    args)
                x2 = matmul_res(o.reshape(M, -1), W["w_o_raw"][j], x2, tm)
            else:
                qc, qr, gates = qg_project(x2.reshape(B, T, D), h_norm, W["w_qg"][j],
                                           W["q_gain2"][j:j + 1], cos_n, sin_n, W["bd"], tq, qb)
                o = nsa_attend_dec(table, qc, qr, gates, cmp_p, sel_pages, kvp, win_rows,
                                   qb, past_len, n_sel)
                x2 = matmul_res(o, W["w_o_raw"][j], x2, tm)
        if T % tf == 0:
            act, tail = ffn_in(x2, P["norm_ffn"][layer], W["ffn_w_in"][layer], conv0[layer],
                               P["ffn_conv_w"][layer], P["ffn_conv_b"][layer], T, tf, 256)
            conv_states.append(tail)
        else:
            proj = norm_matmul(x2, P["norm_ffn"][layer], W["ffn_w_in"][layer], tm, 512)
            proj3 = proj.reshape(B, T, 2 * D_FF)
            act = ffn_mid(proj3, conv0[layer], P["ffn_conv_w"][layer], P["ffn_conv_b"][layer], tt)
            conv_states.append(proj3[:, T - 2:, :D_FF])
        x2 = matmul_res(act.reshape(M, D_FF), W["ffn_w_out"][layer], x2, min(1024, M))

    nk = N_KV * HEAD_DIM
    new_cmp = kvp[:, :, 0:2 * nk].reshape(B, T, 2, N_KV, HEAD_DIM)
    new_sel = kvp[:, :, 2 * nk:4 * nk].reshape(B, T, 2, N_KV, HEAD_DIM)
    new_win = kvp[:, :, 4 * nk:6 * nk].reshape(B, T, 2, N_KV, HEAD_DIM)
    return (x2.reshape(B, T, D), jnp.stack(ret_states), jnp.stack(conv_states),
            new_cmp, new_sel, new_win)


def kernel(x_prompt, x_sample, cache_cmp_kv, cache_sel_kv, cache_win_kv, state_ret, state_conv,
           page_table, norm_mix, norm_ffn, ret_w_in, ret_w_out, ffn_w_in, ffn_conv_w, ffn_conv_b,
           ffn_w_out, kv_norm, kv_w, kv_knorm, cmp_pos, cmp_w1, cmp_w2, nsa_w_qg, nsa_qnorm, nsa_w_o):
    W = _prep_weights(ret_w_in, ret_w_out, ffn_w_in, ffn_w_out, kv_w, kv_knorm, cmp_w1, cmp_w2,
                      nsa_w_qg, nsa_qnorm, nsa_w_o)
    W["pb"] = pos_bias(cmp_pos, cmp_w1)
    P = dict(norm_mix=norm_mix, norm_ffn=norm_ffn, ret_w_in=ret_w_in, ffn_conv_w=ffn_conv_w,
             ffn_conv_b=ffn_conv_b, kv_norm=kv_norm)
    depth = norm_mix.shape[0]
    n_a = ret_w_in.shape[0]
    B, T, _ = x_prompt.shape
    zero_ret = jnp.zeros((n_a, B, RET_HEADS, RET_DK, RET_DV), F32)
    zero_conv = jnp.zeros((depth, B, 2, D_FF), F32)
    y_p, ret_p, conv_p, cmp_p, sel_p, win_p = _trunk(x_prompt, 0, zero_ret, zero_conv, None, W, P)
    win_p = win_p[:, T - min(WINDOW, T):]

    db, ts, _ = x_sample.shape
    past_len = page_table.shape[1] * PAGE
    ctx = (cache_cmp_kv, cache_sel_kv, cache_win_kv, page_table)
    y_s, ret_s, conv_s, cmp_s, sel_s, win_new = _trunk(x_sample, past_len, state_ret, state_conv,
                                                        ctx, W, P)
    all_win = jnp.concatenate([cache_win_kv, win_new], axis=1)
    win_s = all_win[:, all_win.shape[1] - min(WINDOW, past_len + ts):]
    return (y_p, y_s, ret_p, ret_s, conv_p, conv_s, cmp_p, cmp_s, sel_p, sel_s, win_p, win_s)
```
